```python
import jax, jax.numpy as jnp
from jax import lax
import numpy as np

D_MODEL = 1024
BATCH = 16
SEQ = 4096
DEPTH = 4

N_MIXERS = 4
HEAD_DIM = 64
FOX_HEADS = D_MODEL // HEAD_DIM
SB_HEADS = D_MODEL // HEAD_DIM
Q_BLOCK = 128
GM_CHUNK = 128
GM_WIDTH = D_MODEL
GM_GROUPS = 8
GM_GROUP_DIM = GM_WIDTH // GM_GROUPS
CONV_WIDTH = 31
FFN_HIDDEN = -(-8 * D_MODEL // (3 * 256)) * 256
DN_ALPHA = (2.0 * DEPTH) ** 0.25
DN_BETA = (8.0 * DEPTH) ** -0.25
N_GM = (DEPTH + 3) // N_MIXERS
N_FOX = (DEPTH + 2) // N_MIXERS
N_SB = (DEPTH + 1) // N_MIXERS
N_CV = DEPTH // N_MIXERS
LN_EPS = 1e-5
NEG_INF = -1e30

kernel_name = "hybrid_interleaved_gmlp_fox_stickbreak_conformer"


def _layer_norm(x, g, b):
    xf = x.astype(jnp.float32)
    mu = jnp.mean(xf, axis=-1, keepdims=True)
    xc = xf - mu
    var = jnp.mean(xc * xc, axis=-1, keepdims=True)
    return (xc * lax.rsqrt(var + LN_EPS) * g + b).astype(x.dtype)


def _gmlp_mixer(h, w_in, b_in, ln_g, ln_b, w_s, b_s, w_out):
    B, S, _ = h.shape
    z = jax.nn.gelu(h @ w_in + b_in, approximate=False)
    u, v = jnp.split(z, 2, axis=-1)
    v = _layer_norm(v, ln_g, ln_b)
    nc = S // GM_CHUNK
    v5 = v.reshape(B, nc, GM_CHUNK, GM_GROUPS, GM_GROUP_DIM)
    causal = jnp.tril(jnp.ones((GM_CHUNK, GM_CHUNK), dtype=w_s.dtype))
    w_m = w_s * causal
    sv = jnp.einsum('gts,bnsgc->bntgc', w_m, v5) + b_s.T[None, None, :, :, None]
    y = u * sv.reshape(B, S, GM_WIDTH)
    return y @ w_out


def _fox_mixer(h, w_in, b_f, w_out):
    B, S, D = h.shape
    H = FOX_HEADS
    scale = HEAD_DIM ** -0.5
    proj = h @ w_in
    q, k, v, f_logit = jnp.split(proj, [D, 2 * D, 3 * D], axis=-1)
    to_heads = lambda t: t.reshape(B, S, H, HEAD_DIM).transpose(0, 2, 1, 3)
    q, k, v = to_heads(q), to_heads(k), to_heads(v)
    log_f = jax.nn.log_sigmoid((f_logit + b_f).astype(jnp.float32))
    F = jnp.cumsum(log_f, axis=1).transpose(0, 2, 1)
    nb = S // Q_BLOCK
    kpos = jnp.arange(S)
    qb = q.reshape(B, H, nb, Q_BLOCK, HEAD_DIM).transpose(2, 0, 1, 3, 4)
    Fb = F.reshape(B, H, nb, Q_BLOCK).transpose(2, 0, 1, 3)
    pb = kpos.reshape(nb, Q_BLOCK)

    def block(args):
        q_blk, F_blk, q_pos = args
        s = jnp.einsum('bhqd,bhkd->bhqk', q_blk, k).astype(jnp.float32) * scale
        s = s + F_blk[..., None] - F[:, :, None, :]
        s = jnp.where(kpos[None, :] <= q_pos[:, None], s, NEG_INF)
        p = jax.nn.softmax(s, axis=-1).astype(v.dtype)
        return jnp.einsum('bhqk,bhkd->bhqd', p, v)

    o = lax.map(block, (qb, Fb, pb))
    o = o.transpose(1, 0, 3, 2, 4).reshape(B, S, D)
    return o @ w_out


def _stick_breaking_mixer(h, w_in, w_out):
    B, S, D = h.shape
    H = SB_HEADS
    scale = HEAD_DIM ** -0.5
    q, k, v = jnp.split(h @ w_in, 3, axis=-1)
    to_heads = lambda t: t.reshape(B, S, H, HEAD_DIM).transpose(0, 2, 1, 3)
    q, k, v = to_heads(q), to_heads(k), to_heads(v)
    nb = S // Q_BLOCK
    kpos = jnp.arange(S)
    qb = q.reshape(B, H, nb, Q_BLOCK, HEAD_DIM).transpose(2, 0, 1, 3, 4)
    pb = kpos.reshape(nb, Q_BLOCK)

    def block(args):
        q_blk, q_pos = args
        z = jnp.einsum('bhqd,bhkd->bhqk', q_blk, k).astype(jnp.float32) * scale
        mask = kpos[None, :] < q_pos[:, None]
        log_beta = jax.nn.log_sigmoid(z)
        log_1m = jnp.where(mask, jax.nn.log_sigmoid(-z), 0.0)
        rest = lax.cumsum(log_1m, axis=3, reverse=True) - log_1m
        a = jnp.where(mask, jnp.exp(log_beta + rest), 0.0).astype(v.dtype)
        return jnp.einsum('bhqk,bhkd->bhqd', a, v)

    o = lax.map(block, (qb, pb))
    o = o.transpose(1, 0, 3, 2, 4).reshape(B, S, D)
    return o @ w_out


def _conformer_conv_mixer(h, w_in, b_in, dw, dw_b, ln_g, ln_b, w_out, b_out):
    D = h.shape[-1]
    a, g = jnp.split(h @ w_in + b_in, 2, axis=-1)
    y = a * jax.nn.sigmoid(g)
    y = lax.conv_general_dilated(
        y, dw[:, None, :], window_strides=(1,), padding=[(CONV_WIDTH - 1, 0)],
        dimension_numbers=('NWC', 'WIO', 'NWC'), feature_group_count=D) + dw_b
    y = jax.nn.silu(_layer_norm(y, ln_g, ln_b))
    return y @ w_out + b_out


def _swiglu(h, w_in, w_out):
    g, u = jnp.split(h @ w_in, 2, axis=-1)
    return (jax.nn.silu(g) * u) @ w_out


def _fwd_setup_inputs(seed: int = 0) -> dict:
    key = jax.random.key(seed)
    ks = iter(jax.random.split(key, 40))
    D = D_MODEL
    f32 = jnp.float32
    nrm = lambda shape, s: jax.random.normal(next(ks), shape, f32) * s
    gain = lambda shape: 1.0 + nrm(shape, 0.02)
    return {
        "x": nrm((BATCH, SEQ, D), 1.0),
        "c": nrm((BATCH, D), 1.0),
        "mod_w": nrm((DEPTH, D, 6 * D), 0.1 * D ** -0.5),
        "mod_b": nrm((DEPTH, 6 * D), 0.01),
        "ln1_g": gain((DEPTH, D)),
        "ln1_b": nrm((DEPTH, D), 0.02),
        "ln2_g": gain((DEPTH, D)),
        "ln2_b": nrm((DEPTH, D), 0.02),
        "ffn_w_in": nrm((DEPTH, D, 2 * FFN_HIDDEN), D ** -0.5),
        "ffn_w_out": nrm((DEPTH, FFN_HIDDEN, D), FFN_HIDDEN ** -0.5 * DN_BETA),
        "gm_w_in": nrm((N_GM, D, 2 * GM_WIDTH), D ** -0.5),
        "gm_b_in": nrm((N_GM, 2 * GM_WIDTH), 0.02),
        "gm_ln_g": gain((N_GM, GM_WIDTH)),
        "gm_ln_b": nrm((N_GM, GM_WIDTH), 0.02),
        "gm_w_s": nrm((N_GM, GM_GROUPS, GM_CHUNK, GM_CHUNK), 0.5 * GM_CHUNK ** -0.5),
        "gm_b_s": gain((N_GM, GM_GROUPS, GM_CHUNK)),
        "gm_w_out": nrm((N_GM, GM_WIDTH, D), GM_WIDTH ** -0.5 * DN_BETA),
        "fox_w_in": nrm((N_FOX, D, 3 * D + FOX_HEADS), D ** -0.5),
        "fox_b_f": jax.random.uniform(next(ks), (N_FOX, FOX_HEADS), f32, 1.0, 4.0),
        "fox_w_out": nrm((N_FOX, D, D), D ** -0.5 * DN_BETA),
        "sb_w_in": nrm((N_SB, D, 3 * D), D ** -0.5),
        "sb_w_out": nrm((N_SB, D, D), D ** -0.5 * DN_BETA),
        "cv_w_in": nrm((N_CV, D, 2 * D), D ** -0.5),
        "cv_b_in": nrm((N_CV, 2 * D), 0.02),
        "cv_dw": nrm((N_CV, CONV_WIDTH, D), CONV_WIDTH ** -0.5),
        "cv_dw_b": nrm((N_CV, D), 0.02),
        "cv_ln_g": gain((N_CV, D)),
        "cv_ln_b": nrm((N_CV, D), 0.02),
        "cv_w_out": nrm((N_CV, D, D), D ** -0.5 * DN_BETA),
        "cv_b_out": nrm((N_CV, D), 0.02),
    }


def _fwd_reference(x, c, mod_w, mod_b, ln1_g, ln1_b, ln2_g, ln2_b, ffn_w_in, ffn_w_out,
              gm_w_in, gm_b_in, gm_ln_g, gm_ln_b, gm_w_s, gm_b_s, gm_w_out,
              fox_w_in, fox_b_f, fox_w_out,
              sb_w_in, sb_w_out,
              cv_w_in, cv_b_in, cv_dw, cv_dw_b, cv_ln_g, cv_ln_b, cv_w_out, cv_b_out):
    c_act = jax.nn.silu(c)
    for l in range(DEPTH):
        m, j = l % N_MIXERS, l // N_MIXERS
        mod = c_act @ mod_w[l] + mod_b[l]
        sh1, sc1, g1, sh2, sc2, g2 = [t[:, None, :] for t in jnp.split(mod, 6, axis=-1)]
        h = x * (1.0 + sc1) + sh1
        if m == 0:
            y = _gmlp_mixer(h, gm_w_in[j], gm_b_in[j], gm_ln_g[j], gm_ln_b[j],
                            gm_w_s[j], gm_b_s[j], gm_w_out[j])
        elif m == 1:
            y = _fox_mixer(h, fox_w_in[j], fox_b_f[j], fox_w_out[j])
        elif m == 2:
            y = _stick_breaking_mixer(h, sb_w_in[j], sb_w_out[j])
        else:
            y = _conformer_conv_mixer(h, cv_w_in[j], cv_b_in[j], cv_dw[j], cv_dw_b[j],
                                      cv_ln_g[j], cv_ln_b[j], cv_w_out[j], cv_b_out[j])
        x = _layer_norm(DN_ALPHA * x + (1.0 + g1) * y, ln1_g[l], ln1_b[l])
        h = x * (1.0 + sc2) + sh2
        y = _swiglu(h, ffn_w_in[l], ffn_w_out[l])
        x = _layer_norm(DN_ALPHA * x + (1.0 + g2) * y, ln2_g[l], ln2_b[l])
    return x


import jax as _jax
import jax.numpy as _jnp

TWIN_FORMAT = 'train_step'
FWD_PARAMS = ['x', 'c', 'mod_w', 'mod_b', 'ln1_g', 'ln1_b', 'ln2_g', 'ln2_b', 'ffn_w_in', 'ffn_w_out', 'gm_w_in', 'gm_b_in', 'gm_ln_g', 'gm_ln_b', 'gm_w_s', 'gm_b_s', 'gm_w_out', 'fox_w_in', 'fox_b_f', 'fox_w_out', 'sb_w_in', 'sb_w_out', 'cv_w_in', 'cv_b_in', 'cv_dw', 'cv_dw_b', 'cv_ln_g', 'cv_ln_b', 'cv_w_out', 'cv_b_out']
TWIN_WEIGHTS = ['mod_w', 'mod_b', 'ln1_g', 'ln1_b', 'ln2_g', 'ln2_b', 'ffn_w_in', 'ffn_w_out', 'gm_w_in', 'gm_b_in', 'gm_ln_g', 'gm_ln_b', 'gm_w_s', 'gm_b_s', 'gm_w_out', 'fox_w_in', 'fox_b_f', 'fox_w_out', 'sb_w_in', 'sb_w_out', 'cv_w_in', 'cv_b_in', 'cv_dw', 'cv_dw_b', 'cv_ln_g', 'cv_ln_b', 'cv_w_out', 'cv_b_out']
TWIN_DIFF_INPUT = 'x'
TWIN_INPUTS = ['x', 'c', 'mod_w', 'mod_b', 'ln1_g', 'ln1_b', 'ln2_g', 'ln2_b', 'ffn_w_in', 'ffn_w_out', 'gm_w_in', 'gm_b_in', 'gm_ln_g', 'gm_ln_b', 'gm_w_s', 'gm_b_s', 'gm_w_out', 'fox_w_in', 'fox_b_f', 'fox_w_out', 'sb_w_in', 'sb_w_out', 'cv_w_in', 'cv_b_in', 'cv_dw', 'cv_dw_b', 'cv_ln_g', 'cv_ln_b', 'cv_w_out', 'cv_b_out', 'loss_target', 'm_mod_w', 'm_mod_b', 'm_ln1_g', 'm_ln1_b', 'm_ln2_g', 'm_ln2_b', 'm_ffn_w_in', 'm_ffn_w_out', 'm_gm_w_in', 'm_gm_b_in', 'm_gm_ln_g', 'm_gm_ln_b', 'm_gm_w_s', 'm_gm_b_s', 'm_gm_w_out', 'm_fox_w_in', 'm_fox_b_f', 'm_fox_w_out', 'm_sb_w_in', 'm_sb_w_out', 'm_cv_w_in', 'm_cv_b_in', 'm_cv_dw', 'm_cv_dw_b', 'm_cv_ln_g', 'm_cv_ln_b', 'm_cv_w_out', 'm_cv_b_out', 'v_mod_w', 'v_mod_b', 'v_ln1_g', 'v_ln1_b', 'v_ln2_g', 'v_ln2_b', 'v_ffn_w_in', 'v_ffn_w_out', 'v_gm_w_in', 'v_gm_b_in', 'v_gm_ln_g', 'v_gm_ln_b', 'v_gm_w_s', 'v_gm_b_s', 'v_gm_w_out', 'v_fox_w_in', 'v_fox_b_f', 'v_fox_w_out', 'v_sb_w_in', 'v_sb_w_out', 'v_cv_w_in', 'v_cv_b_in', 'v_cv_dw', 'v_cv_dw_b', 'v_cv_ln_g', 'v_cv_ln_b', 'v_cv_w_out', 'v_cv_b_out']
TWIN_OUTPUTS = ['loss', 'grad_x', 'grad_mod_w', 'grad_mod_b', 'grad_ln1_g', 'grad_ln1_b', 'grad_ln2_g', 'grad_ln2_b', 'grad_ffn_w_in', 'grad_ffn_w_out', 'grad_gm_w_in', 'grad_gm_b_in', 'grad_gm_ln_g', 'grad_gm_ln_b', 'grad_gm_w_s', 'grad_gm_b_s', 'grad_gm_w_out', 'grad_fox_w_in', 'grad_fox_b_f', 'grad_fox_w_out', 'grad_sb_w_in', 'grad_sb_w_out', 'grad_cv_w_in', 'grad_cv_b_in', 'grad_cv_dw', 'grad_cv_dw_b', 'grad_cv_ln_g', 'grad_cv_ln_b', 'grad_cv_w_out', 'grad_cv_b_out', 'delta_mod_w', 'delta_mod_b', 'delta_ln1_g', 'delta_ln1_b', 'delta_ln2_g', 'delta_ln2_b', 'delta_ffn_w_in', 'delta_ffn_w_out', 'delta_gm_w_in', 'delta_gm_b_in', 'delta_gm_ln_g', 'delta_gm_ln_b', 'delta_gm_w_s', 'delta_gm_b_s', 'delta_gm_w_out', 'delta_fox_w_in', 'delta_fox_b_f', 'delta_fox_w_out', 'delta_sb_w_in', 'delta_sb_w_out', 'delta_cv_w_in', 'delta_cv_b_in', 'delta_cv_dw', 'delta_cv_dw_b', 'delta_cv_ln_g', 'delta_cv_ln_b', 'delta_cv_w_out', 'delta_cv_b_out', 'new_m_mod_w', 'new_m_mod_b', 'new_m_ln1_g', 'new_m_ln1_b', 'new_m_ln2_g', 'new_m_ln2_b', 'new_m_ffn_w_in', 'new_m_ffn_w_out', 'new_m_gm_w_in', 'new_m_gm_b_in', 'new_m_gm_ln_g', 'new_m_gm_ln_b', 'new_m_gm_w_s', 'new_m_gm_b_s', 'new_m_gm_w_out', 'new_m_fox_w_in', 'new_m_fox_b_f', 'new_m_fox_w_out', 'new_m_sb_w_in', 'new_m_sb_w_out', 'new_m_cv_w_in', 'new_m_cv_b_in', 'new_m_cv_dw', 'new_m_cv_dw_b', 'new_m_cv_ln_g', 'new_m_cv_ln_b', 'new_m_cv_w_out', 'new_m_cv_b_out', 'new_v_mod_w', 'new_v_mod_b', 'new_v_ln1_g', 'new_v_ln1_b', 'new_v_ln2_g', 'new_v_ln2_b', 'new_v_ffn_w_in', 'new_v_ffn_w_out', 'new_v_gm_w_in', 'new_v_gm_b_in', 'new_v_gm_ln_g', 'new_v_gm_ln_b', 'new_v_gm_w_s', 'new_v_gm_b_s', 'new_v_gm_w_out', 'new_v_fox_w_in', 'new_v_fox_b_f', 'new_v_fox_w_out', 'new_v_sb_w_in', 'new_v_sb_w_out', 'new_v_cv_w_in', 'new_v_cv_b_in', 'new_v_cv_dw', 'new_v_cv_dw_b', 'new_v_cv_ln_g', 'new_v_cv_ln_b', 'new_v_cv_w_out', 'new_v_cv_b_out']
TWIN_LEAF_KINDS = {'loss': 'loss', 'grad_x': 'grad_x', 'grad_mod_w': 'grad_w', 'grad_mod_b': 'grad_w', 'grad_ln1_g': 'grad_w', 'grad_ln1_b': 'grad_w', 'grad_ln2_g': 'grad_w', 'grad_ln2_b': 'grad_w', 'grad_ffn_w_in': 'grad_w', 'grad_ffn_w_out': 'grad_w', 'grad_gm_w_in': 'grad_w', 'grad_gm_b_in': 'grad_w', 'grad_gm_ln_g': 'grad_w', 'grad_gm_ln_b': 'grad_w', 'grad_gm_w_s': 'grad_w', 'grad_gm_b_s': 'grad_w', 'grad_gm_w_out': 'grad_w', 'grad_fox_w_in': 'grad_w', 'grad_fox_b_f': 'grad_w', 'grad_fox_w_out': 'grad_w', 'grad_sb_w_in': 'grad_w', 'grad_sb_w_out': 'grad_w', 'grad_cv_w_in': 'grad_w', 'grad_cv_b_in': 'grad_w', 'grad_cv_dw': 'grad_w', 'grad_cv_dw_b': 'grad_w', 'grad_cv_ln_g': 'grad_w', 'grad_cv_ln_b': 'grad_w', 'grad_cv_w_out': 'grad_w', 'grad_cv_b_out': 'grad_w', 'delta_mod_w': 'delta_w', 'delta_mod_b': 'delta_w', 'delta_ln1_g': 'delta_w', 'delta_ln1_b': 'delta_w', 'delta_ln2_g': 'delta_w', 'delta_ln2_b': 'delta_w', 'delta_ffn_w_in': 'delta_w', 'delta_ffn_w_out': 'delta_w', 'delta_gm_w_in': 'delta_w', 'delta_gm_b_in': 'delta_w', 'delta_gm_ln_g': 'delta_w', 'delta_gm_ln_b': 'delta_w', 'delta_gm_w_s': 'delta_w', 'delta_gm_b_s': 'delta_w', 'delta_gm_w_out': 'delta_w', 'delta_fox_w_in': 'delta_w', 'delta_fox_b_f': 'delta_w', 'delta_fox_w_out': 'delta_w', 'delta_sb_w_in': 'delta_w', 'delta_sb_w_out': 'delta_w', 'delta_cv_w_in': 'delta_w', 'delta_cv_b_in': 'delta_w', 'delta_cv_dw': 'delta_w', 'delta_cv_dw_b': 'delta_w', 'delta_cv_ln_g': 'delta_w', 'delta_cv_ln_b': 'delta_w', 'delta_cv_w_out': 'delta_w', 'delta_cv_b_out': 'delta_w', 'new_m_mod_w': 'new_m', 'new_m_mod_b': 'new_m', 'new_m_ln1_g': 'new_m', 'new_m_ln1_b': 'new_m', 'new_m_ln2_g': 'new_m', 'new_m_ln2_b': 'new_m', 'new_m_ffn_w_in': 'new_m', 'new_m_ffn_w_out': 'new_m', 'new_m_gm_w_in': 'new_m', 'new_m_gm_b_in': 'new_m', 'new_m_gm_ln_g': 'new_m', 'new_m_gm_ln_b': 'new_m', 'new_m_gm_w_s': 'new_m', 'new_m_gm_b_s': 'new_m', 'new_m_gm_w_out': 'new_m', 'new_m_fox_w_in': 'new_m', 'new_m_fox_b_f': 'new_m', 'new_m_fox_w_out': 'new_m', 'new_m_sb_w_in': 'new_m', 'new_m_sb_w_out': 'new_m', 'new_m_cv_w_in': 'new_m', 'new_m_cv_b_in': 'new_m', 'new_m_cv_dw': 'new_m', 'new_m_cv_dw_b': 'new_m', 'new_m_cv_ln_g': 'new_m', 'new_m_cv_ln_b': 'new_m', 'new_m_cv_w_out': 'new_m', 'new_m_cv_b_out': 'new_m', 'new_v_mod_w': 'new_v', 'new_v_mod_b': 'new_v', 'new_v_ln1_g': 'new_v', 'new_v_ln1_b': 'new_v', 'new_v_ln2_g': 'new_v', 'new_v_ln2_b': 'new_v', 'new_v_ffn_w_in': 'new_v', 'new_v_ffn_w_out': 'new_v', 'new_v_gm_w_in': 'new_v', 'new_v_gm_b_in': 'new_v', 'new_v_gm_ln_g': 'new_v', 'new_v_gm_ln_b': 'new_v', 'new_v_gm_w_s': 'new_v', 'new_v_gm_b_s': 'new_v', 'new_v_gm_w_out': 'new_v', 'new_v_fox_w_in': 'new_v', 'new_v_fox_b_f': 'new_v', 'new_v_fox_w_out': 'new_v', 'new_v_sb_w_in': 'new_v', 'new_v_sb_w_out': 'new_v', 'new_v_cv_w_in': 'new_v', 'new_v_cv_b_in': 'new_v', 'new_v_cv_dw': 'new_v', 'new_v_cv_dw_b': 'new_v', 'new_v_cv_ln_g': 'new_v', 'new_v_cv_ln_b': 'new_v', 'new_v_cv_w_out': 'new_v', 'new_v_cv_b_out': 'new_v'}


def _forward(args):
    return _fwd_reference(*[args[k] for k in FWD_PARAMS])


def _output_shape():
    out = _jax.eval_shape(lambda: _forward(_fwd_setup_inputs(0)))
    return out.shape, out.dtype

N_MICROBATCH = 1
ADAM_LR = 0.001
ADAM_B1 = 0.9
ADAM_B2 = 0.999
ADAM_EPS = 1e-08
ADAM_WD = 0.01
ADAM_STEP = 10
PER_EXAMPLE_BATCH_AXIS = {'x': 0, 'c': 0, 'loss_target': 0}
SHARED_INPUTS = []
_WEIGHT_DTYPES = {'mod_w': _jnp.float32, 'mod_b': _jnp.float32, 'ln1_g': _jnp.float32, 'ln1_b': _jnp.float32, 'ln2_g': _jnp.float32, 'ln2_b': _jnp.float32, 'ffn_w_in': _jnp.float32, 'ffn_w_out': _jnp.float32, 'gm_w_in': _jnp.float32, 'gm_b_in': _jnp.float32, 'gm_ln_g': _jnp.float32, 'gm_ln_b': _jnp.float32, 'gm_w_s': _jnp.float32, 'gm_b_s': _jnp.float32, 'gm_w_out': _jnp.float32, 'fox_w_in': _jnp.float32, 'fox_b_f': _jnp.float32, 'fox_w_out': _jnp.float32, 'sb_w_in': _jnp.float32, 'sb_w_out': _jnp.float32, 'cv_w_in': _jnp.float32, 'cv_b_in': _jnp.float32, 'cv_dw': _jnp.float32, 'cv_dw_b': _jnp.float32, 'cv_ln_g': _jnp.float32, 'cv_ln_b': _jnp.float32, 'cv_w_out': _jnp.float32, 'cv_b_out': _jnp.float32}
MOMENT_SCALE = {'mod_w': 3.459196e-02, 'mod_b': 6.939312e-02, 'ln1_g': 1.802375e+00, 'ln1_b': 7.286494e-01, 'ln2_g': 3.210272e+01, 'ln2_b': 3.007130e+00, 'ffn_w_in': 2.344609e-02, 'ffn_w_out': 9.099600e-02, 'gm_w_in': 3.565224e-02, 'gm_b_in': 6.202021e-02, 'gm_ln_g': 1.517913e-02, 'gm_ln_b': 1.600112e-02, 'gm_w_s': 3.051157e-02, 'gm_b_s': 4.342382e-02, 'gm_w_out': 1.419016e-01, 'fox_w_in': 2.018700e-02, 'fox_b_f': 1.325942e-01, 'fox_w_out': 5.682457e-02, 'sb_w_in': 2.916888e-02, 'sb_w_out': 1.020364e-01, 'cv_w_in': 2.953491e-02, 'cv_b_in': 4.611928e-02, 'cv_dw': 4.037264e-02, 'cv_dw_b': 1.180540e-01, 'cv_ln_g': 5.725099e-02, 'cv_ln_b': 7.718186e-02, 'cv_w_out': 1.106624e-01, 'cv_b_out': 4.346971e-01}


def _to_microbatches(a, axis):
    t = _jnp.moveaxis(a, axis, 0)
    t = t.reshape((N_MICROBATCH, t.shape[0] // N_MICROBATCH) + t.shape[1:])
    return _jnp.moveaxis(t, 1, axis + 1)


def setup_inputs(seed: int = 0) -> dict:
    inp = _fwd_setup_inputs(seed)
    key = _jax.random.fold_in(_jax.random.key(seed), 7919)
    shape, _ = _output_shape()
    out = dict(inp)
    out["loss_target"] = _jax.random.normal(_jax.random.fold_in(key, 0), shape, _jnp.float32)
    for i, name in enumerate(TWIN_WEIGHTS):
        w = inp[name].astype(_jnp.float32)
        if MOMENT_SCALE is None:
            s = _jnp.sqrt(_jnp.mean(_jnp.square(w)) + 1e-30)
        else:
            s = MOMENT_SCALE[name]
        km, kv = _jax.random.split(_jax.random.fold_in(key, i + 1))
        out[name] = w
        out["m_" + name] = s * _jax.random.normal(km, w.shape, _jnp.float32)
        out["v_" + name] = (s * s) * _jax.random.uniform(kv, w.shape, _jnp.float32, 0.5, 1.5)
    if N_MICROBATCH > 1:
        for name, axis in PER_EXAMPLE_BATCH_AXIS.items():
            out[name] = _to_microbatches(out[name], axis)
    return {'x': out['x'], 'c': out['c'], 'mod_w': out['mod_w'], 'mod_b': out['mod_b'], 'ln1_g': out['ln1_g'], 'ln1_b': out['ln1_b'], 'ln2_g': out['ln2_g'], 'ln2_b': out['ln2_b'], 'ffn_w_in': out['ffn_w_in'], 'ffn_w_out': out['ffn_w_out'], 'gm_w_in': out['gm_w_in'], 'gm_b_in': out['gm_b_in'], 'gm_ln_g': out['gm_ln_g'], 'gm_ln_b': out['gm_ln_b'], 'gm_w_s': out['gm_w_s'], 'gm_b_s': out['gm_b_s'], 'gm_w_out': out['gm_w_out'], 'fox_w_in': out['fox_w_in'], 'fox_b_f': out['fox_b_f'], 'fox_w_out': out['fox_w_out'], 'sb_w_in': out['sb_w_in'], 'sb_w_out': out['sb_w_out'], 'cv_w_in': out['cv_w_in'], 'cv_b_in': out['cv_b_in'], 'cv_dw': out['cv_dw'], 'cv_dw_b': out['cv_dw_b'], 'cv_ln_g': out['cv_ln_g'], 'cv_ln_b': out['cv_ln_b'], 'cv_w_out': out['cv_w_out'], 'cv_b_out': out['cv_b_out'], 'loss_target': out['loss_target'], 'm_mod_w': out['m_mod_w'], 'm_mod_b': out['m_mod_b'], 'm_ln1_g': out['m_ln1_g'], 'm_ln1_b': out['m_ln1_b'], 'm_ln2_g': out['m_ln2_g'], 'm_ln2_b': out['m_ln2_b'], 'm_ffn_w_in': out['m_ffn_w_in'], 'm_ffn_w_out': out['m_ffn_w_out'], 'm_gm_w_in': out['m_gm_w_in'], 'm_gm_b_in': out['m_gm_b_in'], 'm_gm_ln_g': out['m_gm_ln_g'], 'm_gm_ln_b': out['m_gm_ln_b'], 'm_gm_w_s': out['m_gm_w_s'], 'm_gm_b_s': out['m_gm_b_s'], 'm_gm_w_out': out['m_gm_w_out'], 'm_fox_w_in': out['m_fox_w_in'], 'm_fox_b_f': out['m_fox_b_f'], 'm_fox_w_out': out['m_fox_w_out'], 'm_sb_w_in': out['m_sb_w_in'], 'm_sb_w_out': out['m_sb_w_out'], 'm_cv_w_in': out['m_cv_w_in'], 'm_cv_b_in': out['m_cv_b_in'], 'm_cv_dw': out['m_cv_dw'], 'm_cv_dw_b': out['m_cv_dw_b'], 'm_cv_ln_g': out['m_cv_ln_g'], 'm_cv_ln_b': out['m_cv_ln_b'], 'm_cv_w_out': out['m_cv_w_out'], 'm_cv_b_out': out['m_cv_b_out'], 'v_mod_w': out['v_mod_w'], 'v_mod_b': out['v_mod_b'], 'v_ln1_g': out['v_ln1_g'], 'v_ln1_b': out['v_ln1_b'], 'v_ln2_g': out['v_ln2_g'], 'v_ln2_b': out['v_ln2_b'], 'v_ffn_w_in': out['v_ffn_w_in'], 'v_ffn_w_out': out['v_ffn_w_out'], 'v_gm_w_in': out['v_gm_w_in'], 'v_gm_b_in': out['v_gm_b_in'], 'v_gm_ln_g': out['v_gm_ln_g'], 'v_gm_ln_b': out['v_gm_ln_b'], 'v_gm_w_s': out['v_gm_w_s'], 'v_gm_b_s': out['v_gm_b_s'], 'v_gm_w_out': out['v_gm_w_out'], 'v_fox_w_in': out['v_fox_w_in'], 'v_fox_b_f': out['v_fox_b_f'], 'v_fox_w_out': out['v_fox_w_out'], 'v_sb_w_in': out['v_sb_w_in'], 'v_sb_w_out': out['v_sb_w_out'], 'v_cv_w_in': out['v_cv_w_in'], 'v_cv_b_in': out['v_cv_b_in'], 'v_cv_dw': out['v_cv_dw'], 'v_cv_dw_b': out['v_cv_dw_b'], 'v_cv_ln_g': out['v_cv_ln_g'], 'v_cv_ln_b': out['v_cv_ln_b'], 'v_cv_w_out': out['v_cv_w_out'], 'v_cv_b_out': out['v_cv_b_out']}


def _loss(weights, diff, rest, loss_target):
    with _jax.named_scope("forward"):
        args = {**rest, TWIN_DIFF_INPUT: diff, **{k: w.astype(_WEIGHT_DTYPES[k]) for k, w in weights.items()}}
        y = _forward(args)
    with _jax.named_scope("loss_head"):
        err = _jnp.square(y.astype(_jnp.float32) - loss_target)
        return 0.5 * _jnp.sum(_jnp.mean(err, axis=-1)) if err.ndim else 0.5 * err


def _adamw(w, g, m, v):
    m = ADAM_B1 * m + (1.0 - ADAM_B1) * g
    v = ADAM_B2 * v + (1.0 - ADAM_B2) * _jnp.square(g)
    m_hat = m / (1.0 - ADAM_B1 ** ADAM_STEP)
    v_hat = v / (1.0 - ADAM_B2 ** ADAM_STEP)
    delta = -ADAM_LR * (m_hat / (_jnp.sqrt(v_hat) + ADAM_EPS) + ADAM_WD * w)
    return delta, m, v


def reference(x, c, mod_w, mod_b, ln1_g, ln1_b, ln2_g, ln2_b, ffn_w_in, ffn_w_out, gm_w_in, gm_b_in, gm_ln_g, gm_ln_b, gm_w_s, gm_b_s, gm_w_out, fox_w_in, fox_b_f, fox_w_out, sb_w_in, sb_w_out, cv_w_in, cv_b_in, cv_dw, cv_dw_b, cv_ln_g, cv_ln_b, cv_w_out, cv_b_out, loss_target, m_mod_w, m_mod_b, m_ln1_g, m_ln1_b, m_ln2_g, m_ln2_b, m_ffn_w_in, m_ffn_w_out, m_gm_w_in, m_gm_b_in, m_gm_ln_g, m_gm_ln_b, m_gm_w_s, m_gm_b_s, m_gm_w_out, m_fox_w_in, m_fox_b_f, m_fox_w_out, m_sb_w_in, m_sb_w_out, m_cv_w_in, m_cv_b_in, m_cv_dw, m_cv_dw_b, m_cv_ln_g, m_cv_ln_b, m_cv_w_out, m_cv_b_out, v_mod_w, v_mod_b, v_ln1_g, v_ln1_b, v_ln2_g, v_ln2_b, v_ffn_w_in, v_ffn_w_out, v_gm_w_in, v_gm_b_in, v_gm_ln_g, v_gm_ln_b, v_gm_w_s, v_gm_b_s, v_gm_w_out, v_fox_w_in, v_fox_b_f, v_fox_w_out, v_sb_w_in, v_sb_w_out, v_cv_w_in, v_cv_b_in, v_cv_dw, v_cv_dw_b, v_cv_ln_g, v_cv_ln_b, v_cv_w_out, v_cv_b_out):
    given = dict(x=x, c=c, mod_w=mod_w, mod_b=mod_b, ln1_g=ln1_g, ln1_b=ln1_b, ln2_g=ln2_g, ln2_b=ln2_b, ffn_w_in=ffn_w_in, ffn_w_out=ffn_w_out, gm_w_in=gm_w_in, gm_b_in=gm_b_in, gm_ln_g=gm_ln_g, gm_ln_b=gm_ln_b, gm_w_s=gm_w_s, gm_b_s=gm_b_s, gm_w_out=gm_w_out, fox_w_in=fox_w_in, fox_b_f=fox_b_f, fox_w_out=fox_w_out, sb_w_in=sb_w_in, sb_w_out=sb_w_out, cv_w_in=cv_w_in, cv_b_in=cv_b_in, cv_dw=cv_dw, cv_dw_b=cv_dw_b, cv_ln_g=cv_ln_g, cv_ln_b=cv_ln_b, cv_w_out=cv_w_out, cv_b_out=cv_b_out, loss_target=loss_target, m_mod_w=m_mod_w, m_mod_b=m_mod_b, m_ln1_g=m_ln1_g, m_ln1_b=m_ln1_b, m_ln2_g=m_ln2_g, m_ln2_b=m_ln2_b, m_ffn_w_in=m_ffn_w_in, m_ffn_w_out=m_ffn_w_out, m_gm_w_in=m_gm_w_in, m_gm_b_in=m_gm_b_in, m_gm_ln_g=m_gm_ln_g, m_gm_ln_b=m_gm_ln_b, m_gm_w_s=m_gm_w_s, m_gm_b_s=m_gm_b_s, m_gm_w_out=m_gm_w_out, m_fox_w_in=m_fox_w_in, m_fox_b_f=m_fox_b_f, m_fox_w_out=m_fox_w_out, m_sb_w_in=m_sb_w_in, m_sb_w_out=m_sb_w_out, m_cv_w_in=m_cv_w_in, m_cv_b_in=m_cv_b_in, m_cv_dw=m_cv_dw, m_cv_dw_b=m_cv_dw_b, m_cv_ln_g=m_cv_ln_g, m_cv_ln_b=m_cv_ln_b, m_cv_w_out=m_cv_w_out, m_cv_b_out=m_cv_b_out, v_mod_w=v_mod_w, v_mod_b=v_mod_b, v_ln1_g=v_ln1_g, v_ln1_b=v_ln1_b, v_ln2_g=v_ln2_g, v_ln2_b=v_ln2_b, v_ffn_w_in=v_ffn_w_in, v_ffn_w_out=v_ffn_w_out, v_gm_w_in=v_gm_w_in, v_gm_b_in=v_gm_b_in, v_gm_ln_g=v_gm_ln_g, v_gm_ln_b=v_gm_ln_b, v_gm_w_s=v_gm_w_s, v_gm_b_s=v_gm_b_s, v_gm_w_out=v_gm_w_out, v_fox_w_in=v_fox_w_in, v_fox_b_f=v_fox_b_f, v_fox_w_out=v_fox_w_out, v_sb_w_in=v_sb_w_in, v_sb_w_out=v_sb_w_out, v_cv_w_in=v_cv_w_in, v_cv_b_in=v_cv_b_in, v_cv_dw=v_cv_dw, v_cv_dw_b=v_cv_dw_b, v_cv_ln_g=v_cv_ln_g, v_cv_ln_b=v_cv_ln_b, v_cv_w_out=v_cv_w_out, v_cv_b_out=v_cv_b_out)
    weights = {n: given[n] for n in TWIN_WEIGHTS}
    shared = {n: given[n] for n in SHARED_INPUTS}
    per_example = {n: given[n] for n in ['x', 'c']}
    grad_fn = _jax.value_and_grad(_loss, argnums=(0, 1))

    def one_microbatch(ex, loss_target):
        ex = dict(ex)
        diff = ex.pop(TWIN_DIFF_INPUT)
        return grad_fn(weights, diff, {**shared, **ex}, loss_target)

    if N_MICROBATCH == 1:
        loss, (grad_w, grad_x) = one_microbatch(per_example, given["loss_target"])
    else:
        def body(carry, xs):
            loss_sum, grad_sum = carry
            l_k, (gw_k, gx_k) = one_microbatch(xs[0], xs[1])
            with _jax.named_scope("update"):
                return (loss_sum + l_k, _jax.tree.map(_jnp.add, grad_sum, gw_k)), gx_k

        init = (_jnp.zeros((), _jnp.float32), _jax.tree.map(_jnp.zeros_like, weights))
        (loss, grad_w), grad_x = _jax.lax.scan(body, init, (per_example, given["loss_target"]))
    with _jax.named_scope("update"):
        delta_w, new_m, new_v = {}, {}, {}
        for n in TWIN_WEIGHTS:
            delta_w[n], new_m[n], new_v[n] = _adamw(weights[n], grad_w[n], given["m_" + n], given["v_" + n])
    return (loss, grad_x, *[grad_w[n] for n in TWIN_WEIGHTS], *[delta_w[n] for n in TWIN_WEIGHTS],
            *[new_m[n] for n in TWIN_WEIGHTS], *[new_v[n] for n in TWIN_WEIGHTS])
```

```python
import functools
import math

import jax
import jax.numpy as jnp
from jax import lax
from jax.experimental import pallas as pl
from jax.experimental.pallas import tpu as pltpu

F32 = jnp.float32
BF16 = jnp.bfloat16
MESH = pl.DeviceIdType.MESH

N_DEV = 8
HEAD_DIM = 64
LANES = 128
GM_CHUNK = 128
GM_GROUPS = 8
CONV_WIDTH = 31
CONV_HALO = 32
LN_EPS = 1e-5
NEG_INF = -1e30

ADAM_LR = 0.001
ADAM_B1 = 0.9
ADAM_B2 = 0.999
ADAM_EPS = 1e-08
ADAM_WD = 0.01
ADAM_STEP = 10

TM = 512
TMM = 1024
TQ = 256
TT = 512
VMEM_BIG = 56 * 1024 * 1024

NN = (((1,), (0,)), ((), ()))
NT = (((1,), (1,)), ((), ()))
TN = (((0,), (0,)), ((), ()))


def _call(body, name, out_shape, grid=None, in_specs=None, out_specs=None, scratch=(), vmem=None):
    params = {}
    if grid is not None:
        params["dimension_semantics"] = ("arbitrary",) * len(grid)
    if vmem is not None:
        params["vmem_limit_bytes"] = vmem
    kw = {}
    if grid is not None:
        kw["grid"] = grid
    if in_specs is not None:
        kw["in_specs"] = in_specs
    if out_specs is not None:
        kw["out_specs"] = out_specs
    return pl.pallas_call(body, name=name, out_shape=out_shape, scratch_shapes=list(scratch),
                          compiler_params=pltpu.CompilerParams(**params), **kw)


def _sds(shape, dtype):
    return jax.ShapeDtypeStruct(tuple(shape), dtype)


def _dot(a, b, dims=NN):
    return lax.dot_general(a.astype(BF16), b.astype(BF16), dims, preferred_element_type=F32)


def _split3(x):
    h1 = x.astype(BF16)
    r1 = x - h1.astype(F32)
    h2 = r1.astype(BF16)
    h3 = (r1 - h2.astype(F32)).astype(BF16)
    return h1, h2, h3


def _dot_exact(x, m, dims=NN):
    h1, h2, h3 = _split3(x)
    d = lambda h: lax.dot_general(h, m, dims, preferred_element_type=F32)
    return (d(h1) + d(h2)) + d(h3)


def _dot_exact_rhs(m, x, dims=NN):
    h1, h2, h3 = _split3(x)
    d = lambda h: lax.dot_general(m, h, dims, preferred_element_type=F32)
    return (d(h1) + d(h2)) + d(h3)


def _silu(x):
    return x * jax.nn.sigmoid(x)


def _gelu(x):
    return 0.5 * x * (1.0 + lax.erf(x * (2.0 ** -0.5)))


def _gelu_grad(x):
    return 0.5 * (1.0 + lax.erf(x * (2.0 ** -0.5))) + x * jnp.exp(-0.5 * x * x) * ((2.0 * math.pi) ** -0.5)


def _log_sigmoid(z):
    return jnp.minimum(z, 0.0) - jnp.log(1.0 + jnp.exp(-jnp.abs(z)))


def _ln_stats(r):
    mu = jnp.mean(r, axis=-1, keepdims=True)
    rc = r - mu
    var = jnp.mean(rc * rc, axis=-1, keepdims=True)
    return rc, lax.rsqrt(var + LN_EPS)


def _colsum(x):
    return jnp.sum(x, axis=0, keepdims=True)


def _peers():
    mx, my, mc = lax.axis_index("x"), lax.axis_index("y"), lax.axis_index("c")
    me = 4 * mx + 2 * my + mc
    out = []
    for k in range(1, N_DEV):
        px = 1 - mx if (k >> 2) & 1 else mx
        py = 1 - my if (k >> 1) & 1 else my
        pc = 1 - mc if k & 1 else mc
        out.append(((px, py, pc), 4 * px + 2 * py + pc))
    return me, out


def _exchange_small(x, name, reduce):
    rows, cols = x.shape

    def body(x_ref, o_ref, *rest):
        if reduce:
            land, send_sems, recv_sems, local_sem = rest
        else:
            land = o_ref
            send_sems, recv_sems, local_sem = rest
        me, peers = _peers()
        mine = pltpu.make_async_copy(x_ref, land.at[me], local_sem)
        mine.start()
        sends = []
        for k, (peer, _) in enumerate(peers):
            cp = pltpu.make_async_remote_copy(src_ref=x_ref, dst_ref=land.at[me], send_sem=send_sems.at[k],
                                              recv_sem=recv_sems.at[k], device_id=peer, device_id_type=MESH)
            cp.start()
            sends.append(cp)
        for k, (peer, blk) in enumerate(peers):
            pltpu.make_async_remote_copy(src_ref=x_ref, dst_ref=land.at[blk], send_sem=send_sems.at[k],
                                         recv_sem=recv_sems.at[k], device_id=peer, device_id_type=MESH).wait_recv()
        for cp in sends:
            cp.wait_send()
        mine.wait()
        if reduce:
            acc = land[0]
            for s in range(1, N_DEV):
                acc = acc + land[s]
            o_ref[...] = acc

    vm = pl.BlockSpec(memory_space=pltpu.VMEM)
    scratch = [pltpu.SemaphoreType.DMA((N_DEV - 1,)), pltpu.SemaphoreType.DMA((N_DEV - 1,)), pltpu.SemaphoreType.DMA]
    if reduce:
        scratch = [pltpu.VMEM((N_DEV, rows, cols), F32)] + scratch
        out = _sds((rows, cols), F32)
    else:
        out = _sds((N_DEV, rows, cols), F32)
    return _call(body, name, out, in_specs=[vm], out_specs=vm, scratch=scratch, vmem=VMEM_BIG)(x)


def _gather_weights(shards):
    n = len(shards)

    def body(*refs):
        ins, outs = refs[:n], refs[n:2 * n]
        send_sems, recv_sems, local_sems = refs[2 * n:]
        me, peers = _peers()
        local = []
        sends = []
        for a in range(n):
            cp = pltpu.make_async_copy(ins[a], outs[a].at[:, me], local_sems.at[a])
            cp.start()
            local.append(cp)
            for k, (peer, _) in enumerate(peers):
                cp = pltpu.make_async_remote_copy(src_ref=ins[a], dst_ref=outs[a].at[:, me], send_sem=send_sems.at[a, k],
                                                  recv_sem=recv_sems.at[a, k], device_id=peer, device_id_type=MESH)
                cp.start()
                sends.append(cp)
        for a in range(n):
            for k, (peer, blk) in enumerate(peers):
                pltpu.make_async_remote_copy(src_ref=ins[a], dst_ref=outs[a].at[:, blk], send_sem=send_sems.at[a, k],
                                             recv_sem=recv_sems.at[a, k], device_id=peer, device_id_type=MESH).wait_recv()
        for cp in sends:
            cp.wait_send()
        for cp in local:
            cp.wait()

    hbm = pl.BlockSpec(memory_space=pl.ANY)
    out = [_sds((s.shape[0], N_DEV) + s.shape[1:], s.dtype) for s in shards]
    scratch = [pltpu.SemaphoreType.DMA((n, N_DEV - 1)), pltpu.SemaphoreType.DMA((n, N_DEV - 1)), pltpu.SemaphoreType.DMA((n,))]
    return _call(body, "gather_weights", out, in_specs=[hbm] * n, out_specs=[hbm] * n, scratch=scratch)(*shards)


def _scatter_grads(grads):
    n = len(grads)

    def body(*refs):
        ins, outs = refs[:n], refs[n:2 * n]
        send_sems, recv_sems, local_sems = refs[2 * n:]
        me, peers = _peers()
        local = []
        sends = []
        for a in range(n):
            cp = pltpu.make_async_copy(ins[a].at[:, me], outs[a].at[me], local_sems.at[a])
            cp.start()
            local.append(cp)
            for k, (peer, blk) in enumerate(peers):
                cp = pltpu.make_async_remote_copy(src_ref=ins[a].at[:, blk], dst_ref=outs[a].at[me], send_sem=send_sems.at[a, k],
                                                  recv_sem=recv_sems.at[a, k], device_id=peer, device_id_type=MESH)
                cp.start()
                sends.append(cp)
        for a in range(n):
            for k, (peer, blk) in enumerate(peers):
                pltpu.make_async_remote_copy(src_ref=ins[a].at[:, me], dst_ref=outs[a].at[blk], send_sem=send_sems.at[a, k],
                                             recv_sem=recv_sems.at[a, k], device_id=peer, device_id_type=MESH).wait_recv()
        for cp in sends:
            cp.wait_send()
        for cp in local:
            cp.wait()

    hbm = pl.BlockSpec(memory_space=pl.ANY)
    out = [_sds((N_DEV, g.shape[0]) + g.shape[2:], g.dtype) for g in grads]
    scratch = [pltpu.SemaphoreType.DMA((n, N_DEV - 1)), pltpu.SemaphoreType.DMA((n, N_DEV - 1)), pltpu.SemaphoreType.DMA((n,))]
    return _call(body, "scatter_grads", out, in_specs=[hbm] * n, out_specs=[hbm] * n, scratch=scratch)(*grads)


def _mm(name, a, b, out, grid, a_spec, b_spec, o_spec, dims, extra=(), extra_specs=(), epilogue=None, vmem=VMEM_BIG):
    nred = grid[-1]
    red_axis = len(grid) - 1
    acc_shape = tuple(d for d in o_spec.block_shape if d is not None)
    n_extra = len(extra)

    def body(a_ref, b_ref, *rest):
        ex = rest[:n_extra]
        o_ref = rest[n_extra]

        def finish(acc):
            if epilogue is not None:
                acc = epilogue(acc, *[e[...] for e in ex])
            o_ref[...] = acc.astype(o_ref.dtype)

        prod = _dot(a_ref[...], b_ref[...], dims)
        if nred == 1:
            finish(prod)
        else:
            acc_ref = rest[n_extra + 1]
            r = pl.program_id(red_axis)

            @pl.when(r == 0)
            def _():
                acc_ref[...] = prod

            @pl.when(r > 0)
            def _():
                acc_ref[...] += prod

            @pl.when(r == nred - 1)
            def _():
                finish(acc_ref[...])

    scratch = [pltpu.VMEM(acc_shape, F32)] if nred > 1 else []
    return _call(body, name, out, grid=grid, in_specs=[a_spec, b_spec, *extra_specs], out_specs=o_spec,
                 scratch=scratch, vmem=vmem)(a, b, *extra)


def _add(acc, x):
    return acc + x


def _proj_cols(name, h, w, n_slot, bias=None, out_dtype=F32):
    t, k = h.shape
    s = w.shape[1]
    tm = min(TMM, t)
    extra, especs, epi = (), (), None
    if bias is not None:
        extra, especs, epi = (bias,), (pl.BlockSpec((1, n_slot), lambda j, i, r: (0, j)),), _add
    return _mm(name, h, w, _sds((t, s * n_slot), out_dtype), (s, t // tm, 1),
               pl.BlockSpec((tm, k), lambda j, i, r: (i, 0)),
               pl.BlockSpec((None, None, k, n_slot), lambda j, i, r: (0, j, 0, 0)),
               pl.BlockSpec((tm, n_slot), lambda j, i, r: (i, j)), NN, extra, especs, epi)


def _grad_cols(name, h, g, n_slot):
    t, k = h.shape
    s = g.shape[1] // n_slot
    tk = min(TMM, t)
    return _mm(name, h, g, _sds((1, s, k, n_slot), F32), (s, t // tk),
               pl.BlockSpec((tk, k), lambda j, r: (r, 0)),
               pl.BlockSpec((tk, n_slot), lambda j, r: (r, j)),
               pl.BlockSpec((None, None, k, n_slot), lambda j, r: (0, j, 0, 0)), TN)


def _back_cols(name, g, w, n_slot, extra=(), extra_specs=(), epilogue=None):
    t = g.shape[0]
    s, k = w.shape[1], w.shape[2]
    tm = min(TMM, t)
    return _mm(name, g, w, _sds((t, k), F32), (t // tm, s),
               pl.BlockSpec((tm, n_slot), lambda i, r: (i, r)),
               pl.BlockSpec((None, None, k, n_slot), lambda i, r: (0, r, 0, 0)),
               pl.BlockSpec((tm, k), lambda i, r: (i, 0)), NT, extra, extra_specs, epilogue)


def _mm_plain(name, a, b, dims, out_dtype=F32, bias=None):
    if dims == TN:
        t, k = a.shape
        n = b.shape[1]
        tk = min(TMM, t)
        return _mm(name, a, b, _sds((k, n), out_dtype), (1, t // tk),
                   pl.BlockSpec((tk, k), lambda j, r: (r, 0)), pl.BlockSpec((tk, n), lambda j, r: (r, 0)),
                   pl.BlockSpec((k, n), lambda j, r: (0, 0)), TN)
    t = a.shape[0]
    tm = min(TMM, t)
    n = b.shape[1] if dims == NN else b.shape[0]
    extra, especs, epi = (), (), None
    if bias is not None:
        extra, especs, epi = (bias,), (pl.BlockSpec((1, n), lambda i, r: (0, 0)),), _add
    return _mm(name, a, b, _sds((t, n), out_dtype), (t // tm, 1),
               pl.BlockSpec((tm, a.shape[1]), lambda i, r: (i, 0)), pl.BlockSpec(b.shape, lambda i, r: (0, 0)),
               pl.BlockSpec((tm, n), lambda i, r: (i, 0)), dims, extra, especs, epi)


def _mod_fwd(c_all, mod_w, mod_b_loc):
    nl, d, n = mod_w.shape
    nb = c_all.shape[0]

    def body(c_ref, w_ref, b_ref, o_ref):
        o_ref[...] = _dot(_silu(c_ref[...]), w_ref[...]) + b_ref[...]

    return _call(body, "mod_fwd", _sds((nl, nb, n), F32), grid=(nl,),
                 in_specs=[pl.BlockSpec((nb, d), lambda l: (0, 0)), pl.BlockSpec((None, d, n), lambda l: (l, 0, 0)),
                           pl.BlockSpec((None, 1, n), lambda l: (l, 0, 0))],
                 out_specs=pl.BlockSpec((None, nb, n), lambda l: (l, 0, 0)))(c_all, mod_w, mod_b_loc)


def _mod_bwd(c_all, dmod_loc, dmod_all):
    nl, nb, n = dmod_loc.shape
    d = c_all.shape[1]
    n_all = dmod_all.shape[2]

    def body(c_ref, dl_ref, da_ref, gw_ref, gb_ref):
        gw_ref[...] = _dot(_silu(c_ref[...]), dl_ref[...], TN)
        gb_ref[...] = _colsum(da_ref[...])

    return _call(body, "mod_bwd", (_sds((nl, d, n), F32), _sds((nl, 1, n_all), F32)), grid=(nl,),
                 in_specs=[pl.BlockSpec((nb, d), lambda l: (0, 0)), pl.BlockSpec((None, nb, n), lambda l: (l, 0, 0)),
                           pl.BlockSpec((None, nb, n_all), lambda l: (l, 0, 0))],
                 out_specs=(pl.BlockSpec((None, d, n), lambda l: (l, 0, 0)), pl.BlockSpec((None, 1, n_all), lambda l: (l, 0, 0))),
                 )(c_all, dmod_loc, dmod_all)


def _row_spec(d, tpb):
    return pl.BlockSpec((None, 1, d), lambda i: (i // tpb, 0, 0))


def _tile_spec(tm, d):
    return pl.BlockSpec((tm, d), lambda i: (i, 0))


def _vec_spec(d):
    return pl.BlockSpec((1, d), lambda i: (0, 0))


def _modulate(x, sc, sh, seq):
    t, d = x.shape
    tm = min(TM, seq)
    tpb = seq // tm

    def body(x_ref, sc_ref, sh_ref, h_ref):
        h_ref[...] = (x_ref[...] * (1.0 + sc_ref[...]) + sh_ref[...]).astype(BF16)

    return _call(body, "modulate", _sds((t, d), BF16), grid=(t // tm,),
                 in_specs=[_tile_spec(tm, d), _row_spec(d, tpb), _row_spec(d, tpb)], out_specs=_tile_spec(tm, d))(x, sc, sh)


def _lnres_fwd(x, y, gate, lg, lb, alpha, seq, nxt=None):
    t, d = x.shape
    tm = min(TM, seq)
    tpb = seq // tm

    def body(x_ref, y_ref, g_ref, lg_ref, lb_ref, *rest):
        r = alpha * x_ref[...] + (1.0 + g_ref[...]) * y_ref[...]
        rc, rstd = _ln_stats(r)
        xn = rc * rstd * lg_ref[...] + lb_ref[...]
        if nxt is None:
            rest[0][...] = xn
        else:
            sc_ref, sh_ref, xo_ref, h_ref = rest
            xo_ref[...] = xn
            h_ref[...] = (xn * (1.0 + sc_ref[...]) + sh_ref[...]).astype(BF16)

    ins = [_tile_spec(tm, d), _tile_spec(tm, d), _row_spec(d, tpb), _vec_spec(d), _vec_spec(d)]
    if nxt is None:
        return _call(body, "lnres_fwd_last", _sds((t, d), F32), grid=(t // tm,), in_specs=ins,
                     out_specs=_tile_spec(tm, d))(x, y, gate, lg, lb)
    return _call(body, "lnres_fwd", (_sds((t, d), F32), _sds((t, d), BF16)), grid=(t // tm,),
                 in_specs=ins + [_row_spec(d, tpb), _row_spec(d, tpb)],
                 out_specs=(_tile_spec(tm, d), _tile_spec(tm, d)))(x, y, gate, lg, lb, *nxt)


def _loss_head(x, tgt):
    t, d = x.shape
    tm = min(TM, t)

    def body(x_ref, t_ref, dx_ref, sq_ref):
        e = x_ref[...] - t_ref[...]
        dx_ref[...] = e * (1.0 / d)

        @pl.when(pl.program_id(0) == 0)
        def _():
            sq_ref[...] = jnp.zeros_like(sq_ref)

        sq_ref[...] += _colsum(e * e)

    return _call(body, "loss_head", (_sds((t, d), F32), _sds((1, d), F32)), grid=(t // tm,),
                 in_specs=[_tile_spec(tm, d), _tile_spec(tm, d)], out_specs=(_tile_spec(tm, d), _vec_spec(d)))(x, tgt)


def _lnres_bwd(dxo, x, y, gate, lg, alpha, seq):
    t, d = x.shape
    tm = min(TM, seq)
    tpb = seq // tm
    nb = t // seq

    def body(dxo_ref, x_ref, y_ref, g_ref, lg_ref, dxr_ref, dy_ref, dlg_ref, dlb_ref, dys_ref, dg_ref):
        i = pl.program_id(0)
        yv = y_ref[...]
        r = alpha * x_ref[...] + (1.0 + g_ref[...]) * yv
        rc, rstd = _ln_stats(r)
        xhat = rc * rstd
        dxo_v = dxo_ref[...]
        dxh = dxo_v * lg_ref[...]
        m1 = jnp.mean(dxh, axis=-1, keepdims=True)
        m2 = jnp.mean(dxh * xhat, axis=-1, keepdims=True)
        dr = rstd * (dxh - m1 - xhat * m2)
        dyv = (1.0 + g_ref[...]) * dr
        dxr_ref[...] = alpha * dr
        dy_ref[...] = dyv.astype(BF16)

        @pl.when(i == 0)
        def _():
            dlg_ref[...] = jnp.zeros_like(dlg_ref)
            dlb_ref[...] = jnp.zeros_like(dlb_ref)
            dys_ref[...] = jnp.zeros_like(dys_ref)

        @pl.when(i % tpb == 0)
        def _():
            dg_ref[...] = jnp.zeros_like(dg_ref)

        dlg_ref[...] += _colsum(dxo_v * xhat)
        dlb_ref[...] += _colsum(dxo_v)
        dys_ref[...] += _colsum(dyv)
        dg_ref[...] += _colsum(dr * yv)

    return _call(body, "lnres_bwd",
                 (_sds((t, d), F32), _sds((t, d), BF16), _sds((1, d), F32), _sds((1, d), F32), _sds((1, d), F32), _sds((nb, 1, d), F32)),
                 grid=(t // tm,),
                 in_specs=[_tile_spec(tm, d), _tile_spec(tm, d), _tile_spec(tm, d), _row_spec(d, tpb), _vec_spec(d)],
                 out_specs=(_tile_spec(tm, d), _tile_spec(tm, d), _vec_spec(d), _vec_spec(d), _vec_spec(d), _row_spec(d, tpb)),
                 )(dxo, x, y, gate, lg)


def _modulate_bwd(dh, dxr, x, sc, seq):
    t, d = x.shape
    tm = min(TM, seq)
    tpb = seq // tm
    nb = t // seq

    def body(dh_ref, dxr_ref, x_ref, sc_ref, dx_ref, dsc_ref, dsh_ref):
        dhv = dh_ref[...]
        dx_ref[...] = dxr_ref[...] + dhv * (1.0 + sc_ref[...])

        @pl.when(pl.program_id(0) % tpb == 0)
        def _():
            dsc_ref[...] = jnp.zeros_like(dsc_ref)
            dsh_ref[...] = jnp.zeros_like(dsh_ref)

        dsc_ref[...] += _colsum(dhv * x_ref[...])
        dsh_ref[...] += _colsum(dhv)

    return _call(body, "modulate_bwd", (_sds((t, d), F32), _sds((nb, 1, d), F32), _sds((nb, 1, d), F32)), grid=(t // tm,),
                 in_specs=[_tile_spec(tm, d), _tile_spec(tm, d), _tile_spec(tm, d), _row_spec(d, tpb)],
                 out_specs=(_tile_spec(tm, d), _row_spec(d, tpb), _row_spec(d, tpb)))(dh, dxr, x, sc)


def _ffn_in(h, w_in, layer):
    t, d = h.shape
    n = w_in.shape[3]
    half = N_DEV // 2
    tm = min(TM, t)

    def body(h_ref, wg_ref, wu_ref, g_ref, u_ref, a_ref):
        hv = h_ref[...]
        g = _dot(hv, wg_ref[...])
        u = _dot(hv, wu_ref[...])
        g_ref[...] = g
        u_ref[...] = u
        a_ref[...] = (_silu(g) * u).astype(BF16)

    blk = pl.BlockSpec((None, tm, n), lambda p, i: (p, i, 0))
    return _call(body, "ffn_in", (_sds((half, t, n), F32), _sds((half, t, n), F32), _sds((half, t, n), BF16)),
                 grid=(half, t // tm),
                 in_specs=[pl.BlockSpec((tm, d), lambda p, i: (i, 0)),
                           pl.BlockSpec((None, None, d, n), lambda p, i: (layer, p, 0, 0)),
                           pl.BlockSpec((None, None, d, n), lambda p, i: (layer, p + half, 0, 0))],
                 out_specs=(blk, blk, blk), vmem=VMEM_BIG)(h, w_in, w_in)


def _ffn_out(act, w_out, layer):
    half, t, n = act.shape
    d = w_out.shape[2]
    tm = min(TMM, t)
    return _mm("ffn_out", act, w_out, _sds((t, d), F32), (t // tm, half),
               pl.BlockSpec((None, tm, n), lambda i, r: (r, i, 0)),
               pl.BlockSpec((None, n, d), lambda i, r: (layer + r, 0, 0)),
               pl.BlockSpec((tm, d), lambda i, r: (i, 0)), NN)


def _ffn_dact(dy, w_out, hg, hu, layer):
    half, t, n = hg.shape
    d = dy.shape[1]
    tm = min(TM, t)

    def body(dy_ref, w_ref, g_ref, u_ref, dg_ref, du_ref):
        da = _dot(dy_ref[...], w_ref[...], NT)
        g = g_ref[...]
        sg = jax.nn.sigmoid(g)
        dg_ref[...] = (da * u_ref[...] * (sg * (1.0 + g * (1.0 - sg)))).astype(BF16)
        du_ref[...] = (da * (g * sg)).astype(BF16)

    blk = pl.BlockSpec((None, tm, n), lambda p, i: (p, i, 0))
    dg, du = _call(body, "ffn_dact", (_sds((half, t, n), BF16), _sds((half, t, n), BF16)), grid=(half, t // tm),
                   in_specs=[pl.BlockSpec((tm, d), lambda p, i: (i, 0)),
                             pl.BlockSpec((None, n, d), lambda p, i: (layer + p, 0, 0)), blk, blk],
                   out_specs=(blk, blk), vmem=VMEM_BIG)(dy, w_out, hg, hu)
    return dg, du


def _ffn_bwd_weights(h, dy, act, dg, du):
    half, t, n = act.shape
    d = h.shape[1]
    tk = min(TMM, t)
    a_spec = pl.BlockSpec((tk, d), lambda j, r: (r, 0))
    g_spec = pl.BlockSpec((None, tk, n), lambda j, r: (j, r, 0))
    o_spec = pl.BlockSpec((None, d, n), lambda j, r: (j, 0, 0))
    dwg = _mm("ffn_dw_gate", h, dg, _sds((half, d, n), F32), (half, t // tk), a_spec, g_spec, o_spec, TN)
    dwu = _mm("ffn_dw_up", h, du, _sds((half, d, n), F32), (half, t // tk), a_spec, g_spec, o_spec, TN)
    dwo = _mm("ffn_dw_out", act, dy, _sds((half, n, d), F32), (half, t // tk),
              pl.BlockSpec((None, tk, n), lambda j, r: (j, r, 0)), pl.BlockSpec((tk, d), lambda j, r: (r, 0)),
              pl.BlockSpec((None, n, d), lambda j, r: (j, 0, 0)), TN)
    return dwg, dwu, dwo


def _ffn_dh(dg, du, w_in, layer):
    half, t, n = dg.shape
    d = w_in.shape[2]
    tm = min(TMM, t)
    g_spec = pl.BlockSpec((None, tm, n), lambda i, r: (r, i, 0))
    o_spec = pl.BlockSpec((tm, d), lambda i, r: (i, 0))
    part = _mm("ffn_dh_gate", dg, w_in, _sds((t, d), F32), (t // tm, half), g_spec,
               pl.BlockSpec((None, None, d, n), lambda i, r: (layer, r, 0, 0)), o_spec, NT)
    return _mm("ffn_dh_up", du, w_in, _sds((t, d), F32), (t // tm, half), g_spec,
               pl.BlockSpec((None, None, d, n), lambda i, r: (layer, r + half, 0, 0)), o_spec, NT,
               (part,), (o_spec,), _add)


def _tril(n, strict=False):
    r = lax.broadcasted_iota(jnp.int32, (n, n), 0)
    c = lax.broadcasted_iota(jnp.int32, (n, n), 1)
    return c < r if strict else c <= r


def _gm_spatial_fwd(pre, lng, lnb, w_s, b_st, seq):
    t, w2 = pre.shape
    w = w2 // 2
    gd = w // GM_GROUPS
    tm = min(TM, seq)
    nch = tm // GM_CHUNK

    def body(pre_ref, lng_ref, lnb_ref, ws_ref, bs_ref, y_ref):
        v = _gelu(pre_ref[:, w:])
        vc, rstd = _ln_stats(v)
        vn = (vc * rstd * lng_ref[...] + lnb_ref[...]).astype(BF16)
        keep = _tril(GM_CHUNK)
        for g in range(GM_GROUPS):
            wm = jnp.where(keep, ws_ref[g], 0.0).astype(BF16)
            for ci in range(nch):
                rows = slice(ci * GM_CHUNK, (ci + 1) * GM_CHUNK)
                cols = slice(g * gd, (g + 1) * gd)
                sv = _dot(wm, vn[rows, cols]) + bs_ref[:, g:g + 1]
                u = _gelu(pre_ref[rows, cols])
                y_ref[rows, cols] = (u * sv).astype(BF16)

    return _call(body, "gm_spatial_fwd", _sds((t, w), BF16), grid=(t // tm,),
                 in_specs=[_tile_spec(tm, w2), _vec_spec(w), _vec_spec(w),
                           pl.BlockSpec((GM_GROUPS, GM_CHUNK, GM_CHUNK), lambda i: (0, 0, 0)),
                           pl.BlockSpec((GM_CHUNK, GM_GROUPS), lambda i: (0, 0))],
                 out_specs=_tile_spec(tm, w), vmem=VMEM_BIG)(pre, lng, lnb, w_s, b_st)


def _gm_spatial_bwd(pre, dyv, lng, lnb, w_s, b_st, seq):
    t, w2 = pre.shape
    w = w2 // 2
    gd = w // GM_GROUPS
    tm = min(TM, seq)
    nch = tm // GM_CHUNK

    def body(pre_ref, dyv_ref, lng_ref, lnb_ref, ws_ref, bs_ref, dpre_ref, dws_ref, dbs_ref, dlg_ref, dlb_ref, dbin_ref, dvn_ref):
        @pl.when(pl.program_id(0) == 0)
        def _():
            dws_ref[...] = jnp.zeros_like(dws_ref)
            dbs_ref[...] = jnp.zeros_like(dbs_ref)
            dlg_ref[...] = jnp.zeros_like(dlg_ref)
            dlb_ref[...] = jnp.zeros_like(dlb_ref)
            dbin_ref[...] = jnp.zeros_like(dbin_ref)

        pv = pre_ref[:, w:]
        v = _gelu(pv)
        vc, rstd = _ln_stats(v)
        vhat = vc * rstd
        vn = (vhat * lng_ref[...] + lnb_ref[...]).astype(BF16)
        keep = _tril(GM_CHUNK)
        dbs_cols = []
        for g in range(GM_GROUPS):
            wm = jnp.where(keep, ws_ref[g], 0.0).astype(BF16)
            dwm = jnp.zeros((GM_CHUNK, GM_CHUNK), F32)
            dbs = jnp.zeros((GM_CHUNK, 1), F32)
            for ci in range(nch):
                rows = slice(ci * GM_CHUNK, (ci + 1) * GM_CHUNK)
                cols = slice(g * gd, (g + 1) * gd)
                vn_b = vn[rows, cols]
                sv = _dot(wm, vn_b) + bs_ref[:, g:g + 1]
                pu = pre_ref[rows, cols]
                dy = dyv_ref[rows, cols]
                du = dy * sv
                dsv = dy * _gelu(pu)
                dpu = du * _gelu_grad(pu)
                dpre_ref[rows, cols] = dpu.astype(BF16)
                dbin_ref[:, cols] += _colsum(dpu)
                dsv_b = dsv.astype(BF16)
                dwm = dwm + _dot(dsv_b, vn_b, NT)
                dbs = dbs + jnp.sum(dsv, axis=-1, keepdims=True)
                dvn_ref[rows, cols] = _dot(wm, dsv_b, TN)
            dws_ref[g] += jnp.where(keep, dwm, 0.0)
            dbs_cols.append(dbs)
        dbs_ref[...] += jnp.concatenate(dbs_cols, axis=1)
        dvn = dvn_ref[...]
        dlg_ref[...] += _colsum(dvn * vhat)
        dlb_ref[...] += _colsum(dvn)
        dvh = dvn * lng_ref[...]
        m1 = jnp.mean(dvh, axis=-1, keepdims=True)
        m2 = jnp.mean(dvh * vhat, axis=-1, keepdims=True)
        dv = rstd * (dvh - m1 - vhat * m2)
        dpv = dv * _gelu_grad(pv)
        dpre_ref[:, w:] = dpv.astype(BF16)
        dbin_ref[:, w:] += _colsum(dpv)

    full3 = pl.BlockSpec((GM_GROUPS, GM_CHUNK, GM_CHUNK), lambda i: (0, 0, 0))
    bst = pl.BlockSpec((GM_CHUNK, GM_GROUPS), lambda i: (0, 0))
    return _call(body, "gm_spatial_bwd",
                 (_sds((t, w2), BF16), _sds((GM_GROUPS, GM_CHUNK, GM_CHUNK), F32), _sds((GM_CHUNK, GM_GROUPS), F32),
                  _sds((1, w), F32), _sds((1, w), F32), _sds((1, w2), F32)),
                 grid=(t // tm,),
                 in_specs=[_tile_spec(tm, w2), _tile_spec(tm, w), _vec_spec(w), _vec_spec(w), full3, bst],
                 out_specs=(_tile_spec(tm, w2), full3, bst, _vec_spec(w), _vec_spec(w), _vec_spec(w2)),
                 scratch=[pltpu.VMEM((tm, w), F32)], vmem=VMEM_BIG)(pre, dyv, lng, lnb, w_s, b_st)


def _head_masks():
    lane = lax.broadcasted_iota(jnp.int32, (1, LANES), 1)
    return lane < HEAD_DIM


def _two_heads(x, m0):
    z = jnp.zeros_like(x)
    return jnp.where(m0, x, z), jnp.where(m0, z, x)


def _qkv_specs(seq, nq, blocked_q):
    if blocked_q:
        q = pl.BlockSpec((None, TQ_(seq), LANES), lambda b, p, i: (0, b * nq + i, p))
        k = pl.BlockSpec((None, seq, LANES), lambda b, p, i: (1, b, p))
        v = pl.BlockSpec((None, seq, LANES), lambda b, p, i: (2, b, p))
    else:
        q = pl.BlockSpec((None, seq, LANES), lambda b, p: (0, b, p))
        k = pl.BlockSpec((None, seq, LANES), lambda b, p: (1, b, p))
        v = pl.BlockSpec((None, seq, LANES), lambda b, p: (2, b, p))
    return q, k, v


def TQ_(seq):
    return min(TQ, seq)


def _fox_gate_fwd(ft, b_f, seq):
    nh, t = ft.shape
    nch = seq // LANES

    def body(ft_ref, bf_ref, fr_ref, fc_ref):
        r = lax.broadcasted_iota(jnp.int32, (LANES, LANES), 0)
        c = lax.broadcasted_iota(jnp.int32, (LANES, LANES), 1)
        upper = jnp.where(r <= c, 1.0, 0.0).astype(BF16)
        rr = lax.broadcasted_iota(jnp.int32, (nh, LANES), 0)
        cc = lax.broadcasted_iota(jnp.int32, (nh, LANES), 1)
        eye = jnp.where(rr == cc, 1.0, 0.0).astype(BF16)
        carry = jnp.zeros((nh, 1), F32)
        for ci in range(nch):
            cols = slice(ci * LANES, (ci + 1) * LANES)
            lf = _log_sigmoid(ft_ref[:, cols] + bf_ref[...])
            cs = _dot_exact(lf, upper) + carry
            fr_ref[:, cols] = cs
            carry = cs[:, LANES - 1:LANES]
            fc_ref[cols, :] = _dot_exact(cs, eye, TN)[:, :nh]

    return _call(body, "fox_gate_fwd", (_sds((nh, t), F32), _sds((t, nh), F32)), grid=(t // seq,),
                 in_specs=[pl.BlockSpec((nh, seq), lambda b: (0, b)), pl.BlockSpec((nh, 1), lambda b: (0, 0))],
                 out_specs=(pl.BlockSpec((nh, seq), lambda b: (0, b)), pl.BlockSpec((seq, nh), lambda b: (b, 0))))(ft, b_f)


def _fox_gate_bwd(ft, b_f, dfk, dfq, seq):
    nh, t = ft.shape
    nch = seq // LANES

    def body(ft_ref, bf_ref, dfk_ref, dfq_ref, dl_ref, db_ref):
        @pl.when(pl.program_id(0) == 0)
        def _():
            db_ref[...] = jnp.zeros_like(db_ref)

        r = lax.broadcasted_iota(jnp.int32, (LANES, LANES), 0)
        c = lax.broadcasted_iota(jnp.int32, (LANES, LANES), 1)
        lower = jnp.where(r >= c, 1.0, 0.0).astype(BF16)
        carry = jnp.zeros((nh, 1), F32)
        tot = jnp.zeros((nh, 1), F32)
        for ci in reversed(range(nch)):
            cols = slice(ci * LANES, (ci + 1) * LANES)
            rc = _dot_exact(dfk_ref[:, cols] + dfq_ref[:, cols], lower) + carry
            carry = rc[:, 0:1]
            dl = rc * jax.nn.sigmoid(-(ft_ref[:, cols] + bf_ref[...]))
            dl_ref[:, cols] = dl
            tot = tot + jnp.sum(dl, axis=-1, keepdims=True)
        db_ref[...] += tot

    blk = pl.BlockSpec((nh, seq), lambda b: (0, b))
    one = pl.BlockSpec((nh, 1), lambda b: (0, 0))
    return _call(body, "fox_gate_bwd", (_sds((nh, t), F32), _sds((nh, 1), F32)), grid=(t // seq,),
                 in_specs=[blk, one, blk, blk], out_specs=(blk, one))(ft, b_f, dfk, dfq)


def _fox_fwd(qkv, fcol, frow, nb, seq):
    _, t, d = qkv.shape
    npair = d // LANES
    tq = TQ_(seq)
    nq = seq // tq
    scale = HEAD_DIM ** -0.5

    def body(q_ref, k_ref, v_ref, fc_ref, fr_ref, o_ref, lse_ref):
        i = pl.program_id(2)
        m0 = _head_masks()
        qm = _two_heads(q_ref[...], m0)
        fq = (fc_ref[:, 0:1], fc_ref[:, 1:2])
        row = lax.broadcasted_iota(jnp.int32, (tq, tq), 0)
        col = lax.broadcasted_iota(jnp.int32, (tq, tq), 1)

        def step(j, carry, diag):
            off = pl.multiple_of(j * tq, tq)
            kb = k_ref[pl.ds(off, tq), :]
            vb = v_ref[pl.ds(off, tq), :]
            nxt, parts = [], []
            for hh in range(2):
                m, l = carry[2 * hh], carry[2 * hh + 1]
                s = lax.dot_general(qm[hh], kb, NT, preferred_element_type=F32) * scale
                s = s + fq[hh] - fr_ref[hh:hh + 1, pl.ds(off, tq)]
                if diag:
                    s = jnp.where(col <= row, s, NEG_INF)
                mn = jnp.maximum(m, jnp.max(s, axis=-1, keepdims=True))
                a = jnp.exp(m - mn)
                p = jnp.exp(s - mn)
                nxt += [mn, a * l + jnp.sum(p, axis=-1, keepdims=True)]
                parts.append((a, jnp.dot(p.astype(BF16), vb, preferred_element_type=F32)))
            acc = carry[4]
            acc = jnp.where(m0, parts[0][0] * acc + parts[0][1], parts[1][0] * acc + parts[1][1])
            return (*nxt, acc)

        neg = jnp.full((tq, 1), NEG_INF, F32)
        zero = jnp.zeros((tq, 1), F32)
        carry = (neg, zero, neg, zero, jnp.zeros((tq, LANES), F32))
        carry = lax.fori_loop(0, i, lambda j, c: step(j, c, False), carry)
        m_a, l_a, m_b, l_b, acc = step(i, carry, True)
        o_ref[...] = acc / jnp.where(m0, l_a, l_b)
        lse_ref[:, 0:1] = m_a + jnp.log(l_a)
        lse_ref[:, 1:2] = m_b + jnp.log(l_b)

    q_spec, k_spec, v_spec = _qkv_specs(seq, nq, True)
    col_spec = pl.BlockSpec((None, tq, 2), lambda b, p, i: (p, b * nq + i, 0))
    return _call(body, "fox_fwd", (_sds((t, d), F32), _sds((npair, t, 2), F32)), grid=(nb, npair, nq),
                 in_specs=[q_spec, k_spec, v_spec, col_spec, pl.BlockSpec((None, 2, seq), lambda b, p, i: (p, 0, b))],
                 out_specs=(pl.BlockSpec((tq, LANES), lambda b, p, i: (b * nq + i, p)), col_spec),
                 vmem=VMEM_BIG)(qkv, qkv, qkv, fcol, frow)


def _fox_bwd(qkv, fcol, frow, o, do, lse, nb, seq):
    _, t, d = qkv.shape
    npair = d // LANES
    tq = TQ_(seq)
    nq = seq // tq
    scale = HEAD_DIM ** -0.5

    def body(q_ref, k_ref, v_ref, fc_ref, fr_ref, o_ref, do_ref, lse_ref, dqkv_ref, df_ref, dfq_ref, dk_acc, dv_acc):
        m0 = _head_masks()
        row = lax.broadcasted_iota(jnp.int32, (tq, tq), 0)
        col = lax.broadcasted_iota(jnp.int32, (tq, tq), 1)
        dk_acc[...] = jnp.zeros_like(dk_acc)
        dv_acc[...] = jnp.zeros_like(dv_acc)
        df_ref[...] = jnp.zeros_like(df_ref)

        def q_block(i, _):
            qoff = pl.multiple_of(i * tq, tq)
            qrows = pl.ds(qoff, tq)
            qm = _two_heads(q_ref[qrows, :], m0)
            dov = do_ref[qrows, :]
            dd = dov * o_ref[qrows, :]
            dm = _two_heads(dov.astype(BF16), m0)
            delta = (jnp.sum(jnp.where(m0, dd, 0.0), axis=-1, keepdims=True),
                     jnp.sum(jnp.where(m0, 0.0, dd), axis=-1, keepdims=True))
            fq = (fc_ref[qrows, 0:1], fc_ref[qrows, 1:2])
            ls = (lse_ref[qrows, 0:1], lse_ref[qrows, 1:2])

            def step(j, carry, diag):
                off = pl.multiple_of(j * tq, tq)
                krows = pl.ds(off, tq)
                kb = k_ref[krows, :]
                vb = v_ref[krows, :]
                dqs, rowsums = [], []
                dk = jnp.zeros((tq, LANES), F32)
                dv = jnp.zeros((tq, LANES), F32)
                for hh in range(2):
                    s = lax.dot_general(qm[hh], kb, NT, preferred_element_type=F32) * scale
                    s = s + fq[hh] - fr_ref[hh:hh + 1, krows]
                    if diag:
                        s = jnp.where(col <= row, s, NEG_INF)
                    p = jnp.exp(s - ls[hh])
                    dp = lax.dot_general(dm[hh], vb, NT, preferred_element_type=F32)
                    ds = p * (dp - delta[hh])
                    df_ref[hh:hh + 1, krows] -= _colsum(ds)
                    rowsums.append(carry[1 + hh] + jnp.sum(ds, axis=-1, keepdims=True))
                    ds_b = ds.astype(BF16)
                    dqs.append(jnp.dot(ds_b, kb, preferred_element_type=F32))
                    dk = dk + lax.dot_general(ds_b, qm[hh], TN, preferred_element_type=F32)
                    dv = dv + lax.dot_general(p.astype(BF16), dm[hh], TN, preferred_element_type=F32)
                dk_acc[krows, :] += dk
                dv_acc[krows, :] += dv
                return (carry[0] + jnp.where(m0, dqs[0], dqs[1]), *rowsums)

            zero = jnp.zeros((tq, 1), F32)
            carry = lax.fori_loop(0, i, lambda j, c: step(j, c, False), (jnp.zeros((tq, LANES), F32), zero, zero))
            dq, rs_a, rs_b = step(i, carry, True)
            dqkv_ref[0, qrows, :] = (dq * scale).astype(BF16)
            dfq_ref[qrows, 0:1] = rs_a
            dfq_ref[qrows, 1:2] = rs_b
            return 0

        lax.fori_loop(0, nq, q_block, 0)
        dqkv_ref[1] = (dk_acc[...] * scale).astype(BF16)
        dqkv_ref[2] = dv_acc[...].astype(BF16)

    q_spec, k_spec, v_spec = _qkv_specs(seq, nq, False)
    col_spec = pl.BlockSpec((None, seq, 2), lambda b, p: (p, b, 0))
    row_spec = pl.BlockSpec((None, 2, seq), lambda b, p: (p, 0, b))
    tile = pl.BlockSpec((seq, LANES), lambda b, p: (b, p))
    return _call(body, "fox_bwd", (_sds((3, t, d), BF16), _sds((npair, 2, t), F32), _sds((npair, t, 2), F32)), grid=(nb, npair),
                 in_specs=[q_spec, k_spec, v_spec, col_spec, row_spec, tile, tile, col_spec],
                 out_specs=(pl.BlockSpec((3, seq, LANES), lambda b, p: (0, b, p)), row_spec, col_spec),
                 scratch=[pltpu.VMEM((seq, LANES), F32), pltpu.VMEM((seq, LANES), F32)],
                 vmem=VMEM_BIG)(qkv, qkv, qkv, fcol, frow, o, do, lse)


def _split2(x):
    hi = x.astype(BF16)
    return hi, (x - hi.astype(F32)).astype(BF16)


def _sum_right(x, tri):
    hi, lo = _split2(x)
    return jnp.dot(hi, tri, preferred_element_type=F32) + jnp.dot(lo, tri, preferred_element_type=F32)


def _sb_scores(qm_h, kb, scale, mask):
    z = lax.dot_general(qm_h, kb, NT, preferred_element_type=F32) * scale
    lb = _log_sigmoid(z)
    l1m = lb - z
    if mask is not None:
        l1m = jnp.where(mask, l1m, 0.0)
    return lb, l1m


def _sb_fwd(qkv, nb, seq):
    _, t, d = qkv.shape
    npair = d // LANES
    tq = TQ_(seq)
    nq = seq // tq
    scale = HEAD_DIM ** -0.5

    def body(q_ref, k_ref, v_ref, o_ref, lt_ref):
        i = pl.program_id(2)
        m0 = _head_masks()
        qm = _two_heads(q_ref[...], m0)
        row = lax.broadcasted_iota(jnp.int32, (tq, tq), 0)
        col = lax.broadcasted_iota(jnp.int32, (tq, tq), 1)
        after = jnp.where(row > col, 1.0, 0.0).astype(BF16)

        def step(j, carry, diag):
            off = pl.multiple_of(j * tq, tq)
            kb = k_ref[pl.ds(off, tq), :]
            vb = v_ref[pl.ds(off, tq), :]
            mask = (col < row) if diag else None
            nxt, parts = [], []
            for hh in range(2):
                lb, l1m = _sb_scores(qm[hh], kb, scale, mask)
                rest = _sum_right(l1m, after) + carry[hh]
                a = jnp.exp(lb + rest)
                if diag:
                    a = jnp.where(mask, a, 0.0)
                parts.append(jnp.dot(a.astype(BF16), vb, preferred_element_type=F32))
                nxt.append(carry[hh] + jnp.sum(l1m, axis=-1, keepdims=True))
            return (*nxt, carry[2] + jnp.where(m0, parts[0], parts[1]))

        zero = jnp.zeros((tq, 1), F32)
        carry = step(i, (zero, zero, jnp.zeros((tq, LANES), F32)), True)
        carry = lax.fori_loop(0, i, lambda jj, c: step(i - 1 - jj, c, False), carry)
        o_ref[...] = carry[2]
        lt_ref[:, 0:1] = carry[0]
        lt_ref[:, 1:2] = carry[1]

    q_spec, k_spec, v_spec = _qkv_specs(seq, nq, True)
    return _call(body, "sb_fwd", (_sds((t, d), F32), _sds((npair, t, 2), F32)), grid=(nb, npair, nq),
                 in_specs=[q_spec, k_spec, v_spec],
                 out_specs=(pl.BlockSpec((tq, LANES), lambda b, p, i: (b * nq + i, p)),
                            pl.BlockSpec((None, tq, 2), lambda b, p, i: (p, b * nq + i, 0))), vmem=VMEM_BIG)(qkv, qkv, qkv)


def _sb_bwd(qkv, do, ltot, nb, seq):
    _, t, d = qkv.shape
    npair = d // LANES
    tq = TQ_(seq)
    nq = seq // tq
    scale = HEAD_DIM ** -0.5

    def body(q_ref, k_ref, v_ref, do_ref, lt_ref, dqkv_ref, dk_acc, dv_acc):
        m0 = _head_masks()
        row = lax.broadcasted_iota(jnp.int32, (tq, tq), 0)
        col = lax.broadcasted_iota(jnp.int32, (tq, tq), 1)
        upto = jnp.where(row <= col, 1.0, 0.0).astype(BF16)
        left_of = jnp.where(row < col, 1.0, 0.0).astype(BF16)
        dk_acc[...] = jnp.zeros_like(dk_acc)
        dv_acc[...] = jnp.zeros_like(dv_acc)

        def q_block(i, _):
            qoff = pl.multiple_of(i * tq, tq)
            qrows = pl.ds(qoff, tq)
            qm = _two_heads(q_ref[qrows, :], m0)
            dm = _two_heads(do_ref[qrows, :].astype(BF16), m0)
            ltot = (lt_ref[qrows, 0:1], lt_ref[qrows, 1:2])

            def step(j, carry, diag):
                off = pl.multiple_of(j * tq, tq)
                krows = pl.ds(off, tq)
                kb = k_ref[krows, :]
                vb = v_ref[krows, :]
                mask = (col < row) if diag else None
                nxt, dqs = [], []
                dk = jnp.zeros((tq, LANES), F32)
                dv = jnp.zeros((tq, LANES), F32)
                for hh in range(2):
                    cl, ce = carry[2 * hh], carry[2 * hh + 1]
                    lb, l1m = _sb_scores(qm[hh], kb, scale, mask)
                    a = jnp.exp(lb + (ltot[hh] - (_sum_right(l1m, upto) + cl)))
                    if diag:
                        a = jnp.where(mask, a, 0.0)
                    e = lax.dot_general(dm[hh], vb, NT, preferred_element_type=F32) * a
                    before = _sum_right(e, left_of) + ce
                    beta = jnp.exp(lb)
                    dz = e * (1.0 - beta) - before * beta
                    if diag:
                        dz = jnp.where(mask, dz, 0.0)
                    dz_b = dz.astype(BF16)
                    dqs.append(jnp.dot(dz_b, kb, preferred_element_type=F32))
                    dk = dk + lax.dot_general(dz_b, qm[hh], TN, preferred_element_type=F32)
                    dv = dv + lax.dot_general(a.astype(BF16), dm[hh], TN, preferred_element_type=F32)
                    nxt += [cl + jnp.sum(l1m, axis=-1, keepdims=True), ce + jnp.sum(e, axis=-1, keepdims=True)]
                dk_acc[krows, :] += dk
                dv_acc[krows, :] += dv
                return (*nxt, carry[4] + jnp.where(m0, dqs[0], dqs[1]))

            zero = jnp.zeros((tq, 1), F32)
            carry = (zero, zero, zero, zero, jnp.zeros((tq, LANES), F32))
            carry = lax.fori_loop(0, i, lambda j, c: step(j, c, False), carry)
            carry = step(i, carry, True)
            dqkv_ref[0, qrows, :] = (carry[4] * scale).astype(BF16)
            return 0

        lax.fori_loop(0, nq, q_block, 0)
        dqkv_ref[1] = (dk_acc[...] * scale).astype(BF16)
        dqkv_ref[2] = dv_acc[...].astype(BF16)

    q_spec, k_spec, v_spec = _qkv_specs(seq, nq, False)
    tile = pl.BlockSpec((seq, LANES), lambda b, p: (b, p))
    return _call(body, "sb_bwd", _sds((3, t, d), BF16), grid=(nb, npair),
                 in_specs=[q_spec, k_spec, v_spec, tile, pl.BlockSpec((None, seq, 2), lambda b, p: (p, b, 0))],
                 out_specs=pl.BlockSpec((3, seq, LANES), lambda b, p: (0, b, p)),
                 scratch=[pltpu.VMEM((seq, LANES), F32), pltpu.VMEM((seq, LANES), F32)], vmem=VMEM_BIG)(qkv, qkv, qkv, do, ltot)


def _qkv_proj(name, h, w):
    t, d = h.shape
    tm = min(TMM, t)
    return _mm(name, h, w, _sds((3, t, d), BF16), (3, t // tm, 1),
               pl.BlockSpec((tm, d), lambda s, i, r: (i, 0)), pl.BlockSpec((d, d), lambda s, i, r: (0, s)),
               pl.BlockSpec((None, tm, d), lambda s, i, r: (s, i, 0)), NN)


def _qkv_dw(name, h, dqkv):
    t, d = h.shape
    tk = min(TMM, t)
    return _mm(name, h, dqkv, _sds((d, 3 * d), F32), (3, t // tk),
               pl.BlockSpec((tk, d), lambda s, r: (r, 0)), pl.BlockSpec((None, tk, d), lambda s, r: (s, r, 0)),
               pl.BlockSpec((d, d), lambda s, r: (0, s)), TN)


def _qkv_dh(name, dqkv, w, extra=(), extra_specs=(), epilogue=None):
    _, t, d = dqkv.shape
    tm = min(TMM, t)
    return _mm(name, dqkv, w, _sds((t, d), F32), (t // tm, 3),
               pl.BlockSpec((None, tm, d), lambda i, r: (r, i, 0)), pl.BlockSpec((d, d), lambda i, r: (0, r)),
               pl.BlockSpec((tm, d), lambda i, r: (i, 0)), NT, extra, extra_specs, epilogue)


def _glu(pre_block, d):
    return pre_block[:, :d] * jax.nn.sigmoid(pre_block[:, d:])


def _cv_conv_fwd(pre, dw, dwb, lng, lnb, seq):
    t, d2 = pre.shape
    d = d2 // 2
    tt = min(TT, seq)
    nt = seq // tt
    hb = tt // CONV_HALO

    def body(pre_ref, halo_ref, dw_ref, dwb_ref, lng_ref, lnb_ref, y1_ref, y2_ref, ext_ref):
        i = pl.program_id(1)
        ext_ref[0:CONV_HALO, :] = jnp.where(i == 0, 0.0, _glu(halo_ref[...], d))
        ext_ref[CONV_HALO:, :] = _glu(pre_ref[...], d)
        acc = jnp.zeros((tt, d), F32) + dwb_ref[...]
        for k in range(CONV_WIDTH):
            acc = acc + ext_ref[pl.ds(CONV_HALO - (CONV_WIDTH - 1) + k, tt), :] * dw_ref[k:k + 1, :]
        y1_ref[...] = acc
        yc, rstd = _ln_stats(acc)
        y2_ref[...] = _silu(yc * rstd * lng_ref[...] + lnb_ref[...]).astype(BF16)

    vec = pl.BlockSpec((1, d), lambda b, i: (0, 0))
    tile = pl.BlockSpec((tt, d), lambda b, i: (b * nt + i, 0))
    return _call(body, "cv_conv_fwd", (_sds((t, d), F32), _sds((t, d), BF16)), grid=(t // seq, nt),
                 in_specs=[pl.BlockSpec((tt, d2), lambda b, i: (b * nt + i, 0)),
                           pl.BlockSpec((CONV_HALO, d2), lambda b, i: (jnp.maximum((b * nt + i) * hb - 1, 0), 0)),
                           pl.BlockSpec((CONV_HALO, d), lambda b, i: (0, 0)), vec, vec, vec],
                 out_specs=(tile, tile), scratch=[pltpu.VMEM((tt + CONV_HALO, d), F32)], vmem=VMEM_BIG)(pre, pre, dw, dwb, lng, lnb)


def _cv_norm_bwd(y1, dy2, lng, lnb):
    t, d = y1.shape
    tm = min(TM, t)

    def body(y1_ref, dy2_ref, lng_ref, lnb_ref, dy1_ref, dlg_ref, dlb_ref, dsum_ref):
        @pl.when(pl.program_id(0) == 0)
        def _():
            dlg_ref[...] = jnp.zeros_like(dlg_ref)
            dlb_ref[...] = jnp.zeros_like(dlb_ref)
            dsum_ref[...] = jnp.zeros_like(dsum_ref)

        yc, rstd = _ln_stats(y1_ref[...])
        yhat = yc * rstd
        n = yhat * lng_ref[...] + lnb_ref[...]
        sg = jax.nn.sigmoid(n)
        dn = dy2_ref[...] * (sg * (1.0 + n * (1.0 - sg)))
        dlg_ref[...] += _colsum(dn * yhat)
        dlb_ref[...] += _colsum(dn)
        dyh = dn * lng_ref[...]
        m1 = jnp.mean(dyh, axis=-1, keepdims=True)
        m2 = jnp.mean(dyh * yhat, axis=-1, keepdims=True)
        dy1 = rstd * (dyh - m1 - yhat * m2)
        dy1_ref[...] = dy1
        dsum_ref[...] += _colsum(dy1)

    return _call(body, "cv_norm_bwd", (_sds((t, d), F32), _sds((1, d), F32), _sds((1, d), F32), _sds((1, d), F32)),
                 grid=(t // tm,), in_specs=[_tile_spec(tm, d), _tile_spec(tm, d), _vec_spec(d), _vec_spec(d)],
                 out_specs=(_tile_spec(tm, d), _vec_spec(d), _vec_spec(d), _vec_spec(d)))(y1, dy2, lng, lnb)


def _cv_conv_bwd(pre, dy1, dw, seq):
    t, d2 = pre.shape
    d = d2 // 2
    tt = min(TT, seq)
    nt = seq // tt
    hb = tt // CONV_HALO
    last_halo = t // CONV_HALO - 1

    def body(pre_ref, halo_ref, dy_ref, dyn_ref, dw_ref, dpre_ref, ddw_ref, dbin_ref, ext_ref, dext_ref):
        b, i = pl.program_id(0), pl.program_id(1)

        @pl.when((b == 0) & (i == 0))
        def _():
            ddw_ref[...] = jnp.zeros_like(ddw_ref)
            dbin_ref[...] = jnp.zeros_like(dbin_ref)

        pv = pre_ref[...]
        ext_ref[0:CONV_HALO, :] = jnp.where(i == 0, 0.0, _glu(halo_ref[...], d))
        ext_ref[CONV_HALO:, :] = _glu(pv, d)
        dyv = dy_ref[...]
        dext_ref[0:tt, :] = dyv
        dext_ref[tt:, :] = jnp.where(i == nt - 1, 0.0, dyn_ref[...])
        dy0 = jnp.zeros((tt, d), F32)
        for k in range(CONV_WIDTH):
            ddw_ref[k:k + 1, :] += _colsum(dyv * ext_ref[pl.ds(CONV_HALO - (CONV_WIDTH - 1) + k, tt), :])
            dy0 = dy0 + dext_ref[pl.ds(CONV_WIDTH - 1 - k, tt), :] * dw_ref[k:k + 1, :]
        a = pv[:, :d]
        sg = jax.nn.sigmoid(pv[:, d:])
        da = dy0 * sg
        dg = dy0 * a * sg * (1.0 - sg)
        dpre_ref[:, :d] = da.astype(BF16)
        dpre_ref[:, d:] = dg.astype(BF16)
        dbin_ref[:, :d] += _colsum(da)
        dbin_ref[:, d:] += _colsum(dg)

    return _call(body, "cv_conv_bwd", (_sds((t, d2), BF16), _sds((CONV_HALO, d), F32), _sds((1, d2), F32)), grid=(t // seq, nt),
                 in_specs=[pl.BlockSpec((tt, d2), lambda b, i: (b * nt + i, 0)),
                           pl.BlockSpec((CONV_HALO, d2), lambda b, i: (jnp.maximum((b * nt + i) * hb - 1, 0), 0)),
                           pl.BlockSpec((tt, d), lambda b, i: (b * nt + i, 0)),
                           pl.BlockSpec((CONV_HALO, d), lambda b, i: (jnp.minimum((b * nt + i + 1) * hb, last_halo), 0)),
                           pl.BlockSpec((CONV_HALO, d), lambda b, i: (0, 0))],
                 out_specs=(pl.BlockSpec((tt, d2), lambda b, i: (b * nt + i, 0)),
                            pl.BlockSpec((CONV_HALO, d), lambda b, i: (0, 0)), pl.BlockSpec((1, d2), lambda b, i: (0, 0))),
                 scratch=[pltpu.VMEM((tt + CONV_HALO, d), F32), pltpu.VMEM((tt + CONV_HALO, d), F32)],
                 vmem=VMEM_BIG)(pre, pre, dy1, dy1, dw)


def _adamw(name, w, m, v, g=None, parts=None):
    rows, cols = w.shape
    tr = rows
    for cand in ((512,) if parts is None else ()) + (256, 128, 64, 32, 16, 8):
        if rows % cand == 0 and rows > cand:
            tr = cand
            break
    bc1 = 1.0 - ADAM_B1 ** ADAM_STEP
    bc2 = 1.0 - ADAM_B2 ** ADAM_STEP

    def body(w_ref, m_ref, v_ref, g_ref, go_ref, d_ref, mo_ref, vo_ref):
        if parts is None:
            gv = g_ref[...]
        else:
            gv = g_ref[0]
            for s in range(1, N_DEV):
                gv = gv + g_ref[s]
        mn = ADAM_B1 * m_ref[...] + (1.0 - ADAM_B1) * gv
        vn = ADAM_B2 * v_ref[...] + (1.0 - ADAM_B2) * (gv * gv)
        m_hat = mn / bc1
        v_hat = vn / bc2
        go_ref[...] = gv
        d_ref[...] = -ADAM_LR * (m_hat / (jnp.sqrt(v_hat) + ADAM_EPS) + ADAM_WD * w_ref[...])
        mo_ref[...] = mn
        vo_ref[...] = vn

    blk = pl.BlockSpec((tr, cols), lambda i: (i, 0))
    g_in, g_spec = (g, blk) if parts is None else (parts, pl.BlockSpec((N_DEV, tr, cols), lambda i: (0, i, 0)))
    out = _sds((rows, cols), F32)
    return _call(body, name, (out, out, out, out), grid=(rows // tr,), in_specs=[blk, blk, blk, g_spec],
                 out_specs=(blk, blk, blk, blk), vmem=VMEM_BIG)(w, m, v, g_in)


def _pad_rows(a, rows):
    return jnp.pad(a, ((0, rows - a.shape[0]), (0, 0)))


def _full_cols(gathered, n):
    k = gathered.shape[2]
    return jnp.transpose(gathered[0], (1, 0, 2)).reshape(k, N_DEV * n)


def _col_blocks(full, n):
    k = full.shape[0]
    return jnp.transpose(full.reshape(k, N_DEV, n), (1, 0, 2))[None]


def kernel(x, c, mod_w, mod_b, ln1_g, ln1_b, ln2_g, ln2_b, ffn_w_in, ffn_w_out, gm_w_in, gm_b_in, gm_ln_g, gm_ln_b, gm_w_s, gm_b_s, gm_w_out, fox_w_in, fox_b_f, fox_w_out, sb_w_in, sb_w_out, cv_w_in, cv_b_in, cv_dw, cv_dw_b, cv_ln_g, cv_ln_b, cv_w_out, cv_b_out, loss_target, m_mod_w, m_mod_b, m_ln1_g, m_ln1_b, m_ln2_g, m_ln2_b, m_ffn_w_in, m_ffn_w_out, m_gm_w_in, m_gm_b_in, m_gm_ln_g, m_gm_ln_b, m_gm_w_s, m_gm_b_s, m_gm_w_out, m_fox_w_in, m_fox_b_f, m_fox_w_out, m_sb_w_in, m_sb_w_out, m_cv_w_in, m_cv_b_in, m_cv_dw, m_cv_dw_b, m_cv_ln_g, m_cv_ln_b, m_cv_w_out, m_cv_b_out, v_mod_w, v_mod_b, v_ln1_g, v_ln1_b, v_ln2_g, v_ln2_b, v_ffn_w_in, v_ffn_w_out, v_gm_w_in, v_gm_b_in, v_gm_ln_g, v_gm_ln_b, v_gm_w_s, v_gm_b_s, v_gm_w_out, v_fox_w_in, v_fox_b_f, v_fox_w_out, v_sb_w_in, v_sb_w_out, v_cv_w_in, v_cv_b_in, v_cv_dw, v_cv_dw_b, v_cv_ln_g, v_cv_ln_b, v_cv_w_out, v_cv_b_out):
    weights = dict(mod_w=mod_w, mod_b=mod_b, ln1_g=ln1_g, ln1_b=ln1_b, ln2_g=ln2_g, ln2_b=ln2_b, ffn_w_in=ffn_w_in, ffn_w_out=ffn_w_out, gm_w_in=gm_w_in, gm_b_in=gm_b_in, gm_ln_g=gm_ln_g, gm_ln_b=gm_ln_b, gm_w_s=gm_w_s, gm_b_s=gm_b_s, gm_w_out=gm_w_out, fox_w_in=fox_w_in, fox_b_f=fox_b_f, fox_w_out=fox_w_out, sb_w_in=sb_w_in, sb_w_out=sb_w_out, cv_w_in=cv_w_in, cv_b_in=cv_b_in, cv_dw=cv_dw, cv_dw_b=cv_dw_b, cv_ln_g=cv_ln_g, cv_ln_b=cv_ln_b, cv_w_out=cv_w_out, cv_b_out=cv_b_out)
    mom1 = dict(mod_w=m_mod_w, mod_b=m_mod_b, ln1_g=m_ln1_g, ln1_b=m_ln1_b, ln2_g=m_ln2_g, ln2_b=m_ln2_b, ffn_w_in=m_ffn_w_in, ffn_w_out=m_ffn_w_out, gm_w_in=m_gm_w_in, gm_b_in=m_gm_b_in, gm_ln_g=m_gm_ln_g, gm_ln_b=m_gm_ln_b, gm_w_s=m_gm_w_s, gm_b_s=m_gm_b_s, gm_w_out=m_gm_w_out, fox_w_in=m_fox_w_in, fox_b_f=m_fox_b_f, fox_w_out=m_fox_w_out, sb_w_in=m_sb_w_in, sb_w_out=m_sb_w_out, cv_w_in=m_cv_w_in, cv_b_in=m_cv_b_in, cv_dw=m_cv_dw, cv_dw_b=m_cv_dw_b, cv_ln_g=m_cv_ln_g, cv_ln_b=m_cv_ln_b, cv_w_out=m_cv_w_out, cv_b_out=m_cv_b_out)
    mom2 = dict(mod_w=v_mod_w, mod_b=v_mod_b, ln1_g=v_ln1_g, ln1_b=v_ln1_b, ln2_g=v_ln2_g, ln2_b=v_ln2_b, ffn_w_in=v_ffn_w_in, ffn_w_out=v_ffn_w_out, gm_w_in=v_gm_w_in, gm_b_in=v_gm_b_in, gm_ln_g=v_gm_ln_g, gm_ln_b=v_gm_ln_b, gm_w_s=v_gm_w_s, gm_b_s=v_gm_b_s, gm_w_out=v_gm_w_out, fox_w_in=v_fox_w_in, fox_b_f=v_fox_b_f, fox_w_out=v_fox_w_out, sb_w_in=v_sb_w_in, sb_w_out=v_sb_w_out, cv_w_in=v_cv_w_in, cv_b_in=v_cv_b_in, cv_dw=v_cv_dw, cv_dw_b=v_cv_dw_b, cv_ln_g=v_cv_ln_g, cv_ln_b=v_cv_ln_b, cv_w_out=v_cv_w_out, cv_b_out=v_cv_b_out)
    names = list(weights)

    nb, seq, d = x.shape
    t = nb * seq
    nl = mod_w.shape[0]
    alpha = (2.0 * nl) ** 0.25
    me = 4 * lax.axis_index("x") + 2 * lax.axis_index("y") + lax.axis_index("c")
    xs = x.reshape(t, d)
    tgt = loss_target.reshape(t, d)
    n_mod = mod_w.shape[2]
    n_ffn = ffn_w_in.shape[2]
    n_heads = d // HEAD_DIM
    npair = d // LANES

    c_all = _exchange_small(_pad_rows(c, 8), "gather_c", False)[:, :nb].reshape(N_DEV * nb, d)
    mod_b_loc = lax.dynamic_slice_in_dim(mod_b, me * n_mod, n_mod, axis=1)[:, None, :]
    mod_loc = _mod_fwd(c_all, mod_w, mod_b_loc)
    mod_g = _exchange_small(mod_loc.reshape(nl * N_DEV * nb, n_mod), "gather_mod", False)
    mod_all = jnp.transpose(mod_g.reshape(N_DEV, nl, N_DEV * nb, n_mod), (1, 2, 0, 3)).reshape(nl, N_DEV * nb, N_DEV * n_mod)
    mod_me = lax.dynamic_slice_in_dim(mod_all, me * nb, nb, axis=1)
    mods = [[mod_me[l, :, k * d:(k + 1) * d][:, None, :] for k in range(6)] for l in range(nl)]

    big = ["ffn_w_in", "ffn_w_out", "gm_w_in", "gm_w_out", "fox_w_in", "fox_w_out", "sb_w_in", "sb_w_out", "cv_w_in", "cv_w_out"]
    gathered = dict(zip(big, _gather_weights([weights[n].astype(BF16) for n in big])))
    w_ffn_in = gathered["ffn_w_in"]
    w_ffn_out = gathered["ffn_w_out"].reshape(nl, N_DEV // 2, n_ffn, d)
    w_ffn_out_rows = w_ffn_out.reshape(nl * (N_DEV // 2), n_ffn, d)
    sq = lambda n: gathered[n].reshape(d, d)
    fox_full = _full_cols(gathered["fox_w_in"], fox_w_in.shape[2])
    fox_qkv_w, fox_f_wt = fox_full[:, :3 * d], jnp.transpose(fox_full[:, 3 * d:])
    sb_qkv_w = _full_cols(gathered["sb_w_in"], sb_w_in.shape[2])
    cvp = d // N_DEV
    cv_small = jnp.concatenate([_pad_rows(cv_dw[0], CONV_HALO), cv_dw_b, cv_ln_g, cv_ln_b, cv_b_out,
                                cv_b_in.reshape(2, cvp), jnp.zeros((2, cvp), F32)], axis=0)
    cv_all = _exchange_small(cv_small, "gather_cv_small", False)
    cv_rows = jnp.transpose(cv_all, (1, 0, 2)).reshape(cv_small.shape[0], d)
    cv_dw_f, cv_dwb_f, cv_lng_f, cv_lnb_f, cv_bout_f = (cv_rows[:CONV_HALO], cv_rows[32:33], cv_rows[33:34], cv_rows[34:35], cv_rows[35:36])
    cv_bin_f = cv_all[:, 36:38, :].reshape(1, 2 * d)

    saved = []
    h = _modulate(xs, mods[0][1], mods[0][0], seq)
    xin = xs
    for l in range(nl):
        kind = l % 4
        sv = dict(x=xin, h=h)
        if kind == 0:
            pre = _proj_cols("gm_in", h, gathered["gm_w_in"], gm_w_in.shape[2], bias=gm_b_in)
            yv = _gm_spatial_fwd(pre, gm_ln_g, gm_ln_b, gm_w_s[0], jnp.transpose(gm_b_s[0]), seq)
            y = _mm_plain("gm_out", yv, sq("gm_w_out"), NN)
            sv.update(pre=pre, yv=yv)
        elif kind == 1:
            qkv = _qkv_proj("fox_qkv", h, fox_qkv_w)
            ft = _mm("fox_gate_proj", fox_f_wt, h, _sds((n_heads, t), F32), (t // min(TMM, t), 1),
                     pl.BlockSpec((n_heads, d), lambda i, r: (0, 0)), pl.BlockSpec((min(TMM, t), d), lambda i, r: (i, 0)),
                     pl.BlockSpec((n_heads, min(TMM, t)), lambda i, r: (0, i)), NT)
            b_f = jnp.transpose(fox_b_f)
            frow, fcol = _fox_gate_fwd(ft, b_f, seq)
            frow_p = frow.reshape(npair, 2, t)
            fcol_p = jnp.transpose(fcol.reshape(t, npair, 2), (1, 0, 2))
            o, lse = _fox_fwd(qkv, fcol_p, frow_p, nb, seq)
            y = _mm_plain("fox_out", o, sq("fox_w_out"), NN)
            sv.update(qkv=qkv, ft=ft, b_f=b_f, frow=frow_p, fcol=fcol_p, o=o, lse=lse)
        elif kind == 2:
            qkv = _qkv_proj("sb_qkv", h, sb_qkv_w)
            o, ltot = _sb_fwd(qkv, nb, seq)
            y = _mm_plain("sb_out", o, sq("sb_w_out"), NN)
            sv.update(qkv=qkv, o=o, ltot=ltot)
        else:
            pre = _proj_cols("cv_in", h, gathered["cv_w_in"], cv_w_in.shape[2], bias=cv_bin_f)
            y1, y2 = _cv_conv_fwd(pre, cv_dw_f, cv_dwb_f, cv_lng_f, cv_lnb_f, seq)
            y = _mm_plain("cv_out", y2, sq("cv_w_out"), NN, bias=cv_bout_f)
            sv.update(pre=pre, y1=y1, y2=y2)
        x1, h2 = _lnres_fwd(xin, y, mods[l][2], ln1_g[l:l + 1], ln1_b[l:l + 1], alpha, seq, nxt=(mods[l][4], mods[l][3]))
        hg, hu, act = _ffn_in(h2, w_ffn_in, l)
        y2f = _ffn_out(act, w_ffn_out_rows, l * (N_DEV // 2))
        sv.update(y=y, x1=x1, h2=h2, hg=hg, hu=hu, act=act, y2f=y2f)
        if l + 1 < nl:
            xin, h = _lnres_fwd(x1, y2f, mods[l][5], ln2_g[l:l + 1], ln2_b[l:l + 1], alpha, seq, nxt=(mods[l + 1][1], mods[l + 1][0]))
        else:
            xin = _lnres_fwd(x1, y2f, mods[l][5], ln2_g[l:l + 1], ln2_b[l:l + 1], alpha, seq)
        saved.append(sv)

    dx, sq_err = _loss_head(xin, tgt)
    loss = lax.psum(0.5 * jnp.sum(sq_err) / d, ("x", "y", "c"))

    small = {}
    bigg = {}
    dmods = [None] * nl
    d_ln = dict(ln1_g=[None] * nl, ln1_b=[None] * nl, ln2_g=[None] * nl, ln2_b=[None] * nl)
    ffn_gi, ffn_go = [None] * nl, [None] * nl
    for l in reversed(range(nl)):
        sv = saved[l]
        kind = l % 4
        dxr, dy2, dlg, dlb, _, dgate2 = _lnres_bwd(dx, sv["x1"], sv["y2f"], mods[l][5], ln2_g[l:l + 1], alpha, seq)
        d_ln["ln2_g"][l], d_ln["ln2_b"][l] = dlg, dlb
        dg_, du_ = _ffn_dact(dy2, w_ffn_out_rows, sv["hg"], sv["hu"], l * (N_DEV // 2))
        dwg, dwu, dwo = _ffn_bwd_weights(sv["h2"], dy2, sv["act"], dg_, du_)
        ffn_gi[l] = jnp.concatenate([dwg, dwu], axis=0)
        ffn_go[l] = dwo.reshape(N_DEV, n_ffn // 2, d)
        dh2 = _ffn_dh(dg_, du_, w_ffn_in, l)
        dx1, dsc2, dsh2 = _modulate_bwd(dh2, dxr, sv["x1"], mods[l][4], seq)
        dxr, dy, dlg, dlb, dysum, dgate1 = _lnres_bwd(dx1, sv["x"], sv["y"], mods[l][2], ln1_g[l:l + 1], alpha, seq)
        d_ln["ln1_g"][l], d_ln["ln1_b"][l] = dlg, dlb
        hh = sv["h"]
        if kind == 0:
            dyv = _mm_plain("gm_out_bwd", dy, sq("gm_w_out"), NT)
            bigg["gm_w_out"] = _mm_plain("gm_out_dw", sv["yv"], dy, TN).reshape(1, N_DEV, d // N_DEV, d)
            dpre, dws, dbst, dlng, dlnb, dbin = _gm_spatial_bwd(sv["pre"], dyv, gm_ln_g, gm_ln_b, gm_w_s[0], jnp.transpose(gm_b_s[0]), seq)
            small.update(gm_w_s=dws[None], gm_b_s=jnp.transpose(dbst)[None], gm_ln_g=dlng, gm_ln_b=dlnb, gm_b_in=dbin)
            bigg["gm_w_in"] = _grad_cols("gm_in_dw", hh, dpre, gm_w_in.shape[2])
            dh = _back_cols("gm_in_bwd", dpre, gathered["gm_w_in"], gm_w_in.shape[2])
        elif kind == 1:
            do = _mm_plain("fox_out_bwd", dy, sq("fox_w_out"), NT)
            bigg["fox_w_out"] = _mm_plain("fox_out_dw", sv["o"], dy, TN).reshape(1, N_DEV, d // N_DEV, d)
            dqkv, dfr, dfq = _fox_bwd(sv["qkv"], sv["fcol"], sv["frow"], sv["o"], do, sv["lse"], nb, seq)
            dft, dbf = _fox_gate_bwd(sv["ft"], sv["b_f"], dfr.reshape(n_heads, t),
                                     jnp.transpose(dfq, (0, 2, 1)).reshape(n_heads, t), seq)
            small["fox_b_f"] = jnp.transpose(dbf)
            dw_qkv = _qkv_dw("fox_qkv_dw", hh, dqkv)
            tk = min(TMM, t)
            dw_ft = _mm("fox_gate_dw", dft, hh, _sds((n_heads, d), F32), (1, t // tk),
                        pl.BlockSpec((n_heads, tk), lambda j, r: (0, r)), pl.BlockSpec((tk, d), lambda j, r: (r, 0)),
                        pl.BlockSpec((n_heads, d), lambda j, r: (0, 0)), NN)
            bigg["fox_w_in"] = _col_blocks(jnp.concatenate([dw_qkv, jnp.transpose(dw_ft)], axis=1), fox_w_in.shape[2])
            dh_a = _qkv_dh("fox_qkv_bwd", dqkv, fox_qkv_w)
            tm = min(TMM, t)
            dh = _mm("fox_gate_bwd_h", dft, fox_f_wt, _sds((t, d), F32), (t // tm, 1),
                     pl.BlockSpec((n_heads, tm), lambda i, r: (0, i)), pl.BlockSpec((n_heads, d), lambda i, r: (0, 0)),
                     pl.BlockSpec((tm, d), lambda i, r: (i, 0)), TN, (dh_a,), (pl.BlockSpec((tm, d), lambda i, r: (i, 0)),), _add)
        elif kind == 2:
            do = _mm_plain("sb_out_bwd", dy, sq("sb_w_out"), NT)
            bigg["sb_w_out"] = _mm_plain("sb_out_dw", sv["o"], dy, TN).reshape(1, N_DEV, d // N_DEV, d)
            dqkv = _sb_bwd(sv["qkv"], do, sv["ltot"], nb, seq)
            bigg["sb_w_in"] = _col_blocks(_qkv_dw("sb_qkv_dw", hh, dqkv), sb_w_in.shape[2])
            dh = _qkv_dh("sb_qkv_bwd", dqkv, sb_qkv_w)
        else:
            dy2c = _mm_plain("cv_out_bwd", dy, sq("cv_w_out"), NT)
            bigg["cv_w_out"] = _mm_plain("cv_out_dw", sv["y2"], dy, TN).reshape(1, N_DEV, d // N_DEV, d)
            dy1, dlng, dlnb, ddwb = _cv_norm_bwd(sv["y1"], dy2c, cv_lng_f, cv_lnb_f)
            dpre, ddw, dbin = _cv_conv_bwd(sv["pre"], dy1, cv_dw_f, seq)
            small.update(cv_b_out=dysum, cv_ln_g=dlng, cv_ln_b=dlnb, cv_dw_b=ddwb, cv_dw=ddw[:CONV_WIDTH], cv_b_in=dbin)
            bigg["cv_w_in"] = _grad_cols("cv_in_dw", hh, dpre, cv_w_in.shape[2])
            dh = _back_cols("cv_in_bwd", dpre, gathered["cv_w_in"], cv_w_in.shape[2])
        dx, dsc1, dsh1 = _modulate_bwd(dh, dxr, sv["x"], mods[l][1], seq)
        dmods[l] = jnp.concatenate([dsh1, dsc1, dgate1, dsh2, dsc2, dgate2], axis=2)[:, 0, :]
    grad_x = dx.reshape(nb, seq, d)
    bigg["ffn_w_in"] = jnp.stack(ffn_gi)
    bigg["ffn_w_out"] = jnp.stack(ffn_go)
    for n in d_ln:
        small[n] = jnp.concatenate(d_ln[n], axis=0)

    dmod_rows = jnp.stack(dmods).reshape(nl * nb, 6 * d)
    dmod_g = _exchange_small(_pad_rows(dmod_rows, 8 * ((nl * nb + 7) // 8)), "gather_dmod", False)[:, :nl * nb]
    dmod_all = jnp.transpose(dmod_g.reshape(N_DEV, nl, nb, 6 * d), (1, 0, 2, 3)).reshape(nl, N_DEV * nb, 6 * d)
    dmod_loc = lax.dynamic_slice_in_dim(dmod_all, me * n_mod, n_mod, axis=2)
    g_mod_w, g_mod_b = _mod_bwd(c_all, dmod_loc, dmod_all)
    grads = dict(mod_w=g_mod_w, mod_b=g_mod_b[:, 0, :])

    rep = ["ln1_g", "ln1_b", "ln2_g", "ln2_b", "gm_b_in", "gm_ln_g", "gm_ln_b", "gm_w_s", "gm_b_s", "fox_b_f"]
    cvs = ["cv_b_in", "cv_dw", "cv_dw_b", "cv_ln_g", "cv_ln_b", "cv_b_out"]

    def rows_of(a):
        flat = a.reshape(-1)
        pad = (-flat.shape[0]) % d
        return jnp.pad(flat, (0, pad)).reshape(-1, d)

    pack_rows = [rows_of(small[n]) for n in rep + cvs]
    counts = [r.shape[0] for r in pack_rows]
    total = sum(counts)
    pack = _pad_rows(jnp.concatenate(pack_rows, axis=0), 8 * ((total + 7) // 8))
    summed = _exchange_small(pack, "allreduce_small", True)
    offs = [sum(counts[:i]) for i in range(len(counts))]
    rep_rows = sum(counts[:len(rep)])
    for n, o_, cnt in zip(rep + cvs, offs, counts):
        full = summed[o_:o_ + cnt].reshape(-1)
        if n in rep:
            grads[n] = full[:weights[n].size].reshape(weights[n].shape)
        else:
            wshape = weights[n].shape
            cols = wshape[-1]
            full = full[:math.prod(wshape[:-1]) * cols * N_DEV].reshape(wshape[:-1] + (cols * N_DEV,))
            grads[n] = lax.dynamic_slice_in_dim(full, me * cols, cols, axis=full.ndim - 1)

    recv = dict(zip(big, _scatter_grads([bigg[n] for n in big])))

    outs = {}

    def view2(a):
        return a.reshape(-1, a.shape[-1])

    for n in big:
        w2 = view2(weights[n])
        res = _adamw("adamw_" + n, w2, view2(mom1[n]), view2(mom2[n]), parts=recv[n].reshape((N_DEV,) + w2.shape))
        outs[n] = [r.reshape(weights[n].shape) for r in res]
    res = _adamw("adamw_mod_w", view2(mod_w), view2(m_mod_w), view2(v_mod_w), g=view2(grads["mod_w"]))
    outs["mod_w"] = [r.reshape(mod_w.shape) for r in res]
    rp = lambda src: _pad_rows(jnp.concatenate([rows_of(src[n]) for n in rep], axis=0), 8 * ((rep_rows + 7) // 8))
    res = _adamw("adamw_replicated", rp(weights), rp(mom1), rp(mom2), g=rp(grads))
    for n, o_, cnt in zip(rep, offs, counts):
        outs[n] = [r[o_:o_ + cnt].reshape(-1)[:weights[n].size].reshape(weights[n].shape) for r in res]
    cv_cols = weights["cv_b_out"].shape[-1]
    cp = lambda src: jnp.concatenate([src[n].reshape(-1, cv_cols) for n in cvs], axis=0)
    cv_cnt = [weights[n].size // cv_cols for n in cvs]
    cv_tot = sum(cv_cnt)
    cpp = lambda src: _pad_rows(cp(src), 8 * ((cv_tot + 7) // 8))
    res = _adamw("adamw_cv_small", cpp(weights), cpp(mom1), cpp(mom2), g=cpp(grads))
    o_ = 0
    for n, cnt in zip(cvs, cv_cnt):
        outs[n] = [r[o_:o_ + cnt].reshape(weights[n].shape) for r in res]
        o_ += cnt
    res = _adamw("adamw_mod_b", mod_b, m_mod_b, v_mod_b, g=grads["mod_b"])
    outs["mod_b"] = list(res)

    return (loss, grad_x, *[outs[n][0] for n in names], *[outs[n][1] for n in names],
            *[outs[n][2] for n in names], *[outs[n][3] for n in names])
```

```python
import functools
import math

import jax
import jax.numpy as jnp
from jax import lax
from jax.experimental import pallas as pl
from jax.experimental.pallas import tpu as pltpu

F32 = jnp.float32
BF16 = jnp.bfloat16
MESH = pl.DeviceIdType.MESH

N_DEV = 8
HEAD_DIM = 64
LANES = 128
GM_CHUNK = 128
GM_GROUPS = 8
CONV_WIDTH = 31
CONV_HALO = 32
LN_EPS = 1e-5
NEG_INF = -1e30
SB_DEAD = -100.0

ADAM_LR = 0.001
ADAM_B1 = 0.9
ADAM_B2 = 0.999
ADAM_EPS = 1e-08
ADAM_WD = 0.01
ADAM_STEP = 10

TM = 512
TMM = 1024
TQ = 256
TT = 512
VMEM_BIG = 56 * 1024 * 1024

NN = (((1,), (0,)), ((), ()))
NT = (((1,), (1,)), ((), ()))
TN = (((0,), (0,)), ((), ()))


def _call(body, name, out_shape, grid=None, in_specs=None, out_specs=None, scratch=(), vmem=None):
    params = {}
    if grid is not None:
        params["dimension_semantics"] = ("arbitrary",) * len(grid)
    if vmem is not None:
        params["vmem_limit_bytes"] = vmem
    kw = {}
    if grid is not None:
        kw["grid"] = grid
    if in_specs is not None:
        kw["in_specs"] = in_specs
    if out_specs is not None:
        kw["out_specs"] = out_specs
    return pl.pallas_call(body, name=name, out_shape=out_shape, scratch_shapes=list(scratch),
                          compiler_params=pltpu.CompilerParams(**params), **kw)


def _sds(shape, dtype):
    return jax.ShapeDtypeStruct(tuple(shape), dtype)


def _dot(a, b, dims=NN):
    return lax.dot_general(a.astype(BF16), b.astype(BF16), dims, preferred_element_type=F32)


def _split3(x):
    h1 = x.astype(BF16)
    r1 = x - h1.astype(F32)
    h2 = r1.astype(BF16)
    h3 = (r1 - h2.astype(F32)).astype(BF16)
    return h1, h2, h3


def _dot_exact(x, m, dims=NN):
    h1, h2, h3 = _split3(x)
    d = lambda h: lax.dot_general(h, m, dims, preferred_element_type=F32)
    return (d(h1) + d(h2)) + d(h3)


def _dot_exact_rhs(m, x, dims=NN):
    h1, h2, h3 = _split3(x)
    d = lambda h: lax.dot_general(m, h, dims, preferred_element_type=F32)
    return (d(h1) + d(h2)) + d(h3)


def _silu(x):
    return x * jax.nn.sigmoid(x)


def _gelu(x):
    return 0.5 * x * (1.0 + lax.erf(x * (2.0 ** -0.5)))


def _gelu_grad(x):
    return 0.5 * (1.0 + lax.erf(x * (2.0 ** -0.5))) + x * jnp.exp(-0.5 * x * x) * ((2.0 * math.pi) ** -0.5)


def _log_sigmoid(z):
    return jnp.minimum(z, 0.0) - jnp.log(1.0 + jnp.exp(-jnp.abs(z)))


def _ln_stats(r):
    mu = jnp.mean(r, axis=-1, keepdims=True)
    rc = r - mu
    var = jnp.mean(rc * rc, axis=-1, keepdims=True)
    return rc, lax.rsqrt(var + LN_EPS)


def _colsum(x):
    return jnp.sum(x, axis=0, keepdims=True)


def _peers():
    mx, my, mc = lax.axis_index("x"), lax.axis_index("y"), lax.axis_index("c")
    me = 4 * mx + 2 * my + mc
    out = []
    for k in range(1, N_DEV):
        px = 1 - mx if (k >> 2) & 1 else mx
        py = 1 - my if (k >> 1) & 1 else my
        pc = 1 - mc if k & 1 else mc
        out.append(((px, py, pc), 4 * px + 2 * py + pc))
    return me, out


def _exchange_small(x, name, reduce):
    rows, cols = x.shape

    def body(x_ref, o_ref, *rest):
        if reduce:
            land, send_sems, recv_sems, local_sem = rest
        else:
            land = o_ref
            send_sems, recv_sems, local_sem = rest
        me, peers = _peers()
        mine = pltpu.make_async_copy(x_ref, land.at[me], local_sem)
        mine.start()
        sends = []
        for k, (peer, _) in enumerate(peers):
            cp = pltpu.make_async_remote_copy(src_ref=x_ref, dst_ref=land.at[me], send_sem=send_sems.at[k],
                                              recv_sem=recv_sems.at[k], device_id=peer, device_id_type=MESH)
            cp.start()
            sends.append(cp)
        for k, (peer, blk) in enumerate(peers):
            pltpu.make_async_remote_copy(src_ref=x_ref, dst_ref=land.at[blk], send_sem=send_sems.at[k],
                                         recv_sem=recv_sems.at[k], device_id=peer, device_id_type=MESH).wait_recv()
        for cp in sends:
            cp.wait_send()
        mine.wait()
        if reduce:
            acc = land[0]
            for s in range(1, N_DEV):
                acc = acc + land[s]
            o_ref[...] = acc

    vm = pl.BlockSpec(memory_space=pltpu.VMEM)
    scratch = [pltpu.SemaphoreType.DMA((N_DEV - 1,)), pltpu.SemaphoreType.DMA((N_DEV - 1,)), pltpu.SemaphoreType.DMA]
    if reduce:
        scratch = [pltpu.VMEM((N_DEV, rows, cols), F32)] + scratch
        out = _sds((rows, cols), F32)
    else:
        out = _sds((N_DEV, rows, cols), F32)
    return _call(body, name, out, in_specs=[vm], out_specs=vm, scratch=scratch, vmem=VMEM_BIG)(x)


def _gather_weights(shards):
    n = len(shards)

    def body(*refs):
        ins, outs = refs[:n], refs[n:2 * n]
        send_sems, recv_sems, local_sems = refs[2 * n:]
        me, peers = _peers()
        local = []
        sends = []
        for a in range(n):
            cp = pltpu.make_async_copy(ins[a], outs[a].at[:, me], local_sems.at[a])
            cp.start()
            local.append(cp)
            for k, (peer, _) in enumerate(peers):
                cp = pltpu.make_async_remote_copy(src_ref=ins[a], dst_ref=outs[a].at[:, me], send_sem=send_sems.at[a, k],
                                                  recv_sem=recv_sems.at[a, k], device_id=peer, device_id_type=MESH)
                cp.start()
                sends.append(cp)
        for a in range(n):
            for k, (peer, blk) in enumerate(peers):
                pltpu.make_async_remote_copy(src_ref=ins[a], dst_ref=outs[a].at[:, blk], send_sem=send_sems.at[a, k],
                                             recv_sem=recv_sems.at[a, k], device_id=peer, device_id_type=MESH).wait_recv()
        for cp in sends:
            cp.wait_send()
        for cp in local:
            cp.wait()

    hbm = pl.BlockSpec(memory_space=pl.ANY)
    out = [_sds((s.shape[0], N_DEV) + s.shape[1:], s.dtype) for s in shards]
    scratch = [pltpu.SemaphoreType.DMA((n, N_DEV - 1)), pltpu.SemaphoreType.DMA((n, N_DEV - 1)), pltpu.SemaphoreType.DMA((n,))]
    return _call(body, "gather_weights", out, in_specs=[hbm] * n, out_specs=[hbm] * n, scratch=scratch)(*shards)


def _scatter_grads(grads):
    n = len(grads)

    def body(*refs):
        ins, outs = refs[:n], refs[n:2 * n]
        send_sems, recv_sems, local_sems = refs[2 * n:]
        me, peers = _peers()
        local = []
        sends = []
        for a in range(n):
            cp = pltpu.make_async_copy(ins[a].at[:, me], outs[a].at[me], local_sems.at[a])
            cp.start()
            local.append(cp)
            for k, (peer, blk) in enumerate(peers):
                cp = pltpu.make_async_remote_copy(src_ref=ins[a].at[:, blk], dst_ref=outs[a].at[me], send_sem=send_sems.at[a, k],
                                                  recv_sem=recv_sems.at[a, k], device_id=peer, device_id_type=MESH)
                cp.start()
                sends.append(cp)
        for a in range(n):
            for k, (peer, blk) in enumerate(peers):
                pltpu.make_async_remote_copy(src_ref=ins[a].at[:, me], dst_ref=outs[a].at[blk], send_sem=send_sems.at[a, k],
                                             recv_sem=recv_sems.at[a, k], device_id=peer, device_id_type=MESH).wait_recv()
        for cp in sends:
            cp.wait_send()
        for cp in local:
            cp.wait()

    hbm = pl.BlockSpec(memory_space=pl.ANY)
    out = [_sds((N_DEV, g.shape[0]) + g.shape[2:], g.dtype) for g in grads]
    scratch = [pltpu.SemaphoreType.DMA((n, N_DEV - 1)), pltpu.SemaphoreType.DMA((n, N_DEV - 1)), pltpu.SemaphoreType.DMA((n,))]
    return _call(body, "scatter_grads", out, in_specs=[hbm] * n, out_specs=[hbm] * n, scratch=scratch)(*grads)


def _mm(name, a, b, out, grid, a_spec, b_spec, o_spec, dims, extra=(), extra_specs=(), epilogue=None, vmem=VMEM_BIG):
    nred = grid[-1]
    red_axis = len(grid) - 1
    acc_shape = tuple(d for d in o_spec.block_shape if d is not None)
    n_extra = len(extra)

    def body(a_ref, b_ref, *rest):
        ex = rest[:n_extra]
        o_ref = rest[n_extra]

        def finish(acc):
            if epilogue is not None:
                acc = epilogue(acc, *[e[...] for e in ex])
            o_ref[...] = acc.astype(o_ref.dtype)

        prod = _dot(a_ref[...], b_ref[...], dims)
        if nred == 1:
            finish(prod)
        else:
            acc_ref = rest[n_extra + 1]
            r = pl.program_id(red_axis)

            @pl.when(r == 0)
            def _():
                acc_ref[...] = prod

            @pl.when(r > 0)
            def _():
                acc_ref[...] += prod

            @pl.when(r == nred - 1)
            def _():
                finish(acc_ref[...])

    scratch = [pltpu.VMEM(acc_shape, F32)] if nred > 1 else []
    return _call(body, name, out, grid=grid, in_specs=[a_spec, b_spec, *extra_specs], out_specs=o_spec,
                 scratch=scratch, vmem=vmem)(a, b, *extra)


def _add(acc, x):
    return acc + x


def _proj_cols(name, h, w, n_slot, bias=None, out_dtype=F32):
    t, k = h.shape
    s = w.shape[1]
    tm = min(TMM, t)
    extra, especs, epi = (), (), None
    if bias is not None:
        extra, especs, epi = (bias,), (pl.BlockSpec((1, n_slot), lambda j, i, r: (0, j)),), _add
    return _mm(name, h, w, _sds((t, s * n_slot), out_dtype), (s, t // tm, 1),
               pl.BlockSpec((tm, k), lambda j, i, r: (i, 0)),
               pl.BlockSpec((None, None, k, n_slot), lambda j, i, r: (0, j, 0, 0)),
               pl.BlockSpec((tm, n_slot), lambda j, i, r: (i, j)), NN, extra, especs, epi)


def _grad_cols(name, h, g, n_slot):
    t, k = h.shape
    s = g.shape[1] // n_slot
    tk = min(TMM, t)
    return _mm(name, h, g, _sds((1, s, k, n_slot), F32), (s, t // tk),
               pl.BlockSpec((tk, k), lambda j, r: (r, 0)),
               pl.BlockSpec((tk, n_slot), lambda j, r: (r, j)),
               pl.BlockSpec((None, None, k, n_slot), lambda j, r: (0, j, 0, 0)), TN)


def _back_cols(name, g, w, n_slot, extra=(), extra_specs=(), epilogue=None):
    t = g.shape[0]
    s, k = w.shape[1], w.shape[2]
    tm = min(TMM, t)
    return _mm(name, g, w, _sds((t, k), F32), (t // tm, s),
               pl.BlockSpec((tm, n_slot), lambda i, r: (i, r)),
               pl.BlockSpec((None, None, k, n_slot), lambda i, r: (0, r, 0, 0)),
               pl.BlockSpec((tm, k), lambda i, r: (i, 0)), NT, extra, extra_specs, epilogue)


def _mm_plain(name, a, b, dims, out_dtype=F32, bias=None):
    if dims == TN:
        t, k = a.shape
        n = b.shape[1]
        tk = min(TMM, t)
        return _mm(name, a, b, _sds((k, n), out_dtype), (1, t // tk),
                   pl.BlockSpec((tk, k), lambda j, r: (r, 0)), pl.BlockSpec((tk, n), lambda j, r: (r, 0)),
                   pl.BlockSpec((k, n), lambda j, r: (0, 0)), TN)
    t = a.shape[0]
    tm = min(TMM, t)
    n = b.shape[1] if dims == NN else b.shape[0]
    extra, especs, epi = (), (), None
    if bias is not None:
        extra, especs, epi = (bias,), (pl.BlockSpec((1, n), lambda i, r: (0, 0)),), _add
    return _mm(name, a, b, _sds((t, n), out_dtype), (t // tm, 1),
               pl.BlockSpec((tm, a.shape[1]), lambda i, r: (i, 0)), pl.BlockSpec(b.shape, lambda i, r: (0, 0)),
               pl.BlockSpec((tm, n), lambda i, r: (i, 0)), dims, extra, especs, epi)


def _mod_fwd(c_all, mod_w, mod_b_loc):
    nl, d, n = mod_w.shape
    nb = c_all.shape[0]

    def body(c_ref, w_ref, b_ref, o_ref):
        o_ref[...] = _dot(_silu(c_ref[...]), w_ref[...]) + b_ref[...]

    return _call(body, "mod_fwd", _sds((nl, nb, n), F32), grid=(nl,),
                 in_specs=[pl.BlockSpec((nb, d), lambda l: (0, 0)), pl.BlockSpec((None, d, n), lambda l: (l, 0, 0)),
                           pl.BlockSpec((None, 1, n), lambda l: (l, 0, 0))],
                 out_specs=pl.BlockSpec((None, nb, n), lambda l: (l, 0, 0)))(c_all, mod_w, mod_b_loc)


def _mod_bwd(c_all, dmod_loc, dmod_all):
    nl, nb, n = dmod_loc.shape
    d = c_all.shape[1]
    n_all = dmod_all.shape[2]

    def body(c_ref, dl_ref, da_ref, gw_ref, gb_ref):
        gw_ref[...] = _dot(_silu(c_ref[...]), dl_ref[...], TN)
        gb_ref[...] = _colsum(da_ref[...])

    return _call(body, "mod_bwd", (_sds((nl, d, n), F32), _sds((nl, 1, n_all), F32)), grid=(nl,),
                 in_specs=[pl.BlockSpec((nb, d), lambda l: (0, 0)), pl.BlockSpec((None, nb, n), lambda l: (l, 0, 0)),
                           pl.BlockSpec((None, nb, n_all), lambda l: (l, 0, 0))],
                 out_specs=(pl.BlockSpec((None, d, n), lambda l: (l, 0, 0)), pl.BlockSpec((None, 1, n_all), lambda l: (l, 0, 0))),
                 )(c_all, dmod_loc, dmod_all)


def _row_spec(d, tpb):
    return pl.BlockSpec((None, 1, d), lambda i: (i // tpb, 0, 0))


def _tile_spec(tm, d):
    return pl.BlockSpec((tm, d), lambda i: (i, 0))


def _vec_spec(d):
    return pl.BlockSpec((1, d), lambda i: (0, 0))


def _modulate(x, sc, sh, seq):
    t, d = x.shape
    tm = min(TM, seq)
    tpb = seq // tm

    def body(x_ref, sc_ref, sh_ref, h_ref):
        h_ref[...] = (x_ref[...] * (1.0 + sc_ref[...]) + sh_ref[...]).astype(BF16)

    return _call(body, "modulate", _sds((t, d), BF16), grid=(t // tm,),
                 in_specs=[_tile_spec(tm, d), _row_spec(d, tpb), _row_spec(d, tpb)], out_specs=_tile_spec(tm, d))(x, sc, sh)


def _lnres_fwd(x, y, gate, lg, lb, alpha, seq, nxt=None):
    t, d = x.shape
    tm = min(TM, seq)
    tpb = seq // tm

    def body(x_ref, y_ref, g_ref, lg_ref, lb_ref, *rest):
        r = alpha * x_ref[...] + (1.0 + g_ref[...]) * y_ref[...]
        rc, rstd = _ln_stats(r)
        xn = rc * rstd * lg_ref[...] + lb_ref[...]
        if nxt is None:
            rest[0][...] = xn
        else:
            sc_ref, sh_ref, xo_ref, h_ref = rest
            xo_ref[...] = xn
            h_ref[...] = (xn * (1.0 + sc_ref[...]) + sh_ref[...]).astype(BF16)

    ins = [_tile_spec(tm, d), _tile_spec(tm, d), _row_spec(d, tpb), _vec_spec(d), _vec_spec(d)]
    if nxt is None:
        return _call(body, "lnres_fwd_last", _sds((t, d), F32), grid=(t // tm,), in_specs=ins,
                     out_specs=_tile_spec(tm, d))(x, y, gate, lg, lb)
    return _call(body, "lnres_fwd", (_sds((t, d), F32), _sds((t, d), BF16)), grid=(t // tm,),
                 in_specs=ins + [_row_spec(d, tpb), _row_spec(d, tpb)],
                 out_specs=(_tile_spec(tm, d), _tile_spec(tm, d)))(x, y, gate, lg, lb, *nxt)


def _loss_head(x, tgt):
    t, d = x.shape
    tm = min(TM, t)

    def body(x_ref, t_ref, dx_ref, sq_ref):
        e = x_ref[...] - t_ref[...]
        dx_ref[...] = e * (1.0 / d)

        @pl.when(pl.program_id(0) == 0)
        def _():
            sq_ref[...] = jnp.zeros_like(sq_ref)

        sq_ref[...] += _colsum(e * e)

    return _call(body, "loss_head", (_sds((t, d), F32), _sds((1, d), F32)), grid=(t // tm,),
                 in_specs=[_tile_spec(tm, d), _tile_spec(tm, d)], out_specs=(_tile_spec(tm, d), _vec_spec(d)))(x, tgt)


def _lnres_bwd(dxo, x, y, gate, lg, alpha, seq):
    t, d = x.shape
    tm = min(TM, seq)
    tpb = seq // tm
    nb = t // seq

    def body(dxo_ref, x_ref, y_ref, g_ref, lg_ref, dxr_ref, dy_ref, dlg_ref, dlb_ref, dys_ref, dg_ref):
        i = pl.program_id(0)
        yv = y_ref[...]
        r = alpha * x_ref[...] + (1.0 + g_ref[...]) * yv
        rc, rstd = _ln_stats(r)
        xhat = rc * rstd
        dxo_v = dxo_ref[...]
        dxh = dxo_v * lg_ref[...]
        m1 = jnp.mean(dxh, axis=-1, keepdims=True)
        m2 = jnp.mean(dxh * xhat, axis=-1, keepdims=True)
        dr = rstd * (dxh - m1 - xhat * m2)
        dyv = (1.0 + g_ref[...]) * dr
        dxr_ref[...] = alpha * dr
        dy_ref[...] = dyv.astype(BF16)

        @pl.when(i == 0)
        def _():
            dlg_ref[...] = jnp.zeros_like(dlg_ref)
            dlb_ref[...] = jnp.zeros_like(dlb_ref)
            dys_ref[...] = jnp.zeros_like(dys_ref)

        @pl.when(i % tpb == 0)
        def _():
            dg_ref[...] = jnp.zeros_like(dg_ref)

        dlg_ref[...] += _colsum(dxo_v * xhat)
        dlb_ref[...] += _colsum(dxo_v)
        dys_ref[...] += _colsum(dyv)
        dg_ref[...] += _colsum(dr * yv)

    return _call(body, "lnres_bwd",
                 (_sds((t, d), F32), _sds((t, d), BF16), _sds((1, d), F32), _sds((1, d), F32), _sds((1, d), F32), _sds((nb, 1, d), F32)),
                 grid=(t // tm,),
                 in_specs=[_tile_spec(tm, d), _tile_spec(tm, d), _tile_spec(tm, d), _row_spec(d, tpb), _vec_spec(d)],
                 out_specs=(_tile_spec(tm, d), _tile_spec(tm, d), _vec_spec(d), _vec_spec(d), _vec_spec(d), _row_spec(d, tpb)),
                 )(dxo, x, y, gate, lg)


def _modulate_bwd(dh, dxr, x, sc, seq):
    t, d = x.shape
    tm = min(TM, seq)
    tpb = seq // tm
    nb = t // seq

    def body(dh_ref, dxr_ref, x_ref, sc_ref, dx_ref, dsc_ref, dsh_ref):
        dhv = dh_ref[...]
        dx_ref[...] = dxr_ref[...] + dhv * (1.0 + sc_ref[...])

        @pl.when(pl.program_id(0) % tpb == 0)
        def _():
            dsc_ref[...] = jnp.zeros_like(dsc_ref)
            dsh_ref[...] = jnp.zeros_like(dsh_ref)

        dsc_ref[...] += _colsum(dhv * x_ref[...])
        dsh_ref[...] += _colsum(dhv)

    return _call(body, "modulate_bwd", (_sds((t, d), F32), _sds((nb, 1, d), F32), _sds((nb, 1, d), F32)), grid=(t // tm,),
                 in_specs=[_tile_spec(tm, d), _tile_spec(tm, d), _tile_spec(tm, d), _row_spec(d, tpb)],
                 out_specs=(_tile_spec(tm, d), _row_spec(d, tpb), _row_spec(d, tpb)))(dh, dxr, x, sc)


def _ffn_in(h, w_in, layer):
    t, d = h.shape
    n = w_in.shape[3]
    half = N_DEV // 2
    tm = min(TM, t)

    def body(h_ref, wg_ref, wu_ref, g_ref, u_ref, a_ref):
        hv = h_ref[...]
        g = _dot(hv, wg_ref[...])
        u = _dot(hv, wu_ref[...])
        g_ref[...] = g
        u_ref[...] = u
        a_ref[...] = (_silu(g) * u).astype(BF16)

    blk = pl.BlockSpec((None, tm, n), lambda p, i: (p, i, 0))
    return _call(body, "ffn_in", (_sds((half, t, n), F32), _sds((half, t, n), F32), _sds((half, t, n), BF16)),
                 grid=(half, t // tm),
                 in_specs=[pl.BlockSpec((tm, d), lambda p, i: (i, 0)),
                           pl.BlockSpec((None, None, d, n), lambda p, i: (layer, p, 0, 0)),
                           pl.BlockSpec((None, None, d, n), lambda p, i: (layer, p + half, 0, 0))],
                 out_specs=(blk, blk, blk), vmem=VMEM_BIG)(h, w_in, w_in)


def _ffn_out(act, w_out, layer):
    half, t, n = act.shape
    d = w_out.shape[2]
    tm = min(TMM, t)
    return _mm("ffn_out", act, w_out, _sds((t, d), F32), (t // tm, half),
               pl.BlockSpec((None, tm, n), lambda i, r: (r, i, 0)),
               pl.BlockSpec((None, n, d), lambda i, r: (layer + r, 0, 0)),
               pl.BlockSpec((tm, d), lambda i, r: (i, 0)), NN)


def _ffn_dact(dy, w_out, hg, hu, layer):
    half, t, n = hg.shape
    d = dy.shape[1]
    tm = min(TM, t)

    def body(dy_ref, w_ref, g_ref, u_ref, dg_ref, du_ref):
        da = _dot(dy_ref[...], w_ref[...], NT)
        g = g_ref[...]
        sg = jax.nn.sigmoid(g)
        dg_ref[...] = (da * u_ref[...] * (sg * (1.0 + g * (1.0 - sg)))).astype(BF16)
        du_ref[...] = (da * (g * sg)).astype(BF16)

    blk = pl.BlockSpec((None, tm, n), lambda p, i: (p, i, 0))
    dg, du = _call(body, "ffn_dact", (_sds((half, t, n), BF16), _sds((half, t, n), BF16)), grid=(half, t // tm),
                   in_specs=[pl.BlockSpec((tm, d), lambda p, i: (i, 0)),
                             pl.BlockSpec((None, n, d), lambda p, i: (layer + p, 0, 0)), blk, blk],
                   out_specs=(blk, blk), vmem=VMEM_BIG)(dy, w_out, hg, hu)
    return dg, du


def _ffn_bwd_weights(h, dy, act, dg, du):
    half, t, n = act.shape
    d = h.shape[1]
    tk = min(TMM, t)
    a_spec = pl.BlockSpec((tk, d), lambda j, r: (r, 0))
    g_spec = pl.BlockSpec((None, tk, n), lambda j, r: (j, r, 0))
    o_spec = pl.BlockSpec((None, d, n), lambda j, r: (j, 0, 0))
    dwg = _mm("ffn_dw_gate", h, dg, _sds((half, d, n), F32), (half, t // tk), a_spec, g_spec, o_spec, TN)
    dwu = _mm("ffn_dw_up", h, du, _sds((half, d, n), F32), (half, t // tk), a_spec, g_spec, o_spec, TN)
    dwo = _mm("ffn_dw_out", act, dy, _sds((half, n, d), F32), (half, t // tk),
              pl.BlockSpec((None, tk, n), lambda j, r: (j, r, 0)), pl.BlockSpec((tk, d), lambda j, r: (r, 0)),
              pl.BlockSpec((None, n, d), lambda j, r: (j, 0, 0)), TN)
    return dwg, dwu, dwo


def _ffn_dh(dg, du, w_in, layer):
    half, t, n = dg.shape
    d = w_in.shape[2]
    tm = min(TMM, t)
    g_spec = pl.BlockSpec((None, tm, n), lambda i, r: (r, i, 0))
    o_spec = pl.BlockSpec((tm, d), lambda i, r: (i, 0))
    part = _mm("ffn_dh_gate", dg, w_in, _sds((t, d), F32), (t // tm, half), g_spec,
               pl.BlockSpec((None, None, d, n), lambda i, r: (layer, r, 0, 0)), o_spec, NT)
    return _mm("ffn_dh_up", du, w_in, _sds((t, d), F32), (t // tm, half), g_spec,
               pl.BlockSpec((None, None, d, n), lambda i, r: (layer, r + half, 0, 0)), o_spec, NT,
               (part,), (o_spec,), _add)


def _tril(n, strict=False):
    r = lax.broadcasted_iota(jnp.int32, (n, n), 0)
    c = lax.broadcasted_iota(jnp.int32, (n, n), 1)
    return c < r if strict else c <= r


def _gm_spatial_fwd(pre, lng, lnb, w_s, b_st, seq):
    t, w2 = pre.shape
    w = w2 // 2
    gd = w // GM_GROUPS
    tm = min(TM, seq)
    nch = tm // GM_CHUNK

    def body(pre_ref, lng_ref, lnb_ref, ws_ref, bs_ref, y_ref):
        v = _gelu(pre_ref[:, w:])
        vc, rstd = _ln_stats(v)
        vn = (vc * rstd * lng_ref[...] + lnb_ref[...]).astype(BF16)
        keep = _tril(GM_CHUNK)
        for g in range(GM_GROUPS):
            wm = jnp.where(keep, ws_ref[g], 0.0).astype(BF16)
            for ci in range(nch):
                rows = slice(ci * GM_CHUNK, (ci + 1) * GM_CHUNK)
                cols = slice(g * gd, (g + 1) * gd)
                sv = _dot(wm, vn[rows, cols]) + bs_ref[:, g:g + 1]
                u = _gelu(pre_ref[rows, cols])
                y_ref[rows, cols] = (u * sv).astype(BF16)

    return _call(body, "gm_spatial_fwd", _sds((t, w), BF16), grid=(t // tm,),
                 in_specs=[_tile_spec(tm, w2), _vec_spec(w), _vec_spec(w),
                           pl.BlockSpec((GM_GROUPS, GM_CHUNK, GM_CHUNK), lambda i: (0, 0, 0)),
                           pl.BlockSpec((GM_CHUNK, GM_GROUPS), lambda i: (0, 0))],
                 out_specs=_tile_spec(tm, w), vmem=VMEM_BIG)(pre, lng, lnb, w_s, b_st)


def _gm_spatial_bwd(pre, dyv, lng, lnb, w_s, b_st, seq):
    t, w2 = pre.shape
    w = w2 // 2
    gd = w // GM_GROUPS
    tm = min(TM, seq)
    nch = tm // GM_CHUNK

    def body(pre_ref, dyv_ref, lng_ref, lnb_ref, ws_ref, bs_ref, dpre_ref, dws_ref, dbs_ref, dlg_ref, dlb_ref, dbin_ref, dvn_ref):
        @pl.when(pl.program_id(0) == 0)
        def _():
            dws_ref[...] = jnp.zeros_like(dws_ref)
            dbs_ref[...] = jnp.zeros_like(dbs_ref)
            dlg_ref[...] = jnp.zeros_like(dlg_ref)
            dlb_ref[...] = jnp.zeros_like(dlb_ref)
            dbin_ref[...] = jnp.zeros_like(dbin_ref)

        pv = pre_ref[:, w:]
        v = _gelu(pv)
        vc, rstd = _ln_stats(v)
        vhat = vc * rstd
        vn = (vhat * lng_ref[...] + lnb_ref[...]).astype(BF16)
        keep = _tril(GM_CHUNK)
        dbs_cols = []
        for g in range(GM_GROUPS):
            wm = jnp.where(keep, ws_ref[g], 0.0).astype(BF16)
            dwm = jnp.zeros((GM_CHUNK, GM_CHUNK), F32)
            dbs = jnp.zeros((GM_CHUNK, 1), F32)
            for ci in range(nch):
                rows = slice(ci * GM_CHUNK, (ci + 1) * GM_CHUNK)
                cols = slice(g * gd, (g + 1) * gd)
                vn_b = vn[rows, cols]
                sv = _dot(wm, vn_b) + bs_ref[:, g:g + 1]
                pu = pre_ref[rows, cols]
                dy = dyv_ref[rows, cols]
                du = dy * sv
                dsv = dy * _gelu(pu)
                dpu = du * _gelu_grad(pu)
                dpre_ref[rows, cols] = dpu.astype(BF16)
                dbin_ref[:, cols] += _colsum(dpu)
                dsv_b = dsv.astype(BF16)
                dwm = dwm + _dot(dsv_b, vn_b, NT)
                dbs = dbs + jnp.sum(dsv, axis=-1, keepdims=True)
                dvn_ref[rows, cols] = _dot(wm, dsv_b, TN)
            dws_ref[g] += jnp.where(keep, dwm, 0.0)
            dbs_cols.append(dbs)
        dbs_ref[...] += jnp.concatenate(dbs_cols, axis=1)
        dvn = dvn_ref[...]
        dlg_ref[...] += _colsum(dvn * vhat)
        dlb_ref[...] += _colsum(dvn)
        dvh = dvn * lng_ref[...]
        m1 = jnp.mean(dvh, axis=-1, keepdims=True)
        m2 = jnp.mean(dvh * vhat, axis=-1, keepdims=True)
        dv = rstd * (dvh - m1 - vhat * m2)
        dpv = dv * _gelu_grad(pv)
        dpre_ref[:, w:] = dpv.astype(BF16)
        dbin_ref[:, w:] += _colsum(dpv)

    full3 = pl.BlockSpec((GM_GROUPS, GM_CHUNK, GM_CHUNK), lambda i: (0, 0, 0))
    bst = pl.BlockSpec((GM_CHUNK, GM_GROUPS), lambda i: (0, 0))
    return _call(body, "gm_spatial_bwd",
                 (_sds((t, w2), BF16), _sds((GM_GROUPS, GM_CHUNK, GM_CHUNK), F32), _sds((GM_CHUNK, GM_GROUPS), F32),
                  _sds((1, w), F32), _sds((1, w), F32), _sds((1, w2), F32)),
                 grid=(t // tm,),
                 in_specs=[_tile_spec(tm, w2), _tile_spec(tm, w), _vec_spec(w), _vec_spec(w), full3, bst],
                 out_specs=(_tile_spec(tm, w2), full3, bst, _vec_spec(w), _vec_spec(w), _vec_spec(w2)),
                 scratch=[pltpu.VMEM((tm, w), F32)], vmem=VMEM_BIG)(pre, dyv, lng, lnb, w_s, b_st)


def _head_masks():
    lane = lax.broadcasted_iota(jnp.int32, (1, LANES), 1)
    return lane < HEAD_DIM


def _two_heads(x, m0):
    z = jnp.zeros_like(x)
    return jnp.where(m0, x, z), jnp.where(m0, z, x)


def _qkv_specs(seq, nq, blocked_q):
    if blocked_q:
        q = pl.BlockSpec((None, TQ_(seq), LANES), lambda b, p, i: (0, b * nq + i, p))
        k = pl.BlockSpec((None, seq, LANES), lambda b, p, i: (1, b, p))
        v = pl.BlockSpec((None, seq, LANES), lambda b, p, i: (2, b, p))
    else:
        q = pl.BlockSpec((None, seq, LANES), lambda b, p: (0, b, p))
        k = pl.BlockSpec((None, seq, LANES), lambda b, p: (1, b, p))
        v = pl.BlockSpec((None, seq, LANES), lambda b, p: (2, b, p))
    return q, k, v


def TQ_(seq):
    return min(TQ, seq)


def _fox_gate_fwd(ft, b_f, seq):
    nh, t = ft.shape
    nch = seq // LANES

    def body(ft_ref, bf_ref, fr_ref, fc_ref):
        r = lax.broadcasted_iota(jnp.int32, (LANES, LANES), 0)
        c = lax.broadcasted_iota(jnp.int32, (LANES, LANES), 1)
        upper = jnp.where(r <= c, 1.0, 0.0).astype(BF16)
        rr = lax.broadcasted_iota(jnp.int32, (nh, LANES), 0)
        cc = lax.broadcasted_iota(jnp.int32, (nh, LANES), 1)
        eye = jnp.where(rr == cc, 1.0, 0.0).astype(BF16)
        carry = jnp.zeros((nh, 1), F32)
        for ci in range(nch):
            cols = slice(ci * LANES, (ci + 1) * LANES)
            lf = _log_sigmoid(ft_ref[:, cols] + bf_ref[...])
            cs = _dot_exact(lf, upper) + carry
            fr_ref[:, cols] = cs
            carry = cs[:, LANES - 1:LANES]
            fc_ref[cols, :] = _dot_exact(cs, eye, TN)[:, :nh]

    return _call(body, "fox_gate_fwd", (_sds((nh, t), F32), _sds((t, nh), F32)), grid=(t // seq,),
                 in_specs=[pl.BlockSpec((nh, seq), lambda b: (0, b)), pl.BlockSpec((nh, 1), lambda b: (0, 0))],
                 out_specs=(pl.BlockSpec((nh, seq), lambda b: (0, b)), pl.BlockSpec((seq, nh), lambda b: (b, 0))))(ft, b_f)


def _fox_gate_bwd(ft, b_f, dfk, dfq, seq):
    nh, t = ft.shape
    nch = seq // LANES

    def body(ft_ref, bf_ref, dfk_ref, dfq_ref, dl_ref, db_ref):
        @pl.when(pl.program_id(0) == 0)
        def _():
            db_ref[...] = jnp.zeros_like(db_ref)

        r = lax.broadcasted_iota(jnp.int32, (LANES, LANES), 0)
        c = lax.broadcasted_iota(jnp.int32, (LANES, LANES), 1)
        lower = jnp.where(r >= c, 1.0, 0.0).astype(BF16)
        carry = jnp.zeros((nh, 1), F32)
        tot = jnp.zeros((nh, 1), F32)
        for ci in reversed(range(nch)):
            cols = slice(ci * LANES, (ci + 1) * LANES)
            rc = _dot_exact(dfk_ref[:, cols] + dfq_ref[:, cols], lower) + carry
            carry = rc[:, 0:1]
            dl = rc * jax.nn.sigmoid(-(ft_ref[:, cols] + bf_ref[...]))
            dl_ref[:, cols] = dl
            tot = tot + jnp.sum(dl, axis=-1, keepdims=True)
        db_ref[...] += tot

    blk = pl.BlockSpec((nh, seq), lambda b: (0, b))
    one = pl.BlockSpec((nh, 1), lambda b: (0, 0))
    return _call(body, "fox_gate_bwd", (_sds((nh, t), F32), _sds((nh, 1), F32)), grid=(t // seq,),
                 in_specs=[blk, one, blk, blk], out_specs=(blk, one))(ft, b_f, dfk, dfq)


def _fox_fwd(qkv, fcol, frow, nb, seq):
    _, t, d = qkv.shape
    npair = d // LANES
    tq = TQ_(seq)
    nq = seq // tq
    scale = HEAD_DIM ** -0.5

    def body(q_ref, k_ref, v_ref, fc_ref, fr_ref, o_ref, lse_ref):
        i = pl.program_id(2)
        m0 = _head_masks()
        qm = _two_heads(q_ref[...], m0)
        fq = (fc_ref[:, 0:1], fc_ref[:, 1:2])
        row = lax.broadcasted_iota(jnp.int32, (tq, tq), 0)
        col = lax.broadcasted_iota(jnp.int32, (tq, tq), 1)

        def step(j, carry, diag):
            off = pl.multiple_of(j * tq, tq)
            kb = k_ref[pl.ds(off, tq), :]
            vb = v_ref[pl.ds(off, tq), :]
            nxt, parts = [], []
            for hh in range(2):
                m, l = carry[2 * hh], carry[2 * hh + 1]
                s = lax.dot_general(qm[hh], kb, NT, preferred_element_type=F32) * scale
                s = s + fq[hh] - fr_ref[hh:hh + 1, pl.ds(off, tq)]
                if diag:
                    s = jnp.where(col <= row, s, NEG_INF)
                mn = jnp.maximum(m, jnp.max(s, axis=-1, keepdims=True))
                a = jnp.exp(m - mn)
                p = jnp.exp(s - mn)
                nxt += [mn, a * l + jnp.sum(p, axis=-1, keepdims=True)]
                parts.append((a, jnp.dot(p.astype(BF16), vb, preferred_element_type=F32)))
            acc = carry[4]
            acc = jnp.where(m0, parts[0][0] * acc + parts[0][1], parts[1][0] * acc + parts[1][1])
            return (*nxt, acc)

        neg = jnp.full((tq, 1), NEG_INF, F32)
        zero = jnp.zeros((tq, 1), F32)
        carry = (neg, zero, neg, zero, jnp.zeros((tq, LANES), F32))
        carry = lax.fori_loop(0, i, lambda j, c: step(j, c, False), carry)
        m_a, l_a, m_b, l_b, acc = step(i, carry, True)
        o_ref[...] = acc / jnp.where(m0, l_a, l_b)
        lse_ref[:, 0:1] = m_a + jnp.log(l_a)
        lse_ref[:, 1:2] = m_b + jnp.log(l_b)

    q_spec, k_spec, v_spec = _qkv_specs(seq, nq, True)
    col_spec = pl.BlockSpec((None, tq, 2), lambda b, p, i: (p, b * nq + i, 0))
    return _call(body, "fox_fwd", (_sds((t, d), F32), _sds((npair, t, 2), F32)), grid=(nb, npair, nq),
                 in_specs=[q_spec, k_spec, v_spec, col_spec, pl.BlockSpec((None, 2, seq), lambda b, p, i: (p, 0, b))],
                 out_specs=(pl.BlockSpec((tq, LANES), lambda b, p, i: (b * nq + i, p)), col_spec),
                 vmem=VMEM_BIG)(qkv, qkv, qkv, fcol, frow)


def _fox_bwd(qkv, fcol, frow, o, do, lse, nb, seq):
    _, t, d = qkv.shape
    npair = d // LANES
    tq = TQ_(seq)
    nq = seq // tq
    scale = HEAD_DIM ** -0.5

    def body(q_ref, k_ref, v_ref, fc_ref, fr_ref, o_ref, do_ref, lse_ref, dqkv_ref, df_ref, dfq_ref, dk_acc, dv_acc):
        m0 = _head_masks()
        row = lax.broadcasted_iota(jnp.int32, (tq, tq), 0)
        col = lax.broadcasted_iota(jnp.int32, (tq, tq), 1)
        dk_acc[...] = jnp.zeros_like(dk_acc)
        dv_acc[...] = jnp.zeros_like(dv_acc)
        df_ref[...] = jnp.zeros_like(df_ref)

        def q_block(i, _):
            qoff = pl.multiple_of(i * tq, tq)
            qrows = pl.ds(qoff, tq)
            qm = _two_heads(q_ref[qrows, :], m0)
            dov = do_ref[qrows, :]
            dd = dov * o_ref[qrows, :]
            dm = _two_heads(dov.astype(BF16), m0)
            delta = (jnp.sum(jnp.where(m0, dd, 0.0), axis=-1, keepdims=True),
                     jnp.sum(jnp.where(m0, 0.0, dd), axis=-1, keepdims=True))
            fq = (fc_ref[qrows, 0:1], fc_ref[qrows, 1:2])
            ls = (lse_ref[qrows, 0:1], lse_ref[qrows, 1:2])

            def step(j, carry, diag):
                off = pl.multiple_of(j * tq, tq)
                krows = pl.ds(off, tq)
                kb = k_ref[krows, :]
                vb = v_ref[krows, :]
                dqs, rowsums = [], []
                dk = jnp.zeros((tq, LANES), F32)
                dv = jnp.zeros((tq, LANES), F32)
                for hh in range(2):
                    s = lax.dot_general(qm[hh], kb, NT, preferred_element_type=F32) * scale
                    s = s + fq[hh] - fr_ref[hh:hh + 1, krows]
                    if diag:
                        s = jnp.where(col <= row, s, NEG_INF)
                    p = jnp.exp(s - ls[hh])
                    dp = lax.dot_general(dm[hh], vb, NT, preferred_element_type=F32)
                    ds = p * (dp - delta[hh])
                    df_ref[hh:hh + 1, krows] -= _colsum(ds)
                    rowsums.append(carry[1 + hh] + jnp.sum(ds, axis=-1, keepdims=True))
                    ds_b = ds.astype(BF16)
                    dqs.append(jnp.dot(ds_b, kb, preferred_element_type=F32))
                    dk = dk + lax.dot_general(ds_b, qm[hh], TN, preferred_element_type=F32)
                    dv = dv + lax.dot_general(p.astype(BF16), dm[hh], TN, preferred_element_type=F32)
                dk_acc[krows, :] += dk
                dv_acc[krows, :] += dv
                return (carry[0] + jnp.where(m0, dqs[0], dqs[1]), *rowsums)

            zero = jnp.zeros((tq, 1), F32)
            carry = lax.fori_loop(0, i, lambda j, c: step(j, c, False), (jnp.zeros((tq, LANES), F32), zero, zero))
            dq, rs_a, rs_b = step(i, carry, True)
            dqkv_ref[0, qrows, :] = (dq * scale).astype(BF16)
            dfq_ref[qrows, 0:1] = rs_a
            dfq_ref[qrows, 1:2] = rs_b
            return 0

        lax.fori_loop(0, nq, q_block, 0)
        dqkv_ref[1] = (dk_acc[...] * scale).astype(BF16)
        dqkv_ref[2] = dv_acc[...].astype(BF16)

    q_spec, k_spec, v_spec = _qkv_specs(seq, nq, False)
    col_spec = pl.BlockSpec((None, seq, 2), lambda b, p: (p, b, 0))
    row_spec = pl.BlockSpec((None, 2, seq), lambda b, p: (p, 0, b))
    tile = pl.BlockSpec((seq, LANES), lambda b, p: (b, p))
    return _call(body, "fox_bwd", (_sds((3, t, d), BF16), _sds((npair, 2, t), F32), _sds((npair, t, 2), F32)), grid=(nb, npair),
                 in_specs=[q_spec, k_spec, v_spec, col_spec, row_spec, tile, tile, col_spec],
                 out_specs=(pl.BlockSpec((3, seq, LANES), lambda b, p: (0, b, p)), row_spec, col_spec),
                 scratch=[pltpu.VMEM((seq, LANES), F32), pltpu.VMEM((seq, LANES), F32)],
                 vmem=VMEM_BIG)(qkv, qkv, qkv, fcol, frow, o, do, lse)


def _split2(x):
    hi = x.astype(BF16)
    return hi, (x - hi.astype(F32)).astype(BF16)


def _sum_right(x, tri):
    hi, lo = _split2(x)
    return jnp.dot(hi, tri, preferred_element_type=F32) + jnp.dot(lo, tri, preferred_element_type=F32)


def _sb_scores(qm_h, kb, scale, mask):
    z = lax.dot_general(qm_h, kb, NT, preferred_element_type=F32) * scale
    lb = _log_sigmoid(z)
    l1m = lb - z
    if mask is not None:
        l1m = jnp.where(mask, l1m, 0.0)
    return lb, l1m


def _sb_fwd(qkv, nb, seq):
    _, t, d = qkv.shape
    npair = d // LANES
    tq = TQ_(seq)
    nq = seq // tq
    scale = HEAD_DIM ** -0.5

    def body(q_ref, k_ref, v_ref, o_ref, lt_ref):
        i = pl.program_id(2)
        m0 = _head_masks()
        qm = _two_heads(q_ref[...], m0)
        row = lax.broadcasted_iota(jnp.int32, (tq, tq), 0)
        col = lax.broadcasted_iota(jnp.int32, (tq, tq), 1)
        after = jnp.where(row > col, 1.0, 0.0).astype(BF16)

        def step(j, carry, diag):
            off = pl.multiple_of(j * tq, tq)
            kb = k_ref[pl.ds(off, tq), :]
            vb = v_ref[pl.ds(off, tq), :]
            mask = (col < row) if diag else None
            nxt, parts = [], []
            for hh in range(2):
                lb, l1m = _sb_scores(qm[hh], kb, scale, mask)
                rest = _sum_right(l1m, after) + carry[hh]
                a = jnp.exp(lb + rest)
                if diag:
                    a = jnp.where(mask, a, 0.0)
                parts.append(jnp.dot(a.astype(BF16), vb, preferred_element_type=F32))
                nxt.append(carry[hh] + jnp.sum(l1m, axis=-1, keepdims=True))
            return (*nxt, carry[2] + jnp.where(m0, parts[0], parts[1]))

        zero = jnp.zeros((tq, 1), F32)
        carry = step(i, (zero, zero, jnp.zeros((tq, LANES), F32)), True)

        def alive(st):
            return (st[0] < i) & (jnp.max(jnp.maximum(st[1], st[2])) > SB_DEAD)

        def more(st):
            return (st[0] + 1, *step(i - 1 - st[0], st[1:], False))

        done, lt_a, lt_b, acc = lax.while_loop(alive, more, (jnp.int32(0), *carry))
        o_ref[...] = acc
        lt_ref[:, 0:1] = lt_a
        lt_ref[:, 1:2] = lt_b
        lt_ref[:, 2:3] = jnp.zeros((tq, 1), F32) + done.astype(F32)
        lt_ref[:, 3:4] = zero

    q_spec, k_spec, v_spec = _qkv_specs(seq, nq, True)
    return _call(body, "sb_fwd", (_sds((t, d), F32), _sds((npair, t, 4), F32)), grid=(nb, npair, nq),
                 in_specs=[q_spec, k_spec, v_spec],
                 out_specs=(pl.BlockSpec((tq, LANES), lambda b, p, i: (b * nq + i, p)),
                            pl.BlockSpec((None, tq, 4), lambda b, p, i: (p, b * nq + i, 0))), vmem=VMEM_BIG)(qkv, qkv, qkv)


def _sb_bwd(qkv, do, ltot, nb, seq):
    _, t, d = qkv.shape
    npair = d // LANES
    tq = TQ_(seq)
    nq = seq // tq
    scale = HEAD_DIM ** -0.5

    def body(q_ref, k_ref, v_ref, do_ref, lt_ref, dqkv_ref, dk_acc, dv_acc):
        m0 = _head_masks()
        row = lax.broadcasted_iota(jnp.int32, (tq, tq), 0)
        col = lax.broadcasted_iota(jnp.int32, (tq, tq), 1)
        upto = jnp.where(row <= col, 1.0, 0.0).astype(BF16)
        left_of = jnp.where(row < col, 1.0, 0.0).astype(BF16)
        dk_acc[...] = jnp.zeros_like(dk_acc)
        dv_acc[...] = jnp.zeros_like(dv_acc)

        def q_block(i, _):
            qoff = pl.multiple_of(i * tq, tq)
            qrows = pl.ds(qoff, tq)
            qm = _two_heads(q_ref[qrows, :], m0)
            dm = _two_heads(do_ref[qrows, :].astype(BF16), m0)
            ltot = (lt_ref[qrows, 0:1], lt_ref[qrows, 1:2])

            def step(j, carry, diag):
                off = pl.multiple_of(j * tq, tq)
                krows = pl.ds(off, tq)
                kb = k_ref[krows, :]
                vb = v_ref[krows, :]
                mask = (col < row) if diag else None
                nxt, dqs = [], []
                dk = jnp.zeros((tq, LANES), F32)
                dv = jnp.zeros((tq, LANES), F32)
                for hh in range(2):
                    cl, ce = carry[2 * hh], carry[2 * hh + 1]
                    lb, l1m = _sb_scores(qm[hh], kb, scale, mask)
                    a = jnp.exp(lb + (ltot[hh] - (_sum_right(l1m, upto) + cl)))
                    if diag:
                        a = jnp.where(mask, a, 0.0)
                    e = lax.dot_general(dm[hh], vb, NT, preferred_element_type=F32) * a
                    before = _sum_right(e, left_of) + ce
                    beta = jnp.exp(lb)
                    dz = e * (1.0 - beta) - before * beta
                    if diag:
                        dz = jnp.where(mask, dz, 0.0)
                    dz_b = dz.astype(BF16)
                    dqs.append(jnp.dot(dz_b, kb, preferred_element_type=F32))
                    dk = dk + lax.dot_general(dz_b, qm[hh], TN, preferred_element_type=F32)
                    dv = dv + lax.dot_general(a.astype(BF16), dm[hh], TN, preferred_element_type=F32)
                    nxt += [cl + jnp.sum(l1m, axis=-1, keepdims=True), ce + jnp.sum(e, axis=-1, keepdims=True)]
                dk_acc[krows, :] += dk
                dv_acc[krows, :] += dv
                return (*nxt, carry[4] + jnp.where(m0, dqs[0], dqs[1]))

            zero = jnp.zeros((tq, 1), F32)
            carry = (zero, zero, zero, zero, jnp.zeros((tq, LANES), F32))
            first = i - jnp.max(lt_ref[qrows, 2:3]).astype(jnp.int32)
            carry = lax.fori_loop(first, i, lambda j, c: step(j, c, False), carry)
            carry = step(i, carry, True)
            dqkv_ref[0, qrows, :] = (carry[4] * scale).astype(BF16)
            return 0

        lax.fori_loop(0, nq, q_block, 0)
        dqkv_ref[1] = (dk_acc[...] * scale).astype(BF16)
        dqkv_ref[2] = dv_acc[...].astype(BF16)

    q_spec, k_spec, v_spec = _qkv_specs(seq, nq, False)
    tile = pl.BlockSpec((seq, LANES), lambda b, p: (b, p))
    return _call(body, "sb_bwd", _sds((3, t, d), BF16), grid=(nb, npair),
                 in_specs=[q_spec, k_spec, v_spec, tile, pl.BlockSpec((None, seq, 4), lambda b, p: (p, b, 0))],
                 out_specs=pl.BlockSpec((3, seq, LANES), lambda b, p: (0, b, p)),
                 scratch=[pltpu.VMEM((seq, LANES), F32), pltpu.VMEM((seq, LANES), F32)], vmem=VMEM_BIG)(qkv, qkv, qkv, do, ltot)


def _qkv_proj(name, h, w):
    t, d = h.shape
    tm = min(TMM, t)
    return _mm(name, h, w, _sds((3, t, d), BF16), (3, t // tm, 1),
               pl.BlockSpec((tm, d), lambda s, i, r: (i, 0)), pl.BlockSpec((d, d), lambda s, i, r: (0, s)),
               pl.BlockSpec((None, tm, d), lambda s, i, r: (s, i, 0)), NN)


def _qkv_dw(name, h, dqkv):
    t, d = h.shape
    tk = min(TMM, t)
    return _mm(name, h, dqkv, _sds((d, 3 * d), F32), (3, t // tk),
               pl.BlockSpec((tk, d), lambda s, r: (r, 0)), pl.BlockSpec((None, tk, d), lambda s, r: (s, r, 0)),
               pl.BlockSpec((d, d), lambda s, r: (0, s)), TN)


def _qkv_dh(name, dqkv, w, extra=(), extra_specs=(), epilogue=None):
    _, t, d = dqkv.shape
    tm = min(TMM, t)
    return _mm(name, dqkv, w, _sds((t, d), F32), (t // tm, 3),
               pl.BlockSpec((None, tm, d), lambda i, r: (r, i, 0)), pl.BlockSpec((d, d), lambda i, r: (0, r)),
               pl.BlockSpec((tm, d), lambda i, r: (i, 0)), NT, extra, extra_specs, epilogue)


def _glu(pre_block, d):
    return pre_block[:, :d] * jax.nn.sigmoid(pre_block[:, d:])


def _cv_conv_fwd(pre, dw, dwb, lng, lnb, seq):
    t, d2 = pre.shape
    d = d2 // 2
    tt = min(TT, seq)
    nt = seq // tt
    hb = tt // CONV_HALO

    def body(pre_ref, halo_ref, dw_ref, dwb_ref, lng_ref, lnb_ref, y1_ref, y2_ref, ext_ref):
        i = pl.program_id(1)
        ext_ref[0:CONV_HALO, :] = jnp.where(i == 0, 0.0, _glu(halo_ref[...], d))
        ext_ref[CONV_HALO:, :] = _glu(pre_ref[...], d)
        acc = jnp.zeros((tt, d), F32) + dwb_ref[...]
        for k in range(CONV_WIDTH):
            acc = acc + ext_ref[pl.ds(CONV_HALO - (CONV_WIDTH - 1) + k, tt), :] * dw_ref[k:k + 1, :]
        y1_ref[...] = acc
        yc, rstd = _ln_stats(acc)
        y2_ref[...] = _silu(yc * rstd * lng_ref[...] + lnb_ref[...]).astype(BF16)

    vec = pl.BlockSpec((1, d), lambda b, i: (0, 0))
    tile = pl.BlockSpec((tt, d), lambda b, i: (b * nt + i, 0))
    return _call(body, "cv_conv_fwd", (_sds((t, d), F32), _sds((t, d), BF16)), grid=(t // seq, nt),
                 in_specs=[pl.BlockSpec((tt, d2), lambda b, i: (b * nt + i, 0)),
                           pl.BlockSpec((CONV_HALO, d2), lambda b, i: (jnp.maximum((b * nt + i) * hb - 1, 0), 0)),
                           pl.BlockSpec((CONV_HALO, d), lambda b, i: (0, 0)), vec, vec, vec],
                 out_specs=(tile, tile), scratch=[pltpu.VMEM((tt + CONV_HALO, d), F32)], vmem=VMEM_BIG)(pre, pre, dw, dwb, lng, lnb)


def _cv_norm_bwd(y1, dy2, lng, lnb):
    t, d = y1.shape
    tm = min(TM, t)

    def body(y1_ref, dy2_ref, lng_ref, lnb_ref, dy1_ref, dlg_ref, dlb_ref, dsum_ref):
        @pl.when(pl.program_id(0) == 0)
        def _():
            dlg_ref[...] = jnp.zeros_like(dlg_ref)
            dlb_ref[...] = jnp.zeros_like(dlb_ref)
            dsum_ref[...] = jnp.zeros_like(dsum_ref)

        yc, rstd = _ln_stats(y1_ref[...])
        yhat = yc * rstd
        n = yhat * lng_ref[...] + lnb_ref[...]
        sg = jax.nn.sigmoid(n)
        dn = dy2_ref[...] * (sg * (1.0 + n * (1.0 - sg)))
        dlg_ref[...] += _colsum(dn * yhat)
        dlb_ref[...] += _colsum(dn)
        dyh = dn * lng_ref[...]
        m1 = jnp.mean(dyh, axis=-1, keepdims=True)
        m2 = jnp.mean(dyh * yhat, axis=-1, keepdims=True)
        dy1 = rstd * (dyh - m1 - yhat * m2)
        dy1_ref[...] = dy1
        dsum_ref[...] += _colsum(dy1)

    return _call(body, "cv_norm_bwd", (_sds((t, d), F32), _sds((1, d), F32), _sds((1, d), F32), _sds((1, d), F32)),
                 grid=(t // tm,), in_specs=[_tile_spec(tm, d), _tile_spec(tm, d), _vec_spec(d), _vec_spec(d)],
                 out_specs=(_tile_spec(tm, d), _vec_spec(d), _vec_spec(d), _vec_spec(d)))(y1, dy2, lng, lnb)


def _cv_conv_bwd(pre, dy1, dw, seq):
    t, d2 = pre.shape
    d = d2 // 2
    tt = min(TT, seq)
    nt = seq // tt
    hb = tt // CONV_HALO
    last_halo = t // CONV_HALO - 1

    def body(pre_ref, halo_ref, dy_ref, dyn_ref, dw_ref, dpre_ref, ddw_ref, dbin_ref, ext_ref, dext_ref):
        b, i = pl.program_id(0), pl.program_id(1)

        @pl.when((b == 0) & (i == 0))
        def _():
            ddw_ref[...] = jnp.zeros_like(ddw_ref)
            dbin_ref[...] = jnp.zeros_like(dbin_ref)

        pv = pre_ref[...]
        ext_ref[0:CONV_HALO, :] = jnp.where(i == 0, 0.0, _glu(halo_ref[...], d))
        ext_ref[CONV_HALO:, :] = _glu(pv, d)
        dyv = dy_ref[...]
        dext_ref[0:tt, :] = dyv
        dext_ref[tt:, :] = jnp.where(i == nt - 1, 0.0, dyn_ref[...])
        dy0 = jnp.zeros((tt, d), F32)
        for k in range(CONV_WIDTH):
            ddw_ref[k:k + 1, :] += _colsum(dyv * ext_ref[pl.ds(CONV_HALO - (CONV_WIDTH - 1) + k, tt), :])
            dy0 = dy0 + dext_ref[pl.ds(CONV_WIDTH - 1 - k, tt), :] * dw_ref[k:k + 1, :]
        a = pv[:, :d]
        sg = jax.nn.sigmoid(pv[:, d:])
        da = dy0 * sg
        dg = dy0 * a * sg * (1.0 - sg)
        dpre_ref[:, :d] = da.astype(BF16)
        dpre_ref[:, d:] = dg.astype(BF16)
        dbin_ref[:, :d] += _colsum(da)
        dbin_ref[:, d:] += _colsum(dg)

    return _call(body, "cv_conv_bwd", (_sds((t, d2), BF16), _sds((CONV_HALO, d), F32), _sds((1, d2), F32)), grid=(t // seq, nt),
                 in_specs=[pl.BlockSpec((tt, d2), lambda b, i: (b * nt + i, 0)),
                           pl.BlockSpec((CONV_HALO, d2), lambda b, i: (jnp.maximum((b * nt + i) * hb - 1, 0), 0)),
                           pl.BlockSpec((tt, d), lambda b, i: (b * nt + i, 0)),
                           pl.BlockSpec((CONV_HALO, d), lambda b, i: (jnp.minimum((b * nt + i + 1) * hb, last_halo), 0)),
                           pl.BlockSpec((CONV_HALO, d), lambda b, i: (0, 0))],
                 out_specs=(pl.BlockSpec((tt, d2), lambda b, i: (b * nt + i, 0)),
                            pl.BlockSpec((CONV_HALO, d), lambda b, i: (0, 0)), pl.BlockSpec((1, d2), lambda b, i: (0, 0))),
                 scratch=[pltpu.VMEM((tt + CONV_HALO, d), F32), pltpu.VMEM((tt + CONV_HALO, d), F32)],
                 vmem=VMEM_BIG)(pre, pre, dy1, dy1, dw)


def _adamw(name, w, m, v, g=None, parts=None):
    rows, cols = w.shape
    tr = rows
    for cand in ((512,) if parts is None else ()) + (256, 128, 64, 32, 16, 8):
        if rows % cand == 0 and rows > cand:
            tr = cand
            break
    bc1 = 1.0 - ADAM_B1 ** ADAM_STEP
    bc2 = 1.0 - ADAM_B2 ** ADAM_STEP

    def body(w_ref, m_ref, v_ref, g_ref, go_ref, d_ref, mo_ref, vo_ref):
        if parts is None:
            gv = g_ref[...]
        else:
            gv = g_ref[0]
            for s in range(1, N_DEV):
                gv = gv + g_ref[s]
        mn = ADAM_B1 * m_ref[...] + (1.0 - ADAM_B1) * gv
        vn = ADAM_B2 * v_ref[...] + (1.0 - ADAM_B2) * (gv * gv)
        m_hat = mn / bc1
        v_hat = vn / bc2
        go_ref[...] = gv
        d_ref[...] = -ADAM_LR * (m_hat / (jnp.sqrt(v_hat) + ADAM_EPS) + ADAM_WD * w_ref[...])
        mo_ref[...] = mn
        vo_ref[...] = vn

    blk = pl.BlockSpec((tr, cols), lambda i: (i, 0))
    g_in, g_spec = (g, blk) if parts is None else (parts, pl.BlockSpec((N_DEV, tr, cols), lambda i: (0, i, 0)))
    out = _sds((rows, cols), F32)
    return _call(body, name, (out, out, out, out), grid=(rows // tr,), in_specs=[blk, blk, blk, g_spec],
                 out_specs=(blk, blk, blk, blk), vmem=VMEM_BIG)(w, m, v, g_in)


def _pad_rows(a, rows):
    return jnp.pad(a, ((0, rows - a.shape[0]), (0, 0)))


def _full_cols(gathered, n):
    k = gathered.shape[2]
    return jnp.transpose(gathered[0], (1, 0, 2)).reshape(k, N_DEV * n)


def _col_blocks(full, n):
    k = full.shape[0]
    return jnp.transpose(full.reshape(k, N_DEV, n), (1, 0, 2))[None]


def kernel(x, c, mod_w, mod_b, ln1_g, ln1_b, ln2_g, ln2_b, ffn_w_in, ffn_w_out, gm_w_in, gm_b_in, gm_ln_g, gm_ln_b, gm_w_s, gm_b_s, gm_w_out, fox_w_in, fox_b_f, fox_w_out, sb_w_in, sb_w_out, cv_w_in, cv_b_in, cv_dw, cv_dw_b, cv_ln_g, cv_ln_b, cv_w_out, cv_b_out, loss_target, m_mod_w, m_mod_b, m_ln1_g, m_ln1_b, m_ln2_g, m_ln2_b, m_ffn_w_in, m_ffn_w_out, m_gm_w_in, m_gm_b_in, m_gm_ln_g, m_gm_ln_b, m_gm_w_s, m_gm_b_s, m_gm_w_out, m_fox_w_in, m_fox_b_f, m_fox_w_out, m_sb_w_in, m_sb_w_out, m_cv_w_in, m_cv_b_in, m_cv_dw, m_cv_dw_b, m_cv_ln_g, m_cv_ln_b, m_cv_w_out, m_cv_b_out, v_mod_w, v_mod_b, v_ln1_g, v_ln1_b, v_ln2_g, v_ln2_b, v_ffn_w_in, v_ffn_w_out, v_gm_w_in, v_gm_b_in, v_gm_ln_g, v_gm_ln_b, v_gm_w_s, v_gm_b_s, v_gm_w_out, v_fox_w_in, v_fox_b_f, v_fox_w_out, v_sb_w_in, v_sb_w_out, v_cv_w_in, v_cv_b_in, v_cv_dw, v_cv_dw_b, v_cv_ln_g, v_cv_ln_b, v_cv_w_out, v_cv_b_out):
    weights = dict(mod_w=mod_w, mod_b=mod_b, ln1_g=ln1_g, ln1_b=ln1_b, ln2_g=ln2_g, ln2_b=ln2_b, ffn_w_in=ffn_w_in, ffn_w_out=ffn_w_out, gm_w_in=gm_w_in, gm_b_in=gm_b_in, gm_ln_g=gm_ln_g, gm_ln_b=gm_ln_b, gm_w_s=gm_w_s, gm_b_s=gm_b_s, gm_w_out=gm_w_out, fox_w_in=fox_w_in, fox_b_f=fox_b_f, fox_w_out=fox_w_out, sb_w_in=sb_w_in, sb_w_out=sb_w_out, cv_w_in=cv_w_in, cv_b_in=cv_b_in, cv_dw=cv_dw, cv_dw_b=cv_dw_b, cv_ln_g=cv_ln_g, cv_ln_b=cv_ln_b, cv_w_out=cv_w_out, cv_b_out=cv_b_out)
    mom1 = dict(mod_w=m_mod_w, mod_b=m_mod_b, ln1_g=m_ln1_g, ln1_b=m_ln1_b, ln2_g=m_ln2_g, ln2_b=m_ln2_b, ffn_w_in=m_ffn_w_in, ffn_w_out=m_ffn_w_out, gm_w_in=m_gm_w_in, gm_b_in=m_gm_b_in, gm_ln_g=m_gm_ln_g, gm_ln_b=m_gm_ln_b, gm_w_s=m_gm_w_s, gm_b_s=m_gm_b_s, gm_w_out=m_gm_w_out, fox_w_in=m_fox_w_in, fox_b_f=m_fox_b_f, fox_w_out=m_fox_w_out, sb_w_in=m_sb_w_in, sb_w_out=m_sb_w_out, cv_w_in=m_cv_w_in, cv_b_in=m_cv_b_in, cv_dw=m_cv_dw, cv_dw_b=m_cv_dw_b, cv_ln_g=m_cv_ln_g, cv_ln_b=m_cv_ln_b, cv_w_out=m_cv_w_out, cv_b_out=m_cv_b_out)
    mom2 = dict(mod_w=v_mod_w, mod_b=v_mod_b, ln1_g=v_ln1_g, ln1_b=v_ln1_b, ln2_g=v_ln2_g, ln2_b=v_ln2_b, ffn_w_in=v_ffn_w_in, ffn_w_out=v_ffn_w_out, gm_w_in=v_gm_w_in, gm_b_in=v_gm_b_in, gm_ln_g=v_gm_ln_g, gm_ln_b=v_gm_ln_b, gm_w_s=v_gm_w_s, gm_b_s=v_gm_b_s, gm_w_out=v_gm_w_out, fox_w_in=v_fox_w_in, fox_b_f=v_fox_b_f, fox_w_out=v_fox_w_out, sb_w_in=v_sb_w_in, sb_w_out=v_sb_w_out, cv_w_in=v_cv_w_in, cv_b_in=v_cv_b_in, cv_dw=v_cv_dw, cv_dw_b=v_cv_dw_b, cv_ln_g=v_cv_ln_g, cv_ln_b=v_cv_ln_b, cv_w_out=v_cv_w_out, cv_b_out=v_cv_b_out)
    names = list(weights)

    nb, seq, d = x.shape
    t = nb * seq
    nl = mod_w.shape[0]
    alpha = (2.0 * nl) ** 0.25
    me = 4 * lax.axis_index("x") + 2 * lax.axis_index("y") + lax.axis_index("c")
    xs = x.reshape(t, d)
    tgt = loss_target.reshape(t, d)
    n_mod = mod_w.shape[2]
    n_ffn = ffn_w_in.shape[2]
    n_heads = d // HEAD_DIM
    npair = d // LANES

    c_all = _exchange_small(_pad_rows(c, 8), "gather_c", False)[:, :nb].reshape(N_DEV * nb, d)
    mod_b_loc = lax.dynamic_slice_in_dim(mod_b, me * n_mod, n_mod, axis=1)[:, None, :]
    mod_loc = _mod_fwd(c_all, mod_w, mod_b_loc)
    mod_g = _exchange_small(mod_loc.reshape(nl * N_DEV * nb, n_mod), "gather_mod", False)
    mod_all = jnp.transpose(mod_g.reshape(N_DEV, nl, N_DEV * nb, n_mod), (1, 2, 0, 3)).reshape(nl, N_DEV * nb, N_DEV * n_mod)
    mod_me = lax.dynamic_slice_in_dim(mod_all, me * nb, nb, axis=1)
    mods = [[mod_me[l, :, k * d:(k + 1) * d][:, None, :] for k in range(6)] for l in range(nl)]

    big = ["ffn_w_in", "ffn_w_out", "gm_w_in", "gm_w_out", "fox_w_in", "fox_w_out", "sb_w_in", "sb_w_out", "cv_w_in", "cv_w_out"]
    gathered = dict(zip(big, _gather_weights([weights[n].astype(BF16) for n in big])))
    w_ffn_in = gathered["ffn_w_in"]
    w_ffn_out = gathered["ffn_w_out"].reshape(nl, N_DEV // 2, n_ffn, d)
    w_ffn_out_rows = w_ffn_out.reshape(nl * (N_DEV // 2), n_ffn, d)
    sq = lambda n: gathered[n].reshape(d, d)
    fox_full = _full_cols(gathered["fox_w_in"], fox_w_in.shape[2])
    fox_qkv_w, fox_f_wt = fox_full[:, :3 * d], jnp.transpose(fox_full[:, 3 * d:])
    sb_qkv_w = _full_cols(gathered["sb_w_in"], sb_w_in.shape[2])
    cvp = d // N_DEV
    cv_small = jnp.concatenate([_pad_rows(cv_dw[0], CONV_HALO), cv_dw_b, cv_ln_g, cv_ln_b, cv_b_out,
                                cv_b_in.reshape(2, cvp), jnp.zeros((2, cvp), F32)], axis=0)
    cv_all = _exchange_small(cv_small, "gather_cv_small", False)
    cv_rows = jnp.transpose(cv_all, (1, 0, 2)).reshape(cv_small.shape[0], d)
    cv_dw_f, cv_dwb_f, cv_lng_f, cv_lnb_f, cv_bout_f = (cv_rows[:CONV_HALO], cv_rows[32:33], cv_rows[33:34], cv_rows[34:35], cv_rows[35:36])
    cv_bin_f = cv_all[:, 36:38, :].reshape(1, 2 * d)

    saved = []
    h = _modulate(xs, mods[0][1], mods[0][0], seq)
    xin = xs
    for l in range(nl):
        kind = l % 4
        sv = dict(x=xin, h=h)
        if kind == 0:
            pre = _proj_cols("gm_in", h, gathered["gm_w_in"], gm_w_in.shape[2], bias=gm_b_in)
            yv = _gm_spatial_fwd(pre, gm_ln_g, gm_ln_b, gm_w_s[0], jnp.transpose(gm_b_s[0]), seq)
            y = _mm_plain("gm_out", yv, sq("gm_w_out"), NN)
            sv.update(pre=pre, yv=yv)
        elif kind == 1:
            qkv = _qkv_proj("fox_qkv", h, fox_qkv_w)
            ft = _mm("fox_gate_proj", fox_f_wt, h, _sds((n_heads, t), F32), (t // min(TMM, t), 1),
                     pl.BlockSpec((n_heads, d), lambda i, r: (0, 0)), pl.BlockSpec((min(TMM, t), d), lambda i, r: (i, 0)),
                     pl.BlockSpec((n_heads, min(TMM, t)), lambda i, r: (0, i)), NT)
            b_f = jnp.transpose(fox_b_f)
            frow, fcol = _fox_gate_fwd(ft, b_f, seq)
            frow_p = frow.reshape(npair, 2, t)
            fcol_p = jnp.transpose(fcol.reshape(t, npair, 2), (1, 0, 2))
            o, lse = _fox_fwd(qkv, fcol_p, frow_p, nb, seq)
            y = _mm_plain("fox_out", o, sq("fox_w_out"), NN)
            sv.update(qkv=qkv, ft=ft, b_f=b_f, frow=frow_p, fcol=fcol_p, o=o, lse=lse)
        elif kind == 2:
            qkv = _qkv_proj("sb_qkv", h, sb_qkv_w)
            o, ltot = _sb_fwd(qkv, nb, seq)
            y = _mm_plain("sb_out", o, sq("sb_w_out"), NN)
            sv.update(qkv=qkv, o=o, ltot=ltot)
        else:
            pre = _proj_cols("cv_in", h, gathered["cv_w_in"], cv_w_in.shape[2], bias=cv_bin_f)
            y1, y2 = _cv_conv_fwd(pre, cv_dw_f, cv_dwb_f, cv_lng_f, cv_lnb_f, seq)
            y = _mm_plain("cv_out", y2, sq("cv_w_out"), NN, bias=cv_bout_f)
            sv.update(pre=pre, y1=y1, y2=y2)
        x1, h2 = _lnres_fwd(xin, y, mods[l][2], ln1_g[l:l + 1], ln1_b[l:l + 1], alpha, seq, nxt=(mods[l][4], mods[l][3]))
        hg, hu, act = _ffn_in(h2, w_ffn_in, l)
        y2f = _ffn_out(act, w_ffn_out_rows, l * (N_DEV // 2))
        sv.update(y=y, x1=x1, h2=h2, hg=hg, hu=hu, act=act, y2f=y2f)
        if l + 1 < nl:
            xin, h = _lnres_fwd(x1, y2f, mods[l][5], ln2_g[l:l + 1], ln2_b[l:l + 1], alpha, seq, nxt=(mods[l + 1][1], mods[l + 1][0]))
        else:
            xin = _lnres_fwd(x1, y2f, mods[l][5], ln2_g[l:l + 1], ln2_b[l:l + 1], alpha, seq)
        saved.append(sv)

    dx, sq_err = _loss_head(xin, tgt)
    loss = lax.psum(0.5 * jnp.sum(sq_err) / d, ("x", "y", "c"))

    small = {}
    bigg = {}
    dmods = [None] * nl
    d_ln = dict(ln1_g=[None] * nl, ln1_b=[None] * nl, ln2_g=[None] * nl, ln2_b=[None] * nl)
    ffn_gi, ffn_go = [None] * nl, [None] * nl
    for l in reversed(range(nl)):
        sv = saved[l]
        kind = l % 4
        dxr, dy2, dlg, dlb, _, dgate2 = _lnres_bwd(dx, sv["x1"], sv["y2f"], mods[l][5], ln2_g[l:l + 1], alpha, seq)
        d_ln["ln2_g"][l], d_ln["ln2_b"][l] = dlg, dlb
        dg_, du_ = _ffn_dact(dy2, w_ffn_out_rows, sv["hg"], sv["hu"], l * (N_DEV // 2))
        dwg, dwu, dwo = _ffn_bwd_weights(sv["h2"], dy2, sv["act"], dg_, du_)
        ffn_gi[l] = jnp.concatenate([dwg, dwu], axis=0)
        ffn_go[l] = dwo.reshape(N_DEV, n_ffn // 2, d)
        dh2 = _ffn_dh(dg_, du_, w_ffn_in, l)
        dx1, dsc2, dsh2 = _modulate_bwd(dh2, dxr, sv["x1"], mods[l][4], seq)
        dxr, dy, dlg, dlb, dysum, dgate1 = _lnres_bwd(dx1, sv["x"], sv["y"], mods[l][2], ln1_g[l:l + 1], alpha, seq)
        d_ln["ln1_g"][l], d_ln["ln1_b"][l] = dlg, dlb
        hh = sv["h"]
        if kind == 0:
            dyv = _mm_plain("gm_out_bwd", dy, sq("gm_w_out"), NT)
            bigg["gm_w_out"] = _mm_plain("gm_out_dw", sv["yv"], dy, TN).reshape(1, N_DEV, d // N_DEV, d)
            dpre, dws, dbst, dlng, dlnb, dbin = _gm_spatial_bwd(sv["pre"], dyv, gm_ln_g, gm_ln_b, gm_w_s[0], jnp.transpose(gm_b_s[0]), seq)
            small.update(gm_w_s=dws[None], gm_b_s=jnp.transpose(dbst)[None], gm_ln_g=dlng, gm_ln_b=dlnb, gm_b_in=dbin)
            bigg["gm_w_in"] = _grad_cols("gm_in_dw", hh, dpre, gm_w_in.shape[2])
            dh = _back_cols("gm_in_bwd", dpre, gathered["gm_w_in"], gm_w_in.shape[2])
        elif kind == 1:
            do = _mm_plain("fox_out_bwd", dy, sq("fox_w_out"), NT)
            bigg["fox_w_out"] = _mm_plain("fox_out_dw", sv["o"], dy, TN).reshape(1, N_DEV, d // N_DEV, d)
            dqkv, dfr, dfq = _fox_bwd(sv["qkv"], sv["fcol"], sv["frow"], sv["o"], do, sv["lse"], nb, seq)
            dft, dbf = _fox_gate_bwd(sv["ft"], sv["b_f"], dfr.reshape(n_heads, t),
                                     jnp.transpose(dfq, (0, 2, 1)).reshape(n_heads, t), seq)
            small["fox_b_f"] = jnp.transpose(dbf)
            dw_qkv = _qkv_dw("fox_qkv_dw", hh, dqkv)
            tk = min(TMM, t)
            dw_ft = _mm("fox_gate_dw", dft, hh, _sds((n_heads, d), F32), (1, t // tk),
                        pl.BlockSpec((n_heads, tk), lambda j, r: (0, r)), pl.BlockSpec((tk, d), lambda j, r: (r, 0)),
                        pl.BlockSpec((n_heads, d), lambda j, r: (0, 0)), NN)
            bigg["fox_w_in"] = _col_blocks(jnp.concatenate([dw_qkv, jnp.transpose(dw_ft)], axis=1), fox_w_in.shape[2])
            dh_a = _qkv_dh("fox_qkv_bwd", dqkv, fox_qkv_w)
            tm = min(TMM, t)
            dh = _mm("fox_gate_bwd_h", dft, fox_f_wt, _sds((t, d), F32), (t // tm, 1),
                     pl.BlockSpec((n_heads, tm), lambda i, r: (0, i)), pl.BlockSpec((n_heads, d), lambda i, r: (0, 0)),
                     pl.BlockSpec((tm, d), lambda i, r: (i, 0)), TN, (dh_a,), (pl.BlockSpec((tm, d), lambda i, r: (i, 0)),), _add)
        elif kind == 2:
            do = _mm_plain("sb_out_bwd", dy, sq("sb_w_out"), NT)
            bigg["sb_w_out"] = _mm_plain("sb_out_dw", sv["o"], dy, TN).reshape(1, N_DEV, d // N_DEV, d)
            dqkv = _sb_bwd(sv["qkv"], do, sv["ltot"], nb, seq)
            bigg["sb_w_in"] = _col_blocks(_qkv_dw("sb_qkv_dw", hh, dqkv), sb_w_in.shape[2])
            dh = _qkv_dh("sb_qkv_bwd", dqkv, sb_qkv_w)
        else:
            dy2c = _mm_plain("cv_out_bwd", dy, sq("cv_w_out"), NT)
            bigg["cv_w_out"] = _mm_plain("cv_out_dw", sv["y2"], dy, TN).reshape(1, N_DEV, d // N_DEV, d)
            dy1, dlng, dlnb, ddwb = _cv_norm_bwd(sv["y1"], dy2c, cv_lng_f, cv_lnb_f)
            dpre, ddw, dbin = _cv_conv_bwd(sv["pre"], dy1, cv_dw_f, seq)
            small.update(cv_b_out=dysum, cv_ln_g=dlng, cv_ln_b=dlnb, cv_dw_b=ddwb, cv_dw=ddw[:CONV_WIDTH], cv_b_in=dbin)
            bigg["cv_w_in"] = _grad_cols("cv_in_dw", hh, dpre, cv_w_in.shape[2])
            dh = _back_cols("cv_in_bwd", dpre, gathered["cv_w_in"], cv_w_in.shape[2])
        dx, dsc1, dsh1 = _modulate_bwd(dh, dxr, sv["x"], mods[l][1], seq)
        dmods[l] = jnp.concatenate([dsh1, dsc1, dgate1, dsh2, dsc2, dgate2], axis=2)[:, 0, :]
    grad_x = dx.reshape(nb, seq, d)
    bigg["ffn_w_in"] = jnp.stack(ffn_gi)
    bigg["ffn_w_out"] = jnp.stack(ffn_go)
    for n in d_ln:
        small[n] = jnp.concatenate(d_ln[n], axis=0)

    dmod_rows = jnp.stack(dmods).reshape(nl * nb, 6 * d)
    dmod_g = _exchange_small(_pad_rows(dmod_rows, 8 * ((nl * nb + 7) // 8)), "gather_dmod", False)[:, :nl * nb]
    dmod_all = jnp.transpose(dmod_g.reshape(N_DEV, nl, nb, 6 * d), (1, 0, 2, 3)).reshape(nl, N_DEV * nb, 6 * d)
    dmod_loc = lax.dynamic_slice_in_dim(dmod_all, me * n_mod, n_mod, axis=2)
    g_mod_w, g_mod_b = _mod_bwd(c_all, dmod_loc, dmod_all)
    grads = dict(mod_w=g_mod_w, mod_b=g_mod_b[:, 0, :])

    rep = ["ln1_g", "ln1_b", "ln2_g", "ln2_b", "gm_b_in", "gm_ln_g", "gm_ln_b", "gm_w_s", "gm_b_s", "fox_b_f"]
    cvs = ["cv_b_in", "cv_dw", "cv_dw_b", "cv_ln_g", "cv_ln_b", "cv_b_out"]

    def rows_of(a):
        flat = a.reshape(-1)
        pad = (-flat.shape[0]) % d
        return jnp.pad(flat, (0, pad)).reshape(-1, d)

    pack_rows = [rows_of(small[n]) for n in rep + cvs]
    counts = [r.shape[0] for r in pack_rows]
    total = sum(counts)
    pack = _pad_rows(jnp.concatenate(pack_rows, axis=0), 8 * ((total + 7) // 8))
    summed = _exchange_small(pack, "allreduce_small", True)
    offs = [sum(counts[:i]) for i in range(len(counts))]
    rep_rows = sum(counts[:len(rep)])
    for n, o_, cnt in zip(rep + cvs, offs, counts):
        full = summed[o_:o_ + cnt].reshape(-1)
        if n in rep:
            grads[n] = full[:weights[n].size].reshape(weights[n].shape)
        else:
            wshape = weights[n].shape
            cols = wshape[-1]
            full = full[:math.prod(wshape[:-1]) * cols * N_DEV].reshape(wshape[:-1] + (cols * N_DEV,))
            grads[n] = lax.dynamic_slice_in_dim(full, me * cols, cols, axis=full.ndim - 1)

    recv = dict(zip(big, _scatter_grads([bigg[n] for n in big])))

    outs = {}

    def view2(a):
        return a.reshape(-1, a.shape[-1])

    for n in big:
        w2 = view2(weights[n])
        res = _adamw("adamw_" + n, w2, view2(mom1[n]), view2(mom2[n]), parts=recv[n].reshape((N_DEV,) + w2.shape))
        outs[n] = [r.reshape(weights[n].shape) for r in res]
    res = _adamw("adamw_mod_w", view2(mod_w), view2(m_mod_w), view2(v_mod_w), g=view2(grads["mod_w"]))
    outs["mod_w"] = [r.reshape(mod_w.shape) for r in res]
    rp = lambda src: _pad_rows(jnp.concatenate([rows_of(src[n]) for n in rep], axis=0), 8 * ((rep_rows + 7) // 8))
    res = _adamw("adamw_replicated", rp(weights), rp(mom1), rp(mom2), g=rp(grads))
    for n, o_, cnt in zip(rep, offs, counts):
        outs[n] = [r[o_:o_ + cnt].reshape(-1)[:weights[n].size].reshape(weights[n].shape) for r in res]
    cv_cols = weights["cv_b_out"].shape[-1]
    cp = lambda src: jnp.concatenate([src[n].reshape(-1, cv_cols) for n in cvs], axis=0)
    cv_cnt = [weights[n].size // cv_cols for n in cvs]
    cv_tot = sum(cv_cnt)
    cpp = lambda src: _pad_rows(cp(src), 8 * ((cv_tot + 7) // 8))
    res = _adamw("adamw_cv_small", cpp(weights), cpp(mom1), cpp(mom2), g=cpp(grads))
    o_ = 0
    for n, cnt in zip(cvs, cv_cnt):
        outs[n] = [r[o_:o_ + cnt].reshape(weights[n].shape) for r in res]
        o_ += cnt
    res = _adamw("adamw_mod_b", mod_b, m_mod_b, v_mod_b, g=grads["mod_b"])
    outs["mod_b"] = list(res)

    return (loss, grad_x, *[outs[n][0] for n in names], *[outs[n][1] for n in names],
            *[outs[n][2] for n in names], *[outs[n][3] for n in names])
```

```python
import functools
import math

import jax
import jax.numpy as jnp
from jax import lax
from jax.experimental import pallas as pl
from jax.experimental.pallas import tpu as pltpu

F32 = jnp.float32
BF16 = jnp.bfloat16
MESH = pl.DeviceIdType.MESH

N_DEV = 8
HEAD_DIM = 64
LANES = 128
GM_CHUNK = 128
GM_GROUPS = 8
CONV_WIDTH = 31
CONV_HALO = 32
LN_EPS = 1e-5
NEG_INF = -1e30
SB_DEAD = -100.0
GRAD_WIRE = jnp.bfloat16

ADAM_LR = 0.001
ADAM_B1 = 0.9
ADAM_B2 = 0.999
ADAM_EPS = 1e-08
ADAM_WD = 0.01
ADAM_STEP = 10

TM = 512
TMM = 1024
TQ = 256
FOX_FWD_UNROLL = 4
FOX_BWD_UNROLL = 2
TT = 512
VMEM_BIG = 56 * 1024 * 1024

NN = (((1,), (0,)), ((), ()))
NT = (((1,), (1,)), ((), ()))
TN = (((0,), (0,)), ((), ()))


def _call(body, name, out_shape, grid=None, in_specs=None, out_specs=None, scratch=(), vmem=None):
    params = {}
    if grid is not None:
        params["dimension_semantics"] = ("arbitrary",) * len(grid)
    if vmem is not None:
        params["vmem_limit_bytes"] = vmem
    kw = {}
    if grid is not None:
        kw["grid"] = grid
    if in_specs is not None:
        kw["in_specs"] = in_specs
    if out_specs is not None:
        kw["out_specs"] = out_specs
    return pl.pallas_call(body, name=name, out_shape=out_shape, scratch_shapes=list(scratch),
                          compiler_params=pltpu.CompilerParams(**params), **kw)


def _sds(shape, dtype):
    return jax.ShapeDtypeStruct(tuple(shape), dtype)


def _dot(a, b, dims=NN):
    return lax.dot_general(a.astype(BF16), b.astype(BF16), dims, preferred_element_type=F32)


def _split3(x):
    h1 = x.astype(BF16)
    r1 = x - h1.astype(F32)
    h2 = r1.astype(BF16)
    h3 = (r1 - h2.astype(F32)).astype(BF16)
    return h1, h2, h3


def _dot_exact(x, m, dims=NN):
    h1, h2, h3 = _split3(x)
    d = lambda h: lax.dot_general(h, m, dims, preferred_element_type=F32)
    return (d(h1) + d(h2)) + d(h3)


def _dot_exact_rhs(m, x, dims=NN):
    h1, h2, h3 = _split3(x)
    d = lambda h: lax.dot_general(m, h, dims, preferred_element_type=F32)
    return (d(h1) + d(h2)) + d(h3)


def _silu(x):
    return x * jax.nn.sigmoid(x)


def _gelu(x):
    return 0.5 * x * (1.0 + lax.erf(x * (2.0 ** -0.5)))


def _gelu_grad(x):
    return 0.5 * (1.0 + lax.erf(x * (2.0 ** -0.5))) + x * jnp.exp(-0.5 * x * x) * ((2.0 * math.pi) ** -0.5)


def _log_sigmoid(z):
    return jnp.minimum(z, 0.0) - jnp.log(1.0 + jnp.exp(-jnp.abs(z)))


def _ln_stats(r):
    mu = jnp.mean(r, axis=-1, keepdims=True)
    rc = r - mu
    var = jnp.mean(rc * rc, axis=-1, keepdims=True)
    return rc, lax.rsqrt(var + LN_EPS)


def _colsum(x):
    return jnp.sum(x, axis=0, keepdims=True)


def _peers():
    mx, my, mc = lax.axis_index("x"), lax.axis_index("y"), lax.axis_index("c")
    me = 4 * mx + 2 * my + mc
    out = []
    for k in range(1, N_DEV):
        px = 1 - mx if (k >> 2) & 1 else mx
        py = 1 - my if (k >> 1) & 1 else my
        pc = 1 - mc if k & 1 else mc
        out.append(((px, py, pc), 4 * px + 2 * py + pc))
    return me, out


def _exchange_small(x, name, reduce):
    rows, cols = x.shape

    def body(x_ref, o_ref, *rest):
        if reduce:
            land, send_sems, recv_sems, local_sem = rest
        else:
            land = o_ref
            send_sems, recv_sems, local_sem = rest
        me, peers = _peers()
        mine = pltpu.make_async_copy(x_ref, land.at[me], local_sem)
        mine.start()
        sends = []
        for k, (peer, _) in enumerate(peers):
            cp = pltpu.make_async_remote_copy(src_ref=x_ref, dst_ref=land.at[me], send_sem=send_sems.at[k],
                                              recv_sem=recv_sems.at[k], device_id=peer, device_id_type=MESH)
            cp.start()
            sends.append(cp)
        for k, (peer, blk) in enumerate(peers):
            pltpu.make_async_remote_copy(src_ref=x_ref, dst_ref=land.at[blk], send_sem=send_sems.at[k],
                                         recv_sem=recv_sems.at[k], device_id=peer, device_id_type=MESH).wait_recv()
        for cp in sends:
            cp.wait_send()
        mine.wait()
        if reduce:
            acc = land[0]
            for s in range(1, N_DEV):
                acc = acc + land[s]
            o_ref[...] = acc

    vm = pl.BlockSpec(memory_space=pltpu.VMEM)
    scratch = [pltpu.SemaphoreType.DMA((N_DEV - 1,)), pltpu.SemaphoreType.DMA((N_DEV - 1,)), pltpu.SemaphoreType.DMA]
    if reduce:
        scratch = [pltpu.VMEM((N_DEV, rows, cols), F32)] + scratch
        out = _sds((rows, cols), F32)
    else:
        out = _sds((N_DEV, rows, cols), F32)
    return _call(body, name, out, in_specs=[vm], out_specs=vm, scratch=scratch, vmem=VMEM_BIG)(x)


def _gather_weights(shards):
    n = len(shards)

    def body(*refs):
        ins, outs = refs[:n], refs[n:2 * n]
        send_sems, recv_sems, local_sems = refs[2 * n:]
        mx, my, mc = lax.axis_index("x"), lax.axis_index("y"), lax.axis_index("c")
        sibling = (mx, my, 1 - mc)
        chips = [(1 - mx, my), (mx, 1 - my), (1 - mx, 1 - my)]

        def block(px, py, pc):
            return 4 * px + 2 * py + pc

        def copy(a, k, blk, to, src=None):
            dst = outs[a].at[:, blk]
            return pltpu.make_async_remote_copy(src_ref=dst if src is None else src, dst_ref=dst, send_sem=send_sems.at[a, k],
                                                recv_sem=recv_sems.at[a, k], device_id=to, device_id_type=MESH)

        me = block(mx, my, mc)
        local, sends = [], []
        for a in range(n):
            cp = pltpu.make_async_copy(ins[a], outs[a].at[:, me], local_sems.at[a])
            cp.start()
            local.append(cp)
            first = [copy(a, 0, me, sibling, src=ins[a])]
            first += [copy(a, 1 + j, me, (*chip, mc), src=ins[a]) for j, chip in enumerate(chips)]
            for cp in first:
                cp.start()
            sends += first
        for a in range(n):
            for j, chip in enumerate(chips):
                blk = block(*chip, mc)
                copy(a, 1 + j, blk, (mx, my, mc)).wait_recv()
                cp = copy(a, 4 + j, blk, sibling)
                cp.start()
                sends.append(cp)
        for a in range(n):
            copy(a, 0, block(mx, my, 1 - mc), (mx, my, mc)).wait_recv()
            for j, chip in enumerate(chips):
                copy(a, 4 + j, block(*chip, 1 - mc), (mx, my, mc)).wait_recv()
        for cp in sends:
            cp.wait_send()
        for cp in local:
            cp.wait()

    hbm = pl.BlockSpec(memory_space=pl.ANY)
    out = [_sds((s.shape[0], N_DEV) + s.shape[1:], s.dtype) for s in shards]
    scratch = [pltpu.SemaphoreType.DMA((n, N_DEV - 1)), pltpu.SemaphoreType.DMA((n, N_DEV - 1)), pltpu.SemaphoreType.DMA((n,))]
    return _call(body, "gather_weights", out, in_specs=[hbm] * n, out_specs=[hbm] * n, scratch=scratch)(*shards)


def _scatter_grads(grads):
    n = len(grads)

    def body(*refs):
        ins, outs = refs[:n], refs[n:2 * n]
        send_sems, recv_sems, local_sems = refs[2 * n:]
        me, peers = _peers()
        local = []
        sends = []
        for a in range(n):
            cp = pltpu.make_async_copy(ins[a].at[:, me], outs[a].at[me], local_sems.at[a])
            cp.start()
            local.append(cp)
            for k, (peer, blk) in enumerate(peers):
                cp = pltpu.make_async_remote_copy(src_ref=ins[a].at[:, blk], dst_ref=outs[a].at[me], send_sem=send_sems.at[a, k],
                                                  recv_sem=recv_sems.at[a, k], device_id=peer, device_id_type=MESH)
                cp.start()
                sends.append(cp)
        for a in range(n):
            for k, (peer, blk) in enumerate(peers):
                pltpu.make_async_remote_copy(src_ref=ins[a].at[:, me], dst_ref=outs[a].at[blk], send_sem=send_sems.at[a, k],
                                             recv_sem=recv_sems.at[a, k], device_id=peer, device_id_type=MESH).wait_recv()
        for cp in sends:
            cp.wait_send()
        for cp in local:
            cp.wait()

    hbm = pl.BlockSpec(memory_space=pl.ANY)
    out = [_sds((N_DEV, g.shape[0]) + g.shape[2:], g.dtype) for g in grads]
    scratch = [pltpu.SemaphoreType.DMA((n, N_DEV - 1)), pltpu.SemaphoreType.DMA((n, N_DEV - 1)), pltpu.SemaphoreType.DMA((n,))]
    return _call(body, "scatter_grads", out, in_specs=[hbm] * n, out_specs=[hbm] * n, scratch=scratch)(*grads)


def _mm(name, a, b, out, grid, a_spec, b_spec, o_spec, dims, extra=(), extra_specs=(), epilogue=None, vmem=VMEM_BIG):
    nred = grid[-1]
    red_axis = len(grid) - 1
    acc_shape = tuple(d for d in o_spec.block_shape if d is not None)
    n_extra = len(extra)

    def body(a_ref, b_ref, *rest):
        ex = rest[:n_extra]
        o_ref = rest[n_extra]

        def finish(acc):
            if epilogue is not None:
                acc = epilogue(acc, *[e[...] for e in ex])
            o_ref[...] = acc.astype(o_ref.dtype)

        prod = _dot(a_ref[...], b_ref[...], dims)
        if nred == 1:
            finish(prod)
        else:
            acc_ref = rest[n_extra + 1]
            r = pl.program_id(red_axis)

            @pl.when(r == 0)
            def _():
                acc_ref[...] = prod

            @pl.when(r > 0)
            def _():
                acc_ref[...] += prod

            @pl.when(r == nred - 1)
            def _():
                finish(acc_ref[...])

    scratch = [pltpu.VMEM(acc_shape, F32)] if nred > 1 else []
    return _call(body, name, out, grid=grid, in_specs=[a_spec, b_spec, *extra_specs], out_specs=o_spec,
                 scratch=scratch, vmem=vmem)(a, b, *extra)


def _add(acc, x):
    return acc + x


def _proj_cols(name, h, w, n_slot, bias=None, out_dtype=F32):
    t, k = h.shape
    s = w.shape[1]
    tm = min(TMM, t)
    extra, especs, epi = (), (), None
    if bias is not None:
        extra, especs, epi = (bias,), (pl.BlockSpec((1, n_slot), lambda j, i, r: (0, j)),), _add
    return _mm(name, h, w, _sds((t, s * n_slot), out_dtype), (s, t // tm, 1),
               pl.BlockSpec((tm, k), lambda j, i, r: (i, 0)),
               pl.BlockSpec((None, None, k, n_slot), lambda j, i, r: (0, j, 0, 0)),
               pl.BlockSpec((tm, n_slot), lambda j, i, r: (i, j)), NN, extra, especs, epi)


def _grad_cols(name, h, g, n_slot):
    t, k = h.shape
    s = g.shape[1] // n_slot
    tk = min(TMM, t)
    return _mm(name, h, g, _sds((1, s, k, n_slot), GRAD_WIRE), (s, t // tk),
               pl.BlockSpec((tk, k), lambda j, r: (r, 0)),
               pl.BlockSpec((tk, n_slot), lambda j, r: (r, j)),
               pl.BlockSpec((None, None, k, n_slot), lambda j, r: (0, j, 0, 0)), TN)


def _back_cols(name, g, w, n_slot, extra=(), extra_specs=(), epilogue=None):
    t = g.shape[0]
    s, k = w.shape[1], w.shape[2]
    tm = min(TMM, t)
    return _mm(name, g, w, _sds((t, k), F32), (t // tm, s),
               pl.BlockSpec((tm, n_slot), lambda i, r: (i, r)),
               pl.BlockSpec((None, None, k, n_slot), lambda i, r: (0, r, 0, 0)),
               pl.BlockSpec((tm, k), lambda i, r: (i, 0)), NT, extra, extra_specs, epilogue)


def _mm_plain(name, a, b, dims, out_dtype=F32, bias=None):
    if dims == TN:
        t, k = a.shape
        n = b.shape[1]
        tk = min(TMM, t)
        return _mm(name, a, b, _sds((k, n), out_dtype), (1, t // tk),
                   pl.BlockSpec((tk, k), lambda j, r: (r, 0)), pl.BlockSpec((tk, n), lambda j, r: (r, 0)),
                   pl.BlockSpec((k, n), lambda j, r: (0, 0)), TN)
    t = a.shape[0]
    tm = min(TMM, t)
    n = b.shape[1] if dims == NN else b.shape[0]
    extra, especs, epi = (), (), None
    if bias is not None:
        extra, especs, epi = (bias,), (pl.BlockSpec((1, n), lambda i, r: (0, 0)),), _add
    return _mm(name, a, b, _sds((t, n), out_dtype), (t // tm, 1),
               pl.BlockSpec((tm, a.shape[1]), lambda i, r: (i, 0)), pl.BlockSpec(b.shape, lambda i, r: (0, 0)),
               pl.BlockSpec((tm, n), lambda i, r: (i, 0)), dims, extra, especs, epi)


def _mod_fwd(c_all, mod_w, mod_b_loc):
    nl, d, n = mod_w.shape
    nb = c_all.shape[0]

    def body(c_ref, w_ref, b_ref, o_ref):
        o_ref[...] = _dot(_silu(c_ref[...]), w_ref[...]) + b_ref[...]

    return _call(body, "mod_fwd", _sds((nl, nb, n), F32), grid=(nl,),
                 in_specs=[pl.BlockSpec((nb, d), lambda l: (0, 0)), pl.BlockSpec((None, d, n), lambda l: (l, 0, 0)),
                           pl.BlockSpec((None, 1, n), lambda l: (l, 0, 0))],
                 out_specs=pl.BlockSpec((None, nb, n), lambda l: (l, 0, 0)))(c_all, mod_w, mod_b_loc)


def _mod_bwd(c_all, dmod_loc, dmod_all):
    nl, nb, n = dmod_loc.shape
    d = c_all.shape[1]
    n_all = dmod_all.shape[2]

    def body(c_ref, dl_ref, da_ref, gw_ref, gb_ref):
        gw_ref[...] = _dot(_silu(c_ref[...]), dl_ref[...], TN)
        gb_ref[...] = _colsum(da_ref[...])

    return _call(body, "mod_bwd", (_sds((nl, d, n), F32), _sds((nl, 1, n_all), F32)), grid=(nl,),
                 in_specs=[pl.BlockSpec((nb, d), lambda l: (0, 0)), pl.BlockSpec((None, nb, n), lambda l: (l, 0, 0)),
                           pl.BlockSpec((None, nb, n_all), lambda l: (l, 0, 0))],
                 out_specs=(pl.BlockSpec((None, d, n), lambda l: (l, 0, 0)), pl.BlockSpec((None, 1, n_all), lambda l: (l, 0, 0))),
                 )(c_all, dmod_loc, dmod_all)


def _row_spec(d, tpb):
    return pl.BlockSpec((None, 1, d), lambda i: (i // tpb, 0, 0))


def _tile_spec(tm, d):
    return pl.BlockSpec((tm, d), lambda i: (i, 0))


def _vec_spec(d):
    return pl.BlockSpec((1, d), lambda i: (0, 0))


def _modulate(x, sc, sh, seq):
    t, d = x.shape
    tm = min(TM, seq)
    tpb = seq // tm

    def body(x_ref, sc_ref, sh_ref, h_ref):
        h_ref[...] = (x_ref[...] * (1.0 + sc_ref[...]) + sh_ref[...]).astype(BF16)

    return _call(body, "modulate", _sds((t, d), BF16), grid=(t // tm,),
                 in_specs=[_tile_spec(tm, d), _row_spec(d, tpb), _row_spec(d, tpb)], out_specs=_tile_spec(tm, d))(x, sc, sh)


def _lnres_fwd(x, y, gate, lg, lb, alpha, seq, nxt=None):
    t, d = x.shape
    tm = min(TM, seq)
    tpb = seq // tm

    def body(x_ref, y_ref, g_ref, lg_ref, lb_ref, *rest):
        r = alpha * x_ref[...] + (1.0 + g_ref[...]) * y_ref[...]
        rc, rstd = _ln_stats(r)
        xn = rc * rstd * lg_ref[...] + lb_ref[...]
        if nxt is None:
            rest[0][...] = xn
        else:
            sc_ref, sh_ref, xo_ref, h_ref = rest
            xo_ref[...] = xn
            h_ref[...] = (xn * (1.0 + sc_ref[...]) + sh_ref[...]).astype(BF16)

    ins = [_tile_spec(tm, d), _tile_spec(tm, d), _row_spec(d, tpb), _vec_spec(d), _vec_spec(d)]
    if nxt is None:
        return _call(body, "lnres_fwd_last", _sds((t, d), F32), grid=(t // tm,), in_specs=ins,
                     out_specs=_tile_spec(tm, d))(x, y, gate, lg, lb)
    return _call(body, "lnres_fwd", (_sds((t, d), F32), _sds((t, d), BF16)), grid=(t // tm,),
                 in_specs=ins + [_row_spec(d, tpb), _row_spec(d, tpb)],
                 out_specs=(_tile_spec(tm, d), _tile_spec(tm, d)))(x, y, gate, lg, lb, *nxt)


def _loss_head(x, tgt):
    t, d = x.shape
    tm = min(TM, t)

    def body(x_ref, t_ref, dx_ref, sq_ref):
        e = x_ref[...] - t_ref[...]
        dx_ref[...] = e * (1.0 / d)

        @pl.when(pl.program_id(0) == 0)
        def _():
            sq_ref[...] = jnp.zeros_like(sq_ref)

        sq_ref[...] += _colsum(e * e)

    return _call(body, "loss_head", (_sds((t, d), F32), _sds((1, d), F32)), grid=(t // tm,),
                 in_specs=[_tile_spec(tm, d), _tile_spec(tm, d)], out_specs=(_tile_spec(tm, d), _vec_spec(d)))(x, tgt)


def _lnres_bwd(dxo, x, y, gate, lg, alpha, seq):
    t, d = x.shape
    tm = min(TM, seq)
    tpb = seq // tm
    nb = t // seq

    def body(dxo_ref, x_ref, y_ref, g_ref, lg_ref, dxr_ref, dy_ref, dlg_ref, dlb_ref, dys_ref, dg_ref):
        i = pl.program_id(0)
        yv = y_ref[...]
        r = alpha * x_ref[...] + (1.0 + g_ref[...]) * yv
        rc, rstd = _ln_stats(r)
        xhat = rc * rstd
        dxo_v = dxo_ref[...]
        dxh = dxo_v * lg_ref[...]
        m1 = jnp.mean(dxh, axis=-1, keepdims=True)
        m2 = jnp.mean(dxh * xhat, axis=-1, keepdims=True)
        dr = rstd * (dxh - m1 - xhat * m2)
        dyv = (1.0 + g_ref[...]) * dr
        dxr_ref[...] = alpha * dr
        dy_ref[...] = dyv.astype(BF16)

        @pl.when(i == 0)
        def _():
            dlg_ref[...] = jnp.zeros_like(dlg_ref)
            dlb_ref[...] = jnp.zeros_like(dlb_ref)
            dys_ref[...] = jnp.zeros_like(dys_ref)

        @pl.when(i % tpb == 0)
        def _():
            dg_ref[...] = jnp.zeros_like(dg_ref)

        dlg_ref[...] += _colsum(dxo_v * xhat)
        dlb_ref[...] += _colsum(dxo_v)
        dys_ref[...] += _colsum(dyv)
        dg_ref[...] += _colsum(dr * yv)

    return _call(body, "lnres_bwd",
                 (_sds((t, d), F32), _sds((t, d), BF16), _sds((1, d), F32), _sds((1, d), F32), _sds((1, d), F32), _sds((nb, 1, d), F32)),
                 grid=(t // tm,),
                 in_specs=[_tile_spec(tm, d), _tile_spec(tm, d), _tile_spec(tm, d), _row_spec(d, tpb), _vec_spec(d)],
                 out_specs=(_tile_spec(tm, d), _tile_spec(tm, d), _vec_spec(d), _vec_spec(d), _vec_spec(d), _row_spec(d, tpb)),
                 )(dxo, x, y, gate, lg)


def _modulate_bwd(dh, dxr, x, sc, seq):
    t, d = x.shape
    tm = min(TM, seq)
    tpb = seq // tm
    nb = t // seq

    def body(dh_ref, dxr_ref, x_ref, sc_ref, dx_ref, dsc_ref, dsh_ref):
        dhv = dh_ref[...]
        dx_ref[...] = dxr_ref[...] + dhv * (1.0 + sc_ref[...])

        @pl.when(pl.program_id(0) % tpb == 0)
        def _():
            dsc_ref[...] = jnp.zeros_like(dsc_ref)
            dsh_ref[...] = jnp.zeros_like(dsh_ref)

        dsc_ref[...] += _colsum(dhv * x_ref[...])
        dsh_ref[...] += _colsum(dhv)

    return _call(body, "modulate_bwd", (_sds((t, d), F32), _sds((nb, 1, d), F32), _sds((nb, 1, d), F32)), grid=(t // tm,),
                 in_specs=[_tile_spec(tm, d), _tile_spec(tm, d), _tile_spec(tm, d), _row_spec(d, tpb)],
                 out_specs=(_tile_spec(tm, d), _row_spec(d, tpb), _row_spec(d, tpb)))(dh, dxr, x, sc)


def _ffn_in(h, w_in, layer):
    t, d = h.shape
    n = w_in.shape[3]
    half = N_DEV // 2
    tm = min(TM, t)

    def body(h_ref, wg_ref, wu_ref, g_ref, u_ref, a_ref):
        hv = h_ref[...]
        g = _dot(hv, wg_ref[...])
        u = _dot(hv, wu_ref[...])
        g_ref[...] = g
        u_ref[...] = u
        a_ref[...] = (_silu(g) * u).astype(BF16)

    blk = pl.BlockSpec((None, tm, n), lambda p, i: (p, i, 0))
    return _call(body, "ffn_in", (_sds((half, t, n), F32), _sds((half, t, n), F32), _sds((half, t, n), BF16)),
                 grid=(half, t // tm),
                 in_specs=[pl.BlockSpec((tm, d), lambda p, i: (i, 0)),
                           pl.BlockSpec((None, None, d, n), lambda p, i: (layer, p, 0, 0)),
                           pl.BlockSpec((None, None, d, n), lambda p, i: (layer, p + half, 0, 0))],
                 out_specs=(blk, blk, blk), vmem=VMEM_BIG)(h, w_in, w_in)


def _ffn_out(act, w_out, layer):
    half, t, n = act.shape
    d = w_out.shape[2]
    tm = min(TMM, t)
    return _mm("ffn_out", act, w_out, _sds((t, d), F32), (t // tm, half),
               pl.BlockSpec((None, tm, n), lambda i, r: (r, i, 0)),
               pl.BlockSpec((None, n, d), lambda i, r: (layer + r, 0, 0)),
               pl.BlockSpec((tm, d), lambda i, r: (i, 0)), NN)


def _ffn_dact(dy, w_out, hg, hu, layer):
    half, t, n = hg.shape
    d = dy.shape[1]
    tm = min(TM, t)

    def body(dy_ref, w_ref, g_ref, u_ref, dg_ref, du_ref):
        da = _dot(dy_ref[...], w_ref[...], NT)
        g = g_ref[...]
        sg = jax.nn.sigmoid(g)
        dg_ref[...] = (da * u_ref[...] * (sg * (1.0 + g * (1.0 - sg)))).astype(BF16)
        du_ref[...] = (da * (g * sg)).astype(BF16)

    blk = pl.BlockSpec((None, tm, n), lambda p, i: (p, i, 0))
    dg, du = _call(body, "ffn_dact", (_sds((half, t, n), BF16), _sds((half, t, n), BF16)), grid=(half, t // tm),
                   in_specs=[pl.BlockSpec((tm, d), lambda p, i: (i, 0)),
                             pl.BlockSpec((None, n, d), lambda p, i: (layer + p, 0, 0)), blk, blk],
                   out_specs=(blk, blk), vmem=VMEM_BIG)(dy, w_out, hg, hu)
    return dg, du


def _ffn_bwd_weights(h, dy, act, dg, du):
    half, t, n = act.shape
    d = h.shape[1]
    tk = min(TMM, t)
    a_spec = pl.BlockSpec((tk, d), lambda j, r: (r, 0))
    g_spec = pl.BlockSpec((None, tk, n), lambda j, r: (j, r, 0))
    o_spec = pl.BlockSpec((None, d, n), lambda j, r: (j, 0, 0))
    dwg = _mm("ffn_dw_gate", h, dg, _sds((half, d, n), GRAD_WIRE), (half, t // tk), a_spec, g_spec, o_spec, TN)
    dwu = _mm("ffn_dw_up", h, du, _sds((half, d, n), GRAD_WIRE), (half, t // tk), a_spec, g_spec, o_spec, TN)
    dwo = _mm("ffn_dw_out", act, dy, _sds((half, n, d), GRAD_WIRE), (half, t // tk),
              pl.BlockSpec((None, tk, n), lambda j, r: (j, r, 0)), pl.BlockSpec((tk, d), lambda j, r: (r, 0)),
              pl.BlockSpec((None, n, d), lambda j, r: (j, 0, 0)), TN)
    return dwg, dwu, dwo


def _ffn_dh(dg, du, w_in, layer):
    half, t, n = dg.shape
    d = w_in.shape[2]
    tm = min(TMM, t)
    g_spec = pl.BlockSpec((None, tm, n), lambda i, r: (r, i, 0))
    o_spec = pl.BlockSpec((tm, d), lambda i, r: (i, 0))
    part = _mm("ffn_dh_gate", dg, w_in, _sds((t, d), F32), (t // tm, half), g_spec,
               pl.BlockSpec((None, None, d, n), lambda i, r: (layer, r, 0, 0)), o_spec, NT)
    return _mm("ffn_dh_up", du, w_in, _sds((t, d), F32), (t // tm, half), g_spec,
               pl.BlockSpec((None, None, d, n), lambda i, r: (layer, r + half, 0, 0)), o_spec, NT,
               (part,), (o_spec,), _add)


def _tril(n, strict=False):
    r = lax.broadcasted_iota(jnp.int32, (n, n), 0)
    c = lax.broadcasted_iota(jnp.int32, (n, n), 1)
    return c < r if strict else c <= r


def _gm_spatial_fwd(pre, lng, lnb, w_s, b_st, seq):
    t, w2 = pre.shape
    w = w2 // 2
    gd = w // GM_GROUPS
    tm = min(TM, seq)
    nch = tm // GM_CHUNK

    def body(pre_ref, lng_ref, lnb_ref, ws_ref, bs_ref, y_ref):
        v = _gelu(pre_ref[:, w:])
        vc, rstd = _ln_stats(v)
        vn = (vc * rstd * lng_ref[...] + lnb_ref[...]).astype(BF16)
        keep = _tril(GM_CHUNK)
        for g in range(GM_GROUPS):
            wm = jnp.where(keep, ws_ref[g], 0.0).astype(BF16)
            for ci in range(nch):
                rows = slice(ci * GM_CHUNK, (ci + 1) * GM_CHUNK)
                cols = slice(g * gd, (g + 1) * gd)
                sv = _dot(wm, vn[rows, cols]) + bs_ref[:, g:g + 1]
                u = _gelu(pre_ref[rows, cols])
                y_ref[rows, cols] = (u * sv).astype(BF16)

    return _call(body, "gm_spatial_fwd", _sds((t, w), BF16), grid=(t // tm,),
                 in_specs=[_tile_spec(tm, w2), _vec_spec(w), _vec_spec(w),
                           pl.BlockSpec((GM_GROUPS, GM_CHUNK, GM_CHUNK), lambda i: (0, 0, 0)),
                           pl.BlockSpec((GM_CHUNK, GM_GROUPS), lambda i: (0, 0))],
                 out_specs=_tile_spec(tm, w), vmem=VMEM_BIG)(pre, lng, lnb, w_s, b_st)


def _gm_spatial_bwd(pre, dyv, lng, lnb, w_s, b_st, seq):
    t, w2 = pre.shape
    w = w2 // 2
    gd = w // GM_GROUPS
    tm = min(TM, seq)
    nch = tm // GM_CHUNK

    def body(pre_ref, dyv_ref, lng_ref, lnb_ref, ws_ref, bs_ref, dpre_ref, dws_ref, dbs_ref, dlg_ref, dlb_ref, dbin_ref, dvn_ref):
        @pl.when(pl.program_id(0) == 0)
        def _():
            dws_ref[...] = jnp.zeros_like(dws_ref)
            dbs_ref[...] = jnp.zeros_like(dbs_ref)
            dlg_ref[...] = jnp.zeros_like(dlg_ref)
            dlb_ref[...] = jnp.zeros_like(dlb_ref)
            dbin_ref[...] = jnp.zeros_like(dbin_ref)

        pv = pre_ref[:, w:]
        v = _gelu(pv)
        vc, rstd = _ln_stats(v)
        vhat = vc * rstd
        vn = (vhat * lng_ref[...] + lnb_ref[...]).astype(BF16)
        keep = _tril(GM_CHUNK)
        dbs_cols = []
        for g in range(GM_GROUPS):
            wm = jnp.where(keep, ws_ref[g], 0.0).astype(BF16)
            dwm = jnp.zeros((GM_CHUNK, GM_CHUNK), F32)
            dbs = jnp.zeros((GM_CHUNK, 1), F32)
            for ci in range(nch):
                rows = slice(ci * GM_CHUNK, (ci + 1) * GM_CHUNK)
                cols = slice(g * gd, (g + 1) * gd)
                vn_b = vn[rows, cols]
                sv = _dot(wm, vn_b) + bs_ref[:, g:g + 1]
                pu = pre_ref[rows, cols]
                dy = dyv_ref[rows, cols]
                du = dy * sv
                dsv = dy * _gelu(pu)
                dpu = du * _gelu_grad(pu)
                dpre_ref[rows, cols] = dpu.astype(BF16)
                dbin_ref[:, cols] += _colsum(dpu)
                dsv_b = dsv.astype(BF16)
                dwm = dwm + _dot(dsv_b, vn_b, NT)
                dbs = dbs + jnp.sum(dsv, axis=-1, keepdims=True)
                dvn_ref[rows, cols] = _dot(wm, dsv_b, TN)
            dws_ref[g] += jnp.where(keep, dwm, 0.0)
            dbs_cols.append(dbs)
        dbs_ref[...] += jnp.concatenate(dbs_cols, axis=1)
        dvn = dvn_ref[...]
        dlg_ref[...] += _colsum(dvn * vhat)
        dlb_ref[...] += _colsum(dvn)
        dvh = dvn * lng_ref[...]
        m1 = jnp.mean(dvh, axis=-1, keepdims=True)
        m2 = jnp.mean(dvh * vhat, axis=-1, keepdims=True)
        dv = rstd * (dvh - m1 - vhat * m2)
        dpv = dv * _gelu_grad(pv)
        dpre_ref[:, w:] = dpv.astype(BF16)
        dbin_ref[:, w:] += _colsum(dpv)

    full3 = pl.BlockSpec((GM_GROUPS, GM_CHUNK, GM_CHUNK), lambda i: (0, 0, 0))
    bst = pl.BlockSpec((GM_CHUNK, GM_GROUPS), lambda i: (0, 0))
    return _call(body, "gm_spatial_bwd",
                 (_sds((t, w2), BF16), _sds((GM_GROUPS, GM_CHUNK, GM_CHUNK), F32), _sds((GM_CHUNK, GM_GROUPS), F32),
                  _sds((1, w), F32), _sds((1, w), F32), _sds((1, w2), F32)),
                 grid=(t // tm,),
                 in_specs=[_tile_spec(tm, w2), _tile_spec(tm, w), _vec_spec(w), _vec_spec(w), full3, bst],
                 out_specs=(_tile_spec(tm, w2), full3, bst, _vec_spec(w), _vec_spec(w), _vec_spec(w2)),
                 scratch=[pltpu.VMEM((tm, w), F32)], vmem=VMEM_BIG)(pre, dyv, lng, lnb, w_s, b_st)


def _head_masks():
    lane = lax.broadcasted_iota(jnp.int32, (1, LANES), 1)
    return lane < HEAD_DIM


def _two_heads(x, m0):
    z = jnp.zeros_like(x)
    return jnp.where(m0, x, z), jnp.where(m0, z, x)


def _qkv_specs(seq, nq, blocked_q):
    if blocked_q:
        q = pl.BlockSpec((None, TQ_(seq), LANES), lambda b, p, i: (0, b * nq + i, p))
        k = pl.BlockSpec((None, seq, LANES), lambda b, p, i: (1, b, p))
        v = pl.BlockSpec((None, seq, LANES), lambda b, p, i: (2, b, p))
    else:
        q = pl.BlockSpec((None, seq, LANES), lambda b, p: (0, b, p))
        k = pl.BlockSpec((None, seq, LANES), lambda b, p: (1, b, p))
        v = pl.BlockSpec((None, seq, LANES), lambda b, p: (2, b, p))
    return q, k, v


def TQ_(seq):
    return min(TQ, seq)


def _fox_gate_fwd(ft, b_f, seq):
    nh, t = ft.shape
    nch = seq // LANES

    def body(ft_ref, bf_ref, fr_ref):
        r = lax.broadcasted_iota(jnp.int32, (LANES, LANES), 0)
        c = lax.broadcasted_iota(jnp.int32, (LANES, LANES), 1)
        upper = jnp.where(r <= c, 1.0, 0.0).astype(BF16)
        carry = jnp.zeros((nh, 1), F32)
        for ci in range(nch):
            cols = slice(ci * LANES, (ci + 1) * LANES)
            lf = _log_sigmoid(ft_ref[:, cols] + bf_ref[...])
            cs = _dot_exact(lf, upper) + carry
            fr_ref[:, cols] = cs
            carry = cs[:, LANES - 1:LANES]

    return _call(body, "fox_gate_fwd", _sds((nh, t), F32), grid=(t // seq,),
                 in_specs=[pl.BlockSpec((nh, seq), lambda b: (0, b)), pl.BlockSpec((nh, 1), lambda b: (0, 0))],
                 out_specs=pl.BlockSpec((nh, seq), lambda b: (0, b)))(ft, b_f)


def _fox_gate_bwd(ft, b_f, dfk, dfq, seq):
    nh, t = ft.shape
    nch = seq // LANES

    def body(ft_ref, bf_ref, dfk_ref, dfq_ref, dl_ref, db_ref):
        @pl.when(pl.program_id(0) == 0)
        def _():
            db_ref[...] = jnp.zeros_like(db_ref)

        r = lax.broadcasted_iota(jnp.int32, (LANES, LANES), 0)
        c = lax.broadcasted_iota(jnp.int32, (LANES, LANES), 1)
        lower = jnp.where(r >= c, 1.0, 0.0).astype(BF16)
        carry = jnp.zeros((nh, 1), F32)
        tot = jnp.zeros((nh, 1), F32)
        for ci in reversed(range(nch)):
            cols = slice(ci * LANES, (ci + 1) * LANES)
            rc = _dot_exact(dfk_ref[:, cols] + dfq_ref[:, cols], lower) + carry
            carry = rc[:, 0:1]
            dl = rc * jax.nn.sigmoid(-(ft_ref[:, cols] + bf_ref[...]))
            dl_ref[:, cols] = dl
            tot = tot + jnp.sum(dl, axis=-1, keepdims=True)
        db_ref[...] += tot

    blk = pl.BlockSpec((nh, seq), lambda b: (0, b))
    one = pl.BlockSpec((nh, 1), lambda b: (0, 0))
    return _call(body, "fox_gate_bwd", (_sds((nh, t), F32), _sds((nh, 1), F32)), grid=(t // seq,),
                 in_specs=[blk, one, blk, blk], out_specs=(blk, one))(ft, b_f, dfk, dfq)


def _sweep(step, n_off, unroll, init):
    def group(_, st):
        base, carry = st[0], st[1:]
        for u in range(unroll):
            carry = step(base + u, carry, False)
        return (base + unroll, *carry)

    def tail(r):
        def run(st):
            base, carry = st[0], st[1:]
            for u in range(r):
                carry = step(base + u, carry, False)
            return step(base + r, carry, True)
        return run

    def pick(idx, fns, st):
        if len(fns) == 1:
            return fns[0](st)
        half = len(fns) // 2
        return lax.cond(idx < half, lambda s: pick(idx, fns[:half], s), lambda s: pick(idx - half, fns[half:], s), st)

    st = lax.fori_loop(0, n_off // unroll, group, (jnp.int32(0), *init))
    return pick(n_off % unroll, [tail(r) for r in range(unroll)], st)


def _fox_fwd(qkv, frow, nb, seq):
    _, t, d = qkv.shape
    npair = d // LANES
    tq = TQ_(seq)
    nq = seq // tq
    scale = HEAD_DIM ** -0.5

    def body(q_ref, k_ref, v_ref, fr_ref, o_ref, lse_ref):
        m0 = _head_masks()
        qm = _two_heads(q_ref[...] * scale, m0)
        row = lax.broadcasted_iota(jnp.int32, (tq, tq), 0)
        col = lax.broadcasted_iota(jnp.int32, (tq, tq), 1)
        one = jnp.ones((tq, LANES), BF16)

        def step(j, carry, diag):
            off = pl.multiple_of(j * tq, tq)
            kb = k_ref[pl.ds(off, tq), :]
            vb = v_ref[pl.ds(off, tq), :]
            vv = (jnp.where(m0, vb, one), jnp.where(m0, one, vb))
            out = []
            for hh in range(2):
                m, acc = carry[2 * hh], carry[2 * hh + 1]
                s = lax.dot_general(qm[hh], kb, NT, preferred_element_type=F32) - fr_ref[hh:hh + 1, pl.ds(off, tq)]
                if diag:
                    s = jnp.where(col <= row, s, NEG_INF)
                mn = jnp.maximum(m, jnp.max(s, axis=-1, keepdims=True))
                p = jnp.exp(s - mn)
                out += [mn, jnp.exp(m - mn) * acc + jnp.dot(p.astype(BF16), vv[hh], preferred_element_type=F32)]
            return tuple(out)

        neg = jnp.full((tq, 1), NEG_INF, F32)
        zacc = jnp.zeros((tq, LANES), F32)
        m_a, acc_a, m_b, acc_b = _sweep(step, pl.program_id(2), FOX_FWD_UNROLL, (neg, zacc, neg, zacc))
        l_a = pltpu.roll(acc_a, HEAD_DIM, 1)
        l_b = pltpu.roll(acc_b, HEAD_DIM, 1)
        o_ref[...] = jnp.where(m0, acc_a / l_a, acc_b / l_b)
        lse_ref[:, 0:1] = m_a + jnp.log(l_a[:, 0:1])
        lse_ref[:, 1:2] = m_b + jnp.log(l_b[:, HEAD_DIM:HEAD_DIM + 1])

    q_spec, k_spec, v_spec = _qkv_specs(seq, nq, True)
    col_spec = pl.BlockSpec((None, tq, 2), lambda b, p, i: (p, b * nq + i, 0))
    return _call(body, "fox_fwd", (_sds((t, d), F32), _sds((npair, t, 2), F32)), grid=(nb, npair, nq),
                 in_specs=[q_spec, k_spec, v_spec, pl.BlockSpec((None, 2, seq), lambda b, p, i: (p, 0, b))],
                 out_specs=(pl.BlockSpec((tq, LANES), lambda b, p, i: (b * nq + i, p)), col_spec),
                 vmem=VMEM_BIG)(qkv, qkv, qkv, frow)


def _fox_bwd(qkv, frow, o, do, lse, nb, seq):
    _, t, d = qkv.shape
    npair = d // LANES
    tq = TQ_(seq)
    nq = seq // tq
    scale = HEAD_DIM ** -0.5

    def body(q_ref, k_ref, v_ref, fr_ref, o_ref, do_ref, lse_ref, dqkv_ref, df_ref, dfq_ref, dk_acc, dv_acc):
        m0 = _head_masks()
        row = lax.broadcasted_iota(jnp.int32, (tq, tq), 0)
        col = lax.broadcasted_iota(jnp.int32, (tq, tq), 1)
        dk_acc[...] = jnp.zeros_like(dk_acc)
        dv_acc[...] = jnp.zeros_like(dv_acc)
        df_ref[...] = jnp.zeros_like(df_ref)

        def q_block(i, _):
            qoff = pl.multiple_of(i * tq, tq)
            qrows = pl.ds(qoff, tq)
            qm = _two_heads(q_ref[qrows, :] * scale, m0)
            dov = do_ref[qrows, :]
            dd = dov * o_ref[qrows, :]
            dm = _two_heads(dov.astype(BF16), m0)
            delta = (jnp.sum(jnp.where(m0, dd, 0.0), axis=-1, keepdims=True),
                     jnp.sum(jnp.where(m0, 0.0, dd), axis=-1, keepdims=True))
            ls = (lse_ref[qrows, 0:1], lse_ref[qrows, 1:2])

            def step(j, carry, diag):
                off = pl.multiple_of(j * tq, tq)
                krows = pl.ds(off, tq)
                kb = k_ref[krows, :]
                vb = v_ref[krows, :]
                dqs, rowsums = [], []
                dk = jnp.zeros((tq, LANES), F32)
                dv = jnp.zeros((tq, LANES), F32)
                for hh in range(2):
                    s = lax.dot_general(qm[hh], kb, NT, preferred_element_type=F32) - fr_ref[hh:hh + 1, krows]
                    if diag:
                        s = jnp.where(col <= row, s, NEG_INF)
                    p = jnp.exp(s - ls[hh])
                    dp = lax.dot_general(dm[hh], vb, NT, preferred_element_type=F32)
                    ds = p * (dp - delta[hh])
                    df_ref[hh:hh + 1, krows] -= _colsum(ds)
                    rowsums.append(carry[1 + hh] + jnp.sum(ds, axis=-1, keepdims=True))
                    ds_b = ds.astype(BF16)
                    dqs.append(jnp.dot(ds_b, kb, preferred_element_type=F32))
                    dk = dk + lax.dot_general(ds_b, qm[hh], TN, preferred_element_type=F32)
                    dv = dv + lax.dot_general(p.astype(BF16), dm[hh], TN, preferred_element_type=F32)
                dk_acc[krows, :] += dk
                dv_acc[krows, :] += dv
                return (carry[0] + jnp.where(m0, dqs[0], dqs[1]), *rowsums)

            zero = jnp.zeros((tq, 1), F32)
            dq, rs_a, rs_b = _sweep(step, i, FOX_BWD_UNROLL, (jnp.zeros((tq, LANES), F32), zero, zero))
            dqkv_ref[0, qrows, :] = (dq * scale).astype(BF16)
            dfq_ref[qrows, 0:1] = rs_a
            dfq_ref[qrows, 1:2] = rs_b
            return 0

        lax.fori_loop(0, nq, q_block, 0)
        dqkv_ref[1] = dk_acc[...].astype(BF16)
        dqkv_ref[2] = dv_acc[...].astype(BF16)

    q_spec, k_spec, v_spec = _qkv_specs(seq, nq, False)
    col_spec = pl.BlockSpec((None, seq, 2), lambda b, p: (p, b, 0))
    row_spec = pl.BlockSpec((None, 2, seq), lambda b, p: (p, 0, b))
    tile = pl.BlockSpec((seq, LANES), lambda b, p: (b, p))
    return _call(body, "fox_bwd", (_sds((3, t, d), BF16), _sds((npair, 2, t), F32), _sds((npair, t, 2), F32)), grid=(nb, npair),
                 in_specs=[q_spec, k_spec, v_spec, row_spec, tile, tile, col_spec],
                 out_specs=(pl.BlockSpec((3, seq, LANES), lambda b, p: (0, b, p)), row_spec, col_spec),
                 scratch=[pltpu.VMEM((seq, LANES), F32), pltpu.VMEM((seq, LANES), F32)],
                 vmem=VMEM_BIG)(qkv, qkv, qkv, frow, o, do, lse)


def _split2(x):
    hi = x.astype(BF16)
    return hi, (x - hi.astype(F32)).astype(BF16)


def _sum_right(x, tri):
    hi, lo = _split2(x)
    return jnp.dot(hi, tri, preferred_element_type=F32) + jnp.dot(lo, tri, preferred_element_type=F32)


def _sb_scores(qm_h, kb, scale, mask):
    z = lax.dot_general(qm_h, kb, NT, preferred_element_type=F32) * scale
    lb = _log_sigmoid(z)
    l1m = lb - z
    if mask is not None:
        l1m = jnp.where(mask, l1m, 0.0)
    return lb, l1m


def _sb_fwd(qkv, nb, seq):
    _, t, d = qkv.shape
    npair = d // LANES
    tq = TQ_(seq)
    nq = seq // tq
    scale = HEAD_DIM ** -0.5

    def body(q_ref, k_ref, v_ref, o_ref, lt_ref):
        i = pl.program_id(2)
        m0 = _head_masks()
        qm = _two_heads(q_ref[...], m0)
        row = lax.broadcasted_iota(jnp.int32, (tq, tq), 0)
        col = lax.broadcasted_iota(jnp.int32, (tq, tq), 1)
        after = jnp.where(row > col, 1.0, 0.0).astype(BF16)

        def step(j, carry, diag):
            off = pl.multiple_of(j * tq, tq)
            kb = k_ref[pl.ds(off, tq), :]
            vb = v_ref[pl.ds(off, tq), :]
            mask = (col < row) if diag else None
            nxt, parts = [], []
            for hh in range(2):
                lb, l1m = _sb_scores(qm[hh], kb, scale, mask)
                rest = _sum_right(l1m, after) + carry[hh]
                a = jnp.exp(lb + rest)
                if diag:
                    a = jnp.where(mask, a, 0.0)
                parts.append(jnp.dot(a.astype(BF16), vb, preferred_element_type=F32))
                nxt.append(carry[hh] + jnp.sum(l1m, axis=-1, keepdims=True))
            return (*nxt, carry[2] + jnp.where(m0, parts[0], parts[1]))

        zero = jnp.zeros((tq, 1), F32)
        carry = step(i, (zero, zero, jnp.zeros((tq, LANES), F32)), True)

        def alive(st):
            return (st[0] < i) & (jnp.max(jnp.maximum(st[1], st[2])) > SB_DEAD)

        def more(st):
            return (st[0] + 1, *step(i - 1 - st[0], st[1:], False))

        done, lt_a, lt_b, acc = lax.while_loop(alive, more, (jnp.int32(0), *carry))
        o_ref[...] = acc
        lt_ref[:, 0:1] = lt_a
        lt_ref[:, 1:2] = lt_b
        lt_ref[:, 2:3] = jnp.zeros((tq, 1), F32) + done.astype(F32)
        lt_ref[:, 3:4] = zero

    q_spec, k_spec, v_spec = _qkv_specs(seq, nq, True)
    return _call(body, "sb_fwd", (_sds((t, d), F32), _sds((npair, t, 4), F32)), grid=(nb, npair, nq),
                 in_specs=[q_spec, k_spec, v_spec],
                 out_specs=(pl.BlockSpec((tq, LANES), lambda b, p, i: (b * nq + i, p)),
                            pl.BlockSpec((None, tq, 4), lambda b, p, i: (p, b * nq + i, 0))), vmem=VMEM_BIG)(qkv, qkv, qkv)


def _sb_bwd(qkv, do, ltot, nb, seq):
    _, t, d = qkv.shape
    npair = d // LANES
    tq = TQ_(seq)
    nq = seq // tq
    scale = HEAD_DIM ** -0.5

    def body(q_ref, k_ref, v_ref, do_ref, lt_ref, dqkv_ref, dk_acc, dv_acc):
        m0 = _head_masks()
        row = lax.broadcasted_iota(jnp.int32, (tq, tq), 0)
        col = lax.broadcasted_iota(jnp.int32, (tq, tq), 1)
        upto = jnp.where(row <= col, 1.0, 0.0).astype(BF16)
        left_of = jnp.where(row < col, 1.0, 0.0).astype(BF16)
        dk_acc[...] = jnp.zeros_like(dk_acc)
        dv_acc[...] = jnp.zeros_like(dv_acc)

        def q_block(i, _):
            qoff = pl.multiple_of(i * tq, tq)
            qrows = pl.ds(qoff, tq)
            qm = _two_heads(q_ref[qrows, :], m0)
            dm = _two_heads(do_ref[qrows, :].astype(BF16), m0)
            ltot = (lt_ref[qrows, 0:1], lt_ref[qrows, 1:2])

            def step(j, carry, diag):
                off = pl.multiple_of(j * tq, tq)
                krows = pl.ds(off, tq)
                kb = k_ref[krows, :]
                vb = v_ref[krows, :]
                mask = (col < row) if diag else None
                nxt, dqs = [], []
                dk = jnp.zeros((tq, LANES), F32)
                dv = jnp.zeros((tq, LANES), F32)
                for hh in range(2):
                    cl, ce = carry[2 * hh], carry[2 * hh + 1]
                    lb, l1m = _sb_scores(qm[hh], kb, scale, mask)
                    a = jnp.exp(lb + (ltot[hh] - (_sum_right(l1m, upto) + cl)))
                    if diag:
                        a = jnp.where(mask, a, 0.0)
                    e = lax.dot_general(dm[hh], vb, NT, preferred_element_type=F32) * a
                    before = _sum_right(e, left_of) + ce
                    beta = jnp.exp(lb)
                    dz = e * (1.0 - beta) - before * beta
                    if diag:
                        dz = jnp.where(mask, dz, 0.0)
                    dz_b = dz.astype(BF16)
                    dqs.append(jnp.dot(dz_b, kb, preferred_element_type=F32))
                    dk = dk + lax.dot_general(dz_b, qm[hh], TN, preferred_element_type=F32)
                    dv = dv + lax.dot_general(a.astype(BF16), dm[hh], TN, preferred_element_type=F32)
                    nxt += [cl + jnp.sum(l1m, axis=-1, keepdims=True), ce + jnp.sum(e, axis=-1, keepdims=True)]
                dk_acc[krows, :] += dk
                dv_acc[krows, :] += dv
                return (*nxt, carry[4] + jnp.where(m0, dqs[0], dqs[1]))

            zero = jnp.zeros((tq, 1), F32)
            carry = (zero, zero, zero, zero, jnp.zeros((tq, LANES), F32))
            first = i - jnp.max(lt_ref[qrows, 2:3]).astype(jnp.int32)
            carry = lax.fori_loop(first, i, lambda j, c: step(j, c, False), carry)
            carry = step(i, carry, True)
            dqkv_ref[0, qrows, :] = (carry[4] * scale).astype(BF16)
            return 0

        lax.fori_loop(0, nq, q_block, 0)
        dqkv_ref[1] = (dk_acc[...] * scale).astype(BF16)
        dqkv_ref[2] = dv_acc[...].astype(BF16)

    q_spec, k_spec, v_spec = _qkv_specs(seq, nq, False)
    tile = pl.BlockSpec((seq, LANES), lambda b, p: (b, p))
    return _call(body, "sb_bwd", _sds((3, t, d), BF16), grid=(nb, npair),
                 in_specs=[q_spec, k_spec, v_spec, tile, pl.BlockSpec((None, seq, 4), lambda b, p: (p, b, 0))],
                 out_specs=pl.BlockSpec((3, seq, LANES), lambda b, p: (0, b, p)),
                 scratch=[pltpu.VMEM((seq, LANES), F32), pltpu.VMEM((seq, LANES), F32)], vmem=VMEM_BIG)(qkv, qkv, qkv, do, ltot)


def _qkv_proj(name, h, w):
    t, d = h.shape
    tm = min(TMM, t)
    return _mm(name, h, w, _sds((3, t, d), BF16), (3, t // tm, 1),
               pl.BlockSpec((tm, d), lambda s, i, r: (i, 0)), pl.BlockSpec((d, d), lambda s, i, r: (0, s)),
               pl.BlockSpec((None, tm, d), lambda s, i, r: (s, i, 0)), NN)


def _qkv_dw(name, h, dqkv):
    t, d = h.shape
    tk = min(TMM, t)
    return _mm(name, h, dqkv, _sds((d, 3 * d), GRAD_WIRE), (3, t // tk),
               pl.BlockSpec((tk, d), lambda s, r: (r, 0)), pl.BlockSpec((None, tk, d), lambda s, r: (s, r, 0)),
               pl.BlockSpec((d, d), lambda s, r: (0, s)), TN)


def _qkv_dh(name, dqkv, w, extra=(), extra_specs=(), epilogue=None):
    _, t, d = dqkv.shape
    tm = min(TMM, t)
    return _mm(name, dqkv, w, _sds((t, d), F32), (t // tm, 3),
               pl.BlockSpec((None, tm, d), lambda i, r: (r, i, 0)), pl.BlockSpec((d, d), lambda i, r: (0, r)),
               pl.BlockSpec((tm, d), lambda i, r: (i, 0)), NT, extra, extra_specs, epilogue)


def _glu(pre_block, d):
    return pre_block[:, :d] * jax.nn.sigmoid(pre_block[:, d:])


def _cv_conv_fwd(pre, dw, dwb, lng, lnb, seq):
    t, d2 = pre.shape
    d = d2 // 2
    tt = min(TT, seq)
    nt = seq // tt
    hb = tt // CONV_HALO

    def body(pre_ref, halo_ref, dw_ref, dwb_ref, lng_ref, lnb_ref, y1_ref, y2_ref, ext_ref):
        i = pl.program_id(1)
        ext_ref[0:CONV_HALO, :] = jnp.where(i == 0, 0.0, _glu(halo_ref[...], d))
        ext_ref[CONV_HALO:, :] = _glu(pre_ref[...], d)
        acc = jnp.zeros((tt, d), F32) + dwb_ref[...]
        for k in range(CONV_WIDTH):
            acc = acc + ext_ref[pl.ds(CONV_HALO - (CONV_WIDTH - 1) + k, tt), :] * dw_ref[k:k + 1, :]
        y1_ref[...] = acc
        yc, rstd = _ln_stats(acc)
        y2_ref[...] = _silu(yc * rstd * lng_ref[...] + lnb_ref[...]).astype(BF16)

    vec = pl.BlockSpec((1, d), lambda b, i: (0, 0))
    tile = pl.BlockSpec((tt, d), lambda b, i: (b * nt + i, 0))
    return _call(body, "cv_conv_fwd", (_sds((t, d), F32), _sds((t, d), BF16)), grid=(t // seq, nt),
                 in_specs=[pl.BlockSpec((tt, d2), lambda b, i: (b * nt + i, 0)),
                           pl.BlockSpec((CONV_HALO, d2), lambda b, i: (jnp.maximum((b * nt + i) * hb - 1, 0), 0)),
                           pl.BlockSpec((CONV_HALO, d), lambda b, i: (0, 0)), vec, vec, vec],
                 out_specs=(tile, tile), scratch=[pltpu.VMEM((tt + CONV_HALO, d), F32)], vmem=VMEM_BIG)(pre, pre, dw, dwb, lng, lnb)


def _cv_norm_bwd(y1, dy2, lng, lnb):
    t, d = y1.shape
    tm = min(TM, t)

    def body(y1_ref, dy2_ref, lng_ref, lnb_ref, dy1_ref, dlg_ref, dlb_ref, dsum_ref):
        @pl.when(pl.program_id(0) == 0)
        def _():
            dlg_ref[...] = jnp.zeros_like(dlg_ref)
            dlb_ref[...] = jnp.zeros_like(dlb_ref)
            dsum_ref[...] = jnp.zeros_like(dsum_ref)

        yc, rstd = _ln_stats(y1_ref[...])
        yhat = yc * rstd
        n = yhat * lng_ref[...] + lnb_ref[...]
        sg = jax.nn.sigmoid(n)
        dn = dy2_ref[...] * (sg * (1.0 + n * (1.0 - sg)))
        dlg_ref[...] += _colsum(dn * yhat)
        dlb_ref[...] += _colsum(dn)
        dyh = dn * lng_ref[...]
        m1 = jnp.mean(dyh, axis=-1, keepdims=True)
        m2 = jnp.mean(dyh * yhat, axis=-1, keepdims=True)
        dy1 = rstd * (dyh - m1 - yhat * m2)
        dy1_ref[...] = dy1
        dsum_ref[...] += _colsum(dy1)

    return _call(body, "cv_norm_bwd", (_sds((t, d), F32), _sds((1, d), F32), _sds((1, d), F32), _sds((1, d), F32)),
                 grid=(t // tm,), in_specs=[_tile_spec(tm, d), _tile_spec(tm, d), _vec_spec(d), _vec_spec(d)],
                 out_specs=(_tile_spec(tm, d), _vec_spec(d), _vec_spec(d), _vec_spec(d)))(y1, dy2, lng, lnb)


def _cv_conv_bwd(pre, dy1, dw, seq):
    t, d2 = pre.shape
    d = d2 // 2
    tt = min(TT, seq)
    nt = seq // tt
    hb = tt // CONV_HALO
    last_halo = t // CONV_HALO - 1

    def body(pre_ref, halo_ref, dy_ref, dyn_ref, dw_ref, dpre_ref, ddw_ref, dbin_ref, ext_ref, dext_ref):
        b, i = pl.program_id(0), pl.program_id(1)

        @pl.when((b == 0) & (i == 0))
        def _():
            ddw_ref[...] = jnp.zeros_like(ddw_ref)
            dbin_ref[...] = jnp.zeros_like(dbin_ref)

        pv = pre_ref[...]
        ext_ref[0:CONV_HALO, :] = jnp.where(i == 0, 0.0, _glu(halo_ref[...], d))
        ext_ref[CONV_HALO:, :] = _glu(pv, d)
        dyv = dy_ref[...]
        dext_ref[0:tt, :] = dyv
        dext_ref[tt:, :] = jnp.where(i == nt - 1, 0.0, dyn_ref[...])
        dy0 = jnp.zeros((tt, d), F32)
        for k in range(CONV_WIDTH):
            ddw_ref[k:k + 1, :] += _colsum(dyv * ext_ref[pl.ds(CONV_HALO - (CONV_WIDTH - 1) + k, tt), :])
            dy0 = dy0 + dext_ref[pl.ds(CONV_WIDTH - 1 - k, tt), :] * dw_ref[k:k + 1, :]
        a = pv[:, :d]
        sg = jax.nn.sigmoid(pv[:, d:])
        da = dy0 * sg
        dg = dy0 * a * sg * (1.0 - sg)
        dpre_ref[:, :d] = da.astype(BF16)
        dpre_ref[:, d:] = dg.astype(BF16)
        dbin_ref[:, :d] += _colsum(da)
        dbin_ref[:, d:] += _colsum(dg)

    return _call(body, "cv_conv_bwd", (_sds((t, d2), BF16), _sds((CONV_HALO, d), F32), _sds((1, d2), F32)), grid=(t // seq, nt),
                 in_specs=[pl.BlockSpec((tt, d2), lambda b, i: (b * nt + i, 0)),
                           pl.BlockSpec((CONV_HALO, d2), lambda b, i: (jnp.maximum((b * nt + i) * hb - 1, 0), 0)),
                           pl.BlockSpec((tt, d), lambda b, i: (b * nt + i, 0)),
                           pl.BlockSpec((CONV_HALO, d), lambda b, i: (jnp.minimum((b * nt + i + 1) * hb, last_halo), 0)),
                           pl.BlockSpec((CONV_HALO, d), lambda b, i: (0, 0))],
                 out_specs=(pl.BlockSpec((tt, d2), lambda b, i: (b * nt + i, 0)),
                            pl.BlockSpec((CONV_HALO, d), lambda b, i: (0, 0)), pl.BlockSpec((1, d2), lambda b, i: (0, 0))),
                 scratch=[pltpu.VMEM((tt + CONV_HALO, d), F32), pltpu.VMEM((tt + CONV_HALO, d), F32)],
                 vmem=VMEM_BIG)(pre, pre, dy1, dy1, dw)


def _adamw(name, w, m, v, g=None, parts=None):
    rows, cols = w.shape
    tr = rows
    for cand in ((512,) if parts is None else ()) + (256, 128, 64, 32, 16, 8):
        if rows % cand == 0 and rows > cand:
            tr = cand
            break
    bc1 = 1.0 - ADAM_B1 ** ADAM_STEP
    bc2 = 1.0 - ADAM_B2 ** ADAM_STEP

    def body(w_ref, m_ref, v_ref, g_ref, go_ref, d_ref, mo_ref, vo_ref):
        if parts is None:
            gv = g_ref[...]
        else:
            gv = g_ref[0].astype(F32)
            for s in range(1, N_DEV):
                gv = gv + g_ref[s].astype(F32)
        mn = ADAM_B1 * m_ref[...] + (1.0 - ADAM_B1) * gv
        vn = ADAM_B2 * v_ref[...] + (1.0 - ADAM_B2) * (gv * gv)
        m_hat = mn / bc1
        v_hat = vn / bc2
        go_ref[...] = gv
        d_ref[...] = -ADAM_LR * (m_hat / (jnp.sqrt(v_hat) + ADAM_EPS) + ADAM_WD * w_ref[...])
        mo_ref[...] = mn
        vo_ref[...] = vn

    blk = pl.BlockSpec((tr, cols), lambda i: (i, 0))
    g_in, g_spec = (g, blk) if parts is None else (parts, pl.BlockSpec((N_DEV, tr, cols), lambda i: (0, i, 0)))
    out = _sds((rows, cols), F32)
    return _call(body, name, (out, out, out, out), grid=(rows // tr,), in_specs=[blk, blk, blk, g_spec],
                 out_specs=(blk, blk, blk, blk), vmem=VMEM_BIG)(w, m, v, g_in)


def _pad_rows(a, rows):
    return jnp.pad(a, ((0, rows - a.shape[0]), (0, 0)))


def _full_cols(gathered, n):
    k = gathered.shape[2]
    return jnp.transpose(gathered[0], (1, 0, 2)).reshape(k, N_DEV * n)


def _col_blocks(full, n):
    k = full.shape[0]
    return jnp.transpose(full.reshape(k, N_DEV, n), (1, 0, 2))[None]


def kernel(x, c, mod_w, mod_b, ln1_g, ln1_b, ln2_g, ln2_b, ffn_w_in, ffn_w_out, gm_w_in, gm_b_in, gm_ln_g, gm_ln_b, gm_w_s, gm_b_s, gm_w_out, fox_w_in, fox_b_f, fox_w_out, sb_w_in, sb_w_out, cv_w_in, cv_b_in, cv_dw, cv_dw_b, cv_ln_g, cv_ln_b, cv_w_out, cv_b_out, loss_target, m_mod_w, m_mod_b, m_ln1_g, m_ln1_b, m_ln2_g, m_ln2_b, m_ffn_w_in, m_ffn_w_out, m_gm_w_in, m_gm_b_in, m_gm_ln_g, m_gm_ln_b, m_gm_w_s, m_gm_b_s, m_gm_w_out, m_fox_w_in, m_fox_b_f, m_fox_w_out, m_sb_w_in, m_sb_w_out, m_cv_w_in, m_cv_b_in, m_cv_dw, m_cv_dw_b, m_cv_ln_g, m_cv_ln_b, m_cv_w_out, m_cv_b_out, v_mod_w, v_mod_b, v_ln1_g, v_ln1_b, v_ln2_g, v_ln2_b, v_ffn_w_in, v_ffn_w_out, v_gm_w_in, v_gm_b_in, v_gm_ln_g, v_gm_ln_b, v_gm_w_s, v_gm_b_s, v_gm_w_out, v_fox_w_in, v_fox_b_f, v_fox_w_out, v_sb_w_in, v_sb_w_out, v_cv_w_in, v_cv_b_in, v_cv_dw, v_cv_dw_b, v_cv_ln_g, v_cv_ln_b, v_cv_w_out, v_cv_b_out):
    weights = dict(mod_w=mod_w, mod_b=mod_b, ln1_g=ln1_g, ln1_b=ln1_b, ln2_g=ln2_g, ln2_b=ln2_b, ffn_w_in=ffn_w_in, ffn_w_out=ffn_w_out, gm_w_in=gm_w_in, gm_b_in=gm_b_in, gm_ln_g=gm_ln_g, gm_ln_b=gm_ln_b, gm_w_s=gm_w_s, gm_b_s=gm_b_s, gm_w_out=gm_w_out, fox_w_in=fox_w_in, fox_b_f=fox_b_f, fox_w_out=fox_w_out, sb_w_in=sb_w_in, sb_w_out=sb_w_out, cv_w_in=cv_w_in, cv_b_in=cv_b_in, cv_dw=cv_dw, cv_dw_b=cv_dw_b, cv_ln_g=cv_ln_g, cv_ln_b=cv_ln_b, cv_w_out=cv_w_out, cv_b_out=cv_b_out)
    mom1 = dict(mod_w=m_mod_w, mod_b=m_mod_b, ln1_g=m_ln1_g, ln1_b=m_ln1_b, ln2_g=m_ln2_g, ln2_b=m_ln2_b, ffn_w_in=m_ffn_w_in, ffn_w_out=m_ffn_w_out, gm_w_in=m_gm_w_in, gm_b_in=m_gm_b_in, gm_ln_g=m_gm_ln_g, gm_ln_b=m_gm_ln_b, gm_w_s=m_gm_w_s, gm_b_s=m_gm_b_s, gm_w_out=m_gm_w_out, fox_w_in=m_fox_w_in, fox_b_f=m_fox_b_f, fox_w_out=m_fox_w_out, sb_w_in=m_sb_w_in, sb_w_out=m_sb_w_out, cv_w_in=m_cv_w_in, cv_b_in=m_cv_b_in, cv_dw=m_cv_dw, cv_dw_b=m_cv_dw_b, cv_ln_g=m_cv_ln_g, cv_ln_b=m_cv_ln_b, cv_w_out=m_cv_w_out, cv_b_out=m_cv_b_out)
    mom2 = dict(mod_w=v_mod_w, mod_b=v_mod_b, ln1_g=v_ln1_g, ln1_b=v_ln1_b, ln2_g=v_ln2_g, ln2_b=v_ln2_b, ffn_w_in=v_ffn_w_in, ffn_w_out=v_ffn_w_out, gm_w_in=v_gm_w_in, gm_b_in=v_gm_b_in, gm_ln_g=v_gm_ln_g, gm_ln_b=v_gm_ln_b, gm_w_s=v_gm_w_s, gm_b_s=v_gm_b_s, gm_w_out=v_gm_w_out, fox_w_in=v_fox_w_in, fox_b_f=v_fox_b_f, fox_w_out=v_fox_w_out, sb_w_in=v_sb_w_in, sb_w_out=v_sb_w_out, cv_w_in=v_cv_w_in, cv_b_in=v_cv_b_in, cv_dw=v_cv_dw, cv_dw_b=v_cv_dw_b, cv_ln_g=v_cv_ln_g, cv_ln_b=v_cv_ln_b, cv_w_out=v_cv_w_out, cv_b_out=v_cv_b_out)
    names = list(weights)

    nb, seq, d = x.shape
    t = nb * seq
    nl = mod_w.shape[0]
    alpha = (2.0 * nl) ** 0.25
    me = 4 * lax.axis_index("x") + 2 * lax.axis_index("y") + lax.axis_index("c")
    xs = x.reshape(t, d)
    tgt = loss_target.reshape(t, d)
    n_mod = mod_w.shape[2]
    n_ffn = ffn_w_in.shape[2]
    n_heads = d // HEAD_DIM
    npair = d // LANES

    c_all = _exchange_small(_pad_rows(c, 8), "gather_c", False)[:, :nb].reshape(N_DEV * nb, d)
    mod_b_loc = lax.dynamic_slice_in_dim(mod_b, me * n_mod, n_mod, axis=1)[:, None, :]
    mod_loc = _mod_fwd(c_all, mod_w, mod_b_loc)
    mod_g = _exchange_small(mod_loc.reshape(nl * N_DEV * nb, n_mod), "gather_mod", False)
    mod_all = jnp.transpose(mod_g.reshape(N_DEV, nl, N_DEV * nb, n_mod), (1, 2, 0, 3)).reshape(nl, N_DEV * nb, N_DEV * n_mod)
    mod_me = lax.dynamic_slice_in_dim(mod_all, me * nb, nb, axis=1)
    mods = [[mod_me[l, :, k * d:(k + 1) * d][:, None, :] for k in range(6)] for l in range(nl)]

    big = ["ffn_w_in", "ffn_w_out", "gm_w_in", "gm_w_out", "fox_w_in", "fox_w_out", "sb_w_in", "sb_w_out", "cv_w_in", "cv_w_out"]
    gathered = dict(zip(big, _gather_weights([weights[n].astype(BF16) for n in big])))
    w_ffn_in = gathered["ffn_w_in"]
    w_ffn_out = gathered["ffn_w_out"].reshape(nl, N_DEV // 2, n_ffn, d)
    w_ffn_out_rows = w_ffn_out.reshape(nl * (N_DEV // 2), n_ffn, d)
    sq = lambda n: gathered[n].reshape(d, d)
    fox_full = _full_cols(gathered["fox_w_in"], fox_w_in.shape[2])
    fox_qkv_w, fox_f_wt = fox_full[:, :3 * d], jnp.transpose(fox_full[:, 3 * d:])
    sb_qkv_w = _full_cols(gathered["sb_w_in"], sb_w_in.shape[2])
    cvp = d // N_DEV
    cv_small = jnp.concatenate([_pad_rows(cv_dw[0], CONV_HALO), cv_dw_b, cv_ln_g, cv_ln_b, cv_b_out,
                                cv_b_in.reshape(2, cvp), jnp.zeros((2, cvp), F32)], axis=0)
    cv_all = _exchange_small(cv_small, "gather_cv_small", False)
    cv_rows = jnp.transpose(cv_all, (1, 0, 2)).reshape(cv_small.shape[0], d)
    cv_dw_f, cv_dwb_f, cv_lng_f, cv_lnb_f, cv_bout_f = (cv_rows[:CONV_HALO], cv_rows[32:33], cv_rows[33:34], cv_rows[34:35], cv_rows[35:36])
    cv_bin_f = cv_all[:, 36:38, :].reshape(1, 2 * d)

    saved = []
    h = _modulate(xs, mods[0][1], mods[0][0], seq)
    xin = xs
    for l in range(nl):
        kind = l % 4
        sv = dict(x=xin, h=h)
        if kind == 0:
            pre = _proj_cols("gm_in", h, gathered["gm_w_in"], gm_w_in.shape[2], bias=gm_b_in)
            yv = _gm_spatial_fwd(pre, gm_ln_g, gm_ln_b, gm_w_s[0], jnp.transpose(gm_b_s[0]), seq)
            y = _mm_plain("gm_out", yv, sq("gm_w_out"), NN)
            sv.update(pre=pre, yv=yv)
        elif kind == 1:
            qkv = _qkv_proj("fox_qkv", h, fox_qkv_w)
            ft = _mm("fox_gate_proj", fox_f_wt, h, _sds((n_heads, t), F32), (t // min(TMM, t), 1),
                     pl.BlockSpec((n_heads, d), lambda i, r: (0, 0)), pl.BlockSpec((min(TMM, t), d), lambda i, r: (i, 0)),
                     pl.BlockSpec((n_heads, min(TMM, t)), lambda i, r: (0, i)), NT)
            b_f = jnp.transpose(fox_b_f)
            frow_p = _fox_gate_fwd(ft, b_f, seq).reshape(npair, 2, t)
            o, lse = _fox_fwd(qkv, frow_p, nb, seq)
            y = _mm_plain("fox_out", o, sq("fox_w_out"), NN)
            sv.update(qkv=qkv, ft=ft, b_f=b_f, frow=frow_p, o=o, lse=lse)
        elif kind == 2:
            qkv = _qkv_proj("sb_qkv", h, sb_qkv_w)
            o, ltot = _sb_fwd(qkv, nb, seq)
            y = _mm_plain("sb_out", o, sq("sb_w_out"), NN)
            sv.update(qkv=qkv, o=o, ltot=ltot)
        else:
            pre = _proj_cols("cv_in", h, gathered["cv_w_in"], cv_w_in.shape[2], bias=cv_bin_f)
            y1, y2 = _cv_conv_fwd(pre, cv_dw_f, cv_dwb_f, cv_lng_f, cv_lnb_f, seq)
            y = _mm_plain("cv_out", y2, sq("cv_w_out"), NN, bias=cv_bout_f)
            sv.update(pre=pre, y1=y1, y2=y2)
        x1, h2 = _lnres_fwd(xin, y, mods[l][2], ln1_g[l:l + 1], ln1_b[l:l + 1], alpha, seq, nxt=(mods[l][4], mods[l][3]))
        hg, hu, act = _ffn_in(h2, w_ffn_in, l)
        y2f = _ffn_out(act, w_ffn_out_rows, l * (N_DEV // 2))
        sv.update(y=y, x1=x1, h2=h2, hg=hg, hu=hu, act=act, y2f=y2f)
        if l + 1 < nl:
            xin, h = _lnres_fwd(x1, y2f, mods[l][5], ln2_g[l:l + 1], ln2_b[l:l + 1], alpha, seq, nxt=(mods[l + 1][1], mods[l + 1][0]))
        else:
            xin = _lnres_fwd(x1, y2f, mods[l][5], ln2_g[l:l + 1], ln2_b[l:l + 1], alpha, seq)
        saved.append(sv)

    dx, sq_err = _loss_head(xin, tgt)
    loss = lax.psum(0.5 * jnp.sum(sq_err) / d, ("x", "y", "c"))

    small = {}
    bigg = {}
    dmods = [None] * nl
    d_ln = dict(ln1_g=[None] * nl, ln1_b=[None] * nl, ln2_g=[None] * nl, ln2_b=[None] * nl)
    ffn_gi, ffn_go = [None] * nl, [None] * nl
    for l in reversed(range(nl)):
        sv = saved[l]
        kind = l % 4
        dxr, dy2, dlg, dlb, _, dgate2 = _lnres_bwd(dx, sv["x1"], sv["y2f"], mods[l][5], ln2_g[l:l + 1], alpha, seq)
        d_ln["ln2_g"][l], d_ln["ln2_b"][l] = dlg, dlb
        dg_, du_ = _ffn_dact(dy2, w_ffn_out_rows, sv["hg"], sv["hu"], l * (N_DEV // 2))
        dwg, dwu, dwo = _ffn_bwd_weights(sv["h2"], dy2, sv["act"], dg_, du_)
        ffn_gi[l] = jnp.concatenate([dwg, dwu], axis=0)
        ffn_go[l] = dwo.reshape(N_DEV, n_ffn // 2, d)
        dh2 = _ffn_dh(dg_, du_, w_ffn_in, l)
        dx1, dsc2, dsh2 = _modulate_bwd(dh2, dxr, sv["x1"], mods[l][4], seq)
        dxr, dy, dlg, dlb, dysum, dgate1 = _lnres_bwd(dx1, sv["x"], sv["y"], mods[l][2], ln1_g[l:l + 1], alpha, seq)
        d_ln["ln1_g"][l], d_ln["ln1_b"][l] = dlg, dlb
        hh = sv["h"]
        if kind == 0:
            dyv = _mm_plain("gm_out_bwd", dy, sq("gm_w_out"), NT)
            bigg["gm_w_out"] = _mm_plain("gm_out_dw", sv["yv"], dy, TN, GRAD_WIRE).reshape(1, N_DEV, d // N_DEV, d)
            dpre, dws, dbst, dlng, dlnb, dbin = _gm_spatial_bwd(sv["pre"], dyv, gm_ln_g, gm_ln_b, gm_w_s[0], jnp.transpose(gm_b_s[0]), seq)
            small.update(gm_w_s=dws[None], gm_b_s=jnp.transpose(dbst)[None], gm_ln_g=dlng, gm_ln_b=dlnb, gm_b_in=dbin)
            bigg["gm_w_in"] = _grad_cols("gm_in_dw", hh, dpre, gm_w_in.shape[2])
            dh = _back_cols("gm_in_bwd", dpre, gathered["gm_w_in"], gm_w_in.shape[2])
        elif kind == 1:
            do = _mm_plain("fox_out_bwd", dy, sq("fox_w_out"), NT)
            bigg["fox_w_out"] = _mm_plain("fox_out_dw", sv["o"], dy, TN, GRAD_WIRE).reshape(1, N_DEV, d // N_DEV, d)
            dqkv, dfr, dfq = _fox_bwd(sv["qkv"], sv["frow"], sv["o"], do, sv["lse"], nb, seq)
            dft, dbf = _fox_gate_bwd(sv["ft"], sv["b_f"], dfr.reshape(n_heads, t),
                                     jnp.transpose(dfq, (0, 2, 1)).reshape(n_heads, t), seq)
            small["fox_b_f"] = jnp.transpose(dbf)
            dw_qkv = _qkv_dw("fox_qkv_dw", hh, dqkv)
            tk = min(TMM, t)
            dw_ft = _mm("fox_gate_dw", dft, hh, _sds((n_heads, d), F32), (1, t // tk),
                        pl.BlockSpec((n_heads, tk), lambda j, r: (0, r)), pl.BlockSpec((tk, d), lambda j, r: (r, 0)),
                        pl.BlockSpec((n_heads, d), lambda j, r: (0, 0)), NN)
            bigg["fox_w_in"] = _col_blocks(jnp.concatenate([dw_qkv, jnp.transpose(dw_ft).astype(GRAD_WIRE)], axis=1), fox_w_in.shape[2])
            dh_a = _qkv_dh("fox_qkv_bwd", dqkv, fox_qkv_w)
            tm = min(TMM, t)
            dh = _mm("fox_gate_bwd_h", dft, fox_f_wt, _sds((t, d), F32), (t // tm, 1),
                     pl.BlockSpec((n_heads, tm), lambda i, r: (0, i)), pl.BlockSpec((n_heads, d), lambda i, r: (0, 0)),
                     pl.BlockSpec((tm, d), lambda i, r: (i, 0)), TN, (dh_a,), (pl.BlockSpec((tm, d), lambda i, r: (i, 0)),), _add)
        elif kind == 2:
            do = _mm_plain("sb_out_bwd", dy, sq("sb_w_out"), NT)
            bigg["sb_w_out"] = _mm_plain("sb_out_dw", sv["o"], dy, TN, GRAD_WIRE).reshape(1, N_DEV, d // N_DEV, d)
            dqkv = _sb_bwd(sv["qkv"], do, sv["ltot"], nb, seq)
            bigg["sb_w_in"] = _col_blocks(_qkv_dw("sb_qkv_dw", hh, dqkv), sb_w_in.shape[2])
            dh = _qkv_dh("sb_qkv_bwd", dqkv, sb_qkv_w)
        else:
            dy2c = _mm_plain("cv_out_bwd", dy, sq("cv_w_out"), NT)
            bigg["cv_w_out"] = _mm_plain("cv_out_dw", sv["y2"], dy, TN, GRAD_WIRE).reshape(1, N_DEV, d // N_DEV, d)
            dy1, dlng, dlnb, ddwb = _cv_norm_bwd(sv["y1"], dy2c, cv_lng_f, cv_lnb_f)
            dpre, ddw, dbin = _cv_conv_bwd(sv["pre"], dy1, cv_dw_f, seq)
            small.update(cv_b_out=dysum, cv_ln_g=dlng, cv_ln_b=dlnb, cv_dw_b=ddwb, cv_dw=ddw[:CONV_WIDTH], cv_b_in=dbin)
            bigg["cv_w_in"] = _grad_cols("cv_in_dw", hh, dpre, cv_w_in.shape[2])
            dh = _back_cols("cv_in_bwd", dpre, gathered["cv_w_in"], cv_w_in.shape[2])
        dx, dsc1, dsh1 = _modulate_bwd(dh, dxr, sv["x"], mods[l][1], seq)
        dmods[l] = jnp.concatenate([dsh1, dsc1, dgate1, dsh2, dsc2, dgate2], axis=2)[:, 0, :]
    grad_x = dx.reshape(nb, seq, d)
    bigg["ffn_w_in"] = jnp.stack(ffn_gi)
    bigg["ffn_w_out"] = jnp.stack(ffn_go)
    for n in d_ln:
        small[n] = jnp.concatenate(d_ln[n], axis=0)

    dmod_rows = jnp.stack(dmods).reshape(nl * nb, 6 * d)
    dmod_g = _exchange_small(_pad_rows(dmod_rows, 8 * ((nl * nb + 7) // 8)), "gather_dmod", False)[:, :nl * nb]
    dmod_all = jnp.transpose(dmod_g.reshape(N_DEV, nl, nb, 6 * d), (1, 0, 2, 3)).reshape(nl, N_DEV * nb, 6 * d)
    dmod_loc = lax.dynamic_slice_in_dim(dmod_all, me * n_mod, n_mod, axis=2)
    g_mod_w, g_mod_b = _mod_bwd(c_all, dmod_loc, dmod_all)
    grads = dict(mod_w=g_mod_w, mod_b=g_mod_b[:, 0, :])

    rep = ["ln1_g", "ln1_b", "ln2_g", "ln2_b", "gm_b_in", "gm_ln_g", "gm_ln_b", "gm_w_s", "gm_b_s", "fox_b_f"]
    cvs = ["cv_b_in", "cv_dw", "cv_dw_b", "cv_ln_g", "cv_ln_b", "cv_b_out"]

    def rows_of(a):
        flat = a.reshape(-1)
        pad = (-flat.shape[0]) % d
        return jnp.pad(flat, (0, pad)).reshape(-1, d)

    pack_rows = [rows_of(small[n]) for n in rep + cvs]
    counts = [r.shape[0] for r in pack_rows]
    total = sum(counts)
    pack = _pad_rows(jnp.concatenate(pack_rows, axis=0), 8 * ((total + 7) // 8))
    summed = _exchange_small(pack, "allreduce_small", True)
    offs = [sum(counts[:i]) for i in range(len(counts))]
    rep_rows = sum(counts[:len(rep)])
    for n, o_, cnt in zip(rep + cvs, offs, counts):
        full = summed[o_:o_ + cnt].reshape(-1)
        if n in rep:
            grads[n] = full[:weights[n].size].reshape(weights[n].shape)
        else:
            wshape = weights[n].shape
            cols = wshape[-1]
            full = full[:math.prod(wshape[:-1]) * cols * N_DEV].reshape(wshape[:-1] + (cols * N_DEV,))
            grads[n] = lax.dynamic_slice_in_dim(full, me * cols, cols, axis=full.ndim - 1)

    recv = dict(zip(big, _scatter_grads([bigg[n] for n in big])))

    outs = {}

    def view2(a):
        return a.reshape(-1, a.shape[-1])

    for n in big:
        w2 = view2(weights[n])
        res = _adamw("adamw_" + n, w2, view2(mom1[n]), view2(mom2[n]), parts=recv[n].reshape((N_DEV,) + w2.shape))
        outs[n] = [r.reshape(weights[n].shape) for r in res]
    res = _adamw("adamw_mod_w", view2(mod_w), view2(m_mod_w), view2(v_mod_w), g=view2(grads["mod_w"]))
    outs["mod_w"] = [r.reshape(mod_w.shape) for r in res]
    rp = lambda src: _pad_rows(jnp.concatenate([rows_of(src[n]) for n in rep], axis=0), 8 * ((rep_rows + 7) // 8))
    res = _adamw("adamw_replicated", rp(weights), rp(mom1), rp(mom2), g=rp(grads))
    for n, o_, cnt in zip(rep, offs, counts):
        outs[n] = [r[o_:o_ + cnt].reshape(-1)[:weights[n].size].reshape(weights[n].shape) for r in res]
    cv_cols = weights["cv_b_out"].shape[-1]
    cp = lambda src: jnp.concatenate([src[n].reshape(-1, cv_cols) for n in cvs], axis=0)
    cv_cnt = [weights[n].size // cv_cols for n in cvs]
    cv_tot = sum(cv_cnt)
    cpp = lambda src: _pad_rows(cp(src), 8 * ((cv_tot + 7) // 8))
    res = _adamw("adamw_cv_small", cpp(weights), cpp(mom1), cpp(mom2), g=cpp(grads))
    o_ = 0
    for n, cnt in zip(cvs, cv_cnt):
        outs[n] = [r[o_:o_ + cnt].reshape(weights[n].shape) for r in res]
        o_ += cnt
    res = _adamw("adamw_mod_b", mod_b, m_mod_b, v_mod_b, g=grads["mod_b"])
    outs["mod_b"] = list(res)

    return (loss, grad_x, *[outs[n][0] for n in names], *[outs[n][1] for n in names],
            *[outs[n][2] for n in names], *[outs[n][3] for n in names])
```

```python
import functools
import math

import jax
import jax.numpy as jnp
from jax import lax
from jax.experimental import pallas as pl
from jax.experimental.pallas import tpu as pltpu

F32 = jnp.float32
BF16 = jnp.bfloat16
MESH = pl.DeviceIdType.MESH

N_DEV = 8
HEAD_DIM = 64
LANES = 128
GM_CHUNK = 128
GM_GROUPS = 8
CONV_WIDTH = 31
CONV_HALO = 32
LN_EPS = 1e-5
NEG_INF = -1e30
SB_DEAD = -100.0
GRAD_WIRE = jnp.bfloat16

ADAM_LR = 0.001
ADAM_B1 = 0.9
ADAM_B2 = 0.999
ADAM_EPS = 1e-08
ADAM_WD = 0.01
ADAM_STEP = 10

TM = 512
TMM = 1024
TQ = 256
FOX_FWD_UNROLL = 4
FOX_BWD_UNROLL = 2
TT = 512
VMEM_BIG = 56 * 1024 * 1024

NN = (((1,), (0,)), ((), ()))
NT = (((1,), (1,)), ((), ()))
TN = (((0,), (0,)), ((), ()))


def _call(body, name, out_shape, grid=None, in_specs=None, out_specs=None, scratch=(), vmem=None):
    params = {}
    if grid is not None:
        params["dimension_semantics"] = ("arbitrary",) * len(grid)
    if vmem is not None:
        params["vmem_limit_bytes"] = vmem
    kw = {}
    if grid is not None:
        kw["grid"] = grid
    if in_specs is not None:
        kw["in_specs"] = in_specs
    if out_specs is not None:
        kw["out_specs"] = out_specs
    return pl.pallas_call(body, name=name, out_shape=out_shape, scratch_shapes=list(scratch),
                          compiler_params=pltpu.CompilerParams(**params), **kw)


def _sds(shape, dtype):
    return jax.ShapeDtypeStruct(tuple(shape), dtype)


def _dot(a, b, dims=NN):
    return lax.dot_general(a.astype(BF16), b.astype(BF16), dims, preferred_element_type=F32)


def _split3(x):
    h1 = x.astype(BF16)
    r1 = x - h1.astype(F32)
    h2 = r1.astype(BF16)
    h3 = (r1 - h2.astype(F32)).astype(BF16)
    return h1, h2, h3


def _dot_exact(x, m, dims=NN):
    h1, h2, h3 = _split3(x)
    d = lambda h: lax.dot_general(h, m, dims, preferred_element_type=F32)
    return (d(h1) + d(h2)) + d(h3)


def _dot_exact_rhs(m, x, dims=NN):
    h1, h2, h3 = _split3(x)
    d = lambda h: lax.dot_general(m, h, dims, preferred_element_type=F32)
    return (d(h1) + d(h2)) + d(h3)


def _silu(x):
    return x * jax.nn.sigmoid(x)


def _gelu(x):
    return 0.5 * x * (1.0 + lax.erf(x * (2.0 ** -0.5)))


def _gelu_grad(x):
    return 0.5 * (1.0 + lax.erf(x * (2.0 ** -0.5))) + x * jnp.exp(-0.5 * x * x) * ((2.0 * math.pi) ** -0.5)


def _log_sigmoid(z):
    return jnp.minimum(z, 0.0) - jnp.log(1.0 + jnp.exp(-jnp.abs(z)))


def _ln_stats(r):
    mu = jnp.mean(r, axis=-1, keepdims=True)
    rc = r - mu
    var = jnp.mean(rc * rc, axis=-1, keepdims=True)
    return rc, lax.rsqrt(var + LN_EPS)


def _colsum(x):
    return jnp.sum(x, axis=0, keepdims=True)


def _peers():
    mx, my, mc = lax.axis_index("x"), lax.axis_index("y"), lax.axis_index("c")
    me = 4 * mx + 2 * my + mc
    out = []
    for k in range(1, N_DEV):
        px = 1 - mx if (k >> 2) & 1 else mx
        py = 1 - my if (k >> 1) & 1 else my
        pc = 1 - mc if k & 1 else mc
        out.append(((px, py, pc), 4 * px + 2 * py + pc))
    return me, out


def _exchange_small(x, name, reduce):
    rows, cols = x.shape

    def body(x_ref, o_ref, *rest):
        if reduce:
            land, send_sems, recv_sems, local_sem = rest
        else:
            land = o_ref
            send_sems, recv_sems, local_sem = rest
        me, peers = _peers()
        mine = pltpu.make_async_copy(x_ref, land.at[me], local_sem)
        mine.start()
        sends = []
        for k, (peer, _) in enumerate(peers):
            cp = pltpu.make_async_remote_copy(src_ref=x_ref, dst_ref=land.at[me], send_sem=send_sems.at[k],
                                              recv_sem=recv_sems.at[k], device_id=peer, device_id_type=MESH)
            cp.start()
            sends.append(cp)
        for k, (peer, blk) in enumerate(peers):
            pltpu.make_async_remote_copy(src_ref=x_ref, dst_ref=land.at[blk], send_sem=send_sems.at[k],
                                         recv_sem=recv_sems.at[k], device_id=peer, device_id_type=MESH).wait_recv()
        for cp in sends:
            cp.wait_send()
        mine.wait()
        if reduce:
            acc = land[0]
            for s in range(1, N_DEV):
                acc = acc + land[s]
            o_ref[...] = acc

    vm = pl.BlockSpec(memory_space=pltpu.VMEM)
    scratch = [pltpu.SemaphoreType.DMA((N_DEV - 1,)), pltpu.SemaphoreType.DMA((N_DEV - 1,)), pltpu.SemaphoreType.DMA]
    if reduce:
        scratch = [pltpu.VMEM((N_DEV, rows, cols), F32)] + scratch
        out = _sds((rows, cols), F32)
    else:
        out = _sds((N_DEV, rows, cols), F32)
    return _call(body, name, out, in_specs=[vm], out_specs=vm, scratch=scratch, vmem=VMEM_BIG)(x)


def _gather_weights(shards):
    n = len(shards)

    def body(*refs):
        ins, outs = refs[:n], refs[n:2 * n]
        send_sems, recv_sems, local_sems = refs[2 * n:]
        mx, my, mc = lax.axis_index("x"), lax.axis_index("y"), lax.axis_index("c")
        sibling = (mx, my, 1 - mc)
        chips = [(1 - mx, my), (mx, 1 - my), (1 - mx, 1 - my)]

        def block(px, py, pc):
            return 4 * px + 2 * py + pc

        def copy(a, k, blk, to, src=None):
            dst = outs[a].at[:, blk]
            return pltpu.make_async_remote_copy(src_ref=dst if src is None else src, dst_ref=dst, send_sem=send_sems.at[a, k],
                                                recv_sem=recv_sems.at[a, k], device_id=to, device_id_type=MESH)

        me = block(mx, my, mc)
        local, sends = [], []
        for a in range(n):
            cp = pltpu.make_async_copy(ins[a], outs[a].at[:, me], local_sems.at[a])
            cp.start()
            local.append(cp)
            first = [copy(a, 0, me, sibling, src=ins[a])]
            first += [copy(a, 1 + j, me, (*chip, mc), src=ins[a]) for j, chip in enumerate(chips)]
            for cp in first:
                cp.start()
            sends += first
        for a in range(n):
            for j, chip in enumerate(chips):
                blk = block(*chip, mc)
                copy(a, 1 + j, blk, (mx, my, mc)).wait_recv()
                cp = copy(a, 4 + j, blk, sibling)
                cp.start()
                sends.append(cp)
        for a in range(n):
            copy(a, 0, block(mx, my, 1 - mc), (mx, my, mc)).wait_recv()
            for j, chip in enumerate(chips):
                copy(a, 4 + j, block(*chip, 1 - mc), (mx, my, mc)).wait_recv()
        for cp in sends:
            cp.wait_send()
        for cp in local:
            cp.wait()

    hbm = pl.BlockSpec(memory_space=pl.ANY)
    out = [_sds((s.shape[0], N_DEV) + s.shape[1:], s.dtype) for s in shards]
    scratch = [pltpu.SemaphoreType.DMA((n, N_DEV - 1)), pltpu.SemaphoreType.DMA((n, N_DEV - 1)), pltpu.SemaphoreType.DMA((n,))]
    return _call(body, "gather_weights", out, in_specs=[hbm] * n, out_specs=[hbm] * n, scratch=scratch)(*shards)


def _scatter_grads(grads):
    n = len(grads)

    def body(*refs):
        ins, outs = refs[:n], refs[n:2 * n]
        send_sems, recv_sems, local_sems = refs[2 * n:]
        me, peers = _peers()
        local = []
        sends = []
        for a in range(n):
            cp = pltpu.make_async_copy(ins[a].at[:, me], outs[a].at[me], local_sems.at[a])
            cp.start()
            local.append(cp)
            for k, (peer, blk) in enumerate(peers):
                cp = pltpu.make_async_remote_copy(src_ref=ins[a].at[:, blk], dst_ref=outs[a].at[me], send_sem=send_sems.at[a, k],
                                                  recv_sem=recv_sems.at[a, k], device_id=peer, device_id_type=MESH)
                cp.start()
                sends.append(cp)
        for a in range(n):
            for k, (peer, blk) in enumerate(peers):
                pltpu.make_async_remote_copy(src_ref=ins[a].at[:, me], dst_ref=outs[a].at[blk], send_sem=send_sems.at[a, k],
                                             recv_sem=recv_sems.at[a, k], device_id=peer, device_id_type=MESH).wait_recv()
        for cp in sends:
            cp.wait_send()
        for cp in local:
            cp.wait()

    hbm = pl.BlockSpec(memory_space=pl.ANY)
    out = [_sds((N_DEV, g.shape[0]) + g.shape[2:], g.dtype) for g in grads]
    scratch = [pltpu.SemaphoreType.DMA((n, N_DEV - 1)), pltpu.SemaphoreType.DMA((n, N_DEV - 1)), pltpu.SemaphoreType.DMA((n,))]
    return _call(body, "scatter_grads", out, in_specs=[hbm] * n, out_specs=[hbm] * n, scratch=scratch)(*grads)


def _mm(name, a, b, out, grid, a_spec, b_spec, o_spec, dims, extra=(), extra_specs=(), epilogue=None, vmem=VMEM_BIG):
    nred = grid[-1]
    red_axis = len(grid) - 1
    acc_shape = tuple(d for d in o_spec.block_shape if d is not None)
    n_extra = len(extra)

    def body(a_ref, b_ref, *rest):
        ex = rest[:n_extra]
        o_ref = rest[n_extra]

        def finish(acc):
            if epilogue is not None:
                acc = epilogue(acc, *[e[...] for e in ex])
            o_ref[...] = acc.astype(o_ref.dtype)

        prod = _dot(a_ref[...], b_ref[...], dims)
        if nred == 1:
            finish(prod)
        else:
            acc_ref = rest[n_extra + 1]
            r = pl.program_id(red_axis)

            @pl.when(r == 0)
            def _():
                acc_ref[...] = prod

            @pl.when(r > 0)
            def _():
                acc_ref[...] += prod

            @pl.when(r == nred - 1)
            def _():
                finish(acc_ref[...])

    scratch = [pltpu.VMEM(acc_shape, F32)] if nred > 1 else []
    return _call(body, name, out, grid=grid, in_specs=[a_spec, b_spec, *extra_specs], out_specs=o_spec,
                 scratch=scratch, vmem=vmem)(a, b, *extra)


def _add(acc, x):
    return acc + x


def _proj_cols(name, h, w, n_slot, bias=None, out_dtype=F32):
    t, k = h.shape
    s = w.shape[1]
    tm = min(TMM, t)
    extra, especs, epi = (), (), None
    if bias is not None:
        extra, especs, epi = (bias,), (pl.BlockSpec((1, n_slot), lambda i, j, r: (0, j)),), _add
    return _mm(name, h, w, _sds((t, s * n_slot), out_dtype), (t // tm, s, 1),
               pl.BlockSpec((tm, k), lambda i, j, r: (i, 0)),
               pl.BlockSpec((None, None, k, n_slot), lambda i, j, r: (0, j, 0, 0)),
               pl.BlockSpec((tm, n_slot), lambda i, j, r: (i, j)), NN, extra, especs, epi)


def _accumulate_over_tokens(name, ins, in_specs, out, o_spec, acc_shape, n_steps, terms, store=None):
    def body(*refs):
        o_ref, acc_ref = refs[len(ins)], refs[len(ins) + 1]
        r = pl.program_id(0)

        @pl.when(r == 0)
        def _():
            acc_ref[...] = jnp.zeros_like(acc_ref)

        for s, prod in enumerate(terms(*refs[:len(ins)])):
            acc_ref[s] += prod

        @pl.when(r == n_steps - 1)
        def _():
            if store is None:
                o_ref[...] = acc_ref[...].reshape(o_ref.shape).astype(o_ref.dtype)
            else:
                store(o_ref, acc_ref)

    return _call(body, name, out, grid=(n_steps,), in_specs=in_specs, out_specs=o_spec,
                 scratch=[pltpu.VMEM(acc_shape, F32)], vmem=VMEM_BIG)(*ins)


def _grad_cols(name, h, g, n_slot):
    t, k = h.shape
    s = g.shape[1] // n_slot
    tk = min(TMM, t)

    def terms(h_ref, g_ref):
        hv = h_ref[...]
        return [_dot(hv, g_ref[:, j * n_slot:(j + 1) * n_slot], TN) for j in range(s)]

    return _accumulate_over_tokens(name, (h, g), [pl.BlockSpec((tk, k), lambda r: (r, 0)), pl.BlockSpec((tk, s * n_slot), lambda r: (r, 0))],
                                   _sds((1, s, k, n_slot), GRAD_WIRE), pl.BlockSpec((1, s, k, n_slot), lambda r: (0, 0, 0, 0)),
                                   (s, k, n_slot), t // tk, terms)


def _back_cols(name, g, w, n_slot):
    t = g.shape[0]
    s, k = w.shape[1], w.shape[2]
    tm = min(TM, t)

    def body(g_ref, w_ref, o_ref):
        acc = None
        for j in range(s):
            term = _dot(g_ref[:, j * n_slot:(j + 1) * n_slot], w_ref[j], NT)
            acc = term if acc is None else acc + term
        o_ref[...] = acc

    return _call(body, name, _sds((t, k), F32), grid=(t // tm,),
                 in_specs=[pl.BlockSpec((tm, s * n_slot), lambda i: (i, 0)), pl.BlockSpec((None, s, k, n_slot), lambda i: (0, 0, 0, 0))],
                 out_specs=pl.BlockSpec((tm, k), lambda i: (i, 0)), vmem=VMEM_BIG)(g, w)


def _mm_plain(name, a, b, dims, out_dtype=F32, bias=None):
    if dims == TN:
        t, k = a.shape
        n = b.shape[1]
        tk = min(TMM, t)
        return _mm(name, a, b, _sds((k, n), out_dtype), (1, t // tk),
                   pl.BlockSpec((tk, k), lambda j, r: (r, 0)), pl.BlockSpec((tk, n), lambda j, r: (r, 0)),
                   pl.BlockSpec((k, n), lambda j, r: (0, 0)), TN)
    t = a.shape[0]
    tm = min(TMM, t)
    n = b.shape[1] if dims == NN else b.shape[0]
    extra, especs, epi = (), (), None
    if bias is not None:
        extra, especs, epi = (bias,), (pl.BlockSpec((1, n), lambda i, r: (0, 0)),), _add
    return _mm(name, a, b, _sds((t, n), out_dtype), (t // tm, 1),
               pl.BlockSpec((tm, a.shape[1]), lambda i, r: (i, 0)), pl.BlockSpec(b.shape, lambda i, r: (0, 0)),
               pl.BlockSpec((tm, n), lambda i, r: (i, 0)), dims, extra, especs, epi)


def _mod_fwd(c_all, mod_w, mod_b_loc):
    nl, d, n = mod_w.shape
    nb = c_all.shape[0]

    def body(c_ref, w_ref, b_ref, o_ref):
        o_ref[...] = _dot(_silu(c_ref[...]), w_ref[...]) + b_ref[...]

    return _call(body, "mod_fwd", _sds((nl, nb, n), F32), grid=(nl,),
                 in_specs=[pl.BlockSpec((nb, d), lambda l: (0, 0)), pl.BlockSpec((None, d, n), lambda l: (l, 0, 0)),
                           pl.BlockSpec((None, 1, n), lambda l: (l, 0, 0))],
                 out_specs=pl.BlockSpec((None, nb, n), lambda l: (l, 0, 0)))(c_all, mod_w, mod_b_loc)


def _mod_bwd(c_all, dmod_loc, dmod_all):
    nl, nb, n = dmod_loc.shape
    d = c_all.shape[1]
    n_all = dmod_all.shape[2]

    def body(c_ref, dl_ref, da_ref, gw_ref, gb_ref):
        gw_ref[...] = _dot(_silu(c_ref[...]), dl_ref[...], TN)
        gb_ref[...] = _colsum(da_ref[...])

    return _call(body, "mod_bwd", (_sds((nl, d, n), F32), _sds((nl, 1, n_all), F32)), grid=(nl,),
                 in_specs=[pl.BlockSpec((nb, d), lambda l: (0, 0)), pl.BlockSpec((None, nb, n), lambda l: (l, 0, 0)),
                           pl.BlockSpec((None, nb, n_all), lambda l: (l, 0, 0))],
                 out_specs=(pl.BlockSpec((None, d, n), lambda l: (l, 0, 0)), pl.BlockSpec((None, 1, n_all), lambda l: (l, 0, 0))),
                 )(c_all, dmod_loc, dmod_all)


def _row_spec(d, tpb):
    return pl.BlockSpec((None, 1, d), lambda i: (i // tpb, 0, 0))


def _tile_spec(tm, d):
    return pl.BlockSpec((tm, d), lambda i: (i, 0))


def _vec_spec(d):
    return pl.BlockSpec((1, d), lambda i: (0, 0))


def _modulate(x, sc, sh, seq):
    t, d = x.shape
    tm = min(TM, seq)
    tpb = seq // tm

    def body(x_ref, sc_ref, sh_ref, h_ref):
        h_ref[...] = (x_ref[...] * (1.0 + sc_ref[...]) + sh_ref[...]).astype(BF16)

    return _call(body, "modulate", _sds((t, d), BF16), grid=(t // tm,),
                 in_specs=[_tile_spec(tm, d), _row_spec(d, tpb), _row_spec(d, tpb)], out_specs=_tile_spec(tm, d))(x, sc, sh)


def _lnres_fwd(x, y, gate, lg, lb, alpha, seq, nxt=None):
    t, d = x.shape
    tm = min(TM, seq)
    tpb = seq // tm

    def body(x_ref, y_ref, g_ref, lg_ref, lb_ref, *rest):
        r = alpha * x_ref[...] + (1.0 + g_ref[...]) * y_ref[...]
        rc, rstd = _ln_stats(r)
        xn = rc * rstd * lg_ref[...] + lb_ref[...]
        if nxt is None:
            rest[0][...] = xn
        else:
            sc_ref, sh_ref, xo_ref, h_ref = rest
            xo_ref[...] = xn
            h_ref[...] = (xn * (1.0 + sc_ref[...]) + sh_ref[...]).astype(BF16)

    ins = [_tile_spec(tm, d), _tile_spec(tm, d), _row_spec(d, tpb), _vec_spec(d), _vec_spec(d)]
    if nxt is None:
        return _call(body, "lnres_fwd_last", _sds((t, d), F32), grid=(t // tm,), in_specs=ins,
                     out_specs=_tile_spec(tm, d))(x, y, gate, lg, lb)
    return _call(body, "lnres_fwd", (_sds((t, d), F32), _sds((t, d), BF16)), grid=(t // tm,),
                 in_specs=ins + [_row_spec(d, tpb), _row_spec(d, tpb)],
                 out_specs=(_tile_spec(tm, d), _tile_spec(tm, d)))(x, y, gate, lg, lb, *nxt)


def _loss_head(x, tgt):
    t, d = x.shape
    tm = min(TM, t)

    def body(x_ref, t_ref, dx_ref, sq_ref):
        e = x_ref[...] - t_ref[...]
        dx_ref[...] = e * (1.0 / d)

        @pl.when(pl.program_id(0) == 0)
        def _():
            sq_ref[...] = jnp.zeros_like(sq_ref)

        sq_ref[...] += _colsum(e * e)

    return _call(body, "loss_head", (_sds((t, d), F32), _sds((1, d), F32)), grid=(t // tm,),
                 in_specs=[_tile_spec(tm, d), _tile_spec(tm, d)], out_specs=(_tile_spec(tm, d), _vec_spec(d)))(x, tgt)


def _lnres_bwd(dxo, x, y, gate, lg, alpha, seq):
    t, d = x.shape
    tm = min(TM, seq)
    tpb = seq // tm
    nb = t // seq

    def body(dxo_ref, x_ref, y_ref, g_ref, lg_ref, dxr_ref, dy_ref, dlg_ref, dlb_ref, dys_ref, dg_ref):
        i = pl.program_id(0)
        yv = y_ref[...]
        r = alpha * x_ref[...] + (1.0 + g_ref[...]) * yv
        rc, rstd = _ln_stats(r)
        xhat = rc * rstd
        dxo_v = dxo_ref[...]
        dxh = dxo_v * lg_ref[...]
        m1 = jnp.mean(dxh, axis=-1, keepdims=True)
        m2 = jnp.mean(dxh * xhat, axis=-1, keepdims=True)
        dr = rstd * (dxh - m1 - xhat * m2)
        dyv = (1.0 + g_ref[...]) * dr
        dxr_ref[...] = alpha * dr
        dy_ref[...] = dyv.astype(BF16)

        @pl.when(i == 0)
        def _():
            dlg_ref[...] = jnp.zeros_like(dlg_ref)
            dlb_ref[...] = jnp.zeros_like(dlb_ref)
            dys_ref[...] = jnp.zeros_like(dys_ref)

        @pl.when(i % tpb == 0)
        def _():
            dg_ref[...] = jnp.zeros_like(dg_ref)

        dlg_ref[...] += _colsum(dxo_v * xhat)
        dlb_ref[...] += _colsum(dxo_v)
        dys_ref[...] += _colsum(dyv)
        dg_ref[...] += _colsum(dr * yv)

    return _call(body, "lnres_bwd",
                 (_sds((t, d), F32), _sds((t, d), BF16), _sds((1, d), F32), _sds((1, d), F32), _sds((1, d), F32), _sds((nb, 1, d), F32)),
                 grid=(t // tm,),
                 in_specs=[_tile_spec(tm, d), _tile_spec(tm, d), _tile_spec(tm, d), _row_spec(d, tpb), _vec_spec(d)],
                 out_specs=(_tile_spec(tm, d), _tile_spec(tm, d), _vec_spec(d), _vec_spec(d), _vec_spec(d), _row_spec(d, tpb)),
                 )(dxo, x, y, gate, lg)


def _modulate_bwd(dh, dxr, x, sc, seq):
    t, d = x.shape
    tm = min(TM, seq)
    tpb = seq // tm
    nb = t // seq

    def body(dh_ref, dxr_ref, x_ref, sc_ref, dx_ref, dsc_ref, dsh_ref):
        dhv = dh_ref[...]
        dx_ref[...] = dxr_ref[...] + dhv * (1.0 + sc_ref[...])

        @pl.when(pl.program_id(0) % tpb == 0)
        def _():
            dsc_ref[...] = jnp.zeros_like(dsc_ref)
            dsh_ref[...] = jnp.zeros_like(dsh_ref)

        dsc_ref[...] += _colsum(dhv * x_ref[...])
        dsh_ref[...] += _colsum(dhv)

    return _call(body, "modulate_bwd", (_sds((t, d), F32), _sds((nb, 1, d), F32), _sds((nb, 1, d), F32)), grid=(t // tm,),
                 in_specs=[_tile_spec(tm, d), _tile_spec(tm, d), _tile_spec(tm, d), _row_spec(d, tpb)],
                 out_specs=(_tile_spec(tm, d), _row_spec(d, tpb), _row_spec(d, tpb)))(dh, dxr, x, sc)


def _ffn_in(h, w_in, layer):
    t, d = h.shape
    n = w_in.shape[3]
    half = N_DEV // 2
    tm = min(TMM, t)

    def body(h_ref, wg_ref, wu_ref, g_ref, u_ref, a_ref):
        hv = h_ref[...]
        g = _dot(hv, wg_ref[...])
        u = _dot(hv, wu_ref[...])
        g_ref[...] = g
        u_ref[...] = u
        a_ref[...] = (_silu(g) * u).astype(BF16)

    blk = pl.BlockSpec((None, tm, n), lambda p, i: (p, i, 0))
    return _call(body, "ffn_in", (_sds((half, t, n), F32), _sds((half, t, n), F32), _sds((half, t, n), BF16)),
                 grid=(half, t // tm),
                 in_specs=[pl.BlockSpec((tm, d), lambda p, i: (i, 0)),
                           pl.BlockSpec((None, None, d, n), lambda p, i: (layer, p, 0, 0)),
                           pl.BlockSpec((None, None, d, n), lambda p, i: (layer, p + half, 0, 0))],
                 out_specs=(blk, blk, blk), vmem=VMEM_BIG)(h, w_in, w_in)


def _ffn_out(act, w_out, layer):
    half, t, n = act.shape
    d = w_out.shape[2]
    tm = min(TM, t)

    def body(a_ref, w_ref, o_ref):
        acc = _dot(a_ref[0], w_ref[0])
        for p in range(1, half):
            acc = acc + _dot(a_ref[p], w_ref[p])
        o_ref[...] = acc

    return _call(body, "ffn_out", _sds((t, d), F32), grid=(t // tm,),
                 in_specs=[pl.BlockSpec((half, tm, n), lambda i: (0, i, 0)),
                           pl.BlockSpec((half, n, d), lambda i: (layer // half, 0, 0))],
                 out_specs=pl.BlockSpec((tm, d), lambda i: (i, 0)), vmem=VMEM_BIG)(act, w_out)


def _ffn_dact(dy, w_out, hg, hu, layer):
    half, t, n = hg.shape
    d = dy.shape[1]
    tm = min(TMM, t)

    def body(dy_ref, w_ref, g_ref, u_ref, dg_ref, du_ref):
        da = _dot(dy_ref[...], w_ref[...], NT)
        g = g_ref[...]
        sg = jax.nn.sigmoid(g)
        dg_ref[...] = (da * u_ref[...] * (sg * (1.0 + g * (1.0 - sg)))).astype(BF16)
        du_ref[...] = (da * (g * sg)).astype(BF16)

    blk = pl.BlockSpec((None, tm, n), lambda p, i: (p, i, 0))
    dg, du = _call(body, "ffn_dact", (_sds((half, t, n), BF16), _sds((half, t, n), BF16)), grid=(half, t // tm),
                   in_specs=[pl.BlockSpec((tm, d), lambda p, i: (i, 0)),
                             pl.BlockSpec((None, n, d), lambda p, i: (layer + p, 0, 0)), blk, blk],
                   out_specs=(blk, blk), vmem=VMEM_BIG)(dy, w_out, hg, hu)
    return dg, du


def _ffn_bwd_weights(h, dy, act, dg, du):
    half, t, n = act.shape
    d = h.shape[1]
    tk = min(TMM, t)
    h_spec = pl.BlockSpec((tk, d), lambda r: (r, 0))
    g_spec = pl.BlockSpec((half, tk, n), lambda r: (0, r, 0))

    def in_terms(h_ref, g_ref):
        hv = h_ref[...]
        return [_dot(hv, g_ref[p], TN) for p in range(half)]

    def out_terms(a_ref, dy_ref):
        dyv = dy_ref[...]
        return [_dot(a_ref[p], dyv, TN) for p in range(half)]

    w_in_spec = pl.BlockSpec((half, d, n), lambda r: (0, 0, 0))
    dwg = _accumulate_over_tokens("ffn_dw_gate", (h, dg), [h_spec, g_spec], _sds((half, d, n), GRAD_WIRE), w_in_spec,
                                  (half, d, n), t // tk, in_terms)
    dwu = _accumulate_over_tokens("ffn_dw_up", (h, du), [h_spec, g_spec], _sds((half, d, n), GRAD_WIRE), w_in_spec,
                                  (half, d, n), t // tk, in_terms)
    dwo = _accumulate_over_tokens("ffn_dw_out", (act, dy), [g_spec, h_spec], _sds((half, n, d), GRAD_WIRE),
                                  pl.BlockSpec((half, n, d), lambda r: (0, 0, 0)), (half, n, d), t // tk, out_terms)
    return dwg, dwu, dwo


def _ffn_dh(dg, du, w_in, layer):
    half, t, n = dg.shape
    d = w_in.shape[2]
    tm = min(TM, t)

    def body(dg_ref, du_ref, w_ref, o_ref):
        acc = None
        for p in range(half):
            for ref, q in ((dg_ref, p), (du_ref, p + half)):
                term = _dot(ref[p], w_ref[q], NT)
                acc = term if acc is None else acc + term
        o_ref[...] = acc

    g_spec = pl.BlockSpec((half, tm, n), lambda i: (0, i, 0))
    return _call(body, "ffn_dh", _sds((t, d), F32), grid=(t // tm,),
                 in_specs=[g_spec, g_spec, pl.BlockSpec((None, 2 * half, d, n), lambda i: (layer, 0, 0, 0))],
                 out_specs=pl.BlockSpec((tm, d), lambda i: (i, 0)), vmem=VMEM_BIG)(dg, du, w_in)


def _tril(n, strict=False):
    r = lax.broadcasted_iota(jnp.int32, (n, n), 0)
    c = lax.broadcasted_iota(jnp.int32, (n, n), 1)
    return c < r if strict else c <= r


def _gm_spatial_fwd(pre, lng, lnb, w_s, b_st, seq):
    t, w2 = pre.shape
    w = w2 // 2
    gd = w // GM_GROUPS
    tm = min(TM, seq)
    nch = tm // GM_CHUNK

    def body(pre_ref, lng_ref, lnb_ref, ws_ref, bs_ref, y_ref):
        v = _gelu(pre_ref[:, w:])
        vc, rstd = _ln_stats(v)
        vn = (vc * rstd * lng_ref[...] + lnb_ref[...]).astype(BF16)
        keep = _tril(GM_CHUNK)
        for g in range(GM_GROUPS):
            wm = jnp.where(keep, ws_ref[g], 0.0).astype(BF16)
            for ci in range(nch):
                rows = slice(ci * GM_CHUNK, (ci + 1) * GM_CHUNK)
                cols = slice(g * gd, (g + 1) * gd)
                sv = _dot(wm, vn[rows, cols]) + bs_ref[:, g:g + 1]
                u = _gelu(pre_ref[rows, cols])
                y_ref[rows, cols] = (u * sv).astype(BF16)

    return _call(body, "gm_spatial_fwd", _sds((t, w), BF16), grid=(t // tm,),
                 in_specs=[_tile_spec(tm, w2), _vec_spec(w), _vec_spec(w),
                           pl.BlockSpec((GM_GROUPS, GM_CHUNK, GM_CHUNK), lambda i: (0, 0, 0)),
                           pl.BlockSpec((GM_CHUNK, GM_GROUPS), lambda i: (0, 0))],
                 out_specs=_tile_spec(tm, w), vmem=VMEM_BIG)(pre, lng, lnb, w_s, b_st)


def _gm_spatial_bwd(pre, dyv, lng, lnb, w_s, b_st, seq):
    t, w2 = pre.shape
    w = w2 // 2
    gd = w // GM_GROUPS
    tm = min(TM, seq)
    nch = tm // GM_CHUNK

    def body(pre_ref, dyv_ref, lng_ref, lnb_ref, ws_ref, bs_ref, dpre_ref, dws_ref, dbs_ref, dlg_ref, dlb_ref, dbin_ref, dvn_ref):
        @pl.when(pl.program_id(0) == 0)
        def _():
            dws_ref[...] = jnp.zeros_like(dws_ref)
            dbs_ref[...] = jnp.zeros_like(dbs_ref)
            dlg_ref[...] = jnp.zeros_like(dlg_ref)
            dlb_ref[...] = jnp.zeros_like(dlb_ref)
            dbin_ref[...] = jnp.zeros_like(dbin_ref)

        pv = pre_ref[:, w:]
        v = _gelu(pv)
        vc, rstd = _ln_stats(v)
        vhat = vc * rstd
        vn = (vhat * lng_ref[...] + lnb_ref[...]).astype(BF16)
        keep = _tril(GM_CHUNK)
        dbs_cols = []
        for g in range(GM_GROUPS):
            wm = jnp.where(keep, ws_ref[g], 0.0).astype(BF16)
            dwm = jnp.zeros((GM_CHUNK, GM_CHUNK), F32)
            dbs = jnp.zeros((GM_CHUNK, 1), F32)
            for ci in range(nch):
                rows = slice(ci * GM_CHUNK, (ci + 1) * GM_CHUNK)
                cols = slice(g * gd, (g + 1) * gd)
                vn_b = vn[rows, cols]
                sv = _dot(wm, vn_b) + bs_ref[:, g:g + 1]
                pu = pre_ref[rows, cols]
                dy = dyv_ref[rows, cols]
                du = dy * sv
                dsv = dy * _gelu(pu)
                dpu = du * _gelu_grad(pu)
                dpre_ref[rows, cols] = dpu.astype(BF16)
                dbin_ref[:, cols] += _colsum(dpu)
                dsv_b = dsv.astype(BF16)
                dwm = dwm + _dot(dsv_b, vn_b, NT)
                dbs = dbs + jnp.sum(dsv, axis=-1, keepdims=True)
                dvn_ref[rows, cols] = _dot(wm, dsv_b, TN)
            dws_ref[g] += jnp.where(keep, dwm, 0.0)
            dbs_cols.append(dbs)
        dbs_ref[...] += jnp.concatenate(dbs_cols, axis=1)
        dvn = dvn_ref[...]
        dlg_ref[...] += _colsum(dvn * vhat)
        dlb_ref[...] += _colsum(dvn)
        dvh = dvn * lng_ref[...]
        m1 = jnp.mean(dvh, axis=-1, keepdims=True)
        m2 = jnp.mean(dvh * vhat, axis=-1, keepdims=True)
        dv = rstd * (dvh - m1 - vhat * m2)
        dpv = dv * _gelu_grad(pv)
        dpre_ref[:, w:] = dpv.astype(BF16)
        dbin_ref[:, w:] += _colsum(dpv)

    full3 = pl.BlockSpec((GM_GROUPS, GM_CHUNK, GM_CHUNK), lambda i: (0, 0, 0))
    bst = pl.BlockSpec((GM_CHUNK, GM_GROUPS), lambda i: (0, 0))
    return _call(body, "gm_spatial_bwd",
                 (_sds((t, w2), BF16), _sds((GM_GROUPS, GM_CHUNK, GM_CHUNK), F32), _sds((GM_CHUNK, GM_GROUPS), F32),
                  _sds((1, w), F32), _sds((1, w), F32), _sds((1, w2), F32)),
                 grid=(t // tm,),
                 in_specs=[_tile_spec(tm, w2), _tile_spec(tm, w), _vec_spec(w), _vec_spec(w), full3, bst],
                 out_specs=(_tile_spec(tm, w2), full3, bst, _vec_spec(w), _vec_spec(w), _vec_spec(w2)),
                 scratch=[pltpu.VMEM((tm, w), F32)], vmem=VMEM_BIG)(pre, dyv, lng, lnb, w_s, b_st)


def _head_masks():
    lane = lax.broadcasted_iota(jnp.int32, (1, LANES), 1)
    return lane < HEAD_DIM


def _two_heads(x, m0):
    z = jnp.zeros_like(x)
    return jnp.where(m0, x, z), jnp.where(m0, z, x)


def _qkv_specs(seq, nq, blocked_q):
    if blocked_q:
        q = pl.BlockSpec((None, TQ_(seq), LANES), lambda b, p, i: (0, b * nq + i, p))
        k = pl.BlockSpec((None, seq, LANES), lambda b, p, i: (1, b, p))
        v = pl.BlockSpec((None, seq, LANES), lambda b, p, i: (2, b, p))
    else:
        q = pl.BlockSpec((None, seq, LANES), lambda b, p: (0, b, p))
        k = pl.BlockSpec((None, seq, LANES), lambda b, p: (1, b, p))
        v = pl.BlockSpec((None, seq, LANES), lambda b, p: (2, b, p))
    return q, k, v


def TQ_(seq):
    return min(TQ, seq)


def _fox_gate_fwd(ft, b_f, seq):
    nh, t = ft.shape
    nch = seq // LANES

    def body(ft_ref, bf_ref, fr_ref):
        r = lax.broadcasted_iota(jnp.int32, (LANES, LANES), 0)
        c = lax.broadcasted_iota(jnp.int32, (LANES, LANES), 1)
        upper = jnp.where(r <= c, 1.0, 0.0).astype(BF16)
        carry = jnp.zeros((nh, 1), F32)
        for ci in range(nch):
            cols = slice(ci * LANES, (ci + 1) * LANES)
            lf = _log_sigmoid(ft_ref[:, cols] + bf_ref[...])
            cs = _dot_exact(lf, upper) + carry
            fr_ref[:, cols] = cs
            carry = cs[:, LANES - 1:LANES]

    return _call(body, "fox_gate_fwd", _sds((nh, t), F32), grid=(t // seq,),
                 in_specs=[pl.BlockSpec((nh, seq), lambda b: (0, b)), pl.BlockSpec((nh, 1), lambda b: (0, 0))],
                 out_specs=pl.BlockSpec((nh, seq), lambda b: (0, b)))(ft, b_f)


def _fox_gate_bwd(ft, b_f, dfk, dfq, seq):
    nh, t = ft.shape
    nch = seq // LANES

    def body(ft_ref, bf_ref, dfk_ref, dfq_ref, dl_ref, db_ref):
        @pl.when(pl.program_id(0) == 0)
        def _():
            db_ref[...] = jnp.zeros_like(db_ref)

        r = lax.broadcasted_iota(jnp.int32, (LANES, LANES), 0)
        c = lax.broadcasted_iota(jnp.int32, (LANES, LANES), 1)
        lower = jnp.where(r >= c, 1.0, 0.0).astype(BF16)
        carry = jnp.zeros((nh, 1), F32)
        tot = jnp.zeros((nh, 1), F32)
        for ci in reversed(range(nch)):
            cols = slice(ci * LANES, (ci + 1) * LANES)
            rc = _dot_exact(dfk_ref[:, cols] + dfq_ref[:, cols], lower) + carry
            carry = rc[:, 0:1]
            dl = rc * jax.nn.sigmoid(-(ft_ref[:, cols] + bf_ref[...]))
            dl_ref[:, cols] = dl
            tot = tot + jnp.sum(dl, axis=-1, keepdims=True)
        db_ref[...] += tot

    blk = pl.BlockSpec((nh, seq), lambda b: (0, b))
    one = pl.BlockSpec((nh, 1), lambda b: (0, 0))
    return _call(body, "fox_gate_bwd", (_sds((nh, t), F32), _sds((nh, 1), F32)), grid=(t // seq,),
                 in_specs=[blk, one, blk, blk], out_specs=(blk, one))(ft, b_f, dfk, dfq)


def _sweep(step, n_off, unroll, init):
    def group(_, st):
        base, carry = st[0], st[1:]
        for u in range(unroll):
            carry = step(base + u, carry, False)
        return (base + unroll, *carry)

    def tail(r):
        def run(st):
            base, carry = st[0], st[1:]
            for u in range(r):
                carry = step(base + u, carry, False)
            return step(base + r, carry, True)
        return run

    def pick(idx, fns, st):
        if len(fns) == 1:
            return fns[0](st)
        half = len(fns) // 2
        return lax.cond(idx < half, lambda s: pick(idx, fns[:half], s), lambda s: pick(idx - half, fns[half:], s), st)

    st = lax.fori_loop(0, n_off // unroll, group, (jnp.int32(0), *init))
    return pick(n_off % unroll, [tail(r) for r in range(unroll)], st)


def _fox_fwd(qkv, frow, nb, seq):
    _, t, d = qkv.shape
    npair = d // LANES
    tq = TQ_(seq)
    nq = seq // tq
    scale = HEAD_DIM ** -0.5

    def body(q_ref, k_ref, v_ref, fr_ref, o_ref, lse_ref):
        m0 = _head_masks()
        qm = _two_heads(q_ref[...] * scale, m0)
        row = lax.broadcasted_iota(jnp.int32, (tq, tq), 0)
        col = lax.broadcasted_iota(jnp.int32, (tq, tq), 1)
        one = jnp.ones((tq, LANES), BF16)

        def step(j, carry, diag):
            off = pl.multiple_of(j * tq, tq)
            kb = k_ref[pl.ds(off, tq), :]
            vb = v_ref[pl.ds(off, tq), :]
            vv = (jnp.where(m0, vb, one), jnp.where(m0, one, vb))
            out = []
            for hh in range(2):
                m, acc = carry[2 * hh], carry[2 * hh + 1]
                s = lax.dot_general(qm[hh], kb, NT, preferred_element_type=F32) - fr_ref[hh:hh + 1, pl.ds(off, tq)]
                if diag:
                    s = jnp.where(col <= row, s, NEG_INF)
                mn = jnp.maximum(m, jnp.max(s, axis=-1, keepdims=True))
                p = jnp.exp(s - mn)
                out += [mn, jnp.exp(m - mn) * acc + jnp.dot(p.astype(BF16), vv[hh], preferred_element_type=F32)]
            return tuple(out)

        neg = jnp.full((tq, 1), NEG_INF, F32)
        zacc = jnp.zeros((tq, LANES), F32)
        m_a, acc_a, m_b, acc_b = _sweep(step, pl.program_id(2), FOX_FWD_UNROLL, (neg, zacc, neg, zacc))
        l_a = pltpu.roll(acc_a, HEAD_DIM, 1)
        l_b = pltpu.roll(acc_b, HEAD_DIM, 1)
        o_ref[...] = jnp.where(m0, acc_a / l_a, acc_b / l_b)
        lse_ref[:, 0:1] = m_a + jnp.log(l_a[:, 0:1])
        lse_ref[:, 1:2] = m_b + jnp.log(l_b[:, HEAD_DIM:HEAD_DIM + 1])

    q_spec, k_spec, v_spec = _qkv_specs(seq, nq, True)
    col_spec = pl.BlockSpec((None, tq, 2), lambda b, p, i: (p, b * nq + i, 0))
    return _call(body, "fox_fwd", (_sds((t, d), F32), _sds((npair, t, 2), F32)), grid=(nb, npair, nq),
                 in_specs=[q_spec, k_spec, v_spec, pl.BlockSpec((None, 2, seq), lambda b, p, i: (p, 0, b))],
                 out_specs=(pl.BlockSpec((tq, LANES), lambda b, p, i: (b * nq + i, p)), col_spec),
                 vmem=VMEM_BIG)(qkv, qkv, qkv, frow)


def _fox_bwd(qkv, frow, o, do, lse, nb, seq):
    _, t, d = qkv.shape
    npair = d // LANES
    tq = TQ_(seq)
    nq = seq // tq
    scale = HEAD_DIM ** -0.5

    def body(q_ref, k_ref, v_ref, fr_ref, o_ref, do_ref, lse_ref, dqkv_ref, df_ref, dfq_ref, dk_acc, dv_acc):
        m0 = _head_masks()
        row = lax.broadcasted_iota(jnp.int32, (tq, tq), 0)
        col = lax.broadcasted_iota(jnp.int32, (tq, tq), 1)
        dk_acc[...] = jnp.zeros_like(dk_acc)
        dv_acc[...] = jnp.zeros_like(dv_acc)
        df_ref[...] = jnp.zeros_like(df_ref)

        def q_block(i, _):
            qoff = pl.multiple_of(i * tq, tq)
            qrows = pl.ds(qoff, tq)
            qm = _two_heads(q_ref[qrows, :] * scale, m0)
            dov = do_ref[qrows, :]
            dd = dov * o_ref[qrows, :]
            dm = _two_heads(dov.astype(BF16), m0)
            delta = (jnp.sum(jnp.where(m0, dd, 0.0), axis=-1, keepdims=True),
                     jnp.sum(jnp.where(m0, 0.0, dd), axis=-1, keepdims=True))
            ls = (lse_ref[qrows, 0:1], lse_ref[qrows, 1:2])

            def step(j, carry, diag):
                off = pl.multiple_of(j * tq, tq)
                krows = pl.ds(off, tq)
                kb = k_ref[krows, :]
                vb = v_ref[krows, :]
                dqs, rowsums = [], []
                dk = jnp.zeros((tq, LANES), F32)
                dv = jnp.zeros((tq, LANES), F32)
                for hh in range(2):
                    s = lax.dot_general(qm[hh], kb, NT, preferred_element_type=F32) - fr_ref[hh:hh + 1, krows]
                    if diag:
                        s = jnp.where(col <= row, s, NEG_INF)
                    p = jnp.exp(s - ls[hh])
                    dp = lax.dot_general(dm[hh], vb, NT, preferred_element_type=F32)
                    ds = p * (dp - delta[hh])
                    df_ref[hh:hh + 1, krows] -= _colsum(ds)
                    rowsums.append(carry[1 + hh] + jnp.sum(ds, axis=-1, keepdims=True))
                    ds_b = ds.astype(BF16)
                    dqs.append(jnp.dot(ds_b, kb, preferred_element_type=F32))
                    dk = dk + lax.dot_general(ds_b, qm[hh], TN, preferred_element_type=F32)
                    dv = dv + lax.dot_general(p.astype(BF16), dm[hh], TN, preferred_element_type=F32)
                dk_acc[krows, :] += dk
                dv_acc[krows, :] += dv
                return (carry[0] + jnp.where(m0, dqs[0], dqs[1]), *rowsums)

            zero = jnp.zeros((tq, 1), F32)
            dq, rs_a, rs_b = _sweep(step, i, FOX_BWD_UNROLL, (jnp.zeros((tq, LANES), F32), zero, zero))
            dqkv_ref[0, qrows, :] = (dq * scale).astype(BF16)
            dfq_ref[qrows, 0:1] = rs_a
            dfq_ref[qrows, 1:2] = rs_b
            return 0

        lax.fori_loop(0, nq, q_block, 0)
        dqkv_ref[1] = dk_acc[...].astype(BF16)
        dqkv_ref[2] = dv_acc[...].astype(BF16)

    q_spec, k_spec, v_spec = _qkv_specs(seq, nq, False)
    col_spec = pl.BlockSpec((None, seq, 2), lambda b, p: (p, b, 0))
    row_spec = pl.BlockSpec((None, 2, seq), lambda b, p: (p, 0, b))
    tile = pl.BlockSpec((seq, LANES), lambda b, p: (b, p))
    return _call(body, "fox_bwd", (_sds((3, t, d), BF16), _sds((npair, 2, t), F32), _sds((npair, t, 2), F32)), grid=(nb, npair),
                 in_specs=[q_spec, k_spec, v_spec, row_spec, tile, tile, col_spec],
                 out_specs=(pl.BlockSpec((3, seq, LANES), lambda b, p: (0, b, p)), row_spec, col_spec),
                 scratch=[pltpu.VMEM((seq, LANES), F32), pltpu.VMEM((seq, LANES), F32)],
                 vmem=VMEM_BIG)(qkv, qkv, qkv, frow, o, do, lse)


def _split2(x):
    hi = x.astype(BF16)
    return hi, (x - hi.astype(F32)).astype(BF16)


def _sum_right(x, tri):
    hi, lo = _split2(x)
    return jnp.dot(hi, tri, preferred_element_type=F32) + jnp.dot(lo, tri, preferred_element_type=F32)


def _sb_scores(qm_h, kb, scale, mask):
    z = lax.dot_general(qm_h, kb, NT, preferred_element_type=F32) * scale
    lb = _log_sigmoid(z)
    l1m = lb - z
    if mask is not None:
        l1m = jnp.where(mask, l1m, 0.0)
    return lb, l1m


def _sb_fwd(qkv, nb, seq):
    _, t, d = qkv.shape
    npair = d // LANES
    tq = TQ_(seq)
    nq = seq // tq
    scale = HEAD_DIM ** -0.5

    def body(q_ref, k_ref, v_ref, o_ref, lt_ref):
        i = pl.program_id(2)
        m0 = _head_masks()
        qm = _two_heads(q_ref[...], m0)
        row = lax.broadcasted_iota(jnp.int32, (tq, tq), 0)
        col = lax.broadcasted_iota(jnp.int32, (tq, tq), 1)
        after = jnp.where(row > col, 1.0, 0.0).astype(BF16)

        def step(j, carry, diag):
            off = pl.multiple_of(j * tq, tq)
            kb = k_ref[pl.ds(off, tq), :]
            vb = v_ref[pl.ds(off, tq), :]
            mask = (col < row) if diag else None
            nxt, parts = [], []
            for hh in range(2):
                lb, l1m = _sb_scores(qm[hh], kb, scale, mask)
                rest = _sum_right(l1m, after) + carry[hh]
                a = jnp.exp(lb + rest)
                if diag:
                    a = jnp.where(mask, a, 0.0)
                parts.append(jnp.dot(a.astype(BF16), vb, preferred_element_type=F32))
                nxt.append(carry[hh] + jnp.sum(l1m, axis=-1, keepdims=True))
            return (*nxt, carry[2] + jnp.where(m0, parts[0], parts[1]))

        zero = jnp.zeros((tq, 1), F32)
        carry = step(i, (zero, zero, jnp.zeros((tq, LANES), F32)), True)

        def alive(st):
            return (st[0] < i) & (jnp.max(jnp.maximum(st[1], st[2])) > SB_DEAD)

        def more(st):
            return (st[0] + 1, *step(i - 1 - st[0], st[1:], False))

        done, lt_a, lt_b, acc = lax.while_loop(alive, more, (jnp.int32(0), *carry))
        o_ref[...] = acc
        lt_ref[:, 0:1] = lt_a
        lt_ref[:, 1:2] = lt_b
        lt_ref[:, 2:3] = jnp.zeros((tq, 1), F32) + done.astype(F32)
        lt_ref[:, 3:4] = zero

    q_spec, k_spec, v_spec = _qkv_specs(seq, nq, True)
    return _call(body, "sb_fwd", (_sds((t, d), F32), _sds((npair, t, 4), F32)), grid=(nb, npair, nq),
                 in_specs=[q_spec, k_spec, v_spec],
                 out_specs=(pl.BlockSpec((tq, LANES), lambda b, p, i: (b * nq + i, p)),
                            pl.BlockSpec((None, tq, 4), lambda b, p, i: (p, b * nq + i, 0))), vmem=VMEM_BIG)(qkv, qkv, qkv)


def _sb_bwd(qkv, do, ltot, nb, seq):
    _, t, d = qkv.shape
    npair = d // LANES
    tq = TQ_(seq)
    nq = seq // tq
    scale = HEAD_DIM ** -0.5

    def body(q_ref, k_ref, v_ref, do_ref, lt_ref, dqkv_ref, dk_acc, dv_acc):
        m0 = _head_masks()
        row = lax.broadcasted_iota(jnp.int32, (tq, tq), 0)
        col = lax.broadcasted_iota(jnp.int32, (tq, tq), 1)
        upto = jnp.where(row <= col, 1.0, 0.0).astype(BF16)
        left_of = jnp.where(row < col, 1.0, 0.0).astype(BF16)
        dk_acc[...] = jnp.zeros_like(dk_acc)
        dv_acc[...] = jnp.zeros_like(dv_acc)

        def q_block(i, _):
            qoff = pl.multiple_of(i * tq, tq)
            qrows = pl.ds(qoff, tq)
            qm = _two_heads(q_ref[qrows, :], m0)
            dm = _two_heads(do_ref[qrows, :].astype(BF16), m0)
            ltot = (lt_ref[qrows, 0:1], lt_ref[qrows, 1:2])

            def step(j, carry, diag):
                off = pl.multiple_of(j * tq, tq)
                krows = pl.ds(off, tq)
                kb = k_ref[krows, :]
                vb = v_ref[krows, :]
                mask = (col < row) if diag else None
                nxt, dqs = [], []
                dk = jnp.zeros((tq, LANES), F32)
                dv = jnp.zeros((tq, LANES), F32)
                for hh in range(2):
                    cl, ce = carry[2 * hh], carry[2 * hh + 1]
                    lb, l1m = _sb_scores(qm[hh], kb, scale, mask)
                    a = jnp.exp(lb + (ltot[hh] - (_sum_right(l1m, upto) + cl)))
                    if diag:
                        a = jnp.where(mask, a, 0.0)
                    e = lax.dot_general(dm[hh], vb, NT, preferred_element_type=F32) * a
                    before = _sum_right(e, left_of) + ce
                    beta = jnp.exp(lb)
                    dz = e * (1.0 - beta) - before * beta
                    if diag:
                        dz = jnp.where(mask, dz, 0.0)
                    dz_b = dz.astype(BF16)
                    dqs.append(jnp.dot(dz_b, kb, preferred_element_type=F32))
                    dk = dk + lax.dot_general(dz_b, qm[hh], TN, preferred_element_type=F32)
                    dv = dv + lax.dot_general(a.astype(BF16), dm[hh], TN, preferred_element_type=F32)
                    nxt += [cl + jnp.sum(l1m, axis=-1, keepdims=True), ce + jnp.sum(e, axis=-1, keepdims=True)]
                dk_acc[krows, :] += dk
                dv_acc[krows, :] += dv
                return (*nxt, carry[4] + jnp.where(m0, dqs[0], dqs[1]))

            zero = jnp.zeros((tq, 1), F32)
            carry = (zero, zero, zero, zero, jnp.zeros((tq, LANES), F32))
            first = i - jnp.max(lt_ref[qrows, 2:3]).astype(jnp.int32)
            carry = lax.fori_loop(first, i, lambda j, c: step(j, c, False), carry)
            carry = step(i, carry, True)
            dqkv_ref[0, qrows, :] = (carry[4] * scale).astype(BF16)
            return 0

        lax.fori_loop(0, nq, q_block, 0)
        dqkv_ref[1] = (dk_acc[...] * scale).astype(BF16)
        dqkv_ref[2] = dv_acc[...].astype(BF16)

    q_spec, k_spec, v_spec = _qkv_specs(seq, nq, False)
    tile = pl.BlockSpec((seq, LANES), lambda b, p: (b, p))
    return _call(body, "sb_bwd", _sds((3, t, d), BF16), grid=(nb, npair),
                 in_specs=[q_spec, k_spec, v_spec, tile, pl.BlockSpec((None, seq, 4), lambda b, p: (p, b, 0))],
                 out_specs=pl.BlockSpec((3, seq, LANES), lambda b, p: (0, b, p)),
                 scratch=[pltpu.VMEM((seq, LANES), F32), pltpu.VMEM((seq, LANES), F32)], vmem=VMEM_BIG)(qkv, qkv, qkv, do, ltot)


def _qkv_proj(name, h, w):
    t, d = h.shape
    tm = min(TMM, t)
    return _mm(name, h, w, _sds((3, t, d), BF16), (3, t // tm, 1),
               pl.BlockSpec((tm, d), lambda s, i, r: (i, 0)), pl.BlockSpec((d, d), lambda s, i, r: (0, s)),
               pl.BlockSpec((None, tm, d), lambda s, i, r: (s, i, 0)), NN)


def _qkv_dw(name, h, dqkv):
    t, d = h.shape
    tk = min(TMM, t)

    def terms(h_ref, g_ref):
        hv = h_ref[...]
        return [_dot(hv, g_ref[s], TN) for s in range(3)]

    def store(o_ref, acc_ref):
        for s in range(3):
            o_ref[:, s * d:(s + 1) * d] = acc_ref[s].astype(o_ref.dtype)

    return _accumulate_over_tokens(name, (h, dqkv), [pl.BlockSpec((tk, d), lambda r: (r, 0)), pl.BlockSpec((3, tk, d), lambda r: (0, r, 0))],
                                   _sds((d, 3 * d), GRAD_WIRE), pl.BlockSpec((d, 3 * d), lambda r: (0, 0)), (3, d, d), t // tk, terms, store)


def _qkv_dh(name, dqkv, w):
    _, t, d = dqkv.shape
    tm = min(TM, t)

    def body(g_ref, w_ref, o_ref):
        acc = None
        for s in range(3):
            term = _dot(g_ref[s], w_ref[:, s * d:(s + 1) * d], NT)
            acc = term if acc is None else acc + term
        o_ref[...] = acc

    return _call(body, name, _sds((t, d), F32), grid=(t // tm,),
                 in_specs=[pl.BlockSpec((3, tm, d), lambda i: (0, i, 0)), pl.BlockSpec((d, 3 * d), lambda i: (0, 0))],
                 out_specs=pl.BlockSpec((tm, d), lambda i: (i, 0)), vmem=VMEM_BIG)(dqkv, w)


def _glu(pre_block, d):
    return pre_block[:, :d] * jax.nn.sigmoid(pre_block[:, d:])


def _cv_conv_fwd(pre, dw, dwb, lng, lnb, seq):
    t, d2 = pre.shape
    d = d2 // 2
    tt = min(TT, seq)
    nt = seq // tt
    hb = tt // CONV_HALO

    def body(pre_ref, halo_ref, dw_ref, dwb_ref, lng_ref, lnb_ref, y1_ref, y2_ref, ext_ref):
        i = pl.program_id(1)
        ext_ref[0:CONV_HALO, :] = jnp.where(i == 0, 0.0, _glu(halo_ref[...], d))
        ext_ref[CONV_HALO:, :] = _glu(pre_ref[...], d)
        acc = jnp.zeros((tt, d), F32) + dwb_ref[...]
        for k in range(CONV_WIDTH):
            acc = acc + ext_ref[pl.ds(CONV_HALO - (CONV_WIDTH - 1) + k, tt), :] * dw_ref[k:k + 1, :]
        y1_ref[...] = acc
        yc, rstd = _ln_stats(acc)
        y2_ref[...] = _silu(yc * rstd * lng_ref[...] + lnb_ref[...]).astype(BF16)

    vec = pl.BlockSpec((1, d), lambda b, i: (0, 0))
    tile = pl.BlockSpec((tt, d), lambda b, i: (b * nt + i, 0))
    return _call(body, "cv_conv_fwd", (_sds((t, d), F32), _sds((t, d), BF16)), grid=(t // seq, nt),
                 in_specs=[pl.BlockSpec((tt, d2), lambda b, i: (b * nt + i, 0)),
                           pl.BlockSpec((CONV_HALO, d2), lambda b, i: (jnp.maximum((b * nt + i) * hb - 1, 0), 0)),
                           pl.BlockSpec((CONV_HALO, d), lambda b, i: (0, 0)), vec, vec, vec],
                 out_specs=(tile, tile), scratch=[pltpu.VMEM((tt + CONV_HALO, d), F32)], vmem=VMEM_BIG)(pre, pre, dw, dwb, lng, lnb)


def _cv_norm_bwd(y1, dy2, lng, lnb):
    t, d = y1.shape
    tm = min(TM, t)

    def body(y1_ref, dy2_ref, lng_ref, lnb_ref, dy1_ref, dlg_ref, dlb_ref, dsum_ref):
        @pl.when(pl.program_id(0) == 0)
        def _():
            dlg_ref[...] = jnp.zeros_like(dlg_ref)
            dlb_ref[...] = jnp.zeros_like(dlb_ref)
            dsum_ref[...] = jnp.zeros_like(dsum_ref)

        yc, rstd = _ln_stats(y1_ref[...])
        yhat = yc * rstd
        n = yhat * lng_ref[...] + lnb_ref[...]
        sg = jax.nn.sigmoid(n)
        dn = dy2_ref[...] * (sg * (1.0 + n * (1.0 - sg)))
        dlg_ref[...] += _colsum(dn * yhat)
        dlb_ref[...] += _colsum(dn)
        dyh = dn * lng_ref[...]
        m1 = jnp.mean(dyh, axis=-1, keepdims=True)
        m2 = jnp.mean(dyh * yhat, axis=-1, keepdims=True)
        dy1 = rstd * (dyh - m1 - yhat * m2)
        dy1_ref[...] = dy1
        dsum_ref[...] += _colsum(dy1)

    return _call(body, "cv_norm_bwd", (_sds((t, d), F32), _sds((1, d), F32), _sds((1, d), F32), _sds((1, d), F32)),
                 grid=(t // tm,), in_specs=[_tile_spec(tm, d), _tile_spec(tm, d), _vec_spec(d), _vec_spec(d)],
                 out_specs=(_tile_spec(tm, d), _vec_spec(d), _vec_spec(d), _vec_spec(d)))(y1, dy2, lng, lnb)


def _cv_conv_bwd(pre, dy1, dw, seq):
    t, d2 = pre.shape
    d = d2 // 2
    tt = min(TT, seq)
    nt = seq // tt
    hb = tt // CONV_HALO
    last_halo = t // CONV_HALO - 1

    def body(pre_ref, halo_ref, dy_ref, dyn_ref, dw_ref, dpre_ref, ddw_ref, dbin_ref, ext_ref, dext_ref):
        b, i = pl.program_id(0), pl.program_id(1)

        @pl.when((b == 0) & (i == 0))
        def _():
            ddw_ref[...] = jnp.zeros_like(ddw_ref)
            dbin_ref[...] = jnp.zeros_like(dbin_ref)

        pv = pre_ref[...]
        ext_ref[0:CONV_HALO, :] = jnp.where(i == 0, 0.0, _glu(halo_ref[...], d))
        ext_ref[CONV_HALO:, :] = _glu(pv, d)
        dyv = dy_ref[...]
        dext_ref[0:tt, :] = dyv
        dext_ref[tt:, :] = jnp.where(i == nt - 1, 0.0, dyn_ref[...])
        dy0 = jnp.zeros((tt, d), F32)
        for k in range(CONV_WIDTH):
            ddw_ref[k:k + 1, :] += _colsum(dyv * ext_ref[pl.ds(CONV_HALO - (CONV_WIDTH - 1) + k, tt), :])
            dy0 = dy0 + dext_ref[pl.ds(CONV_WIDTH - 1 - k, tt), :] * dw_ref[k:k + 1, :]
        a = pv[:, :d]
        sg = jax.nn.sigmoid(pv[:, d:])
        da = dy0 * sg
        dg = dy0 * a * sg * (1.0 - sg)
        dpre_ref[:, :d] = da.astype(BF16)
        dpre_ref[:, d:] = dg.astype(BF16)
        dbin_ref[:, :d] += _colsum(da)
        dbin_ref[:, d:] += _colsum(dg)

    return _call(body, "cv_conv_bwd", (_sds((t, d2), BF16), _sds((CONV_HALO, d), F32), _sds((1, d2), F32)), grid=(t // seq, nt),
                 in_specs=[pl.BlockSpec((tt, d2), lambda b, i: (b * nt + i, 0)),
                           pl.BlockSpec((CONV_HALO, d2), lambda b, i: (jnp.maximum((b * nt + i) * hb - 1, 0), 0)),
                           pl.BlockSpec((tt, d), lambda b, i: (b * nt + i, 0)),
                           pl.BlockSpec((CONV_HALO, d), lambda b, i: (jnp.minimum((b * nt + i + 1) * hb, last_halo), 0)),
                           pl.BlockSpec((CONV_HALO, d), lambda b, i: (0, 0))],
                 out_specs=(pl.BlockSpec((tt, d2), lambda b, i: (b * nt + i, 0)),
                            pl.BlockSpec((CONV_HALO, d), lambda b, i: (0, 0)), pl.BlockSpec((1, d2), lambda b, i: (0, 0))),
                 scratch=[pltpu.VMEM((tt + CONV_HALO, d), F32), pltpu.VMEM((tt + CONV_HALO, d), F32)],
                 vmem=VMEM_BIG)(pre, pre, dy1, dy1, dw)


def _adamw(name, w, m, v, g=None, parts=None):
    rows, cols = w.shape
    tr = rows
    for cand in ((512,) if parts is None else ()) + (256, 128, 64, 32, 16, 8):
        if rows % cand == 0 and rows > cand:
            tr = cand
            break
    bc1 = 1.0 - ADAM_B1 ** ADAM_STEP
    bc2 = 1.0 - ADAM_B2 ** ADAM_STEP

    def body(w_ref, m_ref, v_ref, g_ref, go_ref, d_ref, mo_ref, vo_ref):
        if parts is None:
            gv = g_ref[...]
        else:
            gv = g_ref[0].astype(F32)
            for s in range(1, N_DEV):
                gv = gv + g_ref[s].astype(F32)
        mn = ADAM_B1 * m_ref[...] + (1.0 - ADAM_B1) * gv
        vn = ADAM_B2 * v_ref[...] + (1.0 - ADAM_B2) * (gv * gv)
        m_hat = mn / bc1
        v_hat = vn / bc2
        go_ref[...] = gv
        d_ref[...] = -ADAM_LR * (m_hat / (jnp.sqrt(v_hat) + ADAM_EPS) + ADAM_WD * w_ref[...])
        mo_ref[...] = mn
        vo_ref[...] = vn

    blk = pl.BlockSpec((tr, cols), lambda i: (i, 0))
    g_in, g_spec = (g, blk) if parts is None else (parts, pl.BlockSpec((N_DEV, tr, cols), lambda i: (0, i, 0)))
    out = _sds((rows, cols), F32)
    return _call(body, name, (out, out, out, out), grid=(rows // tr,), in_specs=[blk, blk, blk, g_spec],
                 out_specs=(blk, blk, blk, blk), vmem=VMEM_BIG)(w, m, v, g_in)


def _pad_rows(a, rows):
    return jnp.pad(a, ((0, rows - a.shape[0]), (0, 0)))


def _full_cols(gathered, n):
    k = gathered.shape[2]
    return jnp.transpose(gathered[0], (1, 0, 2)).reshape(k, N_DEV * n)


def _col_blocks(full, n):
    k = full.shape[0]
    return jnp.transpose(full.reshape(k, N_DEV, n), (1, 0, 2))[None]


def kernel(x, c, mod_w, mod_b, ln1_g, ln1_b, ln2_g, ln2_b, ffn_w_in, ffn_w_out, gm_w_in, gm_b_in, gm_ln_g, gm_ln_b, gm_w_s, gm_b_s, gm_w_out, fox_w_in, fox_b_f, fox_w_out, sb_w_in, sb_w_out, cv_w_in, cv_b_in, cv_dw, cv_dw_b, cv_ln_g, cv_ln_b, cv_w_out, cv_b_out, loss_target, m_mod_w, m_mod_b, m_ln1_g, m_ln1_b, m_ln2_g, m_ln2_b, m_ffn_w_in, m_ffn_w_out, m_gm_w_in, m_gm_b_in, m_gm_ln_g, m_gm_ln_b, m_gm_w_s, m_gm_b_s, m_gm_w_out, m_fox_w_in, m_fox_b_f, m_fox_w_out, m_sb_w_in, m_sb_w_out, m_cv_w_in, m_cv_b_in, m_cv_dw, m_cv_dw_b, m_cv_ln_g, m_cv_ln_b, m_cv_w_out, m_cv_b_out, v_mod_w, v_mod_b, v_ln1_g, v_ln1_b, v_ln2_g, v_ln2_b, v_ffn_w_in, v_ffn_w_out, v_gm_w_in, v_gm_b_in, v_gm_ln_g, v_gm_ln_b, v_gm_w_s, v_gm_b_s, v_gm_w_out, v_fox_w_in, v_fox_b_f, v_fox_w_out, v_sb_w_in, v_sb_w_out, v_cv_w_in, v_cv_b_in, v_cv_dw, v_cv_dw_b, v_cv_ln_g, v_cv_ln_b, v_cv_w_out, v_cv_b_out):
    weights = dict(mod_w=mod_w, mod_b=mod_b, ln1_g=ln1_g, ln1_b=ln1_b, ln2_g=ln2_g, ln2_b=ln2_b, ffn_w_in=ffn_w_in, ffn_w_out=ffn_w_out, gm_w_in=gm_w_in, gm_b_in=gm_b_in, gm_ln_g=gm_ln_g, gm_ln_b=gm_ln_b, gm_w_s=gm_w_s, gm_b_s=gm_b_s, gm_w_out=gm_w_out, fox_w_in=fox_w_in, fox_b_f=fox_b_f, fox_w_out=fox_w_out, sb_w_in=sb_w_in, sb_w_out=sb_w_out, cv_w_in=cv_w_in, cv_b_in=cv_b_in, cv_dw=cv_dw, cv_dw_b=cv_dw_b, cv_ln_g=cv_ln_g, cv_ln_b=cv_ln_b, cv_w_out=cv_w_out, cv_b_out=cv_b_out)
    mom1 = dict(mod_w=m_mod_w, mod_b=m_mod_b, ln1_g=m_ln1_g, ln1_b=m_ln1_b, ln2_g=m_ln2_g, ln2_b=m_ln2_b, ffn_w_in=m_ffn_w_in, ffn_w_out=m_ffn_w_out, gm_w_in=m_gm_w_in, gm_b_in=m_gm_b_in, gm_ln_g=m_gm_ln_g, gm_ln_b=m_gm_ln_b, gm_w_s=m_gm_w_s, gm_b_s=m_gm_b_s, gm_w_out=m_gm_w_out, fox_w_in=m_fox_w_in, fox_b_f=m_fox_b_f, fox_w_out=m_fox_w_out, sb_w_in=m_sb_w_in, sb_w_out=m_sb_w_out, cv_w_in=m_cv_w_in, cv_b_in=m_cv_b_in, cv_dw=m_cv_dw, cv_dw_b=m_cv_dw_b, cv_ln_g=m_cv_ln_g, cv_ln_b=m_cv_ln_b, cv_w_out=m_cv_w_out, cv_b_out=m_cv_b_out)
    mom2 = dict(mod_w=v_mod_w, mod_b=v_mod_b, ln1_g=v_ln1_g, ln1_b=v_ln1_b, ln2_g=v_ln2_g, ln2_b=v_ln2_b, ffn_w_in=v_ffn_w_in, ffn_w_out=v_ffn_w_out, gm_w_in=v_gm_w_in, gm_b_in=v_gm_b_in, gm_ln_g=v_gm_ln_g, gm_ln_b=v_gm_ln_b, gm_w_s=v_gm_w_s, gm_b_s=v_gm_b_s, gm_w_out=v_gm_w_out, fox_w_in=v_fox_w_in, fox_b_f=v_fox_b_f, fox_w_out=v_fox_w_out, sb_w_in=v_sb_w_in, sb_w_out=v_sb_w_out, cv_w_in=v_cv_w_in, cv_b_in=v_cv_b_in, cv_dw=v_cv_dw, cv_dw_b=v_cv_dw_b, cv_ln_g=v_cv_ln_g, cv_ln_b=v_cv_ln_b, cv_w_out=v_cv_w_out, cv_b_out=v_cv_b_out)
    names = list(weights)

    nb, seq, d = x.shape
    t = nb * seq
    nl = mod_w.shape[0]
    alpha = (2.0 * nl) ** 0.25
    me = 4 * lax.axis_index("x") + 2 * lax.axis_index("y") + lax.axis_index("c")
    xs = x.reshape(t, d)
    tgt = loss_target.reshape(t, d)
    n_mod = mod_w.shape[2]
    n_ffn = ffn_w_in.shape[2]
    n_heads = d // HEAD_DIM
    npair = d // LANES

    c_all = _exchange_small(_pad_rows(c, 8), "gather_c", False)[:, :nb].reshape(N_DEV * nb, d)
    mod_b_loc = lax.dynamic_slice_in_dim(mod_b, me * n_mod, n_mod, axis=1)[:, None, :]
    mod_loc = _mod_fwd(c_all, mod_w, mod_b_loc)
    mod_g = _exchange_small(mod_loc.reshape(nl * N_DEV * nb, n_mod), "gather_mod", False)
    mod_all = jnp.transpose(mod_g.reshape(N_DEV, nl, N_DEV * nb, n_mod), (1, 2, 0, 3)).reshape(nl, N_DEV * nb, N_DEV * n_mod)
    mod_me = lax.dynamic_slice_in_dim(mod_all, me * nb, nb, axis=1)
    mods = [[mod_me[l, :, k * d:(k + 1) * d][:, None, :] for k in range(6)] for l in range(nl)]

    big = ["ffn_w_in", "ffn_w_out", "gm_w_in", "gm_w_out", "fox_w_in", "fox_w_out", "sb_w_in", "sb_w_out", "cv_w_in", "cv_w_out"]
    gathered = dict(zip(big, _gather_weights([weights[n].astype(BF16) for n in big])))
    w_ffn_in = gathered["ffn_w_in"]
    w_ffn_out = gathered["ffn_w_out"].reshape(nl, N_DEV // 2, n_ffn, d)
    w_ffn_out_rows = w_ffn_out.reshape(nl * (N_DEV // 2), n_ffn, d)
    sq = lambda n: gathered[n].reshape(d, d)
    fox_full = _full_cols(gathered["fox_w_in"], fox_w_in.shape[2])
    fox_qkv_w, fox_f_wt = fox_full[:, :3 * d], jnp.transpose(fox_full[:, 3 * d:])
    sb_qkv_w = _full_cols(gathered["sb_w_in"], sb_w_in.shape[2])
    cvp = d // N_DEV
    cv_small = jnp.concatenate([_pad_rows(cv_dw[0], CONV_HALO), cv_dw_b, cv_ln_g, cv_ln_b, cv_b_out,
                                cv_b_in.reshape(2, cvp), jnp.zeros((2, cvp), F32)], axis=0)
    cv_all = _exchange_small(cv_small, "gather_cv_small", False)
    cv_rows = jnp.transpose(cv_all, (1, 0, 2)).reshape(cv_small.shape[0], d)
    cv_dw_f, cv_dwb_f, cv_lng_f, cv_lnb_f, cv_bout_f = (cv_rows[:CONV_HALO], cv_rows[32:33], cv_rows[33:34], cv_rows[34:35], cv_rows[35:36])
    cv_bin_f = cv_all[:, 36:38, :].reshape(1, 2 * d)

    saved = []
    h = _modulate(xs, mods[0][1], mods[0][0], seq)
    xin = xs
    for l in range(nl):
        kind = l % 4
        sv = dict(x=xin, h=h)
        if kind == 0:
            pre = _proj_cols("gm_in", h, gathered["gm_w_in"], gm_w_in.shape[2], bias=gm_b_in)
            yv = _gm_spatial_fwd(pre, gm_ln_g, gm_ln_b, gm_w_s[0], jnp.transpose(gm_b_s[0]), seq)
            y = _mm_plain("gm_out", yv, sq("gm_w_out"), NN)
            sv.update(pre=pre, yv=yv)
        elif kind == 1:
            qkv = _qkv_proj("fox_qkv", h, fox_qkv_w)
            ft = _mm("fox_gate_proj", fox_f_wt, h, _sds((n_heads, t), F32), (t // min(TMM, t), 1),
                     pl.BlockSpec((n_heads, d), lambda i, r: (0, 0)), pl.BlockSpec((min(TMM, t), d), lambda i, r: (i, 0)),
                     pl.BlockSpec((n_heads, min(TMM, t)), lambda i, r: (0, i)), NT)
            b_f = jnp.transpose(fox_b_f)
            frow_p = _fox_gate_fwd(ft, b_f, seq).reshape(npair, 2, t)
            o, lse = _fox_fwd(qkv, frow_p, nb, seq)
            y = _mm_plain("fox_out", o, sq("fox_w_out"), NN)
            sv.update(qkv=qkv, ft=ft, b_f=b_f, frow=frow_p, o=o, lse=lse)
        elif kind == 2:
            qkv = _qkv_proj("sb_qkv", h, sb_qkv_w)
            o, ltot = _sb_fwd(qkv, nb, seq)
            y = _mm_plain("sb_out", o, sq("sb_w_out"), NN)
            sv.update(qkv=qkv, o=o, ltot=ltot)
        else:
            pre = _proj_cols("cv_in", h, gathered["cv_w_in"], cv_w_in.shape[2], bias=cv_bin_f)
            y1, y2 = _cv_conv_fwd(pre, cv_dw_f, cv_dwb_f, cv_lng_f, cv_lnb_f, seq)
            y = _mm_plain("cv_out", y2, sq("cv_w_out"), NN, bias=cv_bout_f)
            sv.update(pre=pre, y1=y1, y2=y2)
        x1, h2 = _lnres_fwd(xin, y, mods[l][2], ln1_g[l:l + 1], ln1_b[l:l + 1], alpha, seq, nxt=(mods[l][4], mods[l][3]))
        hg, hu, act = _ffn_in(h2, w_ffn_in, l)
        y2f = _ffn_out(act, w_ffn_out_rows, l * (N_DEV // 2))
        sv.update(y=y, x1=x1, h2=h2, hg=hg, hu=hu, act=act, y2f=y2f)
        if l + 1 < nl:
            xin, h = _lnres_fwd(x1, y2f, mods[l][5], ln2_g[l:l + 1], ln2_b[l:l + 1], alpha, seq, nxt=(mods[l + 1][1], mods[l + 1][0]))
        else:
            xin = _lnres_fwd(x1, y2f, mods[l][5], ln2_g[l:l + 1], ln2_b[l:l + 1], alpha, seq)
        saved.append(sv)

    dx, sq_err = _loss_head(xin, tgt)
    loss = lax.psum(0.5 * jnp.sum(sq_err) / d, ("x", "y", "c"))

    small = {}
    bigg = {}
    dmods = [None] * nl
    d_ln = dict(ln1_g=[None] * nl, ln1_b=[None] * nl, ln2_g=[None] * nl, ln2_b=[None] * nl)
    ffn_gi, ffn_go = [None] * nl, [None] * nl
    for l in reversed(range(nl)):
        sv = saved[l]
        kind = l % 4
        dxr, dy2, dlg, dlb, _, dgate2 = _lnres_bwd(dx, sv["x1"], sv["y2f"], mods[l][5], ln2_g[l:l + 1], alpha, seq)
        d_ln["ln2_g"][l], d_ln["ln2_b"][l] = dlg, dlb
        dg_, du_ = _ffn_dact(dy2, w_ffn_out_rows, sv["hg"], sv["hu"], l * (N_DEV // 2))
        dwg, dwu, dwo = _ffn_bwd_weights(sv["h2"], dy2, sv["act"], dg_, du_)
        ffn_gi[l] = jnp.concatenate([dwg, dwu], axis=0)
        ffn_go[l] = dwo.reshape(N_DEV, n_ffn // 2, d)
        dh2 = _ffn_dh(dg_, du_, w_ffn_in, l)
        dx1, dsc2, dsh2 = _modulate_bwd(dh2, dxr, sv["x1"], mods[l][4], seq)
        dxr, dy, dlg, dlb, dysum, dgate1 = _lnres_bwd(dx1, sv["x"], sv["y"], mods[l][2], ln1_g[l:l + 1], alpha, seq)
        d_ln["ln1_g"][l], d_ln["ln1_b"][l] = dlg, dlb
        hh = sv["h"]
        if kind == 0:
            dyv = _mm_plain("gm_out_bwd", dy, sq("gm_w_out"), NT)
            bigg["gm_w_out"] = _mm_plain("gm_out_dw", sv["yv"], dy, TN, GRAD_WIRE).reshape(1, N_DEV, d // N_DEV, d)
            dpre, dws, dbst, dlng, dlnb, dbin = _gm_spatial_bwd(sv["pre"], dyv, gm_ln_g, gm_ln_b, gm_w_s[0], jnp.transpose(gm_b_s[0]), seq)
            small.update(gm_w_s=dws[None], gm_b_s=jnp.transpose(dbst)[None], gm_ln_g=dlng, gm_ln_b=dlnb, gm_b_in=dbin)
            bigg["gm_w_in"] = _grad_cols("gm_in_dw", hh, dpre, gm_w_in.shape[2])
            dh = _back_cols("gm_in_bwd", dpre, gathered["gm_w_in"], gm_w_in.shape[2])
        elif kind == 1:
            do = _mm_plain("fox_out_bwd", dy, sq("fox_w_out"), NT)
            bigg["fox_w_out"] = _mm_plain("fox_out_dw", sv["o"], dy, TN, GRAD_WIRE).reshape(1, N_DEV, d // N_DEV, d)
            dqkv, dfr, dfq = _fox_bwd(sv["qkv"], sv["frow"], sv["o"], do, sv["lse"], nb, seq)
            dft, dbf = _fox_gate_bwd(sv["ft"], sv["b_f"], dfr.reshape(n_heads, t),
                                     jnp.transpose(dfq, (0, 2, 1)).reshape(n_heads, t), seq)
            small["fox_b_f"] = jnp.transpose(dbf)
            dw_qkv = _qkv_dw("fox_qkv_dw", hh, dqkv)
            tk = min(TMM, t)
            dw_ft = _mm("fox_gate_dw", dft, hh, _sds((n_heads, d), F32), (1, t // tk),
                        pl.BlockSpec((n_heads, tk), lambda j, r: (0, r)), pl.BlockSpec((tk, d), lambda j, r: (r, 0)),
                        pl.BlockSpec((n_heads, d), lambda j, r: (0, 0)), NN)
            bigg["fox_w_in"] = _col_blocks(jnp.concatenate([dw_qkv, jnp.transpose(dw_ft).astype(GRAD_WIRE)], axis=1), fox_w_in.shape[2])
            dh_a = _qkv_dh("fox_qkv_bwd", dqkv, fox_qkv_w)
            tm = min(TMM, t)
            dh = _mm("fox_gate_bwd_h", dft, fox_f_wt, _sds((t, d), F32), (t // tm, 1),
                     pl.BlockSpec((n_heads, tm), lambda i, r: (0, i)), pl.BlockSpec((n_heads, d), lambda i, r: (0, 0)),
                     pl.BlockSpec((tm, d), lambda i, r: (i, 0)), TN, (dh_a,), (pl.BlockSpec((tm, d), lambda i, r: (i, 0)),), _add)
        elif kind == 2:
            do = _mm_plain("sb_out_bwd", dy, sq("sb_w_out"), NT)
            bigg["sb_w_out"] = _mm_plain("sb_out_dw", sv["o"], dy, TN, GRAD_WIRE).reshape(1, N_DEV, d // N_DEV, d)
            dqkv = _sb_bwd(sv["qkv"], do, sv["ltot"], nb, seq)
            bigg["sb_w_in"] = _col_blocks(_qkv_dw("sb_qkv_dw", hh, dqkv), sb_w_in.shape[2])
            dh = _qkv_dh("sb_qkv_bwd", dqkv, sb_qkv_w)
        else:
            dy2c = _mm_plain("cv_out_bwd", dy, sq("cv_w_out"), NT)
            bigg["cv_w_out"] = _mm_plain("cv_out_dw", sv["y2"], dy, TN, GRAD_WIRE).reshape(1, N_DEV, d // N_DEV, d)
            dy1, dlng, dlnb, ddwb = _cv_norm_bwd(sv["y1"], dy2c, cv_lng_f, cv_lnb_f)
            dpre, ddw, dbin = _cv_conv_bwd(sv["pre"], dy1, cv_dw_f, seq)
            small.update(cv_b_out=dysum, cv_ln_g=dlng, cv_ln_b=dlnb, cv_dw_b=ddwb, cv_dw=ddw[:CONV_WIDTH], cv_b_in=dbin)
            bigg["cv_w_in"] = _grad_cols("cv_in_dw", hh, dpre, cv_w_in.shape[2])
            dh = _back_cols("cv_in_bwd", dpre, gathered["cv_w_in"], cv_w_in.shape[2])
        dx, dsc1, dsh1 = _modulate_bwd(dh, dxr, sv["x"], mods[l][1], seq)
        dmods[l] = jnp.concatenate([dsh1, dsc1, dgate1, dsh2, dsc2, dgate2], axis=2)[:, 0, :]
    grad_x = dx.reshape(nb, seq, d)
    bigg["ffn_w_in"] = jnp.stack(ffn_gi)
    bigg["ffn_w_out"] = jnp.stack(ffn_go)
    for n in d_ln:
        small[n] = jnp.concatenate(d_ln[n], axis=0)

    dmod_rows = jnp.stack(dmods).reshape(nl * nb, 6 * d)
    dmod_g = _exchange_small(_pad_rows(dmod_rows, 8 * ((nl * nb + 7) // 8)), "gather_dmod", False)[:, :nl * nb]
    dmod_all = jnp.transpose(dmod_g.reshape(N_DEV, nl, nb, 6 * d), (1, 0, 2, 3)).reshape(nl, N_DEV * nb, 6 * d)
    dmod_loc = lax.dynamic_slice_in_dim(dmod_all, me * n_mod, n_mod, axis=2)
    g_mod_w, g_mod_b = _mod_bwd(c_all, dmod_loc, dmod_all)
    grads = dict(mod_w=g_mod_w, mod_b=g_mod_b[:, 0, :])

    rep = ["ln1_g", "ln1_b", "ln2_g", "ln2_b", "gm_b_in", "gm_ln_g", "gm_ln_b", "gm_w_s", "gm_b_s", "fox_b_f"]
    cvs = ["cv_b_in", "cv_dw", "cv_dw_b", "cv_ln_g", "cv_ln_b", "cv_b_out"]

    def rows_of(a):
        flat = a.reshape(-1)
        pad = (-flat.shape[0]) % d
        return jnp.pad(flat, (0, pad)).reshape(-1, d)

    pack_rows = [rows_of(small[n]) for n in rep + cvs]
    counts = [r.shape[0] for r in pack_rows]
    total = sum(counts)
    pack = _pad_rows(jnp.concatenate(pack_rows, axis=0), 8 * ((total + 7) // 8))
    summed = _exchange_small(pack, "allreduce_small", True)
    offs = [sum(counts[:i]) for i in range(len(counts))]
    rep_rows = sum(counts[:len(rep)])
    for n, o_, cnt in zip(rep + cvs, offs, counts):
        full = summed[o_:o_ + cnt].reshape(-1)
        if n in rep:
            grads[n] = full[:weights[n].size].reshape(weights[n].shape)
        else:
            wshape = weights[n].shape
            cols = wshape[-1]
            full = full[:math.prod(wshape[:-1]) * cols * N_DEV].reshape(wshape[:-1] + (cols * N_DEV,))
            grads[n] = lax.dynamic_slice_in_dim(full, me * cols, cols, axis=full.ndim - 1)

    recv = dict(zip(big, _scatter_grads([bigg[n] for n in big])))

    outs = {}

    def view2(a):
        return a.reshape(-1, a.shape[-1])

    for n in big:
        w2 = view2(weights[n])
        res = _adamw("adamw_" + n, w2, view2(mom1[n]), view2(mom2[n]), parts=recv[n].reshape((N_DEV,) + w2.shape))
        outs[n] = [r.reshape(weights[n].shape) for r in res]
    res = _adamw("adamw_mod_w", view2(mod_w), view2(m_mod_w), view2(v_mod_w), g=view2(grads["mod_w"]))
    outs["mod_w"] = [r.reshape(mod_w.shape) for r in res]
    rp = lambda src: _pad_rows(jnp.concatenate([rows_of(src[n]) for n in rep], axis=0), 8 * ((rep_rows + 7) // 8))
    res = _adamw("adamw_replicated", rp(weights), rp(mom1), rp(mom2), g=rp(grads))
    for n, o_, cnt in zip(rep, offs, counts):
        outs[n] = [r[o_:o_ + cnt].reshape(-1)[:weights[n].size].reshape(weights[n].shape) for r in res]
    cv_cols = weights["cv_b_out"].shape[-1]
    cp = lambda src: jnp.concatenate([src[n].reshape(-1, cv_cols) for n in cvs], axis=0)
    cv_cnt = [weights[n].size // cv_cols for n in cvs]
    cv_tot = sum(cv_cnt)
    cpp = lambda src: _pad_rows(cp(src), 8 * ((cv_tot + 7) // 8))
    res = _adamw("adamw_cv_small", cpp(weights), cpp(mom1), cpp(mom2), g=cpp(grads))
    o_ = 0
    for n, cnt in zip(cvs, cv_cnt):
        outs[n] = [r[o_:o_ + cnt].reshape(weights[n].shape) for r in res]
        o_ += cnt
    res = _adamw("adamw_mod_b", mod_b, m_mod_b, v_mod_b, g=grads["mod_b"])
    outs["mod_b"] = list(res)

    return (loss, grad_x, *[outs[n][0] for n in names], *[outs[n][1] for n in names],
            *[outs[n][2] for n in names], *[outs[n][3] for n in names])
```

```python
import functools
import math

import jax
import jax.numpy as jnp
from jax import lax
from jax.experimental import pallas as pl
from jax.experimental.pallas import tpu as pltpu

F32 = jnp.float32
BF16 = jnp.bfloat16
MESH = pl.DeviceIdType.MESH

N_DEV = 8
HEAD_DIM = 64
LANES = 128
SUBLANES = 8
GM_CHUNK = 128
GM_GROUPS = 8
CONV_WIDTH = 31
CONV_HALO = 32
CONV_ROWS = 32
LN_EPS = 1e-5
NEG_INF = -1e30
SB_DEAD = -100.0
GRAD_WIRE = jnp.bfloat16

ADAM_LR = 0.001
ADAM_B1 = 0.9
ADAM_B2 = 0.999
ADAM_EPS = 1e-08
ADAM_WD = 0.01
ADAM_STEP = 10

TM = 512
TMM = 1024
TQ = 256
FOX_FWD_UNROLL = 4
FOX_BWD_UNROLL = 4
SB_BWD_UNROLL = 2
TT = 512
VMEM_BIG = 56 * 1024 * 1024

NN = (((1,), (0,)), ((), ()))
NT = (((1,), (1,)), ((), ()))
TN = (((0,), (0,)), ((), ()))


def _call(body, name, out_shape, grid=None, in_specs=None, out_specs=None, scratch=(), vmem=None):
    params = {}
    if grid is not None:
        params["dimension_semantics"] = ("arbitrary",) * len(grid)
    if vmem is not None:
        params["vmem_limit_bytes"] = vmem
    kw = {}
    if grid is not None:
        kw["grid"] = grid
    if in_specs is not None:
        kw["in_specs"] = in_specs
    if out_specs is not None:
        kw["out_specs"] = out_specs
    return pl.pallas_call(body, name=name, out_shape=out_shape, scratch_shapes=list(scratch),
                          compiler_params=pltpu.CompilerParams(**params), **kw)


def _sds(shape, dtype):
    return jax.ShapeDtypeStruct(tuple(shape), dtype)


def _dot(a, b, dims=NN):
    return lax.dot_general(a.astype(BF16), b.astype(BF16), dims, preferred_element_type=F32)


def _split3(x):
    h1 = x.astype(BF16)
    r1 = x - h1.astype(F32)
    h2 = r1.astype(BF16)
    h3 = (r1 - h2.astype(F32)).astype(BF16)
    return h1, h2, h3


def _dot_exact(x, m, dims=NN):
    h1, h2, h3 = _split3(x)
    d = lambda h: lax.dot_general(h, m, dims, preferred_element_type=F32)
    return (d(h1) + d(h2)) + d(h3)


def _dot_exact_rhs(m, x, dims=NN):
    h1, h2, h3 = _split3(x)
    d = lambda h: lax.dot_general(m, h, dims, preferred_element_type=F32)
    return (d(h1) + d(h2)) + d(h3)


def _silu(x):
    return x * jax.nn.sigmoid(x)


def _gelu(x):
    return 0.5 * x * (1.0 + lax.erf(x * (2.0 ** -0.5)))


def _gelu_grad(x):
    return 0.5 * (1.0 + lax.erf(x * (2.0 ** -0.5))) + x * jnp.exp(-0.5 * x * x) * ((2.0 * math.pi) ** -0.5)


def _log_sigmoid(z):
    return jnp.minimum(z, 0.0) - jnp.log(1.0 + jnp.exp(-jnp.abs(z)))


def _ln_stats(r):
    mu = jnp.mean(r, axis=-1, keepdims=True)
    rc = r - mu
    var = jnp.mean(rc * rc, axis=-1, keepdims=True)
    return rc, lax.rsqrt(var + LN_EPS)


def _colsum(x):
    return jnp.sum(x, axis=0, keepdims=True)


def _peers():
    mx, my, mc = lax.axis_index("x"), lax.axis_index("y"), lax.axis_index("c")
    me = 4 * mx + 2 * my + mc
    out = []
    for k in range(1, N_DEV):
        px = 1 - mx if (k >> 2) & 1 else mx
        py = 1 - my if (k >> 1) & 1 else my
        pc = 1 - mc if k & 1 else mc
        out.append(((px, py, pc), 4 * px + 2 * py + pc))
    return me, out


def _exchange_small(x, name, reduce):
    rows, cols = x.shape

    def body(x_ref, o_ref, *rest):
        if reduce:
            land, send_sems, recv_sems, local_sem = rest
        else:
            land = o_ref
            send_sems, recv_sems, local_sem = rest
        me, peers = _peers()
        mine = pltpu.make_async_copy(x_ref, land.at[me], local_sem)
        mine.start()
        sends = []
        for k, (peer, _) in enumerate(peers):
            cp = pltpu.make_async_remote_copy(src_ref=x_ref, dst_ref=land.at[me], send_sem=send_sems.at[k],
                                              recv_sem=recv_sems.at[k], device_id=peer, device_id_type=MESH)
            cp.start()
            sends.append(cp)
        for k, (peer, blk) in enumerate(peers):
            pltpu.make_async_remote_copy(src_ref=x_ref, dst_ref=land.at[blk], send_sem=send_sems.at[k],
                                         recv_sem=recv_sems.at[k], device_id=peer, device_id_type=MESH).wait_recv()
        for cp in sends:
            cp.wait_send()
        mine.wait()
        if reduce:
            acc = land[0]
            for s in range(1, N_DEV):
                acc = acc + land[s]
            o_ref[...] = acc

    vm = pl.BlockSpec(memory_space=pltpu.VMEM)
    scratch = [pltpu.SemaphoreType.DMA((N_DEV - 1,)), pltpu.SemaphoreType.DMA((N_DEV - 1,)), pltpu.SemaphoreType.DMA]
    if reduce:
        scratch = [pltpu.VMEM((N_DEV, rows, cols), F32)] + scratch
        out = _sds((rows, cols), F32)
    else:
        out = _sds((N_DEV, rows, cols), F32)
    return _call(body, name, out, in_specs=[vm], out_specs=vm, scratch=scratch, vmem=VMEM_BIG)(x)


def _gather_weights(shards):
    n = len(shards)

    def body(*refs):
        ins, outs = refs[:n], refs[n:2 * n]
        send_sems, recv_sems, local_sems = refs[2 * n:]
        mx, my, mc = lax.axis_index("x"), lax.axis_index("y"), lax.axis_index("c")
        sibling = (mx, my, 1 - mc)
        chips = [(1 - mx, my), (mx, 1 - my), (1 - mx, 1 - my)]

        def block(px, py, pc):
            return 4 * px + 2 * py + pc

        def copy(a, k, blk, to, src=None):
            dst = outs[a].at[:, blk]
            return pltpu.make_async_remote_copy(src_ref=dst if src is None else src, dst_ref=dst, send_sem=send_sems.at[a, k],
                                                recv_sem=recv_sems.at[a, k], device_id=to, device_id_type=MESH)

        me = block(mx, my, mc)
        local, sends = [], []
        for a in range(n):
            cp = pltpu.make_async_copy(ins[a], outs[a].at[:, me], local_sems.at[a])
            cp.start()
            local.append(cp)
            first = [copy(a, 0, me, sibling, src=ins[a])]
            first += [copy(a, 1 + j, me, (*chip, mc), src=ins[a]) for j, chip in enumerate(chips)]
            for cp in first:
                cp.start()
            sends += first
        for a in range(n):
            for j, chip in enumerate(chips):
                blk = block(*chip, mc)
                copy(a, 1 + j, blk, (mx, my, mc)).wait_recv()
                cp = copy(a, 4 + j, blk, sibling)
                cp.start()
                sends.append(cp)
        for a in range(n):
            copy(a, 0, block(mx, my, 1 - mc), (mx, my, mc)).wait_recv()
            for j, chip in enumerate(chips):
                copy(a, 4 + j, block(*chip, 1 - mc), (mx, my, mc)).wait_recv()
        for cp in sends:
            cp.wait_send()
        for cp in local:
            cp.wait()

    hbm = pl.BlockSpec(memory_space=pl.ANY)
    out = [_sds((s.shape[0], N_DEV) + s.shape[1:], s.dtype) for s in shards]
    scratch = [pltpu.SemaphoreType.DMA((n, N_DEV - 1)), pltpu.SemaphoreType.DMA((n, N_DEV - 1)), pltpu.SemaphoreType.DMA((n,))]
    return _call(body, "gather_weights", out, in_specs=[hbm] * n, out_specs=[hbm] * n, scratch=scratch)(*shards)


def _scatter_grads(grads):
    n = len(grads)

    def body(*refs):
        ins, outs = refs[:n], refs[n:2 * n]
        send_sems, recv_sems, local_sems = refs[2 * n:]
        me, peers = _peers()
        local = []
        sends = []
        for a in range(n):
            cp = pltpu.make_async_copy(ins[a].at[:, me], outs[a].at[me], local_sems.at[a])
            cp.start()
            local.append(cp)
            for k, (peer, blk) in enumerate(peers):
                cp = pltpu.make_async_remote_copy(src_ref=ins[a].at[:, blk], dst_ref=outs[a].at[me], send_sem=send_sems.at[a, k],
                                                  recv_sem=recv_sems.at[a, k], device_id=peer, device_id_type=MESH)
                cp.start()
                sends.append(cp)
        for a in range(n):
            for k, (peer, blk) in enumerate(peers):
                pltpu.make_async_remote_copy(src_ref=ins[a].at[:, me], dst_ref=outs[a].at[blk], send_sem=send_sems.at[a, k],
                                             recv_sem=recv_sems.at[a, k], device_id=peer, device_id_type=MESH).wait_recv()
        for cp in sends:
            cp.wait_send()
        for cp in local:
            cp.wait()

    hbm = pl.BlockSpec(memory_space=pl.ANY)
    out = [_sds((N_DEV, g.shape[0]) + g.shape[2:], g.dtype) for g in grads]
    scratch = [pltpu.SemaphoreType.DMA((n, N_DEV - 1)), pltpu.SemaphoreType.DMA((n, N_DEV - 1)), pltpu.SemaphoreType.DMA((n,))]
    return _call(body, "scatter_grads", out, in_specs=[hbm] * n, out_specs=[hbm] * n, scratch=scratch)(*grads)


def _mm(name, a, b, out, grid, a_spec, b_spec, o_spec, dims, extra=(), extra_specs=(), epilogue=None, vmem=VMEM_BIG):
    nred = grid[-1]
    red_axis = len(grid) - 1
    acc_shape = tuple(d for d in o_spec.block_shape if d is not None)
    n_extra = len(extra)

    def body(a_ref, b_ref, *rest):
        ex = rest[:n_extra]
        o_ref = rest[n_extra]

        def finish(acc):
            if epilogue is not None:
                acc = epilogue(acc, *[e[...] for e in ex])
            o_ref[...] = acc.astype(o_ref.dtype)

        prod = _dot(a_ref[...], b_ref[...], dims)
        if nred == 1:
            finish(prod)
        else:
            acc_ref = rest[n_extra + 1]
            r = pl.program_id(red_axis)

            @pl.when(r == 0)
            def _():
                acc_ref[...] = prod

            @pl.when(r > 0)
            def _():
                acc_ref[...] += prod

            @pl.when(r == nred - 1)
            def _():
                finish(acc_ref[...])

    scratch = [pltpu.VMEM(acc_shape, F32)] if nred > 1 else []
    return _call(body, name, out, grid=grid, in_specs=[a_spec, b_spec, *extra_specs], out_specs=o_spec,
                 scratch=scratch, vmem=vmem)(a, b, *extra)


def _add(acc, x):
    return acc + x


def _proj_cols(name, h, w, n_slot, bias=None, out_dtype=F32):
    t, k = h.shape
    s = w.shape[1]
    tm = min(TMM, t)
    extra, especs, epi = (), (), None
    if bias is not None:
        extra, especs, epi = (bias,), (pl.BlockSpec((1, n_slot), lambda i, j, r: (0, j)),), _add
    return _mm(name, h, w, _sds((t, s * n_slot), out_dtype), (t // tm, s, 1),
               pl.BlockSpec((tm, k), lambda i, j, r: (i, 0)),
               pl.BlockSpec((None, None, k, n_slot), lambda i, j, r: (0, j, 0, 0)),
               pl.BlockSpec((tm, n_slot), lambda i, j, r: (i, j)), NN, extra, especs, epi)


def _accumulate_over_tokens(name, ins, in_specs, out, o_spec, acc_shape, n_steps, terms, store=None):
    def body(*refs):
        o_ref, acc_ref = refs[len(ins)], refs[len(ins) + 1]
        r = pl.program_id(0)

        @pl.when(r == 0)
        def _():
            acc_ref[...] = jnp.zeros_like(acc_ref)

        for s, prod in enumerate(terms(*refs[:len(ins)])):
            acc_ref[s] += prod

        @pl.when(r == n_steps - 1)
        def _():
            if store is None:
                o_ref[...] = acc_ref[...].reshape(o_ref.shape).astype(o_ref.dtype)
            else:
                store(o_ref, acc_ref)

    return _call(body, name, out, grid=(n_steps,), in_specs=in_specs, out_specs=o_spec,
                 scratch=[pltpu.VMEM(acc_shape, F32)], vmem=VMEM_BIG)(*ins)


def _grad_cols(name, h, g, n_slot):
    t, k = h.shape
    s = g.shape[1] // n_slot
    tk = min(TMM, t)

    def terms(h_ref, g_ref):
        hv = h_ref[...]
        return [_dot(hv, g_ref[:, j * n_slot:(j + 1) * n_slot], TN) for j in range(s)]

    return _accumulate_over_tokens(name, (h, g), [pl.BlockSpec((tk, k), lambda r: (r, 0)), pl.BlockSpec((tk, s * n_slot), lambda r: (r, 0))],
                                   _sds((1, s, k, n_slot), GRAD_WIRE), pl.BlockSpec((1, s, k, n_slot), lambda r: (0, 0, 0, 0)),
                                   (s, k, n_slot), t // tk, terms)


def _back_cols(name, g, w, n_slot):
    t = g.shape[0]
    s, k = w.shape[1], w.shape[2]
    tm = min(TM, t)

    def body(g_ref, w_ref, o_ref):
        acc = None
        for j in range(s):
            term = _dot(g_ref[:, j * n_slot:(j + 1) * n_slot], w_ref[j], NT)
            acc = term if acc is None else acc + term
        o_ref[...] = acc

    return _call(body, name, _sds((t, k), F32), grid=(t // tm,),
                 in_specs=[pl.BlockSpec((tm, s * n_slot), lambda i: (i, 0)), pl.BlockSpec((None, s, k, n_slot), lambda i: (0, 0, 0, 0))],
                 out_specs=pl.BlockSpec((tm, k), lambda i: (i, 0)), vmem=VMEM_BIG)(g, w)


def _mm_plain(name, a, b, dims, out_dtype=F32, bias=None):
    if dims == TN:
        t, k = a.shape
        n = b.shape[1]
        tk = min(TMM, t)
        return _mm(name, a, b, _sds((k, n), out_dtype), (1, t // tk),
                   pl.BlockSpec((tk, k), lambda j, r: (r, 0)), pl.BlockSpec((tk, n), lambda j, r: (r, 0)),
                   pl.BlockSpec((k, n), lambda j, r: (0, 0)), TN)
    t = a.shape[0]
    tm = min(TMM, t)
    n = b.shape[1] if dims == NN else b.shape[0]
    extra, especs, epi = (), (), None
    if bias is not None:
        extra, especs, epi = (bias,), (pl.BlockSpec((1, n), lambda i, r: (0, 0)),), _add
    return _mm(name, a, b, _sds((t, n), out_dtype), (t // tm, 1),
               pl.BlockSpec((tm, a.shape[1]), lambda i, r: (i, 0)), pl.BlockSpec(b.shape, lambda i, r: (0, 0)),
               pl.BlockSpec((tm, n), lambda i, r: (i, 0)), dims, extra, especs, epi)


def _mod_fwd(c_all, mod_w, mod_b_loc):
    nl, d, n = mod_w.shape
    nb = c_all.shape[0]

    def body(c_ref, w_ref, b_ref, o_ref):
        o_ref[...] = _dot(_silu(c_ref[...]), w_ref[...]) + b_ref[...]

    return _call(body, "mod_fwd", _sds((nl, nb, n), F32), grid=(nl,),
                 in_specs=[pl.BlockSpec((nb, d), lambda l: (0, 0)), pl.BlockSpec((None, d, n), lambda l: (l, 0, 0)),
                           pl.BlockSpec((None, 1, n), lambda l: (l, 0, 0))],
                 out_specs=pl.BlockSpec((None, nb, n), lambda l: (l, 0, 0)))(c_all, mod_w, mod_b_loc)


def _mod_bwd(c_all, dmod_loc, dmod_all):
    nl, nb, n = dmod_loc.shape
    d = c_all.shape[1]
    n_all = dmod_all.shape[2]

    def body(c_ref, dl_ref, da_ref, gw_ref, gb_ref):
        gw_ref[...] = _dot(_silu(c_ref[...]), dl_ref[...], TN)
        gb_ref[...] = _colsum(da_ref[...])

    return _call(body, "mod_bwd", (_sds((nl, d, n), F32), _sds((nl, 1, n_all), F32)), grid=(nl,),
                 in_specs=[pl.BlockSpec((nb, d), lambda l: (0, 0)), pl.BlockSpec((None, nb, n), lambda l: (l, 0, 0)),
                           pl.BlockSpec((None, nb, n_all), lambda l: (l, 0, 0))],
                 out_specs=(pl.BlockSpec((None, d, n), lambda l: (l, 0, 0)), pl.BlockSpec((None, 1, n_all), lambda l: (l, 0, 0))),
                 )(c_all, dmod_loc, dmod_all)


def _row_spec(d, tpb):
    return pl.BlockSpec((None, 1, d), lambda i: (i // tpb, 0, 0))


def _tile_spec(tm, d):
    return pl.BlockSpec((tm, d), lambda i: (i, 0))


def _vec_spec(d):
    return pl.BlockSpec((1, d), lambda i: (0, 0))


def _modulate(x, sc, sh, seq):
    t, d = x.shape
    tm = min(TM, seq)
    tpb = seq // tm

    def body(x_ref, sc_ref, sh_ref, h_ref):
        h_ref[...] = (x_ref[...] * (1.0 + sc_ref[...]) + sh_ref[...]).astype(BF16)

    return _call(body, "modulate", _sds((t, d), BF16), grid=(t // tm,),
                 in_specs=[_tile_spec(tm, d), _row_spec(d, tpb), _row_spec(d, tpb)], out_specs=_tile_spec(tm, d))(x, sc, sh)


def _lnres_fwd(x, y, gate, lg, lb, alpha, seq, nxt=None):
    t, d = x.shape
    tm = min(TM, seq)
    tpb = seq // tm

    def body(x_ref, y_ref, g_ref, lg_ref, lb_ref, *rest):
        r = alpha * x_ref[...] + (1.0 + g_ref[...]) * y_ref[...]
        rc, rstd = _ln_stats(r)
        xn = rc * rstd * lg_ref[...] + lb_ref[...]
        if nxt is None:
            rest[0][...] = xn
        else:
            sc_ref, sh_ref, xo_ref, h_ref = rest
            xo_ref[...] = xn
            h_ref[...] = (xn * (1.0 + sc_ref[...]) + sh_ref[...]).astype(BF16)

    ins = [_tile_spec(tm, d), _tile_spec(tm, d), _row_spec(d, tpb), _vec_spec(d), _vec_spec(d)]
    if nxt is None:
        return _call(body, "lnres_fwd_last", _sds((t, d), F32), grid=(t // tm,), in_specs=ins,
                     out_specs=_tile_spec(tm, d))(x, y, gate, lg, lb)
    return _call(body, "lnres_fwd", (_sds((t, d), F32), _sds((t, d), BF16)), grid=(t // tm,),
                 in_specs=ins + [_row_spec(d, tpb), _row_spec(d, tpb)],
                 out_specs=(_tile_spec(tm, d), _tile_spec(tm, d)))(x, y, gate, lg, lb, *nxt)


def _loss_head(x, tgt):
    t, d = x.shape
    tm = min(TM, t)

    def body(x_ref, t_ref, dx_ref, sq_ref):
        e = x_ref[...] - t_ref[...]
        dx_ref[...] = e * (1.0 / d)

        @pl.when(pl.program_id(0) == 0)
        def _():
            sq_ref[...] = jnp.zeros_like(sq_ref)

        sq_ref[...] += _colsum(e * e)

    return _call(body, "loss_head", (_sds((t, d), F32), _sds((1, d), F32)), grid=(t // tm,),
                 in_specs=[_tile_spec(tm, d), _tile_spec(tm, d)], out_specs=(_tile_spec(tm, d), _vec_spec(d)))(x, tgt)


def _lnres_bwd(dxo, x, y, gate, lg, alpha, seq):
    t, d = x.shape
    tm = min(TM, seq)
    tpb = seq // tm
    nb = t // seq

    def body(dxo_ref, x_ref, y_ref, g_ref, lg_ref, dxr_ref, dy_ref, dlg_ref, dlb_ref, dys_ref, dg_ref):
        i = pl.program_id(0)
        yv = y_ref[...]
        r = alpha * x_ref[...] + (1.0 + g_ref[...]) * yv
        rc, rstd = _ln_stats(r)
        xhat = rc * rstd
        dxo_v = dxo_ref[...]
        dxh = dxo_v * lg_ref[...]
        m1 = jnp.mean(dxh, axis=-1, keepdims=True)
        m2 = jnp.mean(dxh * xhat, axis=-1, keepdims=True)
        dr = rstd * (dxh - m1 - xhat * m2)
        dyv = (1.0 + g_ref[...]) * dr
        dxr_ref[...] = alpha * dr
        dy_ref[...] = dyv.astype(BF16)

        @pl.when(i == 0)
        def _():
            dlg_ref[...] = jnp.zeros_like(dlg_ref)
            dlb_ref[...] = jnp.zeros_like(dlb_ref)
            dys_ref[...] = jnp.zeros_like(dys_ref)

        @pl.when(i % tpb == 0)
        def _():
            dg_ref[...] = jnp.zeros_like(dg_ref)

        dlg_ref[...] += _colsum(dxo_v * xhat)
        dlb_ref[...] += _colsum(dxo_v)
        dys_ref[...] += _colsum(dyv)
        dg_ref[...] += _colsum(dr * yv)

    return _call(body, "lnres_bwd",
                 (_sds((t, d), F32), _sds((t, d), BF16), _sds((1, d), F32), _sds((1, d), F32), _sds((1, d), F32), _sds((nb, 1, d), F32)),
                 grid=(t // tm,),
                 in_specs=[_tile_spec(tm, d), _tile_spec(tm, d), _tile_spec(tm, d), _row_spec(d, tpb), _vec_spec(d)],
                 out_specs=(_tile_spec(tm, d), _tile_spec(tm, d), _vec_spec(d), _vec_spec(d), _vec_spec(d), _row_spec(d, tpb)),
                 )(dxo, x, y, gate, lg)


def _modulate_bwd(dh, dxr, x, sc, seq):
    t, d = x.shape
    tm = min(TM, seq)
    tpb = seq // tm
    nb = t // seq

    def body(dh_ref, dxr_ref, x_ref, sc_ref, dx_ref, dsc_ref, dsh_ref):
        dhv = dh_ref[...]
        dx_ref[...] = dxr_ref[...] + dhv * (1.0 + sc_ref[...])

        @pl.when(pl.program_id(0) % tpb == 0)
        def _():
            dsc_ref[...] = jnp.zeros_like(dsc_ref)
            dsh_ref[...] = jnp.zeros_like(dsh_ref)

        dsc_ref[...] += _colsum(dhv * x_ref[...])
        dsh_ref[...] += _colsum(dhv)

    return _call(body, "modulate_bwd", (_sds((t, d), F32), _sds((nb, 1, d), F32), _sds((nb, 1, d), F32)), grid=(t // tm,),
                 in_specs=[_tile_spec(tm, d), _tile_spec(tm, d), _tile_spec(tm, d), _row_spec(d, tpb)],
                 out_specs=(_tile_spec(tm, d), _row_spec(d, tpb), _row_spec(d, tpb)))(dh, dxr, x, sc)


def _ffn_in(h, w_in, layer):
    t, d = h.shape
    n = w_in.shape[3]
    half = N_DEV // 2
    tm = min(TMM, t)

    def body(h_ref, wg_ref, wu_ref, g_ref, u_ref, a_ref):
        hv = h_ref[...]
        g = _dot(hv, wg_ref[...])
        u = _dot(hv, wu_ref[...])
        g_ref[...] = g
        u_ref[...] = u
        a_ref[...] = (_silu(g) * u).astype(BF16)

    blk = pl.BlockSpec((None, tm, n), lambda p, i: (p, i, 0))
    return _call(body, "ffn_in", (_sds((half, t, n), F32), _sds((half, t, n), F32), _sds((half, t, n), BF16)),
                 grid=(half, t // tm),
                 in_specs=[pl.BlockSpec((tm, d), lambda p, i: (i, 0)),
                           pl.BlockSpec((None, None, d, n), lambda p, i: (layer, p, 0, 0)),
                           pl.BlockSpec((None, None, d, n), lambda p, i: (layer, p + half, 0, 0))],
                 out_specs=(blk, blk, blk), vmem=VMEM_BIG)(h, w_in, w_in)


def _ffn_out(act, w_out, layer):
    half, t, n = act.shape
    d = w_out.shape[2]
    tm = min(TM, t)

    def body(a_ref, w_ref, o_ref):
        acc = _dot(a_ref[0], w_ref[0])
        for p in range(1, half):
            acc = acc + _dot(a_ref[p], w_ref[p])
        o_ref[...] = acc

    return _call(body, "ffn_out", _sds((t, d), F32), grid=(t // tm,),
                 in_specs=[pl.BlockSpec((half, tm, n), lambda i: (0, i, 0)),
                           pl.BlockSpec((half, n, d), lambda i: (layer // half, 0, 0))],
                 out_specs=pl.BlockSpec((tm, d), lambda i: (i, 0)), vmem=VMEM_BIG)(act, w_out)


def _ffn_dact(dy, w_out, hg, hu, layer):
    half, t, n = hg.shape
    d = dy.shape[1]
    tm = min(TMM, t)

    def body(dy_ref, w_ref, g_ref, u_ref, dg_ref, du_ref):
        da = _dot(dy_ref[...], w_ref[...], NT)
        g = g_ref[...]
        sg = jax.nn.sigmoid(g)
        dg_ref[...] = (da * u_ref[...] * (sg * (1.0 + g * (1.0 - sg)))).astype(BF16)
        du_ref[...] = (da * (g * sg)).astype(BF16)

    blk = pl.BlockSpec((None, tm, n), lambda p, i: (p, i, 0))
    dg, du = _call(body, "ffn_dact", (_sds((half, t, n), BF16), _sds((half, t, n), BF16)), grid=(half, t // tm),
                   in_specs=[pl.BlockSpec((tm, d), lambda p, i: (i, 0)),
                             pl.BlockSpec((None, n, d), lambda p, i: (layer + p, 0, 0)), blk, blk],
                   out_specs=(blk, blk), vmem=VMEM_BIG)(dy, w_out, hg, hu)
    return dg, du


def _ffn_bwd_weights(h, dy, act, dg, du):
    half, t, n = act.shape
    d = h.shape[1]
    tk = min(TMM, t)
    h_spec = pl.BlockSpec((tk, d), lambda r: (r, 0))
    g_spec = pl.BlockSpec((half, tk, n), lambda r: (0, r, 0))

    def in_terms(h_ref, g_ref):
        hv = h_ref[...]
        return [_dot(hv, g_ref[p], TN) for p in range(half)]

    def out_terms(a_ref, dy_ref):
        dyv = dy_ref[...]
        return [_dot(a_ref[p], dyv, TN) for p in range(half)]

    w_in_spec = pl.BlockSpec((half, d, n), lambda r: (0, 0, 0))
    dwg = _accumulate_over_tokens("ffn_dw_gate", (h, dg), [h_spec, g_spec], _sds((half, d, n), GRAD_WIRE), w_in_spec,
                                  (half, d, n), t // tk, in_terms)
    dwu = _accumulate_over_tokens("ffn_dw_up", (h, du), [h_spec, g_spec], _sds((half, d, n), GRAD_WIRE), w_in_spec,
                                  (half, d, n), t // tk, in_terms)
    dwo = _accumulate_over_tokens("ffn_dw_out", (act, dy), [g_spec, h_spec], _sds((half, n, d), GRAD_WIRE),
                                  pl.BlockSpec((half, n, d), lambda r: (0, 0, 0)), (half, n, d), t // tk, out_terms)
    return dwg, dwu, dwo


def _ffn_dh(dg, du, w_in, layer):
    half, t, n = dg.shape
    d = w_in.shape[2]
    tm = min(TM, t)

    def body(dg_ref, du_ref, w_ref, o_ref):
        acc = None
        for p in range(half):
            for ref, q in ((dg_ref, p), (du_ref, p + half)):
                term = _dot(ref[p], w_ref[q], NT)
                acc = term if acc is None else acc + term
        o_ref[...] = acc

    g_spec = pl.BlockSpec((half, tm, n), lambda i: (0, i, 0))
    return _call(body, "ffn_dh", _sds((t, d), F32), grid=(t // tm,),
                 in_specs=[g_spec, g_spec, pl.BlockSpec((None, 2 * half, d, n), lambda i: (layer, 0, 0, 0))],
                 out_specs=pl.BlockSpec((tm, d), lambda i: (i, 0)), vmem=VMEM_BIG)(dg, du, w_in)


def _tril(n, strict=False):
    r = lax.broadcasted_iota(jnp.int32, (n, n), 0)
    c = lax.broadcasted_iota(jnp.int32, (n, n), 1)
    return c < r if strict else c <= r


def _gm_spatial_fwd(pre, lng, lnb, w_s, b_st, seq):
    t, w2 = pre.shape
    w = w2 // 2
    gd = w // GM_GROUPS
    tm = min(TM, seq)
    nch = tm // GM_CHUNK

    def body(pre_ref, lng_ref, lnb_ref, ws_ref, bs_ref, y_ref):
        v = _gelu(pre_ref[:, w:])
        vc, rstd = _ln_stats(v)
        vn = (vc * rstd * lng_ref[...] + lnb_ref[...]).astype(BF16)
        keep = _tril(GM_CHUNK)
        for g in range(GM_GROUPS):
            wm = jnp.where(keep, ws_ref[g], 0.0).astype(BF16)
            for ci in range(nch):
                rows = slice(ci * GM_CHUNK, (ci + 1) * GM_CHUNK)
                cols = slice(g * gd, (g + 1) * gd)
                sv = _dot(wm, vn[rows, cols]) + bs_ref[:, g:g + 1]
                u = _gelu(pre_ref[rows, cols])
                y_ref[rows, cols] = (u * sv).astype(BF16)

    return _call(body, "gm_spatial_fwd", _sds((t, w), BF16), grid=(t // tm,),
                 in_specs=[_tile_spec(tm, w2), _vec_spec(w), _vec_spec(w),
                           pl.BlockSpec((GM_GROUPS, GM_CHUNK, GM_CHUNK), lambda i: (0, 0, 0)),
                           pl.BlockSpec((GM_CHUNK, GM_GROUPS), lambda i: (0, 0))],
                 out_specs=_tile_spec(tm, w), vmem=VMEM_BIG)(pre, lng, lnb, w_s, b_st)


def _gm_spatial_bwd(pre, dyv, lng, lnb, w_s, b_st, seq):
    t, w2 = pre.shape
    w = w2 // 2
    gd = w // GM_GROUPS
    tm = min(TM, seq)
    nch = tm // GM_CHUNK

    def body(pre_ref, dyv_ref, lng_ref, lnb_ref, ws_ref, bs_ref, dpre_ref, dws_ref, dbs_ref, dlg_ref, dlb_ref, dbin_ref, dvn_ref):
        @pl.when(pl.program_id(0) == 0)
        def _():
            dws_ref[...] = jnp.zeros_like(dws_ref)
            dbs_ref[...] = jnp.zeros_like(dbs_ref)
            dlg_ref[...] = jnp.zeros_like(dlg_ref)
            dlb_ref[...] = jnp.zeros_like(dlb_ref)
            dbin_ref[...] = jnp.zeros_like(dbin_ref)

        pv = pre_ref[:, w:]
        v = _gelu(pv)
        vc, rstd = _ln_stats(v)
        vhat = vc * rstd
        vn = (vhat * lng_ref[...] + lnb_ref[...]).astype(BF16)
        keep = _tril(GM_CHUNK)
        dbs_cols = []
        for g in range(GM_GROUPS):
            wm = jnp.where(keep, ws_ref[g], 0.0).astype(BF16)
            dwm = jnp.zeros((GM_CHUNK, GM_CHUNK), F32)
            dbs = jnp.zeros((GM_CHUNK, 1), F32)
            for ci in range(nch):
                rows = slice(ci * GM_CHUNK, (ci + 1) * GM_CHUNK)
                cols = slice(g * gd, (g + 1) * gd)
                vn_b = vn[rows, cols]
                sv = _dot(wm, vn_b) + bs_ref[:, g:g + 1]
                pu = pre_ref[rows, cols]
                dy = dyv_ref[rows, cols]
                du = dy * sv
                dsv = dy * _gelu(pu)
                dpu = du * _gelu_grad(pu)
                dpre_ref[rows, cols] = dpu.astype(BF16)
                dbin_ref[:, cols] += _colsum(dpu)
                dsv_b = dsv.astype(BF16)
                dwm = dwm + _dot(dsv_b, vn_b, NT)
                dbs = dbs + jnp.sum(dsv, axis=-1, keepdims=True)
                dvn_ref[rows, cols] = _dot(wm, dsv_b, TN)
            dws_ref[g] += jnp.where(keep, dwm, 0.0)
            dbs_cols.append(dbs)
        dbs_ref[...] += jnp.concatenate(dbs_cols, axis=1)
        dvn = dvn_ref[...]
        dlg_ref[...] += _colsum(dvn * vhat)
        dlb_ref[...] += _colsum(dvn)
        dvh = dvn * lng_ref[...]
        m1 = jnp.mean(dvh, axis=-1, keepdims=True)
        m2 = jnp.mean(dvh * vhat, axis=-1, keepdims=True)
        dv = rstd * (dvh - m1 - vhat * m2)
        dpv = dv * _gelu_grad(pv)
        dpre_ref[:, w:] = dpv.astype(BF16)
        dbin_ref[:, w:] += _colsum(dpv)

    full3 = pl.BlockSpec((GM_GROUPS, GM_CHUNK, GM_CHUNK), lambda i: (0, 0, 0))
    bst = pl.BlockSpec((GM_CHUNK, GM_GROUPS), lambda i: (0, 0))
    return _call(body, "gm_spatial_bwd",
                 (_sds((t, w2), BF16), _sds((GM_GROUPS, GM_CHUNK, GM_CHUNK), F32), _sds((GM_CHUNK, GM_GROUPS), F32),
                  _sds((1, w), F32), _sds((1, w), F32), _sds((1, w2), F32)),
                 grid=(t // tm,),
                 in_specs=[_tile_spec(tm, w2), _tile_spec(tm, w), _vec_spec(w), _vec_spec(w), full3, bst],
                 out_specs=(_tile_spec(tm, w2), full3, bst, _vec_spec(w), _vec_spec(w), _vec_spec(w2)),
                 scratch=[pltpu.VMEM((tm, w), F32)], vmem=VMEM_BIG)(pre, dyv, lng, lnb, w_s, b_st)


def _head_masks():
    lane = lax.broadcasted_iota(jnp.int32, (1, LANES), 1)
    return lane < HEAD_DIM


def _two_heads(x, m0):
    z = jnp.zeros_like(x)
    return jnp.where(m0, x, z), jnp.where(m0, z, x)


def _transposed(pair):
    return tuple(x.astype(F32).T.astype(BF16) for x in pair)


def _store_transposed(dqkv_ref, dk_acc, dv_acc, nq, tq):
    for c in range(nq):
        cols = slice(c * tq, (c + 1) * tq)
        dqkv_ref[1, cols, :] = dk_acc[:, cols].T.astype(BF16)
        dqkv_ref[2, cols, :] = dv_acc[:, cols].T.astype(BF16)


def _qkv_specs(seq, nq, blocked_q):
    if blocked_q:
        q = pl.BlockSpec((None, TQ_(seq), LANES), lambda b, p, i: (0, b * nq + i, p))
        k = pl.BlockSpec((None, seq, LANES), lambda b, p, i: (1, b, p))
        v = pl.BlockSpec((None, seq, LANES), lambda b, p, i: (2, b, p))
    else:
        q = pl.BlockSpec((None, seq, LANES), lambda b, p: (0, b, p))
        k = pl.BlockSpec((None, seq, LANES), lambda b, p: (1, b, p))
        v = pl.BlockSpec((None, seq, LANES), lambda b, p: (2, b, p))
    return q, k, v


def TQ_(seq):
    return min(TQ, seq)


def _fox_gate_fwd(ft, b_f, seq):
    nh, t = ft.shape
    nch = seq // LANES

    def body(ft_ref, bf_ref, fr_ref):
        r = lax.broadcasted_iota(jnp.int32, (LANES, LANES), 0)
        c = lax.broadcasted_iota(jnp.int32, (LANES, LANES), 1)
        upper = jnp.where(r <= c, 1.0, 0.0).astype(BF16)
        carry = jnp.zeros((nh, 1), F32)
        for ci in range(nch):
            cols = slice(ci * LANES, (ci + 1) * LANES)
            lf = _log_sigmoid(ft_ref[:, cols] + bf_ref[...])
            cs = _dot_exact(lf, upper) + carry
            fr_ref[:, cols] = cs
            carry = cs[:, LANES - 1:LANES]

    return _call(body, "fox_gate_fwd", _sds((nh, t), F32), grid=(t // seq,),
                 in_specs=[pl.BlockSpec((nh, seq), lambda b: (0, b)), pl.BlockSpec((nh, 1), lambda b: (0, 0))],
                 out_specs=pl.BlockSpec((nh, seq), lambda b: (0, b)))(ft, b_f)


def _fox_gate_bwd(ft, b_f, dfk, dfq, seq):
    nh, t = ft.shape
    nch = seq // LANES

    def body(ft_ref, bf_ref, dfk_ref, dfq_ref, dl_ref, db_ref):
        @pl.when(pl.program_id(0) == 0)
        def _():
            db_ref[...] = jnp.zeros_like(db_ref)

        r = lax.broadcasted_iota(jnp.int32, (LANES, LANES), 0)
        c = lax.broadcasted_iota(jnp.int32, (LANES, LANES), 1)
        lower = jnp.where(r >= c, 1.0, 0.0).astype(BF16)
        carry = jnp.zeros((nh, 1), F32)
        tot = jnp.zeros((nh, 1), F32)
        for ci in reversed(range(nch)):
            cols = slice(ci * LANES, (ci + 1) * LANES)
            rc = _dot_exact(dfk_ref[:, cols] + dfq_ref[:, cols], lower) + carry
            carry = rc[:, 0:1]
            dl = rc * jax.nn.sigmoid(-(ft_ref[:, cols] + bf_ref[...]))
            dl_ref[:, cols] = dl
            tot = tot + jnp.sum(dl, axis=-1, keepdims=True)
        db_ref[...] += tot

    blk = pl.BlockSpec((nh, seq), lambda b: (0, b))
    one = pl.BlockSpec((nh, 1), lambda b: (0, 0))
    return _call(body, "fox_gate_bwd", (_sds((nh, t), F32), _sds((nh, 1), F32)), grid=(t // seq,),
                 in_specs=[blk, one, blk, blk], out_specs=(blk, one))(ft, b_f, dfk, dfq)


def _sweep(step, n_off, unroll, init, start=0):
    def group(_, st):
        base, carry = st[0], st[1:]
        for u in range(unroll):
            carry = step(base + u, carry, False)
        return (base + unroll, *carry)

    def tail(r):
        def run(st):
            base, carry = st[0], st[1:]
            for u in range(r):
                carry = step(base + u, carry, False)
            return step(base + r, carry, True)
        return run

    def pick(idx, fns, st):
        if len(fns) == 1:
            return fns[0](st)
        half = len(fns) // 2
        return lax.cond(idx < half, lambda s: pick(idx, fns[:half], s), lambda s: pick(idx - half, fns[half:], s), st)

    st = lax.fori_loop(0, n_off // unroll, group, (jnp.int32(0) + start, *init))
    return pick(n_off % unroll, [tail(r) for r in range(unroll)], st)


def _fox_fwd(qkv, frow, nb, seq):
    _, t, d = qkv.shape
    npair = d // LANES
    tq = TQ_(seq)
    nq = seq // tq
    scale = HEAD_DIM ** -0.5

    def body(q_ref, k_ref, v_ref, fr_ref, o_ref, lse_ref):
        m0 = _head_masks()
        qm = _two_heads(q_ref[...] * scale, m0)
        row = lax.broadcasted_iota(jnp.int32, (tq, tq), 0)
        col = lax.broadcasted_iota(jnp.int32, (tq, tq), 1)
        one = jnp.ones((tq, LANES), BF16)

        def step(j, carry, diag):
            off = pl.multiple_of(j * tq, tq)
            kb = k_ref[pl.ds(off, tq), :]
            vb = v_ref[pl.ds(off, tq), :]
            vv = (jnp.where(m0, vb, one), jnp.where(m0, one, vb))
            out = []
            for hh in range(2):
                m, acc = carry[2 * hh], carry[2 * hh + 1]
                s = lax.dot_general(qm[hh], kb, NT, preferred_element_type=F32) - fr_ref[hh:hh + 1, pl.ds(off, tq)]
                if diag:
                    s = jnp.where(col <= row, s, NEG_INF)
                mn = jnp.maximum(m, jnp.max(s, axis=-1, keepdims=True))
                p = jnp.exp(s - mn)
                out += [mn, jnp.exp(m - mn) * acc + jnp.dot(p.astype(BF16), vv[hh], preferred_element_type=F32)]
            return tuple(out)

        neg = jnp.full((tq, 1), NEG_INF, F32)
        zacc = jnp.zeros((tq, LANES), F32)
        m_a, acc_a, m_b, acc_b = _sweep(step, pl.program_id(2), FOX_FWD_UNROLL, (neg, zacc, neg, zacc))
        l_a = pltpu.roll(acc_a, HEAD_DIM, 1)
        l_b = pltpu.roll(acc_b, HEAD_DIM, 1)
        o_ref[...] = jnp.where(m0, acc_a / l_a, acc_b / l_b)
        lse_ref[:, 0:1] = m_a + jnp.log(l_a[:, 0:1])
        lse_ref[:, 1:2] = m_b + jnp.log(l_b[:, HEAD_DIM:HEAD_DIM + 1])

    q_spec, k_spec, v_spec = _qkv_specs(seq, nq, True)
    col_spec = pl.BlockSpec((None, tq, 2), lambda b, p, i: (p, b * nq + i, 0))
    return _call(body, "fox_fwd", (_sds((t, d), F32), _sds((npair, t, 2), F32)), grid=(nb, npair, nq),
                 in_specs=[q_spec, k_spec, v_spec, pl.BlockSpec((None, 2, seq), lambda b, p, i: (p, 0, b))],
                 out_specs=(pl.BlockSpec((tq, LANES), lambda b, p, i: (b * nq + i, p)), col_spec),
                 vmem=VMEM_BIG)(qkv, qkv, qkv, frow)


def _fox_bwd(qkv, frow, o, do, lse, nb, seq):
    _, t, d = qkv.shape
    npair = d // LANES
    tq = TQ_(seq)
    nq = seq // tq
    scale = HEAD_DIM ** -0.5

    def body(q_ref, k_ref, v_ref, fr_ref, o_ref, do_ref, lse_ref, dqkv_ref, df_ref, dfq_ref, dk_acc, dv_acc):
        m0 = _head_masks()
        row = lax.broadcasted_iota(jnp.int32, (tq, tq), 0)
        col = lax.broadcasted_iota(jnp.int32, (tq, tq), 1)
        dk_acc[...] = jnp.zeros_like(dk_acc)
        dv_acc[...] = jnp.zeros_like(dv_acc)
        df_ref[...] = jnp.zeros_like(df_ref)

        def q_block(i, _):
            qoff = pl.multiple_of(i * tq, tq)
            qrows = pl.ds(qoff, tq)
            qm = _two_heads(q_ref[qrows, :] * scale, m0)
            dov = do_ref[qrows, :]
            dd = dov * o_ref[qrows, :]
            dm = _two_heads(dov.astype(BF16), m0)
            qmt = _transposed(qm)
            dmt = _transposed(dm)
            delta = (jnp.sum(jnp.where(m0, dd, 0.0), axis=-1, keepdims=True),
                     jnp.sum(jnp.where(m0, 0.0, dd), axis=-1, keepdims=True))
            ls = (lse_ref[qrows, 0:1], lse_ref[qrows, 1:2])

            def step(j, carry, diag):
                off = pl.multiple_of(j * tq, tq)
                krows = pl.ds(off, tq)
                kb = k_ref[krows, :]
                vb = v_ref[krows, :]
                dqs, rowsums = [], []
                dk = jnp.zeros((LANES, tq), F32)
                dv = jnp.zeros((LANES, tq), F32)
                for hh in range(2):
                    s = lax.dot_general(qm[hh], kb, NT, preferred_element_type=F32) - fr_ref[hh:hh + 1, krows]
                    if diag:
                        s = jnp.where(col <= row, s, NEG_INF)
                    p = jnp.exp(s - ls[hh])
                    dp = lax.dot_general(dm[hh], vb, NT, preferred_element_type=F32)
                    ds = p * (dp - delta[hh])
                    df_ref[hh:hh + 1, krows] -= _colsum(ds)
                    rowsums.append(carry[1 + hh] + jnp.sum(ds, axis=-1, keepdims=True))
                    ds_b = ds.astype(BF16)
                    dqs.append(jnp.dot(ds_b, kb, preferred_element_type=F32))
                    dk = dk + jnp.dot(qmt[hh], ds_b, preferred_element_type=F32)
                    dv = dv + jnp.dot(dmt[hh], p.astype(BF16), preferred_element_type=F32)
                dk_acc[:, krows] += dk
                dv_acc[:, krows] += dv
                return (carry[0] + jnp.where(m0, dqs[0], dqs[1]), *rowsums)

            zero = jnp.zeros((tq, 1), F32)
            dq, rs_a, rs_b = _sweep(step, i, FOX_BWD_UNROLL, (jnp.zeros((tq, LANES), F32), zero, zero))
            dqkv_ref[0, qrows, :] = (dq * scale).astype(BF16)
            dfq_ref[qrows, 0:1] = rs_a
            dfq_ref[qrows, 1:2] = rs_b
            return 0

        lax.fori_loop(0, nq, q_block, 0)
        _store_transposed(dqkv_ref, dk_acc, dv_acc, nq, tq)

    q_spec, k_spec, v_spec = _qkv_specs(seq, nq, False)
    col_spec = pl.BlockSpec((None, seq, 2), lambda b, p: (p, b, 0))
    row_spec = pl.BlockSpec((None, 2, seq), lambda b, p: (p, 0, b))
    tile = pl.BlockSpec((seq, LANES), lambda b, p: (b, p))
    return _call(body, "fox_bwd", (_sds((3, t, d), BF16), _sds((npair, 2, t), F32), _sds((npair, t, 2), F32)), grid=(nb, npair),
                 in_specs=[q_spec, k_spec, v_spec, row_spec, tile, tile, col_spec],
                 out_specs=(pl.BlockSpec((3, seq, LANES), lambda b, p: (0, b, p)), row_spec, col_spec),
                 scratch=[pltpu.VMEM((LANES, seq), F32), pltpu.VMEM((LANES, seq), F32)],
                 vmem=VMEM_BIG)(qkv, qkv, qkv, frow, o, do, lse)


def _split2(x):
    hi = x.astype(BF16)
    return hi, (x - hi.astype(F32)).astype(BF16)


def _sum_right(x, tri):
    hi, lo = _split2(x)
    return jnp.dot(hi, tri, preferred_element_type=F32) + jnp.dot(lo, tri, preferred_element_type=F32)


def _sb_scores(qm_h, kb, mask):
    z = lax.dot_general(qm_h, kb, NT, preferred_element_type=F32)
    lb = _log_sigmoid(z)
    l1m = lb - z
    if mask is not None:
        l1m = jnp.where(mask, l1m, 0.0)
    return lb, l1m


def _sb_fwd(qkv, nb, seq):
    _, t, d = qkv.shape
    npair = d // LANES
    tq = TQ_(seq)
    nq = seq // tq
    scale = HEAD_DIM ** -0.5

    def body(q_ref, k_ref, v_ref, o_ref, lt_ref):
        i = pl.program_id(2)
        m0 = _head_masks()
        qm = _two_heads(q_ref[...] * scale, m0)
        row = lax.broadcasted_iota(jnp.int32, (tq, tq), 0)
        col = lax.broadcasted_iota(jnp.int32, (tq, tq), 1)
        after = jnp.where(row > col, 1.0, 0.0).astype(BF16)

        def step(j, carry, diag):
            off = pl.multiple_of(j * tq, tq)
            kb = k_ref[pl.ds(off, tq), :]
            vb = v_ref[pl.ds(off, tq), :]
            mask = (col < row) if diag else None
            nxt, parts = [], []
            for hh in range(2):
                lb, l1m = _sb_scores(qm[hh], kb, mask)
                rest = _sum_right(l1m, after) + carry[hh]
                a = jnp.exp(lb + rest)
                if diag:
                    a = jnp.where(mask, a, 0.0)
                parts.append(jnp.dot(a.astype(BF16), vb, preferred_element_type=F32))
                nxt.append(carry[hh] + jnp.sum(l1m, axis=-1, keepdims=True))
            return (*nxt, carry[2] + jnp.where(m0, parts[0], parts[1]))

        zero = jnp.zeros((tq, 1), F32)
        init = (zero, zero, jnp.zeros((tq, LANES), F32))
        carry = lax.cond(i > 0, lambda c: step(i - 1, step(i, c, True), False), lambda c: step(i, c, True), init)

        def alive(st):
            return (st[0] < i) & (jnp.max(jnp.maximum(st[1], st[2])) > SB_DEAD)

        def more(st):
            return (st[0] + 1, *step(i - 1 - st[0], st[1:], False))

        done, lt_a, lt_b, acc = lax.while_loop(alive, more, (jnp.minimum(i, 1), *carry))
        o_ref[...] = acc
        lt_ref[:, 0:1] = lt_a
        lt_ref[:, 1:2] = lt_b
        lt_ref[:, 2:3] = jnp.zeros((tq, 1), F32) + done.astype(F32)
        lt_ref[:, 3:4] = zero

    q_spec, k_spec, v_spec = _qkv_specs(seq, nq, True)
    return _call(body, "sb_fwd", (_sds((t, d), F32), _sds((npair, t, 4), F32)), grid=(nb, npair, nq),
                 in_specs=[q_spec, k_spec, v_spec],
                 out_specs=(pl.BlockSpec((tq, LANES), lambda b, p, i: (b * nq + i, p)),
                            pl.BlockSpec((None, tq, 4), lambda b, p, i: (p, b * nq + i, 0))), vmem=VMEM_BIG)(qkv, qkv, qkv)


def _sb_bwd(qkv, do, ltot, nb, seq):
    _, t, d = qkv.shape
    npair = d // LANES
    tq = TQ_(seq)
    nq = seq // tq
    scale = HEAD_DIM ** -0.5

    def body(q_ref, k_ref, v_ref, do_ref, lt_ref, dqkv_ref, dk_acc, dv_acc):
        m0 = _head_masks()
        row = lax.broadcasted_iota(jnp.int32, (tq, tq), 0)
        col = lax.broadcasted_iota(jnp.int32, (tq, tq), 1)
        upto = jnp.where(row <= col, 1.0, 0.0).astype(BF16)
        left_of = jnp.where(row < col, 1.0, 0.0).astype(BF16)
        dk_acc[...] = jnp.zeros_like(dk_acc)
        dv_acc[...] = jnp.zeros_like(dv_acc)

        def q_block(i, _):
            qoff = pl.multiple_of(i * tq, tq)
            qrows = pl.ds(qoff, tq)
            qm = _two_heads(q_ref[qrows, :] * scale, m0)
            dm = _two_heads(do_ref[qrows, :].astype(BF16), m0)
            qmt = _transposed(qm)
            dmt = _transposed(dm)
            ltot = (lt_ref[qrows, 0:1], lt_ref[qrows, 1:2])

            def step(j, carry, diag):
                off = pl.multiple_of(j * tq, tq)
                krows = pl.ds(off, tq)
                kb = k_ref[krows, :]
                vb = v_ref[krows, :]
                mask = (col < row) if diag else None
                nxt, dqs = [], []
                dk = jnp.zeros((LANES, tq), F32)
                dv = jnp.zeros((LANES, tq), F32)
                for hh in range(2):
                    cl, ce = carry[2 * hh], carry[2 * hh + 1]
                    lb, l1m = _sb_scores(qm[hh], kb, mask)
                    a = jnp.exp(lb + (ltot[hh] - (_sum_right(l1m, upto) + cl)))
                    if diag:
                        a = jnp.where(mask, a, 0.0)
                    e = lax.dot_general(dm[hh], vb, NT, preferred_element_type=F32) * a
                    before = _sum_right(e, left_of) + ce
                    beta = jnp.exp(lb)
                    dz = e * (1.0 - beta) - before * beta
                    if diag:
                        dz = jnp.where(mask, dz, 0.0)
                    dz_b = dz.astype(BF16)
                    dqs.append(jnp.dot(dz_b, kb, preferred_element_type=F32))
                    dk = dk + jnp.dot(qmt[hh], dz_b, preferred_element_type=F32)
                    dv = dv + jnp.dot(dmt[hh], a.astype(BF16), preferred_element_type=F32)
                    nxt += [cl + jnp.sum(l1m, axis=-1, keepdims=True), ce + jnp.sum(e, axis=-1, keepdims=True)]
                dk_acc[:, krows] += dk
                dv_acc[:, krows] += dv
                return (*nxt, carry[4] + jnp.where(m0, dqs[0], dqs[1]))

            zero = jnp.zeros((tq, 1), F32)
            visited = jnp.max(lt_ref[qrows, 2:3]).astype(jnp.int32)
            carry = _sweep(step, visited, SB_BWD_UNROLL, (zero, zero, zero, zero, jnp.zeros((tq, LANES), F32)), start=i - visited)
            dqkv_ref[0, qrows, :] = (carry[4] * scale).astype(BF16)
            return 0

        lax.fori_loop(0, nq, q_block, 0)
        _store_transposed(dqkv_ref, dk_acc, dv_acc, nq, tq)

    q_spec, k_spec, v_spec = _qkv_specs(seq, nq, False)
    tile = pl.BlockSpec((seq, LANES), lambda b, p: (b, p))
    return _call(body, "sb_bwd", _sds((3, t, d), BF16), grid=(nb, npair),
                 in_specs=[q_spec, k_spec, v_spec, tile, pl.BlockSpec((None, seq, 4), lambda b, p: (p, b, 0))],
                 out_specs=pl.BlockSpec((3, seq, LANES), lambda b, p: (0, b, p)),
                 scratch=[pltpu.VMEM((LANES, seq), F32), pltpu.VMEM((LANES, seq), F32)], vmem=VMEM_BIG)(qkv, qkv, qkv, do, ltot)


def _qkv_proj(name, h, w):
    t, d = h.shape
    tm = min(TMM, t)
    return _mm(name, h, w, _sds((3, t, d), BF16), (3, t // tm, 1),
               pl.BlockSpec((tm, d), lambda s, i, r: (i, 0)), pl.BlockSpec((d, d), lambda s, i, r: (0, s)),
               pl.BlockSpec((None, tm, d), lambda s, i, r: (s, i, 0)), NN)


def _qkv_dw(name, h, dqkv):
    t, d = h.shape
    tk = min(TMM, t)

    def terms(h_ref, g_ref):
        hv = h_ref[...]
        return [_dot(hv, g_ref[s], TN) for s in range(3)]

    def store(o_ref, acc_ref):
        for s in range(3):
            o_ref[:, s * d:(s + 1) * d] = acc_ref[s].astype(o_ref.dtype)

    return _accumulate_over_tokens(name, (h, dqkv), [pl.BlockSpec((tk, d), lambda r: (r, 0)), pl.BlockSpec((3, tk, d), lambda r: (0, r, 0))],
                                   _sds((d, 3 * d), GRAD_WIRE), pl.BlockSpec((d, 3 * d), lambda r: (0, 0)), (3, d, d), t // tk, terms, store)


def _qkv_dh(name, dqkv, w):
    _, t, d = dqkv.shape
    tm = min(TM, t)

    def body(g_ref, w_ref, o_ref):
        acc = None
        for s in range(3):
            term = _dot(g_ref[s], w_ref[:, s * d:(s + 1) * d], NT)
            acc = term if acc is None else acc + term
        o_ref[...] = acc

    return _call(body, name, _sds((t, d), F32), grid=(t // tm,),
                 in_specs=[pl.BlockSpec((3, tm, d), lambda i: (0, i, 0)), pl.BlockSpec((d, 3 * d), lambda i: (0, 0))],
                 out_specs=pl.BlockSpec((tm, d), lambda i: (i, 0)), vmem=VMEM_BIG)(dqkv, w)


def _glu(pre_block, d):
    return pre_block[:, :d] * jax.nn.sigmoid(pre_block[:, d:])


def _shifted_copies(ext_ref, sh_ref, tt):
    for r in range(1, SUBLANES):
        sh_ref[r - 1] = ext_ref[pl.ds(r, tt + CONV_HALO - SUBLANES), :]


def _rows_from(ext_ref, sh_ref, base, offset, n):
    q, r = divmod(offset, SUBLANES)
    if r == 0:
        return ext_ref[pl.ds(pl.multiple_of(base + offset, SUBLANES), n), :]
    return sh_ref[r - 1, pl.ds(pl.multiple_of(base + q * SUBLANES, SUBLANES), n), :]


def _cv_conv_fwd(pre, dw, dwb, lng, lnb, seq):
    t, d2 = pre.shape
    d = d2 // 2
    tt = min(TT, seq)
    nt = seq // tt
    hb = tt // CONV_HALO

    def body(pre_ref, halo_ref, dw_ref, dwb_ref, lng_ref, lnb_ref, y1_ref, y2_ref, ext_ref, sh_ref):
        i = pl.program_id(1)
        ext_ref[0:CONV_HALO, :] = jnp.where(i == 0, 0.0, _glu(halo_ref[...], d))
        ext_ref[CONV_HALO:, :] = _glu(pre_ref[...], d)
        _shifted_copies(ext_ref, sh_ref, tt)

        acc = jnp.zeros((tt, d), F32) + dwb_ref[...]
        for k in range(CONV_WIDTH):
            acc = acc + _rows_from(ext_ref, sh_ref, 0, CONV_HALO - (CONV_WIDTH - 1) + k, tt) * dw_ref[k:k + 1, :]
        y1_ref[...] = acc
        yc, rstd = _ln_stats(acc)
        y2_ref[...] = _silu(yc * rstd * lng_ref[...] + lnb_ref[...]).astype(BF16)

    vec = pl.BlockSpec((1, d), lambda b, i: (0, 0))
    tile = pl.BlockSpec((tt, d), lambda b, i: (b * nt + i, 0))
    return _call(body, "cv_conv_fwd", (_sds((t, d), F32), _sds((t, d), BF16)), grid=(t // seq, nt),
                 in_specs=[pl.BlockSpec((tt, d2), lambda b, i: (b * nt + i, 0)),
                           pl.BlockSpec((CONV_HALO, d2), lambda b, i: (jnp.maximum((b * nt + i) * hb - 1, 0), 0)),
                           pl.BlockSpec((CONV_HALO, d), lambda b, i: (0, 0)), vec, vec, vec],
                 out_specs=(tile, tile),
                 scratch=[pltpu.VMEM((tt + CONV_HALO, d), F32), pltpu.VMEM((SUBLANES - 1, tt + CONV_HALO - SUBLANES, d), F32)],
                 vmem=VMEM_BIG)(pre, pre, dw, dwb, lng, lnb)


def _cv_norm_bwd(y1, dy2, lng, lnb):
    t, d = y1.shape
    tm = min(TM, t)

    def body(y1_ref, dy2_ref, lng_ref, lnb_ref, dy1_ref, dlg_ref, dlb_ref, dsum_ref):
        @pl.when(pl.program_id(0) == 0)
        def _():
            dlg_ref[...] = jnp.zeros_like(dlg_ref)
            dlb_ref[...] = jnp.zeros_like(dlb_ref)
            dsum_ref[...] = jnp.zeros_like(dsum_ref)

        yc, rstd = _ln_stats(y1_ref[...])
        yhat = yc * rstd
        n = yhat * lng_ref[...] + lnb_ref[...]
        sg = jax.nn.sigmoid(n)
        dn = dy2_ref[...] * (sg * (1.0 + n * (1.0 - sg)))
        dlg_ref[...] += _colsum(dn * yhat)
        dlb_ref[...] += _colsum(dn)
        dyh = dn * lng_ref[...]
        m1 = jnp.mean(dyh, axis=-1, keepdims=True)
        m2 = jnp.mean(dyh * yhat, axis=-1, keepdims=True)
        dy1 = rstd * (dyh - m1 - yhat * m2)
        dy1_ref[...] = dy1
        dsum_ref[...] += _colsum(dy1)

    return _call(body, "cv_norm_bwd", (_sds((t, d), F32), _sds((1, d), F32), _sds((1, d), F32), _sds((1, d), F32)),
                 grid=(t // tm,), in_specs=[_tile_spec(tm, d), _tile_spec(tm, d), _vec_spec(d), _vec_spec(d)],
                 out_specs=(_tile_spec(tm, d), _vec_spec(d), _vec_spec(d), _vec_spec(d)))(y1, dy2, lng, lnb)


def _cv_conv_bwd(pre, dy1, dw, seq):
    t, d2 = pre.shape
    d = d2 // 2
    tt = min(TT // 2, seq)
    nt = seq // tt
    hb = tt // CONV_HALO
    last_halo = t // CONV_HALO - 1

    def body(pre_ref, halo_ref, dy_ref, dyn_ref, dw_ref, dpre_ref, ddw_ref, dbin_ref, ext_ref, dext_ref, sh_ref, dsh_ref):
        b, i = pl.program_id(0), pl.program_id(1)

        @pl.when((b == 0) & (i == 0))
        def _():
            ddw_ref[...] = jnp.zeros_like(ddw_ref)
            dbin_ref[...] = jnp.zeros_like(dbin_ref)

        pv = pre_ref[...]
        ext_ref[0:CONV_HALO, :] = jnp.where(i == 0, 0.0, _glu(halo_ref[...], d))
        ext_ref[CONV_HALO:, :] = _glu(pv, d)
        dyv = dy_ref[...]
        dext_ref[0:tt, :] = dyv
        dext_ref[tt:, :] = jnp.where(i == nt - 1, 0.0, dyn_ref[...])
        _shifted_copies(ext_ref, sh_ref, tt)
        _shifted_copies(dext_ref, dsh_ref, tt)
        nrows = tt // CONV_ROWS

        def input_grad(c, _):
            r0 = pl.multiple_of(c * CONV_ROWS, CONV_ROWS)
            dy0 = jnp.zeros((CONV_ROWS, d), F32)
            for k in range(CONV_WIDTH):
                dy0 = dy0 + _rows_from(dext_ref, dsh_ref, r0, CONV_WIDTH - 1 - k, CONV_ROWS) * dw_ref[k:k + 1, :]
            rows = pl.ds(r0, CONV_ROWS)
            a = pre_ref[rows, :d]
            sg = jax.nn.sigmoid(pre_ref[rows, d:])
            da = dy0 * sg
            dg = dy0 * a * sg * (1.0 - sg)
            dpre_ref[rows, :d] = da.astype(BF16)
            dpre_ref[rows, d:] = dg.astype(BF16)
            dbin_ref[:, :d] += _colsum(da)
            dbin_ref[:, d:] += _colsum(dg)
            return 0

        lax.fori_loop(0, nrows, input_grad, 0)

        for k in range(CONV_WIDTH):
            def tap_grad(c, part, k=k):
                r0 = pl.multiple_of(c * CONV_ROWS, CONV_ROWS)
                prod = dy_ref[pl.ds(r0, CONV_ROWS), :] * _rows_from(ext_ref, sh_ref, r0, CONV_HALO - (CONV_WIDTH - 1) + k, CONV_ROWS)
                return part + jnp.sum(prod.reshape(CONV_ROWS // SUBLANES, SUBLANES, d), axis=0)

            ddw_ref[k:k + 1, :] += _colsum(lax.fori_loop(0, nrows, tap_grad, jnp.zeros((SUBLANES, d), F32)))

    return _call(body, "cv_conv_bwd", (_sds((t, d2), BF16), _sds((CONV_HALO, d), F32), _sds((1, d2), F32)), grid=(t // seq, nt),
                 in_specs=[pl.BlockSpec((tt, d2), lambda b, i: (b * nt + i, 0)),
                           pl.BlockSpec((CONV_HALO, d2), lambda b, i: (jnp.maximum((b * nt + i) * hb - 1, 0), 0)),
                           pl.BlockSpec((tt, d), lambda b, i: (b * nt + i, 0)),
                           pl.BlockSpec((CONV_HALO, d), lambda b, i: (jnp.minimum((b * nt + i + 1) * hb, last_halo), 0)),
                           pl.BlockSpec((CONV_HALO, d), lambda b, i: (0, 0))],
                 out_specs=(pl.BlockSpec((tt, d2), lambda b, i: (b * nt + i, 0)),
                            pl.BlockSpec((CONV_HALO, d), lambda b, i: (0, 0)), pl.BlockSpec((1, d2), lambda b, i: (0, 0))),
                 scratch=[pltpu.VMEM((tt + CONV_HALO, d), F32), pltpu.VMEM((tt + CONV_HALO, d), F32),
                          pltpu.VMEM((SUBLANES - 1, tt + CONV_HALO - SUBLANES, d), F32),
                          pltpu.VMEM((SUBLANES - 1, tt + CONV_HALO - SUBLANES, d), F32)],
                 vmem=VMEM_BIG)(pre, pre, dy1, dy1, dw)


def _adamw(name, w, m, v, g=None, parts=None):
    rows, cols = w.shape
    tr = rows
    for cand in ((512,) if parts is None else ()) + (256, 128, 64, 32, 16, 8):
        if rows % cand == 0 and rows > cand:
            tr = cand
            break
    bc1 = 1.0 - ADAM_B1 ** ADAM_STEP
    bc2 = 1.0 - ADAM_B2 ** ADAM_STEP

    def body(w_ref, m_ref, v_ref, g_ref, go_ref, d_ref, mo_ref, vo_ref):
        if parts is None:
            gv = g_ref[...]
        else:
            gv = g_ref[0].astype(F32)
            for s in range(1, N_DEV):
                gv = gv + g_ref[s].astype(F32)
        mn = ADAM_B1 * m_ref[...] + (1.0 - ADAM_B1) * gv
        vn = ADAM_B2 * v_ref[...] + (1.0 - ADAM_B2) * (gv * gv)
        m_hat = mn / bc1
        v_hat = vn / bc2
        go_ref[...] = gv
        d_ref[...] = -ADAM_LR * (m_hat / (jnp.sqrt(v_hat) + ADAM_EPS) + ADAM_WD * w_ref[...])
        mo_ref[...] = mn
        vo_ref[...] = vn

    blk = pl.BlockSpec((tr, cols), lambda i: (i, 0))
    g_in, g_spec = (g, blk) if parts is None else (parts, pl.BlockSpec((N_DEV, tr, cols), lambda i: (0, i, 0)))
    out = _sds((rows, cols), F32)
    return _call(body, name, (out, out, out, out), grid=(rows // tr,), in_specs=[blk, blk, blk, g_spec],
                 out_specs=(blk, blk, blk, blk), vmem=VMEM_BIG)(w, m, v, g_in)


def _pad_rows(a, rows):
    return jnp.pad(a, ((0, rows - a.shape[0]), (0, 0)))


def _full_cols(gathered, n):
    k = gathered.shape[2]
    return jnp.transpose(gathered[0], (1, 0, 2)).reshape(k, N_DEV * n)


def _col_blocks(full, n):
    k = full.shape[0]
    return jnp.transpose(full.reshape(k, N_DEV, n), (1, 0, 2))[None]


def kernel(x, c, mod_w, mod_b, ln1_g, ln1_b, ln2_g, ln2_b, ffn_w_in, ffn_w_out, gm_w_in, gm_b_in, gm_ln_g, gm_ln_b, gm_w_s, gm_b_s, gm_w_out, fox_w_in, fox_b_f, fox_w_out, sb_w_in, sb_w_out, cv_w_in, cv_b_in, cv_dw, cv_dw_b, cv_ln_g, cv_ln_b, cv_w_out, cv_b_out, loss_target, m_mod_w, m_mod_b, m_ln1_g, m_ln1_b, m_ln2_g, m_ln2_b, m_ffn_w_in, m_ffn_w_out, m_gm_w_in, m_gm_b_in, m_gm_ln_g, m_gm_ln_b, m_gm_w_s, m_gm_b_s, m_gm_w_out, m_fox_w_in, m_fox_b_f, m_fox_w_out, m_sb_w_in, m_sb_w_out, m_cv_w_in, m_cv_b_in, m_cv_dw, m_cv_dw_b, m_cv_ln_g, m_cv_ln_b, m_cv_w_out, m_cv_b_out, v_mod_w, v_mod_b, v_ln1_g, v_ln1_b, v_ln2_g, v_ln2_b, v_ffn_w_in, v_ffn_w_out, v_gm_w_in, v_gm_b_in, v_gm_ln_g, v_gm_ln_b, v_gm_w_s, v_gm_b_s, v_gm_w_out, v_fox_w_in, v_fox_b_f, v_fox_w_out, v_sb_w_in, v_sb_w_out, v_cv_w_in, v_cv_b_in, v_cv_dw, v_cv_dw_b, v_cv_ln_g, v_cv_ln_b, v_cv_w_out, v_cv_b_out):
    weights = dict(mod_w=mod_w, mod_b=mod_b, ln1_g=ln1_g, ln1_b=ln1_b, ln2_g=ln2_g, ln2_b=ln2_b, ffn_w_in=ffn_w_in, ffn_w_out=ffn_w_out, gm_w_in=gm_w_in, gm_b_in=gm_b_in, gm_ln_g=gm_ln_g, gm_ln_b=gm_ln_b, gm_w_s=gm_w_s, gm_b_s=gm_b_s, gm_w_out=gm_w_out, fox_w_in=fox_w_in, fox_b_f=fox_b_f, fox_w_out=fox_w_out, sb_w_in=sb_w_in, sb_w_out=sb_w_out, cv_w_in=cv_w_in, cv_b_in=cv_b_in, cv_dw=cv_dw, cv_dw_b=cv_dw_b, cv_ln_g=cv_ln_g, cv_ln_b=cv_ln_b, cv_w_out=cv_w_out, cv_b_out=cv_b_out)
    mom1 = dict(mod_w=m_mod_w, mod_b=m_mod_b, ln1_g=m_ln1_g, ln1_b=m_ln1_b, ln2_g=m_ln2_g, ln2_b=m_ln2_b, ffn_w_in=m_ffn_w_in, ffn_w_out=m_ffn_w_out, gm_w_in=m_gm_w_in, gm_b_in=m_gm_b_in, gm_ln_g=m_gm_ln_g, gm_ln_b=m_gm_ln_b, gm_w_s=m_gm_w_s, gm_b_s=m_gm_b_s, gm_w_out=m_gm_w_out, fox_w_in=m_fox_w_in, fox_b_f=m_fox_b_f, fox_w_out=m_fox_w_out, sb_w_in=m_sb_w_in, sb_w_out=m_sb_w_out, cv_w_in=m_cv_w_in, cv_b_in=m_cv_b_in, cv_dw=m_cv_dw, cv_dw_b=m_cv_dw_b, cv_ln_g=m_cv_ln_g, cv_ln_b=m_cv_ln_b, cv_w_out=m_cv_w_out, cv_b_out=m_cv_b_out)
    mom2 = dict(mod_w=v_mod_w, mod_b=v_mod_b, ln1_g=v_ln1_g, ln1_b=v_ln1_b, ln2_g=v_ln2_g, ln2_b=v_ln2_b, ffn_w_in=v_ffn_w_in, ffn_w_out=v_ffn_w_out, gm_w_in=v_gm_w_in, gm_b_in=v_gm_b_in, gm_ln_g=v_gm_ln_g, gm_ln_b=v_gm_ln_b, gm_w_s=v_gm_w_s, gm_b_s=v_gm_b_s, gm_w_out=v_gm_w_out, fox_w_in=v_fox_w_in, fox_b_f=v_fox_b_f, fox_w_out=v_fox_w_out, sb_w_in=v_sb_w_in, sb_w_out=v_sb_w_out, cv_w_in=v_cv_w_in, cv_b_in=v_cv_b_in, cv_dw=v_cv_dw, cv_dw_b=v_cv_dw_b, cv_ln_g=v_cv_ln_g, cv_ln_b=v_cv_ln_b, cv_w_out=v_cv_w_out, cv_b_out=v_cv_b_out)
    names = list(weights)

    nb, seq, d = x.shape
    t = nb * seq
    nl = mod_w.shape[0]
    alpha = (2.0 * nl) ** 0.25
    me = 4 * lax.axis_index("x") + 2 * lax.axis_index("y") + lax.axis_index("c")
    xs = x.reshape(t, d)
    tgt = loss_target.reshape(t, d)
    n_mod = mod_w.shape[2]
    n_ffn = ffn_w_in.shape[2]
    n_heads = d // HEAD_DIM
    npair = d // LANES

    c_all = _exchange_small(_pad_rows(c, 8), "gather_c", False)[:, :nb].reshape(N_DEV * nb, d)
    mod_b_loc = lax.dynamic_slice_in_dim(mod_b, me * n_mod, n_mod, axis=1)[:, None, :]
    mod_loc = _mod_fwd(c_all, mod_w, mod_b_loc)
    mod_g = _exchange_small(mod_loc.reshape(nl * N_DEV * nb, n_mod), "gather_mod", False)
    mod_all = jnp.transpose(mod_g.reshape(N_DEV, nl, N_DEV * nb, n_mod), (1, 2, 0, 3)).reshape(nl, N_DEV * nb, N_DEV * n_mod)
    mod_me = lax.dynamic_slice_in_dim(mod_all, me * nb, nb, axis=1)
    mods = [[mod_me[l, :, k * d:(k + 1) * d][:, None, :] for k in range(6)] for l in range(nl)]

    big = ["ffn_w_in", "ffn_w_out", "gm_w_in", "gm_w_out", "fox_w_in", "fox_w_out", "sb_w_in", "sb_w_out", "cv_w_in", "cv_w_out"]
    gathered = dict(zip(big, _gather_weights([weights[n].astype(BF16) for n in big])))
    w_ffn_in = gathered["ffn_w_in"]
    w_ffn_out = gathered["ffn_w_out"].reshape(nl, N_DEV // 2, n_ffn, d)
    w_ffn_out_rows = w_ffn_out.reshape(nl * (N_DEV // 2), n_ffn, d)
    sq = lambda n: gathered[n].reshape(d, d)
    fox_full = _full_cols(gathered["fox_w_in"], fox_w_in.shape[2])
    fox_qkv_w, fox_f_wt = fox_full[:, :3 * d], jnp.transpose(fox_full[:, 3 * d:])
    sb_qkv_w = _full_cols(gathered["sb_w_in"], sb_w_in.shape[2])
    cvp = d // N_DEV
    cv_small = jnp.concatenate([_pad_rows(cv_dw[0], CONV_HALO), cv_dw_b, cv_ln_g, cv_ln_b, cv_b_out,
                                cv_b_in.reshape(2, cvp), jnp.zeros((2, cvp), F32)], axis=0)
    cv_all = _exchange_small(cv_small, "gather_cv_small", False)
    cv_rows = jnp.transpose(cv_all, (1, 0, 2)).reshape(cv_small.shape[0], d)
    cv_dw_f, cv_dwb_f, cv_lng_f, cv_lnb_f, cv_bout_f = (cv_rows[:CONV_HALO], cv_rows[32:33], cv_rows[33:34], cv_rows[34:35], cv_rows[35:36])
    cv_bin_f = cv_all[:, 36:38, :].reshape(1, 2 * d)

    saved = []
    h = _modulate(xs, mods[0][1], mods[0][0], seq)
    xin = xs
    for l in range(nl):
        kind = l % 4
        sv = dict(x=xin, h=h)
        if kind == 0:
            pre = _proj_cols("gm_in", h, gathered["gm_w_in"], gm_w_in.shape[2], bias=gm_b_in)
            yv = _gm_spatial_fwd(pre, gm_ln_g, gm_ln_b, gm_w_s[0], jnp.transpose(gm_b_s[0]), seq)
            y = _mm_plain("gm_out", yv, sq("gm_w_out"), NN)
            sv.update(pre=pre, yv=yv)
        elif kind == 1:
            qkv = _qkv_proj("fox_qkv", h, fox_qkv_w)
            ft = _mm("fox_gate_proj", fox_f_wt, h, _sds((n_heads, t), F32), (t // min(TMM, t), 1),
                     pl.BlockSpec((n_heads, d), lambda i, r: (0, 0)), pl.BlockSpec((min(TMM, t), d), lambda i, r: (i, 0)),
                     pl.BlockSpec((n_heads, min(TMM, t)), lambda i, r: (0, i)), NT)
            b_f = jnp.transpose(fox_b_f)
            frow_p = _fox_gate_fwd(ft, b_f, seq).reshape(npair, 2, t)
            o, lse = _fox_fwd(qkv, frow_p, nb, seq)
            y = _mm_plain("fox_out", o, sq("fox_w_out"), NN)
            sv.update(qkv=qkv, ft=ft, b_f=b_f, frow=frow_p, o=o, lse=lse)
        elif kind == 2:
            qkv = _qkv_proj("sb_qkv", h, sb_qkv_w)
            o, ltot = _sb_fwd(qkv, nb, seq)
            y = _mm_plain("sb_out", o, sq("sb_w_out"), NN)
            sv.update(qkv=qkv, o=o, ltot=ltot)
        else:
            pre = _proj_cols("cv_in", h, gathered["cv_w_in"], cv_w_in.shape[2], bias=cv_bin_f)
            y1, y2 = _cv_conv_fwd(pre, cv_dw_f, cv_dwb_f, cv_lng_f, cv_lnb_f, seq)
            y = _mm_plain("cv_out", y2, sq("cv_w_out"), NN, bias=cv_bout_f)
            sv.update(pre=pre, y1=y1, y2=y2)
        x1, h2 = _lnres_fwd(xin, y, mods[l][2], ln1_g[l:l + 1], ln1_b[l:l + 1], alpha, seq, nxt=(mods[l][4], mods[l][3]))
        hg, hu, act = _ffn_in(h2, w_ffn_in, l)
        y2f = _ffn_out(act, w_ffn_out_rows, l * (N_DEV // 2))
        sv.update(y=y, x1=x1, h2=h2, hg=hg, hu=hu, act=act, y2f=y2f)
        if l + 1 < nl:
            xin, h = _lnres_fwd(x1, y2f, mods[l][5], ln2_g[l:l + 1], ln2_b[l:l + 1], alpha, seq, nxt=(mods[l + 1][1], mods[l + 1][0]))
        else:
            xin = _lnres_fwd(x1, y2f, mods[l][5], ln2_g[l:l + 1], ln2_b[l:l + 1], alpha, seq)
        saved.append(sv)

    dx, sq_err = _loss_head(xin, tgt)
    loss = lax.psum(0.5 * jnp.sum(sq_err) / d, ("x", "y", "c"))

    small = {}
    bigg = {}
    dmods = [None] * nl
    d_ln = dict(ln1_g=[None] * nl, ln1_b=[None] * nl, ln2_g=[None] * nl, ln2_b=[None] * nl)
    ffn_gi, ffn_go = [None] * nl, [None] * nl
    for l in reversed(range(nl)):
        sv = saved[l]
        kind = l % 4
        dxr, dy2, dlg, dlb, _, dgate2 = _lnres_bwd(dx, sv["x1"], sv["y2f"], mods[l][5], ln2_g[l:l + 1], alpha, seq)
        d_ln["ln2_g"][l], d_ln["ln2_b"][l] = dlg, dlb
        dg_, du_ = _ffn_dact(dy2, w_ffn_out_rows, sv["hg"], sv["hu"], l * (N_DEV // 2))
        dwg, dwu, dwo = _ffn_bwd_weights(sv["h2"], dy2, sv["act"], dg_, du_)
        ffn_gi[l] = jnp.concatenate([dwg, dwu], axis=0)
        ffn_go[l] = dwo.reshape(N_DEV, n_ffn // 2, d)
        dh2 = _ffn_dh(dg_, du_, w_ffn_in, l)
        dx1, dsc2, dsh2 = _modulate_bwd(dh2, dxr, sv["x1"], mods[l][4], seq)
        dxr, dy, dlg, dlb, dysum, dgate1 = _lnres_bwd(dx1, sv["x"], sv["y"], mods[l][2], ln1_g[l:l + 1], alpha, seq)
        d_ln["ln1_g"][l], d_ln["ln1_b"][l] = dlg, dlb
        hh = sv["h"]
        if kind == 0:
            dyv = _mm_plain("gm_out_bwd", dy, sq("gm_w_out"), NT)
            bigg["gm_w_out"] = _mm_plain("gm_out_dw", sv["yv"], dy, TN, GRAD_WIRE).reshape(1, N_DEV, d // N_DEV, d)
            dpre, dws, dbst, dlng, dlnb, dbin = _gm_spatial_bwd(sv["pre"], dyv, gm_ln_g, gm_ln_b, gm_w_s[0], jnp.transpose(gm_b_s[0]), seq)
            small.update(gm_w_s=dws[None], gm_b_s=jnp.transpose(dbst)[None], gm_ln_g=dlng, gm_ln_b=dlnb, gm_b_in=dbin)
            bigg["gm_w_in"] = _grad_cols("gm_in_dw", hh, dpre, gm_w_in.shape[2])
            dh = _back_cols("gm_in_bwd", dpre, gathered["gm_w_in"], gm_w_in.shape[2])
        elif kind == 1:
            do = _mm_plain("fox_out_bwd", dy, sq("fox_w_out"), NT)
            bigg["fox_w_out"] = _mm_plain("fox_out_dw", sv["o"], dy, TN, GRAD_WIRE).reshape(1, N_DEV, d // N_DEV, d)
            dqkv, dfr, dfq = _fox_bwd(sv["qkv"], sv["frow"], sv["o"], do, sv["lse"], nb, seq)
            dft, dbf = _fox_gate_bwd(sv["ft"], sv["b_f"], dfr.reshape(n_heads, t),
                                     jnp.transpose(dfq, (0, 2, 1)).reshape(n_heads, t), seq)
            small["fox_b_f"] = jnp.transpose(dbf)
            dw_qkv = _qkv_dw("fox_qkv_dw", hh, dqkv)
            tk = min(TMM, t)
            dw_ft = _mm("fox_gate_dw", dft, hh, _sds((n_heads, d), F32), (1, t // tk),
                        pl.BlockSpec((n_heads, tk), lambda j, r: (0, r)), pl.BlockSpec((tk, d), lambda j, r: (r, 0)),
                        pl.BlockSpec((n_heads, d), lambda j, r: (0, 0)), NN)
            bigg["fox_w_in"] = _col_blocks(jnp.concatenate([dw_qkv, jnp.transpose(dw_ft).astype(GRAD_WIRE)], axis=1), fox_w_in.shape[2])
            dh_a = _qkv_dh("fox_qkv_bwd", dqkv, fox_qkv_w)
            tm = min(TMM, t)
            dh = _mm("fox_gate_bwd_h", dft, fox_f_wt, _sds((t, d), F32), (t // tm, 1),
                     pl.BlockSpec((n_heads, tm), lambda i, r: (0, i)), pl.BlockSpec((n_heads, d), lambda i, r: (0, 0)),
                     pl.BlockSpec((tm, d), lambda i, r: (i, 0)), TN, (dh_a,), (pl.BlockSpec((tm, d), lambda i, r: (i, 0)),), _add)
        elif kind == 2:
            do = _mm_plain("sb_out_bwd", dy, sq("sb_w_out"), NT)
            bigg["sb_w_out"] = _mm_plain("sb_out_dw", sv["o"], dy, TN, GRAD_WIRE).reshape(1, N_DEV, d // N_DEV, d)
            dqkv = _sb_bwd(sv["qkv"], do, sv["ltot"], nb, seq)
            bigg["sb_w_in"] = _col_blocks(_qkv_dw("sb_qkv_dw", hh, dqkv), sb_w_in.shape[2])
            dh = _qkv_dh("sb_qkv_bwd", dqkv, sb_qkv_w)
        else:
            dy2c = _mm_plain("cv_out_bwd", dy, sq("cv_w_out"), NT)
            bigg["cv_w_out"] = _mm_plain("cv_out_dw", sv["y2"], dy, TN, GRAD_WIRE).reshape(1, N_DEV, d // N_DEV, d)
            dy1, dlng, dlnb, ddwb = _cv_norm_bwd(sv["y1"], dy2c, cv_lng_f, cv_lnb_f)
            dpre, ddw, dbin = _cv_conv_bwd(sv["pre"], dy1, cv_dw_f, seq)
            small.update(cv_b_out=dysum, cv_ln_g=dlng, cv_ln_b=dlnb, cv_dw_b=ddwb, cv_dw=ddw[:CONV_WIDTH], cv_b_in=dbin)
            bigg["cv_w_in"] = _grad_cols("cv_in_dw", hh, dpre, cv_w_in.shape[2])
            dh = _back_cols("cv_in_bwd", dpre, gathered["cv_w_in"], cv_w_in.shape[2])
        dx, dsc1, dsh1 = _modulate_bwd(dh, dxr, sv["x"], mods[l][1], seq)
        dmods[l] = jnp.concatenate([dsh1, dsc1, dgate1, dsh2, dsc2, dgate2], axis=2)[:, 0, :]
    grad_x = dx.reshape(nb, seq, d)
    bigg["ffn_w_in"] = jnp.stack(ffn_gi)
    bigg["ffn_w_out"] = jnp.stack(ffn_go)
    for n in d_ln:
        small[n] = jnp.concatenate(d_ln[n], axis=0)

    dmod_rows = jnp.stack(dmods).reshape(nl * nb, 6 * d)
    dmod_g = _exchange_small(_pad_rows(dmod_rows, 8 * ((nl * nb + 7) // 8)), "gather_dmod", False)[:, :nl * nb]
    dmod_all = jnp.transpose(dmod_g.reshape(N_DEV, nl, nb, 6 * d), (1, 0, 2, 3)).reshape(nl, N_DEV * nb, 6 * d)
    dmod_loc = lax.dynamic_slice_in_dim(dmod_all, me * n_mod, n_mod, axis=2)
    g_mod_w, g_mod_b = _mod_bwd(c_all, dmod_loc, dmod_all)
    grads = dict(mod_w=g_mod_w, mod_b=g_mod_b[:, 0, :])

    rep = ["ln1_g", "ln1_b", "ln2_g", "ln2_b", "gm_b_in", "gm_ln_g", "gm_ln_b", "gm_w_s", "gm_b_s", "fox_b_f"]
    cvs = ["cv_b_in", "cv_dw", "cv_dw_b", "cv_ln_g", "cv_ln_b", "cv_b_out"]

    def rows_of(a):
        flat = a.reshape(-1)
        pad = (-flat.shape[0]) % d
        return jnp.pad(flat, (0, pad)).reshape(-1, d)

    pack_rows = [rows_of(small[n]) for n in rep + cvs]
    counts = [r.shape[0] for r in pack_rows]
    total = sum(counts)
    pack = _pad_rows(jnp.concatenate(pack_rows, axis=0), 8 * ((total + 7) // 8))
    summed = _exchange_small(pack, "allreduce_small", True)
    offs = [sum(counts[:i]) for i in range(len(counts))]
    rep_rows = sum(counts[:len(rep)])
    for n, o_, cnt in zip(rep + cvs, offs, counts):
        full = summed[o_:o_ + cnt].reshape(-1)
        if n in rep:
            grads[n] = full[:weights[n].size].reshape(weights[n].shape)
        else:
            wshape = weights[n].shape
            cols = wshape[-1]
            full = full[:math.prod(wshape[:-1]) * cols * N_DEV].reshape(wshape[:-1] + (cols * N_DEV,))
            grads[n] = lax.dynamic_slice_in_dim(full, me * cols, cols, axis=full.ndim - 1)

    recv = dict(zip(big, _scatter_grads([bigg[n] for n in big])))

    outs = {}

    def view2(a):
        return a.reshape(-1, a.shape[-1])

    for n in big:
        w2 = view2(weights[n])
        res = _adamw("adamw_" + n, w2, view2(mom1[n]), view2(mom2[n]), parts=recv[n].reshape((N_DEV,) + w2.shape))
        outs[n] = [r.reshape(weights[n].shape) for r in res]
    res = _adamw("adamw_mod_w", view2(mod_w), view2(m_mod_w), view2(v_mod_w), g=view2(grads["mod_w"]))
    outs["mod_w"] = [r.reshape(mod_w.shape) for r in res]
    rp = lambda src: _pad_rows(jnp.concatenate([rows_of(src[n]) for n in rep], axis=0), 8 * ((rep_rows + 7) // 8))
    res = _adamw("adamw_replicated", rp(weights), rp(mom1), rp(mom2), g=rp(grads))
    for n, o_, cnt in zip(rep, offs, counts):
        outs[n] = [r[o_:o_ + cnt].reshape(-1)[:weights[n].size].reshape(weights[n].shape) for r in res]
    cv_cols = weights["cv_b_out"].shape[-1]
    cp = lambda src: jnp.concatenate([src[n].reshape(-1, cv_cols) for n in cvs], axis=0)
    cv_cnt = [weights[n].size // cv_cols for n in cvs]
    cv_tot = sum(cv_cnt)
    cpp = lambda src: _pad_rows(cp(src), 8 * ((cv_tot + 7) // 8))
    res = _adamw("adamw_cv_small", cpp(weights), cpp(mom1), cpp(mom2), g=cpp(grads))
    o_ = 0
    for n, cnt in zip(cvs, cv_cnt):
        outs[n] = [r[o_:o_ + cnt].reshape(weights[n].shape) for r in res]
        o_ += cnt
    res = _adamw("adamw_mod_b", mod_b, m_mod_b, v_mod_b, g=grads["mod_b"])
    outs["mod_b"] = list(res)

    return (loss, grad_x, *[outs[n][0] for n in names], *[outs[n][1] for n in names],
            *[outs[n][2] for n in names], *[outs[n][3] for n in names])
```

```python
import functools
import math

import jax
import jax.numpy as jnp
from jax import lax
from jax.experimental import pallas as pl
from jax.experimental.pallas import tpu as pltpu

F32 = jnp.float32
BF16 = jnp.bfloat16
MESH = pl.DeviceIdType.MESH

N_DEV = 8
HEAD_DIM = 64
LANES = 128
SUBLANES = 8
GM_CHUNK = 128
GM_GROUPS = 8
CONV_WIDTH = 31
CONV_HALO = 32
CONV_ROWS = 32
LN_EPS = 1e-5
NEG_INF = -1e30
SB_DEAD = -100.0
GRAD_WIRE = jnp.bfloat16

ADAM_LR = 0.001
ADAM_B1 = 0.9
ADAM_B2 = 0.999
ADAM_EPS = 1e-08
ADAM_WD = 0.01
ADAM_STEP = 10

TM = 512
TMM = 1024
TQ = 256
FOX_FWD_UNROLL = 4
FOX_BWD_UNROLL = 4
SB_BWD_UNROLL = 2
TT = 512
VMEM_BIG = 56 * 1024 * 1024

NN = (((1,), (0,)), ((), ()))
NT = (((1,), (1,)), ((), ()))
TN = (((0,), (0,)), ((), ()))


def _call(body, name, out_shape, grid=None, in_specs=None, out_specs=None, scratch=(), vmem=None):
    params = {}
    if grid is not None:
        params["dimension_semantics"] = ("arbitrary",) * len(grid)
    if vmem is not None:
        params["vmem_limit_bytes"] = vmem
    kw = {}
    if grid is not None:
        kw["grid"] = grid
    if in_specs is not None:
        kw["in_specs"] = in_specs
    if out_specs is not None:
        kw["out_specs"] = out_specs
    return pl.pallas_call(body, name=name, out_shape=out_shape, scratch_shapes=list(scratch),
                          compiler_params=pltpu.CompilerParams(**params), **kw)


def _sds(shape, dtype):
    return jax.ShapeDtypeStruct(tuple(shape), dtype)


def _dot(a, b, dims=NN):
    return lax.dot_general(a.astype(BF16), b.astype(BF16), dims, preferred_element_type=F32)


def _split3(x):
    h1 = x.astype(BF16)
    r1 = x - h1.astype(F32)
    h2 = r1.astype(BF16)
    h3 = (r1 - h2.astype(F32)).astype(BF16)
    return h1, h2, h3


def _dot_exact(x, m, dims=NN):
    h1, h2, h3 = _split3(x)
    d = lambda h: lax.dot_general(h, m, dims, preferred_element_type=F32)
    return (d(h1) + d(h2)) + d(h3)


def _dot_exact_rhs(m, x, dims=NN):
    h1, h2, h3 = _split3(x)
    d = lambda h: lax.dot_general(m, h, dims, preferred_element_type=F32)
    return (d(h1) + d(h2)) + d(h3)


def _silu(x):
    return x * jax.nn.sigmoid(x)


def _gelu(x):
    return 0.5 * x * (1.0 + lax.erf(x * (2.0 ** -0.5)))


def _gelu_grad(x):
    return 0.5 * (1.0 + lax.erf(x * (2.0 ** -0.5))) + x * jnp.exp(-0.5 * x * x) * ((2.0 * math.pi) ** -0.5)


def _log_sigmoid(z):
    return jnp.minimum(z, 0.0) - jnp.log(1.0 + jnp.exp(-jnp.abs(z)))


def _ln_stats(r):
    mu = jnp.mean(r, axis=-1, keepdims=True)
    rc = r - mu
    var = jnp.mean(rc * rc, axis=-1, keepdims=True)
    return rc, lax.rsqrt(var + LN_EPS)


def _colsum(x):
    return jnp.sum(x, axis=0, keepdims=True)


def _peers():
    mx, my, mc = lax.axis_index("x"), lax.axis_index("y"), lax.axis_index("c")
    me = 4 * mx + 2 * my + mc
    out = []
    for k in range(1, N_DEV):
        px = 1 - mx if (k >> 2) & 1 else mx
        py = 1 - my if (k >> 1) & 1 else my
        pc = 1 - mc if k & 1 else mc
        out.append(((px, py, pc), 4 * px + 2 * py + pc))
    return me, out


def _exchange_small(x, name, reduce):
    rows, cols = x.shape

    def body(x_ref, o_ref, *rest):
        if reduce:
            land, send_sems, recv_sems, local_sem = rest
        else:
            land = o_ref
            send_sems, recv_sems, local_sem = rest
        me, peers = _peers()
        mine = pltpu.make_async_copy(x_ref, land.at[me], local_sem)
        mine.start()
        sends = []
        for k, (peer, _) in enumerate(peers):
            cp = pltpu.make_async_remote_copy(src_ref=x_ref, dst_ref=land.at[me], send_sem=send_sems.at[k],
                                              recv_sem=recv_sems.at[k], device_id=peer, device_id_type=MESH)
            cp.start()
            sends.append(cp)
        for k, (peer, blk) in enumerate(peers):
            pltpu.make_async_remote_copy(src_ref=x_ref, dst_ref=land.at[blk], send_sem=send_sems.at[k],
                                         recv_sem=recv_sems.at[k], device_id=peer, device_id_type=MESH).wait_recv()
        for cp in sends:
            cp.wait_send()
        mine.wait()
        if reduce:
            acc = land[0]
            for s in range(1, N_DEV):
                acc = acc + land[s]
            o_ref[...] = acc

    vm = pl.BlockSpec(memory_space=pltpu.VMEM)
    scratch = [pltpu.SemaphoreType.DMA((N_DEV - 1,)), pltpu.SemaphoreType.DMA((N_DEV - 1,)), pltpu.SemaphoreType.DMA]
    if reduce:
        scratch = [pltpu.VMEM((N_DEV, rows, cols), F32)] + scratch
        out = _sds((rows, cols), F32)
    else:
        out = _sds((N_DEV, rows, cols), F32)
    return _call(body, name, out, in_specs=[vm], out_specs=vm, scratch=scratch, vmem=VMEM_BIG)(x)


def _comm_scratch(n):
    return [pltpu.SemaphoreType.DMA((n, N_DEV - 1)), pltpu.SemaphoreType.DMA((n, N_DEV - 1)), pltpu.SemaphoreType.DMA((n,))]


def _gather_stage(stage, ins, outs, send_sems, recv_sems, local_sems):
    n = len(ins)
    mx, my, mc = lax.axis_index("x"), lax.axis_index("y"), lax.axis_index("c")
    here, sibling = (mx, my, mc), (mx, my, 1 - mc)
    chips = [(1 - mx, my), (mx, 1 - my), (1 - mx, 1 - my)]

    def block(px, py, pc):
        return 4 * px + 2 * py + pc

    def copy(a, k, blk, to, src=None):
        dst = outs[a].at[:, blk]
        return pltpu.make_async_remote_copy(src_ref=dst if src is None else src, dst_ref=dst, send_sem=send_sems.at[a, k],
                                            recv_sem=recv_sems.at[a, k], device_id=to, device_id_type=MESH)

    me = block(*here)
    for a in range(n):
        local = pltpu.make_async_copy(ins[a], outs[a].at[:, me], local_sems.at[a])
        first = [copy(a, 0, me, sibling, src=ins[a])] + [copy(a, 1 + j, me, (*chip, mc), src=ins[a]) for j, chip in enumerate(chips)]
        if stage == 0:
            local.start()
            for cp in first:
                cp.start()
        if stage == 1:
            for j, chip in enumerate(chips):
                copy(a, 1 + j, block(*chip, mc), here).wait_recv()
                copy(a, 4 + j, block(*chip, mc), sibling).start()
        if stage == 2:
            copy(a, 0, block(mx, my, 1 - mc), here).wait_recv()
            for j, chip in enumerate(chips):
                copy(a, 4 + j, block(*chip, 1 - mc), here).wait_recv()
            for cp in first:
                cp.wait_send()
            for j, chip in enumerate(chips):
                copy(a, 4 + j, block(*chip, mc), sibling).wait_send()
            local.wait()


def _gather_weights(shards):
    n = len(shards)

    def body(*refs):
        for stage in range(3):
            _gather_stage(stage, refs[:n], refs[n:2 * n], *refs[2 * n:])

    hbm = pl.BlockSpec(memory_space=pl.ANY)
    return _call(body, "gather_weights", _gathered_shapes(shards), in_specs=[hbm] * n, out_specs=[hbm] * n, scratch=_comm_scratch(n))(*shards)


def _gathered_shapes(shards):
    return [_sds((s.shape[0], N_DEV) + s.shape[1:], s.dtype) for s in shards]


def _scatter_grads(grads):
    n = len(grads)

    def body(*refs):
        for stage in range(2):
            _scatter_stage(stage, refs[:n], refs[n:2 * n], *refs[2 * n:])

    hbm = pl.BlockSpec(memory_space=pl.ANY)
    return _call(body, "scatter_grads", _scattered_shapes(grads), in_specs=[hbm] * n, out_specs=[hbm] * n, scratch=_comm_scratch(n))(*grads)


def _scattered_shapes(grads):
    return [_sds((N_DEV, g.shape[0]) + g.shape[2:], g.dtype) for g in grads]


def _scatter_stage(stage, ins, outs, send_sems, recv_sems, local_sems):
    me, peers = _peers()
    for a in range(len(ins)):
        local = pltpu.make_async_copy(ins[a].at[:, me], outs[a].at[me], local_sems.at[a])
        sends = [pltpu.make_async_remote_copy(src_ref=ins[a].at[:, blk], dst_ref=outs[a].at[me], send_sem=send_sems.at[a, k],
                                              recv_sem=recv_sems.at[a, k], device_id=peer, device_id_type=MESH)
                 for k, (peer, blk) in enumerate(peers)]
        if stage == 0:
            local.start()
            for cp in sends:
                cp.start()
        if stage == 1:
            for k, (peer, blk) in enumerate(peers):
                pltpu.make_async_remote_copy(src_ref=ins[a].at[:, me], dst_ref=outs[a].at[blk], send_sem=send_sems.at[a, k],
                                             recv_sem=recv_sems.at[a, k], device_id=peer, device_id_type=MESH).wait_recv()
            for cp in sends:
                cp.wait_send()
            local.wait()


def _mm(name, a, b, out, grid, a_spec, b_spec, o_spec, dims, extra=(), extra_specs=(), epilogue=None, vmem=VMEM_BIG):
    nred = grid[-1]
    red_axis = len(grid) - 1
    acc_shape = tuple(d for d in o_spec.block_shape if d is not None)
    n_extra = len(extra)

    def body(a_ref, b_ref, *rest):
        ex = rest[:n_extra]
        o_ref = rest[n_extra]

        def finish(acc):
            if epilogue is not None:
                acc = epilogue(acc, *[e[...] for e in ex])
            o_ref[...] = acc.astype(o_ref.dtype)

        prod = _dot(a_ref[...], b_ref[...], dims)
        if nred == 1:
            finish(prod)
        else:
            acc_ref = rest[n_extra + 1]
            r = pl.program_id(red_axis)

            @pl.when(r == 0)
            def _():
                acc_ref[...] = prod

            @pl.when(r > 0)
            def _():
                acc_ref[...] += prod

            @pl.when(r == nred - 1)
            def _():
                finish(acc_ref[...])

    scratch = [pltpu.VMEM(acc_shape, F32)] if nred > 1 else []
    return _call(body, name, out, grid=grid, in_specs=[a_spec, b_spec, *extra_specs], out_specs=o_spec,
                 scratch=scratch, vmem=vmem)(a, b, *extra)


def _add(acc, x):
    return acc + x


def _proj_cols(name, h, w, n_slot, bias=None, out_dtype=F32):
    t, k = h.shape
    s = w.shape[1]
    tm = min(TMM, t)
    extra, especs, epi = (), (), None
    if bias is not None:
        extra, especs, epi = (bias,), (pl.BlockSpec((1, n_slot), lambda i, j, r: (0, j)),), _add
    return _mm(name, h, w, _sds((t, s * n_slot), out_dtype), (t // tm, s, 1),
               pl.BlockSpec((tm, k), lambda i, j, r: (i, 0)),
               pl.BlockSpec((None, None, k, n_slot), lambda i, j, r: (0, j, 0, 0)),
               pl.BlockSpec((tm, n_slot), lambda i, j, r: (i, j)), NN, extra, especs, epi)


def _accumulate_over_tokens(name, ins, in_specs, out, o_spec, acc_shape, n_steps, terms, store=None):
    def body(*refs):
        o_ref, acc_ref = refs[len(ins)], refs[len(ins) + 1]
        r = pl.program_id(0)

        @pl.when(r == 0)
        def _():
            acc_ref[...] = jnp.zeros_like(acc_ref)

        for s, prod in enumerate(terms(*refs[:len(ins)])):
            acc_ref[s] += prod

        @pl.when(r == n_steps - 1)
        def _():
            if store is None:
                o_ref[...] = acc_ref[...].reshape(o_ref.shape).astype(o_ref.dtype)
            else:
                store(o_ref, acc_ref)

    return _call(body, name, out, grid=(n_steps,), in_specs=in_specs, out_specs=o_spec,
                 scratch=[pltpu.VMEM(acc_shape, F32)], vmem=VMEM_BIG)(*ins)


def _grad_cols(name, h, g, n_slot):
    t, k = h.shape
    s = g.shape[1] // n_slot
    tk = min(TMM, t)

    def terms(h_ref, g_ref):
        hv = h_ref[...]
        return [_dot(hv, g_ref[:, j * n_slot:(j + 1) * n_slot], TN) for j in range(s)]

    return _accumulate_over_tokens(name, (h, g), [pl.BlockSpec((tk, k), lambda r: (r, 0)), pl.BlockSpec((tk, s * n_slot), lambda r: (r, 0))],
                                   _sds((1, s, k, n_slot), GRAD_WIRE), pl.BlockSpec((1, s, k, n_slot), lambda r: (0, 0, 0, 0)),
                                   (s, k, n_slot), t // tk, terms)


def _back_cols(name, g, w, n_slot):
    t = g.shape[0]
    s, k = w.shape[1], w.shape[2]
    tm = min(TM, t)

    def body(g_ref, w_ref, o_ref):
        acc = None
        for j in range(s):
            term = _dot(g_ref[:, j * n_slot:(j + 1) * n_slot], w_ref[j], NT)
            acc = term if acc is None else acc + term
        o_ref[...] = acc

    return _call(body, name, _sds((t, k), F32), grid=(t // tm,),
                 in_specs=[pl.BlockSpec((tm, s * n_slot), lambda i: (i, 0)), pl.BlockSpec((None, s, k, n_slot), lambda i: (0, 0, 0, 0))],
                 out_specs=pl.BlockSpec((tm, k), lambda i: (i, 0)), vmem=VMEM_BIG)(g, w)


def _mm_plain(name, a, b, dims, out_dtype=F32, bias=None):
    if dims == TN:
        t, k = a.shape
        n = b.shape[1]
        tk = min(TMM, t)
        return _mm(name, a, b, _sds((k, n), out_dtype), (1, t // tk),
                   pl.BlockSpec((tk, k), lambda j, r: (r, 0)), pl.BlockSpec((tk, n), lambda j, r: (r, 0)),
                   pl.BlockSpec((k, n), lambda j, r: (0, 0)), TN)
    t = a.shape[0]
    tm = min(TMM, t)
    n = b.shape[1] if dims == NN else b.shape[0]
    extra, especs, epi = (), (), None
    if bias is not None:
        extra, especs, epi = (bias,), (pl.BlockSpec((1, n), lambda i, r: (0, 0)),), _add
    return _mm(name, a, b, _sds((t, n), out_dtype), (t // tm, 1),
               pl.BlockSpec((tm, a.shape[1]), lambda i, r: (i, 0)), pl.BlockSpec(b.shape, lambda i, r: (0, 0)),
               pl.BlockSpec((tm, n), lambda i, r: (i, 0)), dims, extra, especs, epi)


def _mod_fwd(c_all, mod_w, mod_b_loc):
    nl, d, n = mod_w.shape
    nb = c_all.shape[0]

    def body(c_ref, w_ref, b_ref, o_ref):
        o_ref[...] = _dot(_silu(c_ref[...]), w_ref[...]) + b_ref[...]

    return _call(body, "mod_fwd", _sds((nl, nb, n), F32), grid=(nl,),
                 in_specs=[pl.BlockSpec((nb, d), lambda l: (0, 0)), pl.BlockSpec((None, d, n), lambda l: (l, 0, 0)),
                           pl.BlockSpec((None, 1, n), lambda l: (l, 0, 0))],
                 out_specs=pl.BlockSpec((None, nb, n), lambda l: (l, 0, 0)))(c_all, mod_w, mod_b_loc)


def _mod_bwd(c_all, dmod_loc, dmod_all):
    nl, nb, n = dmod_loc.shape
    d = c_all.shape[1]
    n_all = dmod_all.shape[2]

    def body(c_ref, dl_ref, da_ref, gw_ref, gb_ref):
        gw_ref[...] = _dot(_silu(c_ref[...]), dl_ref[...], TN)
        gb_ref[...] = _colsum(da_ref[...])

    return _call(body, "mod_bwd", (_sds((nl, d, n), F32), _sds((nl, 1, n_all), F32)), grid=(nl,),
                 in_specs=[pl.BlockSpec((nb, d), lambda l: (0, 0)), pl.BlockSpec((None, nb, n), lambda l: (l, 0, 0)),
                           pl.BlockSpec((None, nb, n_all), lambda l: (l, 0, 0))],
                 out_specs=(pl.BlockSpec((None, d, n), lambda l: (l, 0, 0)), pl.BlockSpec((None, 1, n_all), lambda l: (l, 0, 0))),
                 )(c_all, dmod_loc, dmod_all)


def _row_spec(d, tpb):
    return pl.BlockSpec((None, 1, d), lambda i: (i // tpb, 0, 0))


def _tile_spec(tm, d):
    return pl.BlockSpec((tm, d), lambda i: (i, 0))


def _vec_spec(d):
    return pl.BlockSpec((1, d), lambda i: (0, 0))


def _modulate(x, sc, sh, seq):
    t, d = x.shape
    tm = min(TM, seq)
    tpb = seq // tm

    def body(x_ref, sc_ref, sh_ref, h_ref):
        h_ref[...] = (x_ref[...] * (1.0 + sc_ref[...]) + sh_ref[...]).astype(BF16)

    return _call(body, "modulate", _sds((t, d), BF16), grid=(t // tm,),
                 in_specs=[_tile_spec(tm, d), _row_spec(d, tpb), _row_spec(d, tpb)], out_specs=_tile_spec(tm, d))(x, sc, sh)


def _lnres_fwd(x, y, gate, lg, lb, alpha, seq, nxt=None):
    t, d = x.shape
    tm = min(TM, seq)
    tpb = seq // tm

    def body(x_ref, y_ref, g_ref, lg_ref, lb_ref, *rest):
        r = alpha * x_ref[...] + (1.0 + g_ref[...]) * y_ref[...]
        rc, rstd = _ln_stats(r)
        xn = rc * rstd * lg_ref[...] + lb_ref[...]
        if nxt is None:
            rest[0][...] = xn
        else:
            sc_ref, sh_ref, xo_ref, h_ref = rest
            xo_ref[...] = xn
            h_ref[...] = (xn * (1.0 + sc_ref[...]) + sh_ref[...]).astype(BF16)

    ins = [_tile_spec(tm, d), _tile_spec(tm, d), _row_spec(d, tpb), _vec_spec(d), _vec_spec(d)]
    if nxt is None:
        return _call(body, "lnres_fwd_last", _sds((t, d), F32), grid=(t // tm,), in_specs=ins,
                     out_specs=_tile_spec(tm, d))(x, y, gate, lg, lb)
    return _call(body, "lnres_fwd", (_sds((t, d), F32), _sds((t, d), BF16)), grid=(t // tm,),
                 in_specs=ins + [_row_spec(d, tpb), _row_spec(d, tpb)],
                 out_specs=(_tile_spec(tm, d), _tile_spec(tm, d)))(x, y, gate, lg, lb, *nxt)


def _loss_head(x, tgt):
    t, d = x.shape
    tm = min(TM, t)

    def body(x_ref, t_ref, dx_ref, sq_ref):
        e = x_ref[...] - t_ref[...]
        dx_ref[...] = e * (1.0 / d)

        @pl.when(pl.program_id(0) == 0)
        def _():
            sq_ref[...] = jnp.zeros_like(sq_ref)

        sq_ref[...] += _colsum(e * e)

    return _call(body, "loss_head", (_sds((t, d), F32), _sds((1, d), F32)), grid=(t // tm,),
                 in_specs=[_tile_spec(tm, d), _tile_spec(tm, d)], out_specs=(_tile_spec(tm, d), _vec_spec(d)))(x, tgt)


def _lnres_bwd(dxo, x, y, gate, lg, alpha, seq):
    t, d = x.shape
    tm = min(TM, seq)
    tpb = seq // tm
    nb = t // seq

    def body(dxo_ref, x_ref, y_ref, g_ref, lg_ref, dxr_ref, dy_ref, dlg_ref, dlb_ref, dys_ref, dg_ref):
        i = pl.program_id(0)
        yv = y_ref[...]
        r = alpha * x_ref[...] + (1.0 + g_ref[...]) * yv
        rc, rstd = _ln_stats(r)
        xhat = rc * rstd
        dxo_v = dxo_ref[...]
        dxh = dxo_v * lg_ref[...]
        m1 = jnp.mean(dxh, axis=-1, keepdims=True)
        m2 = jnp.mean(dxh * xhat, axis=-1, keepdims=True)
        dr = rstd * (dxh - m1 - xhat * m2)
        dyv = (1.0 + g_ref[...]) * dr
        dxr_ref[...] = alpha * dr
        dy_ref[...] = dyv.astype(BF16)

        @pl.when(i == 0)
        def _():
            dlg_ref[...] = jnp.zeros_like(dlg_ref)
            dlb_ref[...] = jnp.zeros_like(dlb_ref)
            dys_ref[...] = jnp.zeros_like(dys_ref)

        @pl.when(i % tpb == 0)
        def _():
            dg_ref[...] = jnp.zeros_like(dg_ref)

        dlg_ref[...] += _colsum(dxo_v * xhat)
        dlb_ref[...] += _colsum(dxo_v)
        dys_ref[...] += _colsum(dyv)
        dg_ref[...] += _colsum(dr * yv)

    return _call(body, "lnres_bwd",
                 (_sds((t, d), F32), _sds((t, d), BF16), _sds((1, d), F32), _sds((1, d), F32), _sds((1, d), F32), _sds((nb, 1, d), F32)),
                 grid=(t // tm,),
                 in_specs=[_tile_spec(tm, d), _tile_spec(tm, d), _tile_spec(tm, d), _row_spec(d, tpb), _vec_spec(d)],
                 out_specs=(_tile_spec(tm, d), _tile_spec(tm, d), _vec_spec(d), _vec_spec(d), _vec_spec(d), _row_spec(d, tpb)),
                 )(dxo, x, y, gate, lg)


def _modulate_bwd(dh, dxr, x, sc, seq):
    t, d = x.shape
    tm = min(TM, seq)
    tpb = seq // tm
    nb = t // seq

    def body(dh_ref, dxr_ref, x_ref, sc_ref, dx_ref, dsc_ref, dsh_ref):
        dhv = dh_ref[...]
        dx_ref[...] = dxr_ref[...] + dhv * (1.0 + sc_ref[...])

        @pl.when(pl.program_id(0) % tpb == 0)
        def _():
            dsc_ref[...] = jnp.zeros_like(dsc_ref)
            dsh_ref[...] = jnp.zeros_like(dsh_ref)

        dsc_ref[...] += _colsum(dhv * x_ref[...])
        dsh_ref[...] += _colsum(dhv)

    return _call(body, "modulate_bwd", (_sds((t, d), F32), _sds((nb, 1, d), F32), _sds((nb, 1, d), F32)), grid=(t // tm,),
                 in_specs=[_tile_spec(tm, d), _tile_spec(tm, d), _tile_spec(tm, d), _row_spec(d, tpb)],
                 out_specs=(_tile_spec(tm, d), _row_spec(d, tpb), _row_spec(d, tpb)))(dh, dxr, x, sc)


def _ffn_in(h, w_in, layer):
    t, d = h.shape
    n = w_in.shape[3]
    half = N_DEV // 2
    tm = min(TMM, t)

    def body(h_ref, wg_ref, wu_ref, g_ref, u_ref, a_ref):
        hv = h_ref[...]
        g = _dot(hv, wg_ref[...])
        u = _dot(hv, wu_ref[...])
        g_ref[...] = g
        u_ref[...] = u
        a_ref[...] = (_silu(g) * u).astype(BF16)

    blk = pl.BlockSpec((None, tm, n), lambda p, i: (p, i, 0))
    return _call(body, "ffn_in", (_sds((half, t, n), F32), _sds((half, t, n), F32), _sds((half, t, n), BF16)),
                 grid=(half, t // tm),
                 in_specs=[pl.BlockSpec((tm, d), lambda p, i: (i, 0)),
                           pl.BlockSpec((None, None, d, n), lambda p, i: (layer, p, 0, 0)),
                           pl.BlockSpec((None, None, d, n), lambda p, i: (layer, p + half, 0, 0))],
                 out_specs=(blk, blk, blk), vmem=VMEM_BIG)(h, w_in, w_in)


def _ffn_out(act, w_out, layer):
    half, t, n = act.shape
    d = w_out.shape[2]
    tm = min(TM, t)

    def body(a_ref, w_ref, o_ref):
        acc = _dot(a_ref[0], w_ref[0])
        for p in range(1, half):
            acc = acc + _dot(a_ref[p], w_ref[p])
        o_ref[...] = acc

    return _call(body, "ffn_out", _sds((t, d), F32), grid=(t // tm,),
                 in_specs=[pl.BlockSpec((half, tm, n), lambda i: (0, i, 0)),
                           pl.BlockSpec((half, n, d), lambda i: (layer // half, 0, 0))],
                 out_specs=pl.BlockSpec((tm, d), lambda i: (i, 0)), vmem=VMEM_BIG)(act, w_out)


def _ffn_dact(dy, w_out, hg, hu, layer):
    half, t, n = hg.shape
    d = dy.shape[1]
    tm = min(TMM, t)

    def body(dy_ref, w_ref, g_ref, u_ref, dg_ref, du_ref):
        da = _dot(dy_ref[...], w_ref[...], NT)
        g = g_ref[...]
        sg = jax.nn.sigmoid(g)
        dg_ref[...] = (da * u_ref[...] * (sg * (1.0 + g * (1.0 - sg)))).astype(BF16)
        du_ref[...] = (da * (g * sg)).astype(BF16)

    blk = pl.BlockSpec((None, tm, n), lambda p, i: (p, i, 0))
    dg, du = _call(body, "ffn_dact", (_sds((half, t, n), BF16), _sds((half, t, n), BF16)), grid=(half, t // tm),
                   in_specs=[pl.BlockSpec((tm, d), lambda p, i: (i, 0)),
                             pl.BlockSpec((None, n, d), lambda p, i: (layer + p, 0, 0)), blk, blk],
                   out_specs=(blk, blk), vmem=VMEM_BIG)(dy, w_out, hg, hu)
    return dg, du


def _ffn_bwd_weights(h, dy, act, dg, du):
    half, t, n = act.shape
    d = h.shape[1]
    tk = min(TMM, t)
    h_spec = pl.BlockSpec((tk, d), lambda r: (r, 0))
    g_spec = pl.BlockSpec((half, tk, n), lambda r: (0, r, 0))

    def in_terms(h_ref, g_ref):
        hv = h_ref[...]
        return [_dot(hv, g_ref[p], TN) for p in range(half)]

    def out_terms(a_ref, dy_ref):
        dyv = dy_ref[...]
        return [_dot(a_ref[p], dyv, TN) for p in range(half)]

    w_in_spec = pl.BlockSpec((half, d, n), lambda r: (0, 0, 0))
    dwg = _accumulate_over_tokens("ffn_dw_gate", (h, dg), [h_spec, g_spec], _sds((half, d, n), GRAD_WIRE), w_in_spec,
                                  (half, d, n), t // tk, in_terms)
    dwu = _accumulate_over_tokens("ffn_dw_up", (h, du), [h_spec, g_spec], _sds((half, d, n), GRAD_WIRE), w_in_spec,
                                  (half, d, n), t // tk, in_terms)
    dwo = _accumulate_over_tokens("ffn_dw_out", (act, dy), [g_spec, h_spec], _sds((half, n, d), GRAD_WIRE),
                                  pl.BlockSpec((half, n, d), lambda r: (0, 0, 0)), (half, n, d), t // tk, out_terms)
    return dwg, dwu, dwo


def _ffn_dh(dg, du, w_in, layer):
    half, t, n = dg.shape
    d = w_in.shape[2]
    tm = min(TM, t)

    def body(dg_ref, du_ref, w_ref, o_ref):
        acc = None
        for p in range(half):
            for ref, q in ((dg_ref, p), (du_ref, p + half)):
                term = _dot(ref[p], w_ref[q], NT)
                acc = term if acc is None else acc + term
        o_ref[...] = acc

    g_spec = pl.BlockSpec((half, tm, n), lambda i: (0, i, 0))
    return _call(body, "ffn_dh", _sds((t, d), F32), grid=(t // tm,),
                 in_specs=[g_spec, g_spec, pl.BlockSpec((None, 2 * half, d, n), lambda i: (layer, 0, 0, 0))],
                 out_specs=pl.BlockSpec((tm, d), lambda i: (i, 0)), vmem=VMEM_BIG)(dg, du, w_in)


def _tril(n, strict=False):
    r = lax.broadcasted_iota(jnp.int32, (n, n), 0)
    c = lax.broadcasted_iota(jnp.int32, (n, n), 1)
    return c < r if strict else c <= r


def _gm_spatial_fwd(pre, lng, lnb, w_s, b_st, seq):
    t, w2 = pre.shape
    w = w2 // 2
    gd = w // GM_GROUPS
    tm = min(TM, seq)
    nch = tm // GM_CHUNK

    def body(pre_ref, lng_ref, lnb_ref, ws_ref, bs_ref, y_ref):
        v = _gelu(pre_ref[:, w:])
        vc, rstd = _ln_stats(v)
        vn = (vc * rstd * lng_ref[...] + lnb_ref[...]).astype(BF16)
        keep = _tril(GM_CHUNK)
        for g in range(GM_GROUPS):
            wm = jnp.where(keep, ws_ref[g], 0.0).astype(BF16)
            for ci in range(nch):
                rows = slice(ci * GM_CHUNK, (ci + 1) * GM_CHUNK)
                cols = slice(g * gd, (g + 1) * gd)
                sv = _dot(wm, vn[rows, cols]) + bs_ref[:, g:g + 1]
                u = _gelu(pre_ref[rows, cols])
                y_ref[rows, cols] = (u * sv).astype(BF16)

    return _call(body, "gm_spatial_fwd", _sds((t, w), BF16), grid=(t // tm,),
                 in_specs=[_tile_spec(tm, w2), _vec_spec(w), _vec_spec(w),
                           pl.BlockSpec((GM_GROUPS, GM_CHUNK, GM_CHUNK), lambda i: (0, 0, 0)),
                           pl.BlockSpec((GM_CHUNK, GM_GROUPS), lambda i: (0, 0))],
                 out_specs=_tile_spec(tm, w), vmem=VMEM_BIG)(pre, lng, lnb, w_s, b_st)


def _gm_spatial_bwd(pre, dyv, lng, lnb, w_s, b_st, seq):
    t, w2 = pre.shape
    w = w2 // 2
    gd = w // GM_GROUPS
    tm = min(TM, seq)
    nch = tm // GM_CHUNK

    def body(pre_ref, dyv_ref, lng_ref, lnb_ref, ws_ref, bs_ref, dpre_ref, dws_ref, dbs_ref, dlg_ref, dlb_ref, dbin_ref, dvn_ref):
        @pl.when(pl.program_id(0) == 0)
        def _():
            dws_ref[...] = jnp.zeros_like(dws_ref)
            dbs_ref[...] = jnp.zeros_like(dbs_ref)
            dlg_ref[...] = jnp.zeros_like(dlg_ref)
            dlb_ref[...] = jnp.zeros_like(dlb_ref)
            dbin_ref[...] = jnp.zeros_like(dbin_ref)

        pv = pre_ref[:, w:]
        v = _gelu(pv)
        vc, rstd = _ln_stats(v)
        vhat = vc * rstd
        vn = (vhat * lng_ref[...] + lnb_ref[...]).astype(BF16)
        keep = _tril(GM_CHUNK)
        dbs_cols = []
        for g in range(GM_GROUPS):
            wm = jnp.where(keep, ws_ref[g], 0.0).astype(BF16)
            dwm = jnp.zeros((GM_CHUNK, GM_CHUNK), F32)
            dbs = jnp.zeros((GM_CHUNK, 1), F32)
            for ci in range(nch):
                rows = slice(ci * GM_CHUNK, (ci + 1) * GM_CHUNK)
                cols = slice(g * gd, (g + 1) * gd)
                vn_b = vn[rows, cols]
                sv = _dot(wm, vn_b) + bs_ref[:, g:g + 1]
                pu = pre_ref[rows, cols]
                dy = dyv_ref[rows, cols]
                du = dy * sv
                dsv = dy * _gelu(pu)
                dpu = du * _gelu_grad(pu)
                dpre_ref[rows, cols] = dpu.astype(BF16)
                dbin_ref[:, cols] += _colsum(dpu)
                dsv_b = dsv.astype(BF16)
                dwm = dwm + _dot(dsv_b, vn_b, NT)
                dbs = dbs + jnp.sum(dsv, axis=-1, keepdims=True)
                dvn_ref[rows, cols] = _dot(wm, dsv_b, TN)
            dws_ref[g] += jnp.where(keep, dwm, 0.0)
            dbs_cols.append(dbs)
        dbs_ref[...] += jnp.concatenate(dbs_cols, axis=1)
        dvn = dvn_ref[...]
        dlg_ref[...] += _colsum(dvn * vhat)
        dlb_ref[...] += _colsum(dvn)
        dvh = dvn * lng_ref[...]
        m1 = jnp.mean(dvh, axis=-1, keepdims=True)
        m2 = jnp.mean(dvh * vhat, axis=-1, keepdims=True)
        dv = rstd * (dvh - m1 - vhat * m2)
        dpv = dv * _gelu_grad(pv)
        dpre_ref[:, w:] = dpv.astype(BF16)
        dbin_ref[:, w:] += _colsum(dpv)

    full3 = pl.BlockSpec((GM_GROUPS, GM_CHUNK, GM_CHUNK), lambda i: (0, 0, 0))
    bst = pl.BlockSpec((GM_CHUNK, GM_GROUPS), lambda i: (0, 0))
    return _call(body, "gm_spatial_bwd",
                 (_sds((t, w2), BF16), _sds((GM_GROUPS, GM_CHUNK, GM_CHUNK), F32), _sds((GM_CHUNK, GM_GROUPS), F32),
                  _sds((1, w), F32), _sds((1, w), F32), _sds((1, w2), F32)),
                 grid=(t // tm,),
                 in_specs=[_tile_spec(tm, w2), _tile_spec(tm, w), _vec_spec(w), _vec_spec(w), full3, bst],
                 out_specs=(_tile_spec(tm, w2), full3, bst, _vec_spec(w), _vec_spec(w), _vec_spec(w2)),
                 scratch=[pltpu.VMEM((tm, w), F32)], vmem=VMEM_BIG)(pre, dyv, lng, lnb, w_s, b_st)


def _head_masks():
    lane = lax.broadcasted_iota(jnp.int32, (1, LANES), 1)
    return lane < HEAD_DIM


def _two_heads(x, m0):
    z = jnp.zeros_like(x)
    return jnp.where(m0, x, z), jnp.where(m0, z, x)


def _transposed(pair):
    return tuple(x.astype(F32).T.astype(BF16) for x in pair)


def _store_transposed(dqkv_ref, dk_acc, dv_acc, nq, tq):
    for c in range(nq):
        cols = slice(c * tq, (c + 1) * tq)
        dqkv_ref[1, cols, :] = dk_acc[:, cols].T.astype(BF16)
        dqkv_ref[2, cols, :] = dv_acc[:, cols].T.astype(BF16)


def _qkv_specs(seq, nq, blocked_q):
    if blocked_q:
        q = pl.BlockSpec((None, TQ_(seq), LANES), lambda b, p, i: (0, b * nq + i, p))
        k = pl.BlockSpec((None, seq, LANES), lambda b, p, i: (1, b, p))
        v = pl.BlockSpec((None, seq, LANES), lambda b, p, i: (2, b, p))
    else:
        q = pl.BlockSpec((None, seq, LANES), lambda b, p: (0, b, p))
        k = pl.BlockSpec((None, seq, LANES), lambda b, p: (1, b, p))
        v = pl.BlockSpec((None, seq, LANES), lambda b, p: (2, b, p))
    return q, k, v


def TQ_(seq):
    return min(TQ, seq)


def _fox_gate_fwd(ft, b_f, seq):
    nh, t = ft.shape
    nch = seq // LANES

    def body(ft_ref, bf_ref, fr_ref):
        r = lax.broadcasted_iota(jnp.int32, (LANES, LANES), 0)
        c = lax.broadcasted_iota(jnp.int32, (LANES, LANES), 1)
        upper = jnp.where(r <= c, 1.0, 0.0).astype(BF16)
        carry = jnp.zeros((nh, 1), F32)
        for ci in range(nch):
            cols = slice(ci * LANES, (ci + 1) * LANES)
            lf = _log_sigmoid(ft_ref[:, cols] + bf_ref[...])
            cs = _dot_exact(lf, upper) + carry
            fr_ref[:, cols] = cs
            carry = cs[:, LANES - 1:LANES]

    return _call(body, "fox_gate_fwd", _sds((nh, t), F32), grid=(t // seq,),
                 in_specs=[pl.BlockSpec((nh, seq), lambda b: (0, b)), pl.BlockSpec((nh, 1), lambda b: (0, 0))],
                 out_specs=pl.BlockSpec((nh, seq), lambda b: (0, b)))(ft, b_f)


def _fox_gate_bwd(ft, b_f, dfk, dfq, seq):
    nh, t = ft.shape
    nch = seq // LANES

    def body(ft_ref, bf_ref, dfk_ref, dfq_ref, dl_ref, db_ref):
        @pl.when(pl.program_id(0) == 0)
        def _():
            db_ref[...] = jnp.zeros_like(db_ref)

        r = lax.broadcasted_iota(jnp.int32, (LANES, LANES), 0)
        c = lax.broadcasted_iota(jnp.int32, (LANES, LANES), 1)
        lower = jnp.where(r >= c, 1.0, 0.0).astype(BF16)
        carry = jnp.zeros((nh, 1), F32)
        tot = jnp.zeros((nh, 1), F32)
        for ci in reversed(range(nch)):
            cols = slice(ci * LANES, (ci + 1) * LANES)
            rc = _dot_exact(dfk_ref[:, cols] + dfq_ref[:, cols], lower) + carry
            carry = rc[:, 0:1]
            dl = rc * jax.nn.sigmoid(-(ft_ref[:, cols] + bf_ref[...]))
            dl_ref[:, cols] = dl
            tot = tot + jnp.sum(dl, axis=-1, keepdims=True)
        db_ref[...] += tot

    blk = pl.BlockSpec((nh, seq), lambda b: (0, b))
    one = pl.BlockSpec((nh, 1), lambda b: (0, 0))
    return _call(body, "fox_gate_bwd", (_sds((nh, t), F32), _sds((nh, 1), F32)), grid=(t // seq,),
                 in_specs=[blk, one, blk, blk], out_specs=(blk, one))(ft, b_f, dfk, dfq)


def _sweep(step, n_off, unroll, init, start=0):
    def group(_, st):
        base, carry = st[0], st[1:]
        for u in range(unroll):
            carry = step(base + u, carry, False)
        return (base + unroll, *carry)

    def tail(r):
        def run(st):
            base, carry = st[0], st[1:]
            for u in range(r):
                carry = step(base + u, carry, False)
            return step(base + r, carry, True)
        return run

    def pick(idx, fns, st):
        if len(fns) == 1:
            return fns[0](st)
        half = len(fns) // 2
        return lax.cond(idx < half, lambda s: pick(idx, fns[:half], s), lambda s: pick(idx - half, fns[half:], s), st)

    st = lax.fori_loop(0, n_off // unroll, group, (jnp.int32(0) + start, *init))
    return pick(n_off % unroll, [tail(r) for r in range(unroll)], st)


def _carried(n_comm, n_in, n_out, refs):
    own_in, send = refs[:n_in], refs[n_in:n_in + n_comm]
    rest = refs[n_in + n_comm:]
    n_sem = 3 if n_comm else 0
    return own_in, send, rest[:n_out], rest[n_out:n_out + n_comm], rest[n_out + n_comm:n_out + n_comm + n_sem], rest[n_out + n_comm + n_sem:]


def _fox_fwd(qkv, frow, nb, seq, gather=()):
    _, t, d = qkv.shape
    npair = d // LANES
    tq = TQ_(seq)
    nq = seq // tq
    scale = HEAD_DIM ** -0.5
    n_comm = len(gather)
    total = nb * npair * nq

    def body(*refs):
        (q_ref, k_ref, v_ref, fr_ref), send, (o_ref, lse_ref), land, sems, _ = _carried(n_comm, 4, 2, refs)
        at = (pl.program_id(0) * npair + pl.program_id(1)) * nq + pl.program_id(2)
        if n_comm:
            for stage, when in ((0, 0), (1, (3 * total) // 4)):
                @pl.when(at == when)
                def _(stage=stage):
                    _gather_stage(stage, send, land, *sems)

        m0 = _head_masks()
        qm = _two_heads(q_ref[...] * scale, m0)
        row = lax.broadcasted_iota(jnp.int32, (tq, tq), 0)
        col = lax.broadcasted_iota(jnp.int32, (tq, tq), 1)
        one = jnp.ones((tq, LANES), BF16)

        def step(j, carry, diag):
            off = pl.multiple_of(j * tq, tq)
            kb = k_ref[pl.ds(off, tq), :]
            vb = v_ref[pl.ds(off, tq), :]
            vv = (jnp.where(m0, vb, one), jnp.where(m0, one, vb))
            out = []
            for hh in range(2):
                m, acc = carry[2 * hh], carry[2 * hh + 1]
                s = lax.dot_general(qm[hh], kb, NT, preferred_element_type=F32) - fr_ref[hh:hh + 1, pl.ds(off, tq)]
                if diag:
                    s = jnp.where(col <= row, s, NEG_INF)
                mn = jnp.maximum(m, jnp.max(s, axis=-1, keepdims=True))
                p = jnp.exp(s - mn)
                out += [mn, jnp.exp(m - mn) * acc + jnp.dot(p.astype(BF16), vv[hh], preferred_element_type=F32)]
            return tuple(out)

        neg = jnp.full((tq, 1), NEG_INF, F32)
        zacc = jnp.zeros((tq, LANES), F32)
        m_a, acc_a, m_b, acc_b = _sweep(step, pl.program_id(2), FOX_FWD_UNROLL, (neg, zacc, neg, zacc))
        l_a = pltpu.roll(acc_a, HEAD_DIM, 1)
        l_b = pltpu.roll(acc_b, HEAD_DIM, 1)
        o_ref[...] = jnp.where(m0, acc_a / l_a, acc_b / l_b)
        lse_ref[:, 0:1] = m_a + jnp.log(l_a[:, 0:1])
        lse_ref[:, 1:2] = m_b + jnp.log(l_b[:, HEAD_DIM:HEAD_DIM + 1])
        if n_comm:
            @pl.when(at == total - 1)
            def _():
                _gather_stage(2, send, land, *sems)

    q_spec, k_spec, v_spec = _qkv_specs(seq, nq, True)
    col_spec = pl.BlockSpec((None, tq, 2), lambda b, p, i: (p, b * nq + i, 0))
    hbm = pl.BlockSpec(memory_space=pl.ANY)
    return _call(body, "fox_fwd", (_sds((t, d), F32), _sds((npair, t, 2), F32), *_gathered_shapes(gather)), grid=(nb, npair, nq),
                 in_specs=[q_spec, k_spec, v_spec, pl.BlockSpec((None, 2, seq), lambda b, p, i: (p, 0, b))] + [hbm] * n_comm,
                 out_specs=(pl.BlockSpec((tq, LANES), lambda b, p, i: (b * nq + i, p)), col_spec, *[hbm] * n_comm),
                 scratch=_comm_scratch(n_comm) if n_comm else (), vmem=VMEM_BIG)(qkv, qkv, qkv, frow, *gather)


def _scatter_beside(stage, n_comm, nb, npair, send, land, sems):
    if n_comm:
        at = pl.program_id(0) * npair + pl.program_id(1)

        @pl.when(at == (0 if stage == 0 else nb * npair - 1))
        def _():
            _scatter_stage(stage, send, land, *sems)


def _fox_bwd(qkv, frow, o, do, lse, nb, seq, scatter=()):
    _, t, d = qkv.shape
    npair = d // LANES
    tq = TQ_(seq)
    nq = seq // tq
    scale = HEAD_DIM ** -0.5
    n_comm = len(scatter)

    def body(*refs):
        own_in, send, (dqkv_ref, df_ref, dfq_ref), land, sems, (dk_acc, dv_acc) = _carried(n_comm, 7, 3, refs)
        q_ref, k_ref, v_ref, fr_ref, o_ref, do_ref, lse_ref = own_in
        _scatter_beside(0, n_comm, nb, npair, send, land, sems)
        m0 = _head_masks()
        row = lax.broadcasted_iota(jnp.int32, (tq, tq), 0)
        col = lax.broadcasted_iota(jnp.int32, (tq, tq), 1)
        dk_acc[...] = jnp.zeros_like(dk_acc)
        dv_acc[...] = jnp.zeros_like(dv_acc)
        df_ref[...] = jnp.zeros_like(df_ref)

        def q_block(i, _):
            qoff = pl.multiple_of(i * tq, tq)
            qrows = pl.ds(qoff, tq)
            qm = _two_heads(q_ref[qrows, :] * scale, m0)
            dov = do_ref[qrows, :]
            dd = dov * o_ref[qrows, :]
            dm = _two_heads(dov.astype(BF16), m0)
            qmt = _transposed(qm)
            dmt = _transposed(dm)
            delta = (jnp.sum(jnp.where(m0, dd, 0.0), axis=-1, keepdims=True),
                     jnp.sum(jnp.where(m0, 0.0, dd), axis=-1, keepdims=True))
            ls = (lse_ref[qrows, 0:1], lse_ref[qrows, 1:2])

            def step(j, carry, diag):
                off = pl.multiple_of(j * tq, tq)
                krows = pl.ds(off, tq)
                kb = k_ref[krows, :]
                vb = v_ref[krows, :]
                dqs, rowsums = [], []
                dk = jnp.zeros((LANES, tq), F32)
                dv = jnp.zeros((LANES, tq), F32)
                for hh in range(2):
                    s = lax.dot_general(qm[hh], kb, NT, preferred_element_type=F32) - fr_ref[hh:hh + 1, krows]
                    if diag:
                        s = jnp.where(col <= row, s, NEG_INF)
                    p = jnp.exp(s - ls[hh])
                    dp = lax.dot_general(dm[hh], vb, NT, preferred_element_type=F32)
                    ds = p * (dp - delta[hh])
                    df_ref[hh:hh + 1, krows] -= _colsum(ds)
                    rowsums.append(carry[1 + hh] + jnp.sum(ds, axis=-1, keepdims=True))
                    ds_b = ds.astype(BF16)
                    dqs.append(jnp.dot(ds_b, kb, preferred_element_type=F32))
                    dk = dk + jnp.dot(qmt[hh], ds_b, preferred_element_type=F32)
                    dv = dv + jnp.dot(dmt[hh], p.astype(BF16), preferred_element_type=F32)
                dk_acc[:, krows] += dk
                dv_acc[:, krows] += dv
                return (carry[0] + jnp.where(m0, dqs[0], dqs[1]), *rowsums)

            zero = jnp.zeros((tq, 1), F32)
            dq, rs_a, rs_b = _sweep(step, i, FOX_BWD_UNROLL, (jnp.zeros((tq, LANES), F32), zero, zero))
            dqkv_ref[0, qrows, :] = (dq * scale).astype(BF16)
            dfq_ref[qrows, 0:1] = rs_a
            dfq_ref[qrows, 1:2] = rs_b
            return 0

        lax.fori_loop(0, nq, q_block, 0)
        _store_transposed(dqkv_ref, dk_acc, dv_acc, nq, tq)
        _scatter_beside(1, n_comm, nb, npair, send, land, sems)

    q_spec, k_spec, v_spec = _qkv_specs(seq, nq, False)
    col_spec = pl.BlockSpec((None, seq, 2), lambda b, p: (p, b, 0))
    row_spec = pl.BlockSpec((None, 2, seq), lambda b, p: (p, 0, b))
    tile = pl.BlockSpec((seq, LANES), lambda b, p: (b, p))
    hbm = pl.BlockSpec(memory_space=pl.ANY)
    return _call(body, "fox_bwd", (_sds((3, t, d), BF16), _sds((npair, 2, t), F32), _sds((npair, t, 2), F32), *_scattered_shapes(scatter)),
                 grid=(nb, npair),
                 in_specs=[q_spec, k_spec, v_spec, row_spec, tile, tile, col_spec] + [hbm] * n_comm,
                 out_specs=(pl.BlockSpec((3, seq, LANES), lambda b, p: (0, b, p)), row_spec, col_spec, *[hbm] * n_comm),
                 scratch=(_comm_scratch(n_comm) if n_comm else []) + [pltpu.VMEM((LANES, seq), F32), pltpu.VMEM((LANES, seq), F32)],
                 vmem=VMEM_BIG)(qkv, qkv, qkv, frow, o, do, lse, *scatter)


def _split2(x):
    hi = x.astype(BF16)
    return hi, (x - hi.astype(F32)).astype(BF16)


def _sum_right(x, tri):
    hi, lo = _split2(x)
    return jnp.dot(hi, tri, preferred_element_type=F32) + jnp.dot(lo, tri, preferred_element_type=F32)


def _sb_scores(qm_h, kb, mask):
    z = lax.dot_general(qm_h, kb, NT, preferred_element_type=F32)
    lb = _log_sigmoid(z)
    l1m = lb - z
    if mask is not None:
        l1m = jnp.where(mask, l1m, 0.0)
    return lb, l1m


def _sb_fwd(qkv, nb, seq):
    _, t, d = qkv.shape
    npair = d // LANES
    tq = TQ_(seq)
    nq = seq // tq
    scale = HEAD_DIM ** -0.5

    def body(q_ref, k_ref, v_ref, o_ref, lt_ref):
        i = pl.program_id(2)
        m0 = _head_masks()
        qm = _two_heads(q_ref[...] * scale, m0)
        row = lax.broadcasted_iota(jnp.int32, (tq, tq), 0)
        col = lax.broadcasted_iota(jnp.int32, (tq, tq), 1)
        after = jnp.where(row > col, 1.0, 0.0).astype(BF16)

        def step(j, carry, diag):
            off = pl.multiple_of(j * tq, tq)
            kb = k_ref[pl.ds(off, tq), :]
            vb = v_ref[pl.ds(off, tq), :]
            mask = (col < row) if diag else None
            nxt, parts = [], []
            for hh in range(2):
                lb, l1m = _sb_scores(qm[hh], kb, mask)
                rest = _sum_right(l1m, after) + carry[hh]
                a = jnp.exp(lb + rest)
                if diag:
                    a = jnp.where(mask, a, 0.0)
                parts.append(jnp.dot(a.astype(BF16), vb, preferred_element_type=F32))
                nxt.append(carry[hh] + jnp.sum(l1m, axis=-1, keepdims=True))
            return (*nxt, carry[2] + jnp.where(m0, parts[0], parts[1]))

        zero = jnp.zeros((tq, 1), F32)
        init = (zero, zero, jnp.zeros((tq, LANES), F32))
        carry = lax.cond(i > 0, lambda c: step(i - 1, step(i, c, True), False), lambda c: step(i, c, True), init)

        def alive(st):
            return (st[0] < i) & (jnp.max(jnp.maximum(st[1], st[2])) > SB_DEAD)

        def more(st):
            return (st[0] + 1, *step(i - 1 - st[0], st[1:], False))

        done, lt_a, lt_b, acc = lax.while_loop(alive, more, (jnp.minimum(i, 1), *carry))
        o_ref[...] = acc
        lt_ref[:, 0:1] = lt_a
        lt_ref[:, 1:2] = lt_b
        lt_ref[:, 2:3] = jnp.zeros((tq, 1), F32) + done.astype(F32)
        lt_ref[:, 3:4] = zero

    q_spec, k_spec, v_spec = _qkv_specs(seq, nq, True)
    return _call(body, "sb_fwd", (_sds((t, d), F32), _sds((npair, t, 4), F32)), grid=(nb, npair, nq),
                 in_specs=[q_spec, k_spec, v_spec],
                 out_specs=(pl.BlockSpec((tq, LANES), lambda b, p, i: (b * nq + i, p)),
                            pl.BlockSpec((None, tq, 4), lambda b, p, i: (p, b * nq + i, 0))), vmem=VMEM_BIG)(qkv, qkv, qkv)


def _sb_bwd(qkv, do, ltot, nb, seq, scatter=()):
    _, t, d = qkv.shape
    npair = d // LANES
    tq = TQ_(seq)
    nq = seq // tq
    scale = HEAD_DIM ** -0.5
    n_comm = len(scatter)

    def body(*refs):
        (q_ref, k_ref, v_ref, do_ref, lt_ref), send, (dqkv_ref,), land, sems, (dk_acc, dv_acc) = _carried(n_comm, 5, 1, refs)
        _scatter_beside(0, n_comm, nb, npair, send, land, sems)
        m0 = _head_masks()
        row = lax.broadcasted_iota(jnp.int32, (tq, tq), 0)
        col = lax.broadcasted_iota(jnp.int32, (tq, tq), 1)
        upto = jnp.where(row <= col, 1.0, 0.0).astype(BF16)
        left_of = jnp.where(row < col, 1.0, 0.0).astype(BF16)
        dk_acc[...] = jnp.zeros_like(dk_acc)
        dv_acc[...] = jnp.zeros_like(dv_acc)

        def q_block(i, _):
            qoff = pl.multiple_of(i * tq, tq)
            qrows = pl.ds(qoff, tq)
            qm = _two_heads(q_ref[qrows, :] * scale, m0)
            dm = _two_heads(do_ref[qrows, :].astype(BF16), m0)
            qmt = _transposed(qm)
            dmt = _transposed(dm)
            ltot = (lt_ref[qrows, 0:1], lt_ref[qrows, 1:2])

            def step(j, carry, diag):
                off = pl.multiple_of(j * tq, tq)
                krows = pl.ds(off, tq)
                kb = k_ref[krows, :]
                vb = v_ref[krows, :]
                mask = (col < row) if diag else None
                nxt, dqs = [], []
                dk = jnp.zeros((LANES, tq), F32)
                dv = jnp.zeros((LANES, tq), F32)
                for hh in range(2):
                    cl, ce = carry[2 * hh], carry[2 * hh + 1]
                    lb, l1m = _sb_scores(qm[hh], kb, mask)
                    a = jnp.exp(lb + (ltot[hh] - (_sum_right(l1m, upto) + cl)))
                    if diag:
                        a = jnp.where(mask, a, 0.0)
                    e = lax.dot_general(dm[hh], vb, NT, preferred_element_type=F32) * a
                    before = _sum_right(e, left_of) + ce
                    beta = jnp.exp(lb)
                    dz = e * (1.0 - beta) - before * beta
                    if diag:
                        dz = jnp.where(mask, dz, 0.0)
                    dz_b = dz.astype(BF16)
                    dqs.append(jnp.dot(dz_b, kb, preferred_element_type=F32))
                    dk = dk + jnp.dot(qmt[hh], dz_b, preferred_element_type=F32)
                    dv = dv + jnp.dot(dmt[hh], a.astype(BF16), preferred_element_type=F32)
                    nxt += [cl + jnp.sum(l1m, axis=-1, keepdims=True), ce + jnp.sum(e, axis=-1, keepdims=True)]
                dk_acc[:, krows] += dk
                dv_acc[:, krows] += dv
                return (*nxt, carry[4] + jnp.where(m0, dqs[0], dqs[1]))

            zero = jnp.zeros((tq, 1), F32)
            visited = jnp.max(lt_ref[qrows, 2:3]).astype(jnp.int32)
            carry = _sweep(step, visited, SB_BWD_UNROLL, (zero, zero, zero, zero, jnp.zeros((tq, LANES), F32)), start=i - visited)
            dqkv_ref[0, qrows, :] = (carry[4] * scale).astype(BF16)
            return 0

        lax.fori_loop(0, nq, q_block, 0)
        _store_transposed(dqkv_ref, dk_acc, dv_acc, nq, tq)
        _scatter_beside(1, n_comm, nb, npair, send, land, sems)

    q_spec, k_spec, v_spec = _qkv_specs(seq, nq, False)
    tile = pl.BlockSpec((seq, LANES), lambda b, p: (b, p))
    hbm = pl.BlockSpec(memory_space=pl.ANY)
    return _call(body, "sb_bwd", (_sds((3, t, d), BF16), *_scattered_shapes(scatter)), grid=(nb, npair),
                 in_specs=[q_spec, k_spec, v_spec, tile, pl.BlockSpec((None, seq, 4), lambda b, p: (p, b, 0))] + [hbm] * n_comm,
                 out_specs=(pl.BlockSpec((3, seq, LANES), lambda b, p: (0, b, p)), *[hbm] * n_comm),
                 scratch=(_comm_scratch(n_comm) if n_comm else []) + [pltpu.VMEM((LANES, seq), F32), pltpu.VMEM((LANES, seq), F32)],
                 vmem=VMEM_BIG)(qkv, qkv, qkv, do, ltot, *scatter)


def _qkv_proj(name, h, w):
    t, d = h.shape
    tm = min(TMM, t)
    return _mm(name, h, w, _sds((3, t, d), BF16), (3, t // tm, 1),
               pl.BlockSpec((tm, d), lambda s, i, r: (i, 0)), pl.BlockSpec((d, d), lambda s, i, r: (0, s)),
               pl.BlockSpec((None, tm, d), lambda s, i, r: (s, i, 0)), NN)


def _qkv_dw(name, h, dqkv):
    t, d = h.shape
    tk = min(TMM, t)

    def terms(h_ref, g_ref):
        hv = h_ref[...]
        return [_dot(hv, g_ref[s], TN) for s in range(3)]

    def store(o_ref, acc_ref):
        for s in range(3):
            o_ref[:, s * d:(s + 1) * d] = acc_ref[s].astype(o_ref.dtype)

    return _accumulate_over_tokens(name, (h, dqkv), [pl.BlockSpec((tk, d), lambda r: (r, 0)), pl.BlockSpec((3, tk, d), lambda r: (0, r, 0))],
                                   _sds((d, 3 * d), GRAD_WIRE), pl.BlockSpec((d, 3 * d), lambda r: (0, 0)), (3, d, d), t // tk, terms, store)


def _qkv_dh(name, dqkv, w):
    _, t, d = dqkv.shape
    tm = min(TM, t)

    def body(g_ref, w_ref, o_ref):
        acc = None
        for s in range(3):
            term = _dot(g_ref[s], w_ref[:, s * d:(s + 1) * d], NT)
            acc = term if acc is None else acc + term
        o_ref[...] = acc

    return _call(body, name, _sds((t, d), F32), grid=(t // tm,),
                 in_specs=[pl.BlockSpec((3, tm, d), lambda i: (0, i, 0)), pl.BlockSpec((d, 3 * d), lambda i: (0, 0))],
                 out_specs=pl.BlockSpec((tm, d), lambda i: (i, 0)), vmem=VMEM_BIG)(dqkv, w)


def _glu(pre_block, d):
    return pre_block[:, :d] * jax.nn.sigmoid(pre_block[:, d:])


def _shifted_copies(ext_ref, sh_ref, tt):
    for r in range(1, SUBLANES):
        sh_ref[r - 1] = ext_ref[pl.ds(r, tt + CONV_HALO - SUBLANES), :]


def _rows_from(ext_ref, sh_ref, base, offset, n):
    q, r = divmod(offset, SUBLANES)
    if r == 0:
        return ext_ref[pl.ds(pl.multiple_of(base + offset, SUBLANES), n), :]
    return sh_ref[r - 1, pl.ds(pl.multiple_of(base + q * SUBLANES, SUBLANES), n), :]


def _cv_conv_fwd(pre, dw, dwb, lng, lnb, seq):
    t, d2 = pre.shape
    d = d2 // 2
    tt = min(TT, seq)
    nt = seq // tt
    hb = tt // CONV_HALO

    def body(pre_ref, halo_ref, dw_ref, dwb_ref, lng_ref, lnb_ref, y1_ref, y2_ref, ext_ref, sh_ref):
        i = pl.program_id(1)
        ext_ref[0:CONV_HALO, :] = jnp.where(i == 0, 0.0, _glu(halo_ref[...], d))
        ext_ref[CONV_HALO:, :] = _glu(pre_ref[...], d)
        _shifted_copies(ext_ref, sh_ref, tt)

        acc = jnp.zeros((tt, d), F32) + dwb_ref[...]
        for k in range(CONV_WIDTH):
            acc = acc + _rows_from(ext_ref, sh_ref, 0, CONV_HALO - (CONV_WIDTH - 1) + k, tt) * dw_ref[k:k + 1, :]
        y1_ref[...] = acc
        yc, rstd = _ln_stats(acc)
        y2_ref[...] = _silu(yc * rstd * lng_ref[...] + lnb_ref[...]).astype(BF16)

    vec = pl.BlockSpec((1, d), lambda b, i: (0, 0))
    tile = pl.BlockSpec((tt, d), lambda b, i: (b * nt + i, 0))
    return _call(body, "cv_conv_fwd", (_sds((t, d), F32), _sds((t, d), BF16)), grid=(t // seq, nt),
                 in_specs=[pl.BlockSpec((tt, d2), lambda b, i: (b * nt + i, 0)),
                           pl.BlockSpec((CONV_HALO, d2), lambda b, i: (jnp.maximum((b * nt + i) * hb - 1, 0), 0)),
                           pl.BlockSpec((CONV_HALO, d), lambda b, i: (0, 0)), vec, vec, vec],
                 out_specs=(tile, tile),
                 scratch=[pltpu.VMEM((tt + CONV_HALO, d), F32), pltpu.VMEM((SUBLANES - 1, tt + CONV_HALO - SUBLANES, d), F32)],
                 vmem=VMEM_BIG)(pre, pre, dw, dwb, lng, lnb)


def _cv_norm_bwd(y1, dy2, lng, lnb):
    t, d = y1.shape
    tm = min(TM, t)

    def body(y1_ref, dy2_ref, lng_ref, lnb_ref, dy1_ref, dlg_ref, dlb_ref, dsum_ref):
        @pl.when(pl.program_id(0) == 0)
        def _():
            dlg_ref[...] = jnp.zeros_like(dlg_ref)
            dlb_ref[...] = jnp.zeros_like(dlb_ref)
            dsum_ref[...] = jnp.zeros_like(dsum_ref)

        yc, rstd = _ln_stats(y1_ref[...])
        yhat = yc * rstd
        n = yhat * lng_ref[...] + lnb_ref[...]
        sg = jax.nn.sigmoid(n)
        dn = dy2_ref[...] * (sg * (1.0 + n * (1.0 - sg)))
        dlg_ref[...] += _colsum(dn * yhat)
        dlb_ref[...] += _colsum(dn)
        dyh = dn * lng_ref[...]
        m1 = jnp.mean(dyh, axis=-1, keepdims=True)
        m2 = jnp.mean(dyh * yhat, axis=-1, keepdims=True)
        dy1 = rstd * (dyh - m1 - yhat * m2)
        dy1_ref[...] = dy1
        dsum_ref[...] += _colsum(dy1)

    return _call(body, "cv_norm_bwd", (_sds((t, d), F32), _sds((1, d), F32), _sds((1, d), F32), _sds((1, d), F32)),
                 grid=(t // tm,), in_specs=[_tile_spec(tm, d), _tile_spec(tm, d), _vec_spec(d), _vec_spec(d)],
                 out_specs=(_tile_spec(tm, d), _vec_spec(d), _vec_spec(d), _vec_spec(d)))(y1, dy2, lng, lnb)


def _cv_conv_bwd(pre, dy1, dw, seq):
    t, d2 = pre.shape
    d = d2 // 2
    tt = min(TT // 2, seq)
    nt = seq // tt
    hb = tt // CONV_HALO
    last_halo = t // CONV_HALO - 1

    def body(pre_ref, halo_ref, dy_ref, dyn_ref, dw_ref, dpre_ref, ddw_ref, dbin_ref, ext_ref, dext_ref, sh_ref, dsh_ref):
        b, i = pl.program_id(0), pl.program_id(1)

        @pl.when((b == 0) & (i == 0))
        def _():
            ddw_ref[...] = jnp.zeros_like(ddw_ref)
            dbin_ref[...] = jnp.zeros_like(dbin_ref)

        pv = pre_ref[...]
        ext_ref[0:CONV_HALO, :] = jnp.where(i == 0, 0.0, _glu(halo_ref[...], d))
        ext_ref[CONV_HALO:, :] = _glu(pv, d)
        dyv = dy_ref[...]
        dext_ref[0:tt, :] = dyv
        dext_ref[tt:, :] = jnp.where(i == nt - 1, 0.0, dyn_ref[...])
        _shifted_copies(ext_ref, sh_ref, tt)
        _shifted_copies(dext_ref, dsh_ref, tt)
        nrows = tt // CONV_ROWS

        def input_grad(c, _):
            r0 = pl.multiple_of(c * CONV_ROWS, CONV_ROWS)
            dy0 = jnp.zeros((CONV_ROWS, d), F32)
            for k in range(CONV_WIDTH):
                dy0 = dy0 + _rows_from(dext_ref, dsh_ref, r0, CONV_WIDTH - 1 - k, CONV_ROWS) * dw_ref[k:k + 1, :]
            rows = pl.ds(r0, CONV_ROWS)
            a = pre_ref[rows, :d]
            sg = jax.nn.sigmoid(pre_ref[rows, d:])
            da = dy0 * sg
            dg = dy0 * a * sg * (1.0 - sg)
            dpre_ref[rows, :d] = da.astype(BF16)
            dpre_ref[rows, d:] = dg.astype(BF16)
            dbin_ref[:, :d] += _colsum(da)
            dbin_ref[:, d:] += _colsum(dg)
            return 0

        lax.fori_loop(0, nrows, input_grad, 0)

        for k in range(CONV_WIDTH):
            def tap_grad(c, part, k=k):
                r0 = pl.multiple_of(c * CONV_ROWS, CONV_ROWS)
                prod = dy_ref[pl.ds(r0, CONV_ROWS), :] * _rows_from(ext_ref, sh_ref, r0, CONV_HALO - (CONV_WIDTH - 1) + k, CONV_ROWS)
                return part + jnp.sum(prod.reshape(CONV_ROWS // SUBLANES, SUBLANES, d), axis=0)

            ddw_ref[k:k + 1, :] += _colsum(lax.fori_loop(0, nrows, tap_grad, jnp.zeros((SUBLANES, d), F32)))

    return _call(body, "cv_conv_bwd", (_sds((t, d2), BF16), _sds((CONV_HALO, d), F32), _sds((1, d2), F32)), grid=(t // seq, nt),
                 in_specs=[pl.BlockSpec((tt, d2), lambda b, i: (b * nt + i, 0)),
                           pl.BlockSpec((CONV_HALO, d2), lambda b, i: (jnp.maximum((b * nt + i) * hb - 1, 0), 0)),
                           pl.BlockSpec((tt, d), lambda b, i: (b * nt + i, 0)),
                           pl.BlockSpec((CONV_HALO, d), lambda b, i: (jnp.minimum((b * nt + i + 1) * hb, last_halo), 0)),
                           pl.BlockSpec((CONV_HALO, d), lambda b, i: (0, 0))],
                 out_specs=(pl.BlockSpec((tt, d2), lambda b, i: (b * nt + i, 0)),
                            pl.BlockSpec((CONV_HALO, d), lambda b, i: (0, 0)), pl.BlockSpec((1, d2), lambda b, i: (0, 0))),
                 scratch=[pltpu.VMEM((tt + CONV_HALO, d), F32), pltpu.VMEM((tt + CONV_HALO, d), F32),
                          pltpu.VMEM((SUBLANES - 1, tt + CONV_HALO - SUBLANES, d), F32),
                          pltpu.VMEM((SUBLANES - 1, tt + CONV_HALO - SUBLANES, d), F32)],
                 vmem=VMEM_BIG)(pre, pre, dy1, dy1, dw)


def _adamw(name, w, m, v, g=None, parts=None):
    rows, cols = w.shape
    tr = rows
    for cand in ((512,) if parts is None else ()) + (256, 128, 64, 32, 16, 8):
        if rows % cand == 0 and rows > cand:
            tr = cand
            break
    bc1 = 1.0 - ADAM_B1 ** ADAM_STEP
    bc2 = 1.0 - ADAM_B2 ** ADAM_STEP

    def body(w_ref, m_ref, v_ref, g_ref, go_ref, d_ref, mo_ref, vo_ref):
        if parts is None:
            gv = g_ref[...]
        else:
            gv = g_ref[0].astype(F32)
            for s in range(1, N_DEV):
                gv = gv + g_ref[s].astype(F32)
        mn = ADAM_B1 * m_ref[...] + (1.0 - ADAM_B1) * gv
        vn = ADAM_B2 * v_ref[...] + (1.0 - ADAM_B2) * (gv * gv)
        m_hat = mn / bc1
        v_hat = vn / bc2
        go_ref[...] = gv
        d_ref[...] = -ADAM_LR * (m_hat / (jnp.sqrt(v_hat) + ADAM_EPS) + ADAM_WD * w_ref[...])
        mo_ref[...] = mn
        vo_ref[...] = vn

    blk = pl.BlockSpec((tr, cols), lambda i: (i, 0))
    g_in, g_spec = (g, blk) if parts is None else (parts, pl.BlockSpec((N_DEV, tr, cols), lambda i: (0, i, 0)))
    out = _sds((rows, cols), F32)
    return _call(body, name, (out, out, out, out), grid=(rows // tr,), in_specs=[blk, blk, blk, g_spec],
                 out_specs=(blk, blk, blk, blk), vmem=VMEM_BIG)(w, m, v, g_in)


def _pad_rows(a, rows):
    return jnp.pad(a, ((0, rows - a.shape[0]), (0, 0)))


def _full_cols(gathered, n):
    k = gathered.shape[2]
    return jnp.transpose(gathered[0], (1, 0, 2)).reshape(k, N_DEV * n)


def _col_blocks(full, n):
    k = full.shape[0]
    return jnp.transpose(full.reshape(k, N_DEV, n), (1, 0, 2))[None]


def kernel(x, c, mod_w, mod_b, ln1_g, ln1_b, ln2_g, ln2_b, ffn_w_in, ffn_w_out, gm_w_in, gm_b_in, gm_ln_g, gm_ln_b, gm_w_s, gm_b_s, gm_w_out, fox_w_in, fox_b_f, fox_w_out, sb_w_in, sb_w_out, cv_w_in, cv_b_in, cv_dw, cv_dw_b, cv_ln_g, cv_ln_b, cv_w_out, cv_b_out, loss_target, m_mod_w, m_mod_b, m_ln1_g, m_ln1_b, m_ln2_g, m_ln2_b, m_ffn_w_in, m_ffn_w_out, m_gm_w_in, m_gm_b_in, m_gm_ln_g, m_gm_ln_b, m_gm_w_s, m_gm_b_s, m_gm_w_out, m_fox_w_in, m_fox_b_f, m_fox_w_out, m_sb_w_in, m_sb_w_out, m_cv_w_in, m_cv_b_in, m_cv_dw, m_cv_dw_b, m_cv_ln_g, m_cv_ln_b, m_cv_w_out, m_cv_b_out, v_mod_w, v_mod_b, v_ln1_g, v_ln1_b, v_ln2_g, v_ln2_b, v_ffn_w_in, v_ffn_w_out, v_gm_w_in, v_gm_b_in, v_gm_ln_g, v_gm_ln_b, v_gm_w_s, v_gm_b_s, v_gm_w_out, v_fox_w_in, v_fox_b_f, v_fox_w_out, v_sb_w_in, v_sb_w_out, v_cv_w_in, v_cv_b_in, v_cv_dw, v_cv_dw_b, v_cv_ln_g, v_cv_ln_b, v_cv_w_out, v_cv_b_out):
    weights = dict(mod_w=mod_w, mod_b=mod_b, ln1_g=ln1_g, ln1_b=ln1_b, ln2_g=ln2_g, ln2_b=ln2_b, ffn_w_in=ffn_w_in, ffn_w_out=ffn_w_out, gm_w_in=gm_w_in, gm_b_in=gm_b_in, gm_ln_g=gm_ln_g, gm_ln_b=gm_ln_b, gm_w_s=gm_w_s, gm_b_s=gm_b_s, gm_w_out=gm_w_out, fox_w_in=fox_w_in, fox_b_f=fox_b_f, fox_w_out=fox_w_out, sb_w_in=sb_w_in, sb_w_out=sb_w_out, cv_w_in=cv_w_in, cv_b_in=cv_b_in, cv_dw=cv_dw, cv_dw_b=cv_dw_b, cv_ln_g=cv_ln_g, cv_ln_b=cv_ln_b, cv_w_out=cv_w_out, cv_b_out=cv_b_out)
    mom1 = dict(mod_w=m_mod_w, mod_b=m_mod_b, ln1_g=m_ln1_g, ln1_b=m_ln1_b, ln2_g=m_ln2_g, ln2_b=m_ln2_b, ffn_w_in=m_ffn_w_in, ffn_w_out=m_ffn_w_out, gm_w_in=m_gm_w_in, gm_b_in=m_gm_b_in, gm_ln_g=m_gm_ln_g, gm_ln_b=m_gm_ln_b, gm_w_s=m_gm_w_s, gm_b_s=m_gm_b_s, gm_w_out=m_gm_w_out, fox_w_in=m_fox_w_in, fox_b_f=m_fox_b_f, fox_w_out=m_fox_w_out, sb_w_in=m_sb_w_in, sb_w_out=m_sb_w_out, cv_w_in=m_cv_w_in, cv_b_in=m_cv_b_in, cv_dw=m_cv_dw, cv_dw_b=m_cv_dw_b, cv_ln_g=m_cv_ln_g, cv_ln_b=m_cv_ln_b, cv_w_out=m_cv_w_out, cv_b_out=m_cv_b_out)
    mom2 = dict(mod_w=v_mod_w, mod_b=v_mod_b, ln1_g=v_ln1_g, ln1_b=v_ln1_b, ln2_g=v_ln2_g, ln2_b=v_ln2_b, ffn_w_in=v_ffn_w_in, ffn_w_out=v_ffn_w_out, gm_w_in=v_gm_w_in, gm_b_in=v_gm_b_in, gm_ln_g=v_gm_ln_g, gm_ln_b=v_gm_ln_b, gm_w_s=v_gm_w_s, gm_b_s=v_gm_b_s, gm_w_out=v_gm_w_out, fox_w_in=v_fox_w_in, fox_b_f=v_fox_b_f, fox_w_out=v_fox_w_out, sb_w_in=v_sb_w_in, sb_w_out=v_sb_w_out, cv_w_in=v_cv_w_in, cv_b_in=v_cv_b_in, cv_dw=v_cv_dw, cv_dw_b=v_cv_dw_b, cv_ln_g=v_cv_ln_g, cv_ln_b=v_cv_ln_b, cv_w_out=v_cv_w_out, cv_b_out=v_cv_b_out)
    names = list(weights)

    nb, seq, d = x.shape
    t = nb * seq
    nl = mod_w.shape[0]
    alpha = (2.0 * nl) ** 0.25
    me = 4 * lax.axis_index("x") + 2 * lax.axis_index("y") + lax.axis_index("c")
    xs = x.reshape(t, d)
    tgt = loss_target.reshape(t, d)
    n_mod = mod_w.shape[2]
    n_ffn = ffn_w_in.shape[2]
    n_heads = d // HEAD_DIM
    npair = d // LANES

    c_all = _exchange_small(_pad_rows(c, 8), "gather_c", False)[:, :nb].reshape(N_DEV * nb, d)
    mod_b_loc = lax.dynamic_slice_in_dim(mod_b, me * n_mod, n_mod, axis=1)[:, None, :]
    mod_loc = _mod_fwd(c_all, mod_w, mod_b_loc)
    mod_g = _exchange_small(mod_loc.reshape(nl * N_DEV * nb, n_mod), "gather_mod", False)
    mod_all = jnp.transpose(mod_g.reshape(N_DEV, nl, N_DEV * nb, n_mod), (1, 2, 0, 3)).reshape(nl, N_DEV * nb, N_DEV * n_mod)
    mod_me = lax.dynamic_slice_in_dim(mod_all, me * nb, nb, axis=1)
    mods = [[mod_me[l, :, k * d:(k + 1) * d][:, None, :] for k in range(6)] for l in range(nl)]

    assert nl == 4, "the exchange schedule below is written for the four-layer trunk"
    big = ["ffn_w_in", "ffn_w_out", "gm_w_in", "gm_w_out", "fox_w_in", "fox_w_out", "sb_w_in", "sb_w_out", "cv_w_in", "cv_w_out"]
    shard = {n: weights[n].astype(BF16) for n in big if not n.startswith("ffn")}
    for l in range(nl):
        shard["ffn_w_in", l] = ffn_w_in[l:l + 1].astype(BF16)
        shard["ffn_w_out", l] = ffn_w_out[l:l + 1].astype(BF16)
    now = ["gm_w_in", "gm_w_out", ("ffn_w_in", 0), ("ffn_w_out", 0), "fox_w_in", "fox_w_out"]
    later = [("ffn_w_in", 1), ("ffn_w_out", 1), "sb_w_in", "sb_w_out", ("ffn_w_in", 2), ("ffn_w_out", 2),
             "cv_w_in", "cv_w_out", ("ffn_w_in", 3), ("ffn_w_out", 3)]
    gathered = dict(zip(now, _gather_weights([shard[n] for n in now])))
    w_ffn_out_rows = lambda l: gathered["ffn_w_out", l].reshape(N_DEV // 2, n_ffn, d)
    sq = lambda n: gathered[n].reshape(d, d)
    fox_full = _full_cols(gathered["fox_w_in"], fox_w_in.shape[2])
    fox_qkv_w, fox_f_wt = fox_full[:, :3 * d], jnp.transpose(fox_full[:, 3 * d:])
    cvp = d // N_DEV
    cv_small = jnp.concatenate([_pad_rows(cv_dw[0], CONV_HALO), cv_dw_b, cv_ln_g, cv_ln_b, cv_b_out,
                                cv_b_in.reshape(2, cvp), jnp.zeros((2, cvp), F32)], axis=0)
    cv_all = _exchange_small(cv_small, "gather_cv_small", False)
    cv_rows = jnp.transpose(cv_all, (1, 0, 2)).reshape(cv_small.shape[0], d)
    cv_dw_f, cv_dwb_f, cv_lng_f, cv_lnb_f, cv_bout_f = (cv_rows[:CONV_HALO], cv_rows[32:33], cv_rows[33:34], cv_rows[34:35], cv_rows[35:36])
    cv_bin_f = cv_all[:, 36:38, :].reshape(1, 2 * d)

    saved = []
    h = _modulate(xs, mods[0][1], mods[0][0], seq)
    xin = xs
    for l in range(nl):
        kind = l % 4
        sv = dict(x=xin, h=h)
        if kind == 0:
            pre = _proj_cols("gm_in", h, gathered["gm_w_in"], gm_w_in.shape[2], bias=gm_b_in)
            yv = _gm_spatial_fwd(pre, gm_ln_g, gm_ln_b, gm_w_s[0], jnp.transpose(gm_b_s[0]), seq)
            y = _mm_plain("gm_out", yv, sq("gm_w_out"), NN)
            sv.update(pre=pre, yv=yv)
        elif kind == 1:
            qkv = _qkv_proj("fox_qkv", h, fox_qkv_w)
            ft = _mm("fox_gate_proj", fox_f_wt, h, _sds((n_heads, t), F32), (t // min(TMM, t), 1),
                     pl.BlockSpec((n_heads, d), lambda i, r: (0, 0)), pl.BlockSpec((min(TMM, t), d), lambda i, r: (i, 0)),
                     pl.BlockSpec((n_heads, min(TMM, t)), lambda i, r: (0, i)), NT)
            b_f = jnp.transpose(fox_b_f)
            frow_p = _fox_gate_fwd(ft, b_f, seq).reshape(npair, 2, t)
            o, lse, *arrived = _fox_fwd(qkv, frow_p, nb, seq, gather=[shard[n] for n in later])
            gathered.update(zip(later, arrived))
            sb_qkv_w = _full_cols(gathered["sb_w_in"], sb_w_in.shape[2])
            y = _mm_plain("fox_out", o, sq("fox_w_out"), NN)
            sv.update(qkv=qkv, ft=ft, b_f=b_f, frow=frow_p, o=o, lse=lse)
        elif kind == 2:
            qkv = _qkv_proj("sb_qkv", h, sb_qkv_w)
            o, ltot = _sb_fwd(qkv, nb, seq)
            y = _mm_plain("sb_out", o, sq("sb_w_out"), NN)
            sv.update(qkv=qkv, o=o, ltot=ltot)
        else:
            pre = _proj_cols("cv_in", h, gathered["cv_w_in"], cv_w_in.shape[2], bias=cv_bin_f)
            y1, y2 = _cv_conv_fwd(pre, cv_dw_f, cv_dwb_f, cv_lng_f, cv_lnb_f, seq)
            y = _mm_plain("cv_out", y2, sq("cv_w_out"), NN, bias=cv_bout_f)
            sv.update(pre=pre, y1=y1, y2=y2)
        x1, h2 = _lnres_fwd(xin, y, mods[l][2], ln1_g[l:l + 1], ln1_b[l:l + 1], alpha, seq, nxt=(mods[l][4], mods[l][3]))
        hg, hu, act = _ffn_in(h2, gathered["ffn_w_in", l], 0)
        y2f = _ffn_out(act, w_ffn_out_rows(l), 0)
        sv.update(y=y, x1=x1, h2=h2, hg=hg, hu=hu, act=act, y2f=y2f)
        if l + 1 < nl:
            xin, h = _lnres_fwd(x1, y2f, mods[l][5], ln2_g[l:l + 1], ln2_b[l:l + 1], alpha, seq, nxt=(mods[l + 1][1], mods[l + 1][0]))
        else:
            xin = _lnres_fwd(x1, y2f, mods[l][5], ln2_g[l:l + 1], ln2_b[l:l + 1], alpha, seq)
        saved.append(sv)

    dx, sq_err = _loss_head(xin, tgt)
    loss = lax.psum(0.5 * jnp.sum(sq_err) / d, ("x", "y", "c"))

    small = {}
    bigg = {}
    recv = {}
    dmods = [None] * nl
    d_ln = dict(ln1_g=[None] * nl, ln1_b=[None] * nl, ln2_g=[None] * nl, ln2_b=[None] * nl)
    beside_sb = [("ffn_w_in", 3), ("ffn_w_out", 3), "cv_w_in", "cv_w_out", ("ffn_w_in", 2), ("ffn_w_out", 2)]
    beside_fox = ["sb_w_in", "sb_w_out", ("ffn_w_in", 1), ("ffn_w_out", 1)]
    at_end = ["fox_w_in", "fox_w_out", ("ffn_w_in", 0), ("ffn_w_out", 0), "gm_w_in", "gm_w_out"]
    for l in reversed(range(nl)):
        sv = saved[l]
        kind = l % 4
        dxr, dy2, dlg, dlb, _, dgate2 = _lnres_bwd(dx, sv["x1"], sv["y2f"], mods[l][5], ln2_g[l:l + 1], alpha, seq)
        d_ln["ln2_g"][l], d_ln["ln2_b"][l] = dlg, dlb
        dg_, du_ = _ffn_dact(dy2, w_ffn_out_rows(l), sv["hg"], sv["hu"], 0)
        dwg, dwu, dwo = _ffn_bwd_weights(sv["h2"], dy2, sv["act"], dg_, du_)
        bigg["ffn_w_in", l] = jnp.concatenate([dwg, dwu], axis=0)[None]
        bigg["ffn_w_out", l] = dwo.reshape(1, N_DEV, n_ffn // 2, d)
        dh2 = _ffn_dh(dg_, du_, gathered["ffn_w_in", l], 0)
        dx1, dsc2, dsh2 = _modulate_bwd(dh2, dxr, sv["x1"], mods[l][4], seq)
        dxr, dy, dlg, dlb, dysum, dgate1 = _lnres_bwd(dx1, sv["x"], sv["y"], mods[l][2], ln1_g[l:l + 1], alpha, seq)
        d_ln["ln1_g"][l], d_ln["ln1_b"][l] = dlg, dlb
        hh = sv["h"]
        if kind == 0:
            dyv = _mm_plain("gm_out_bwd", dy, sq("gm_w_out"), NT)
            bigg["gm_w_out"] = _mm_plain("gm_out_dw", sv["yv"], dy, TN, GRAD_WIRE).reshape(1, N_DEV, d // N_DEV, d)
            dpre, dws, dbst, dlng, dlnb, dbin = _gm_spatial_bwd(sv["pre"], dyv, gm_ln_g, gm_ln_b, gm_w_s[0], jnp.transpose(gm_b_s[0]), seq)
            small.update(gm_w_s=dws[None], gm_b_s=jnp.transpose(dbst)[None], gm_ln_g=dlng, gm_ln_b=dlnb, gm_b_in=dbin)
            bigg["gm_w_in"] = _grad_cols("gm_in_dw", hh, dpre, gm_w_in.shape[2])
            dh = _back_cols("gm_in_bwd", dpre, gathered["gm_w_in"], gm_w_in.shape[2])
        elif kind == 1:
            do = _mm_plain("fox_out_bwd", dy, sq("fox_w_out"), NT)
            bigg["fox_w_out"] = _mm_plain("fox_out_dw", sv["o"], dy, TN, GRAD_WIRE).reshape(1, N_DEV, d // N_DEV, d)
            dqkv, dfr, dfq, *landed = _fox_bwd(sv["qkv"], sv["frow"], sv["o"], do, sv["lse"], nb, seq,
                                               scatter=[bigg[n] for n in beside_fox])
            recv.update(zip(beside_fox, landed))
            dft, dbf = _fox_gate_bwd(sv["ft"], sv["b_f"], dfr.reshape(n_heads, t),
                                     jnp.transpose(dfq, (0, 2, 1)).reshape(n_heads, t), seq)
            small["fox_b_f"] = jnp.transpose(dbf)
            dw_qkv = _qkv_dw("fox_qkv_dw", hh, dqkv)
            tk = min(TMM, t)
            dw_ft = _mm("fox_gate_dw", dft, hh, _sds((n_heads, d), F32), (1, t // tk),
                        pl.BlockSpec((n_heads, tk), lambda j, r: (0, r)), pl.BlockSpec((tk, d), lambda j, r: (r, 0)),
                        pl.BlockSpec((n_heads, d), lambda j, r: (0, 0)), NN)
            bigg["fox_w_in"] = _col_blocks(jnp.concatenate([dw_qkv, jnp.transpose(dw_ft).astype(GRAD_WIRE)], axis=1), fox_w_in.shape[2])
            dh_a = _qkv_dh("fox_qkv_bwd", dqkv, fox_qkv_w)
            tm = min(TMM, t)
            dh = _mm("fox_gate_bwd_h", dft, fox_f_wt, _sds((t, d), F32), (t // tm, 1),
                     pl.BlockSpec((n_heads, tm), lambda i, r: (0, i)), pl.BlockSpec((n_heads, d), lambda i, r: (0, 0)),
                     pl.BlockSpec((tm, d), lambda i, r: (i, 0)), TN, (dh_a,), (pl.BlockSpec((tm, d), lambda i, r: (i, 0)),), _add)
        elif kind == 2:
            do = _mm_plain("sb_out_bwd", dy, sq("sb_w_out"), NT)
            bigg["sb_w_out"] = _mm_plain("sb_out_dw", sv["o"], dy, TN, GRAD_WIRE).reshape(1, N_DEV, d // N_DEV, d)
            dqkv, *landed = _sb_bwd(sv["qkv"], do, sv["ltot"], nb, seq, scatter=[bigg[n] for n in beside_sb])
            recv.update(zip(beside_sb, landed))
            bigg["sb_w_in"] = _col_blocks(_qkv_dw("sb_qkv_dw", hh, dqkv), sb_w_in.shape[2])
            dh = _qkv_dh("sb_qkv_bwd", dqkv, sb_qkv_w)
        else:
            dy2c = _mm_plain("cv_out_bwd", dy, sq("cv_w_out"), NT)
            bigg["cv_w_out"] = _mm_plain("cv_out_dw", sv["y2"], dy, TN, GRAD_WIRE).reshape(1, N_DEV, d // N_DEV, d)
            dy1, dlng, dlnb, ddwb = _cv_norm_bwd(sv["y1"], dy2c, cv_lng_f, cv_lnb_f)
            dpre, ddw, dbin = _cv_conv_bwd(sv["pre"], dy1, cv_dw_f, seq)
            small.update(cv_b_out=dysum, cv_ln_g=dlng, cv_ln_b=dlnb, cv_dw_b=ddwb, cv_dw=ddw[:CONV_WIDTH], cv_b_in=dbin)
            bigg["cv_w_in"] = _grad_cols("cv_in_dw", hh, dpre, cv_w_in.shape[2])
            dh = _back_cols("cv_in_bwd", dpre, gathered["cv_w_in"], cv_w_in.shape[2])
        dx, dsc1, dsh1 = _modulate_bwd(dh, dxr, sv["x"], mods[l][1], seq)
        dmods[l] = jnp.concatenate([dsh1, dsc1, dgate1, dsh2, dsc2, dgate2], axis=2)[:, 0, :]
    grad_x = dx.reshape(nb, seq, d)
    for n in d_ln:
        small[n] = jnp.concatenate(d_ln[n], axis=0)

    dmod_rows = jnp.stack(dmods).reshape(nl * nb, 6 * d)
    dmod_g = _exchange_small(_pad_rows(dmod_rows, 8 * ((nl * nb + 7) // 8)), "gather_dmod", False)[:, :nl * nb]
    dmod_all = jnp.transpose(dmod_g.reshape(N_DEV, nl, nb, 6 * d), (1, 0, 2, 3)).reshape(nl, N_DEV * nb, 6 * d)
    dmod_loc = lax.dynamic_slice_in_dim(dmod_all, me * n_mod, n_mod, axis=2)
    g_mod_w, g_mod_b = _mod_bwd(c_all, dmod_loc, dmod_all)
    grads = dict(mod_w=g_mod_w, mod_b=g_mod_b[:, 0, :])

    rep = ["ln1_g", "ln1_b", "ln2_g", "ln2_b", "gm_b_in", "gm_ln_g", "gm_ln_b", "gm_w_s", "gm_b_s", "fox_b_f"]
    cvs = ["cv_b_in", "cv_dw", "cv_dw_b", "cv_ln_g", "cv_ln_b", "cv_b_out"]

    def rows_of(a):
        flat = a.reshape(-1)
        pad = (-flat.shape[0]) % d
        return jnp.pad(flat, (0, pad)).reshape(-1, d)

    pack_rows = [rows_of(small[n]) for n in rep + cvs]
    counts = [r.shape[0] for r in pack_rows]
    total = sum(counts)
    pack = _pad_rows(jnp.concatenate(pack_rows, axis=0), 8 * ((total + 7) // 8))
    summed = _exchange_small(pack, "allreduce_small", True)
    offs = [sum(counts[:i]) for i in range(len(counts))]
    rep_rows = sum(counts[:len(rep)])
    for n, o_, cnt in zip(rep + cvs, offs, counts):
        full = summed[o_:o_ + cnt].reshape(-1)
        if n in rep:
            grads[n] = full[:weights[n].size].reshape(weights[n].shape)
        else:
            wshape = weights[n].shape
            cols = wshape[-1]
            full = full[:math.prod(wshape[:-1]) * cols * N_DEV].reshape(wshape[:-1] + (cols * N_DEV,))
            grads[n] = lax.dynamic_slice_in_dim(full, me * cols, cols, axis=full.ndim - 1)

    recv.update(zip(at_end, _scatter_grads([bigg[n] for n in at_end])))
    for n in ("ffn_w_in", "ffn_w_out"):
        recv[n] = jnp.concatenate([recv[n, l] for l in range(nl)], axis=1)

    outs = {}

    def view2(a):
        return a.reshape(-1, a.shape[-1])

    for n in big:
        w2 = view2(weights[n])
        res = _adamw("adamw_" + n, w2, view2(mom1[n]), view2(mom2[n]), parts=recv[n].reshape((N_DEV,) + w2.shape))
        outs[n] = [r.reshape(weights[n].shape) for r in res]
    res = _adamw("adamw_mod_w", view2(mod_w), view2(m_mod_w), view2(v_mod_w), g=view2(grads["mod_w"]))
    outs["mod_w"] = [r.reshape(mod_w.shape) for r in res]
    rp = lambda src: _pad_rows(jnp.concatenate([rows_of(src[n]) for n in rep], axis=0), 8 * ((rep_rows + 7) // 8))
    res = _adamw("adamw_replicated", rp(weights), rp(mom1), rp(mom2), g=rp(grads))
    for n, o_, cnt in zip(rep, offs, counts):
        outs[n] = [r[o_:o_ + cnt].reshape(-1)[:weights[n].size].reshape(weights[n].shape) for r in res]
    cv_cols = weights["cv_b_out"].shape[-1]
    cp = lambda src: jnp.concatenate([src[n].reshape(-1, cv_cols) for n in cvs], axis=0)
    cv_cnt = [weights[n].size // cv_cols for n in cvs]
    cv_tot = sum(cv_cnt)
    cpp = lambda src: _pad_rows(cp(src), 8 * ((cv_tot + 7) // 8))
    res = _adamw("adamw_cv_small", cpp(weights), cpp(mom1), cpp(mom2), g=cpp(grads))
    o_ = 0
    for n, cnt in zip(cvs, cv_cnt):
        outs[n] = [r[o_:o_ + cnt].reshape(weights[n].shape) for r in res]
        o_ += cnt
    res = _adamw("adamw_mod_b", mod_b, m_mod_b, v_mod_b, g=grads["mod_b"])
    outs["mod_b"] = list(res)

    return (loss, grad_x, *[outs[n][0] for n in names], *[outs[n][1] for n in names],
            *[outs[n][2] for n in names], *[outs[n][3] for n in names])
```

```python
import functools
import math

import jax
import jax.numpy as jnp
from jax import lax
from jax.experimental import pallas as pl
from jax.experimental.pallas import tpu as pltpu

F32 = jnp.float32
BF16 = jnp.bfloat16
MESH = pl.DeviceIdType.MESH

N_DEV = 8
HEAD_DIM = 64
LANES = 128
SUBLANES = 8
GM_CHUNK = 128
GM_GROUPS = 8
CONV_WIDTH = 31
CONV_HALO = 32
CONV_ROWS = 32
LN_EPS = 1e-5
NEG_INF = -1e30
SB_DEAD = -100.0
GRAD_WIRE = jnp.bfloat16
FFN_KEEP = jnp.bfloat16

ADAM_LR = 0.001
ADAM_B1 = 0.9
ADAM_B2 = 0.999
ADAM_EPS = 1e-08
ADAM_WD = 0.01
ADAM_STEP = 10

TM = 512
TMM = 1024
TQ = 256
FOX_FWD_UNROLL = 4
FOX_BWD_UNROLL = 4
SB_BWD_UNROLL = 2
TT = 512
VMEM_BIG = 56 * 1024 * 1024

NN = (((1,), (0,)), ((), ()))
NT = (((1,), (1,)), ((), ()))
TN = (((0,), (0,)), ((), ()))


def _call(body, name, out_shape, grid=None, in_specs=None, out_specs=None, scratch=(), vmem=None):
    params = {}
    if grid is not None:
        params["dimension_semantics"] = ("arbitrary",) * len(grid)
    if vmem is not None:
        params["vmem_limit_bytes"] = vmem
    kw = {}
    if grid is not None:
        kw["grid"] = grid
    if in_specs is not None:
        kw["in_specs"] = in_specs
    if out_specs is not None:
        kw["out_specs"] = out_specs
    return pl.pallas_call(body, name=name, out_shape=out_shape, scratch_shapes=list(scratch),
                          compiler_params=pltpu.CompilerParams(**params), **kw)


def _sds(shape, dtype):
    return jax.ShapeDtypeStruct(tuple(shape), dtype)


def _dot(a, b, dims=NN):
    return lax.dot_general(a.astype(BF16), b.astype(BF16), dims, preferred_element_type=F32)


def _split3(x):
    h1 = x.astype(BF16)
    r1 = x - h1.astype(F32)
    h2 = r1.astype(BF16)
    h3 = (r1 - h2.astype(F32)).astype(BF16)
    return h1, h2, h3


def _dot_exact(x, m, dims=NN):
    h1, h2, h3 = _split3(x)
    d = lambda h: lax.dot_general(h, m, dims, preferred_element_type=F32)
    return (d(h1) + d(h2)) + d(h3)


def _dot_exact_rhs(m, x, dims=NN):
    h1, h2, h3 = _split3(x)
    d = lambda h: lax.dot_general(m, h, dims, preferred_element_type=F32)
    return (d(h1) + d(h2)) + d(h3)


def _silu(x):
    return x * jax.nn.sigmoid(x)


def _gelu(x):
    return 0.5 * x * (1.0 + lax.erf(x * (2.0 ** -0.5)))


def _gelu_grad(x):
    return 0.5 * (1.0 + lax.erf(x * (2.0 ** -0.5))) + x * jnp.exp(-0.5 * x * x) * ((2.0 * math.pi) ** -0.5)


def _log_sigmoid(z):
    return jnp.minimum(z, 0.0) - jnp.log(1.0 + jnp.exp(-jnp.abs(z)))


def _ln_stats(r):
    mu = jnp.mean(r, axis=-1, keepdims=True)
    rc = r - mu
    var = jnp.mean(rc * rc, axis=-1, keepdims=True)
    return rc, lax.rsqrt(var + LN_EPS)


def _colsum(x):
    return jnp.sum(x, axis=0, keepdims=True)


def _peers():
    mx, my, mc = lax.axis_index("x"), lax.axis_index("y"), lax.axis_index("c")
    me = 4 * mx + 2 * my + mc
    out = []
    for k in range(1, N_DEV):
        px = 1 - mx if (k >> 2) & 1 else mx
        py = 1 - my if (k >> 1) & 1 else my
        pc = 1 - mc if k & 1 else mc
        out.append(((px, py, pc), 4 * px + 2 * py + pc))
    return me, out


def _exchange_small(x, name, reduce):
    rows, cols = x.shape

    def body(x_ref, o_ref, *rest):
        if reduce:
            land, send_sems, recv_sems, local_sem = rest
        else:
            land = o_ref
            send_sems, recv_sems, local_sem = rest
        me, peers = _peers()
        mine = pltpu.make_async_copy(x_ref, land.at[me], local_sem)
        mine.start()
        sends = []
        for k, (peer, _) in enumerate(peers):
            cp = pltpu.make_async_remote_copy(src_ref=x_ref, dst_ref=land.at[me], send_sem=send_sems.at[k],
                                              recv_sem=recv_sems.at[k], device_id=peer, device_id_type=MESH)
            cp.start()
            sends.append(cp)
        for k, (peer, blk) in enumerate(peers):
            pltpu.make_async_remote_copy(src_ref=x_ref, dst_ref=land.at[blk], send_sem=send_sems.at[k],
                                         recv_sem=recv_sems.at[k], device_id=peer, device_id_type=MESH).wait_recv()
        for cp in sends:
            cp.wait_send()
        mine.wait()
        if reduce:
            acc = land[0]
            for s in range(1, N_DEV):
                acc = acc + land[s]
            o_ref[...] = acc

    vm = pl.BlockSpec(memory_space=pltpu.VMEM)
    scratch = [pltpu.SemaphoreType.DMA((N_DEV - 1,)), pltpu.SemaphoreType.DMA((N_DEV - 1,)), pltpu.SemaphoreType.DMA]
    if reduce:
        scratch = [pltpu.VMEM((N_DEV, rows, cols), F32)] + scratch
        out = _sds((rows, cols), F32)
    else:
        out = _sds((N_DEV, rows, cols), F32)
    return _call(body, name, out, in_specs=[vm], out_specs=vm, scratch=scratch, vmem=VMEM_BIG)(x)


def _comm_scratch(n):
    return [pltpu.SemaphoreType.DMA((n, N_DEV - 1)), pltpu.SemaphoreType.DMA((n, N_DEV - 1)), pltpu.SemaphoreType.DMA((n,))]


def _gather_stage(stage, ins, outs, send_sems, recv_sems, local_sems):
    n = len(ins)
    mx, my, mc = lax.axis_index("x"), lax.axis_index("y"), lax.axis_index("c")
    here, sibling = (mx, my, mc), (mx, my, 1 - mc)
    chips = [(1 - mx, my), (mx, 1 - my), (1 - mx, 1 - my)]

    def block(px, py, pc):
        return 4 * px + 2 * py + pc

    def copy(a, k, blk, to, src=None):
        dst = outs[a].at[:, blk]
        return pltpu.make_async_remote_copy(src_ref=dst if src is None else src, dst_ref=dst, send_sem=send_sems.at[a, k],
                                            recv_sem=recv_sems.at[a, k], device_id=to, device_id_type=MESH)

    me = block(*here)
    for a in range(n):
        local = pltpu.make_async_copy(ins[a], outs[a].at[:, me], local_sems.at[a])
        first = [copy(a, 0, me, sibling, src=ins[a])] + [copy(a, 1 + j, me, (*chip, mc), src=ins[a]) for j, chip in enumerate(chips)]
        if stage == 0:
            local.start()
            for cp in first:
                cp.start()
        if stage == 1:
            for j, chip in enumerate(chips):
                copy(a, 1 + j, block(*chip, mc), here).wait_recv()
                copy(a, 4 + j, block(*chip, mc), sibling).start()
        if stage == 2:
            copy(a, 0, block(mx, my, 1 - mc), here).wait_recv()
            for j, chip in enumerate(chips):
                copy(a, 4 + j, block(*chip, 1 - mc), here).wait_recv()
            for cp in first:
                cp.wait_send()
            for j, chip in enumerate(chips):
                copy(a, 4 + j, block(*chip, mc), sibling).wait_send()
            local.wait()


def _gather_weights(shards):
    n = len(shards)

    def body(*refs):
        for stage in range(3):
            _gather_stage(stage, refs[:n], refs[n:2 * n], *refs[2 * n:])

    hbm = pl.BlockSpec(memory_space=pl.ANY)
    return _call(body, "gather_weights", _gathered_shapes(shards), in_specs=[hbm] * n, out_specs=[hbm] * n, scratch=_comm_scratch(n))(*shards)


def _gathered_shapes(shards):
    return [_sds((s.shape[0], N_DEV) + s.shape[1:], s.dtype) for s in shards]


def _scatter_grads(grads):
    n = len(grads)

    def body(*refs):
        for stage in range(2):
            _scatter_stage(stage, refs[:n], refs[n:2 * n], *refs[2 * n:])

    hbm = pl.BlockSpec(memory_space=pl.ANY)
    return _call(body, "scatter_grads", _scattered_shapes(grads), in_specs=[hbm] * n, out_specs=[hbm] * n, scratch=_comm_scratch(n))(*grads)


def _scattered_shapes(grads):
    return [_sds((N_DEV, g.shape[0]) + g.shape[2:], g.dtype) for g in grads]


def _scatter_stage(stage, ins, outs, send_sems, recv_sems, local_sems):
    me, peers = _peers()
    for a in range(len(ins)):
        local = pltpu.make_async_copy(ins[a].at[:, me], outs[a].at[me], local_sems.at[a])
        sends = [pltpu.make_async_remote_copy(src_ref=ins[a].at[:, blk], dst_ref=outs[a].at[me], send_sem=send_sems.at[a, k],
                                              recv_sem=recv_sems.at[a, k], device_id=peer, device_id_type=MESH)
                 for k, (peer, blk) in enumerate(peers)]
        if stage == 0:
            local.start()
            for cp in sends:
                cp.start()
        if stage == 1:
            for k, (peer, blk) in enumerate(peers):
                pltpu.make_async_remote_copy(src_ref=ins[a].at[:, me], dst_ref=outs[a].at[blk], send_sem=send_sems.at[a, k],
                                             recv_sem=recv_sems.at[a, k], device_id=peer, device_id_type=MESH).wait_recv()
            for cp in sends:
                cp.wait_send()
            local.wait()


def _mm(name, a, b, out, grid, a_spec, b_spec, o_spec, dims, extra=(), extra_specs=(), epilogue=None, vmem=VMEM_BIG):
    nred = grid[-1]
    red_axis = len(grid) - 1
    acc_shape = tuple(d for d in o_spec.block_shape if d is not None)
    n_extra = len(extra)

    def body(a_ref, b_ref, *rest):
        ex = rest[:n_extra]
        o_ref = rest[n_extra]

        def finish(acc):
            if epilogue is not None:
                acc = epilogue(acc, *[e[...] for e in ex])
            o_ref[...] = acc.astype(o_ref.dtype)

        prod = _dot(a_ref[...], b_ref[...], dims)
        if nred == 1:
            finish(prod)
        else:
            acc_ref = rest[n_extra + 1]
            r = pl.program_id(red_axis)

            @pl.when(r == 0)
            def _():
                acc_ref[...] = prod

            @pl.when(r > 0)
            def _():
                acc_ref[...] += prod

            @pl.when(r == nred - 1)
            def _():
                finish(acc_ref[...])

    scratch = [pltpu.VMEM(acc_shape, F32)] if nred > 1 else []
    return _call(body, name, out, grid=grid, in_specs=[a_spec, b_spec, *extra_specs], out_specs=o_spec,
                 scratch=scratch, vmem=vmem)(a, b, *extra)


def _add(acc, x):
    return acc + x


def _proj_cols(name, h, w, n_slot, bias=None, out_dtype=F32):
    t, k = h.shape
    s = w.shape[1]
    tm = min(TMM, t)
    extra, especs, epi = (), (), None
    if bias is not None:
        extra, especs, epi = (bias,), (pl.BlockSpec((1, n_slot), lambda i, j, r: (0, j)),), _add
    return _mm(name, h, w, _sds((t, s * n_slot), out_dtype), (t // tm, s, 1),
               pl.BlockSpec((tm, k), lambda i, j, r: (i, 0)),
               pl.BlockSpec((None, None, k, n_slot), lambda i, j, r: (0, j, 0, 0)),
               pl.BlockSpec((tm, n_slot), lambda i, j, r: (i, j)), NN, extra, especs, epi)


def _accumulate_over_tokens(name, ins, in_specs, out, o_spec, acc_shape, n_steps, terms, store=None):
    def body(*refs):
        o_ref, acc_ref = refs[len(ins)], refs[len(ins) + 1]
        r = pl.program_id(0)

        @pl.when(r == 0)
        def _():
            acc_ref[...] = jnp.zeros_like(acc_ref)

        for s, prod in enumerate(terms(*refs[:len(ins)])):
            acc_ref[s] += prod

        @pl.when(r == n_steps - 1)
        def _():
            if store is None:
                o_ref[...] = acc_ref[...].reshape(o_ref.shape).astype(o_ref.dtype)
            else:
                store(o_ref, acc_ref)

    return _call(body, name, out, grid=(n_steps,), in_specs=in_specs, out_specs=o_spec,
                 scratch=[pltpu.VMEM(acc_shape, F32)], vmem=VMEM_BIG)(*ins)


def _grad_cols(name, h, g, n_slot):
    t, k = h.shape
    s = g.shape[1] // n_slot
    tk = min(TMM, t)

    def terms(h_ref, g_ref):
        hv = h_ref[...]
        return [_dot(hv, g_ref[:, j * n_slot:(j + 1) * n_slot], TN) for j in range(s)]

    return _accumulate_over_tokens(name, (h, g), [pl.BlockSpec((tk, k), lambda r: (r, 0)), pl.BlockSpec((tk, s * n_slot), lambda r: (r, 0))],
                                   _sds((1, s, k, n_slot), GRAD_WIRE), pl.BlockSpec((1, s, k, n_slot), lambda r: (0, 0, 0, 0)),
                                   (s, k, n_slot), t // tk, terms)


def _back_cols(name, g, w, n_slot):
    t = g.shape[0]
    s, k = w.shape[1], w.shape[2]
    tm = min(TM, t)

    def body(g_ref, w_ref, o_ref):
        acc = None
        for j in range(s):
            term = _dot(g_ref[:, j * n_slot:(j + 1) * n_slot], w_ref[j], NT)
            acc = term if acc is None else acc + term
        o_ref[...] = acc

    return _call(body, name, _sds((t, k), F32), grid=(t // tm,),
                 in_specs=[pl.BlockSpec((tm, s * n_slot), lambda i: (i, 0)), pl.BlockSpec((None, s, k, n_slot), lambda i: (0, 0, 0, 0))],
                 out_specs=pl.BlockSpec((tm, k), lambda i: (i, 0)), vmem=VMEM_BIG)(g, w)


def _mm_plain(name, a, b, dims, out_dtype=F32, bias=None):
    if dims == TN:
        t, k = a.shape
        n = b.shape[1]
        tk = min(TMM, t)
        return _mm(name, a, b, _sds((k, n), out_dtype), (1, t // tk),
                   pl.BlockSpec((tk, k), lambda j, r: (r, 0)), pl.BlockSpec((tk, n), lambda j, r: (r, 0)),
                   pl.BlockSpec((k, n), lambda j, r: (0, 0)), TN)
    t = a.shape[0]
    tm = min(TMM, t)
    n = b.shape[1] if dims == NN else b.shape[0]
    extra, especs, epi = (), (), None
    if bias is not None:
        extra, especs, epi = (bias,), (pl.BlockSpec((1, n), lambda i, r: (0, 0)),), _add
    return _mm(name, a, b, _sds((t, n), out_dtype), (t // tm, 1),
               pl.BlockSpec((tm, a.shape[1]), lambda i, r: (i, 0)), pl.BlockSpec(b.shape, lambda i, r: (0, 0)),
               pl.BlockSpec((tm, n), lambda i, r: (i, 0)), dims, extra, especs, epi)


def _mod_fwd(c_all, mod_w, mod_b_loc):
    nl, d, n = mod_w.shape
    nb = c_all.shape[0]

    def body(c_ref, w_ref, b_ref, o_ref):
        o_ref[...] = _dot(_silu(c_ref[...]), w_ref[...]) + b_ref[...]

    return _call(body, "mod_fwd", _sds((nl, nb, n), F32), grid=(nl,),
                 in_specs=[pl.BlockSpec((nb, d), lambda l: (0, 0)), pl.BlockSpec((None, d, n), lambda l: (l, 0, 0)),
                           pl.BlockSpec((None, 1, n), lambda l: (l, 0, 0))],
                 out_specs=pl.BlockSpec((None, nb, n), lambda l: (l, 0, 0)))(c_all, mod_w, mod_b_loc)


def _mod_bwd(c_all, dmod_loc, dmod_all):
    nl, nb, n = dmod_loc.shape
    d = c_all.shape[1]
    n_all = dmod_all.shape[2]

    def body(c_ref, dl_ref, da_ref, gw_ref, gb_ref):
        gw_ref[...] = _dot(_silu(c_ref[...]), dl_ref[...], TN)
        gb_ref[...] = _colsum(da_ref[...])

    return _call(body, "mod_bwd", (_sds((nl, d, n), F32), _sds((nl, 1, n_all), F32)), grid=(nl,),
                 in_specs=[pl.BlockSpec((nb, d), lambda l: (0, 0)), pl.BlockSpec((None, nb, n), lambda l: (l, 0, 0)),
                           pl.BlockSpec((None, nb, n_all), lambda l: (l, 0, 0))],
                 out_specs=(pl.BlockSpec((None, d, n), lambda l: (l, 0, 0)), pl.BlockSpec((None, 1, n_all), lambda l: (l, 0, 0))),
                 )(c_all, dmod_loc, dmod_all)


def _row_spec(d, tpb):
    return pl.BlockSpec((None, 1, d), lambda i: (i // tpb, 0, 0))


def _tile_spec(tm, d):
    return pl.BlockSpec((tm, d), lambda i: (i, 0))


def _vec_spec(d):
    return pl.BlockSpec((1, d), lambda i: (0, 0))


def _modulate(x, sc, sh, seq):
    t, d = x.shape
    tm = min(TM, seq)
    tpb = seq // tm

    def body(x_ref, sc_ref, sh_ref, h_ref):
        h_ref[...] = (x_ref[...] * (1.0 + sc_ref[...]) + sh_ref[...]).astype(BF16)

    return _call(body, "modulate", _sds((t, d), BF16), grid=(t // tm,),
                 in_specs=[_tile_spec(tm, d), _row_spec(d, tpb), _row_spec(d, tpb)], out_specs=_tile_spec(tm, d))(x, sc, sh)


def _lnres_fwd(x, y, gate, lg, lb, alpha, seq, nxt=None):
    t, d = x.shape
    tm = min(TM, seq)
    tpb = seq // tm

    def body(x_ref, y_ref, g_ref, lg_ref, lb_ref, *rest):
        r = alpha * x_ref[...] + (1.0 + g_ref[...]) * y_ref[...]
        rc, rstd = _ln_stats(r)
        xn = rc * rstd * lg_ref[...] + lb_ref[...]
        if nxt is None:
            rest[0][...] = xn
        else:
            sc_ref, sh_ref, xo_ref, h_ref = rest
            xo_ref[...] = xn
            h_ref[...] = (xn * (1.0 + sc_ref[...]) + sh_ref[...]).astype(BF16)

    ins = [_tile_spec(tm, d), _tile_spec(tm, d), _row_spec(d, tpb), _vec_spec(d), _vec_spec(d)]
    if nxt is None:
        return _call(body, "lnres_fwd_last", _sds((t, d), F32), grid=(t // tm,), in_specs=ins,
                     out_specs=_tile_spec(tm, d))(x, y, gate, lg, lb)
    return _call(body, "lnres_fwd", (_sds((t, d), F32), _sds((t, d), BF16)), grid=(t // tm,),
                 in_specs=ins + [_row_spec(d, tpb), _row_spec(d, tpb)],
                 out_specs=(_tile_spec(tm, d), _tile_spec(tm, d)))(x, y, gate, lg, lb, *nxt)


def _loss_head(x, tgt):
    t, d = x.shape
    tm = min(TM, t)

    def body(x_ref, t_ref, dx_ref, sq_ref):
        e = x_ref[...] - t_ref[...]
        dx_ref[...] = e * (1.0 / d)

        @pl.when(pl.program_id(0) == 0)
        def _():
            sq_ref[...] = jnp.zeros_like(sq_ref)

        sq_ref[...] += _colsum(e * e)

    return _call(body, "loss_head", (_sds((t, d), F32), _sds((1, d), F32)), grid=(t // tm,),
                 in_specs=[_tile_spec(tm, d), _tile_spec(tm, d)], out_specs=(_tile_spec(tm, d), _vec_spec(d)))(x, tgt)


def _lnres_bwd(dxo, x, y, gate, lg, alpha, seq, through=None):
    t, d = x.shape
    tm = min(TM, seq)
    tpb = seq // tm
    nb = t // seq

    def body(dxo_ref, x_ref, y_ref, g_ref, lg_ref, *rest):
        if through is None:
            dxr_ref, dy_ref, dlg_ref, dlb_ref, dys_ref, dg_ref = rest
        else:
            dh_ref, sc_ref, lb_ref, dxr_ref, dy_ref, dlg_ref, dlb_ref, dys_ref, dg_ref, dsc_ref, dsh_ref = rest
        i = pl.program_id(0)
        yv = y_ref[...]
        r = alpha * x_ref[...] + (1.0 + g_ref[...]) * yv
        rc, rstd = _ln_stats(r)
        xhat = rc * rstd
        dxo_v = dxo_ref[...]
        if through is not None:
            dhv = dh_ref[...]
            dxo_v = dxo_v + dhv * (1.0 + sc_ref[...])

            @pl.when(i % tpb == 0)
            def _():
                dsc_ref[...] = jnp.zeros_like(dsc_ref)
                dsh_ref[...] = jnp.zeros_like(dsh_ref)

            dsc_ref[...] += _colsum(dhv * (xhat * lg_ref[...] + lb_ref[...]))
            dsh_ref[...] += _colsum(dhv)
        dxh = dxo_v * lg_ref[...]
        m1 = jnp.mean(dxh, axis=-1, keepdims=True)
        m2 = jnp.mean(dxh * xhat, axis=-1, keepdims=True)
        dr = rstd * (dxh - m1 - xhat * m2)
        dyv = (1.0 + g_ref[...]) * dr
        dxr_ref[...] = alpha * dr
        dy_ref[...] = dyv.astype(BF16)

        @pl.when(i == 0)
        def _():
            dlg_ref[...] = jnp.zeros_like(dlg_ref)
            dlb_ref[...] = jnp.zeros_like(dlb_ref)
            dys_ref[...] = jnp.zeros_like(dys_ref)

        @pl.when(i % tpb == 0)
        def _():
            dg_ref[...] = jnp.zeros_like(dg_ref)

        dlg_ref[...] += _colsum(dxo_v * xhat)
        dlb_ref[...] += _colsum(dxo_v)
        dys_ref[...] += _colsum(dyv)
        dg_ref[...] += _colsum(dr * yv)

    out = [_sds((t, d), F32), _sds((t, d), BF16), _sds((1, d), F32), _sds((1, d), F32), _sds((1, d), F32), _sds((nb, 1, d), F32)]
    out_specs = [_tile_spec(tm, d), _tile_spec(tm, d), _vec_spec(d), _vec_spec(d), _vec_spec(d), _row_spec(d, tpb)]
    ins = [dxo, x, y, gate, lg]
    in_specs = [_tile_spec(tm, d), _tile_spec(tm, d), _tile_spec(tm, d), _row_spec(d, tpb), _vec_spec(d)]
    if through is not None:
        ins += list(through)
        in_specs += [_tile_spec(tm, d), _row_spec(d, tpb), _vec_spec(d)]
        out += [_sds((nb, 1, d), F32), _sds((nb, 1, d), F32)]
        out_specs += [_row_spec(d, tpb), _row_spec(d, tpb)]
    return _call(body, "lnres_bwd" if through is None else "lnres_mod_bwd", tuple(out), grid=(t // tm,),
                 in_specs=in_specs, out_specs=tuple(out_specs))(*ins)


def _modulate_bwd(dh, dxr, x, sc, seq):
    t, d = x.shape
    tm = min(TM, seq)
    tpb = seq // tm
    nb = t // seq

    def body(dh_ref, dxr_ref, x_ref, sc_ref, dx_ref, dsc_ref, dsh_ref):
        dhv = dh_ref[...]
        dx_ref[...] = dxr_ref[...] + dhv * (1.0 + sc_ref[...])

        @pl.when(pl.program_id(0) % tpb == 0)
        def _():
            dsc_ref[...] = jnp.zeros_like(dsc_ref)
            dsh_ref[...] = jnp.zeros_like(dsh_ref)

        dsc_ref[...] += _colsum(dhv * x_ref[...])
        dsh_ref[...] += _colsum(dhv)

    return _call(body, "modulate_bwd", (_sds((t, d), F32), _sds((nb, 1, d), F32), _sds((nb, 1, d), F32)), grid=(t // tm,),
                 in_specs=[_tile_spec(tm, d), _tile_spec(tm, d), _tile_spec(tm, d), _row_spec(d, tpb)],
                 out_specs=(_tile_spec(tm, d), _row_spec(d, tpb), _row_spec(d, tpb)))(dh, dxr, x, sc)


def _ffn_in(h, w_in, layer):
    t, d = h.shape
    n = w_in.shape[3]
    half = N_DEV // 2
    tm = min(TMM, t)

    def body(h_ref, wg_ref, wu_ref, g_ref, u_ref, a_ref):
        hv = h_ref[...]
        g = _dot(hv, wg_ref[...])
        u = _dot(hv, wu_ref[...])
        g_ref[...] = g.astype(FFN_KEEP)
        u_ref[...] = u.astype(FFN_KEEP)
        a_ref[...] = (_silu(g) * u).astype(BF16)

    blk = pl.BlockSpec((None, tm, n), lambda p, i: (p, i, 0))
    return _call(body, "ffn_in", (_sds((half, t, n), FFN_KEEP), _sds((half, t, n), FFN_KEEP), _sds((half, t, n), BF16)),
                 grid=(half, t // tm),
                 in_specs=[pl.BlockSpec((tm, d), lambda p, i: (i, 0)),
                           pl.BlockSpec((None, None, d, n), lambda p, i: (layer, p, 0, 0)),
                           pl.BlockSpec((None, None, d, n), lambda p, i: (layer, p + half, 0, 0))],
                 out_specs=(blk, blk, blk), vmem=VMEM_BIG)(h, w_in, w_in)


def _ffn_out(act, w_out, layer):
    half, t, n = act.shape
    d = w_out.shape[2]
    tm = min(TM, t)

    def body(a_ref, w_ref, o_ref):
        acc = _dot(a_ref[0], w_ref[0])
        for p in range(1, half):
            acc = acc + _dot(a_ref[p], w_ref[p])
        o_ref[...] = acc

    return _call(body, "ffn_out", _sds((t, d), F32), grid=(t // tm,),
                 in_specs=[pl.BlockSpec((half, tm, n), lambda i: (0, i, 0)),
                           pl.BlockSpec((half, n, d), lambda i: (layer // half, 0, 0))],
                 out_specs=pl.BlockSpec((tm, d), lambda i: (i, 0)), vmem=VMEM_BIG)(act, w_out)


def _ffn_dact(dy, w_out, hg, hu, layer):
    half, t, n = hg.shape
    d = dy.shape[1]
    tm = min(TMM, t)

    def body(dy_ref, w_ref, g_ref, u_ref, dg_ref, du_ref):
        da = _dot(dy_ref[...], w_ref[...], NT)
        g = g_ref[...].astype(F32)
        sg = jax.nn.sigmoid(g)
        dg_ref[...] = (da * u_ref[...].astype(F32) * (sg * (1.0 + g * (1.0 - sg)))).astype(BF16)
        du_ref[...] = (da * (g * sg)).astype(BF16)

    blk = pl.BlockSpec((None, tm, n), lambda p, i: (p, i, 0))
    dg, du = _call(body, "ffn_dact", (_sds((half, t, n), BF16), _sds((half, t, n), BF16)), grid=(half, t // tm),
                   in_specs=[pl.BlockSpec((tm, d), lambda p, i: (i, 0)),
                             pl.BlockSpec((None, n, d), lambda p, i: (layer + p, 0, 0)), blk, blk],
                   out_specs=(blk, blk), vmem=VMEM_BIG)(dy, w_out, hg, hu)
    return dg, du


def _ffn_bwd_weights(h, dy, act, dg, du):
    half, t, n = act.shape
    d = h.shape[1]
    tk = min(TMM, t)
    h_spec = pl.BlockSpec((tk, d), lambda r: (r, 0))
    g_spec = pl.BlockSpec((half, tk, n), lambda r: (0, r, 0))

    def in_terms(h_ref, g_ref):
        hv = h_ref[...]
        return [_dot(hv, g_ref[p], TN) for p in range(half)]

    def out_terms(a_ref, dy_ref):
        dyv = dy_ref[...]
        return [_dot(a_ref[p], dyv, TN) for p in range(half)]

    w_in_spec = pl.BlockSpec((half, d, n), lambda r: (0, 0, 0))
    dwg = _accumulate_over_tokens("ffn_dw_gate", (h, dg), [h_spec, g_spec], _sds((half, d, n), GRAD_WIRE), w_in_spec,
                                  (half, d, n), t // tk, in_terms)
    dwu = _accumulate_over_tokens("ffn_dw_up", (h, du), [h_spec, g_spec], _sds((half, d, n), GRAD_WIRE), w_in_spec,
                                  (half, d, n), t // tk, in_terms)
    dwo = _accumulate_over_tokens("ffn_dw_out", (act, dy), [g_spec, h_spec], _sds((half, n, d), GRAD_WIRE),
                                  pl.BlockSpec((half, n, d), lambda r: (0, 0, 0)), (half, n, d), t // tk, out_terms)
    return dwg, dwu, dwo


def _ffn_dh(dg, du, w_in, layer):
    half, t, n = dg.shape
    d = w_in.shape[2]
    tm = min(TM, t)

    def body(dg_ref, du_ref, w_ref, o_ref):
        acc = None
        for p in range(half):
            for ref, q in ((dg_ref, p), (du_ref, p + half)):
                term = _dot(ref[p], w_ref[q], NT)
                acc = term if acc is None else acc + term
        o_ref[...] = acc

    g_spec = pl.BlockSpec((half, tm, n), lambda i: (0, i, 0))
    return _call(body, "ffn_dh", _sds((t, d), F32), grid=(t // tm,),
                 in_specs=[g_spec, g_spec, pl.BlockSpec((None, 2 * half, d, n), lambda i: (layer, 0, 0, 0))],
                 out_specs=pl.BlockSpec((tm, d), lambda i: (i, 0)), vmem=VMEM_BIG)(dg, du, w_in)


def _tril(n, strict=False):
    r = lax.broadcasted_iota(jnp.int32, (n, n), 0)
    c = lax.broadcasted_iota(jnp.int32, (n, n), 1)
    return c < r if strict else c <= r


def _gm_spatial_fwd(pre, lng, lnb, w_s, b_st, seq):
    t, w2 = pre.shape
    w = w2 // 2
    gd = w // GM_GROUPS
    tm = min(TM, seq)
    nch = tm // GM_CHUNK

    def body(pre_ref, lng_ref, lnb_ref, ws_ref, bs_ref, y_ref):
        v = _gelu(pre_ref[:, w:])
        vc, rstd = _ln_stats(v)
        vn = (vc * rstd * lng_ref[...] + lnb_ref[...]).astype(BF16)
        keep = _tril(GM_CHUNK)
        for g in range(GM_GROUPS):
            wm = jnp.where(keep, ws_ref[g], 0.0).astype(BF16)
            for ci in range(nch):
                rows = slice(ci * GM_CHUNK, (ci + 1) * GM_CHUNK)
                cols = slice(g * gd, (g + 1) * gd)
                sv = _dot(wm, vn[rows, cols]) + bs_ref[:, g:g + 1]
                u = _gelu(pre_ref[rows, cols])
                y_ref[rows, cols] = (u * sv).astype(BF16)

    return _call(body, "gm_spatial_fwd", _sds((t, w), BF16), grid=(t // tm,),
                 in_specs=[_tile_spec(tm, w2), _vec_spec(w), _vec_spec(w),
                           pl.BlockSpec((GM_GROUPS, GM_CHUNK, GM_CHUNK), lambda i: (0, 0, 0)),
                           pl.BlockSpec((GM_CHUNK, GM_GROUPS), lambda i: (0, 0))],
                 out_specs=_tile_spec(tm, w), vmem=VMEM_BIG)(pre, lng, lnb, w_s, b_st)


def _gm_spatial_bwd(pre, dyv, lng, lnb, w_s, b_st, seq):
    t, w2 = pre.shape
    w = w2 // 2
    gd = w // GM_GROUPS
    tm = min(TM, seq)
    nch = tm // GM_CHUNK

    def body(pre_ref, dyv_ref, lng_ref, lnb_ref, ws_ref, bs_ref, dpre_ref, dws_ref, dbs_ref, dlg_ref, dlb_ref, dbin_ref, dvn_ref):
        @pl.when(pl.program_id(0) == 0)
        def _():
            dws_ref[...] = jnp.zeros_like(dws_ref)
            dbs_ref[...] = jnp.zeros_like(dbs_ref)
            dlg_ref[...] = jnp.zeros_like(dlg_ref)
            dlb_ref[...] = jnp.zeros_like(dlb_ref)
            dbin_ref[...] = jnp.zeros_like(dbin_ref)

        pv = pre_ref[:, w:]
        v = _gelu(pv)
        vc, rstd = _ln_stats(v)
        vhat = vc * rstd
        vn = (vhat * lng_ref[...] + lnb_ref[...]).astype(BF16)
        keep = _tril(GM_CHUNK)
        dbs_cols = []
        for g in range(GM_GROUPS):
            wm = jnp.where(keep, ws_ref[g], 0.0).astype(BF16)
            dwm = jnp.zeros((GM_CHUNK, GM_CHUNK), F32)
            dbs = jnp.zeros((GM_CHUNK, 1), F32)
            for ci in range(nch):
                rows = slice(ci * GM_CHUNK, (ci + 1) * GM_CHUNK)
                cols = slice(g * gd, (g + 1) * gd)
                vn_b = vn[rows, cols]
                sv = _dot(wm, vn_b) + bs_ref[:, g:g + 1]
                pu = pre_ref[rows, cols]
                dy = dyv_ref[rows, cols]
                du = dy * sv
                dsv = dy * _gelu(pu)
                dpu = du * _gelu_grad(pu)
                dpre_ref[rows, cols] = dpu.astype(BF16)
                dbin_ref[:, cols] += _colsum(dpu)
                dsv_b = dsv.astype(BF16)
                dwm = dwm + _dot(dsv_b, vn_b, NT)
                dbs = dbs + jnp.sum(dsv, axis=-1, keepdims=True)
                dvn_ref[rows, cols] = _dot(wm, dsv_b, TN)
            dws_ref[g] += jnp.where(keep, dwm, 0.0)
            dbs_cols.append(dbs)
        dbs_ref[...] += jnp.concatenate(dbs_cols, axis=1)
        dvn = dvn_ref[...]
        dlg_ref[...] += _colsum(dvn * vhat)
        dlb_ref[...] += _colsum(dvn)
        dvh = dvn * lng_ref[...]
        m1 = jnp.mean(dvh, axis=-1, keepdims=True)
        m2 = jnp.mean(dvh * vhat, axis=-1, keepdims=True)
        dv = rstd * (dvh - m1 - vhat * m2)
        dpv = dv * _gelu_grad(pv)
        dpre_ref[:, w:] = dpv.astype(BF16)
        dbin_ref[:, w:] += _colsum(dpv)

    full3 = pl.BlockSpec((GM_GROUPS, GM_CHUNK, GM_CHUNK), lambda i: (0, 0, 0))
    bst = pl.BlockSpec((GM_CHUNK, GM_GROUPS), lambda i: (0, 0))
    return _call(body, "gm_spatial_bwd",
                 (_sds((t, w2), BF16), _sds((GM_GROUPS, GM_CHUNK, GM_CHUNK), F32), _sds((GM_CHUNK, GM_GROUPS), F32),
                  _sds((1, w), F32), _sds((1, w), F32), _sds((1, w2), F32)),
                 grid=(t // tm,),
                 in_specs=[_tile_spec(tm, w2), _tile_spec(tm, w), _vec_spec(w), _vec_spec(w), full3, bst],
                 out_specs=(_tile_spec(tm, w2), full3, bst, _vec_spec(w), _vec_spec(w), _vec_spec(w2)),
                 scratch=[pltpu.VMEM((tm, w), F32)], vmem=VMEM_BIG)(pre, dyv, lng, lnb, w_s, b_st)


def _head_masks():
    lane = lax.broadcasted_iota(jnp.int32, (1, LANES), 1)
    return lane < HEAD_DIM


def _two_heads(x, m0):
    z = jnp.zeros_like(x)
    return jnp.where(m0, x, z), jnp.where(m0, z, x)


def _transposed(pair):
    return tuple(x.astype(F32).T.astype(BF16) for x in pair)


def _store_transposed(dqkv_ref, dk_acc, dv_acc, nq, tq):
    for c in range(nq):
        cols = slice(c * tq, (c + 1) * tq)
        dqkv_ref[1, cols, :] = dk_acc[:, cols].T.astype(BF16)
        dqkv_ref[2, cols, :] = dv_acc[:, cols].T.astype(BF16)


def _qkv_specs(seq, nq, blocked_q):
    if blocked_q:
        q = pl.BlockSpec((None, TQ_(seq), LANES), lambda b, p, i: (0, b * nq + i, p))
        k = pl.BlockSpec((None, seq, LANES), lambda b, p, i: (1, b, p))
        v = pl.BlockSpec((None, seq, LANES), lambda b, p, i: (2, b, p))
    else:
        q = pl.BlockSpec((None, seq, LANES), lambda b, p: (0, b, p))
        k = pl.BlockSpec((None, seq, LANES), lambda b, p: (1, b, p))
        v = pl.BlockSpec((None, seq, LANES), lambda b, p: (2, b, p))
    return q, k, v


def TQ_(seq):
    return min(TQ, seq)


def _fox_gate_fwd(ft, b_f, seq):
    nh, t = ft.shape
    nch = seq // LANES

    def body(ft_ref, bf_ref, fr_ref):
        r = lax.broadcasted_iota(jnp.int32, (LANES, LANES), 0)
        c = lax.broadcasted_iota(jnp.int32, (LANES, LANES), 1)
        upper = jnp.where(r <= c, 1.0, 0.0).astype(BF16)
        carry = jnp.zeros((nh, 1), F32)
        for ci in range(nch):
            cols = slice(ci * LANES, (ci + 1) * LANES)
            lf = _log_sigmoid(ft_ref[:, cols] + bf_ref[...])
            cs = _dot_exact(lf, upper) + carry
            fr_ref[:, cols] = cs
            carry = cs[:, LANES - 1:LANES]

    return _call(body, "fox_gate_fwd", _sds((nh, t), F32), grid=(t // seq,),
                 in_specs=[pl.BlockSpec((nh, seq), lambda b: (0, b)), pl.BlockSpec((nh, 1), lambda b: (0, 0))],
                 out_specs=pl.BlockSpec((nh, seq), lambda b: (0, b)))(ft, b_f)


def _fox_gate_bwd(ft, b_f, dfk, dfq, seq):
    nh, t = ft.shape
    nch = seq // LANES

    def body(ft_ref, bf_ref, dfk_ref, dfq_ref, dl_ref, db_ref):
        @pl.when(pl.program_id(0) == 0)
        def _():
            db_ref[...] = jnp.zeros_like(db_ref)

        r = lax.broadcasted_iota(jnp.int32, (LANES, LANES), 0)
        c = lax.broadcasted_iota(jnp.int32, (LANES, LANES), 1)
        lower = jnp.where(r >= c, 1.0, 0.0).astype(BF16)
        carry = jnp.zeros((nh, 1), F32)
        tot = jnp.zeros((nh, 1), F32)
        for ci in reversed(range(nch)):
            cols = slice(ci * LANES, (ci + 1) * LANES)
            rc = _dot_exact(dfk_ref[:, cols] + dfq_ref[:, cols], lower) + carry
            carry = rc[:, 0:1]
            dl = rc * jax.nn.sigmoid(-(ft_ref[:, cols] + bf_ref[...]))
            dl_ref[:, cols] = dl
            tot = tot + jnp.sum(dl, axis=-1, keepdims=True)
        db_ref[...] += tot

    blk = pl.BlockSpec((nh, seq), lambda b: (0, b))
    one = pl.BlockSpec((nh, 1), lambda b: (0, 0))
    return _call(body, "fox_gate_bwd", (_sds((nh, t), F32), _sds((nh, 1), F32)), grid=(t // seq,),
                 in_specs=[blk, one, blk, blk], out_specs=(blk, one))(ft, b_f, dfk, dfq)


def _sweep(step, n_off, unroll, init, start=0):
    def group(_, st):
        base, carry = st[0], st[1:]
        for u in range(unroll):
            carry = step(base + u, carry, False)
        return (base + unroll, *carry)

    def tail(r):
        def run(st):
            base, carry = st[0], st[1:]
            for u in range(r):
                carry = step(base + u, carry, False)
            return step(base + r, carry, True)
        return run

    def pick(idx, fns, st):
        if len(fns) == 1:
            return fns[0](st)
        half = len(fns) // 2
        return lax.cond(idx < half, lambda s: pick(idx, fns[:half], s), lambda s: pick(idx - half, fns[half:], s), st)

    st = lax.fori_loop(0, n_off // unroll, group, (jnp.int32(0) + start, *init))
    return pick(n_off % unroll, [tail(r) for r in range(unroll)], st)


def _carried(n_comm, n_in, n_out, refs):
    own_in, send = refs[:n_in], refs[n_in:n_in + n_comm]
    rest = refs[n_in + n_comm:]
    n_sem = 3 if n_comm else 0
    return own_in, send, rest[:n_out], rest[n_out:n_out + n_comm], rest[n_out + n_comm:n_out + n_comm + n_sem], rest[n_out + n_comm + n_sem:]


def _fox_fwd(qkv, frow, nb, seq, gather=()):
    _, t, d = qkv.shape
    npair = d // LANES
    tq = TQ_(seq)
    nq = seq // tq
    scale = HEAD_DIM ** -0.5
    n_comm = len(gather)
    total = nb * npair * nq

    def body(*refs):
        (q_ref, k_ref, v_ref, fr_ref), send, (o_ref, lse_ref), land, sems, _ = _carried(n_comm, 4, 2, refs)
        at = (pl.program_id(0) * npair + pl.program_id(1)) * nq + pl.program_id(2)
        if n_comm:
            for stage, when in ((0, 0), (1, (3 * total) // 4)):
                @pl.when(at == when)
                def _(stage=stage):
                    _gather_stage(stage, send, land, *sems)

        m0 = _head_masks()
        qm = _two_heads(q_ref[...] * scale, m0)
        row = lax.broadcasted_iota(jnp.int32, (tq, tq), 0)
        col = lax.broadcasted_iota(jnp.int32, (tq, tq), 1)
        one = jnp.ones((tq, LANES), BF16)

        def step(j, carry, diag):
            off = pl.multiple_of(j * tq, tq)
            kb = k_ref[pl.ds(off, tq), :]
            vb = v_ref[pl.ds(off, tq), :]
            vv = (jnp.where(m0, vb, one), jnp.where(m0, one, vb))
            out = []
            for hh in range(2):
                m, acc = carry[2 * hh], carry[2 * hh + 1]
                s = lax.dot_general(qm[hh], kb, NT, preferred_element_type=F32) - fr_ref[hh:hh + 1, pl.ds(off, tq)]
                if diag:
                    s = jnp.where(col <= row, s, NEG_INF)
                mn = jnp.maximum(m, jnp.max(s, axis=-1, keepdims=True))
                p = jnp.exp(s - mn)
                out += [mn, jnp.exp(m - mn) * acc + jnp.dot(p.astype(BF16), vv[hh], preferred_element_type=F32)]
            return tuple(out)

        neg = jnp.full((tq, 1), NEG_INF, F32)
        zacc = jnp.zeros((tq, LANES), F32)
        m_a, acc_a, m_b, acc_b = _sweep(step, pl.program_id(2), FOX_FWD_UNROLL, (neg, zacc, neg, zacc))
        l_a = pltpu.roll(acc_a, HEAD_DIM, 1)
        l_b = pltpu.roll(acc_b, HEAD_DIM, 1)
        o_ref[...] = jnp.where(m0, acc_a / l_a, acc_b / l_b)
        lse_ref[:, 0:1] = m_a + jnp.log(l_a[:, 0:1])
        lse_ref[:, 1:2] = m_b + jnp.log(l_b[:, HEAD_DIM:HEAD_DIM + 1])
        if n_comm:
            @pl.when(at == total - 1)
            def _():
                _gather_stage(2, send, land, *sems)

    q_spec, k_spec, v_spec = _qkv_specs(seq, nq, True)
    col_spec = pl.BlockSpec((None, tq, 2), lambda b, p, i: (p, b * nq + i, 0))
    hbm = pl.BlockSpec(memory_space=pl.ANY)
    return _call(body, "fox_fwd", (_sds((t, d), F32), _sds((npair, t, 2), F32), *_gathered_shapes(gather)), grid=(nb, npair, nq),
                 in_specs=[q_spec, k_spec, v_spec, pl.BlockSpec((None, 2, seq), lambda b, p, i: (p, 0, b))] + [hbm] * n_comm,
                 out_specs=(pl.BlockSpec((tq, LANES), lambda b, p, i: (b * nq + i, p)), col_spec, *[hbm] * n_comm),
                 scratch=_comm_scratch(n_comm) if n_comm else (), vmem=VMEM_BIG)(qkv, qkv, qkv, frow, *gather)


def _scatter_beside(stage, n_comm, nb, npair, send, land, sems):
    if n_comm:
        at = pl.program_id(0) * npair + pl.program_id(1)

        @pl.when(at == (0 if stage == 0 else nb * npair - 1))
        def _():
            _scatter_stage(stage, send, land, *sems)


def _fox_bwd(qkv, frow, o, do, lse, nb, seq, scatter=()):
    _, t, d = qkv.shape
    npair = d // LANES
    tq = TQ_(seq)
    nq = seq // tq
    scale = HEAD_DIM ** -0.5
    n_comm = len(scatter)

    def body(*refs):
        own_in, send, (dqkv_ref, df_ref, dfq_ref), land, sems, (dk_acc, dv_acc) = _carried(n_comm, 7, 3, refs)
        q_ref, k_ref, v_ref, fr_ref, o_ref, do_ref, lse_ref = own_in
        _scatter_beside(0, n_comm, nb, npair, send, land, sems)
        m0 = _head_masks()
        row = lax.broadcasted_iota(jnp.int32, (tq, tq), 0)
        col = lax.broadcasted_iota(jnp.int32, (tq, tq), 1)
        dk_acc[...] = jnp.zeros_like(dk_acc)
        dv_acc[...] = jnp.zeros_like(dv_acc)
        df_ref[...] = jnp.zeros_like(df_ref)

        def q_block(i, _):
            qoff = pl.multiple_of(i * tq, tq)
            qrows = pl.ds(qoff, tq)
            qm = _two_heads(q_ref[qrows, :] * scale, m0)
            dov = do_ref[qrows, :]
            dd = dov * o_ref[qrows, :]
            dm = _two_heads(dov.astype(BF16), m0)
            qmt = _transposed(qm)
            dmt = _transposed(dm)
            delta = (jnp.sum(jnp.where(m0, dd, 0.0), axis=-1, keepdims=True),
                     jnp.sum(jnp.where(m0, 0.0, dd), axis=-1, keepdims=True))
            ls = (lse_ref[qrows, 0:1], lse_ref[qrows, 1:2])

            def step(j, carry, diag):
                off = pl.multiple_of(j * tq, tq)
                krows = pl.ds(off, tq)
                kb = k_ref[krows, :]
                vb = v_ref[krows, :]
                dqs, rowsums = [], []
                dk = jnp.zeros((LANES, tq), F32)
                dv = jnp.zeros((LANES, tq), F32)
                for hh in range(2):
                    s = lax.dot_general(qm[hh], kb, NT, preferred_element_type=F32) - fr_ref[hh:hh + 1, krows]
                    if diag:
                        s = jnp.where(col <= row, s, NEG_INF)
                    p = jnp.exp(s - ls[hh])
                    dp = lax.dot_general(dm[hh], vb, NT, preferred_element_type=F32)
                    ds = p * (dp - delta[hh])
                    df_ref[hh:hh + 1, krows] -= _colsum(ds)
                    rowsums.append(carry[1 + hh] + jnp.sum(ds, axis=-1, keepdims=True))
                    ds_b = ds.astype(BF16)
                    dqs.append(jnp.dot(ds_b, kb, preferred_element_type=F32))
                    dk = dk + jnp.dot(qmt[hh], ds_b, preferred_element_type=F32)
                    dv = dv + jnp.dot(dmt[hh], p.astype(BF16), preferred_element_type=F32)
                dk_acc[:, krows] += dk
                dv_acc[:, krows] += dv
                return (carry[0] + jnp.where(m0, dqs[0], dqs[1]), *rowsums)

            zero = jnp.zeros((tq, 1), F32)
            dq, rs_a, rs_b = _sweep(step, i, FOX_BWD_UNROLL, (jnp.zeros((tq, LANES), F32), zero, zero))
            dqkv_ref[0, qrows, :] = (dq * scale).astype(BF16)
            dfq_ref[qrows, 0:1] = rs_a
            dfq_ref[qrows, 1:2] = rs_b
            return 0

        lax.fori_loop(0, nq, q_block, 0)
        _store_transposed(dqkv_ref, dk_acc, dv_acc, nq, tq)
        _scatter_beside(1, n_comm, nb, npair, send, land, sems)

    q_spec, k_spec, v_spec = _qkv_specs(seq, nq, False)
    col_spec = pl.BlockSpec((None, seq, 2), lambda b, p: (p, b, 0))
    row_spec = pl.BlockSpec((None, 2, seq), lambda b, p: (p, 0, b))
    tile = pl.BlockSpec((seq, LANES), lambda b, p: (b, p))
    hbm = pl.BlockSpec(memory_space=pl.ANY)
    return _call(body, "fox_bwd", (_sds((3, t, d), BF16), _sds((npair, 2, t), F32), _sds((npair, t, 2), F32), *_scattered_shapes(scatter)),
                 grid=(nb, npair),
                 in_specs=[q_spec, k_spec, v_spec, row_spec, tile, tile, col_spec] + [hbm] * n_comm,
                 out_specs=(pl.BlockSpec((3, seq, LANES), lambda b, p: (0, b, p)), row_spec, col_spec, *[hbm] * n_comm),
                 scratch=(_comm_scratch(n_comm) if n_comm else []) + [pltpu.VMEM((LANES, seq), F32), pltpu.VMEM((LANES, seq), F32)],
                 vmem=VMEM_BIG)(qkv, qkv, qkv, frow, o, do, lse, *scatter)


def _split2(x):
    hi = x.astype(BF16)
    return hi, (x - hi.astype(F32)).astype(BF16)


def _sum_right(x, tri):
    hi, lo = _split2(x)
    return jnp.dot(hi, tri, preferred_element_type=F32) + jnp.dot(lo, tri, preferred_element_type=F32)


def _sb_scores(qm_h, kb, mask):
    z = lax.dot_general(qm_h, kb, NT, preferred_element_type=F32)
    lb = _log_sigmoid(z)
    l1m = lb - z
    if mask is not None:
        l1m = jnp.where(mask, l1m, 0.0)
    return lb, l1m


def _sb_fwd(qkv, nb, seq):
    _, t, d = qkv.shape
    npair = d // LANES
    tq = TQ_(seq)
    nq = seq // tq
    scale = HEAD_DIM ** -0.5

    def body(q_ref, k_ref, v_ref, o_ref, lt_ref):
        i = pl.program_id(2)
        m0 = _head_masks()
        qm = _two_heads(q_ref[...] * scale, m0)
        row = lax.broadcasted_iota(jnp.int32, (tq, tq), 0)
        col = lax.broadcasted_iota(jnp.int32, (tq, tq), 1)
        after = jnp.where(row > col, 1.0, 0.0).astype(BF16)

        def step(j, carry, diag):
            off = pl.multiple_of(j * tq, tq)
            kb = k_ref[pl.ds(off, tq), :]
            vb = v_ref[pl.ds(off, tq), :]
            mask = (col < row) if diag else None
            nxt, parts = [], []
            for hh in range(2):
                lb, l1m = _sb_scores(qm[hh], kb, mask)
                rest = _sum_right(l1m, after) + carry[hh]
                a = jnp.exp(lb + rest)
                if diag:
                    a = jnp.where(mask, a, 0.0)
                parts.append(jnp.dot(a.astype(BF16), vb, preferred_element_type=F32))
                nxt.append(carry[hh] + jnp.sum(l1m, axis=-1, keepdims=True))
            return (*nxt, carry[2] + jnp.where(m0, parts[0], parts[1]))

        zero = jnp.zeros((tq, 1), F32)
        init = (zero, zero, jnp.zeros((tq, LANES), F32))
        carry = lax.cond(i > 0, lambda c: step(i - 1, step(i, c, True), False), lambda c: step(i, c, True), init)

        def alive(st):
            return (st[0] < i) & (jnp.max(jnp.maximum(st[1], st[2])) > SB_DEAD)

        def more(st):
            return (st[0] + 1, *step(i - 1 - st[0], st[1:], False))

        done, lt_a, lt_b, acc = lax.while_loop(alive, more, (jnp.minimum(i, 1), *carry))
        o_ref[...] = acc
        lt_ref[:, 0:1] = lt_a
        lt_ref[:, 1:2] = lt_b
        lt_ref[:, 2:3] = jnp.zeros((tq, 1), F32) + done.astype(F32)
        lt_ref[:, 3:4] = zero

    q_spec, k_spec, v_spec = _qkv_specs(seq, nq, True)
    return _call(body, "sb_fwd", (_sds((t, d), F32), _sds((npair, t, 4), F32)), grid=(nb, npair, nq),
                 in_specs=[q_spec, k_spec, v_spec],
                 out_specs=(pl.BlockSpec((tq, LANES), lambda b, p, i: (b * nq + i, p)),
                            pl.BlockSpec((None, tq, 4), lambda b, p, i: (p, b * nq + i, 0))), vmem=VMEM_BIG)(qkv, qkv, qkv)


def _sb_bwd(qkv, do, ltot, nb, seq, scatter=()):
    _, t, d = qkv.shape
    npair = d // LANES
    tq = TQ_(seq)
    nq = seq // tq
    scale = HEAD_DIM ** -0.5
    n_comm = len(scatter)

    def body(*refs):
        (q_ref, k_ref, v_ref, do_ref, lt_ref), send, (dqkv_ref,), land, sems, (dk_acc, dv_acc) = _carried(n_comm, 5, 1, refs)
        _scatter_beside(0, n_comm, nb, npair, send, land, sems)
        m0 = _head_masks()
        row = lax.broadcasted_iota(jnp.int32, (tq, tq), 0)
        col = lax.broadcasted_iota(jnp.int32, (tq, tq), 1)
        upto = jnp.where(row <= col, 1.0, 0.0).astype(BF16)
        left_of = jnp.where(row < col, 1.0, 0.0).astype(BF16)
        dk_acc[...] = jnp.zeros_like(dk_acc)
        dv_acc[...] = jnp.zeros_like(dv_acc)

        def q_block(i, _):
            qoff = pl.multiple_of(i * tq, tq)
            qrows = pl.ds(qoff, tq)
            qm = _two_heads(q_ref[qrows, :] * scale, m0)
            dm = _two_heads(do_ref[qrows, :].astype(BF16), m0)
            qmt = _transposed(qm)
            dmt = _transposed(dm)
            ltot = (lt_ref[qrows, 0:1], lt_ref[qrows, 1:2])

            def step(j, carry, diag):
                off = pl.multiple_of(j * tq, tq)
                krows = pl.ds(off, tq)
                kb = k_ref[krows, :]
                vb = v_ref[krows, :]
                mask = (col < row) if diag else None
                nxt, dqs = [], []
                dk = jnp.zeros((LANES, tq), F32)
                dv = jnp.zeros((LANES, tq), F32)
                for hh in range(2):
                    cl, ce = carry[2 * hh], carry[2 * hh + 1]
                    lb, l1m = _sb_scores(qm[hh], kb, mask)
                    a = jnp.exp(lb + (ltot[hh] - (_sum_right(l1m, upto) + cl)))
                    if diag:
                        a = jnp.where(mask, a, 0.0)
                    e = lax.dot_general(dm[hh], vb, NT, preferred_element_type=F32) * a
                    before = _sum_right(e, left_of) + ce
                    beta = jnp.exp(lb)
                    dz = e * (1.0 - beta) - before * beta
                    if diag:
                        dz = jnp.where(mask, dz, 0.0)
                    dz_b = dz.astype(BF16)
                    dqs.append(jnp.dot(dz_b, kb, preferred_element_type=F32))
                    dk = dk + jnp.dot(qmt[hh], dz_b, preferred_element_type=F32)
                    dv = dv + jnp.dot(dmt[hh], a.astype(BF16), preferred_element_type=F32)
                    nxt += [cl + jnp.sum(l1m, axis=-1, keepdims=True), ce + jnp.sum(e, axis=-1, keepdims=True)]
                dk_acc[:, krows] += dk
                dv_acc[:, krows] += dv
                return (*nxt, carry[4] + jnp.where(m0, dqs[0], dqs[1]))

            zero = jnp.zeros((tq, 1), F32)
            visited = jnp.max(lt_ref[qrows, 2:3]).astype(jnp.int32)
            carry = _sweep(step, visited, SB_BWD_UNROLL, (zero, zero, zero, zero, jnp.zeros((tq, LANES), F32)), start=i - visited)
            dqkv_ref[0, qrows, :] = (carry[4] * scale).astype(BF16)
            return 0

        lax.fori_loop(0, nq, q_block, 0)
        _store_transposed(dqkv_ref, dk_acc, dv_acc, nq, tq)
        _scatter_beside(1, n_comm, nb, npair, send, land, sems)

    q_spec, k_spec, v_spec = _qkv_specs(seq, nq, False)
    tile = pl.BlockSpec((seq, LANES), lambda b, p: (b, p))
    hbm = pl.BlockSpec(memory_space=pl.ANY)
    return _call(body, "sb_bwd", (_sds((3, t, d), BF16), *_scattered_shapes(scatter)), grid=(nb, npair),
                 in_specs=[q_spec, k_spec, v_spec, tile, pl.BlockSpec((None, seq, 4), lambda b, p: (p, b, 0))] + [hbm] * n_comm,
                 out_specs=(pl.BlockSpec((3, seq, LANES), lambda b, p: (0, b, p)), *[hbm] * n_comm),
                 scratch=(_comm_scratch(n_comm) if n_comm else []) + [pltpu.VMEM((LANES, seq), F32), pltpu.VMEM((LANES, seq), F32)],
                 vmem=VMEM_BIG)(qkv, qkv, qkv, do, ltot, *scatter)


def _qkv_proj(name, h, w):
    t, d = h.shape
    tm = min(TMM, t)
    return _mm(name, h, w, _sds((3, t, d), BF16), (3, t // tm, 1),
               pl.BlockSpec((tm, d), lambda s, i, r: (i, 0)), pl.BlockSpec((d, d), lambda s, i, r: (0, s)),
               pl.BlockSpec((None, tm, d), lambda s, i, r: (s, i, 0)), NN)


def _qkv_dw(name, h, dqkv):
    t, d = h.shape
    tk = min(TMM, t)

    def terms(h_ref, g_ref):
        hv = h_ref[...]
        return [_dot(hv, g_ref[s], TN) for s in range(3)]

    def store(o_ref, acc_ref):
        for s in range(3):
            o_ref[:, s * d:(s + 1) * d] = acc_ref[s].astype(o_ref.dtype)

    return _accumulate_over_tokens(name, (h, dqkv), [pl.BlockSpec((tk, d), lambda r: (r, 0)), pl.BlockSpec((3, tk, d), lambda r: (0, r, 0))],
                                   _sds((d, 3 * d), GRAD_WIRE), pl.BlockSpec((d, 3 * d), lambda r: (0, 0)), (3, d, d), t // tk, terms, store)


def _qkv_dh(name, dqkv, w):
    _, t, d = dqkv.shape
    tm = min(TM, t)

    def body(g_ref, w_ref, o_ref):
        acc = None
        for s in range(3):
            term = _dot(g_ref[s], w_ref[:, s * d:(s + 1) * d], NT)
            acc = term if acc is None else acc + term
        o_ref[...] = acc

    return _call(body, name, _sds((t, d), F32), grid=(t // tm,),
                 in_specs=[pl.BlockSpec((3, tm, d), lambda i: (0, i, 0)), pl.BlockSpec((d, 3 * d), lambda i: (0, 0))],
                 out_specs=pl.BlockSpec((tm, d), lambda i: (i, 0)), vmem=VMEM_BIG)(dqkv, w)


def _glu(pre_block, d):
    return pre_block[:, :d] * jax.nn.sigmoid(pre_block[:, d:])


def _shifted_copies(ext_ref, sh_ref, tt):
    for r in range(1, SUBLANES):
        sh_ref[r - 1] = ext_ref[pl.ds(r, tt + CONV_HALO - SUBLANES), :]


def _rows_from(ext_ref, sh_ref, base, offset, n):
    q, r = divmod(offset, SUBLANES)
    if r == 0:
        return ext_ref[pl.ds(pl.multiple_of(base + offset, SUBLANES), n), :]
    return sh_ref[r - 1, pl.ds(pl.multiple_of(base + q * SUBLANES, SUBLANES), n), :]


def _cv_conv_fwd(pre, dw, dwb, lng, lnb, seq):
    t, d2 = pre.shape
    d = d2 // 2
    tt = min(TT, seq)
    nt = seq // tt
    hb = tt // CONV_HALO

    def body(pre_ref, halo_ref, dw_ref, dwb_ref, lng_ref, lnb_ref, y1_ref, y2_ref, ext_ref, sh_ref):
        i = pl.program_id(1)
        ext_ref[0:CONV_HALO, :] = jnp.where(i == 0, 0.0, _glu(halo_ref[...], d))
        ext_ref[CONV_HALO:, :] = _glu(pre_ref[...], d)
        _shifted_copies(ext_ref, sh_ref, tt)

        acc = jnp.zeros((tt, d), F32) + dwb_ref[...]
        for k in range(CONV_WIDTH):
            acc = acc + _rows_from(ext_ref, sh_ref, 0, CONV_HALO - (CONV_WIDTH - 1) + k, tt) * dw_ref[k:k + 1, :]
        y1_ref[...] = acc
        yc, rstd = _ln_stats(acc)
        y2_ref[...] = _silu(yc * rstd * lng_ref[...] + lnb_ref[...]).astype(BF16)

    vec = pl.BlockSpec((1, d), lambda b, i: (0, 0))
    tile = pl.BlockSpec((tt, d), lambda b, i: (b * nt + i, 0))
    return _call(body, "cv_conv_fwd", (_sds((t, d), F32), _sds((t, d), BF16)), grid=(t // seq, nt),
                 in_specs=[pl.BlockSpec((tt, d2), lambda b, i: (b * nt + i, 0)),
                           pl.BlockSpec((CONV_HALO, d2), lambda b, i: (jnp.maximum((b * nt + i) * hb - 1, 0), 0)),
                           pl.BlockSpec((CONV_HALO, d), lambda b, i: (0, 0)), vec, vec, vec],
                 out_specs=(tile, tile),
                 scratch=[pltpu.VMEM((tt + CONV_HALO, d), F32), pltpu.VMEM((SUBLANES - 1, tt + CONV_HALO - SUBLANES, d), F32)],
                 vmem=VMEM_BIG)(pre, pre, dw, dwb, lng, lnb)


def _cv_norm_bwd(y1, dy2, lng, lnb):
    t, d = y1.shape
    tm = min(TM, t)

    def body(y1_ref, dy2_ref, lng_ref, lnb_ref, dy1_ref, dlg_ref, dlb_ref, dsum_ref):
        @pl.when(pl.program_id(0) == 0)
        def _():
            dlg_ref[...] = jnp.zeros_like(dlg_ref)
            dlb_ref[...] = jnp.zeros_like(dlb_ref)
            dsum_ref[...] = jnp.zeros_like(dsum_ref)

        yc, rstd = _ln_stats(y1_ref[...])
        yhat = yc * rstd
        n = yhat * lng_ref[...] + lnb_ref[...]
        sg = jax.nn.sigmoid(n)
        dn = dy2_ref[...] * (sg * (1.0 + n * (1.0 - sg)))
        dlg_ref[...] += _colsum(dn * yhat)
        dlb_ref[...] += _colsum(dn)
        dyh = dn * lng_ref[...]
        m1 = jnp.mean(dyh, axis=-1, keepdims=True)
        m2 = jnp.mean(dyh * yhat, axis=-1, keepdims=True)
        dy1 = rstd * (dyh - m1 - yhat * m2)
        dy1_ref[...] = dy1
        dsum_ref[...] += _colsum(dy1)

    return _call(body, "cv_norm_bwd", (_sds((t, d), F32), _sds((1, d), F32), _sds((1, d), F32), _sds((1, d), F32)),
                 grid=(t // tm,), in_specs=[_tile_spec(tm, d), _tile_spec(tm, d), _vec_spec(d), _vec_spec(d)],
                 out_specs=(_tile_spec(tm, d), _vec_spec(d), _vec_spec(d), _vec_spec(d)))(y1, dy2, lng, lnb)


def _cv_conv_bwd(pre, dy1, dw, seq):
    t, d2 = pre.shape
    d = d2 // 2
    tt = min(TT // 2, seq)
    nt = seq // tt
    hb = tt // CONV_HALO
    last_halo = t // CONV_HALO - 1

    def body(pre_ref, halo_ref, dy_ref, dyn_ref, dw_ref, dpre_ref, ddw_ref, dbin_ref, ext_ref, dext_ref, sh_ref, dsh_ref):
        b, i = pl.program_id(0), pl.program_id(1)

        @pl.when((b == 0) & (i == 0))
        def _():
            ddw_ref[...] = jnp.zeros_like(ddw_ref)
            dbin_ref[...] = jnp.zeros_like(dbin_ref)

        pv = pre_ref[...]
        ext_ref[0:CONV_HALO, :] = jnp.where(i == 0, 0.0, _glu(halo_ref[...], d))
        ext_ref[CONV_HALO:, :] = _glu(pv, d)
        dyv = dy_ref[...]
        dext_ref[0:tt, :] = dyv
        dext_ref[tt:, :] = jnp.where(i == nt - 1, 0.0, dyn_ref[...])
        _shifted_copies(ext_ref, sh_ref, tt)
        _shifted_copies(dext_ref, dsh_ref, tt)
        nrows = tt // CONV_ROWS

        def input_grad(c, _):
            r0 = pl.multiple_of(c * CONV_ROWS, CONV_ROWS)
            dy0 = jnp.zeros((CONV_ROWS, d), F32)
            for k in range(CONV_WIDTH):
                dy0 = dy0 + _rows_from(dext_ref, dsh_ref, r0, CONV_WIDTH - 1 - k, CONV_ROWS) * dw_ref[k:k + 1, :]
            rows = pl.ds(r0, CONV_ROWS)
            a = pre_ref[rows, :d]
            sg = jax.nn.sigmoid(pre_ref[rows, d:])
            da = dy0 * sg
            dg = dy0 * a * sg * (1.0 - sg)
            dpre_ref[rows, :d] = da.astype(BF16)
            dpre_ref[rows, d:] = dg.astype(BF16)
            dbin_ref[:, :d] += _colsum(da)
            dbin_ref[:, d:] += _colsum(dg)
            return 0

        lax.fori_loop(0, nrows, input_grad, 0)

        for k in range(CONV_WIDTH):
            def tap_grad(c, part, k=k):
                r0 = pl.multiple_of(c * CONV_ROWS, CONV_ROWS)
                prod = dy_ref[pl.ds(r0, CONV_ROWS), :] * _rows_from(ext_ref, sh_ref, r0, CONV_HALO - (CONV_WIDTH - 1) + k, CONV_ROWS)
                return part + jnp.sum(prod.reshape(CONV_ROWS // SUBLANES, SUBLANES, d), axis=0)

            ddw_ref[k:k + 1, :] += _colsum(lax.fori_loop(0, nrows, tap_grad, jnp.zeros((SUBLANES, d), F32)))

    return _call(body, "cv_conv_bwd", (_sds((t, d2), BF16), _sds((CONV_HALO, d), F32), _sds((1, d2), F32)), grid=(t // seq, nt),
                 in_specs=[pl.BlockSpec((tt, d2), lambda b, i: (b * nt + i, 0)),
                           pl.BlockSpec((CONV_HALO, d2), lambda b, i: (jnp.maximum((b * nt + i) * hb - 1, 0), 0)),
                           pl.BlockSpec((tt, d), lambda b, i: (b * nt + i, 0)),
                           pl.BlockSpec((CONV_HALO, d), lambda b, i: (jnp.minimum((b * nt + i + 1) * hb, last_halo), 0)),
                           pl.BlockSpec((CONV_HALO, d), lambda b, i: (0, 0))],
                 out_specs=(pl.BlockSpec((tt, d2), lambda b, i: (b * nt + i, 0)),
                            pl.BlockSpec((CONV_HALO, d), lambda b, i: (0, 0)), pl.BlockSpec((1, d2), lambda b, i: (0, 0))),
                 scratch=[pltpu.VMEM((tt + CONV_HALO, d), F32), pltpu.VMEM((tt + CONV_HALO, d), F32),
                          pltpu.VMEM((SUBLANES - 1, tt + CONV_HALO - SUBLANES, d), F32),
                          pltpu.VMEM((SUBLANES - 1, tt + CONV_HALO - SUBLANES, d), F32)],
                 vmem=VMEM_BIG)(pre, pre, dy1, dy1, dw)


def _adamw(name, w, m, v, g=None, parts=None):
    rows, cols = w.shape
    tr = rows
    for cand in ((512,) if parts is None else ()) + (256, 128, 64, 32, 16, 8):
        if rows % cand == 0 and rows > cand:
            tr = cand
            break
    bc1 = 1.0 - ADAM_B1 ** ADAM_STEP
    bc2 = 1.0 - ADAM_B2 ** ADAM_STEP

    def body(w_ref, m_ref, v_ref, g_ref, go_ref, d_ref, mo_ref, vo_ref):
        if parts is None:
            gv = g_ref[...]
        else:
            gv = g_ref[0].astype(F32)
            for s in range(1, N_DEV):
                gv = gv + g_ref[s].astype(F32)
        mn = ADAM_B1 * m_ref[...] + (1.0 - ADAM_B1) * gv
        vn = ADAM_B2 * v_ref[...] + (1.0 - ADAM_B2) * (gv * gv)
        m_hat = mn / bc1
        v_hat = vn / bc2
        go_ref[...] = gv
        d_ref[...] = -ADAM_LR * (m_hat / (jnp.sqrt(v_hat) + ADAM_EPS) + ADAM_WD * w_ref[...])
        mo_ref[...] = mn
        vo_ref[...] = vn

    blk = pl.BlockSpec((tr, cols), lambda i: (i, 0))
    g_in, g_spec = (g, blk) if parts is None else (parts, pl.BlockSpec((N_DEV, tr, cols), lambda i: (0, i, 0)))
    out = _sds((rows, cols), F32)
    return _call(body, name, (out, out, out, out), grid=(rows // tr,), in_specs=[blk, blk, blk, g_spec],
                 out_specs=(blk, blk, blk, blk), vmem=VMEM_BIG)(w, m, v, g_in)


def _pad_rows(a, rows):
    return jnp.pad(a, ((0, rows - a.shape[0]), (0, 0)))


def _full_cols(gathered, n):
    k = gathered.shape[2]
    return jnp.transpose(gathered[0], (1, 0, 2)).reshape(k, N_DEV * n)


def _col_blocks(full, n):
    k = full.shape[0]
    return jnp.transpose(full.reshape(k, N_DEV, n), (1, 0, 2))[None]


def kernel(x, c, mod_w, mod_b, ln1_g, ln1_b, ln2_g, ln2_b, ffn_w_in, ffn_w_out, gm_w_in, gm_b_in, gm_ln_g, gm_ln_b, gm_w_s, gm_b_s, gm_w_out, fox_w_in, fox_b_f, fox_w_out, sb_w_in, sb_w_out, cv_w_in, cv_b_in, cv_dw, cv_dw_b, cv_ln_g, cv_ln_b, cv_w_out, cv_b_out, loss_target, m_mod_w, m_mod_b, m_ln1_g, m_ln1_b, m_ln2_g, m_ln2_b, m_ffn_w_in, m_ffn_w_out, m_gm_w_in, m_gm_b_in, m_gm_ln_g, m_gm_ln_b, m_gm_w_s, m_gm_b_s, m_gm_w_out, m_fox_w_in, m_fox_b_f, m_fox_w_out, m_sb_w_in, m_sb_w_out, m_cv_w_in, m_cv_b_in, m_cv_dw, m_cv_dw_b, m_cv_ln_g, m_cv_ln_b, m_cv_w_out, m_cv_b_out, v_mod_w, v_mod_b, v_ln1_g, v_ln1_b, v_ln2_g, v_ln2_b, v_ffn_w_in, v_ffn_w_out, v_gm_w_in, v_gm_b_in, v_gm_ln_g, v_gm_ln_b, v_gm_w_s, v_gm_b_s, v_gm_w_out, v_fox_w_in, v_fox_b_f, v_fox_w_out, v_sb_w_in, v_sb_w_out, v_cv_w_in, v_cv_b_in, v_cv_dw, v_cv_dw_b, v_cv_ln_g, v_cv_ln_b, v_cv_w_out, v_cv_b_out):
    weights = dict(mod_w=mod_w, mod_b=mod_b, ln1_g=ln1_g, ln1_b=ln1_b, ln2_g=ln2_g, ln2_b=ln2_b, ffn_w_in=ffn_w_in, ffn_w_out=ffn_w_out, gm_w_in=gm_w_in, gm_b_in=gm_b_in, gm_ln_g=gm_ln_g, gm_ln_b=gm_ln_b, gm_w_s=gm_w_s, gm_b_s=gm_b_s, gm_w_out=gm_w_out, fox_w_in=fox_w_in, fox_b_f=fox_b_f, fox_w_out=fox_w_out, sb_w_in=sb_w_in, sb_w_out=sb_w_out, cv_w_in=cv_w_in, cv_b_in=cv_b_in, cv_dw=cv_dw, cv_dw_b=cv_dw_b, cv_ln_g=cv_ln_g, cv_ln_b=cv_ln_b, cv_w_out=cv_w_out, cv_b_out=cv_b_out)
    mom1 = dict(mod_w=m_mod_w, mod_b=m_mod_b, ln1_g=m_ln1_g, ln1_b=m_ln1_b, ln2_g=m_ln2_g, ln2_b=m_ln2_b, ffn_w_in=m_ffn_w_in, ffn_w_out=m_ffn_w_out, gm_w_in=m_gm_w_in, gm_b_in=m_gm_b_in, gm_ln_g=m_gm_ln_g, gm_ln_b=m_gm_ln_b, gm_w_s=m_gm_w_s, gm_b_s=m_gm_b_s, gm_w_out=m_gm_w_out, fox_w_in=m_fox_w_in, fox_b_f=m_fox_b_f, fox_w_out=m_fox_w_out, sb_w_in=m_sb_w_in, sb_w_out=m_sb_w_out, cv_w_in=m_cv_w_in, cv_b_in=m_cv_b_in, cv_dw=m_cv_dw, cv_dw_b=m_cv_dw_b, cv_ln_g=m_cv_ln_g, cv_ln_b=m_cv_ln_b, cv_w_out=m_cv_w_out, cv_b_out=m_cv_b_out)
    mom2 = dict(mod_w=v_mod_w, mod_b=v_mod_b, ln1_g=v_ln1_g, ln1_b=v_ln1_b, ln2_g=v_ln2_g, ln2_b=v_ln2_b, ffn_w_in=v_ffn_w_in, ffn_w_out=v_ffn_w_out, gm_w_in=v_gm_w_in, gm_b_in=v_gm_b_in, gm_ln_g=v_gm_ln_g, gm_ln_b=v_gm_ln_b, gm_w_s=v_gm_w_s, gm_b_s=v_gm_b_s, gm_w_out=v_gm_w_out, fox_w_in=v_fox_w_in, fox_b_f=v_fox_b_f, fox_w_out=v_fox_w_out, sb_w_in=v_sb_w_in, sb_w_out=v_sb_w_out, cv_w_in=v_cv_w_in, cv_b_in=v_cv_b_in, cv_dw=v_cv_dw, cv_dw_b=v_cv_dw_b, cv_ln_g=v_cv_ln_g, cv_ln_b=v_cv_ln_b, cv_w_out=v_cv_w_out, cv_b_out=v_cv_b_out)
    names = list(weights)

    nb, seq, d = x.shape
    t = nb * seq
    nl = mod_w.shape[0]
    alpha = (2.0 * nl) ** 0.25
    me = 4 * lax.axis_index("x") + 2 * lax.axis_index("y") + lax.axis_index("c")
    xs = x.reshape(t, d)
    tgt = loss_target.reshape(t, d)
    n_mod = mod_w.shape[2]
    n_ffn = ffn_w_in.shape[2]
    n_heads = d // HEAD_DIM
    npair = d // LANES

    c_all = _exchange_small(_pad_rows(c, 8), "gather_c", False)[:, :nb].reshape(N_DEV * nb, d)
    mod_b_loc = lax.dynamic_slice_in_dim(mod_b, me * n_mod, n_mod, axis=1)[:, None, :]
    mod_loc = _mod_fwd(c_all, mod_w, mod_b_loc)
    mod_g = _exchange_small(mod_loc.reshape(nl * N_DEV * nb, n_mod), "gather_mod", False)
    mod_all = jnp.transpose(mod_g.reshape(N_DEV, nl, N_DEV * nb, n_mod), (1, 2, 0, 3)).reshape(nl, N_DEV * nb, N_DEV * n_mod)
    mod_me = lax.dynamic_slice_in_dim(mod_all, me * nb, nb, axis=1)
    mods = [[mod_me[l, :, k * d:(k + 1) * d][:, None, :] for k in range(6)] for l in range(nl)]

    assert nl == 4, "the exchange schedule below is written for the four-layer trunk"
    big = ["ffn_w_in", "ffn_w_out", "gm_w_in", "gm_w_out", "fox_w_in", "fox_w_out", "sb_w_in", "sb_w_out", "cv_w_in", "cv_w_out"]
    shard = {n: weights[n].astype(BF16) for n in big if not n.startswith("ffn")}
    for l in range(nl):
        shard["ffn_w_in", l] = ffn_w_in[l:l + 1].astype(BF16)
        shard["ffn_w_out", l] = ffn_w_out[l:l + 1].astype(BF16)
    now = ["gm_w_in", "gm_w_out", ("ffn_w_in", 0), ("ffn_w_out", 0), "fox_w_in", "fox_w_out"]
    later = [("ffn_w_in", 1), ("ffn_w_out", 1), "sb_w_in", "sb_w_out", ("ffn_w_in", 2), ("ffn_w_out", 2),
             "cv_w_in", "cv_w_out", ("ffn_w_in", 3), ("ffn_w_out", 3)]
    gathered = dict(zip(now, _gather_weights([shard[n] for n in now])))
    w_ffn_out_rows = lambda l: gathered["ffn_w_out", l].reshape(N_DEV // 2, n_ffn, d)
    sq = lambda n: gathered[n].reshape(d, d)
    fox_full = _full_cols(gathered["fox_w_in"], fox_w_in.shape[2])
    fox_qkv_w, fox_f_wt = fox_full[:, :3 * d], jnp.transpose(fox_full[:, 3 * d:])
    cvp = d // N_DEV
    cv_small = jnp.concatenate([_pad_rows(cv_dw[0], CONV_HALO), cv_dw_b, cv_ln_g, cv_ln_b, cv_b_out,
                                cv_b_in.reshape(2, cvp), jnp.zeros((2, cvp), F32)], axis=0)
    cv_all = _exchange_small(cv_small, "gather_cv_small", False)
    cv_rows = jnp.transpose(cv_all, (1, 0, 2)).reshape(cv_small.shape[0], d)
    cv_dw_f, cv_dwb_f, cv_lng_f, cv_lnb_f, cv_bout_f = (cv_rows[:CONV_HALO], cv_rows[32:33], cv_rows[33:34], cv_rows[34:35], cv_rows[35:36])
    cv_bin_f = cv_all[:, 36:38, :].reshape(1, 2 * d)

    saved = []
    h = _modulate(xs, mods[0][1], mods[0][0], seq)
    xin = xs
    for l in range(nl):
        kind = l % 4
        sv = dict(x=xin, h=h)
        if kind == 0:
            pre = _proj_cols("gm_in", h, gathered["gm_w_in"], gm_w_in.shape[2], bias=gm_b_in)
            yv = _gm_spatial_fwd(pre, gm_ln_g, gm_ln_b, gm_w_s[0], jnp.transpose(gm_b_s[0]), seq)
            y = _mm_plain("gm_out", yv, sq("gm_w_out"), NN)
            sv.update(pre=pre, yv=yv)
        elif kind == 1:
            qkv = _qkv_proj("fox_qkv", h, fox_qkv_w)
            ft = _mm("fox_gate_proj", fox_f_wt, h, _sds((n_heads, t), F32), (t // min(TMM, t), 1),
                     pl.BlockSpec((n_heads, d), lambda i, r: (0, 0)), pl.BlockSpec((min(TMM, t), d), lambda i, r: (i, 0)),
                     pl.BlockSpec((n_heads, min(TMM, t)), lambda i, r: (0, i)), NT)
            b_f = jnp.transpose(fox_b_f)
            frow_p = _fox_gate_fwd(ft, b_f, seq).reshape(npair, 2, t)
            o, lse, *arrived = _fox_fwd(qkv, frow_p, nb, seq, gather=[shard[n] for n in later])
            gathered.update(zip(later, arrived))
            sb_qkv_w = _full_cols(gathered["sb_w_in"], sb_w_in.shape[2])
            y = _mm_plain("fox_out", o, sq("fox_w_out"), NN)
            sv.update(qkv=qkv, ft=ft, b_f=b_f, frow=frow_p, o=o, lse=lse)
        elif kind == 2:
            qkv = _qkv_proj("sb_qkv", h, sb_qkv_w)
            o, ltot = _sb_fwd(qkv, nb, seq)
            y = _mm_plain("sb_out", o, sq("sb_w_out"), NN)
            sv.update(qkv=qkv, o=o, ltot=ltot)
        else:
            pre = _proj_cols("cv_in", h, gathered["cv_w_in"], cv_w_in.shape[2], bias=cv_bin_f)
            y1, y2 = _cv_conv_fwd(pre, cv_dw_f, cv_dwb_f, cv_lng_f, cv_lnb_f, seq)
            y = _mm_plain("cv_out", y2, sq("cv_w_out"), NN, bias=cv_bout_f)
            sv.update(pre=pre, y1=y1, y2=y2)
        x1, h2 = _lnres_fwd(xin, y, mods[l][2], ln1_g[l:l + 1], ln1_b[l:l + 1], alpha, seq, nxt=(mods[l][4], mods[l][3]))
        hg, hu, act = _ffn_in(h2, gathered["ffn_w_in", l], 0)
        y2f = _ffn_out(act, w_ffn_out_rows(l), 0)
        sv.update(y=y, x1=x1, h2=h2, hg=hg, hu=hu, act=act, y2f=y2f)
        if l + 1 < nl:
            xin, h = _lnres_fwd(x1, y2f, mods[l][5], ln2_g[l:l + 1], ln2_b[l:l + 1], alpha, seq, nxt=(mods[l + 1][1], mods[l + 1][0]))
        else:
            xin = _lnres_fwd(x1, y2f, mods[l][5], ln2_g[l:l + 1], ln2_b[l:l + 1], alpha, seq)
        saved.append(sv)

    dx, sq_err = _loss_head(xin, tgt)
    loss = lax.psum(0.5 * jnp.sum(sq_err) / d, ("x", "y", "c"))

    small = {}
    bigg = {}
    recv = {}
    dmod_parts = [dict() for _ in range(nl)]
    pending = None
    d_ln = dict(ln1_g=[None] * nl, ln1_b=[None] * nl, ln2_g=[None] * nl, ln2_b=[None] * nl)
    beside_sb = [("ffn_w_in", 3), ("ffn_w_out", 3), "cv_w_in", "cv_w_out", ("ffn_w_in", 2), ("ffn_w_out", 2)]
    beside_fox = ["sb_w_in", "sb_w_out", ("ffn_w_in", 1), ("ffn_w_out", 1)]
    at_end = ["fox_w_in", "fox_w_out", ("ffn_w_in", 0), ("ffn_w_out", 0), "gm_w_in", "gm_w_out"]
    for l in reversed(range(nl)):
        sv = saved[l]
        kind = l % 4
        if pending is None:
            dxr, dy2, dlg, dlb, _, dgate2 = _lnres_bwd(dx, sv["x1"], sv["y2f"], mods[l][5], ln2_g[l:l + 1], alpha, seq)
        else:
            dxr, dy2, dlg, dlb, _, dgate2, dsc1, dsh1 = _lnres_bwd(pending[1], sv["x1"], sv["y2f"], mods[l][5], ln2_g[l:l + 1], alpha, seq,
                                                                   through=(pending[0], pending[2], ln2_b[l:l + 1]))
            dmod_parts[l + 1].update(sc1=dsc1, sh1=dsh1)
        d_ln["ln2_g"][l], d_ln["ln2_b"][l] = dlg, dlb
        dg_, du_ = _ffn_dact(dy2, w_ffn_out_rows(l), sv["hg"], sv["hu"], 0)
        dwg, dwu, dwo = _ffn_bwd_weights(sv["h2"], dy2, sv["act"], dg_, du_)
        bigg["ffn_w_in", l] = jnp.concatenate([dwg, dwu], axis=0)[None]
        bigg["ffn_w_out", l] = dwo.reshape(1, N_DEV, n_ffn // 2, d)
        dh2 = _ffn_dh(dg_, du_, gathered["ffn_w_in", l], 0)
        dxr, dy, dlg, dlb, dysum, dgate1, dsc2, dsh2 = _lnres_bwd(dxr, sv["x"], sv["y"], mods[l][2], ln1_g[l:l + 1], alpha, seq,
                                                                  through=(dh2, mods[l][4], ln1_b[l:l + 1]))
        d_ln["ln1_g"][l], d_ln["ln1_b"][l] = dlg, dlb
        hh = sv["h"]
        if kind == 0:
            dyv = _mm_plain("gm_out_bwd", dy, sq("gm_w_out"), NT)
            bigg["gm_w_out"] = _mm_plain("gm_out_dw", sv["yv"], dy, TN, GRAD_WIRE).reshape(1, N_DEV, d // N_DEV, d)
            dpre, dws, dbst, dlng, dlnb, dbin = _gm_spatial_bwd(sv["pre"], dyv, gm_ln_g, gm_ln_b, gm_w_s[0], jnp.transpose(gm_b_s[0]), seq)
            small.update(gm_w_s=dws[None], gm_b_s=jnp.transpose(dbst)[None], gm_ln_g=dlng, gm_ln_b=dlnb, gm_b_in=dbin)
            bigg["gm_w_in"] = _grad_cols("gm_in_dw", hh, dpre, gm_w_in.shape[2])
            dh = _back_cols("gm_in_bwd", dpre, gathered["gm_w_in"], gm_w_in.shape[2])
        elif kind == 1:
            do = _mm_plain("fox_out_bwd", dy, sq("fox_w_out"), NT)
            bigg["fox_w_out"] = _mm_plain("fox_out_dw", sv["o"], dy, TN, GRAD_WIRE).reshape(1, N_DEV, d // N_DEV, d)
            dqkv, dfr, dfq, *landed = _fox_bwd(sv["qkv"], sv["frow"], sv["o"], do, sv["lse"], nb, seq,
                                               scatter=[bigg[n] for n in beside_fox])
            recv.update(zip(beside_fox, landed))
            dft, dbf = _fox_gate_bwd(sv["ft"], sv["b_f"], dfr.reshape(n_heads, t),
                                     jnp.transpose(dfq, (0, 2, 1)).reshape(n_heads, t), seq)
            small["fox_b_f"] = jnp.transpose(dbf)
            dw_qkv = _qkv_dw("fox_qkv_dw", hh, dqkv)
            tk = min(TMM, t)
            dw_ft = _mm("fox_gate_dw", dft, hh, _sds((n_heads, d), F32), (1, t // tk),
                        pl.BlockSpec((n_heads, tk), lambda j, r: (0, r)), pl.BlockSpec((tk, d), lambda j, r: (r, 0)),
                        pl.BlockSpec((n_heads, d), lambda j, r: (0, 0)), NN)
            bigg["fox_w_in"] = _col_blocks(jnp.concatenate([dw_qkv, jnp.transpose(dw_ft).astype(GRAD_WIRE)], axis=1), fox_w_in.shape[2])
            dh_a = _qkv_dh("fox_qkv_bwd", dqkv, fox_qkv_w)
            tm = min(TMM, t)
            dh = _mm("fox_gate_bwd_h", dft, fox_f_wt, _sds((t, d), F32), (t // tm, 1),
                     pl.BlockSpec((n_heads, tm), lambda i, r: (0, i)), pl.BlockSpec((n_heads, d), lambda i, r: (0, 0)),
                     pl.BlockSpec((tm, d), lambda i, r: (i, 0)), TN, (dh_a,), (pl.BlockSpec((tm, d), lambda i, r: (i, 0)),), _add)
        elif kind == 2:
            do = _mm_plain("sb_out_bwd", dy, sq("sb_w_out"), NT)
            bigg["sb_w_out"] = _mm_plain("sb_out_dw", sv["o"], dy, TN, GRAD_WIRE).reshape(1, N_DEV, d // N_DEV, d)
            dqkv, *landed = _sb_bwd(sv["qkv"], do, sv["ltot"], nb, seq, scatter=[bigg[n] for n in beside_sb])
            recv.update(zip(beside_sb, landed))
            bigg["sb_w_in"] = _col_blocks(_qkv_dw("sb_qkv_dw", hh, dqkv), sb_w_in.shape[2])
            dh = _qkv_dh("sb_qkv_bwd", dqkv, sb_qkv_w)
        else:
            dy2c = _mm_plain("cv_out_bwd", dy, sq("cv_w_out"), NT)
            bigg["cv_w_out"] = _mm_plain("cv_out_dw", sv["y2"], dy, TN, GRAD_WIRE).reshape(1, N_DEV, d // N_DEV, d)
            dy1, dlng, dlnb, ddwb = _cv_norm_bwd(sv["y1"], dy2c, cv_lng_f, cv_lnb_f)
            dpre, ddw, dbin = _cv_conv_bwd(sv["pre"], dy1, cv_dw_f, seq)
            small.update(cv_b_out=dysum, cv_ln_g=dlng, cv_ln_b=dlnb, cv_dw_b=ddwb, cv_dw=ddw[:CONV_WIDTH], cv_b_in=dbin)
            bigg["cv_w_in"] = _grad_cols("cv_in_dw", hh, dpre, cv_w_in.shape[2])
            dh = _back_cols("cv_in_bwd", dpre, gathered["cv_w_in"], cv_w_in.shape[2])
        pending = (dh, dxr, mods[l][1])
        dmod_parts[l].update(g1=dgate1, sh2=dsh2, sc2=dsc2, g2=dgate2)
    dx, dsc1, dsh1 = _modulate_bwd(pending[0], pending[1], saved[0]["x"], pending[2], seq)
    dmod_parts[0].update(sc1=dsc1, sh1=dsh1)
    dmods = [jnp.concatenate([p["sh1"], p["sc1"], p["g1"], p["sh2"], p["sc2"], p["g2"]], axis=2)[:, 0, :] for p in dmod_parts]
    grad_x = dx.reshape(nb, seq, d)
    for n in d_ln:
        small[n] = jnp.concatenate(d_ln[n], axis=0)

    dmod_rows = jnp.stack(dmods).reshape(nl * nb, 6 * d)
    dmod_g = _exchange_small(_pad_rows(dmod_rows, 8 * ((nl * nb + 7) // 8)), "gather_dmod", False)[:, :nl * nb]
    dmod_all = jnp.transpose(dmod_g.reshape(N_DEV, nl, nb, 6 * d), (1, 0, 2, 3)).reshape(nl, N_DEV * nb, 6 * d)
    dmod_loc = lax.dynamic_slice_in_dim(dmod_all, me * n_mod, n_mod, axis=2)
    g_mod_w, g_mod_b = _mod_bwd(c_all, dmod_loc, dmod_all)
    grads = dict(mod_w=g_mod_w, mod_b=g_mod_b[:, 0, :])

    rep = ["ln1_g", "ln1_b", "ln2_g", "ln2_b", "gm_b_in", "gm_ln_g", "gm_ln_b", "gm_w_s", "gm_b_s", "fox_b_f"]
    cvs = ["cv_b_in", "cv_dw", "cv_dw_b", "cv_ln_g", "cv_ln_b", "cv_b_out"]

    def rows_of(a):
        flat = a.reshape(-1)
        pad = (-flat.shape[0]) % d
        return jnp.pad(flat, (0, pad)).reshape(-1, d)

    pack_rows = [rows_of(small[n]) for n in rep + cvs]
    counts = [r.shape[0] for r in pack_rows]
    total = sum(counts)
    pack = _pad_rows(jnp.concatenate(pack_rows, axis=0), 8 * ((total + 7) // 8))
    summed = _exchange_small(pack, "allreduce_small", True)
    offs = [sum(counts[:i]) for i in range(len(counts))]
    rep_rows = sum(counts[:len(rep)])
    for n, o_, cnt in zip(rep + cvs, offs, counts):
        full = summed[o_:o_ + cnt].reshape(-1)
        if n in rep:
            grads[n] = full[:weights[n].size].reshape(weights[n].shape)
        else:
            wshape = weights[n].shape
            cols = wshape[-1]
            full = full[:math.prod(wshape[:-1]) * cols * N_DEV].reshape(wshape[:-1] + (cols * N_DEV,))
            grads[n] = lax.dynamic_slice_in_dim(full, me * cols, cols, axis=full.ndim - 1)

    recv.update(zip(at_end, _scatter_grads([bigg[n] for n in at_end])))
    for n in ("ffn_w_in", "ffn_w_out"):
        recv[n] = jnp.concatenate([recv[n, l] for l in range(nl)], axis=1)

    outs = {}

    def view2(a):
        return a.reshape(-1, a.shape[-1])

    for n in big:
        w2 = view2(weights[n])
        res = _adamw("adamw_" + n, w2, view2(mom1[n]), view2(mom2[n]), parts=recv[n].reshape((N_DEV,) + w2.shape))
        outs[n] = [r.reshape(weights[n].shape) for r in res]
    res = _adamw("adamw_mod_w", view2(mod_w), view2(m_mod_w), view2(v_mod_w), g=view2(grads["mod_w"]))
    outs["mod_w"] = [r.reshape(mod_w.shape) for r in res]
    rp = lambda src: _pad_rows(jnp.concatenate([rows_of(src[n]) for n in rep], axis=0), 8 * ((rep_rows + 7) // 8))
    res = _adamw("adamw_replicated", rp(weights), rp(mom1), rp(mom2), g=rp(grads))
    for n, o_, cnt in zip(rep, offs, counts):
        outs[n] = [r[o_:o_ + cnt].reshape(-1)[:weights[n].size].reshape(weights[n].shape) for r in res]
    cv_cols = weights["cv_b_out"].shape[-1]
    cp = lambda src: jnp.concatenate([src[n].reshape(-1, cv_cols) for n in cvs], axis=0)
    cv_cnt = [weights[n].size // cv_cols for n in cvs]
    cv_tot = sum(cv_cnt)
    cpp = lambda src: _pad_rows(cp(src), 8 * ((cv_tot + 7) // 8))
    res = _adamw("adamw_cv_small", cpp(weights), cpp(mom1), cpp(mom2), g=cpp(grads))
    o_ = 0
    for n, cnt in zip(cvs, cv_cnt):
        outs[n] = [r[o_:o_ + cnt].reshape(weights[n].shape) for r in res]
        o_ += cnt
    res = _adamw("adamw_mod_b", mod_b, m_mod_b, v_mod_b, g=grads["mod_b"])
    outs["mod_b"] = list(res)

    return (loss, grad_x, *[outs[n][0] for n in names], *[outs[n][1] for n in names],
            *[outs[n][2] for n in names], *[outs[n][3] for n in names])
```

```python
import functools
import math

import jax
import jax.numpy as jnp
from jax import lax
from jax.experimental import pallas as pl
from jax.experimental.pallas import tpu as pltpu

F32 = jnp.float32
BF16 = jnp.bfloat16
MESH = pl.DeviceIdType.MESH

N_DEV = 8
HEAD_DIM = 64
LANES = 128
SUBLANES = 8
GM_CHUNK = 128
GM_GROUPS = 8
CONV_WIDTH = 31
CONV_HALO = 32
CONV_ROWS = 32
LN_EPS = 1e-5
NEG_INF = -1e30
SB_DEAD = -100.0
GRAD_WIRE = jnp.bfloat16
FFN_KEEP = jnp.bfloat16

ADAM_LR = 0.001
ADAM_B1 = 0.9
ADAM_B2 = 0.999
ADAM_EPS = 1e-08
ADAM_WD = 0.01
ADAM_STEP = 10

TM = 512
TMM = 1024
TQ = 256
FOX_FWD_UNROLL = 4
FOX_BWD_UNROLL = 4
SB_BWD_UNROLL = 2
TT = 512
VMEM_BIG = 56 * 1024 * 1024

NN = (((1,), (0,)), ((), ()))
NT = (((1,), (1,)), ((), ()))
TN = (((0,), (0,)), ((), ()))


def _call(body, name, out_shape, grid=None, in_specs=None, out_specs=None, scratch=(), vmem=None):
    params = {}
    if grid is not None:
        params["dimension_semantics"] = ("arbitrary",) * len(grid)
    if vmem is not None:
        params["vmem_limit_bytes"] = vmem
    kw = {}
    if grid is not None:
        kw["grid"] = grid
    if in_specs is not None:
        kw["in_specs"] = in_specs
    if out_specs is not None:
        kw["out_specs"] = out_specs
    return pl.pallas_call(body, name=name, out_shape=out_shape, scratch_shapes=list(scratch),
                          compiler_params=pltpu.CompilerParams(**params), **kw)


def _sds(shape, dtype):
    return jax.ShapeDtypeStruct(tuple(shape), dtype)


def _dot(a, b, dims=NN):
    return lax.dot_general(a.astype(BF16), b.astype(BF16), dims, preferred_element_type=F32)


def _split3(x):
    h1 = x.astype(BF16)
    r1 = x - h1.astype(F32)
    h2 = r1.astype(BF16)
    h3 = (r1 - h2.astype(F32)).astype(BF16)
    return h1, h2, h3


def _dot_exact(x, m, dims=NN):
    h1, h2, h3 = _split3(x)
    d = lambda h: lax.dot_general(h, m, dims, preferred_element_type=F32)
    return (d(h1) + d(h2)) + d(h3)


def _dot_exact_rhs(m, x, dims=NN):
    h1, h2, h3 = _split3(x)
    d = lambda h: lax.dot_general(m, h, dims, preferred_element_type=F32)
    return (d(h1) + d(h2)) + d(h3)


def _silu(x):
    return x * jax.nn.sigmoid(x)


def _gelu(x):
    return 0.5 * x * (1.0 + lax.erf(x * (2.0 ** -0.5)))


def _gelu_grad(x):
    return 0.5 * (1.0 + lax.erf(x * (2.0 ** -0.5))) + x * jnp.exp(-0.5 * x * x) * ((2.0 * math.pi) ** -0.5)


def _log_sigmoid(z):
    return jnp.minimum(z, 0.0) - jnp.log(1.0 + jnp.exp(-jnp.abs(z)))


def _ln_stats(r):
    mu = jnp.mean(r, axis=-1, keepdims=True)
    rc = r - mu
    var = jnp.mean(rc * rc, axis=-1, keepdims=True)
    return rc, lax.rsqrt(var + LN_EPS)


def _colsum(x):
    return jnp.sum(x, axis=0, keepdims=True)


def _peers():
    mx, my, mc = lax.axis_index("x"), lax.axis_index("y"), lax.axis_index("c")
    me = 4 * mx + 2 * my + mc
    out = []
    for k in range(1, N_DEV):
        px = 1 - mx if (k >> 2) & 1 else mx
        py = 1 - my if (k >> 1) & 1 else my
        pc = 1 - mc if k & 1 else mc
        out.append(((px, py, pc), 4 * px + 2 * py + pc))
    return me, out


def _exchange_small(x, name, reduce):
    rows, cols = x.shape

    def body(x_ref, o_ref, *rest):
        if reduce:
            land, send_sems, recv_sems, local_sem = rest
        else:
            land = o_ref
            send_sems, recv_sems, local_sem = rest
        me, peers = _peers()
        mine = pltpu.make_async_copy(x_ref, land.at[me], local_sem)
        mine.start()
        sends = []
        for k, (peer, _) in enumerate(peers):
            cp = pltpu.make_async_remote_copy(src_ref=x_ref, dst_ref=land.at[me], send_sem=send_sems.at[k],
                                              recv_sem=recv_sems.at[k], device_id=peer, device_id_type=MESH)
            cp.start()
            sends.append(cp)
        for k, (peer, blk) in enumerate(peers):
            pltpu.make_async_remote_copy(src_ref=x_ref, dst_ref=land.at[blk], send_sem=send_sems.at[k],
                                         recv_sem=recv_sems.at[k], device_id=peer, device_id_type=MESH).wait_recv()
        for cp in sends:
            cp.wait_send()
        mine.wait()
        if reduce:
            acc = land[0]
            for s in range(1, N_DEV):
                acc = acc + land[s]
            o_ref[...] = acc

    vm = pl.BlockSpec(memory_space=pltpu.VMEM)
    scratch = [pltpu.SemaphoreType.DMA((N_DEV - 1,)), pltpu.SemaphoreType.DMA((N_DEV - 1,)), pltpu.SemaphoreType.DMA]
    if reduce:
        scratch = [pltpu.VMEM((N_DEV, rows, cols), F32)] + scratch
        out = _sds((rows, cols), F32)
    else:
        out = _sds((N_DEV, rows, cols), F32)
    return _call(body, name, out, in_specs=[vm], out_specs=vm, scratch=scratch, vmem=VMEM_BIG)(x)


def _comm_scratch(n):
    return [pltpu.SemaphoreType.DMA((n, N_DEV - 1)), pltpu.SemaphoreType.DMA((n, N_DEV - 1)), pltpu.SemaphoreType.DMA((n,))]


def _gather_stage(stage, ins, outs, send_sems, recv_sems, local_sems):
    n = len(ins)
    mx, my, mc = lax.axis_index("x"), lax.axis_index("y"), lax.axis_index("c")
    here, sibling = (mx, my, mc), (mx, my, 1 - mc)
    chips = [(1 - mx, my), (mx, 1 - my), (1 - mx, 1 - my)]

    def block(px, py, pc):
        return 4 * px + 2 * py + pc

    def copy(a, k, blk, to, src=None):
        dst = outs[a].at[:, blk]
        return pltpu.make_async_remote_copy(src_ref=dst if src is None else src, dst_ref=dst, send_sem=send_sems.at[a, k],
                                            recv_sem=recv_sems.at[a, k], device_id=to, device_id_type=MESH)

    me = block(*here)
    for a in range(n):
        local = pltpu.make_async_copy(ins[a], outs[a].at[:, me], local_sems.at[a])
        first = [copy(a, 0, me, sibling, src=ins[a])] + [copy(a, 1 + j, me, (*chip, mc), src=ins[a]) for j, chip in enumerate(chips)]
        if stage == 0:
            local.start()
            for cp in first:
                cp.start()
        if stage == 1:
            for j, chip in enumerate(chips):
                copy(a, 1 + j, block(*chip, mc), here).wait_recv()
                copy(a, 4 + j, block(*chip, mc), sibling).start()
        if stage == 2:
            copy(a, 0, block(mx, my, 1 - mc), here).wait_recv()
            for j, chip in enumerate(chips):
                copy(a, 4 + j, block(*chip, 1 - mc), here).wait_recv()
            for cp in first:
                cp.wait_send()
            for j, chip in enumerate(chips):
                copy(a, 4 + j, block(*chip, mc), sibling).wait_send()
            local.wait()


def _gather_weights(shards):
    n = len(shards)

    def body(*refs):
        for stage in range(3):
            _gather_stage(stage, refs[:n], refs[n:2 * n], *refs[2 * n:])

    hbm = pl.BlockSpec(memory_space=pl.ANY)
    return _call(body, "gather_weights", _gathered_shapes(shards), in_specs=[hbm] * n, out_specs=[hbm] * n, scratch=_comm_scratch(n))(*shards)


def _gathered_shapes(shards):
    return [_sds((s.shape[0], N_DEV) + s.shape[1:], s.dtype) for s in shards]


def _scatter_grads(grads):
    n = len(grads)

    def body(*refs):
        for stage in range(2):
            _scatter_stage(stage, refs[:n], refs[n:2 * n], *refs[2 * n:])

    hbm = pl.BlockSpec(memory_space=pl.ANY)
    return _call(body, "scatter_grads", _scattered_shapes(grads), in_specs=[hbm] * n, out_specs=[hbm] * n, scratch=_comm_scratch(n))(*grads)


def _scattered_shapes(grads):
    return [_sds((N_DEV, g.shape[0]) + g.shape[2:], g.dtype) for g in grads]


def _run(beside, body, name, out_shape, args, grid, in_specs, out_specs, scratch=(), vmem=None):
    if beside is None:
        return _call(body, name, out_shape, grid=grid, in_specs=in_specs, out_specs=out_specs, scratch=scratch, vmem=vmem)(*args)
    kind, arrays = beside
    n = len(arrays)
    outs = tuple(out_shape) if isinstance(out_shape, (tuple, list)) else (out_shape,)
    ospecs = tuple(out_specs) if isinstance(out_specs, (tuple, list)) else (out_specs,)
    total = math.prod(grid)
    gather = kind == "gather"
    stage_fn = _gather_stage if gather else _scatter_stage
    early = {0: 0, 1: (3 * total) // 4} if gather else {0: 0}

    def carrier(*refs):
        own_in, send, own_out, land, sems, own_scratch = _carried(n, len(in_specs), len(outs), refs)
        at = 0
        for ax, size in enumerate(grid):
            at = at * size + pl.program_id(ax)
        for stage, when in early.items():
            @pl.when(at == when)
            def _(stage=stage):
                stage_fn(stage, send, land, *sems)
        body(*own_in, *own_out, *own_scratch)

        @pl.when(at == total - 1)
        def _():
            stage_fn(2 if gather else 1, send, land, *sems)

    hbm = pl.BlockSpec(memory_space=pl.ANY)
    shapes = _gathered_shapes(arrays) if gather else _scattered_shapes(arrays)
    res = _call(carrier, name, (*outs, *shapes), grid=grid, in_specs=[*in_specs, *[hbm] * n], out_specs=(*ospecs, *[hbm] * n),
                scratch=[*_comm_scratch(n), *scratch], vmem=vmem)(*args, *arrays)
    own = res[:len(outs)]
    return (own if len(own) > 1 else own[0]), list(res[len(outs):])


def _scatter_stage(stage, ins, outs, send_sems, recv_sems, local_sems):
    me, peers = _peers()
    for a in range(len(ins)):
        local = pltpu.make_async_copy(ins[a].at[:, me], outs[a].at[me], local_sems.at[a])
        sends = [pltpu.make_async_remote_copy(src_ref=ins[a].at[:, blk], dst_ref=outs[a].at[me], send_sem=send_sems.at[a, k],
                                              recv_sem=recv_sems.at[a, k], device_id=peer, device_id_type=MESH)
                 for k, (peer, blk) in enumerate(peers)]
        if stage == 0:
            local.start()
            for cp in sends:
                cp.start()
        if stage == 1:
            for k, (peer, blk) in enumerate(peers):
                pltpu.make_async_remote_copy(src_ref=ins[a].at[:, me], dst_ref=outs[a].at[blk], send_sem=send_sems.at[a, k],
                                             recv_sem=recv_sems.at[a, k], device_id=peer, device_id_type=MESH).wait_recv()
            for cp in sends:
                cp.wait_send()
            local.wait()


def _mm(name, a, b, out, grid, a_spec, b_spec, o_spec, dims, extra=(), extra_specs=(), epilogue=None, vmem=VMEM_BIG, beside=None):
    nred = grid[-1]
    red_axis = len(grid) - 1
    acc_shape = tuple(d for d in o_spec.block_shape if d is not None)
    n_extra = len(extra)

    def body(a_ref, b_ref, *rest):
        ex = rest[:n_extra]
        o_ref = rest[n_extra]

        def finish(acc):
            if epilogue is not None:
                acc = epilogue(acc, *[e[...] for e in ex])
            o_ref[...] = acc.astype(o_ref.dtype)

        prod = _dot(a_ref[...], b_ref[...], dims)
        if nred == 1:
            finish(prod)
        else:
            acc_ref = rest[n_extra + 1]
            r = pl.program_id(red_axis)

            @pl.when(r == 0)
            def _():
                acc_ref[...] = prod

            @pl.when(r > 0)
            def _():
                acc_ref[...] += prod

            @pl.when(r == nred - 1)
            def _():
                finish(acc_ref[...])

    scratch = [pltpu.VMEM(acc_shape, F32)] if nred > 1 else []
    return _run(beside, body, name, out, (a, b, *extra), grid, [a_spec, b_spec, *extra_specs], o_spec, scratch, vmem)


def _add(acc, x):
    return acc + x


def _proj_cols(name, h, w, n_slot, bias=None, out_dtype=F32, beside=None):
    t, k = h.shape
    s = w.shape[1]
    tm = min(TMM, t)
    extra, especs, epi = (), (), None
    if bias is not None:
        extra, especs, epi = (bias,), (pl.BlockSpec((1, n_slot), lambda i, j, r: (0, j)),), _add
    return _mm(name, h, w, _sds((t, s * n_slot), out_dtype), (t // tm, s, 1),
               pl.BlockSpec((tm, k), lambda i, j, r: (i, 0)),
               pl.BlockSpec((None, None, k, n_slot), lambda i, j, r: (0, j, 0, 0)),
               pl.BlockSpec((tm, n_slot), lambda i, j, r: (i, j)), NN, extra, especs, epi, beside=beside)


def _accumulate_over_tokens(name, ins, in_specs, out, o_spec, acc_shape, n_steps, terms, store=None):
    def body(*refs):
        o_ref, acc_ref = refs[len(ins)], refs[len(ins) + 1]
        r = pl.program_id(0)

        @pl.when(r == 0)
        def _():
            acc_ref[...] = jnp.zeros_like(acc_ref)

        for s, prod in enumerate(terms(*refs[:len(ins)])):
            acc_ref[s] += prod

        @pl.when(r == n_steps - 1)
        def _():
            if store is None:
                o_ref[...] = acc_ref[...].reshape(o_ref.shape).astype(o_ref.dtype)
            else:
                store(o_ref, acc_ref)

    return _call(body, name, out, grid=(n_steps,), in_specs=in_specs, out_specs=o_spec,
                 scratch=[pltpu.VMEM(acc_shape, F32)], vmem=VMEM_BIG)(*ins)


def _grad_cols(name, h, g, n_slot):
    t, k = h.shape
    s = g.shape[1] // n_slot
    tk = min(TMM, t)

    def terms(h_ref, g_ref):
        hv = h_ref[...]
        return [_dot(hv, g_ref[:, j * n_slot:(j + 1) * n_slot], TN) for j in range(s)]

    return _accumulate_over_tokens(name, (h, g), [pl.BlockSpec((tk, k), lambda r: (r, 0)), pl.BlockSpec((tk, s * n_slot), lambda r: (r, 0))],
                                   _sds((1, s, k, n_slot), GRAD_WIRE), pl.BlockSpec((1, s, k, n_slot), lambda r: (0, 0, 0, 0)),
                                   (s, k, n_slot), t // tk, terms)


def _back_cols(name, g, w, n_slot):
    t = g.shape[0]
    s, k = w.shape[1], w.shape[2]
    tm = min(TM, t)

    def body(g_ref, w_ref, o_ref):
        acc = None
        for j in range(s):
            term = _dot(g_ref[:, j * n_slot:(j + 1) * n_slot], w_ref[j], NT)
            acc = term if acc is None else acc + term
        o_ref[...] = acc

    return _call(body, name, _sds((t, k), F32), grid=(t // tm,),
                 in_specs=[pl.BlockSpec((tm, s * n_slot), lambda i: (i, 0)), pl.BlockSpec((None, s, k, n_slot), lambda i: (0, 0, 0, 0))],
                 out_specs=pl.BlockSpec((tm, k), lambda i: (i, 0)), vmem=VMEM_BIG)(g, w)


def _mm_plain(name, a, b, dims, out_dtype=F32, bias=None):
    if dims == TN:
        t, k = a.shape
        n = b.shape[1]
        tk = min(TMM, t)
        return _mm(name, a, b, _sds((k, n), out_dtype), (1, t // tk),
                   pl.BlockSpec((tk, k), lambda j, r: (r, 0)), pl.BlockSpec((tk, n), lambda j, r: (r, 0)),
                   pl.BlockSpec((k, n), lambda j, r: (0, 0)), TN)
    t = a.shape[0]
    tm = min(TMM, t)
    n = b.shape[1] if dims == NN else b.shape[0]
    extra, especs, epi = (), (), None
    if bias is not None:
        extra, especs, epi = (bias,), (pl.BlockSpec((1, n), lambda i, r: (0, 0)),), _add
    return _mm(name, a, b, _sds((t, n), out_dtype), (t // tm, 1),
               pl.BlockSpec((tm, a.shape[1]), lambda i, r: (i, 0)), pl.BlockSpec(b.shape, lambda i, r: (0, 0)),
               pl.BlockSpec((tm, n), lambda i, r: (i, 0)), dims, extra, especs, epi)


def _mod_fwd(c_all, mod_w, mod_b_loc):
    nl, d, n = mod_w.shape
    nb = c_all.shape[0]

    def body(c_ref, w_ref, b_ref, o_ref):
        o_ref[...] = _dot(_silu(c_ref[...]), w_ref[...]) + b_ref[...]

    return _call(body, "mod_fwd", _sds((nl, nb, n), F32), grid=(nl,),
                 in_specs=[pl.BlockSpec((nb, d), lambda l: (0, 0)), pl.BlockSpec((None, d, n), lambda l: (l, 0, 0)),
                           pl.BlockSpec((None, 1, n), lambda l: (l, 0, 0))],
                 out_specs=pl.BlockSpec((None, nb, n), lambda l: (l, 0, 0)))(c_all, mod_w, mod_b_loc)


def _mod_bwd(c_all, dmod_loc, dmod_all):
    nl, nb, n = dmod_loc.shape
    d = c_all.shape[1]
    n_all = dmod_all.shape[2]

    def body(c_ref, dl_ref, da_ref, gw_ref, gb_ref):
        gw_ref[...] = _dot(_silu(c_ref[...]), dl_ref[...], TN)
        gb_ref[...] = _colsum(da_ref[...])

    return _call(body, "mod_bwd", (_sds((nl, d, n), F32), _sds((nl, 1, n_all), F32)), grid=(nl,),
                 in_specs=[pl.BlockSpec((nb, d), lambda l: (0, 0)), pl.BlockSpec((None, nb, n), lambda l: (l, 0, 0)),
                           pl.BlockSpec((None, nb, n_all), lambda l: (l, 0, 0))],
                 out_specs=(pl.BlockSpec((None, d, n), lambda l: (l, 0, 0)), pl.BlockSpec((None, 1, n_all), lambda l: (l, 0, 0))),
                 )(c_all, dmod_loc, dmod_all)


def _row_spec(d, tpb):
    return pl.BlockSpec((None, 1, d), lambda i: (i // tpb, 0, 0))


def _tile_spec(tm, d):
    return pl.BlockSpec((tm, d), lambda i: (i, 0))


def _vec_spec(d):
    return pl.BlockSpec((1, d), lambda i: (0, 0))


def _modulate(x, sc, sh, seq):
    t, d = x.shape
    tm = min(TM, seq)
    tpb = seq // tm

    def body(x_ref, sc_ref, sh_ref, h_ref):
        h_ref[...] = (x_ref[...] * (1.0 + sc_ref[...]) + sh_ref[...]).astype(BF16)

    return _call(body, "modulate", _sds((t, d), BF16), grid=(t // tm,),
                 in_specs=[_tile_spec(tm, d), _row_spec(d, tpb), _row_spec(d, tpb)], out_specs=_tile_spec(tm, d))(x, sc, sh)


def _lnres_fwd(x, y, gate, lg, lb, alpha, seq, nxt=None):
    t, d = x.shape
    tm = min(TM, seq)
    tpb = seq // tm

    def body(x_ref, y_ref, g_ref, lg_ref, lb_ref, *rest):
        r = alpha * x_ref[...] + (1.0 + g_ref[...]) * y_ref[...]
        rc, rstd = _ln_stats(r)
        xn = rc * rstd * lg_ref[...] + lb_ref[...]
        if nxt is None:
            rest[0][...] = xn
        else:
            sc_ref, sh_ref, xo_ref, h_ref = rest
            xo_ref[...] = xn
            h_ref[...] = (xn * (1.0 + sc_ref[...]) + sh_ref[...]).astype(BF16)

    ins = [_tile_spec(tm, d), _tile_spec(tm, d), _row_spec(d, tpb), _vec_spec(d), _vec_spec(d)]
    if nxt is None:
        return _call(body, "lnres_fwd_last", _sds((t, d), F32), grid=(t // tm,), in_specs=ins,
                     out_specs=_tile_spec(tm, d))(x, y, gate, lg, lb)
    return _call(body, "lnres_fwd", (_sds((t, d), F32), _sds((t, d), BF16)), grid=(t // tm,),
                 in_specs=ins + [_row_spec(d, tpb), _row_spec(d, tpb)],
                 out_specs=(_tile_spec(tm, d), _tile_spec(tm, d)))(x, y, gate, lg, lb, *nxt)


def _loss_head(x, tgt):
    t, d = x.shape
    tm = min(TM, t)

    def body(x_ref, t_ref, dx_ref, sq_ref):
        e = x_ref[...] - t_ref[...]
        dx_ref[...] = e * (1.0 / d)

        @pl.when(pl.program_id(0) == 0)
        def _():
            sq_ref[...] = jnp.zeros_like(sq_ref)

        sq_ref[...] += _colsum(e * e)

    return _call(body, "loss_head", (_sds((t, d), F32), _sds((1, d), F32)), grid=(t // tm,),
                 in_specs=[_tile_spec(tm, d), _tile_spec(tm, d)], out_specs=(_tile_spec(tm, d), _vec_spec(d)))(x, tgt)


def _lnres_bwd(dxo, x, y, gate, lg, alpha, seq, through=None):
    t, d = x.shape
    tm = min(TM, seq)
    tpb = seq // tm
    nb = t // seq

    def body(dxo_ref, x_ref, y_ref, g_ref, lg_ref, *rest):
        if through is None:
            dxr_ref, dy_ref, dlg_ref, dlb_ref, dys_ref, dg_ref = rest
        else:
            dh_ref, sc_ref, lb_ref, dxr_ref, dy_ref, dlg_ref, dlb_ref, dys_ref, dg_ref, dsc_ref, dsh_ref = rest
        i = pl.program_id(0)
        yv = y_ref[...]
        r = alpha * x_ref[...] + (1.0 + g_ref[...]) * yv
        rc, rstd = _ln_stats(r)
        xhat = rc * rstd
        dxo_v = dxo_ref[...]
        if through is not None:
            dhv = dh_ref[...]
            dxo_v = dxo_v + dhv * (1.0 + sc_ref[...])

            @pl.when(i % tpb == 0)
            def _():
                dsc_ref[...] = jnp.zeros_like(dsc_ref)
                dsh_ref[...] = jnp.zeros_like(dsh_ref)

            dsc_ref[...] += _colsum(dhv * (xhat * lg_ref[...] + lb_ref[...]))
            dsh_ref[...] += _colsum(dhv)
        dxh = dxo_v * lg_ref[...]
        m1 = jnp.mean(dxh, axis=-1, keepdims=True)
        m2 = jnp.mean(dxh * xhat, axis=-1, keepdims=True)
        dr = rstd * (dxh - m1 - xhat * m2)
        dyv = (1.0 + g_ref[...]) * dr
        dxr_ref[...] = alpha * dr
        dy_ref[...] = dyv.astype(BF16)

        @pl.when(i == 0)
        def _():
            dlg_ref[...] = jnp.zeros_like(dlg_ref)
            dlb_ref[...] = jnp.zeros_like(dlb_ref)
            dys_ref[...] = jnp.zeros_like(dys_ref)

        @pl.when(i % tpb == 0)
        def _():
            dg_ref[...] = jnp.zeros_like(dg_ref)

        dlg_ref[...] += _colsum(dxo_v * xhat)
        dlb_ref[...] += _colsum(dxo_v)
        dys_ref[...] += _colsum(dyv)
        dg_ref[...] += _colsum(dr * yv)

    out = [_sds((t, d), F32), _sds((t, d), BF16), _sds((1, d), F32), _sds((1, d), F32), _sds((1, d), F32), _sds((nb, 1, d), F32)]
    out_specs = [_tile_spec(tm, d), _tile_spec(tm, d), _vec_spec(d), _vec_spec(d), _vec_spec(d), _row_spec(d, tpb)]
    ins = [dxo, x, y, gate, lg]
    in_specs = [_tile_spec(tm, d), _tile_spec(tm, d), _tile_spec(tm, d), _row_spec(d, tpb), _vec_spec(d)]
    if through is not None:
        ins += list(through)
        in_specs += [_tile_spec(tm, d), _row_spec(d, tpb), _vec_spec(d)]
        out += [_sds((nb, 1, d), F32), _sds((nb, 1, d), F32)]
        out_specs += [_row_spec(d, tpb), _row_spec(d, tpb)]
    return _call(body, "lnres_bwd" if through is None else "lnres_mod_bwd", tuple(out), grid=(t // tm,),
                 in_specs=in_specs, out_specs=tuple(out_specs))(*ins)


def _modulate_bwd(dh, dxr, x, sc, seq):
    t, d = x.shape
    tm = min(TM, seq)
    tpb = seq // tm
    nb = t // seq

    def body(dh_ref, dxr_ref, x_ref, sc_ref, dx_ref, dsc_ref, dsh_ref):
        dhv = dh_ref[...]
        dx_ref[...] = dxr_ref[...] + dhv * (1.0 + sc_ref[...])

        @pl.when(pl.program_id(0) % tpb == 0)
        def _():
            dsc_ref[...] = jnp.zeros_like(dsc_ref)
            dsh_ref[...] = jnp.zeros_like(dsh_ref)

        dsc_ref[...] += _colsum(dhv * x_ref[...])
        dsh_ref[...] += _colsum(dhv)

    return _call(body, "modulate_bwd", (_sds((t, d), F32), _sds((nb, 1, d), F32), _sds((nb, 1, d), F32)), grid=(t // tm,),
                 in_specs=[_tile_spec(tm, d), _tile_spec(tm, d), _tile_spec(tm, d), _row_spec(d, tpb)],
                 out_specs=(_tile_spec(tm, d), _row_spec(d, tpb), _row_spec(d, tpb)))(dh, dxr, x, sc)


def _ffn_in(h, w_in, layer, beside=None):
    t, d = h.shape
    n = w_in.shape[3]
    half = N_DEV // 2
    tm = min(TMM, t)

    def body(h_ref, wg_ref, wu_ref, g_ref, u_ref, a_ref):
        hv = h_ref[...]
        g = _dot(hv, wg_ref[...])
        u = _dot(hv, wu_ref[...])
        g_ref[...] = g.astype(FFN_KEEP)
        u_ref[...] = u.astype(FFN_KEEP)
        a_ref[...] = (_silu(g) * u).astype(BF16)

    blk = pl.BlockSpec((None, tm, n), lambda p, i: (p, i, 0))
    return _run(beside, body, "ffn_in", (_sds((half, t, n), FFN_KEEP), _sds((half, t, n), FFN_KEEP), _sds((half, t, n), BF16)),
                (h, w_in, w_in), (half, t // tm),
                [pl.BlockSpec((tm, d), lambda p, i: (i, 0)),
                 pl.BlockSpec((None, None, d, n), lambda p, i: (layer, p, 0, 0)),
                 pl.BlockSpec((None, None, d, n), lambda p, i: (layer, p + half, 0, 0))],
                (blk, blk, blk), vmem=VMEM_BIG)


def _ffn_out(act, w_out, layer):
    half, t, n = act.shape
    d = w_out.shape[2]
    tm = min(TM, t)

    def body(a_ref, w_ref, o_ref):
        acc = _dot(a_ref[0], w_ref[0])
        for p in range(1, half):
            acc = acc + _dot(a_ref[p], w_ref[p])
        o_ref[...] = acc

    return _call(body, "ffn_out", _sds((t, d), F32), grid=(t // tm,),
                 in_specs=[pl.BlockSpec((half, tm, n), lambda i: (0, i, 0)),
                           pl.BlockSpec((half, n, d), lambda i: (layer // half, 0, 0))],
                 out_specs=pl.BlockSpec((tm, d), lambda i: (i, 0)), vmem=VMEM_BIG)(act, w_out)


def _ffn_dact(dy, w_out, hg, hu, layer, beside=None):
    half, t, n = hg.shape
    d = dy.shape[1]
    tm = min(TMM, t)

    def body(dy_ref, w_ref, g_ref, u_ref, dg_ref, du_ref):
        da = _dot(dy_ref[...], w_ref[...], NT)
        g = g_ref[...].astype(F32)
        sg = jax.nn.sigmoid(g)
        dg_ref[...] = (da * u_ref[...].astype(F32) * (sg * (1.0 + g * (1.0 - sg)))).astype(BF16)
        du_ref[...] = (da * (g * sg)).astype(BF16)

    blk = pl.BlockSpec((None, tm, n), lambda p, i: (p, i, 0))
    return _run(beside, body, "ffn_dact", (_sds((half, t, n), BF16), _sds((half, t, n), BF16)), (dy, w_out, hg, hu), (half, t // tm),
                [pl.BlockSpec((tm, d), lambda p, i: (i, 0)), pl.BlockSpec((None, n, d), lambda p, i: (layer + p, 0, 0)), blk, blk],
                (blk, blk), vmem=VMEM_BIG)


def _ffn_bwd_weights(h, dy, act, dg, du):
    half, t, n = act.shape
    d = h.shape[1]
    tk = min(TMM, t)
    h_spec = pl.BlockSpec((tk, d), lambda r: (r, 0))
    g_spec = pl.BlockSpec((half, tk, n), lambda r: (0, r, 0))

    def in_terms(h_ref, g_ref):
        hv = h_ref[...]
        return [_dot(hv, g_ref[p], TN) for p in range(half)]

    def out_terms(a_ref, dy_ref):
        dyv = dy_ref[...]
        return [_dot(a_ref[p], dyv, TN) for p in range(half)]

    w_in_spec = pl.BlockSpec((half, d, n), lambda r: (0, 0, 0))
    dwg = _accumulate_over_tokens("ffn_dw_gate", (h, dg), [h_spec, g_spec], _sds((half, d, n), GRAD_WIRE), w_in_spec,
                                  (half, d, n), t // tk, in_terms)
    dwu = _accumulate_over_tokens("ffn_dw_up", (h, du), [h_spec, g_spec], _sds((half, d, n), GRAD_WIRE), w_in_spec,
                                  (half, d, n), t // tk, in_terms)
    dwo = _accumulate_over_tokens("ffn_dw_out", (act, dy), [g_spec, h_spec], _sds((half, n, d), GRAD_WIRE),
                                  pl.BlockSpec((half, n, d), lambda r: (0, 0, 0)), (half, n, d), t // tk, out_terms)
    return dwg, dwu, dwo


def _ffn_dh(dg, du, w_in, layer, beside=None):
    half, t, n = dg.shape
    d = w_in.shape[2]
    tm = min(TM, t)

    def body(dg_ref, du_ref, w_ref, o_ref):
        acc = None
        for p in range(half):
            for ref, q in ((dg_ref, p), (du_ref, p + half)):
                term = _dot(ref[p], w_ref[q], NT)
                acc = term if acc is None else acc + term
        o_ref[...] = acc

    g_spec = pl.BlockSpec((half, tm, n), lambda i: (0, i, 0))
    return _run(beside, body, "ffn_dh", _sds((t, d), F32), (dg, du, w_in), (t // tm,),
                [g_spec, g_spec, pl.BlockSpec((None, 2 * half, d, n), lambda i: (layer, 0, 0, 0))],
                pl.BlockSpec((tm, d), lambda i: (i, 0)), vmem=VMEM_BIG)


def _tril(n, strict=False):
    r = lax.broadcasted_iota(jnp.int32, (n, n), 0)
    c = lax.broadcasted_iota(jnp.int32, (n, n), 1)
    return c < r if strict else c <= r


def _gm_spatial_fwd(pre, lng, lnb, w_s, b_st, seq):
    t, w2 = pre.shape
    w = w2 // 2
    gd = w // GM_GROUPS
    tm = min(TM, seq)
    nch = tm // GM_CHUNK

    def body(pre_ref, lng_ref, lnb_ref, ws_ref, bs_ref, y_ref):
        v = _gelu(pre_ref[:, w:])
        vc, rstd = _ln_stats(v)
        vn = (vc * rstd * lng_ref[...] + lnb_ref[...]).astype(BF16)
        keep = _tril(GM_CHUNK)
        for g in range(GM_GROUPS):
            wm = jnp.where(keep, ws_ref[g], 0.0).astype(BF16)
            for ci in range(nch):
                rows = slice(ci * GM_CHUNK, (ci + 1) * GM_CHUNK)
                cols = slice(g * gd, (g + 1) * gd)
                sv = _dot(wm, vn[rows, cols]) + bs_ref[:, g:g + 1]
                u = _gelu(pre_ref[rows, cols])
                y_ref[rows, cols] = (u * sv).astype(BF16)

    return _call(body, "gm_spatial_fwd", _sds((t, w), BF16), grid=(t // tm,),
                 in_specs=[_tile_spec(tm, w2), _vec_spec(w), _vec_spec(w),
                           pl.BlockSpec((GM_GROUPS, GM_CHUNK, GM_CHUNK), lambda i: (0, 0, 0)),
                           pl.BlockSpec((GM_CHUNK, GM_GROUPS), lambda i: (0, 0))],
                 out_specs=_tile_spec(tm, w), vmem=VMEM_BIG)(pre, lng, lnb, w_s, b_st)


def _gm_spatial_bwd(pre, dyv, lng, lnb, w_s, b_st, seq, beside=None):
    t, w2 = pre.shape
    w = w2 // 2
    gd = w // GM_GROUPS
    tm = min(TM, seq)
    nch = tm // GM_CHUNK

    def body(pre_ref, dyv_ref, lng_ref, lnb_ref, ws_ref, bs_ref, dpre_ref, dws_ref, dbs_ref, dlg_ref, dlb_ref, dbin_ref, dvn_ref):
        @pl.when(pl.program_id(0) == 0)
        def _():
            dws_ref[...] = jnp.zeros_like(dws_ref)
            dbs_ref[...] = jnp.zeros_like(dbs_ref)
            dlg_ref[...] = jnp.zeros_like(dlg_ref)
            dlb_ref[...] = jnp.zeros_like(dlb_ref)
            dbin_ref[...] = jnp.zeros_like(dbin_ref)

        pv = pre_ref[:, w:]
        v = _gelu(pv)
        vc, rstd = _ln_stats(v)
        vhat = vc * rstd
        vn = (vhat * lng_ref[...] + lnb_ref[...]).astype(BF16)
        keep = _tril(GM_CHUNK)
        dbs_cols = []
        for g in range(GM_GROUPS):
            wm = jnp.where(keep, ws_ref[g], 0.0).astype(BF16)
            dwm = jnp.zeros((GM_CHUNK, GM_CHUNK), F32)
            dbs = jnp.zeros((GM_CHUNK, 1), F32)
            for ci in range(nch):
                rows = slice(ci * GM_CHUNK, (ci + 1) * GM_CHUNK)
                cols = slice(g * gd, (g + 1) * gd)
                vn_b = vn[rows, cols]
                sv = _dot(wm, vn_b) + bs_ref[:, g:g + 1]
                pu = pre_ref[rows, cols]
                dy = dyv_ref[rows, cols]
                du = dy * sv
                dsv = dy * _gelu(pu)
                dpu = du * _gelu_grad(pu)
                dpre_ref[rows, cols] = dpu.astype(BF16)
                dbin_ref[:, cols] += _colsum(dpu)
                dsv_b = dsv.astype(BF16)
                dwm = dwm + _dot(dsv_b, vn_b, NT)
                dbs = dbs + jnp.sum(dsv, axis=-1, keepdims=True)
                dvn_ref[rows, cols] = _dot(wm, dsv_b, TN)
            dws_ref[g] += jnp.where(keep, dwm, 0.0)
            dbs_cols.append(dbs)
        dbs_ref[...] += jnp.concatenate(dbs_cols, axis=1)
        dvn = dvn_ref[...]
        dlg_ref[...] += _colsum(dvn * vhat)
        dlb_ref[...] += _colsum(dvn)
        dvh = dvn * lng_ref[...]
        m1 = jnp.mean(dvh, axis=-1, keepdims=True)
        m2 = jnp.mean(dvh * vhat, axis=-1, keepdims=True)
        dv = rstd * (dvh - m1 - vhat * m2)
        dpv = dv * _gelu_grad(pv)
        dpre_ref[:, w:] = dpv.astype(BF16)
        dbin_ref[:, w:] += _colsum(dpv)

    full3 = pl.BlockSpec((GM_GROUPS, GM_CHUNK, GM_CHUNK), lambda i: (0, 0, 0))
    bst = pl.BlockSpec((GM_CHUNK, GM_GROUPS), lambda i: (0, 0))
    return _run(beside, body, "gm_spatial_bwd",
                (_sds((t, w2), BF16), _sds((GM_GROUPS, GM_CHUNK, GM_CHUNK), F32), _sds((GM_CHUNK, GM_GROUPS), F32),
                 _sds((1, w), F32), _sds((1, w), F32), _sds((1, w2), F32)),
                (pre, dyv, lng, lnb, w_s, b_st), (t // tm,),
                [_tile_spec(tm, w2), _tile_spec(tm, w), _vec_spec(w), _vec_spec(w), full3, bst],
                (_tile_spec(tm, w2), full3, bst, _vec_spec(w), _vec_spec(w), _vec_spec(w2)),
                scratch=[pltpu.VMEM((tm, w), F32)], vmem=VMEM_BIG)


def _head_masks():
    lane = lax.broadcasted_iota(jnp.int32, (1, LANES), 1)
    return lane < HEAD_DIM


def _two_heads(x, m0):
    z = jnp.zeros_like(x)
    return jnp.where(m0, x, z), jnp.where(m0, z, x)


def _transposed(pair):
    return tuple(x.astype(F32).T.astype(BF16) for x in pair)


def _store_transposed(dqkv_ref, dk_acc, dv_acc, nq, tq):
    for c in range(nq):
        cols = slice(c * tq, (c + 1) * tq)
        dqkv_ref[1, cols, :] = dk_acc[:, cols].T.astype(BF16)
        dqkv_ref[2, cols, :] = dv_acc[:, cols].T.astype(BF16)


def _qkv_specs(seq, nq, blocked_q):
    if blocked_q:
        q = pl.BlockSpec((None, TQ_(seq), LANES), lambda b, p, i: (0, b * nq + i, p))
        k = pl.BlockSpec((None, seq, LANES), lambda b, p, i: (1, b, p))
        v = pl.BlockSpec((None, seq, LANES), lambda b, p, i: (2, b, p))
    else:
        q = pl.BlockSpec((None, seq, LANES), lambda b, p: (0, b, p))
        k = pl.BlockSpec((None, seq, LANES), lambda b, p: (1, b, p))
        v = pl.BlockSpec((None, seq, LANES), lambda b, p: (2, b, p))
    return q, k, v


def TQ_(seq):
    return min(TQ, seq)


def _fox_gate_fwd(ft, b_f, seq):
    nh, t = ft.shape
    nch = seq // LANES

    def body(ft_ref, bf_ref, fr_ref):
        r = lax.broadcasted_iota(jnp.int32, (LANES, LANES), 0)
        c = lax.broadcasted_iota(jnp.int32, (LANES, LANES), 1)
        upper = jnp.where(r <= c, 1.0, 0.0).astype(BF16)
        carry = jnp.zeros((nh, 1), F32)
        for ci in range(nch):
            cols = slice(ci * LANES, (ci + 1) * LANES)
            lf = _log_sigmoid(ft_ref[:, cols] + bf_ref[...])
            cs = _dot_exact(lf, upper) + carry
            fr_ref[:, cols] = cs
            carry = cs[:, LANES - 1:LANES]

    return _call(body, "fox_gate_fwd", _sds((nh, t), F32), grid=(t // seq,),
                 in_specs=[pl.BlockSpec((nh, seq), lambda b: (0, b)), pl.BlockSpec((nh, 1), lambda b: (0, 0))],
                 out_specs=pl.BlockSpec((nh, seq), lambda b: (0, b)))(ft, b_f)


def _fox_gate_bwd(ft, b_f, dfk, dfq, seq):
    nh, t = ft.shape
    nch = seq // LANES

    def body(ft_ref, bf_ref, dfk_ref, dfq_ref, dl_ref, db_ref):
        @pl.when(pl.program_id(0) == 0)
        def _():
            db_ref[...] = jnp.zeros_like(db_ref)

        r = lax.broadcasted_iota(jnp.int32, (LANES, LANES), 0)
        c = lax.broadcasted_iota(jnp.int32, (LANES, LANES), 1)
        lower = jnp.where(r >= c, 1.0, 0.0).astype(BF16)
        carry = jnp.zeros((nh, 1), F32)
        tot = jnp.zeros((nh, 1), F32)
        for ci in reversed(range(nch)):
            cols = slice(ci * LANES, (ci + 1) * LANES)
            rc = _dot_exact(dfk_ref[:, cols] + dfq_ref[:, cols], lower) + carry
            carry = rc[:, 0:1]
            dl = rc * jax.nn.sigmoid(-(ft_ref[:, cols] + bf_ref[...]))
            dl_ref[:, cols] = dl
            tot = tot + jnp.sum(dl, axis=-1, keepdims=True)
        db_ref[...] += tot

    blk = pl.BlockSpec((nh, seq), lambda b: (0, b))
    one = pl.BlockSpec((nh, 1), lambda b: (0, 0))
    return _call(body, "fox_gate_bwd", (_sds((nh, t), F32), _sds((nh, 1), F32)), grid=(t // seq,),
                 in_specs=[blk, one, blk, blk], out_specs=(blk, one))(ft, b_f, dfk, dfq)


def _sweep(step, n_off, unroll, init, start=0):
    def group(_, st):
        base, carry = st[0], st[1:]
        for u in range(unroll):
            carry = step(base + u, carry, False)
        return (base + unroll, *carry)

    def tail(r):
        def run(st):
            base, carry = st[0], st[1:]
            for u in range(r):
                carry = step(base + u, carry, False)
            return step(base + r, carry, True)
        return run

    def pick(idx, fns, st):
        if len(fns) == 1:
            return fns[0](st)
        half = len(fns) // 2
        return lax.cond(idx < half, lambda s: pick(idx, fns[:half], s), lambda s: pick(idx - half, fns[half:], s), st)

    st = lax.fori_loop(0, n_off // unroll, group, (jnp.int32(0) + start, *init))
    return pick(n_off % unroll, [tail(r) for r in range(unroll)], st)


def _carried(n_comm, n_in, n_out, refs):
    own_in, send = refs[:n_in], refs[n_in:n_in + n_comm]
    rest = refs[n_in + n_comm:]
    n_sem = 3 if n_comm else 0
    return own_in, send, rest[:n_out], rest[n_out:n_out + n_comm], rest[n_out + n_comm:n_out + n_comm + n_sem], rest[n_out + n_comm + n_sem:]


def _fox_fwd(qkv, frow, nb, seq, gather=()):
    _, t, d = qkv.shape
    npair = d // LANES
    tq = TQ_(seq)
    nq = seq // tq
    scale = HEAD_DIM ** -0.5
    n_comm = len(gather)
    total = nb * npair * nq

    def body(*refs):
        (q_ref, k_ref, v_ref, fr_ref), send, (o_ref, lse_ref), land, sems, _ = _carried(n_comm, 4, 2, refs)
        at = (pl.program_id(0) * npair + pl.program_id(1)) * nq + pl.program_id(2)
        if n_comm:
            for stage, when in ((0, 0), (1, (3 * total) // 4)):
                @pl.when(at == when)
                def _(stage=stage):
                    _gather_stage(stage, send, land, *sems)

        m0 = _head_masks()
        qm = _two_heads(q_ref[...] * scale, m0)
        row = lax.broadcasted_iota(jnp.int32, (tq, tq), 0)
        col = lax.broadcasted_iota(jnp.int32, (tq, tq), 1)
        one = jnp.ones((tq, LANES), BF16)

        def step(j, carry, diag):
            off = pl.multiple_of(j * tq, tq)
            kb = k_ref[pl.ds(off, tq), :]
            vb = v_ref[pl.ds(off, tq), :]
            vv = (jnp.where(m0, vb, one), jnp.where(m0, one, vb))
            out = []
            for hh in range(2):
                m, acc = carry[2 * hh], carry[2 * hh + 1]
                s = lax.dot_general(qm[hh], kb, NT, preferred_element_type=F32) - fr_ref[hh:hh + 1, pl.ds(off, tq)]
                if diag:
                    s = jnp.where(col <= row, s, NEG_INF)
                mn = jnp.maximum(m, jnp.max(s, axis=-1, keepdims=True))
                p = jnp.exp(s - mn)
                out += [mn, jnp.exp(m - mn) * acc + jnp.dot(p.astype(BF16), vv[hh], preferred_element_type=F32)]
            return tuple(out)

        neg = jnp.full((tq, 1), NEG_INF, F32)
        zacc = jnp.zeros((tq, LANES), F32)
        m_a, acc_a, m_b, acc_b = _sweep(step, pl.program_id(2), FOX_FWD_UNROLL, (neg, zacc, neg, zacc))
        l_a = pltpu.roll(acc_a, HEAD_DIM, 1)
        l_b = pltpu.roll(acc_b, HEAD_DIM, 1)
        o_ref[...] = jnp.where(m0, acc_a / l_a, acc_b / l_b)
        lse_ref[:, 0:1] = m_a + jnp.log(l_a[:, 0:1])
        lse_ref[:, 1:2] = m_b + jnp.log(l_b[:, HEAD_DIM:HEAD_DIM + 1])
        if n_comm:
            @pl.when(at == total - 1)
            def _():
                _gather_stage(2, send, land, *sems)

    q_spec, k_spec, v_spec = _qkv_specs(seq, nq, True)
    col_spec = pl.BlockSpec((None, tq, 2), lambda b, p, i: (p, b * nq + i, 0))
    hbm = pl.BlockSpec(memory_space=pl.ANY)
    return _call(body, "fox_fwd", (_sds((t, d), F32), _sds((npair, t, 2), F32), *_gathered_shapes(gather)), grid=(nb, npair, nq),
                 in_specs=[q_spec, k_spec, v_spec, pl.BlockSpec((None, 2, seq), lambda b, p, i: (p, 0, b))] + [hbm] * n_comm,
                 out_specs=(pl.BlockSpec((tq, LANES), lambda b, p, i: (b * nq + i, p)), col_spec, *[hbm] * n_comm),
                 scratch=_comm_scratch(n_comm) if n_comm else (), vmem=VMEM_BIG)(qkv, qkv, qkv, frow, *gather)


def _scatter_beside(stage, n_comm, nb, npair, send, land, sems):
    if n_comm:
        at = pl.program_id(0) * npair + pl.program_id(1)

        @pl.when(at == (0 if stage == 0 else nb * npair - 1))
        def _():
            _scatter_stage(stage, send, land, *sems)


def _fox_bwd(qkv, frow, o, do, lse, nb, seq, scatter=()):
    _, t, d = qkv.shape
    npair = d // LANES
    tq = TQ_(seq)
    nq = seq // tq
    scale = HEAD_DIM ** -0.5
    n_comm = len(scatter)

    def body(*refs):
        own_in, send, (dqkv_ref, df_ref, dfq_ref), land, sems, (dk_acc, dv_acc) = _carried(n_comm, 7, 3, refs)
        q_ref, k_ref, v_ref, fr_ref, o_ref, do_ref, lse_ref = own_in
        _scatter_beside(0, n_comm, nb, npair, send, land, sems)
        m0 = _head_masks()
        row = lax.broadcasted_iota(jnp.int32, (tq, tq), 0)
        col = lax.broadcasted_iota(jnp.int32, (tq, tq), 1)
        dk_acc[...] = jnp.zeros_like(dk_acc)
        dv_acc[...] = jnp.zeros_like(dv_acc)
        df_ref[...] = jnp.zeros_like(df_ref)

        def q_block(i, _):
            qoff = pl.multiple_of(i * tq, tq)
            qrows = pl.ds(qoff, tq)
            qm = _two_heads(q_ref[qrows, :] * scale, m0)
            dov = do_ref[qrows, :]
            dd = dov * o_ref[qrows, :]
            dm = _two_heads(dov.astype(BF16), m0)
            qmt = _transposed(qm)
            dmt = _transposed(dm)
            delta = (jnp.sum(jnp.where(m0, dd, 0.0), axis=-1, keepdims=True),
                     jnp.sum(jnp.where(m0, 0.0, dd), axis=-1, keepdims=True))
            ls = (lse_ref[qrows, 0:1], lse_ref[qrows, 1:2])

            def step(j, carry, diag):
                off = pl.multiple_of(j * tq, tq)
                krows = pl.ds(off, tq)
                kb = k_ref[krows, :]
                vb = v_ref[krows, :]
                dqs, rowsums = [], []
                dk = jnp.zeros((LANES, tq), F32)
                dv = jnp.zeros((LANES, tq), F32)
                for hh in range(2):
                    s = lax.dot_general(qm[hh], kb, NT, preferred_element_type=F32) - fr_ref[hh:hh + 1, krows]
                    if diag:
                        s = jnp.where(col <= row, s, NEG_INF)
                    p = jnp.exp(s - ls[hh])
                    dp = lax.dot_general(dm[hh], vb, NT, preferred_element_type=F32)
                    ds = p * (dp - delta[hh])
                    df_ref[hh:hh + 1, krows] -= _colsum(ds)
                    rowsums.append(carry[1 + hh] + jnp.sum(ds, axis=-1, keepdims=True))
                    ds_b = ds.astype(BF16)
                    dqs.append(jnp.dot(ds_b, kb, preferred_element_type=F32))
                    dk = dk + jnp.dot(qmt[hh], ds_b, preferred_element_type=F32)
                    dv = dv + jnp.dot(dmt[hh], p.astype(BF16), preferred_element_type=F32)
                dk_acc[:, krows] += dk
                dv_acc[:, krows] += dv
                return (carry[0] + jnp.where(m0, dqs[0], dqs[1]), *rowsums)

            zero = jnp.zeros((tq, 1), F32)
            dq, rs_a, rs_b = _sweep(step, i, FOX_BWD_UNROLL, (jnp.zeros((tq, LANES), F32), zero, zero))
            dqkv_ref[0, qrows, :] = (dq * scale).astype(BF16)
            dfq_ref[qrows, 0:1] = rs_a
            dfq_ref[qrows, 1:2] = rs_b
            return 0

        lax.fori_loop(0, nq, q_block, 0)
        _store_transposed(dqkv_ref, dk_acc, dv_acc, nq, tq)
        _scatter_beside(1, n_comm, nb, npair, send, land, sems)

    q_spec, k_spec, v_spec = _qkv_specs(seq, nq, False)
    col_spec = pl.BlockSpec((None, seq, 2), lambda b, p: (p, b, 0))
    row_spec = pl.BlockSpec((None, 2, seq), lambda b, p: (p, 0, b))
    tile = pl.BlockSpec((seq, LANES), lambda b, p: (b, p))
    hbm = pl.BlockSpec(memory_space=pl.ANY)
    return _call(body, "fox_bwd", (_sds((3, t, d), BF16), _sds((npair, 2, t), F32), _sds((npair, t, 2), F32), *_scattered_shapes(scatter)),
                 grid=(nb, npair),
                 in_specs=[q_spec, k_spec, v_spec, row_spec, tile, tile, col_spec] + [hbm] * n_comm,
                 out_specs=(pl.BlockSpec((3, seq, LANES), lambda b, p: (0, b, p)), row_spec, col_spec, *[hbm] * n_comm),
                 scratch=(_comm_scratch(n_comm) if n_comm else []) + [pltpu.VMEM((LANES, seq), F32), pltpu.VMEM((LANES, seq), F32)],
                 vmem=VMEM_BIG)(qkv, qkv, qkv, frow, o, do, lse, *scatter)


def _split2(x):
    hi = x.astype(BF16)
    return hi, (x - hi.astype(F32)).astype(BF16)


def _sum_right(x, tri):
    hi, lo = _split2(x)
    return jnp.dot(hi, tri, preferred_element_type=F32) + jnp.dot(lo, tri, preferred_element_type=F32)


def _sb_scores(qm_h, kb, mask):
    z = lax.dot_general(qm_h, kb, NT, preferred_element_type=F32)
    lb = _log_sigmoid(z)
    l1m = lb - z
    if mask is not None:
        l1m = jnp.where(mask, l1m, 0.0)
    return lb, l1m


def _sb_fwd(qkv, nb, seq):
    _, t, d = qkv.shape
    npair = d // LANES
    tq = TQ_(seq)
    nq = seq // tq
    scale = HEAD_DIM ** -0.5

    def body(q_ref, k_ref, v_ref, o_ref, lt_ref):
        i = pl.program_id(2)
        m0 = _head_masks()
        qm = _two_heads(q_ref[...] * scale, m0)
        row = lax.broadcasted_iota(jnp.int32, (tq, tq), 0)
        col = lax.broadcasted_iota(jnp.int32, (tq, tq), 1)
        after = jnp.where(row > col, 1.0, 0.0).astype(BF16)

        def step(j, carry, diag):
            off = pl.multiple_of(j * tq, tq)
            kb = k_ref[pl.ds(off, tq), :]
            vb = v_ref[pl.ds(off, tq), :]
            mask = (col < row) if diag else None
            nxt, parts = [], []
            for hh in range(2):
                lb, l1m = _sb_scores(qm[hh], kb, mask)
                rest = _sum_right(l1m, after) + carry[hh]
                a = jnp.exp(lb + rest)
                if diag:
                    a = jnp.where(mask, a, 0.0)
                parts.append(jnp.dot(a.astype(BF16), vb, preferred_element_type=F32))
                nxt.append(carry[hh] + jnp.sum(l1m, axis=-1, keepdims=True))
            return (*nxt, carry[2] + jnp.where(m0, parts[0], parts[1]))

        zero = jnp.zeros((tq, 1), F32)
        init = (zero, zero, jnp.zeros((tq, LANES), F32))
        carry = lax.cond(i > 0, lambda c: step(i - 1, step(i, c, True), False), lambda c: step(i, c, True), init)

        def alive(st):
            return (st[0] < i) & (jnp.max(jnp.maximum(st[1], st[2])) > SB_DEAD)

        def more(st):
            return (st[0] + 1, *step(i - 1 - st[0], st[1:], False))

        done, lt_a, lt_b, acc = lax.while_loop(alive, more, (jnp.minimum(i, 1), *carry))
        o_ref[...] = acc
        lt_ref[:, 0:1] = lt_a
        lt_ref[:, 1:2] = lt_b
        lt_ref[:, 2:3] = jnp.zeros((tq, 1), F32) + done.astype(F32)
        lt_ref[:, 3:4] = zero

    q_spec, k_spec, v_spec = _qkv_specs(seq, nq, True)
    return _call(body, "sb_fwd", (_sds((t, d), F32), _sds((npair, t, 4), F32)), grid=(nb, npair, nq),
                 in_specs=[q_spec, k_spec, v_spec],
                 out_specs=(pl.BlockSpec((tq, LANES), lambda b, p, i: (b * nq + i, p)),
                            pl.BlockSpec((None, tq, 4), lambda b, p, i: (p, b * nq + i, 0))), vmem=VMEM_BIG)(qkv, qkv, qkv)


def _sb_bwd(qkv, do, ltot, nb, seq, scatter=()):
    _, t, d = qkv.shape
    npair = d // LANES
    tq = TQ_(seq)
    nq = seq // tq
    scale = HEAD_DIM ** -0.5
    n_comm = len(scatter)

    def body(*refs):
        (q_ref, k_ref, v_ref, do_ref, lt_ref), send, (dqkv_ref,), land, sems, (dk_acc, dv_acc) = _carried(n_comm, 5, 1, refs)
        _scatter_beside(0, n_comm, nb, npair, send, land, sems)
        m0 = _head_masks()
        row = lax.broadcasted_iota(jnp.int32, (tq, tq), 0)
        col = lax.broadcasted_iota(jnp.int32, (tq, tq), 1)
        upto = jnp.where(row <= col, 1.0, 0.0).astype(BF16)
        left_of = jnp.where(row < col, 1.0, 0.0).astype(BF16)
        dk_acc[...] = jnp.zeros_like(dk_acc)
        dv_acc[...] = jnp.zeros_like(dv_acc)

        def q_block(i, _):
            qoff = pl.multiple_of(i * tq, tq)
            qrows = pl.ds(qoff, tq)
            qm = _two_heads(q_ref[qrows, :] * scale, m0)
            dm = _two_heads(do_ref[qrows, :].astype(BF16), m0)
            qmt = _transposed(qm)
            dmt = _transposed(dm)
            ltot = (lt_ref[qrows, 0:1], lt_ref[qrows, 1:2])

            def step(j, carry, diag):
                off = pl.multiple_of(j * tq, tq)
                krows = pl.ds(off, tq)
                kb = k_ref[krows, :]
                vb = v_ref[krows, :]
                mask = (col < row) if diag else None
                nxt, dqs = [], []
                dk = jnp.zeros((LANES, tq), F32)
                dv = jnp.zeros((LANES, tq), F32)
                for hh in range(2):
                    cl, ce = carry[2 * hh], carry[2 * hh + 1]
                    lb, l1m = _sb_scores(qm[hh], kb, mask)
                    a = jnp.exp(lb + (ltot[hh] - (_sum_right(l1m, upto) + cl)))
                    if diag:
                        a = jnp.where(mask, a, 0.0)
                    e = lax.dot_general(dm[hh], vb, NT, preferred_element_type=F32) * a
                    before = _sum_right(e, left_of) + ce
                    beta = jnp.exp(lb)
                    dz = e * (1.0 - beta) - before * beta
                    if diag:
                        dz = jnp.where(mask, dz, 0.0)
                    dz_b = dz.astype(BF16)
                    dqs.append(jnp.dot(dz_b, kb, preferred_element_type=F32))
                    dk = dk + jnp.dot(qmt[hh], dz_b, preferred_element_type=F32)
                    dv = dv + jnp.dot(dmt[hh], a.astype(BF16), preferred_element_type=F32)
                    nxt += [cl + jnp.sum(l1m, axis=-1, keepdims=True), ce + jnp.sum(e, axis=-1, keepdims=True)]
                dk_acc[:, krows] += dk
                dv_acc[:, krows] += dv
                return (*nxt, carry[4] + jnp.where(m0, dqs[0], dqs[1]))

            zero = jnp.zeros((tq, 1), F32)
            visited = jnp.max(lt_ref[qrows, 2:3]).astype(jnp.int32)
            carry = _sweep(step, visited, SB_BWD_UNROLL, (zero, zero, zero, zero, jnp.zeros((tq, LANES), F32)), start=i - visited)
            dqkv_ref[0, qrows, :] = (carry[4] * scale).astype(BF16)
            return 0

        lax.fori_loop(0, nq, q_block, 0)
        _store_transposed(dqkv_ref, dk_acc, dv_acc, nq, tq)
        _scatter_beside(1, n_comm, nb, npair, send, land, sems)

    q_spec, k_spec, v_spec = _qkv_specs(seq, nq, False)
    tile = pl.BlockSpec((seq, LANES), lambda b, p: (b, p))
    hbm = pl.BlockSpec(memory_space=pl.ANY)
    return _call(body, "sb_bwd", (_sds((3, t, d), BF16), *_scattered_shapes(scatter)), grid=(nb, npair),
                 in_specs=[q_spec, k_spec, v_spec, tile, pl.BlockSpec((None, seq, 4), lambda b, p: (p, b, 0))] + [hbm] * n_comm,
                 out_specs=(pl.BlockSpec((3, seq, LANES), lambda b, p: (0, b, p)), *[hbm] * n_comm),
                 scratch=(_comm_scratch(n_comm) if n_comm else []) + [pltpu.VMEM((LANES, seq), F32), pltpu.VMEM((LANES, seq), F32)],
                 vmem=VMEM_BIG)(qkv, qkv, qkv, do, ltot, *scatter)


def _qkv_proj(name, h, w):
    t, d = h.shape
    tm = min(TMM, t)
    return _mm(name, h, w, _sds((3, t, d), BF16), (3, t // tm, 1),
               pl.BlockSpec((tm, d), lambda s, i, r: (i, 0)), pl.BlockSpec((d, d), lambda s, i, r: (0, s)),
               pl.BlockSpec((None, tm, d), lambda s, i, r: (s, i, 0)), NN)


def _qkv_dw(name, h, dqkv):
    t, d = h.shape
    tk = min(TMM, t)

    def terms(h_ref, g_ref):
        hv = h_ref[...]
        return [_dot(hv, g_ref[s], TN) for s in range(3)]

    def store(o_ref, acc_ref):
        for s in range(3):
            o_ref[:, s * d:(s + 1) * d] = acc_ref[s].astype(o_ref.dtype)

    return _accumulate_over_tokens(name, (h, dqkv), [pl.BlockSpec((tk, d), lambda r: (r, 0)), pl.BlockSpec((3, tk, d), lambda r: (0, r, 0))],
                                   _sds((d, 3 * d), GRAD_WIRE), pl.BlockSpec((d, 3 * d), lambda r: (0, 0)), (3, d, d), t // tk, terms, store)


def _qkv_dh(name, dqkv, w):
    _, t, d = dqkv.shape
    tm = min(TM, t)

    def body(g_ref, w_ref, o_ref):
        acc = None
        for s in range(3):
            term = _dot(g_ref[s], w_ref[:, s * d:(s + 1) * d], NT)
            acc = term if acc is None else acc + term
        o_ref[...] = acc

    return _call(body, name, _sds((t, d), F32), grid=(t // tm,),
                 in_specs=[pl.BlockSpec((3, tm, d), lambda i: (0, i, 0)), pl.BlockSpec((d, 3 * d), lambda i: (0, 0))],
                 out_specs=pl.BlockSpec((tm, d), lambda i: (i, 0)), vmem=VMEM_BIG)(dqkv, w)


def _glu(pre_block, d):
    return pre_block[:, :d] * jax.nn.sigmoid(pre_block[:, d:])


def _shifted_copies(ext_ref, sh_ref, tt):
    for r in range(1, SUBLANES):
        sh_ref[r - 1] = ext_ref[pl.ds(r, tt + CONV_HALO - SUBLANES), :]


def _rows_from(ext_ref, sh_ref, base, offset, n):
    q, r = divmod(offset, SUBLANES)
    if r == 0:
        return ext_ref[pl.ds(pl.multiple_of(base + offset, SUBLANES), n), :]
    return sh_ref[r - 1, pl.ds(pl.multiple_of(base + q * SUBLANES, SUBLANES), n), :]


def _cv_conv_fwd(pre, dw, dwb, lng, lnb, seq):
    t, d2 = pre.shape
    d = d2 // 2
    tt = min(TT, seq)
    nt = seq // tt
    hb = tt // CONV_HALO

    def body(pre_ref, halo_ref, dw_ref, dwb_ref, lng_ref, lnb_ref, y1_ref, y2_ref, ext_ref, sh_ref):
        i = pl.program_id(1)
        ext_ref[0:CONV_HALO, :] = jnp.where(i == 0, 0.0, _glu(halo_ref[...], d))
        ext_ref[CONV_HALO:, :] = _glu(pre_ref[...], d)
        _shifted_copies(ext_ref, sh_ref, tt)

        acc = jnp.zeros((tt, d), F32) + dwb_ref[...]
        for k in range(CONV_WIDTH):
            acc = acc + _rows_from(ext_ref, sh_ref, 0, CONV_HALO - (CONV_WIDTH - 1) + k, tt) * dw_ref[k:k + 1, :]
        y1_ref[...] = acc
        yc, rstd = _ln_stats(acc)
        y2_ref[...] = _silu(yc * rstd * lng_ref[...] + lnb_ref[...]).astype(BF16)

    vec = pl.BlockSpec((1, d), lambda b, i: (0, 0))
    tile = pl.BlockSpec((tt, d), lambda b, i: (b * nt + i, 0))
    return _call(body, "cv_conv_fwd", (_sds((t, d), F32), _sds((t, d), BF16)), grid=(t // seq, nt),
                 in_specs=[pl.BlockSpec((tt, d2), lambda b, i: (b * nt + i, 0)),
                           pl.BlockSpec((CONV_HALO, d2), lambda b, i: (jnp.maximum((b * nt + i) * hb - 1, 0), 0)),
                           pl.BlockSpec((CONV_HALO, d), lambda b, i: (0, 0)), vec, vec, vec],
                 out_specs=(tile, tile),
                 scratch=[pltpu.VMEM((tt + CONV_HALO, d), F32), pltpu.VMEM((SUBLANES - 1, tt + CONV_HALO - SUBLANES, d), F32)],
                 vmem=VMEM_BIG)(pre, pre, dw, dwb, lng, lnb)


def _cv_norm_bwd(y1, dy2, lng, lnb):
    t, d = y1.shape
    tm = min(TM, t)

    def body(y1_ref, dy2_ref, lng_ref, lnb_ref, dy1_ref, dlg_ref, dlb_ref, dsum_ref):
        @pl.when(pl.program_id(0) == 0)
        def _():
            dlg_ref[...] = jnp.zeros_like(dlg_ref)
            dlb_ref[...] = jnp.zeros_like(dlb_ref)
            dsum_ref[...] = jnp.zeros_like(dsum_ref)

        yc, rstd = _ln_stats(y1_ref[...])
        yhat = yc * rstd
        n = yhat * lng_ref[...] + lnb_ref[...]
        sg = jax.nn.sigmoid(n)
        dn = dy2_ref[...] * (sg * (1.0 + n * (1.0 - sg)))
        dlg_ref[...] += _colsum(dn * yhat)
        dlb_ref[...] += _colsum(dn)
        dyh = dn * lng_ref[...]
        m1 = jnp.mean(dyh, axis=-1, keepdims=True)
        m2 = jnp.mean(dyh * yhat, axis=-1, keepdims=True)
        dy1 = rstd * (dyh - m1 - yhat * m2)
        dy1_ref[...] = dy1
        dsum_ref[...] += _colsum(dy1)

    return _call(body, "cv_norm_bwd", (_sds((t, d), F32), _sds((1, d), F32), _sds((1, d), F32), _sds((1, d), F32)),
                 grid=(t // tm,), in_specs=[_tile_spec(tm, d), _tile_spec(tm, d), _vec_spec(d), _vec_spec(d)],
                 out_specs=(_tile_spec(tm, d), _vec_spec(d), _vec_spec(d), _vec_spec(d)))(y1, dy2, lng, lnb)


def _cv_conv_bwd(pre, dy1, dw, seq):
    t, d2 = pre.shape
    d = d2 // 2
    tt = min(TT // 2, seq)
    nt = seq // tt
    hb = tt // CONV_HALO
    last_halo = t // CONV_HALO - 1

    def body(pre_ref, halo_ref, dy_ref, dyn_ref, dw_ref, dpre_ref, ddw_ref, dbin_ref, ext_ref, dext_ref, sh_ref, dsh_ref):
        b, i = pl.program_id(0), pl.program_id(1)

        @pl.when((b == 0) & (i == 0))
        def _():
            ddw_ref[...] = jnp.zeros_like(ddw_ref)
            dbin_ref[...] = jnp.zeros_like(dbin_ref)

        pv = pre_ref[...]
        ext_ref[0:CONV_HALO, :] = jnp.where(i == 0, 0.0, _glu(halo_ref[...], d))
        ext_ref[CONV_HALO:, :] = _glu(pv, d)
        dyv = dy_ref[...]
        dext_ref[0:tt, :] = dyv
        dext_ref[tt:, :] = jnp.where(i == nt - 1, 0.0, dyn_ref[...])
        _shifted_copies(ext_ref, sh_ref, tt)
        _shifted_copies(dext_ref, dsh_ref, tt)
        nrows = tt // CONV_ROWS

        def input_grad(c, _):
            r0 = pl.multiple_of(c * CONV_ROWS, CONV_ROWS)
            dy0 = jnp.zeros((CONV_ROWS, d), F32)
            for k in range(CONV_WIDTH):
                dy0 = dy0 + _rows_from(dext_ref, dsh_ref, r0, CONV_WIDTH - 1 - k, CONV_ROWS) * dw_ref[k:k + 1, :]
            rows = pl.ds(r0, CONV_ROWS)
            a = pre_ref[rows, :d]
            sg = jax.nn.sigmoid(pre_ref[rows, d:])
            da = dy0 * sg
            dg = dy0 * a * sg * (1.0 - sg)
            dpre_ref[rows, :d] = da.astype(BF16)
            dpre_ref[rows, d:] = dg.astype(BF16)
            dbin_ref[:, :d] += _colsum(da)
            dbin_ref[:, d:] += _colsum(dg)
            return 0

        lax.fori_loop(0, nrows, input_grad, 0)

        for k in range(CONV_WIDTH):
            def tap_grad(c, part, k=k):
                r0 = pl.multiple_of(c * CONV_ROWS, CONV_ROWS)
                prod = dy_ref[pl.ds(r0, CONV_ROWS), :] * _rows_from(ext_ref, sh_ref, r0, CONV_HALO - (CONV_WIDTH - 1) + k, CONV_ROWS)
                return part + jnp.sum(prod.reshape(CONV_ROWS // SUBLANES, SUBLANES, d), axis=0)

            ddw_ref[k:k + 1, :] += _colsum(lax.fori_loop(0, nrows, tap_grad, jnp.zeros((SUBLANES, d), F32)))

    return _call(body, "cv_conv_bwd", (_sds((t, d2), BF16), _sds((CONV_HALO, d), F32), _sds((1, d2), F32)), grid=(t // seq, nt),
                 in_specs=[pl.BlockSpec((tt, d2), lambda b, i: (b * nt + i, 0)),
                           pl.BlockSpec((CONV_HALO, d2), lambda b, i: (jnp.maximum((b * nt + i) * hb - 1, 0), 0)),
                           pl.BlockSpec((tt, d), lambda b, i: (b * nt + i, 0)),
                           pl.BlockSpec((CONV_HALO, d), lambda b, i: (jnp.minimum((b * nt + i + 1) * hb, last_halo), 0)),
                           pl.BlockSpec((CONV_HALO, d), lambda b, i: (0, 0))],
                 out_specs=(pl.BlockSpec((tt, d2), lambda b, i: (b * nt + i, 0)),
                            pl.BlockSpec((CONV_HALO, d), lambda b, i: (0, 0)), pl.BlockSpec((1, d2), lambda b, i: (0, 0))),
                 scratch=[pltpu.VMEM((tt + CONV_HALO, d), F32), pltpu.VMEM((tt + CONV_HALO, d), F32),
                          pltpu.VMEM((SUBLANES - 1, tt + CONV_HALO - SUBLANES, d), F32),
                          pltpu.VMEM((SUBLANES - 1, tt + CONV_HALO - SUBLANES, d), F32)],
                 vmem=VMEM_BIG)(pre, pre, dy1, dy1, dw)


def _adamw(name, w, m, v, g=None, parts=None):
    rows, cols = w.shape
    tr = rows
    for cand in ((512,) if parts is None else ()) + (256, 128, 64, 32, 16, 8):
        if rows % cand == 0 and rows > cand:
            tr = cand
            break
    bc1 = 1.0 - ADAM_B1 ** ADAM_STEP
    bc2 = 1.0 - ADAM_B2 ** ADAM_STEP

    def body(w_ref, m_ref, v_ref, g_ref, go_ref, d_ref, mo_ref, vo_ref):
        if parts is None:
            gv = g_ref[...]
        else:
            gv = g_ref[0].astype(F32)
            for s in range(1, N_DEV):
                gv = gv + g_ref[s].astype(F32)
        mn = ADAM_B1 * m_ref[...] + (1.0 - ADAM_B1) * gv
        vn = ADAM_B2 * v_ref[...] + (1.0 - ADAM_B2) * (gv * gv)
        m_hat = mn / bc1
        v_hat = vn / bc2
        go_ref[...] = gv
        d_ref[...] = -ADAM_LR * (m_hat / (jnp.sqrt(v_hat) + ADAM_EPS) + ADAM_WD * w_ref[...])
        mo_ref[...] = mn
        vo_ref[...] = vn

    blk = pl.BlockSpec((tr, cols), lambda i: (i, 0))
    g_in, g_spec = (g, blk) if parts is None else (parts, pl.BlockSpec((N_DEV, tr, cols), lambda i: (0, i, 0)))
    out = _sds((rows, cols), F32)
    return _call(body, name, (out, out, out, out), grid=(rows // tr,), in_specs=[blk, blk, blk, g_spec],
                 out_specs=(blk, blk, blk, blk), vmem=VMEM_BIG)(w, m, v, g_in)


def _pad_rows(a, rows):
    return jnp.pad(a, ((0, rows - a.shape[0]), (0, 0)))


def _full_cols(gathered, n):
    k = gathered.shape[2]
    return jnp.transpose(gathered[0], (1, 0, 2)).reshape(k, N_DEV * n)


def _col_blocks(full, n):
    k = full.shape[0]
    return jnp.transpose(full.reshape(k, N_DEV, n), (1, 0, 2))[None]


def kernel(x, c, mod_w, mod_b, ln1_g, ln1_b, ln2_g, ln2_b, ffn_w_in, ffn_w_out, gm_w_in, gm_b_in, gm_ln_g, gm_ln_b, gm_w_s, gm_b_s, gm_w_out, fox_w_in, fox_b_f, fox_w_out, sb_w_in, sb_w_out, cv_w_in, cv_b_in, cv_dw, cv_dw_b, cv_ln_g, cv_ln_b, cv_w_out, cv_b_out, loss_target, m_mod_w, m_mod_b, m_ln1_g, m_ln1_b, m_ln2_g, m_ln2_b, m_ffn_w_in, m_ffn_w_out, m_gm_w_in, m_gm_b_in, m_gm_ln_g, m_gm_ln_b, m_gm_w_s, m_gm_b_s, m_gm_w_out, m_fox_w_in, m_fox_b_f, m_fox_w_out, m_sb_w_in, m_sb_w_out, m_cv_w_in, m_cv_b_in, m_cv_dw, m_cv_dw_b, m_cv_ln_g, m_cv_ln_b, m_cv_w_out, m_cv_b_out, v_mod_w, v_mod_b, v_ln1_g, v_ln1_b, v_ln2_g, v_ln2_b, v_ffn_w_in, v_ffn_w_out, v_gm_w_in, v_gm_b_in, v_gm_ln_g, v_gm_ln_b, v_gm_w_s, v_gm_b_s, v_gm_w_out, v_fox_w_in, v_fox_b_f, v_fox_w_out, v_sb_w_in, v_sb_w_out, v_cv_w_in, v_cv_b_in, v_cv_dw, v_cv_dw_b, v_cv_ln_g, v_cv_ln_b, v_cv_w_out, v_cv_b_out):
    weights = dict(mod_w=mod_w, mod_b=mod_b, ln1_g=ln1_g, ln1_b=ln1_b, ln2_g=ln2_g, ln2_b=ln2_b, ffn_w_in=ffn_w_in, ffn_w_out=ffn_w_out, gm_w_in=gm_w_in, gm_b_in=gm_b_in, gm_ln_g=gm_ln_g, gm_ln_b=gm_ln_b, gm_w_s=gm_w_s, gm_b_s=gm_b_s, gm_w_out=gm_w_out, fox_w_in=fox_w_in, fox_b_f=fox_b_f, fox_w_out=fox_w_out, sb_w_in=sb_w_in, sb_w_out=sb_w_out, cv_w_in=cv_w_in, cv_b_in=cv_b_in, cv_dw=cv_dw, cv_dw_b=cv_dw_b, cv_ln_g=cv_ln_g, cv_ln_b=cv_ln_b, cv_w_out=cv_w_out, cv_b_out=cv_b_out)
    mom1 = dict(mod_w=m_mod_w, mod_b=m_mod_b, ln1_g=m_ln1_g, ln1_b=m_ln1_b, ln2_g=m_ln2_g, ln2_b=m_ln2_b, ffn_w_in=m_ffn_w_in, ffn_w_out=m_ffn_w_out, gm_w_in=m_gm_w_in, gm_b_in=m_gm_b_in, gm_ln_g=m_gm_ln_g, gm_ln_b=m_gm_ln_b, gm_w_s=m_gm_w_s, gm_b_s=m_gm_b_s, gm_w_out=m_gm_w_out, fox_w_in=m_fox_w_in, fox_b_f=m_fox_b_f, fox_w_out=m_fox_w_out, sb_w_in=m_sb_w_in, sb_w_out=m_sb_w_out, cv_w_in=m_cv_w_in, cv_b_in=m_cv_b_in, cv_dw=m_cv_dw, cv_dw_b=m_cv_dw_b, cv_ln_g=m_cv_ln_g, cv_ln_b=m_cv_ln_b, cv_w_out=m_cv_w_out, cv_b_out=m_cv_b_out)
    mom2 = dict(mod_w=v_mod_w, mod_b=v_mod_b, ln1_g=v_ln1_g, ln1_b=v_ln1_b, ln2_g=v_ln2_g, ln2_b=v_ln2_b, ffn_w_in=v_ffn_w_in, ffn_w_out=v_ffn_w_out, gm_w_in=v_gm_w_in, gm_b_in=v_gm_b_in, gm_ln_g=v_gm_ln_g, gm_ln_b=v_gm_ln_b, gm_w_s=v_gm_w_s, gm_b_s=v_gm_b_s, gm_w_out=v_gm_w_out, fox_w_in=v_fox_w_in, fox_b_f=v_fox_b_f, fox_w_out=v_fox_w_out, sb_w_in=v_sb_w_in, sb_w_out=v_sb_w_out, cv_w_in=v_cv_w_in, cv_b_in=v_cv_b_in, cv_dw=v_cv_dw, cv_dw_b=v_cv_dw_b, cv_ln_g=v_cv_ln_g, cv_ln_b=v_cv_ln_b, cv_w_out=v_cv_w_out, cv_b_out=v_cv_b_out)
    names = list(weights)

    nb, seq, d = x.shape
    t = nb * seq
    nl = mod_w.shape[0]
    alpha = (2.0 * nl) ** 0.25
    me = 4 * lax.axis_index("x") + 2 * lax.axis_index("y") + lax.axis_index("c")
    xs = x.reshape(t, d)
    tgt = loss_target.reshape(t, d)
    n_mod = mod_w.shape[2]
    n_ffn = ffn_w_in.shape[2]
    n_heads = d // HEAD_DIM
    npair = d // LANES

    c_all = _exchange_small(_pad_rows(c, 8), "gather_c", False)[:, :nb].reshape(N_DEV * nb, d)
    mod_b_loc = lax.dynamic_slice_in_dim(mod_b, me * n_mod, n_mod, axis=1)[:, None, :]
    mod_loc = _mod_fwd(c_all, mod_w, mod_b_loc)
    mod_g = _exchange_small(mod_loc.reshape(nl * N_DEV * nb, n_mod), "gather_mod", False)
    mod_all = jnp.transpose(mod_g.reshape(N_DEV, nl, N_DEV * nb, n_mod), (1, 2, 0, 3)).reshape(nl, N_DEV * nb, N_DEV * n_mod)
    mod_me = lax.dynamic_slice_in_dim(mod_all, me * nb, nb, axis=1)
    mods = [[mod_me[l, :, k * d:(k + 1) * d][:, None, :] for k in range(6)] for l in range(nl)]

    assert nl == 4, "the exchange schedule below is written for the four-layer trunk"
    big = ["ffn_w_in", "ffn_w_out", "gm_w_in", "gm_w_out", "fox_w_in", "fox_w_out", "sb_w_in", "sb_w_out", "cv_w_in", "cv_w_out"]
    shard = {n: weights[n].astype(BF16) for n in big if not n.startswith("ffn")}
    for l in range(nl):
        shard["ffn_w_in", l] = ffn_w_in[l:l + 1].astype(BF16)
        shard["ffn_w_out", l] = ffn_w_out[l:l + 1].astype(BF16)
    now = ["gm_w_in", "gm_w_out"]
    with_gm_in = [("ffn_w_in", 0), ("ffn_w_out", 0)]
    with_ffn_0 = ["fox_w_in", "fox_w_out"]
    later = [("ffn_w_in", 1), ("ffn_w_out", 1), "sb_w_in", "sb_w_out", ("ffn_w_in", 2), ("ffn_w_out", 2),
             "cv_w_in", "cv_w_out", ("ffn_w_in", 3), ("ffn_w_out", 3)]
    gathered = dict(zip(now, _gather_weights([shard[n] for n in now])))
    w_ffn_out_rows = lambda l: gathered["ffn_w_out", l].reshape(N_DEV // 2, n_ffn, d)
    sq = lambda n: gathered[n].reshape(d, d)
    cvp = d // N_DEV
    cv_small = jnp.concatenate([_pad_rows(cv_dw[0], CONV_HALO), cv_dw_b, cv_ln_g, cv_ln_b, cv_b_out,
                                cv_b_in.reshape(2, cvp), jnp.zeros((2, cvp), F32)], axis=0)
    cv_all = _exchange_small(cv_small, "gather_cv_small", False)
    cv_rows = jnp.transpose(cv_all, (1, 0, 2)).reshape(cv_small.shape[0], d)
    cv_dw_f, cv_dwb_f, cv_lng_f, cv_lnb_f, cv_bout_f = (cv_rows[:CONV_HALO], cv_rows[32:33], cv_rows[33:34], cv_rows[34:35], cv_rows[35:36])
    cv_bin_f = cv_all[:, 36:38, :].reshape(1, 2 * d)

    saved = []
    h = _modulate(xs, mods[0][1], mods[0][0], seq)
    xin = xs
    for l in range(nl):
        kind = l % 4
        sv = dict(x=xin, h=h)
        if kind == 0:
            pre, arrived = _proj_cols("gm_in", h, gathered["gm_w_in"], gm_w_in.shape[2], bias=gm_b_in,
                                      beside=("gather", [shard[n] for n in with_gm_in]))
            gathered.update(zip(with_gm_in, arrived))
            yv = _gm_spatial_fwd(pre, gm_ln_g, gm_ln_b, gm_w_s[0], jnp.transpose(gm_b_s[0]), seq)
            y = _mm_plain("gm_out", yv, sq("gm_w_out"), NN)
            sv.update(pre=pre, yv=yv)
        elif kind == 1:
            qkv = _qkv_proj("fox_qkv", h, fox_qkv_w)
            ft = _mm("fox_gate_proj", fox_f_wt, h, _sds((n_heads, t), F32), (t // min(TMM, t), 1),
                     pl.BlockSpec((n_heads, d), lambda i, r: (0, 0)), pl.BlockSpec((min(TMM, t), d), lambda i, r: (i, 0)),
                     pl.BlockSpec((n_heads, min(TMM, t)), lambda i, r: (0, i)), NT)
            b_f = jnp.transpose(fox_b_f)
            frow_p = _fox_gate_fwd(ft, b_f, seq).reshape(npair, 2, t)
            o, lse, *arrived = _fox_fwd(qkv, frow_p, nb, seq, gather=[shard[n] for n in later])
            gathered.update(zip(later, arrived))
            sb_qkv_w = _full_cols(gathered["sb_w_in"], sb_w_in.shape[2])
            y = _mm_plain("fox_out", o, sq("fox_w_out"), NN)
            sv.update(qkv=qkv, ft=ft, b_f=b_f, frow=frow_p, o=o, lse=lse)
        elif kind == 2:
            qkv = _qkv_proj("sb_qkv", h, sb_qkv_w)
            o, ltot = _sb_fwd(qkv, nb, seq)
            y = _mm_plain("sb_out", o, sq("sb_w_out"), NN)
            sv.update(qkv=qkv, o=o, ltot=ltot)
        else:
            pre = _proj_cols("cv_in", h, gathered["cv_w_in"], cv_w_in.shape[2], bias=cv_bin_f)
            y1, y2 = _cv_conv_fwd(pre, cv_dw_f, cv_dwb_f, cv_lng_f, cv_lnb_f, seq)
            y = _mm_plain("cv_out", y2, sq("cv_w_out"), NN, bias=cv_bout_f)
            sv.update(pre=pre, y1=y1, y2=y2)
        x1, h2 = _lnres_fwd(xin, y, mods[l][2], ln1_g[l:l + 1], ln1_b[l:l + 1], alpha, seq, nxt=(mods[l][4], mods[l][3]))
        if l == 0:
            (hg, hu, act), arrived = _ffn_in(h2, gathered["ffn_w_in", l], 0, beside=("gather", [shard[n] for n in with_ffn_0]))
            gathered.update(zip(with_ffn_0, arrived))
            fox_full = _full_cols(gathered["fox_w_in"], fox_w_in.shape[2])
            fox_qkv_w, fox_f_wt = fox_full[:, :3 * d], jnp.transpose(fox_full[:, 3 * d:])
        else:
            hg, hu, act = _ffn_in(h2, gathered["ffn_w_in", l], 0)
        y2f = _ffn_out(act, w_ffn_out_rows(l), 0)
        sv.update(y=y, x1=x1, h2=h2, hg=hg, hu=hu, act=act, y2f=y2f)
        if l + 1 < nl:
            xin, h = _lnres_fwd(x1, y2f, mods[l][5], ln2_g[l:l + 1], ln2_b[l:l + 1], alpha, seq, nxt=(mods[l + 1][1], mods[l + 1][0]))
        else:
            xin = _lnres_fwd(x1, y2f, mods[l][5], ln2_g[l:l + 1], ln2_b[l:l + 1], alpha, seq)
        saved.append(sv)

    dx, sq_err = _loss_head(xin, tgt)
    loss = lax.psum(0.5 * jnp.sum(sq_err) / d, ("x", "y", "c"))

    small = {}
    bigg = {}
    recv = {}
    dmod_parts = [dict() for _ in range(nl)]
    pending = None
    d_ln = dict(ln1_g=[None] * nl, ln1_b=[None] * nl, ln2_g=[None] * nl, ln2_b=[None] * nl)
    beside_sb = [("ffn_w_in", 3), ("ffn_w_out", 3), "cv_w_in", "cv_w_out", ("ffn_w_in", 2), ("ffn_w_out", 2)]
    beside_fox = ["sb_w_in", "sb_w_out", ("ffn_w_in", 1), ("ffn_w_out", 1)]
    beside_dact_0 = ["fox_w_in", "fox_w_out"]
    beside_dh_0 = [("ffn_w_in", 0)]
    beside_gm = [("ffn_w_out", 0)]
    at_end = ["gm_w_in", "gm_w_out"]
    for l in reversed(range(nl)):
        sv = saved[l]
        kind = l % 4
        if pending is None:
            dxr, dy2, dlg, dlb, _, dgate2 = _lnres_bwd(dx, sv["x1"], sv["y2f"], mods[l][5], ln2_g[l:l + 1], alpha, seq)
        else:
            dxr, dy2, dlg, dlb, _, dgate2, dsc1, dsh1 = _lnres_bwd(pending[1], sv["x1"], sv["y2f"], mods[l][5], ln2_g[l:l + 1], alpha, seq,
                                                                   through=(pending[0], pending[2], ln2_b[l:l + 1]))
            dmod_parts[l + 1].update(sc1=dsc1, sh1=dsh1)
        d_ln["ln2_g"][l], d_ln["ln2_b"][l] = dlg, dlb
        if l == 0:
            (dg_, du_), landed = _ffn_dact(dy2, w_ffn_out_rows(l), sv["hg"], sv["hu"], 0, beside=("scatter", [bigg[n] for n in beside_dact_0]))
            recv.update(zip(beside_dact_0, landed))
        else:
            dg_, du_ = _ffn_dact(dy2, w_ffn_out_rows(l), sv["hg"], sv["hu"], 0)
        dwg, dwu, dwo = _ffn_bwd_weights(sv["h2"], dy2, sv["act"], dg_, du_)
        bigg["ffn_w_in", l] = jnp.concatenate([dwg, dwu], axis=0)[None]
        bigg["ffn_w_out", l] = dwo.reshape(1, N_DEV, n_ffn // 2, d)
        if l == 0:
            dh2, landed = _ffn_dh(dg_, du_, gathered["ffn_w_in", l], 0, beside=("scatter", [bigg[n] for n in beside_dh_0]))
            recv.update(zip(beside_dh_0, landed))
        else:
            dh2 = _ffn_dh(dg_, du_, gathered["ffn_w_in", l], 0)
        dxr, dy, dlg, dlb, dysum, dgate1, dsc2, dsh2 = _lnres_bwd(dxr, sv["x"], sv["y"], mods[l][2], ln1_g[l:l + 1], alpha, seq,
                                                                  through=(dh2, mods[l][4], ln1_b[l:l + 1]))
        d_ln["ln1_g"][l], d_ln["ln1_b"][l] = dlg, dlb
        hh = sv["h"]
        if kind == 0:
            dyv = _mm_plain("gm_out_bwd", dy, sq("gm_w_out"), NT)
            bigg["gm_w_out"] = _mm_plain("gm_out_dw", sv["yv"], dy, TN, GRAD_WIRE).reshape(1, N_DEV, d // N_DEV, d)
            (dpre, dws, dbst, dlng, dlnb, dbin), landed = _gm_spatial_bwd(sv["pre"], dyv, gm_ln_g, gm_ln_b, gm_w_s[0], jnp.transpose(gm_b_s[0]), seq,
                                                                          beside=("scatter", [bigg[n] for n in beside_gm]))
            recv.update(zip(beside_gm, landed))
            small.update(gm_w_s=dws[None], gm_b_s=jnp.transpose(dbst)[None], gm_ln_g=dlng, gm_ln_b=dlnb, gm_b_in=dbin)
            bigg["gm_w_in"] = _grad_cols("gm_in_dw", hh, dpre, gm_w_in.shape[2])
            dh = _back_cols("gm_in_bwd", dpre, gathered["gm_w_in"], gm_w_in.shape[2])
        elif kind == 1:
            do = _mm_plain("fox_out_bwd", dy, sq("fox_w_out"), NT)
            bigg["fox_w_out"] = _mm_plain("fox_out_dw", sv["o"], dy, TN, GRAD_WIRE).reshape(1, N_DEV, d // N_DEV, d)
            dqkv, dfr, dfq, *landed = _fox_bwd(sv["qkv"], sv["frow"], sv["o"], do, sv["lse"], nb, seq,
                                               scatter=[bigg[n] for n in beside_fox])
            recv.update(zip(beside_fox, landed))
            dft, dbf = _fox_gate_bwd(sv["ft"], sv["b_f"], dfr.reshape(n_heads, t),
                                     jnp.transpose(dfq, (0, 2, 1)).reshape(n_heads, t), seq)
            small["fox_b_f"] = jnp.transpose(dbf)
            dw_qkv = _qkv_dw("fox_qkv_dw", hh, dqkv)
            tk = min(TMM, t)
            dw_ft = _mm("fox_gate_dw", dft, hh, _sds((n_heads, d), F32), (1, t // tk),
                        pl.BlockSpec((n_heads, tk), lambda j, r: (0, r)), pl.BlockSpec((tk, d), lambda j, r: (r, 0)),
                        pl.BlockSpec((n_heads, d), lambda j, r: (0, 0)), NN)
            bigg["fox_w_in"] = _col_blocks(jnp.concatenate([dw_qkv, jnp.transpose(dw_ft).astype(GRAD_WIRE)], axis=1), fox_w_in.shape[2])
            dh_a = _qkv_dh("fox_qkv_bwd", dqkv, fox_qkv_w)
            tm = min(TMM, t)
            dh = _mm("fox_gate_bwd_h", dft, fox_f_wt, _sds((t, d), F32), (t // tm, 1),
                     pl.BlockSpec((n_heads, tm), lambda i, r: (0, i)), pl.BlockSpec((n_heads, d), lambda i, r: (0, 0)),
                     pl.BlockSpec((tm, d), lambda i, r: (i, 0)), TN, (dh_a,), (pl.BlockSpec((tm, d), lambda i, r: (i, 0)),), _add)
        elif kind == 2:
            do = _mm_plain("sb_out_bwd", dy, sq("sb_w_out"), NT)
            bigg["sb_w_out"] = _mm_plain("sb_out_dw", sv["o"], dy, TN, GRAD_WIRE).reshape(1, N_DEV, d // N_DEV, d)
            dqkv, *landed = _sb_bwd(sv["qkv"], do, sv["ltot"], nb, seq, scatter=[bigg[n] for n in beside_sb])
            recv.update(zip(beside_sb, landed))
            bigg["sb_w_in"] = _col_blocks(_qkv_dw("sb_qkv_dw", hh, dqkv), sb_w_in.shape[2])
            dh = _qkv_dh("sb_qkv_bwd", dqkv, sb_qkv_w)
        else:
            dy2c = _mm_plain("cv_out_bwd", dy, sq("cv_w_out"), NT)
            bigg["cv_w_out"] = _mm_plain("cv_out_dw", sv["y2"], dy, TN, GRAD_WIRE).reshape(1, N_DEV, d // N_DEV, d)
            dy1, dlng, dlnb, ddwb = _cv_norm_bwd(sv["y1"], dy2c, cv_lng_f, cv_lnb_f)
            dpre, ddw, dbin = _cv_conv_bwd(sv["pre"], dy1, cv_dw_f, seq)
            small.update(cv_b_out=dysum, cv_ln_g=dlng, cv_ln_b=dlnb, cv_dw_b=ddwb, cv_dw=ddw[:CONV_WIDTH], cv_b_in=dbin)
            bigg["cv_w_in"] = _grad_cols("cv_in_dw", hh, dpre, cv_w_in.shape[2])
            dh = _back_cols("cv_in_bwd", dpre, gathered["cv_w_in"], cv_w_in.shape[2])
        pending = (dh, dxr, mods[l][1])
        dmod_parts[l].update(g1=dgate1, sh2=dsh2, sc2=dsc2, g2=dgate2)
    dx, dsc1, dsh1 = _modulate_bwd(pending[0], pending[1], saved[0]["x"], pending[2], seq)
    dmod_parts[0].update(sc1=dsc1, sh1=dsh1)
    dmods = [jnp.concatenate([p["sh1"], p["sc1"], p["g1"], p["sh2"], p["sc2"], p["g2"]], axis=2)[:, 0, :] for p in dmod_parts]
    grad_x = dx.reshape(nb, seq, d)
    for n in d_ln:
        small[n] = jnp.concatenate(d_ln[n], axis=0)

    dmod_rows = jnp.stack(dmods).reshape(nl * nb, 6 * d)
    dmod_g = _exchange_small(_pad_rows(dmod_rows, 8 * ((nl * nb + 7) // 8)), "gather_dmod", False)[:, :nl * nb]
    dmod_all = jnp.transpose(dmod_g.reshape(N_DEV, nl, nb, 6 * d), (1, 0, 2, 3)).reshape(nl, N_DEV * nb, 6 * d)
    dmod_loc = lax.dynamic_slice_in_dim(dmod_all, me * n_mod, n_mod, axis=2)
    g_mod_w, g_mod_b = _mod_bwd(c_all, dmod_loc, dmod_all)
    grads = dict(mod_w=g_mod_w, mod_b=g_mod_b[:, 0, :])

    rep = ["ln1_g", "ln1_b", "ln2_g", "ln2_b", "gm_b_in", "gm_ln_g", "gm_ln_b", "gm_w_s", "gm_b_s", "fox_b_f"]
    cvs = ["cv_b_in", "cv_dw", "cv_dw_b", "cv_ln_g", "cv_ln_b", "cv_b_out"]

    def rows_of(a):
        flat = a.reshape(-1)
        pad = (-flat.shape[0]) % d
        return jnp.pad(flat, (0, pad)).reshape(-1, d)

    pack_rows = [rows_of(small[n]) for n in rep + cvs]
    counts = [r.shape[0] for r in pack_rows]
    total = sum(counts)
    pack = _pad_rows(jnp.concatenate(pack_rows, axis=0), 8 * ((total + 7) // 8))
    summed = _exchange_small(pack, "allreduce_small", True)
    offs = [sum(counts[:i]) for i in range(len(counts))]
    rep_rows = sum(counts[:len(rep)])
    for n, o_, cnt in zip(rep + cvs, offs, counts):
        full = summed[o_:o_ + cnt].reshape(-1)
        if n in rep:
            grads[n] = full[:weights[n].size].reshape(weights[n].shape)
        else:
            wshape = weights[n].shape
            cols = wshape[-1]
            full = full[:math.prod(wshape[:-1]) * cols * N_DEV].reshape(wshape[:-1] + (cols * N_DEV,))
            grads[n] = lax.dynamic_slice_in_dim(full, me * cols, cols, axis=full.ndim - 1)

    recv.update(zip(at_end, _scatter_grads([bigg[n] for n in at_end])))
    for n in ("ffn_w_in", "ffn_w_out"):
        recv[n] = jnp.concatenate([recv[n, l] for l in range(nl)], axis=1)

    outs = {}

    def view2(a):
        return a.reshape(-1, a.shape[-1])

    for n in big:
        w2 = view2(weights[n])
        res = _adamw("adamw_" + n, w2, view2(mom1[n]), view2(mom2[n]), parts=recv[n].reshape((N_DEV,) + w2.shape))
        outs[n] = [r.reshape(weights[n].shape) for r in res]
    res = _adamw("adamw_mod_w", view2(mod_w), view2(m_mod_w), view2(v_mod_w), g=view2(grads["mod_w"]))
    outs["mod_w"] = [r.reshape(mod_w.shape) for r in res]
    rp = lambda src: _pad_rows(jnp.concatenate([rows_of(src[n]) for n in rep], axis=0), 8 * ((rep_rows + 7) // 8))
    res = _adamw("adamw_replicated", rp(weights), rp(mom1), rp(mom2), g=rp(grads))
    for n, o_, cnt in zip(rep, offs, counts):
        outs[n] = [r[o_:o_ + cnt].reshape(-1)[:weights[n].size].reshape(weights[n].shape) for r in res]
    cv_cols = weights["cv_b_out"].shape[-1]
    cp = lambda src: jnp.concatenate([src[n].reshape(-1, cv_cols) for n in cvs], axis=0)
    cv_cnt = [weights[n].size // cv_cols for n in cvs]
    cv_tot = sum(cv_cnt)
    cpp = lambda src: _pad_rows(cp(src), 8 * ((cv_tot + 7) // 8))
    res = _adamw("adamw_cv_small", cpp(weights), cpp(mom1), cpp(mom2), g=cpp(grads))
    o_ = 0
    for n, cnt in zip(cvs, cv_cnt):
        outs[n] = [r[o_:o_ + cnt].reshape(weights[n].shape) for r in res]
        o_ += cnt
    res = _adamw("adamw_mod_b", mod_b, m_mod_b, v_mod_b, g=grads["mod_b"])
    outs["mod_b"] = list(res)

    return (loss, grad_x, *[outs[n][0] for n in names], *[outs[n][1] for n in names],
            *[outs[n][2] for n in names], *[outs[n][3] for n in names])
```

```python
import functools
import math

import jax
import jax.numpy as jnp
from jax import lax
from jax.experimental import pallas as pl
from jax.experimental.pallas import tpu as pltpu

F32 = jnp.float32
BF16 = jnp.bfloat16
MESH = pl.DeviceIdType.MESH

N_DEV = 8
HEAD_DIM = 64
LANES = 128
SUBLANES = 8
GM_CHUNK = 128
GM_GROUPS = 8
CONV_WIDTH = 31
CONV_HALO = 32
CONV_ROWS = 32
LN_EPS = 1e-5
NEG_INF = -1e30
SB_DEAD = -100.0
GRAD_WIRE = jnp.bfloat16
FFN_KEEP = jnp.bfloat16

ADAM_LR = 0.001
ADAM_B1 = 0.9
ADAM_B2 = 0.999
ADAM_EPS = 1e-08
ADAM_WD = 0.01
ADAM_STEP = 10

TM = 512
TMM = 1024
TQ = 256
FOX_FWD_UNROLL = 8
FOX_BWD_UNROLL = 4
SB_BWD_UNROLL = 2
TT = 512
VMEM_BIG = 56 * 1024 * 1024

NN = (((1,), (0,)), ((), ()))
NT = (((1,), (1,)), ((), ()))
TN = (((0,), (0,)), ((), ()))


def _call(body, name, out_shape, grid=None, in_specs=None, out_specs=None, scratch=(), vmem=None):
    params = {}
    if grid is not None:
        params["dimension_semantics"] = ("arbitrary",) * len(grid)
    if vmem is not None:
        params["vmem_limit_bytes"] = vmem
    kw = {}
    if grid is not None:
        kw["grid"] = grid
    if in_specs is not None:
        kw["in_specs"] = in_specs
    if out_specs is not None:
        kw["out_specs"] = out_specs
    return pl.pallas_call(body, name=name, out_shape=out_shape, scratch_shapes=list(scratch),
                          compiler_params=pltpu.CompilerParams(**params), **kw)


def _sds(shape, dtype):
    return jax.ShapeDtypeStruct(tuple(shape), dtype)


def _dot(a, b, dims=NN):
    return lax.dot_general(a.astype(BF16), b.astype(BF16), dims, preferred_element_type=F32)


def _split3(x):
    h1 = x.astype(BF16)
    r1 = x - h1.astype(F32)
    h2 = r1.astype(BF16)
    h3 = (r1 - h2.astype(F32)).astype(BF16)
    return h1, h2, h3


def _dot_exact(x, m, dims=NN):
    h1, h2, h3 = _split3(x)
    d = lambda h: lax.dot_general(h, m, dims, preferred_element_type=F32)
    return (d(h1) + d(h2)) + d(h3)


def _dot_exact_rhs(m, x, dims=NN):
    h1, h2, h3 = _split3(x)
    d = lambda h: lax.dot_general(m, h, dims, preferred_element_type=F32)
    return (d(h1) + d(h2)) + d(h3)


def _silu(x):
    return x * jax.nn.sigmoid(x)


def _gelu(x):
    return 0.5 * x * (1.0 + lax.erf(x * (2.0 ** -0.5)))


def _gelu_grad(x):
    return 0.5 * (1.0 + lax.erf(x * (2.0 ** -0.5))) + x * jnp.exp(-0.5 * x * x) * ((2.0 * math.pi) ** -0.5)


def _log_sigmoid(z):
    return jnp.minimum(z, 0.0) - jnp.log(1.0 + jnp.exp(-jnp.abs(z)))


def _ln_stats(r):
    mu = jnp.mean(r, axis=-1, keepdims=True)
    rc = r - mu
    var = jnp.mean(rc * rc, axis=-1, keepdims=True)
    return rc, lax.rsqrt(var + LN_EPS)


def _colsum(x):
    return jnp.sum(x, axis=0, keepdims=True)


def _peers():
    mx, my, mc = lax.axis_index("x"), lax.axis_index("y"), lax.axis_index("c")
    me = 4 * mx + 2 * my + mc
    out = []
    for k in range(1, N_DEV):
        px = 1 - mx if (k >> 2) & 1 else mx
        py = 1 - my if (k >> 1) & 1 else my
        pc = 1 - mc if k & 1 else mc
        out.append(((px, py, pc), 4 * px + 2 * py + pc))
    return me, out


def _exchange_small(x, name, reduce):
    rows, cols = x.shape

    def body(x_ref, o_ref, *rest):
        if reduce:
            land, send_sems, recv_sems, local_sem = rest
        else:
            land = o_ref
            send_sems, recv_sems, local_sem = rest
        me, peers = _peers()
        mine = pltpu.make_async_copy(x_ref, land.at[me], local_sem)
        mine.start()
        sends = []
        for k, (peer, _) in enumerate(peers):
            cp = pltpu.make_async_remote_copy(src_ref=x_ref, dst_ref=land.at[me], send_sem=send_sems.at[k],
                                              recv_sem=recv_sems.at[k], device_id=peer, device_id_type=MESH)
            cp.start()
            sends.append(cp)
        for k, (peer, blk) in enumerate(peers):
            pltpu.make_async_remote_copy(src_ref=x_ref, dst_ref=land.at[blk], send_sem=send_sems.at[k],
                                         recv_sem=recv_sems.at[k], device_id=peer, device_id_type=MESH).wait_recv()
        for cp in sends:
            cp.wait_send()
        mine.wait()
        if reduce:
            acc = land[0]
            for s in range(1, N_DEV):
                acc = acc + land[s]
            o_ref[...] = acc

    vm = pl.BlockSpec(memory_space=pltpu.VMEM)
    scratch = [pltpu.SemaphoreType.DMA((N_DEV - 1,)), pltpu.SemaphoreType.DMA((N_DEV - 1,)), pltpu.SemaphoreType.DMA]
    if reduce:
        scratch = [pltpu.VMEM((N_DEV, rows, cols), F32)] + scratch
        out = _sds((rows, cols), F32)
    else:
        out = _sds((N_DEV, rows, cols), F32)
    return _call(body, name, out, in_specs=[vm], out_specs=vm, scratch=scratch, vmem=VMEM_BIG)(x)


def _comm_scratch(n):
    return [pltpu.SemaphoreType.DMA((n, N_DEV - 1)), pltpu.SemaphoreType.DMA((n, N_DEV - 1)), pltpu.SemaphoreType.DMA((n,))]


def _gather_stage(stage, ins, outs, send_sems, recv_sems, local_sems):
    n = len(ins)
    mx, my, mc = lax.axis_index("x"), lax.axis_index("y"), lax.axis_index("c")
    here, sibling = (mx, my, mc), (mx, my, 1 - mc)
    chips = [(1 - mx, my), (mx, 1 - my), (1 - mx, 1 - my)]

    def block(px, py, pc):
        return 4 * px + 2 * py + pc

    def copy(a, k, blk, to, src=None):
        dst = outs[a].at[:, blk]
        return pltpu.make_async_remote_copy(src_ref=dst if src is None else src, dst_ref=dst, send_sem=send_sems.at[a, k],
                                            recv_sem=recv_sems.at[a, k], device_id=to, device_id_type=MESH)

    me = block(*here)
    for a in range(n):
        local = pltpu.make_async_copy(ins[a], outs[a].at[:, me], local_sems.at[a])
        first = [copy(a, 0, me, sibling, src=ins[a])] + [copy(a, 1 + j, me, (*chip, mc), src=ins[a]) for j, chip in enumerate(chips)]
        if stage == 0:
            local.start()
            for cp in first:
                cp.start()
        if stage == 1:
            for j, chip in enumerate(chips):
                copy(a, 1 + j, block(*chip, mc), here).wait_recv()
                copy(a, 4 + j, block(*chip, mc), sibling).start()
        if stage == 2:
            copy(a, 0, block(mx, my, 1 - mc), here).wait_recv()
            for j, chip in enumerate(chips):
                copy(a, 4 + j, block(*chip, 1 - mc), here).wait_recv()
            for cp in first:
                cp.wait_send()
            for j, chip in enumerate(chips):
                copy(a, 4 + j, block(*chip, mc), sibling).wait_send()
            local.wait()


def _gather_weights(shards):
    n = len(shards)

    def body(*refs):
        for stage in range(3):
            _gather_stage(stage, refs[:n], refs[n:2 * n], *refs[2 * n:])

    hbm = pl.BlockSpec(memory_space=pl.ANY)
    return _call(body, "gather_weights", _gathered_shapes(shards), in_specs=[hbm] * n, out_specs=[hbm] * n, scratch=_comm_scratch(n))(*shards)


def _gathered_shapes(shards):
    return [_sds((s.shape[0], N_DEV) + s.shape[1:], s.dtype) for s in shards]


def _scatter_grads(grads):
    n = len(grads)

    def body(*refs):
        for stage in range(2):
            _scatter_stage(stage, refs[:n], refs[n:2 * n], *refs[2 * n:])

    hbm = pl.BlockSpec(memory_space=pl.ANY)
    return _call(body, "scatter_grads", _scattered_shapes(grads), in_specs=[hbm] * n, out_specs=[hbm] * n, scratch=_comm_scratch(n))(*grads)


def _scattered_shapes(grads):
    return [_sds((N_DEV, g.shape[0]) + g.shape[2:], g.dtype) for g in grads]


def _run(beside, body, name, out_shape, args, grid, in_specs, out_specs, scratch=(), vmem=None):
    if beside is None:
        return _call(body, name, out_shape, grid=grid, in_specs=in_specs, out_specs=out_specs, scratch=scratch, vmem=vmem)(*args)
    kind, arrays = beside
    n = len(arrays)
    outs = tuple(out_shape) if isinstance(out_shape, (tuple, list)) else (out_shape,)
    ospecs = tuple(out_specs) if isinstance(out_specs, (tuple, list)) else (out_specs,)
    total = math.prod(grid)
    gather = kind == "gather"
    stage_fn = _gather_stage if gather else _scatter_stage
    early = {0: 0, 1: (3 * total) // 4} if gather else {0: 0}

    def carrier(*refs):
        own_in, send, own_out, land, sems, own_scratch = _carried(n, len(in_specs), len(outs), refs)
        at = 0
        for ax, size in enumerate(grid):
            at = at * size + pl.program_id(ax)
        for stage, when in early.items():
            @pl.when(at == when)
            def _(stage=stage):
                stage_fn(stage, send, land, *sems)
        body(*own_in, *own_out, *own_scratch)

        @pl.when(at == total - 1)
        def _():
            stage_fn(2 if gather else 1, send, land, *sems)

    hbm = pl.BlockSpec(memory_space=pl.ANY)
    shapes = _gathered_shapes(arrays) if gather else _scattered_shapes(arrays)
    res = _call(carrier, name, (*outs, *shapes), grid=grid, in_specs=[*in_specs, *[hbm] * n], out_specs=(*ospecs, *[hbm] * n),
                scratch=[*_comm_scratch(n), *scratch], vmem=vmem)(*args, *arrays)
    own = res[:len(outs)]
    return (own if len(own) > 1 else own[0]), list(res[len(outs):])


def _scatter_stage(stage, ins, outs, send_sems, recv_sems, local_sems):
    me, peers = _peers()
    for a in range(len(ins)):
        local = pltpu.make_async_copy(ins[a].at[:, me], outs[a].at[me], local_sems.at[a])
        sends = [pltpu.make_async_remote_copy(src_ref=ins[a].at[:, blk], dst_ref=outs[a].at[me], send_sem=send_sems.at[a, k],
                                              recv_sem=recv_sems.at[a, k], device_id=peer, device_id_type=MESH)
                 for k, (peer, blk) in enumerate(peers)]
        if stage == 0:
            local.start()
            for cp in sends:
                cp.start()
        if stage == 1:
            for k, (peer, blk) in enumerate(peers):
                pltpu.make_async_remote_copy(src_ref=ins[a].at[:, me], dst_ref=outs[a].at[blk], send_sem=send_sems.at[a, k],
                                             recv_sem=recv_sems.at[a, k], device_id=peer, device_id_type=MESH).wait_recv()
            for cp in sends:
                cp.wait_send()
            local.wait()


def _mm(name, a, b, out, grid, a_spec, b_spec, o_spec, dims, extra=(), extra_specs=(), epilogue=None, vmem=VMEM_BIG, beside=None):
    nred = grid[-1]
    red_axis = len(grid) - 1
    acc_shape = tuple(d for d in o_spec.block_shape if d is not None)
    n_extra = len(extra)

    def body(a_ref, b_ref, *rest):
        ex = rest[:n_extra]
        o_ref = rest[n_extra]

        def finish(acc):
            if epilogue is not None:
                acc = epilogue(acc, *[e[...] for e in ex])
            o_ref[...] = acc.astype(o_ref.dtype)

        prod = _dot(a_ref[...], b_ref[...], dims)
        if nred == 1:
            finish(prod)
        else:
            acc_ref = rest[n_extra + 1]
            r = pl.program_id(red_axis)

            @pl.when(r == 0)
            def _():
                acc_ref[...] = prod

            @pl.when(r > 0)
            def _():
                acc_ref[...] += prod

            @pl.when(r == nred - 1)
            def _():
                finish(acc_ref[...])

    scratch = [pltpu.VMEM(acc_shape, F32)] if nred > 1 else []
    return _run(beside, body, name, out, (a, b, *extra), grid, [a_spec, b_spec, *extra_specs], o_spec, scratch, vmem)


def _add(acc, x):
    return acc + x


def _proj_cols(name, h, w, n_slot, bias=None, out_dtype=F32, beside=None):
    t, k = h.shape
    s = w.shape[1]
    tm = min(TM, t)

    def body(h_ref, w_ref, *rest):
        o_ref = rest[-1]
        hv = h_ref[...]
        for j in range(s):
            cols = slice(j * n_slot, (j + 1) * n_slot)
            acc = _dot(hv, w_ref[j])
            if bias is not None:
                acc = acc + rest[0][:, cols]
            o_ref[:, cols] = acc.astype(o_ref.dtype)

    ins, in_specs = [h, w], [pl.BlockSpec((tm, k), lambda i: (i, 0)), pl.BlockSpec((None, s, k, n_slot), lambda i: (0, 0, 0, 0))]
    if bias is not None:
        ins.append(bias)
        in_specs.append(pl.BlockSpec((1, s * n_slot), lambda i: (0, 0)))
    return _run(beside, body, name, _sds((t, s * n_slot), out_dtype), tuple(ins), (t // tm,), in_specs,
                pl.BlockSpec((tm, s * n_slot), lambda i: (i, 0)), vmem=VMEM_BIG)


def _accumulate_over_tokens(name, ins, in_specs, out, o_spec, acc_shape, n_steps, terms, store=None):
    def body(*refs):
        o_ref, acc_ref = refs[len(ins)], refs[len(ins) + 1]
        r = pl.program_id(0)

        @pl.when(r == 0)
        def _():
            acc_ref[...] = jnp.zeros_like(acc_ref)

        for s, prod in enumerate(terms(*refs[:len(ins)])):
            acc_ref[s] += prod

        @pl.when(r == n_steps - 1)
        def _():
            if store is None:
                o_ref[...] = acc_ref[...].reshape(o_ref.shape).astype(o_ref.dtype)
            else:
                store(o_ref, acc_ref)

    return _call(body, name, out, grid=(n_steps,), in_specs=in_specs, out_specs=o_spec,
                 scratch=[pltpu.VMEM(acc_shape, F32)], vmem=VMEM_BIG)(*ins)


def _grad_cols(name, h, g, n_slot):
    t, k = h.shape
    s = g.shape[1] // n_slot
    tk = min(TMM, t)

    def terms(h_ref, g_ref):
        hv = h_ref[...]
        return [_dot(hv, g_ref[:, j * n_slot:(j + 1) * n_slot], TN) for j in range(s)]

    return _accumulate_over_tokens(name, (h, g), [pl.BlockSpec((tk, k), lambda r: (r, 0)), pl.BlockSpec((tk, s * n_slot), lambda r: (r, 0))],
                                   _sds((1, s, k, n_slot), GRAD_WIRE), pl.BlockSpec((1, s, k, n_slot), lambda r: (0, 0, 0, 0)),
                                   (s, k, n_slot), t // tk, terms)


def _back_cols(name, g, w, n_slot):
    t = g.shape[0]
    s, k = w.shape[1], w.shape[2]
    tm = min(TM, t)

    def body(g_ref, w_ref, o_ref):
        acc = None
        for j in range(s):
            term = _dot(g_ref[:, j * n_slot:(j + 1) * n_slot], w_ref[j], NT)
            acc = term if acc is None else acc + term
        o_ref[...] = acc

    return _call(body, name, _sds((t, k), F32), grid=(t // tm,),
                 in_specs=[pl.BlockSpec((tm, s * n_slot), lambda i: (i, 0)), pl.BlockSpec((None, s, k, n_slot), lambda i: (0, 0, 0, 0))],
                 out_specs=pl.BlockSpec((tm, k), lambda i: (i, 0)), vmem=VMEM_BIG)(g, w)


def _mm_plain(name, a, b, dims, out_dtype=F32, bias=None):
    if dims == TN:
        t, k = a.shape
        n = b.shape[1]
        tk = min(TMM, t)
        return _mm(name, a, b, _sds((k, n), out_dtype), (1, t // tk),
                   pl.BlockSpec((tk, k), lambda j, r: (r, 0)), pl.BlockSpec((tk, n), lambda j, r: (r, 0)),
                   pl.BlockSpec((k, n), lambda j, r: (0, 0)), TN)
    t = a.shape[0]
    tm = min(TMM, t)
    n = b.shape[1] if dims == NN else b.shape[0]
    extra, especs, epi = (), (), None
    if bias is not None:
        extra, especs, epi = (bias,), (pl.BlockSpec((1, n), lambda i, r: (0, 0)),), _add
    return _mm(name, a, b, _sds((t, n), out_dtype), (t // tm, 1),
               pl.BlockSpec((tm, a.shape[1]), lambda i, r: (i, 0)), pl.BlockSpec(b.shape, lambda i, r: (0, 0)),
               pl.BlockSpec((tm, n), lambda i, r: (i, 0)), dims, extra, especs, epi)


def _mod_fwd(c_all, mod_w, mod_b_loc):
    nl, d, n = mod_w.shape
    nb = c_all.shape[0]

    def body(c_ref, w_ref, b_ref, o_ref):
        o_ref[...] = _dot(_silu(c_ref[...]), w_ref[...]) + b_ref[...]

    return _call(body, "mod_fwd", _sds((nl, nb, n), F32), grid=(nl,),
                 in_specs=[pl.BlockSpec((nb, d), lambda l: (0, 0)), pl.BlockSpec((None, d, n), lambda l: (l, 0, 0)),
                           pl.BlockSpec((None, 1, n), lambda l: (l, 0, 0))],
                 out_specs=pl.BlockSpec((None, nb, n), lambda l: (l, 0, 0)))(c_all, mod_w, mod_b_loc)


def _mod_bwd(c_all, dmod_loc, dmod_all):
    nl, nb, n = dmod_loc.shape
    d = c_all.shape[1]
    n_all = dmod_all.shape[2]

    def body(c_ref, dl_ref, da_ref, gw_ref, gb_ref):
        gw_ref[...] = _dot(_silu(c_ref[...]), dl_ref[...], TN)
        gb_ref[...] = _colsum(da_ref[...])

    return _call(body, "mod_bwd", (_sds((nl, d, n), F32), _sds((nl, 1, n_all), F32)), grid=(nl,),
                 in_specs=[pl.BlockSpec((nb, d), lambda l: (0, 0)), pl.BlockSpec((None, nb, n), lambda l: (l, 0, 0)),
                           pl.BlockSpec((None, nb, n_all), lambda l: (l, 0, 0))],
                 out_specs=(pl.BlockSpec((None, d, n), lambda l: (l, 0, 0)), pl.BlockSpec((None, 1, n_all), lambda l: (l, 0, 0))),
                 )(c_all, dmod_loc, dmod_all)


def _row_spec(d, tpb):
    return pl.BlockSpec((None, 1, d), lambda i: (i // tpb, 0, 0))


def _tile_spec(tm, d):
    return pl.BlockSpec((tm, d), lambda i: (i, 0))


def _vec_spec(d):
    return pl.BlockSpec((1, d), lambda i: (0, 0))


def _modulate(x, sc, sh, seq):
    t, d = x.shape
    tm = min(TM, seq)
    tpb = seq // tm

    def body(x_ref, sc_ref, sh_ref, h_ref):
        h_ref[...] = (x_ref[...] * (1.0 + sc_ref[...]) + sh_ref[...]).astype(BF16)

    return _call(body, "modulate", _sds((t, d), BF16), grid=(t // tm,),
                 in_specs=[_tile_spec(tm, d), _row_spec(d, tpb), _row_spec(d, tpb)], out_specs=_tile_spec(tm, d))(x, sc, sh)


def _lnres_fwd(x, y, gate, lg, lb, alpha, seq, nxt=None):
    t, d = x.shape
    tm = min(TM, seq)
    tpb = seq // tm

    def body(x_ref, y_ref, g_ref, lg_ref, lb_ref, *rest):
        r = alpha * x_ref[...] + (1.0 + g_ref[...]) * y_ref[...]
        rc, rstd = _ln_stats(r)
        xn = rc * rstd * lg_ref[...] + lb_ref[...]
        if nxt is None:
            rest[0][...] = xn
        else:
            sc_ref, sh_ref, xo_ref, h_ref = rest
            xo_ref[...] = xn
            h_ref[...] = (xn * (1.0 + sc_ref[...]) + sh_ref[...]).astype(BF16)

    ins = [_tile_spec(tm, d), _tile_spec(tm, d), _row_spec(d, tpb), _vec_spec(d), _vec_spec(d)]
    if nxt is None:
        return _call(body, "lnres_fwd_last", _sds((t, d), F32), grid=(t // tm,), in_specs=ins,
                     out_specs=_tile_spec(tm, d))(x, y, gate, lg, lb)
    return _call(body, "lnres_fwd", (_sds((t, d), F32), _sds((t, d), BF16)), grid=(t // tm,),
                 in_specs=ins + [_row_spec(d, tpb), _row_spec(d, tpb)],
                 out_specs=(_tile_spec(tm, d), _tile_spec(tm, d)))(x, y, gate, lg, lb, *nxt)


def _loss_head(x, tgt):
    t, d = x.shape
    tm = min(TM, t)

    def body(x_ref, t_ref, dx_ref, sq_ref):
        e = x_ref[...] - t_ref[...]
        dx_ref[...] = e * (1.0 / d)

        @pl.when(pl.program_id(0) == 0)
        def _():
            sq_ref[...] = jnp.zeros_like(sq_ref)

        sq_ref[...] += _colsum(e * e)

    return _call(body, "loss_head", (_sds((t, d), F32), _sds((1, d), F32)), grid=(t // tm,),
                 in_specs=[_tile_spec(tm, d), _tile_spec(tm, d)], out_specs=(_tile_spec(tm, d), _vec_spec(d)))(x, tgt)


def _lnres_bwd(dxo, x, y, gate, lg, alpha, seq, through=None):
    t, d = x.shape
    tm = min(TM, seq)
    tpb = seq // tm
    nb = t // seq

    def body(dxo_ref, x_ref, y_ref, g_ref, lg_ref, *rest):
        if through is None:
            dxr_ref, dy_ref, dlg_ref, dlb_ref, dys_ref, dg_ref = rest
        else:
            dh_ref, sc_ref, lb_ref, dxr_ref, dy_ref, dlg_ref, dlb_ref, dys_ref, dg_ref, dsc_ref, dsh_ref = rest
        i = pl.program_id(0)
        yv = y_ref[...]
        r = alpha * x_ref[...] + (1.0 + g_ref[...]) * yv
        rc, rstd = _ln_stats(r)
        xhat = rc * rstd
        dxo_v = dxo_ref[...]
        if through is not None:
            dhv = dh_ref[...]
            dxo_v = dxo_v + dhv * (1.0 + sc_ref[...])

            @pl.when(i % tpb == 0)
            def _():
                dsc_ref[...] = jnp.zeros_like(dsc_ref)
                dsh_ref[...] = jnp.zeros_like(dsh_ref)

            dsc_ref[...] += _colsum(dhv * (xhat * lg_ref[...] + lb_ref[...]))
            dsh_ref[...] += _colsum(dhv)
        dxh = dxo_v * lg_ref[...]
        m1 = jnp.mean(dxh, axis=-1, keepdims=True)
        m2 = jnp.mean(dxh * xhat, axis=-1, keepdims=True)
        dr = rstd * (dxh - m1 - xhat * m2)
        dyv = (1.0 + g_ref[...]) * dr
        dxr_ref[...] = alpha * dr
        dy_ref[...] = dyv.astype(BF16)

        @pl.when(i == 0)
        def _():
            dlg_ref[...] = jnp.zeros_like(dlg_ref)
            dlb_ref[...] = jnp.zeros_like(dlb_ref)
            dys_ref[...] = jnp.zeros_like(dys_ref)

        @pl.when(i % tpb == 0)
        def _():
            dg_ref[...] = jnp.zeros_like(dg_ref)

        dlg_ref[...] += _colsum(dxo_v * xhat)
        dlb_ref[...] += _colsum(dxo_v)
        dys_ref[...] += _colsum(dyv)
        dg_ref[...] += _colsum(dr * yv)

    out = [_sds((t, d), F32), _sds((t, d), BF16), _sds((1, d), F32), _sds((1, d), F32), _sds((1, d), F32), _sds((nb, 1, d), F32)]
    out_specs = [_tile_spec(tm, d), _tile_spec(tm, d), _vec_spec(d), _vec_spec(d), _vec_spec(d), _row_spec(d, tpb)]
    ins = [dxo, x, y, gate, lg]
    in_specs = [_tile_spec(tm, d), _tile_spec(tm, d), _tile_spec(tm, d), _row_spec(d, tpb), _vec_spec(d)]
    if through is not None:
        ins += list(through)
        in_specs += [_tile_spec(tm, d), _row_spec(d, tpb), _vec_spec(d)]
        out += [_sds((nb, 1, d), F32), _sds((nb, 1, d), F32)]
        out_specs += [_row_spec(d, tpb), _row_spec(d, tpb)]
    return _call(body, "lnres_bwd" if through is None else "lnres_mod_bwd", tuple(out), grid=(t // tm,),
                 in_specs=in_specs, out_specs=tuple(out_specs))(*ins)


def _modulate_bwd(dh, dxr, x, sc, seq):
    t, d = x.shape
    tm = min(TM, seq)
    tpb = seq // tm
    nb = t // seq

    def body(dh_ref, dxr_ref, x_ref, sc_ref, dx_ref, dsc_ref, dsh_ref):
        dhv = dh_ref[...]
        dx_ref[...] = dxr_ref[...] + dhv * (1.0 + sc_ref[...])

        @pl.when(pl.program_id(0) % tpb == 0)
        def _():
            dsc_ref[...] = jnp.zeros_like(dsc_ref)
            dsh_ref[...] = jnp.zeros_like(dsh_ref)

        dsc_ref[...] += _colsum(dhv * x_ref[...])
        dsh_ref[...] += _colsum(dhv)

    return _call(body, "modulate_bwd", (_sds((t, d), F32), _sds((nb, 1, d), F32), _sds((nb, 1, d), F32)), grid=(t // tm,),
                 in_specs=[_tile_spec(tm, d), _tile_spec(tm, d), _tile_spec(tm, d), _row_spec(d, tpb)],
                 out_specs=(_tile_spec(tm, d), _row_spec(d, tpb), _row_spec(d, tpb)))(dh, dxr, x, sc)


def _ffn_in(h, w_in, layer, beside=None):
    t, d = h.shape
    n = w_in.shape[3]
    half = N_DEV // 2
    tm = min(TMM, t)

    def body(h_ref, wg_ref, wu_ref, g_ref, u_ref, a_ref):
        hv = h_ref[...]
        g = _dot(hv, wg_ref[...])
        u = _dot(hv, wu_ref[...])
        g_ref[...] = g.astype(FFN_KEEP)
        u_ref[...] = u.astype(FFN_KEEP)
        a_ref[...] = (_silu(g) * u).astype(BF16)

    blk = pl.BlockSpec((None, tm, n), lambda p, i: (p, i, 0))
    return _run(beside, body, "ffn_in", (_sds((half, t, n), FFN_KEEP), _sds((half, t, n), FFN_KEEP), _sds((half, t, n), BF16)),
                (h, w_in, w_in), (half, t // tm),
                [pl.BlockSpec((tm, d), lambda p, i: (i, 0)),
                 pl.BlockSpec((None, None, d, n), lambda p, i: (layer, p, 0, 0)),
                 pl.BlockSpec((None, None, d, n), lambda p, i: (layer, p + half, 0, 0))],
                (blk, blk, blk), vmem=VMEM_BIG)


def _ffn_out(act, w_out, layer):
    half, t, n = act.shape
    d = w_out.shape[2]
    tm = min(TM, t)

    def body(a_ref, w_ref, o_ref):
        acc = _dot(a_ref[0], w_ref[0])
        for p in range(1, half):
            acc = acc + _dot(a_ref[p], w_ref[p])
        o_ref[...] = acc

    return _call(body, "ffn_out", _sds((t, d), F32), grid=(t // tm,),
                 in_specs=[pl.BlockSpec((half, tm, n), lambda i: (0, i, 0)),
                           pl.BlockSpec((half, n, d), lambda i: (layer // half, 0, 0))],
                 out_specs=pl.BlockSpec((tm, d), lambda i: (i, 0)), vmem=VMEM_BIG)(act, w_out)


def _ffn_dact(dy, w_out, hg, hu, layer, beside=None):
    half, t, n = hg.shape
    d = dy.shape[1]
    tm = min(TMM, t)

    def body(dy_ref, w_ref, g_ref, u_ref, dg_ref, du_ref):
        da = _dot(dy_ref[...], w_ref[...], NT)
        g = g_ref[...].astype(F32)
        sg = jax.nn.sigmoid(g)
        dg_ref[...] = (da * u_ref[...].astype(F32) * (sg * (1.0 + g * (1.0 - sg)))).astype(BF16)
        du_ref[...] = (da * (g * sg)).astype(BF16)

    blk = pl.BlockSpec((None, tm, n), lambda p, i: (p, i, 0))
    return _run(beside, body, "ffn_dact", (_sds((half, t, n), BF16), _sds((half, t, n), BF16)), (dy, w_out, hg, hu), (half, t // tm),
                [pl.BlockSpec((tm, d), lambda p, i: (i, 0)), pl.BlockSpec((None, n, d), lambda p, i: (layer + p, 0, 0)), blk, blk],
                (blk, blk), vmem=VMEM_BIG)


def _ffn_bwd_weights(h, dy, act, dg, du):
    half, t, n = act.shape
    d = h.shape[1]
    tk = min(TMM, t)
    h_spec = pl.BlockSpec((tk, d), lambda r: (r, 0))
    g_spec = pl.BlockSpec((half, tk, n), lambda r: (0, r, 0))

    def in_terms(h_ref, g_ref):
        hv = h_ref[...]
        return [_dot(hv, g_ref[p], TN) for p in range(half)]

    def out_terms(a_ref, dy_ref):
        dyv = dy_ref[...]
        return [_dot(a_ref[p], dyv, TN) for p in range(half)]

    w_in_spec = pl.BlockSpec((half, d, n), lambda r: (0, 0, 0))
    dwg = _accumulate_over_tokens("ffn_dw_gate", (h, dg), [h_spec, g_spec], _sds((half, d, n), GRAD_WIRE), w_in_spec,
                                  (half, d, n), t // tk, in_terms)
    dwu = _accumulate_over_tokens("ffn_dw_up", (h, du), [h_spec, g_spec], _sds((half, d, n), GRAD_WIRE), w_in_spec,
                                  (half, d, n), t // tk, in_terms)
    dwo = _accumulate_over_tokens("ffn_dw_out", (act, dy), [g_spec, h_spec], _sds((half, n, d), GRAD_WIRE),
                                  pl.BlockSpec((half, n, d), lambda r: (0, 0, 0)), (half, n, d), t // tk, out_terms)
    return dwg, dwu, dwo


def _ffn_dh(dg, du, w_in, layer, beside=None):
    half, t, n = dg.shape
    d = w_in.shape[2]
    tm = min(TM, t)

    def body(dg_ref, du_ref, w_ref, o_ref):
        acc = None
        for p in range(half):
            for ref, q in ((dg_ref, p), (du_ref, p + half)):
                term = _dot(ref[p], w_ref[q], NT)
                acc = term if acc is None else acc + term
        o_ref[...] = acc

    g_spec = pl.BlockSpec((half, tm, n), lambda i: (0, i, 0))
    return _run(beside, body, "ffn_dh", _sds((t, d), F32), (dg, du, w_in), (t // tm,),
                [g_spec, g_spec, pl.BlockSpec((None, 2 * half, d, n), lambda i: (layer, 0, 0, 0))],
                pl.BlockSpec((tm, d), lambda i: (i, 0)), vmem=VMEM_BIG)


def _tril(n, strict=False):
    r = lax.broadcasted_iota(jnp.int32, (n, n), 0)
    c = lax.broadcasted_iota(jnp.int32, (n, n), 1)
    return c < r if strict else c <= r


def _gm_spatial_fwd(pre, lng, lnb, w_s, b_st, seq):
    t, w2 = pre.shape
    w = w2 // 2
    gd = w // GM_GROUPS
    tm = min(TM, seq)
    nch = tm // GM_CHUNK

    def body(pre_ref, lng_ref, lnb_ref, ws_ref, bs_ref, y_ref):
        v = _gelu(pre_ref[:, w:])
        vc, rstd = _ln_stats(v)
        vn = (vc * rstd * lng_ref[...] + lnb_ref[...]).astype(BF16)
        keep = _tril(GM_CHUNK)
        for g in range(GM_GROUPS):
            wm = jnp.where(keep, ws_ref[g], 0.0).astype(BF16)
            for ci in range(nch):
                rows = slice(ci * GM_CHUNK, (ci + 1) * GM_CHUNK)
                cols = slice(g * gd, (g + 1) * gd)
                sv = _dot(wm, vn[rows, cols]) + bs_ref[:, g:g + 1]
                u = _gelu(pre_ref[rows, cols])
                y_ref[rows, cols] = (u * sv).astype(BF16)

    return _call(body, "gm_spatial_fwd", _sds((t, w), BF16), grid=(t // tm,),
                 in_specs=[_tile_spec(tm, w2), _vec_spec(w), _vec_spec(w),
                           pl.BlockSpec((GM_GROUPS, GM_CHUNK, GM_CHUNK), lambda i: (0, 0, 0)),
                           pl.BlockSpec((GM_CHUNK, GM_GROUPS), lambda i: (0, 0))],
                 out_specs=_tile_spec(tm, w), vmem=VMEM_BIG)(pre, lng, lnb, w_s, b_st)


def _gm_spatial_bwd(pre, dyv, lng, lnb, w_s, b_st, seq, beside=None):
    t, w2 = pre.shape
    w = w2 // 2
    gd = w // GM_GROUPS
    tm = min(TM, seq)
    nch = tm // GM_CHUNK

    def body(pre_ref, dyv_ref, lng_ref, lnb_ref, ws_ref, bs_ref, dpre_ref, dws_ref, dbs_ref, dlg_ref, dlb_ref, dbin_ref, dvn_ref):
        @pl.when(pl.program_id(0) == 0)
        def _():
            dws_ref[...] = jnp.zeros_like(dws_ref)
            dbs_ref[...] = jnp.zeros_like(dbs_ref)
            dlg_ref[...] = jnp.zeros_like(dlg_ref)
            dlb_ref[...] = jnp.zeros_like(dlb_ref)
            dbin_ref[...] = jnp.zeros_like(dbin_ref)

        pv = pre_ref[:, w:]
        v = _gelu(pv)
        vc, rstd = _ln_stats(v)
        vhat = vc * rstd
        vn = (vhat * lng_ref[...] + lnb_ref[...]).astype(BF16)
        keep = _tril(GM_CHUNK)
        dbs_cols = []
        for g in range(GM_GROUPS):
            wm = jnp.where(keep, ws_ref[g], 0.0).astype(BF16)
            dwm = jnp.zeros((GM_CHUNK, GM_CHUNK), F32)
            dbs = jnp.zeros((GM_CHUNK, 1), F32)
            for ci in range(nch):
                rows = slice(ci * GM_CHUNK, (ci + 1) * GM_CHUNK)
                cols = slice(g * gd, (g + 1) * gd)
                vn_b = vn[rows, cols]
                sv = _dot(wm, vn_b) + bs_ref[:, g:g + 1]
                pu = pre_ref[rows, cols]
                dy = dyv_ref[rows, cols]
                du = dy * sv
                dsv = dy * _gelu(pu)
                dpu = du * _gelu_grad(pu)
                dpre_ref[rows, cols] = dpu.astype(BF16)
                dbin_ref[:, cols] += _colsum(dpu)
                dsv_b = dsv.astype(BF16)
                dwm = dwm + _dot(dsv_b, vn_b, NT)
                dbs = dbs + jnp.sum(dsv, axis=-1, keepdims=True)
                dvn_ref[rows, cols] = _dot(wm, dsv_b, TN)
            dws_ref[g] += jnp.where(keep, dwm, 0.0)
            dbs_cols.append(dbs)
        dbs_ref[...] += jnp.concatenate(dbs_cols, axis=1)
        dvn = dvn_ref[...]
        dlg_ref[...] += _colsum(dvn * vhat)
        dlb_ref[...] += _colsum(dvn)
        dvh = dvn * lng_ref[...]
        m1 = jnp.mean(dvh, axis=-1, keepdims=True)
        m2 = jnp.mean(dvh * vhat, axis=-1, keepdims=True)
        dv = rstd * (dvh - m1 - vhat * m2)
        dpv = dv * _gelu_grad(pv)
        dpre_ref[:, w:] = dpv.astype(BF16)
        dbin_ref[:, w:] += _colsum(dpv)

    full3 = pl.BlockSpec((GM_GROUPS, GM_CHUNK, GM_CHUNK), lambda i: (0, 0, 0))
    bst = pl.BlockSpec((GM_CHUNK, GM_GROUPS), lambda i: (0, 0))
    return _run(beside, body, "gm_spatial_bwd",
                (_sds((t, w2), BF16), _sds((GM_GROUPS, GM_CHUNK, GM_CHUNK), F32), _sds((GM_CHUNK, GM_GROUPS), F32),
                 _sds((1, w), F32), _sds((1, w), F32), _sds((1, w2), F32)),
                (pre, dyv, lng, lnb, w_s, b_st), (t // tm,),
                [_tile_spec(tm, w2), _tile_spec(tm, w), _vec_spec(w), _vec_spec(w), full3, bst],
                (_tile_spec(tm, w2), full3, bst, _vec_spec(w), _vec_spec(w), _vec_spec(w2)),
                scratch=[pltpu.VMEM((tm, w), F32)], vmem=VMEM_BIG)


def _head_masks():
    lane = lax.broadcasted_iota(jnp.int32, (1, LANES), 1)
    return lane < HEAD_DIM


def _two_heads(x, m0):
    z = jnp.zeros_like(x)
    return jnp.where(m0, x, z), jnp.where(m0, z, x)


def _transposed(pair):
    return tuple(x.astype(F32).T.astype(BF16) for x in pair)


def _store_transposed(dqkv_ref, dk_acc, dv_acc, nq, tq):
    for c in range(nq):
        cols = slice(c * tq, (c + 1) * tq)
        dqkv_ref[1, cols, :] = dk_acc[:, cols].T.astype(BF16)
        dqkv_ref[2, cols, :] = dv_acc[:, cols].T.astype(BF16)


def _qkv_specs(seq, nq, blocked_q):
    if blocked_q:
        q = pl.BlockSpec((None, TQ_(seq), LANES), lambda b, p, i: (0, b * nq + i, p))
        k = pl.BlockSpec((None, seq, LANES), lambda b, p, i: (1, b, p))
        v = pl.BlockSpec((None, seq, LANES), lambda b, p, i: (2, b, p))
    else:
        q = pl.BlockSpec((None, seq, LANES), lambda b, p: (0, b, p))
        k = pl.BlockSpec((None, seq, LANES), lambda b, p: (1, b, p))
        v = pl.BlockSpec((None, seq, LANES), lambda b, p: (2, b, p))
    return q, k, v


def TQ_(seq):
    return min(TQ, seq)


def _fox_gate_fwd(ft, b_f, seq):
    nh, t = ft.shape
    nch = seq // LANES

    def body(ft_ref, bf_ref, fr_ref):
        r = lax.broadcasted_iota(jnp.int32, (LANES, LANES), 0)
        c = lax.broadcasted_iota(jnp.int32, (LANES, LANES), 1)
        upper = jnp.where(r <= c, 1.0, 0.0).astype(BF16)
        carry = jnp.zeros((nh, 1), F32)
        for ci in range(nch):
            cols = slice(ci * LANES, (ci + 1) * LANES)
            lf = _log_sigmoid(ft_ref[:, cols] + bf_ref[...])
            cs = _dot_exact(lf, upper) + carry
            fr_ref[:, cols] = cs
            carry = cs[:, LANES - 1:LANES]

    return _call(body, "fox_gate_fwd", _sds((nh, t), F32), grid=(t // seq,),
                 in_specs=[pl.BlockSpec((nh, seq), lambda b: (0, b)), pl.BlockSpec((nh, 1), lambda b: (0, 0))],
                 out_specs=pl.BlockSpec((nh, seq), lambda b: (0, b)))(ft, b_f)


def _fox_gate_bwd(ft, b_f, dfk, dfq, seq):
    nh, t = ft.shape
    nch = seq // LANES

    def body(ft_ref, bf_ref, dfk_ref, dfq_ref, dl_ref, db_ref):
        @pl.when(pl.program_id(0) == 0)
        def _():
            db_ref[...] = jnp.zeros_like(db_ref)

        r = lax.broadcasted_iota(jnp.int32, (LANES, LANES), 0)
        c = lax.broadcasted_iota(jnp.int32, (LANES, LANES), 1)
        lower = jnp.where(r >= c, 1.0, 0.0).astype(BF16)
        carry = jnp.zeros((nh, 1), F32)
        tot = jnp.zeros((nh, 1), F32)
        for ci in reversed(range(nch)):
            cols = slice(ci * LANES, (ci + 1) * LANES)
            rc = _dot_exact(dfk_ref[:, cols] + dfq_ref[:, cols], lower) + carry
            carry = rc[:, 0:1]
            dl = rc * jax.nn.sigmoid(-(ft_ref[:, cols] + bf_ref[...]))
            dl_ref[:, cols] = dl
            tot = tot + jnp.sum(dl, axis=-1, keepdims=True)
        db_ref[...] += tot

    blk = pl.BlockSpec((nh, seq), lambda b: (0, b))
    one = pl.BlockSpec((nh, 1), lambda b: (0, 0))
    return _call(body, "fox_gate_bwd", (_sds((nh, t), F32), _sds((nh, 1), F32)), grid=(t // seq,),
                 in_specs=[blk, one, blk, blk], out_specs=(blk, one))(ft, b_f, dfk, dfq)


def _sweep(step, n_off, unroll, init, start=0):
    def group(_, st):
        base, carry = st[0], st[1:]
        for u in range(unroll):
            carry = step(base + u, carry, False)
        return (base + unroll, *carry)

    def tail(r):
        def run(st):
            base, carry = st[0], st[1:]
            for u in range(r):
                carry = step(base + u, carry, False)
            return step(base + r, carry, True)
        return run

    def pick(idx, fns, st):
        if len(fns) == 1:
            return fns[0](st)
        half = len(fns) // 2
        return lax.cond(idx < half, lambda s: pick(idx, fns[:half], s), lambda s: pick(idx - half, fns[half:], s), st)

    st = lax.fori_loop(0, n_off // unroll, group, (jnp.int32(0) + start, *init))
    return pick(n_off % unroll, [tail(r) for r in range(unroll)], st)


def _carried(n_comm, n_in, n_out, refs):
    own_in, send = refs[:n_in], refs[n_in:n_in + n_comm]
    rest = refs[n_in + n_comm:]
    n_sem = 3 if n_comm else 0
    return own_in, send, rest[:n_out], rest[n_out:n_out + n_comm], rest[n_out + n_comm:n_out + n_comm + n_sem], rest[n_out + n_comm + n_sem:]


def _fox_fwd(qkv, frow, nb, seq, gather=()):
    _, t, d = qkv.shape
    npair = d // LANES
    tq = TQ_(seq)
    nq = seq // tq
    scale = HEAD_DIM ** -0.5
    n_comm = len(gather)
    total = nb * npair * nq

    def body(*refs):
        (q_ref, k_ref, v_ref, fr_ref), send, (o_ref, lse_ref), land, sems, _ = _carried(n_comm, 4, 2, refs)
        at = (pl.program_id(0) * npair + pl.program_id(1)) * nq + pl.program_id(2)
        if n_comm:
            for stage, when in ((0, 0), (1, (3 * total) // 4)):
                @pl.when(at == when)
                def _(stage=stage):
                    _gather_stage(stage, send, land, *sems)

        m0 = _head_masks()
        qm = _two_heads(q_ref[...] * scale, m0)
        row = lax.broadcasted_iota(jnp.int32, (tq, tq), 0)
        col = lax.broadcasted_iota(jnp.int32, (tq, tq), 1)
        one = jnp.ones((tq, LANES), BF16)

        def step(j, carry, diag):
            off = pl.multiple_of(j * tq, tq)
            kb = k_ref[pl.ds(off, tq), :]
            vb = v_ref[pl.ds(off, tq), :]
            vv = (jnp.where(m0, vb, one), jnp.where(m0, one, vb))
            out = []
            for hh in range(2):
                m, acc = carry[2 * hh], carry[2 * hh + 1]
                s = lax.dot_general(qm[hh], kb, NT, preferred_element_type=F32) - fr_ref[hh:hh + 1, pl.ds(off, tq)]
                if diag:
                    s = jnp.where(col <= row, s, NEG_INF)
                mn = jnp.maximum(m, jnp.max(s, axis=-1, keepdims=True))
                p = jnp.exp(s - mn)
                out += [mn, jnp.exp(m - mn) * acc + jnp.dot(p.astype(BF16), vv[hh], preferred_element_type=F32)]
            return tuple(out)

        neg = jnp.full((tq, 1), NEG_INF, F32)
        zacc = jnp.zeros((tq, LANES), F32)
        m_a, acc_a, m_b, acc_b = _sweep(step, pl.program_id(2), FOX_FWD_UNROLL, (neg, zacc, neg, zacc))
        l_a = pltpu.roll(acc_a, HEAD_DIM, 1)
        l_b = pltpu.roll(acc_b, HEAD_DIM, 1)
        o_ref[...] = jnp.where(m0, acc_a / l_a, acc_b / l_b)
        lse_ref[:, 0:1] = m_a + jnp.log(l_a[:, 0:1])
        lse_ref[:, 1:2] = m_b + jnp.log(l_b[:, HEAD_DIM:HEAD_DIM + 1])
        if n_comm:
            @pl.when(at == total - 1)
            def _():
                _gather_stage(2, send, land, *sems)

    q_spec, k_spec, v_spec = _qkv_specs(seq, nq, True)
    col_spec = pl.BlockSpec((None, tq, 2), lambda b, p, i: (p, b * nq + i, 0))
    hbm = pl.BlockSpec(memory_space=pl.ANY)
    return _call(body, "fox_fwd", (_sds((t, d), F32), _sds((npair, t, 2), F32), *_gathered_shapes(gather)), grid=(nb, npair, nq),
                 in_specs=[q_spec, k_spec, v_spec, pl.BlockSpec((None, 2, seq), lambda b, p, i: (p, 0, b))] + [hbm] * n_comm,
                 out_specs=(pl.BlockSpec((tq, LANES), lambda b, p, i: (b * nq + i, p)), col_spec, *[hbm] * n_comm),
                 scratch=_comm_scratch(n_comm) if n_comm else (), vmem=VMEM_BIG)(qkv, qkv, qkv, frow, *gather)


def _scatter_beside(stage, n_comm, nb, npair, send, land, sems):
    if n_comm:
        at = pl.program_id(0) * npair + pl.program_id(1)

        @pl.when(at == (0 if stage == 0 else nb * npair - 1))
        def _():
            _scatter_stage(stage, send, land, *sems)


def _fox_bwd(qkv, frow, o, do, lse, nb, seq, scatter=()):
    _, t, d = qkv.shape
    npair = d // LANES
    tq = TQ_(seq)
    nq = seq // tq
    scale = HEAD_DIM ** -0.5
    n_comm = len(scatter)

    def body(*refs):
        own_in, send, (dqkv_ref, df_ref, dfq_ref), land, sems, (dk_acc, dv_acc) = _carried(n_comm, 7, 3, refs)
        q_ref, k_ref, v_ref, fr_ref, o_ref, do_ref, lse_ref = own_in
        _scatter_beside(0, n_comm, nb, npair, send, land, sems)
        m0 = _head_masks()
        row = lax.broadcasted_iota(jnp.int32, (tq, tq), 0)
        col = lax.broadcasted_iota(jnp.int32, (tq, tq), 1)
        dk_acc[...] = jnp.zeros_like(dk_acc)
        dv_acc[...] = jnp.zeros_like(dv_acc)
        df_ref[...] = jnp.zeros_like(df_ref)

        def q_block(i, _):
            qoff = pl.multiple_of(i * tq, tq)
            qrows = pl.ds(qoff, tq)
            qm = _two_heads(q_ref[qrows, :] * scale, m0)
            dov = do_ref[qrows, :]
            dd = dov * o_ref[qrows, :]
            dm = _two_heads(dov.astype(BF16), m0)
            qmt = _transposed(qm)
            dmt = _transposed(dm)
            delta = (jnp.sum(jnp.where(m0, dd, 0.0), axis=-1, keepdims=True),
                     jnp.sum(jnp.where(m0, 0.0, dd), axis=-1, keepdims=True))
            ls = (lse_ref[qrows, 0:1], lse_ref[qrows, 1:2])

            def step(j, carry, diag):
                off = pl.multiple_of(j * tq, tq)
                krows = pl.ds(off, tq)
                kb = k_ref[krows, :]
                vb = v_ref[krows, :]
                dqs, rowsums = [], []
                dk = jnp.zeros((LANES, tq), F32)
                dv = jnp.zeros((LANES, tq), F32)
                for hh in range(2):
                    s = lax.dot_general(qm[hh], kb, NT, preferred_element_type=F32) - fr_ref[hh:hh + 1, krows]
                    if diag:
                        s = jnp.where(col <= row, s, NEG_INF)
                    p = jnp.exp(s - ls[hh])
                    dp = lax.dot_general(dm[hh], vb, NT, preferred_element_type=F32)
                    ds = p * (dp - delta[hh])
                    df_ref[hh:hh + 1, krows] -= _colsum(ds)
                    rowsums.append(carry[1 + hh] + jnp.sum(ds, axis=-1, keepdims=True))
                    ds_b = ds.astype(BF16)
                    dqs.append(jnp.dot(ds_b, kb, preferred_element_type=F32))
                    dk = dk + jnp.dot(qmt[hh], ds_b, preferred_element_type=F32)
                    dv = dv + jnp.dot(dmt[hh], p.astype(BF16), preferred_element_type=F32)
                dk_acc[:, krows] += dk
                dv_acc[:, krows] += dv
                return (carry[0] + jnp.where(m0, dqs[0], dqs[1]), *rowsums)

            zero = jnp.zeros((tq, 1), F32)
            dq, rs_a, rs_b = _sweep(step, i, FOX_BWD_UNROLL, (jnp.zeros((tq, LANES), F32), zero, zero))
            dqkv_ref[0, qrows, :] = (dq * scale).astype(BF16)
            dfq_ref[qrows, 0:1] = rs_a
            dfq_ref[qrows, 1:2] = rs_b
            return 0

        lax.fori_loop(0, nq, q_block, 0)
        _store_transposed(dqkv_ref, dk_acc, dv_acc, nq, tq)
        _scatter_beside(1, n_comm, nb, npair, send, land, sems)

    q_spec, k_spec, v_spec = _qkv_specs(seq, nq, False)
    col_spec = pl.BlockSpec((None, seq, 2), lambda b, p: (p, b, 0))
    row_spec = pl.BlockSpec((None, 2, seq), lambda b, p: (p, 0, b))
    tile = pl.BlockSpec((seq, LANES), lambda b, p: (b, p))
    hbm = pl.BlockSpec(memory_space=pl.ANY)
    return _call(body, "fox_bwd", (_sds((3, t, d), BF16), _sds((npair, 2, t), F32), _sds((npair, t, 2), F32), *_scattered_shapes(scatter)),
                 grid=(nb, npair),
                 in_specs=[q_spec, k_spec, v_spec, row_spec, tile, tile, col_spec] + [hbm] * n_comm,
                 out_specs=(pl.BlockSpec((3, seq, LANES), lambda b, p: (0, b, p)), row_spec, col_spec, *[hbm] * n_comm),
                 scratch=(_comm_scratch(n_comm) if n_comm else []) + [pltpu.VMEM((LANES, seq), F32), pltpu.VMEM((LANES, seq), F32)],
                 vmem=VMEM_BIG)(qkv, qkv, qkv, frow, o, do, lse, *scatter)


def _split2(x):
    hi = x.astype(BF16)
    return hi, (x - hi.astype(F32)).astype(BF16)


def _sum_right(x, tri):
    hi, lo = _split2(x)
    return jnp.dot(hi, tri, preferred_element_type=F32) + jnp.dot(lo, tri, preferred_element_type=F32)


def _sb_scores(qm_h, kb, mask):
    z = lax.dot_general(qm_h, kb, NT, preferred_element_type=F32)
    lb = _log_sigmoid(z)
    l1m = lb - z
    if mask is not None:
        l1m = jnp.where(mask, l1m, 0.0)
    return lb, l1m


def _sb_fwd(qkv, nb, seq):
    _, t, d = qkv.shape
    npair = d // LANES
    tq = TQ_(seq)
    nq = seq // tq
    scale = HEAD_DIM ** -0.5

    def body(q_ref, k_ref, v_ref, o_ref, lt_ref):
        i = pl.program_id(2)
        m0 = _head_masks()
        qm = _two_heads(q_ref[...] * scale, m0)
        row = lax.broadcasted_iota(jnp.int32, (tq, tq), 0)
        col = lax.broadcasted_iota(jnp.int32, (tq, tq), 1)
        after = jnp.where(row > col, 1.0, 0.0).astype(BF16)

        def step(j, carry, diag):
            off = pl.multiple_of(j * tq, tq)
            kb = k_ref[pl.ds(off, tq), :]
            vb = v_ref[pl.ds(off, tq), :]
            mask = (col < row) if diag else None
            nxt, parts = [], []
            for hh in range(2):
                lb, l1m = _sb_scores(qm[hh], kb, mask)
                rest = _sum_right(l1m, after) + carry[hh]
                a = jnp.exp(lb + rest)
                if diag:
                    a = jnp.where(mask, a, 0.0)
                parts.append(jnp.dot(a.astype(BF16), vb, preferred_element_type=F32))
                nxt.append(carry[hh] + jnp.sum(l1m, axis=-1, keepdims=True))
            return (*nxt, carry[2] + jnp.where(m0, parts[0], parts[1]))

        zero = jnp.zeros((tq, 1), F32)
        init = (zero, zero, jnp.zeros((tq, LANES), F32))
        carry = lax.cond(i > 0, lambda c: step(i - 1, step(i, c, True), False), lambda c: step(i, c, True), init)

        def alive(st):
            return (st[0] < i) & (jnp.max(jnp.maximum(st[1], st[2])) > SB_DEAD)

        def more(st):
            return (st[0] + 1, *step(i - 1 - st[0], st[1:], False))

        done, lt_a, lt_b, acc = lax.while_loop(alive, more, (jnp.minimum(i, 1), *carry))
        o_ref[...] = acc
        lt_ref[:, 0:1] = lt_a
        lt_ref[:, 1:2] = lt_b
        lt_ref[:, 2:3] = jnp.zeros((tq, 1), F32) + done.astype(F32)
        lt_ref[:, 3:4] = zero

    q_spec, k_spec, v_spec = _qkv_specs(seq, nq, True)
    return _call(body, "sb_fwd", (_sds((t, d), F32), _sds((npair, t, 4), F32)), grid=(nb, npair, nq),
                 in_specs=[q_spec, k_spec, v_spec],
                 out_specs=(pl.BlockSpec((tq, LANES), lambda b, p, i: (b * nq + i, p)),
                            pl.BlockSpec((None, tq, 4), lambda b, p, i: (p, b * nq + i, 0))), vmem=VMEM_BIG)(qkv, qkv, qkv)


def _sb_bwd(qkv, do, ltot, nb, seq, scatter=()):
    _, t, d = qkv.shape
    npair = d // LANES
    tq = TQ_(seq)
    nq = seq // tq
    scale = HEAD_DIM ** -0.5
    n_comm = len(scatter)

    def body(*refs):
        (q_ref, k_ref, v_ref, do_ref, lt_ref), send, (dqkv_ref,), land, sems, (dk_acc, dv_acc) = _carried(n_comm, 5, 1, refs)
        _scatter_beside(0, n_comm, nb, npair, send, land, sems)
        m0 = _head_masks()
        row = lax.broadcasted_iota(jnp.int32, (tq, tq), 0)
        col = lax.broadcasted_iota(jnp.int32, (tq, tq), 1)
        upto = jnp.where(row <= col, 1.0, 0.0).astype(BF16)
        left_of = jnp.where(row < col, 1.0, 0.0).astype(BF16)
        dk_acc[...] = jnp.zeros_like(dk_acc)
        dv_acc[...] = jnp.zeros_like(dv_acc)

        def q_block(i, _):
            qoff = pl.multiple_of(i * tq, tq)
            qrows = pl.ds(qoff, tq)
            qm = _two_heads(q_ref[qrows, :] * scale, m0)
            dm = _two_heads(do_ref[qrows, :].astype(BF16), m0)
            qmt = _transposed(qm)
            dmt = _transposed(dm)
            ltot = (lt_ref[qrows, 0:1], lt_ref[qrows, 1:2])

            def step(j, carry, diag):
                off = pl.multiple_of(j * tq, tq)
                krows = pl.ds(off, tq)
                kb = k_ref[krows, :]
                vb = v_ref[krows, :]
                mask = (col < row) if diag else None
                nxt, dqs = [], []
                dk = jnp.zeros((LANES, tq), F32)
                dv = jnp.zeros((LANES, tq), F32)
                for hh in range(2):
                    cl, ce = carry[2 * hh], carry[2 * hh + 1]
                    lb, l1m = _sb_scores(qm[hh], kb, mask)
                    a = jnp.exp(lb + (ltot[hh] - (_sum_right(l1m, upto) + cl)))
                    if diag:
                        a = jnp.where(mask, a, 0.0)
                    e = lax.dot_general(dm[hh], vb, NT, preferred_element_type=F32) * a
                    before = _sum_right(e, left_of) + ce
                    beta = jnp.exp(lb)
                    dz = e * (1.0 - beta) - before * beta
                    if diag:
                        dz = jnp.where(mask, dz, 0.0)
                    dz_b = dz.astype(BF16)
                    dqs.append(jnp.dot(dz_b, kb, preferred_element_type=F32))
                    dk = dk + jnp.dot(qmt[hh], dz_b, preferred_element_type=F32)
                    dv = dv + jnp.dot(dmt[hh], a.astype(BF16), preferred_element_type=F32)
                    nxt += [cl + jnp.sum(l1m, axis=-1, keepdims=True), ce + jnp.sum(e, axis=-1, keepdims=True)]
                dk_acc[:, krows] += dk
                dv_acc[:, krows] += dv
                return (*nxt, carry[4] + jnp.where(m0, dqs[0], dqs[1]))

            zero = jnp.zeros((tq, 1), F32)
            visited = jnp.max(lt_ref[qrows, 2:3]).astype(jnp.int32)
            carry = _sweep(step, visited, SB_BWD_UNROLL, (zero, zero, zero, zero, jnp.zeros((tq, LANES), F32)), start=i - visited)
            dqkv_ref[0, qrows, :] = (carry[4] * scale).astype(BF16)
            return 0

        lax.fori_loop(0, nq, q_block, 0)
        _store_transposed(dqkv_ref, dk_acc, dv_acc, nq, tq)
        _scatter_beside(1, n_comm, nb, npair, send, land, sems)

    q_spec, k_spec, v_spec = _qkv_specs(seq, nq, False)
    tile = pl.BlockSpec((seq, LANES), lambda b, p: (b, p))
    hbm = pl.BlockSpec(memory_space=pl.ANY)
    return _call(body, "sb_bwd", (_sds((3, t, d), BF16), *_scattered_shapes(scatter)), grid=(nb, npair),
                 in_specs=[q_spec, k_spec, v_spec, tile, pl.BlockSpec((None, seq, 4), lambda b, p: (p, b, 0))] + [hbm] * n_comm,
                 out_specs=(pl.BlockSpec((3, seq, LANES), lambda b, p: (0, b, p)), *[hbm] * n_comm),
                 scratch=(_comm_scratch(n_comm) if n_comm else []) + [pltpu.VMEM((LANES, seq), F32), pltpu.VMEM((LANES, seq), F32)],
                 vmem=VMEM_BIG)(qkv, qkv, qkv, do, ltot, *scatter)


def _qkv_proj(name, h, w):
    t, d = h.shape
    tm = min(TMM, t)
    return _mm(name, h, w, _sds((3, t, d), BF16), (3, t // tm, 1),
               pl.BlockSpec((tm, d), lambda s, i, r: (i, 0)), pl.BlockSpec((d, d), lambda s, i, r: (0, s)),
               pl.BlockSpec((None, tm, d), lambda s, i, r: (s, i, 0)), NN)


def _qkv_dw(name, h, dqkv):
    t, d = h.shape
    tk = min(TMM, t)

    def terms(h_ref, g_ref):
        hv = h_ref[...]
        return [_dot(hv, g_ref[s], TN) for s in range(3)]

    def store(o_ref, acc_ref):
        for s in range(3):
            o_ref[:, s * d:(s + 1) * d] = acc_ref[s].astype(o_ref.dtype)

    return _accumulate_over_tokens(name, (h, dqkv), [pl.BlockSpec((tk, d), lambda r: (r, 0)), pl.BlockSpec((3, tk, d), lambda r: (0, r, 0))],
                                   _sds((d, 3 * d), GRAD_WIRE), pl.BlockSpec((d, 3 * d), lambda r: (0, 0)), (3, d, d), t // tk, terms, store)


def _qkv_dh(name, dqkv, w):
    _, t, d = dqkv.shape
    tm = min(TM, t)

    def body(g_ref, w_ref, o_ref):
        acc = None
        for s in range(3):
            term = _dot(g_ref[s], w_ref[:, s * d:(s + 1) * d], NT)
            acc = term if acc is None else acc + term
        o_ref[...] = acc

    return _call(body, name, _sds((t, d), F32), grid=(t // tm,),
                 in_specs=[pl.BlockSpec((3, tm, d), lambda i: (0, i, 0)), pl.BlockSpec((d, 3 * d), lambda i: (0, 0))],
                 out_specs=pl.BlockSpec((tm, d), lambda i: (i, 0)), vmem=VMEM_BIG)(dqkv, w)


def _glu(pre_block, d):
    return pre_block[:, :d] * jax.nn.sigmoid(pre_block[:, d:])


def _shifted_copies(ext_ref, sh_ref, tt):
    for r in range(1, SUBLANES):
        sh_ref[r - 1] = ext_ref[pl.ds(r, tt + CONV_HALO - SUBLANES), :]


def _rows_from(ext_ref, sh_ref, base, offset, n):
    q, r = divmod(offset, SUBLANES)
    if r == 0:
        return ext_ref[pl.ds(pl.multiple_of(base + offset, SUBLANES), n), :]
    return sh_ref[r - 1, pl.ds(pl.multiple_of(base + q * SUBLANES, SUBLANES), n), :]


def _cv_conv_fwd(pre, dw, dwb, lng, lnb, seq):
    t, d2 = pre.shape
    d = d2 // 2
    tt = min(TT, seq)
    nt = seq // tt
    hb = tt // CONV_HALO

    def body(pre_ref, halo_ref, dw_ref, dwb_ref, lng_ref, lnb_ref, y1_ref, y2_ref, ext_ref, sh_ref):
        i = pl.program_id(1)
        ext_ref[0:CONV_HALO, :] = jnp.where(i == 0, 0.0, _glu(halo_ref[...], d))
        ext_ref[CONV_HALO:, :] = _glu(pre_ref[...], d)
        _shifted_copies(ext_ref, sh_ref, tt)

        acc = jnp.zeros((tt, d), F32) + dwb_ref[...]
        for k in range(CONV_WIDTH):
            acc = acc + _rows_from(ext_ref, sh_ref, 0, CONV_HALO - (CONV_WIDTH - 1) + k, tt) * dw_ref[k:k + 1, :]
        y1_ref[...] = acc
        yc, rstd = _ln_stats(acc)
        y2_ref[...] = _silu(yc * rstd * lng_ref[...] + lnb_ref[...]).astype(BF16)

    vec = pl.BlockSpec((1, d), lambda b, i: (0, 0))
    tile = pl.BlockSpec((tt, d), lambda b, i: (b * nt + i, 0))
    return _call(body, "cv_conv_fwd", (_sds((t, d), F32), _sds((t, d), BF16)), grid=(t // seq, nt),
                 in_specs=[pl.BlockSpec((tt, d2), lambda b, i: (b * nt + i, 0)),
                           pl.BlockSpec((CONV_HALO, d2), lambda b, i: (jnp.maximum((b * nt + i) * hb - 1, 0), 0)),
                           pl.BlockSpec((CONV_HALO, d), lambda b, i: (0, 0)), vec, vec, vec],
                 out_specs=(tile, tile),
                 scratch=[pltpu.VMEM((tt + CONV_HALO, d), F32), pltpu.VMEM((SUBLANES - 1, tt + CONV_HALO - SUBLANES, d), F32)],
                 vmem=VMEM_BIG)(pre, pre, dw, dwb, lng, lnb)


def _cv_norm_bwd(y1, dy2, lng, lnb):
    t, d = y1.shape
    tm = min(TM, t)

    def body(y1_ref, dy2_ref, lng_ref, lnb_ref, dy1_ref, dlg_ref, dlb_ref, dsum_ref):
        @pl.when(pl.program_id(0) == 0)
        def _():
            dlg_ref[...] = jnp.zeros_like(dlg_ref)
            dlb_ref[...] = jnp.zeros_like(dlb_ref)
            dsum_ref[...] = jnp.zeros_like(dsum_ref)

        yc, rstd = _ln_stats(y1_ref[...])
        yhat = yc * rstd
        n = yhat * lng_ref[...] + lnb_ref[...]
        sg = jax.nn.sigmoid(n)
        dn = dy2_ref[...] * (sg * (1.0 + n * (1.0 - sg)))
        dlg_ref[...] += _colsum(dn * yhat)
        dlb_ref[...] += _colsum(dn)
        dyh = dn * lng_ref[...]
        m1 = jnp.mean(dyh, axis=-1, keepdims=True)
        m2 = jnp.mean(dyh * yhat, axis=-1, keepdims=True)
        dy1 = rstd * (dyh - m1 - yhat * m2)
        dy1_ref[...] = dy1
        dsum_ref[...] += _colsum(dy1)

    return _call(body, "cv_norm_bwd", (_sds((t, d), F32), _sds((1, d), F32), _sds((1, d), F32), _sds((1, d), F32)),
                 grid=(t // tm,), in_specs=[_tile_spec(tm, d), _tile_spec(tm, d), _vec_spec(d), _vec_spec(d)],
                 out_specs=(_tile_spec(tm, d), _vec_spec(d), _vec_spec(d), _vec_spec(d)))(y1, dy2, lng, lnb)


def _cv_conv_bwd(pre, dy1, dw, seq):
    t, d2 = pre.shape
    d = d2 // 2
    tt = min(TT // 2, seq)
    nt = seq // tt
    hb = tt // CONV_HALO
    last_halo = t // CONV_HALO - 1

    def body(pre_ref, halo_ref, dy_ref, dyn_ref, dw_ref, dpre_ref, ddw_ref, dbin_ref, ext_ref, dext_ref, sh_ref, dsh_ref):
        b, i = pl.program_id(0), pl.program_id(1)

        @pl.when((b == 0) & (i == 0))
        def _():
            ddw_ref[...] = jnp.zeros_like(ddw_ref)
            dbin_ref[...] = jnp.zeros_like(dbin_ref)

        pv = pre_ref[...]
        ext_ref[0:CONV_HALO, :] = jnp.where(i == 0, 0.0, _glu(halo_ref[...], d))
        ext_ref[CONV_HALO:, :] = _glu(pv, d)
        dyv = dy_ref[...]
        dext_ref[0:tt, :] = dyv
        dext_ref[tt:, :] = jnp.where(i == nt - 1, 0.0, dyn_ref[...])
        _shifted_copies(ext_ref, sh_ref, tt)
        _shifted_copies(dext_ref, dsh_ref, tt)
        nrows = tt // CONV_ROWS

        def input_grad(c, _):
            r0 = pl.multiple_of(c * CONV_ROWS, CONV_ROWS)
            dy0 = jnp.zeros((CONV_ROWS, d), F32)
            for k in range(CONV_WIDTH):
                dy0 = dy0 + _rows_from(dext_ref, dsh_ref, r0, CONV_WIDTH - 1 - k, CONV_ROWS) * dw_ref[k:k + 1, :]
            rows = pl.ds(r0, CONV_ROWS)
            a = pre_ref[rows, :d]
            sg = jax.nn.sigmoid(pre_ref[rows, d:])
            da = dy0 * sg
            dg = dy0 * a * sg * (1.0 - sg)
            dpre_ref[rows, :d] = da.astype(BF16)
            dpre_ref[rows, d:] = dg.astype(BF16)
            dbin_ref[:, :d] += _colsum(da)
            dbin_ref[:, d:] += _colsum(dg)
            return 0

        lax.fori_loop(0, nrows, input_grad, 0)

        for k in range(CONV_WIDTH):
            def tap_grad(c, part, k=k):
                r0 = pl.multiple_of(c * CONV_ROWS, CONV_ROWS)
                prod = dy_ref[pl.ds(r0, CONV_ROWS), :] * _rows_from(ext_ref, sh_ref, r0, CONV_HALO - (CONV_WIDTH - 1) + k, CONV_ROWS)
                return part + jnp.sum(prod.reshape(CONV_ROWS // SUBLANES, SUBLANES, d), axis=0)

            ddw_ref[k:k + 1, :] += _colsum(lax.fori_loop(0, nrows, tap_grad, jnp.zeros((SUBLANES, d), F32)))

    return _call(body, "cv_conv_bwd", (_sds((t, d2), BF16), _sds((CONV_HALO, d), F32), _sds((1, d2), F32)), grid=(t // seq, nt),
                 in_specs=[pl.BlockSpec((tt, d2), lambda b, i: (b * nt + i, 0)),
                           pl.BlockSpec((CONV_HALO, d2), lambda b, i: (jnp.maximum((b * nt + i) * hb - 1, 0), 0)),
                           pl.BlockSpec((tt, d), lambda b, i: (b * nt + i, 0)),
                           pl.BlockSpec((CONV_HALO, d), lambda b, i: (jnp.minimum((b * nt + i + 1) * hb, last_halo), 0)),
                           pl.BlockSpec((CONV_HALO, d), lambda b, i: (0, 0))],
                 out_specs=(pl.BlockSpec((tt, d2), lambda b, i: (b * nt + i, 0)),
                            pl.BlockSpec((CONV_HALO, d), lambda b, i: (0, 0)), pl.BlockSpec((1, d2), lambda b, i: (0, 0))),
                 scratch=[pltpu.VMEM((tt + CONV_HALO, d), F32), pltpu.VMEM((tt + CONV_HALO, d), F32),
                          pltpu.VMEM((SUBLANES - 1, tt + CONV_HALO - SUBLANES, d), F32),
                          pltpu.VMEM((SUBLANES - 1, tt + CONV_HALO - SUBLANES, d), F32)],
                 vmem=VMEM_BIG)(pre, pre, dy1, dy1, dw)


def _adamw(name, w, m, v, g=None, parts=None):
    rows, cols = w.shape
    tr = rows
    for cand in ((512,) if parts is None else ()) + (256, 128, 64, 32, 16, 8):
        if rows % cand == 0 and rows > cand:
            tr = cand
            break
    bc1 = 1.0 - ADAM_B1 ** ADAM_STEP
    bc2 = 1.0 - ADAM_B2 ** ADAM_STEP

    def body(w_ref, m_ref, v_ref, g_ref, go_ref, d_ref, mo_ref, vo_ref):
        if parts is None:
            gv = g_ref[...]
        else:
            gv = g_ref[0].astype(F32)
            for s in range(1, N_DEV):
                gv = gv + g_ref[s].astype(F32)
        mn = ADAM_B1 * m_ref[...] + (1.0 - ADAM_B1) * gv
        vn = ADAM_B2 * v_ref[...] + (1.0 - ADAM_B2) * (gv * gv)
        m_hat = mn / bc1
        v_hat = vn / bc2
        go_ref[...] = gv
        d_ref[...] = -ADAM_LR * (m_hat / (jnp.sqrt(v_hat) + ADAM_EPS) + ADAM_WD * w_ref[...])
        mo_ref[...] = mn
        vo_ref[...] = vn

    blk = pl.BlockSpec((tr, cols), lambda i: (i, 0))
    g_in, g_spec = (g, blk) if parts is None else (parts, pl.BlockSpec((N_DEV, tr, cols), lambda i: (0, i, 0)))
    out = _sds((rows, cols), F32)
    return _call(body, name, (out, out, out, out), grid=(rows // tr,), in_specs=[blk, blk, blk, g_spec],
                 out_specs=(blk, blk, blk, blk), vmem=VMEM_BIG)(w, m, v, g_in)


def _pad_rows(a, rows):
    return jnp.pad(a, ((0, rows - a.shape[0]), (0, 0)))


def _full_cols(gathered, n):
    k = gathered.shape[2]
    return jnp.transpose(gathered[0], (1, 0, 2)).reshape(k, N_DEV * n)


def _col_blocks(full, n):
    k = full.shape[0]
    return jnp.transpose(full.reshape(k, N_DEV, n), (1, 0, 2))[None]


def kernel(x, c, mod_w, mod_b, ln1_g, ln1_b, ln2_g, ln2_b, ffn_w_in, ffn_w_out, gm_w_in, gm_b_in, gm_ln_g, gm_ln_b, gm_w_s, gm_b_s, gm_w_out, fox_w_in, fox_b_f, fox_w_out, sb_w_in, sb_w_out, cv_w_in, cv_b_in, cv_dw, cv_dw_b, cv_ln_g, cv_ln_b, cv_w_out, cv_b_out, loss_target, m_mod_w, m_mod_b, m_ln1_g, m_ln1_b, m_ln2_g, m_ln2_b, m_ffn_w_in, m_ffn_w_out, m_gm_w_in, m_gm_b_in, m_gm_ln_g, m_gm_ln_b, m_gm_w_s, m_gm_b_s, m_gm_w_out, m_fox_w_in, m_fox_b_f, m_fox_w_out, m_sb_w_in, m_sb_w_out, m_cv_w_in, m_cv_b_in, m_cv_dw, m_cv_dw_b, m_cv_ln_g, m_cv_ln_b, m_cv_w_out, m_cv_b_out, v_mod_w, v_mod_b, v_ln1_g, v_ln1_b, v_ln2_g, v_ln2_b, v_ffn_w_in, v_ffn_w_out, v_gm_w_in, v_gm_b_in, v_gm_ln_g, v_gm_ln_b, v_gm_w_s, v_gm_b_s, v_gm_w_out, v_fox_w_in, v_fox_b_f, v_fox_w_out, v_sb_w_in, v_sb_w_out, v_cv_w_in, v_cv_b_in, v_cv_dw, v_cv_dw_b, v_cv_ln_g, v_cv_ln_b, v_cv_w_out, v_cv_b_out):
    weights = dict(mod_w=mod_w, mod_b=mod_b, ln1_g=ln1_g, ln1_b=ln1_b, ln2_g=ln2_g, ln2_b=ln2_b, ffn_w_in=ffn_w_in, ffn_w_out=ffn_w_out, gm_w_in=gm_w_in, gm_b_in=gm_b_in, gm_ln_g=gm_ln_g, gm_ln_b=gm_ln_b, gm_w_s=gm_w_s, gm_b_s=gm_b_s, gm_w_out=gm_w_out, fox_w_in=fox_w_in, fox_b_f=fox_b_f, fox_w_out=fox_w_out, sb_w_in=sb_w_in, sb_w_out=sb_w_out, cv_w_in=cv_w_in, cv_b_in=cv_b_in, cv_dw=cv_dw, cv_dw_b=cv_dw_b, cv_ln_g=cv_ln_g, cv_ln_b=cv_ln_b, cv_w_out=cv_w_out, cv_b_out=cv_b_out)
    mom1 = dict(mod_w=m_mod_w, mod_b=m_mod_b, ln1_g=m_ln1_g, ln1_b=m_ln1_b, ln2_g=m_ln2_g, ln2_b=m_ln2_b, ffn_w_in=m_ffn_w_in, ffn_w_out=m_ffn_w_out, gm_w_in=m_gm_w_in, gm_b_in=m_gm_b_in, gm_ln_g=m_gm_ln_g, gm_ln_b=m_gm_ln_b, gm_w_s=m_gm_w_s, gm_b_s=m_gm_b_s, gm_w_out=m_gm_w_out, fox_w_in=m_fox_w_in, fox_b_f=m_fox_b_f, fox_w_out=m_fox_w_out, sb_w_in=m_sb_w_in, sb_w_out=m_sb_w_out, cv_w_in=m_cv_w_in, cv_b_in=m_cv_b_in, cv_dw=m_cv_dw, cv_dw_b=m_cv_dw_b, cv_ln_g=m_cv_ln_g, cv_ln_b=m_cv_ln_b, cv_w_out=m_cv_w_out, cv_b_out=m_cv_b_out)
    mom2 = dict(mod_w=v_mod_w, mod_b=v_mod_b, ln1_g=v_ln1_g, ln1_b=v_ln1_b, ln2_g=v_ln2_g, ln2_b=v_ln2_b, ffn_w_in=v_ffn_w_in, ffn_w_out=v_ffn_w_out, gm_w_in=v_gm_w_in, gm_b_in=v_gm_b_in, gm_ln_g=v_gm_ln_g, gm_ln_b=v_gm_ln_b, gm_w_s=v_gm_w_s, gm_b_s=v_gm_b_s, gm_w_out=v_gm_w_out, fox_w_in=v_fox_w_in, fox_b_f=v_fox_b_f, fox_w_out=v_fox_w_out, sb_w_in=v_sb_w_in, sb_w_out=v_sb_w_out, cv_w_in=v_cv_w_in, cv_b_in=v_cv_b_in, cv_dw=v_cv_dw, cv_dw_b=v_cv_dw_b, cv_ln_g=v_cv_ln_g, cv_ln_b=v_cv_ln_b, cv_w_out=v_cv_w_out, cv_b_out=v_cv_b_out)
    names = list(weights)

    nb, seq, d = x.shape
    t = nb * seq
    nl = mod_w.shape[0]
    alpha = (2.0 * nl) ** 0.25
    me = 4 * lax.axis_index("x") + 2 * lax.axis_index("y") + lax.axis_index("c")
    xs = x.reshape(t, d)
    tgt = loss_target.reshape(t, d)
    n_mod = mod_w.shape[2]
    n_ffn = ffn_w_in.shape[2]
    n_heads = d // HEAD_DIM
    npair = d // LANES

    c_all = _exchange_small(_pad_rows(c, 8), "gather_c", False)[:, :nb].reshape(N_DEV * nb, d)
    mod_b_loc = lax.dynamic_slice_in_dim(mod_b, me * n_mod, n_mod, axis=1)[:, None, :]
    mod_loc = _mod_fwd(c_all, mod_w, mod_b_loc)
    mod_g = _exchange_small(mod_loc.reshape(nl * N_DEV * nb, n_mod), "gather_mod", False)
    mod_all = jnp.transpose(mod_g.reshape(N_DEV, nl, N_DEV * nb, n_mod), (1, 2, 0, 3)).reshape(nl, N_DEV * nb, N_DEV * n_mod)
    mod_me = lax.dynamic_slice_in_dim(mod_all, me * nb, nb, axis=1)
    mods = [[mod_me[l, :, k * d:(k + 1) * d][:, None, :] for k in range(6)] for l in range(nl)]

    assert nl == 4, "the exchange schedule below is written for the four-layer trunk"
    big = ["ffn_w_in", "ffn_w_out", "gm_w_in", "gm_w_out", "fox_w_in", "fox_w_out", "sb_w_in", "sb_w_out", "cv_w_in", "cv_w_out"]
    shard = {n: weights[n].astype(BF16) for n in big if not n.startswith("ffn")}
    for l in range(nl):
        shard["ffn_w_in", l] = ffn_w_in[l:l + 1].astype(BF16)
        shard["ffn_w_out", l] = ffn_w_out[l:l + 1].astype(BF16)
    now = ["gm_w_in", "gm_w_out"]
    with_gm_in = [("ffn_w_in", 0), ("ffn_w_out", 0)]
    with_ffn_0 = ["fox_w_in", "fox_w_out"]
    later = [("ffn_w_in", 1), ("ffn_w_out", 1), "sb_w_in", "sb_w_out", ("ffn_w_in", 2), ("ffn_w_out", 2),
             "cv_w_in", "cv_w_out", ("ffn_w_in", 3), ("ffn_w_out", 3)]
    gathered = dict(zip(now, _gather_weights([shard[n] for n in now])))
    w_ffn_out_rows = lambda l: gathered["ffn_w_out", l].reshape(N_DEV // 2, n_ffn, d)
    sq = lambda n: gathered[n].reshape(d, d)
    cvp = d // N_DEV
    cv_small = jnp.concatenate([_pad_rows(cv_dw[0], CONV_HALO), cv_dw_b, cv_ln_g, cv_ln_b, cv_b_out,
                                cv_b_in.reshape(2, cvp), jnp.zeros((2, cvp), F32)], axis=0)
    cv_all = _exchange_small(cv_small, "gather_cv_small", False)
    cv_rows = jnp.transpose(cv_all, (1, 0, 2)).reshape(cv_small.shape[0], d)
    cv_dw_f, cv_dwb_f, cv_lng_f, cv_lnb_f, cv_bout_f = (cv_rows[:CONV_HALO], cv_rows[32:33], cv_rows[33:34], cv_rows[34:35], cv_rows[35:36])
    cv_bin_f = cv_all[:, 36:38, :].reshape(1, 2 * d)

    saved = []
    h = _modulate(xs, mods[0][1], mods[0][0], seq)
    xin = xs
    for l in range(nl):
        kind = l % 4
        sv = dict(x=xin, h=h)
        if kind == 0:
            pre, arrived = _proj_cols("gm_in", h, gathered["gm_w_in"], gm_w_in.shape[2], bias=gm_b_in,
                                      beside=("gather", [shard[n] for n in with_gm_in]))
            gathered.update(zip(with_gm_in, arrived))
            yv = _gm_spatial_fwd(pre, gm_ln_g, gm_ln_b, gm_w_s[0], jnp.transpose(gm_b_s[0]), seq)
            y = _mm_plain("gm_out", yv, sq("gm_w_out"), NN)
            sv.update(pre=pre, yv=yv)
        elif kind == 1:
            qkv = _qkv_proj("fox_qkv", h, fox_qkv_w)
            ft = _mm("fox_gate_proj", fox_f_wt, h, _sds((n_heads, t), F32), (t // min(TMM, t), 1),
                     pl.BlockSpec((n_heads, d), lambda i, r: (0, 0)), pl.BlockSpec((min(TMM, t), d), lambda i, r: (i, 0)),
                     pl.BlockSpec((n_heads, min(TMM, t)), lambda i, r: (0, i)), NT)
            b_f = jnp.transpose(fox_b_f)
            frow_p = _fox_gate_fwd(ft, b_f, seq).reshape(npair, 2, t)
            o, lse, *arrived = _fox_fwd(qkv, frow_p, nb, seq, gather=[shard[n] for n in later])
            gathered.update(zip(later, arrived))
            sb_qkv_w = _full_cols(gathered["sb_w_in"], sb_w_in.shape[2])
            y = _mm_plain("fox_out", o, sq("fox_w_out"), NN)
            sv.update(qkv=qkv, ft=ft, b_f=b_f, frow=frow_p, o=o, lse=lse)
        elif kind == 2:
            qkv = _qkv_proj("sb_qkv", h, sb_qkv_w)
            o, ltot = _sb_fwd(qkv, nb, seq)
            y = _mm_plain("sb_out", o, sq("sb_w_out"), NN)
            sv.update(qkv=qkv, o=o, ltot=ltot)
        else:
            pre = _proj_cols("cv_in", h, gathered["cv_w_in"], cv_w_in.shape[2], bias=cv_bin_f)
            y1, y2 = _cv_conv_fwd(pre, cv_dw_f, cv_dwb_f, cv_lng_f, cv_lnb_f, seq)
            y = _mm_plain("cv_out", y2, sq("cv_w_out"), NN, bias=cv_bout_f)
            sv.update(pre=pre, y1=y1, y2=y2)
        x1, h2 = _lnres_fwd(xin, y, mods[l][2], ln1_g[l:l + 1], ln1_b[l:l + 1], alpha, seq, nxt=(mods[l][4], mods[l][3]))
        if l == 0:
            (hg, hu, act), arrived = _ffn_in(h2, gathered["ffn_w_in", l], 0, beside=("gather", [shard[n] for n in with_ffn_0]))
            gathered.update(zip(with_ffn_0, arrived))
            fox_full = _full_cols(gathered["fox_w_in"], fox_w_in.shape[2])
            fox_qkv_w, fox_f_wt = fox_full[:, :3 * d], jnp.transpose(fox_full[:, 3 * d:])
        else:
            hg, hu, act = _ffn_in(h2, gathered["ffn_w_in", l], 0)
        y2f = _ffn_out(act, w_ffn_out_rows(l), 0)
        sv.update(y=y, x1=x1, h2=h2, hg=hg, hu=hu, act=act, y2f=y2f)
        if l + 1 < nl:
            xin, h = _lnres_fwd(x1, y2f, mods[l][5], ln2_g[l:l + 1], ln2_b[l:l + 1], alpha, seq, nxt=(mods[l + 1][1], mods[l + 1][0]))
        else:
            xin = _lnres_fwd(x1, y2f, mods[l][5], ln2_g[l:l + 1], ln2_b[l:l + 1], alpha, seq)
        saved.append(sv)

    dx, sq_err = _loss_head(xin, tgt)
    loss = lax.psum(0.5 * jnp.sum(sq_err) / d, ("x", "y", "c"))

    small = {}
    bigg = {}
    recv = {}
    dmod_parts = [dict() for _ in range(nl)]
    pending = None
    d_ln = dict(ln1_g=[None] * nl, ln1_b=[None] * nl, ln2_g=[None] * nl, ln2_b=[None] * nl)
    beside_sb = [("ffn_w_in", 3), ("ffn_w_out", 3), "cv_w_in", "cv_w_out", ("ffn_w_in", 2), ("ffn_w_out", 2)]
    beside_fox = ["sb_w_in", "sb_w_out", ("ffn_w_in", 1), ("ffn_w_out", 1)]
    beside_dact_0 = ["fox_w_in", "fox_w_out"]
    beside_dh_0 = [("ffn_w_in", 0)]
    beside_gm = [("ffn_w_out", 0)]
    at_end = ["gm_w_in", "gm_w_out"]
    for l in reversed(range(nl)):
        sv = saved[l]
        kind = l % 4
        if pending is None:
            dxr, dy2, dlg, dlb, _, dgate2 = _lnres_bwd(dx, sv["x1"], sv["y2f"], mods[l][5], ln2_g[l:l + 1], alpha, seq)
        else:
            dxr, dy2, dlg, dlb, _, dgate2, dsc1, dsh1 = _lnres_bwd(pending[1], sv["x1"], sv["y2f"], mods[l][5], ln2_g[l:l + 1], alpha, seq,
                                                                   through=(pending[0], pending[2], ln2_b[l:l + 1]))
            dmod_parts[l + 1].update(sc1=dsc1, sh1=dsh1)
        d_ln["ln2_g"][l], d_ln["ln2_b"][l] = dlg, dlb
        if l == 0:
            (dg_, du_), landed = _ffn_dact(dy2, w_ffn_out_rows(l), sv["hg"], sv["hu"], 0, beside=("scatter", [bigg[n] for n in beside_dact_0]))
            recv.update(zip(beside_dact_0, landed))
        else:
            dg_, du_ = _ffn_dact(dy2, w_ffn_out_rows(l), sv["hg"], sv["hu"], 0)
        dwg, dwu, dwo = _ffn_bwd_weights(sv["h2"], dy2, sv["act"], dg_, du_)
        bigg["ffn_w_in", l] = jnp.concatenate([dwg, dwu], axis=0)[None]
        bigg["ffn_w_out", l] = dwo.reshape(1, N_DEV, n_ffn // 2, d)
        if l == 0:
            dh2, landed = _ffn_dh(dg_, du_, gathered["ffn_w_in", l], 0, beside=("scatter", [bigg[n] for n in beside_dh_0]))
            recv.update(zip(beside_dh_0, landed))
        else:
            dh2 = _ffn_dh(dg_, du_, gathered["ffn_w_in", l], 0)
        dxr, dy, dlg, dlb, dysum, dgate1, dsc2, dsh2 = _lnres_bwd(dxr, sv["x"], sv["y"], mods[l][2], ln1_g[l:l + 1], alpha, seq,
                                                                  through=(dh2, mods[l][4], ln1_b[l:l + 1]))
        d_ln["ln1_g"][l], d_ln["ln1_b"][l] = dlg, dlb
        hh = sv["h"]
        if kind == 0:
            dyv = _mm_plain("gm_out_bwd", dy, sq("gm_w_out"), NT)
            bigg["gm_w_out"] = _mm_plain("gm_out_dw", sv["yv"], dy, TN, GRAD_WIRE).reshape(1, N_DEV, d // N_DEV, d)
            (dpre, dws, dbst, dlng, dlnb, dbin), landed = _gm_spatial_bwd(sv["pre"], dyv, gm_ln_g, gm_ln_b, gm_w_s[0], jnp.transpose(gm_b_s[0]), seq,
                                                                          beside=("scatter", [bigg[n] for n in beside_gm]))
            recv.update(zip(beside_gm, landed))
            small.update(gm_w_s=dws[None], gm_b_s=jnp.transpose(dbst)[None], gm_ln_g=dlng, gm_ln_b=dlnb, gm_b_in=dbin)
            bigg["gm_w_in"] = _grad_cols("gm_in_dw", hh, dpre, gm_w_in.shape[2])
            dh = _back_cols("gm_in_bwd", dpre, gathered["gm_w_in"], gm_w_in.shape[2])
        elif kind == 1:
            do = _mm_plain("fox_out_bwd", dy, sq("fox_w_out"), NT)
            bigg["fox_w_out"] = _mm_plain("fox_out_dw", sv["o"], dy, TN, GRAD_WIRE).reshape(1, N_DEV, d // N_DEV, d)
            dqkv, dfr, dfq, *landed = _fox_bwd(sv["qkv"], sv["frow"], sv["o"], do, sv["lse"], nb, seq,
                                               scatter=[bigg[n] for n in beside_fox])
            recv.update(zip(beside_fox, landed))
            dft, dbf = _fox_gate_bwd(sv["ft"], sv["b_f"], dfr.reshape(n_heads, t),
                                     jnp.transpose(dfq, (0, 2, 1)).reshape(n_heads, t), seq)
            small["fox_b_f"] = jnp.transpose(dbf)
            dw_qkv = _qkv_dw("fox_qkv_dw", hh, dqkv)
            tk = min(TMM, t)
            dw_ft = _mm("fox_gate_dw", dft, hh, _sds((n_heads, d), F32), (1, t // tk),
                        pl.BlockSpec((n_heads, tk), lambda j, r: (0, r)), pl.BlockSpec((tk, d), lambda j, r: (r, 0)),
                        pl.BlockSpec((n_heads, d), lambda j, r: (0, 0)), NN)
            bigg["fox_w_in"] = _col_blocks(jnp.concatenate([dw_qkv, jnp.transpose(dw_ft).astype(GRAD_WIRE)], axis=1), fox_w_in.shape[2])
            dh_a = _qkv_dh("fox_qkv_bwd", dqkv, fox_qkv_w)
            tm = min(TMM, t)
            dh = _mm("fox_gate_bwd_h", dft, fox_f_wt, _sds((t, d), F32), (t // tm, 1),
                     pl.BlockSpec((n_heads, tm), lambda i, r: (0, i)), pl.BlockSpec((n_heads, d), lambda i, r: (0, 0)),
                     pl.BlockSpec((tm, d), lambda i, r: (i, 0)), TN, (dh_a,), (pl.BlockSpec((tm, d), lambda i, r: (i, 0)),), _add)
        elif kind == 2:
            do = _mm_plain("sb_out_bwd", dy, sq("sb_w_out"), NT)
            bigg["sb_w_out"] = _mm_plain("sb_out_dw", sv["o"], dy, TN, GRAD_WIRE).reshape(1, N_DEV, d // N_DEV, d)
            dqkv, *landed = _sb_bwd(sv["qkv"], do, sv["ltot"], nb, seq, scatter=[bigg[n] for n in beside_sb])
            recv.update(zip(beside_sb, landed))
            bigg["sb_w_in"] = _col_blocks(_qkv_dw("sb_qkv_dw", hh, dqkv), sb_w_in.shape[2])
            dh = _qkv_dh("sb_qkv_bwd", dqkv, sb_qkv_w)
        else:
            dy2c = _mm_plain("cv_out_bwd", dy, sq("cv_w_out"), NT)
            bigg["cv_w_out"] = _mm_plain("cv_out_dw", sv["y2"], dy, TN, GRAD_WIRE).reshape(1, N_DEV, d // N_DEV, d)
            dy1, dlng, dlnb, ddwb = _cv_norm_bwd(sv["y1"], dy2c, cv_lng_f, cv_lnb_f)
            dpre, ddw, dbin = _cv_conv_bwd(sv["pre"], dy1, cv_dw_f, seq)
            small.update(cv_b_out=dysum, cv_ln_g=dlng, cv_ln_b=dlnb, cv_dw_b=ddwb, cv_dw=ddw[:CONV_WIDTH], cv_b_in=dbin)
            bigg["cv_w_in"] = _grad_cols("cv_in_dw", hh, dpre, cv_w_in.shape[2])
            dh = _back_cols("cv_in_bwd", dpre, gathered["cv_w_in"], cv_w_in.shape[2])
        pending = (dh, dxr, mods[l][1])
        dmod_parts[l].update(g1=dgate1, sh2=dsh2, sc2=dsc2, g2=dgate2)
    dx, dsc1, dsh1 = _modulate_bwd(pending[0], pending[1], saved[0]["x"], pending[2], seq)
    dmod_parts[0].update(sc1=dsc1, sh1=dsh1)
    dmods = [jnp.concatenate([p["sh1"], p["sc1"], p["g1"], p["sh2"], p["sc2"], p["g2"]], axis=2)[:, 0, :] for p in dmod_parts]
    grad_x = dx.reshape(nb, seq, d)
    for n in d_ln:
        small[n] = jnp.concatenate(d_ln[n], axis=0)

    dmod_rows = jnp.stack(dmods).reshape(nl * nb, 6 * d)
    dmod_g = _exchange_small(_pad_rows(dmod_rows, 8 * ((nl * nb + 7) // 8)), "gather_dmod", False)[:, :nl * nb]
    dmod_all = jnp.transpose(dmod_g.reshape(N_DEV, nl, nb, 6 * d), (1, 0, 2, 3)).reshape(nl, N_DEV * nb, 6 * d)
    dmod_loc = lax.dynamic_slice_in_dim(dmod_all, me * n_mod, n_mod, axis=2)
    g_mod_w, g_mod_b = _mod_bwd(c_all, dmod_loc, dmod_all)
    grads = dict(mod_w=g_mod_w, mod_b=g_mod_b[:, 0, :])

    rep = ["ln1_g", "ln1_b", "ln2_g", "ln2_b", "gm_b_in", "gm_ln_g", "gm_ln_b", "gm_w_s", "gm_b_s", "fox_b_f"]
    cvs = ["cv_b_in", "cv_dw", "cv_dw_b", "cv_ln_g", "cv_ln_b", "cv_b_out"]

    def rows_of(a):
        flat = a.reshape(-1)
        pad = (-flat.shape[0]) % d
        return jnp.pad(flat, (0, pad)).reshape(-1, d)

    pack_rows = [rows_of(small[n]) for n in rep + cvs]
    counts = [r.shape[0] for r in pack_rows]
    total = sum(counts)
    pack = _pad_rows(jnp.concatenate(pack_rows, axis=0), 8 * ((total + 7) // 8))
    summed = _exchange_small(pack, "allreduce_small", True)
    offs = [sum(counts[:i]) for i in range(len(counts))]
    rep_rows = sum(counts[:len(rep)])
    for n, o_, cnt in zip(rep + cvs, offs, counts):
        full = summed[o_:o_ + cnt].reshape(-1)
        if n in rep:
            grads[n] = full[:weights[n].size].reshape(weights[n].shape)
        else:
            wshape = weights[n].shape
            cols = wshape[-1]
            full = full[:math.prod(wshape[:-1]) * cols * N_DEV].reshape(wshape[:-1] + (cols * N_DEV,))
            grads[n] = lax.dynamic_slice_in_dim(full, me * cols, cols, axis=full.ndim - 1)

    recv.update(zip(at_end, _scatter_grads([bigg[n] for n in at_end])))
    for n in ("ffn_w_in", "ffn_w_out"):
        recv[n] = jnp.concatenate([recv[n, l] for l in range(nl)], axis=1)

    outs = {}

    def view2(a):
        return a.reshape(-1, a.shape[-1])

    for n in big:
        w2 = view2(weights[n])
        res = _adamw("adamw_" + n, w2, view2(mom1[n]), view2(mom2[n]), parts=recv[n].reshape((N_DEV,) + w2.shape))
        outs[n] = [r.reshape(weights[n].shape) for r in res]
    res = _adamw("adamw_mod_w", view2(mod_w), view2(m_mod_w), view2(v_mod_w), g=view2(grads["mod_w"]))
    outs["mod_w"] = [r.reshape(mod_w.shape) for r in res]
    rp = lambda src: _pad_rows(jnp.concatenate([rows_of(src[n]) for n in rep], axis=0), 8 * ((rep_rows + 7) // 8))
    res = _adamw("adamw_replicated", rp(weights), rp(mom1), rp(mom2), g=rp(grads))
    for n, o_, cnt in zip(rep, offs, counts):
        outs[n] = [r[o_:o_ + cnt].reshape(-1)[:weights[n].size].reshape(weights[n].shape) for r in res]
    cv_cols = weights["cv_b_out"].shape[-1]
    cp = lambda src: jnp.concatenate([src[n].reshape(-1, cv_cols) for n in cvs], axis=0)
    cv_cnt = [weights[n].size // cv_cols for n in cvs]
    cv_tot = sum(cv_cnt)
    cpp = lambda src: _pad_rows(cp(src), 8 * ((cv_tot + 7) // 8))
    res = _adamw("adamw_cv_small", cpp(weights), cpp(mom1), cpp(mom2), g=cpp(grads))
    o_ = 0
    for n, cnt in zip(cvs, cv_cnt):
        outs[n] = [r[o_:o_ + cnt].reshape(weights[n].shape) for r in res]
        o_ += cnt
    res = _adamw("adamw_mod_b", mod_b, m_mod_b, v_mod_b, g=grads["mod_b"])
    outs["mod_b"] = list(res)

    return (loss, grad_x, *[outs[n][0] for n in names], *[outs[n][1] for n in names],
            *[outs[n][2] for n in names], *[outs[n][3] for n in names])
```

```python
import functools
import math

import jax
import jax.numpy as jnp
from jax import lax
from jax.experimental import pallas as pl
from jax.experimental.pallas import tpu as pltpu

F32 = jnp.float32
BF16 = jnp.bfloat16
MESH = pl.DeviceIdType.MESH

N_DEV = 8
HEAD_DIM = 64
LANES = 128
SUBLANES = 8
GM_CHUNK = 128
GM_GROUPS = 8
CONV_WIDTH = 31
CONV_HALO = 32
CONV_ROWS = 32
LN_EPS = 1e-5
NEG_INF = -1e30
SB_DEAD = -100.0
GRAD_WIRE = jnp.bfloat16
FFN_KEEP = jnp.bfloat16

ADAM_LR = 0.001
ADAM_B1 = 0.9
ADAM_B2 = 0.999
ADAM_EPS = 1e-08
ADAM_WD = 0.01
ADAM_STEP = 10

TM = 512
TMM = 1024
TQ = 256
FOX_FWD_UNROLL = 8
FOX_BWD_UNROLL = 4
SB_BWD_UNROLL = 2
TT = 512
VMEM_BIG = 56 * 1024 * 1024

NN = (((1,), (0,)), ((), ()))
NT = (((1,), (1,)), ((), ()))
TN = (((0,), (0,)), ((), ()))


def _call(body, name, out_shape, grid=None, in_specs=None, out_specs=None, scratch=(), vmem=None):
    params = {}
    if grid is not None:
        params["dimension_semantics"] = ("arbitrary",) * len(grid)
    if vmem is not None:
        params["vmem_limit_bytes"] = vmem
    kw = {}
    if grid is not None:
        kw["grid"] = grid
    if in_specs is not None:
        kw["in_specs"] = in_specs
    if out_specs is not None:
        kw["out_specs"] = out_specs
    return pl.pallas_call(body, name=name, out_shape=out_shape, scratch_shapes=list(scratch),
                          compiler_params=pltpu.CompilerParams(**params), **kw)


def _sds(shape, dtype):
    return jax.ShapeDtypeStruct(tuple(shape), dtype)


def _dot(a, b, dims=NN):
    return lax.dot_general(a.astype(BF16), b.astype(BF16), dims, preferred_element_type=F32)


def _split3(x):
    h1 = x.astype(BF16)
    r1 = x - h1.astype(F32)
    h2 = r1.astype(BF16)
    h3 = (r1 - h2.astype(F32)).astype(BF16)
    return h1, h2, h3


def _dot_exact(x, m, dims=NN):
    h1, h2, h3 = _split3(x)
    d = lambda h: lax.dot_general(h, m, dims, preferred_element_type=F32)
    return (d(h1) + d(h2)) + d(h3)


def _dot_exact_rhs(m, x, dims=NN):
    h1, h2, h3 = _split3(x)
    d = lambda h: lax.dot_general(m, h, dims, preferred_element_type=F32)
    return (d(h1) + d(h2)) + d(h3)


def _silu(x):
    return x * jax.nn.sigmoid(x)


def _gelu(x):
    return 0.5 * x * (1.0 + lax.erf(x * (2.0 ** -0.5)))


def _gelu_grad(x):
    return 0.5 * (1.0 + lax.erf(x * (2.0 ** -0.5))) + x * jnp.exp(-0.5 * x * x) * ((2.0 * math.pi) ** -0.5)


def _log_sigmoid(z):
    return jnp.minimum(z, 0.0) - jnp.log(1.0 + jnp.exp(-jnp.abs(z)))


def _ln_stats(r):
    mu = jnp.mean(r, axis=-1, keepdims=True)
    rc = r - mu
    var = jnp.mean(rc * rc, axis=-1, keepdims=True)
    return rc, lax.rsqrt(var + LN_EPS)


def _colsum(x):
    return jnp.sum(x, axis=0, keepdims=True)


def _peers():
    mx, my, mc = lax.axis_index("x"), lax.axis_index("y"), lax.axis_index("c")
    me = 4 * mx + 2 * my + mc
    out = []
    for k in range(1, N_DEV):
        px = 1 - mx if (k >> 2) & 1 else mx
        py = 1 - my if (k >> 1) & 1 else my
        pc = 1 - mc if k & 1 else mc
        out.append(((px, py, pc), 4 * px + 2 * py + pc))
    return me, out


def _exchange_small(x, name, reduce):
    rows, cols = x.shape

    def body(x_ref, o_ref, *rest):
        if reduce:
            land, send_sems, recv_sems, local_sem = rest
        else:
            land = o_ref
            send_sems, recv_sems, local_sem = rest
        me, peers = _peers()
        mine = pltpu.make_async_copy(x_ref, land.at[me], local_sem)
        mine.start()
        sends = []
        for k, (peer, _) in enumerate(peers):
            cp = pltpu.make_async_remote_copy(src_ref=x_ref, dst_ref=land.at[me], send_sem=send_sems.at[k],
                                              recv_sem=recv_sems.at[k], device_id=peer, device_id_type=MESH)
            cp.start()
            sends.append(cp)
        for k, (peer, blk) in enumerate(peers):
            pltpu.make_async_remote_copy(src_ref=x_ref, dst_ref=land.at[blk], send_sem=send_sems.at[k],
                                         recv_sem=recv_sems.at[k], device_id=peer, device_id_type=MESH).wait_recv()
        for cp in sends:
            cp.wait_send()
        mine.wait()
        if reduce:
            acc = land[0]
            for s in range(1, N_DEV):
                acc = acc + land[s]
            o_ref[...] = acc

    vm = pl.BlockSpec(memory_space=pltpu.VMEM)
    scratch = [pltpu.SemaphoreType.DMA((N_DEV - 1,)), pltpu.SemaphoreType.DMA((N_DEV - 1,)), pltpu.SemaphoreType.DMA]
    if reduce:
        scratch = [pltpu.VMEM((N_DEV, rows, cols), F32)] + scratch
        out = _sds((rows, cols), F32)
    else:
        out = _sds((N_DEV, rows, cols), F32)
    return _call(body, name, out, in_specs=[vm], out_specs=vm, scratch=scratch, vmem=VMEM_BIG)(x)


def _comm_scratch(n):
    return [pltpu.SemaphoreType.DMA((n, N_DEV - 1)), pltpu.SemaphoreType.DMA((n, N_DEV - 1)), pltpu.SemaphoreType.DMA((n,))]


def _gather_stage(stage, ins, outs, send_sems, recv_sems, local_sems):
    n = len(ins)
    mx, my, mc = lax.axis_index("x"), lax.axis_index("y"), lax.axis_index("c")
    here, sibling = (mx, my, mc), (mx, my, 1 - mc)
    chips = [(1 - mx, my), (mx, 1 - my), (1 - mx, 1 - my)]

    def block(px, py, pc):
        return 4 * px + 2 * py + pc

    def copy(a, k, blk, to, src=None):
        dst = outs[a].at[:, blk]
        return pltpu.make_async_remote_copy(src_ref=dst if src is None else src, dst_ref=dst, send_sem=send_sems.at[a, k],
                                            recv_sem=recv_sems.at[a, k], device_id=to, device_id_type=MESH)

    me = block(*here)
    for a in range(n):
        local = pltpu.make_async_copy(ins[a], outs[a].at[:, me], local_sems.at[a])
        first = [copy(a, 0, me, sibling, src=ins[a])] + [copy(a, 1 + j, me, (*chip, mc), src=ins[a]) for j, chip in enumerate(chips)]
        if stage == 0:
            local.start()
            for cp in first:
                cp.start()
        if stage == 1:
            for j, chip in enumerate(chips):
                copy(a, 1 + j, block(*chip, mc), here).wait_recv()
                copy(a, 4 + j, block(*chip, mc), sibling).start()
        if stage == 2:
            copy(a, 0, block(mx, my, 1 - mc), here).wait_recv()
            for j, chip in enumerate(chips):
                copy(a, 4 + j, block(*chip, 1 - mc), here).wait_recv()
            for cp in first:
                cp.wait_send()
            for j, chip in enumerate(chips):
                copy(a, 4 + j, block(*chip, mc), sibling).wait_send()
            local.wait()


def _gather_weights(shards):
    n = len(shards)

    def body(*refs):
        for stage in range(3):
            _gather_stage(stage, refs[:n], refs[n:2 * n], *refs[2 * n:])

    hbm = pl.BlockSpec(memory_space=pl.ANY)
    return _call(body, "gather_weights", _gathered_shapes(shards), in_specs=[hbm] * n, out_specs=[hbm] * n, scratch=_comm_scratch(n))(*shards)


def _gathered_shapes(shards):
    return [_sds((s.shape[0], N_DEV) + s.shape[1:], s.dtype) for s in shards]


def _scattered_shapes(grads):
    return [_sds((N_DEV, g.shape[0]) + g.shape[2:], g.dtype) for g in grads]


def _run(beside, body, name, out_shape, args, grid, in_specs, out_specs, scratch=(), vmem=None):
    if beside is None:
        return _call(body, name, out_shape, grid=grid, in_specs=in_specs, out_specs=out_specs, scratch=scratch, vmem=vmem)(*args)
    kind, arrays = beside
    n = len(arrays)
    outs = tuple(out_shape) if isinstance(out_shape, (tuple, list)) else (out_shape,)
    ospecs = tuple(out_specs) if isinstance(out_specs, (tuple, list)) else (out_specs,)
    total = math.prod(grid)
    gather = kind == "gather"
    stage_fn = _gather_stage if gather else _scatter_stage
    early = {0: 0, 1: (3 * total) // 4} if gather else {0: 0}

    def carrier(*refs):
        own_in, send, own_out, land, sems, own_scratch = _carried(n, len(in_specs), len(outs), refs)
        at = 0
        for ax, size in enumerate(grid):
            at = at * size + pl.program_id(ax)
        for stage, when in early.items():
            @pl.when(at == when)
            def _(stage=stage):
                stage_fn(stage, send, land, *sems)
        body(*own_in, *own_out, *own_scratch)

        @pl.when(at == total - 1)
        def _():
            stage_fn(2 if gather else 1, send, land, *sems)

    hbm = pl.BlockSpec(memory_space=pl.ANY)
    shapes = _gathered_shapes(arrays) if gather else _scattered_shapes(arrays)
    res = _call(carrier, name, (*outs, *shapes), grid=grid, in_specs=[*in_specs, *[hbm] * n], out_specs=(*ospecs, *[hbm] * n),
                scratch=[*_comm_scratch(n), *scratch], vmem=vmem)(*args, *arrays)
    own = res[:len(outs)]
    return (own if len(own) > 1 else own[0]), list(res[len(outs):])


def _scatter_stage(stage, ins, outs, send_sems, recv_sems, local_sems):
    me, peers = _peers()
    for a in range(len(ins)):
        local = pltpu.make_async_copy(ins[a].at[:, me], outs[a].at[me], local_sems.at[a])
        sends = [pltpu.make_async_remote_copy(src_ref=ins[a].at[:, blk], dst_ref=outs[a].at[me], send_sem=send_sems.at[a, k],
                                              recv_sem=recv_sems.at[a, k], device_id=peer, device_id_type=MESH)
                 for k, (peer, blk) in enumerate(peers)]
        if stage == 0:
            local.start()
            for cp in sends:
                cp.start()
        if stage == 1:
            for k, (peer, blk) in enumerate(peers):
                pltpu.make_async_remote_copy(src_ref=ins[a].at[:, me], dst_ref=outs[a].at[blk], send_sem=send_sems.at[a, k],
                                             recv_sem=recv_sems.at[a, k], device_id=peer, device_id_type=MESH).wait_recv()
            for cp in sends:
                cp.wait_send()
            local.wait()


def _mm(name, a, b, out, grid, a_spec, b_spec, o_spec, dims, extra=(), extra_specs=(), epilogue=None, vmem=VMEM_BIG, beside=None):
    nred = grid[-1]
    red_axis = len(grid) - 1
    acc_shape = tuple(d for d in o_spec.block_shape if d is not None)
    n_extra = len(extra)

    def body(a_ref, b_ref, *rest):
        ex = rest[:n_extra]
        o_ref = rest[n_extra]

        def finish(acc):
            if epilogue is not None:
                acc = epilogue(acc, *[e[...] for e in ex])
            o_ref[...] = acc.astype(o_ref.dtype)

        prod = _dot(a_ref[...], b_ref[...], dims)
        if nred == 1:
            finish(prod)
        else:
            acc_ref = rest[n_extra + 1]
            r = pl.program_id(red_axis)

            @pl.when(r == 0)
            def _():
                acc_ref[...] = prod

            @pl.when(r > 0)
            def _():
                acc_ref[...] += prod

            @pl.when(r == nred - 1)
            def _():
                finish(acc_ref[...])

    scratch = [pltpu.VMEM(acc_shape, F32)] if nred > 1 else []
    return _run(beside, body, name, out, (a, b, *extra), grid, [a_spec, b_spec, *extra_specs], o_spec, scratch, vmem)


def _add(acc, x):
    return acc + x


def _proj_cols(name, h, w, n_slot, bias=None, out_dtype=F32, beside=None):
    t, k = h.shape
    s = w.shape[1]
    tm = min(TM, t)

    def body(h_ref, w_ref, *rest):
        o_ref = rest[-1]
        hv = h_ref[...]
        for j in range(s):
            cols = slice(j * n_slot, (j + 1) * n_slot)
            acc = _dot(hv, w_ref[j])
            if bias is not None:
                acc = acc + rest[0][:, cols]
            o_ref[:, cols] = acc.astype(o_ref.dtype)

    ins, in_specs = [h, w], [pl.BlockSpec((tm, k), lambda i: (i, 0)), pl.BlockSpec((None, s, k, n_slot), lambda i: (0, 0, 0, 0))]
    if bias is not None:
        ins.append(bias)
        in_specs.append(pl.BlockSpec((1, s * n_slot), lambda i: (0, 0)))
    return _run(beside, body, name, _sds((t, s * n_slot), out_dtype), tuple(ins), (t // tm,), in_specs,
                pl.BlockSpec((tm, s * n_slot), lambda i: (i, 0)), vmem=VMEM_BIG)


def _accumulate_over_tokens(name, ins, in_specs, out, o_spec, acc_shape, n_steps, terms, store=None):
    def body(*refs):
        o_ref, acc_ref = refs[len(ins)], refs[len(ins) + 1]
        r = pl.program_id(0)

        @pl.when(r == 0)
        def _():
            acc_ref[...] = jnp.zeros_like(acc_ref)

        for s, prod in enumerate(terms(*refs[:len(ins)])):
            acc_ref[s] += prod

        @pl.when(r == n_steps - 1)
        def _():
            if store is None:
                o_ref[...] = acc_ref[...].reshape(o_ref.shape).astype(o_ref.dtype)
            else:
                store(o_ref, acc_ref)

    return _call(body, name, out, grid=(n_steps,), in_specs=in_specs, out_specs=o_spec,
                 scratch=[pltpu.VMEM(acc_shape, F32)], vmem=VMEM_BIG)(*ins)


def _grad_cols(name, h, g, n_slot):
    t, k = h.shape
    s = g.shape[1] // n_slot
    tk = min(TMM, t)

    def terms(h_ref, g_ref):
        hv = h_ref[...]
        return [_dot(hv, g_ref[:, j * n_slot:(j + 1) * n_slot], TN) for j in range(s)]

    return _accumulate_over_tokens(name, (h, g), [pl.BlockSpec((tk, k), lambda r: (r, 0)), pl.BlockSpec((tk, s * n_slot), lambda r: (r, 0))],
                                   _sds((1, s, k, n_slot), GRAD_WIRE), pl.BlockSpec((1, s, k, n_slot), lambda r: (0, 0, 0, 0)),
                                   (s, k, n_slot), t // tk, terms)


def _back_cols(name, g, w, n_slot, beside=None):
    t = g.shape[0]
    s, k = w.shape[1], w.shape[2]
    tm = min(TM, t)

    def body(g_ref, w_ref, o_ref):
        acc = None
        for j in range(s):
            term = _dot(g_ref[:, j * n_slot:(j + 1) * n_slot], w_ref[j], NT)
            acc = term if acc is None else acc + term
        o_ref[...] = acc

    return _run(beside, body, name, _sds((t, k), F32), (g, w), (t // tm,),
                [pl.BlockSpec((tm, s * n_slot), lambda i: (i, 0)), pl.BlockSpec((None, s, k, n_slot), lambda i: (0, 0, 0, 0))],
                pl.BlockSpec((tm, k), lambda i: (i, 0)), vmem=VMEM_BIG)


def _mm_plain(name, a, b, dims, out_dtype=F32, bias=None):
    if dims == TN:
        t, k = a.shape
        n = b.shape[1]
        tk = min(TMM, t)
        return _mm(name, a, b, _sds((k, n), out_dtype), (1, t // tk),
                   pl.BlockSpec((tk, k), lambda j, r: (r, 0)), pl.BlockSpec((tk, n), lambda j, r: (r, 0)),
                   pl.BlockSpec((k, n), lambda j, r: (0, 0)), TN)
    t = a.shape[0]
    tm = min(TMM, t)
    n = b.shape[1] if dims == NN else b.shape[0]
    extra, especs, epi = (), (), None
    if bias is not None:
        extra, especs, epi = (bias,), (pl.BlockSpec((1, n), lambda i, r: (0, 0)),), _add
    return _mm(name, a, b, _sds((t, n), out_dtype), (t // tm, 1),
               pl.BlockSpec((tm, a.shape[1]), lambda i, r: (i, 0)), pl.BlockSpec(b.shape, lambda i, r: (0, 0)),
               pl.BlockSpec((tm, n), lambda i, r: (i, 0)), dims, extra, especs, epi)


def _mod_fwd(c_all, mod_w, mod_b_loc):
    nl, d, n = mod_w.shape
    nb = c_all.shape[0]

    def body(c_ref, w_ref, b_ref, o_ref):
        o_ref[...] = _dot(_silu(c_ref[...]), w_ref[...]) + b_ref[...]

    return _call(body, "mod_fwd", _sds((nl, nb, n), F32), grid=(nl,),
                 in_specs=[pl.BlockSpec((nb, d), lambda l: (0, 0)), pl.BlockSpec((None, d, n), lambda l: (l, 0, 0)),
                           pl.BlockSpec((None, 1, n), lambda l: (l, 0, 0))],
                 out_specs=pl.BlockSpec((None, nb, n), lambda l: (l, 0, 0)))(c_all, mod_w, mod_b_loc)


def _mod_bwd(c_all, dmod_loc, dmod_all):
    nl, nb, n = dmod_loc.shape
    d = c_all.shape[1]
    n_all = dmod_all.shape[2]

    def body(c_ref, dl_ref, da_ref, gw_ref, gb_ref):
        gw_ref[...] = _dot(_silu(c_ref[...]), dl_ref[...], TN)
        gb_ref[...] = _colsum(da_ref[...])

    return _call(body, "mod_bwd", (_sds((nl, d, n), F32), _sds((nl, 1, n_all), F32)), grid=(nl,),
                 in_specs=[pl.BlockSpec((nb, d), lambda l: (0, 0)), pl.BlockSpec((None, nb, n), lambda l: (l, 0, 0)),
                           pl.BlockSpec((None, nb, n_all), lambda l: (l, 0, 0))],
                 out_specs=(pl.BlockSpec((None, d, n), lambda l: (l, 0, 0)), pl.BlockSpec((None, 1, n_all), lambda l: (l, 0, 0))),
                 )(c_all, dmod_loc, dmod_all)


def _row_spec(d, tpb):
    return pl.BlockSpec((None, 1, d), lambda i: (i // tpb, 0, 0))


def _tile_spec(tm, d):
    return pl.BlockSpec((tm, d), lambda i: (i, 0))


def _vec_spec(d):
    return pl.BlockSpec((1, d), lambda i: (0, 0))


def _modulate(x, sc, sh, seq):
    t, d = x.shape
    tm = min(TM, seq)
    tpb = seq // tm

    def body(x_ref, sc_ref, sh_ref, h_ref):
        h_ref[...] = (x_ref[...] * (1.0 + sc_ref[...]) + sh_ref[...]).astype(BF16)

    return _call(body, "modulate", _sds((t, d), BF16), grid=(t // tm,),
                 in_specs=[_tile_spec(tm, d), _row_spec(d, tpb), _row_spec(d, tpb)], out_specs=_tile_spec(tm, d))(x, sc, sh)


def _lnres_fwd(x, y, gate, lg, lb, alpha, seq, nxt=None):
    t, d = x.shape
    tm = min(TM, seq)
    tpb = seq // tm

    def body(x_ref, y_ref, g_ref, lg_ref, lb_ref, *rest):
        r = alpha * x_ref[...] + (1.0 + g_ref[...]) * y_ref[...]
        rc, rstd = _ln_stats(r)
        xn = rc * rstd * lg_ref[...] + lb_ref[...]
        if nxt is None:
            rest[0][...] = xn
        else:
            sc_ref, sh_ref, xo_ref, h_ref = rest
            xo_ref[...] = xn
            h_ref[...] = (xn * (1.0 + sc_ref[...]) + sh_ref[...]).astype(BF16)

    ins = [_tile_spec(tm, d), _tile_spec(tm, d), _row_spec(d, tpb), _vec_spec(d), _vec_spec(d)]
    if nxt is None:
        return _call(body, "lnres_fwd_last", _sds((t, d), F32), grid=(t // tm,), in_specs=ins,
                     out_specs=_tile_spec(tm, d))(x, y, gate, lg, lb)
    return _call(body, "lnres_fwd", (_sds((t, d), F32), _sds((t, d), BF16)), grid=(t // tm,),
                 in_specs=ins + [_row_spec(d, tpb), _row_spec(d, tpb)],
                 out_specs=(_tile_spec(tm, d), _tile_spec(tm, d)))(x, y, gate, lg, lb, *nxt)


def _loss_head(x, tgt):
    t, d = x.shape
    tm = min(TM, t)

    def body(x_ref, t_ref, dx_ref, sq_ref):
        e = x_ref[...] - t_ref[...]
        dx_ref[...] = e * (1.0 / d)

        @pl.when(pl.program_id(0) == 0)
        def _():
            sq_ref[...] = jnp.zeros_like(sq_ref)

        sq_ref[...] += _colsum(e * e)

    return _call(body, "loss_head", (_sds((t, d), F32), _sds((1, d), F32)), grid=(t // tm,),
                 in_specs=[_tile_spec(tm, d), _tile_spec(tm, d)], out_specs=(_tile_spec(tm, d), _vec_spec(d)))(x, tgt)


def _lnres_bwd(dxo, x, y, gate, lg, alpha, seq, through=None):
    t, d = x.shape
    tm = min(TM, seq)
    tpb = seq // tm
    nb = t // seq

    def body(dxo_ref, x_ref, y_ref, g_ref, lg_ref, *rest):
        if through is None:
            dxr_ref, dy_ref, dlg_ref, dlb_ref, dys_ref, dg_ref = rest
        else:
            dh_ref, sc_ref, lb_ref, dxr_ref, dy_ref, dlg_ref, dlb_ref, dys_ref, dg_ref, dsc_ref, dsh_ref = rest
        i = pl.program_id(0)
        yv = y_ref[...]
        r = alpha * x_ref[...] + (1.0 + g_ref[...]) * yv
        rc, rstd = _ln_stats(r)
        xhat = rc * rstd
        dxo_v = dxo_ref[...]
        if through is not None:
            dhv = dh_ref[...]
            dxo_v = dxo_v + dhv * (1.0 + sc_ref[...])

            @pl.when(i % tpb == 0)
            def _():
                dsc_ref[...] = jnp.zeros_like(dsc_ref)
                dsh_ref[...] = jnp.zeros_like(dsh_ref)

            dsc_ref[...] += _colsum(dhv * (xhat * lg_ref[...] + lb_ref[...]))
            dsh_ref[...] += _colsum(dhv)
        dxh = dxo_v * lg_ref[...]
        m1 = jnp.mean(dxh, axis=-1, keepdims=True)
        m2 = jnp.mean(dxh * xhat, axis=-1, keepdims=True)
        dr = rstd * (dxh - m1 - xhat * m2)
        dyv = (1.0 + g_ref[...]) * dr
        dxr_ref[...] = alpha * dr
        dy_ref[...] = dyv.astype(BF16)

        @pl.when(i == 0)
        def _():
            dlg_ref[...] = jnp.zeros_like(dlg_ref)
            dlb_ref[...] = jnp.zeros_like(dlb_ref)
            dys_ref[...] = jnp.zeros_like(dys_ref)

        @pl.when(i % tpb == 0)
        def _():
            dg_ref[...] = jnp.zeros_like(dg_ref)

        dlg_ref[...] += _colsum(dxo_v * xhat)
        dlb_ref[...] += _colsum(dxo_v)
        dys_ref[...] += _colsum(dyv)
        dg_ref[...] += _colsum(dr * yv)

    out = [_sds((t, d), F32), _sds((t, d), BF16), _sds((1, d), F32), _sds((1, d), F32), _sds((1, d), F32), _sds((nb, 1, d), F32)]
    out_specs = [_tile_spec(tm, d), _tile_spec(tm, d), _vec_spec(d), _vec_spec(d), _vec_spec(d), _row_spec(d, tpb)]
    ins = [dxo, x, y, gate, lg]
    in_specs = [_tile_spec(tm, d), _tile_spec(tm, d), _tile_spec(tm, d), _row_spec(d, tpb), _vec_spec(d)]
    if through is not None:
        ins += list(through)
        in_specs += [_tile_spec(tm, d), _row_spec(d, tpb), _vec_spec(d)]
        out += [_sds((nb, 1, d), F32), _sds((nb, 1, d), F32)]
        out_specs += [_row_spec(d, tpb), _row_spec(d, tpb)]
    return _call(body, "lnres_bwd" if through is None else "lnres_mod_bwd", tuple(out), grid=(t // tm,),
                 in_specs=in_specs, out_specs=tuple(out_specs))(*ins)


def _modulate_bwd(dh, dxr, x, sc, seq):
    t, d = x.shape
    tm = min(TM, seq)
    tpb = seq // tm
    nb = t // seq

    def body(dh_ref, dxr_ref, x_ref, sc_ref, dx_ref, dsc_ref, dsh_ref):
        dhv = dh_ref[...]
        dx_ref[...] = dxr_ref[...] + dhv * (1.0 + sc_ref[...])

        @pl.when(pl.program_id(0) % tpb == 0)
        def _():
            dsc_ref[...] = jnp.zeros_like(dsc_ref)
            dsh_ref[...] = jnp.zeros_like(dsh_ref)

        dsc_ref[...] += _colsum(dhv * x_ref[...])
        dsh_ref[...] += _colsum(dhv)

    return _call(body, "modulate_bwd", (_sds((t, d), F32), _sds((nb, 1, d), F32), _sds((nb, 1, d), F32)), grid=(t // tm,),
                 in_specs=[_tile_spec(tm, d), _tile_spec(tm, d), _tile_spec(tm, d), _row_spec(d, tpb)],
                 out_specs=(_tile_spec(tm, d), _row_spec(d, tpb), _row_spec(d, tpb)))(dh, dxr, x, sc)


def _ffn_in(h, w_in, layer, beside=None):
    t, d = h.shape
    n = w_in.shape[3]
    half = N_DEV // 2
    tm = min(TMM, t)

    def body(h_ref, wg_ref, wu_ref, g_ref, u_ref, a_ref):
        hv = h_ref[...]
        g = _dot(hv, wg_ref[...])
        u = _dot(hv, wu_ref[...])
        g_ref[...] = g.astype(FFN_KEEP)
        u_ref[...] = u.astype(FFN_KEEP)
        a_ref[...] = (_silu(g) * u).astype(BF16)

    blk = pl.BlockSpec((None, tm, n), lambda p, i: (p, i, 0))
    return _run(beside, body, "ffn_in", (_sds((half, t, n), FFN_KEEP), _sds((half, t, n), FFN_KEEP), _sds((half, t, n), BF16)),
                (h, w_in, w_in), (half, t // tm),
                [pl.BlockSpec((tm, d), lambda p, i: (i, 0)),
                 pl.BlockSpec((None, None, d, n), lambda p, i: (layer, p, 0, 0)),
                 pl.BlockSpec((None, None, d, n), lambda p, i: (layer, p + half, 0, 0))],
                (blk, blk, blk), vmem=VMEM_BIG)


def _ffn_out(act, w_out, layer):
    half, t, n = act.shape
    d = w_out.shape[2]
    tm = min(TM, t)

    def body(a_ref, w_ref, o_ref):
        acc = _dot(a_ref[0], w_ref[0])
        for p in range(1, half):
            acc = acc + _dot(a_ref[p], w_ref[p])
        o_ref[...] = acc

    return _call(body, "ffn_out", _sds((t, d), F32), grid=(t // tm,),
                 in_specs=[pl.BlockSpec((half, tm, n), lambda i: (0, i, 0)),
                           pl.BlockSpec((half, n, d), lambda i: (layer // half, 0, 0))],
                 out_specs=pl.BlockSpec((tm, d), lambda i: (i, 0)), vmem=VMEM_BIG)(act, w_out)


def _ffn_dact(dy, w_out, hg, hu, layer, beside=None):
    half, t, n = hg.shape
    d = dy.shape[1]
    tm = min(TMM, t)

    def body(dy_ref, w_ref, g_ref, u_ref, dg_ref, du_ref):
        da = _dot(dy_ref[...], w_ref[...], NT)
        g = g_ref[...].astype(F32)
        sg = jax.nn.sigmoid(g)
        dg_ref[...] = (da * u_ref[...].astype(F32) * (sg * (1.0 + g * (1.0 - sg)))).astype(BF16)
        du_ref[...] = (da * (g * sg)).astype(BF16)

    blk = pl.BlockSpec((None, tm, n), lambda p, i: (p, i, 0))
    return _run(beside, body, "ffn_dact", (_sds((half, t, n), BF16), _sds((half, t, n), BF16)), (dy, w_out, hg, hu), (half, t // tm),
                [pl.BlockSpec((tm, d), lambda p, i: (i, 0)), pl.BlockSpec((None, n, d), lambda p, i: (layer + p, 0, 0)), blk, blk],
                (blk, blk), vmem=VMEM_BIG)


def _ffn_bwd_weights(h, dy, act, dg, du):
    half, t, n = act.shape
    d = h.shape[1]
    tk = min(TMM, t)
    h_spec = pl.BlockSpec((tk, d), lambda r: (r, 0))
    g_spec = pl.BlockSpec((half, tk, n), lambda r: (0, r, 0))

    def in_terms(h_ref, g_ref):
        hv = h_ref[...]
        return [_dot(hv, g_ref[p], TN) for p in range(half)]

    def out_terms(a_ref, dy_ref):
        dyv = dy_ref[...]
        return [_dot(a_ref[p], dyv, TN) for p in range(half)]

    w_in_spec = pl.BlockSpec((half, d, n), lambda r: (0, 0, 0))
    dwg = _accumulate_over_tokens("ffn_dw_gate", (h, dg), [h_spec, g_spec], _sds((half, d, n), GRAD_WIRE), w_in_spec,
                                  (half, d, n), t // tk, in_terms)
    dwu = _accumulate_over_tokens("ffn_dw_up", (h, du), [h_spec, g_spec], _sds((half, d, n), GRAD_WIRE), w_in_spec,
                                  (half, d, n), t // tk, in_terms)
    dwo = _accumulate_over_tokens("ffn_dw_out", (act, dy), [g_spec, h_spec], _sds((half, n, d), GRAD_WIRE),
                                  pl.BlockSpec((half, n, d), lambda r: (0, 0, 0)), (half, n, d), t // tk, out_terms)
    return dwg, dwu, dwo


def _ffn_dh(dg, du, w_in, layer, beside=None):
    half, t, n = dg.shape
    d = w_in.shape[2]
    tm = min(TM, t)

    def body(dg_ref, du_ref, w_ref, o_ref):
        acc = None
        for p in range(half):
            for ref, q in ((dg_ref, p), (du_ref, p + half)):
                term = _dot(ref[p], w_ref[q], NT)
                acc = term if acc is None else acc + term
        o_ref[...] = acc

    g_spec = pl.BlockSpec((half, tm, n), lambda i: (0, i, 0))
    return _run(beside, body, "ffn_dh", _sds((t, d), F32), (dg, du, w_in), (t // tm,),
                [g_spec, g_spec, pl.BlockSpec((None, 2 * half, d, n), lambda i: (layer, 0, 0, 0))],
                pl.BlockSpec((tm, d), lambda i: (i, 0)), vmem=VMEM_BIG)


def _tril(n, strict=False):
    r = lax.broadcasted_iota(jnp.int32, (n, n), 0)
    c = lax.broadcasted_iota(jnp.int32, (n, n), 1)
    return c < r if strict else c <= r


def _gm_spatial_fwd(pre, lng, lnb, w_s, b_st, seq, beside=None):
    t, w2 = pre.shape
    w = w2 // 2
    gd = w // GM_GROUPS
    tm = min(TM, seq)
    nch = tm // GM_CHUNK

    def body(pre_ref, lng_ref, lnb_ref, ws_ref, bs_ref, y_ref):
        v = _gelu(pre_ref[:, w:])
        vc, rstd = _ln_stats(v)
        vn = (vc * rstd * lng_ref[...] + lnb_ref[...]).astype(BF16)
        keep = _tril(GM_CHUNK)
        for g in range(GM_GROUPS):
            wm = jnp.where(keep, ws_ref[g], 0.0).astype(BF16)
            for ci in range(nch):
                rows = slice(ci * GM_CHUNK, (ci + 1) * GM_CHUNK)
                cols = slice(g * gd, (g + 1) * gd)
                sv = _dot(wm, vn[rows, cols]) + bs_ref[:, g:g + 1]
                u = _gelu(pre_ref[rows, cols])
                y_ref[rows, cols] = (u * sv).astype(BF16)

    return _run(beside, body, "gm_spatial_fwd", _sds((t, w), BF16), (pre, lng, lnb, w_s, b_st), (t // tm,),
                [_tile_spec(tm, w2), _vec_spec(w), _vec_spec(w),
                 pl.BlockSpec((GM_GROUPS, GM_CHUNK, GM_CHUNK), lambda i: (0, 0, 0)),
                 pl.BlockSpec((GM_CHUNK, GM_GROUPS), lambda i: (0, 0))],
                _tile_spec(tm, w), vmem=VMEM_BIG)


def _gm_spatial_bwd(pre, dyv, lng, lnb, w_s, b_st, seq, beside=None):
    t, w2 = pre.shape
    w = w2 // 2
    gd = w // GM_GROUPS
    tm = min(TM, seq)
    nch = tm // GM_CHUNK

    def body(pre_ref, dyv_ref, lng_ref, lnb_ref, ws_ref, bs_ref, dpre_ref, dws_ref, dbs_ref, dlg_ref, dlb_ref, dbin_ref, dvn_ref):
        @pl.when(pl.program_id(0) == 0)
        def _():
            dws_ref[...] = jnp.zeros_like(dws_ref)
            dbs_ref[...] = jnp.zeros_like(dbs_ref)
            dlg_ref[...] = jnp.zeros_like(dlg_ref)
            dlb_ref[...] = jnp.zeros_like(dlb_ref)
            dbin_ref[...] = jnp.zeros_like(dbin_ref)

        pv = pre_ref[:, w:]
        v = _gelu(pv)
        vc, rstd = _ln_stats(v)
        vhat = vc * rstd
        vn = (vhat * lng_ref[...] + lnb_ref[...]).astype(BF16)
        keep = _tril(GM_CHUNK)
        dbs_cols = []
        for g in range(GM_GROUPS):
            wm = jnp.where(keep, ws_ref[g], 0.0).astype(BF16)
            dwm = jnp.zeros((GM_CHUNK, GM_CHUNK), F32)
            dbs = jnp.zeros((GM_CHUNK, 1), F32)
            for ci in range(nch):
                rows = slice(ci * GM_CHUNK, (ci + 1) * GM_CHUNK)
                cols = slice(g * gd, (g + 1) * gd)
                vn_b = vn[rows, cols]
                sv = _dot(wm, vn_b) + bs_ref[:, g:g + 1]
                pu = pre_ref[rows, cols]
                dy = dyv_ref[rows, cols]
                du = dy * sv
                dsv = dy * _gelu(pu)
                dpu = du * _gelu_grad(pu)
                dpre_ref[rows, cols] = dpu.astype(BF16)
                dbin_ref[:, cols] += _colsum(dpu)
                dsv_b = dsv.astype(BF16)
                dwm = dwm + _dot(dsv_b, vn_b, NT)
                dbs = dbs + jnp.sum(dsv, axis=-1, keepdims=True)
                dvn_ref[rows, cols] = _dot(wm, dsv_b, TN)
            dws_ref[g] += jnp.where(keep, dwm, 0.0)
            dbs_cols.append(dbs)
        dbs_ref[...] += jnp.concatenate(dbs_cols, axis=1)
        dvn = dvn_ref[...]
        dlg_ref[...] += _colsum(dvn * vhat)
        dlb_ref[...] += _colsum(dvn)
        dvh = dvn * lng_ref[...]
        m1 = jnp.mean(dvh, axis=-1, keepdims=True)
        m2 = jnp.mean(dvh * vhat, axis=-1, keepdims=True)
        dv = rstd * (dvh - m1 - vhat * m2)
        dpv = dv * _gelu_grad(pv)
        dpre_ref[:, w:] = dpv.astype(BF16)
        dbin_ref[:, w:] += _colsum(dpv)

    full3 = pl.BlockSpec((GM_GROUPS, GM_CHUNK, GM_CHUNK), lambda i: (0, 0, 0))
    bst = pl.BlockSpec((GM_CHUNK, GM_GROUPS), lambda i: (0, 0))
    return _run(beside, body, "gm_spatial_bwd",
                (_sds((t, w2), BF16), _sds((GM_GROUPS, GM_CHUNK, GM_CHUNK), F32), _sds((GM_CHUNK, GM_GROUPS), F32),
                 _sds((1, w), F32), _sds((1, w), F32), _sds((1, w2), F32)),
                (pre, dyv, lng, lnb, w_s, b_st), (t // tm,),
                [_tile_spec(tm, w2), _tile_spec(tm, w), _vec_spec(w), _vec_spec(w), full3, bst],
                (_tile_spec(tm, w2), full3, bst, _vec_spec(w), _vec_spec(w), _vec_spec(w2)),
                scratch=[pltpu.VMEM((tm, w), F32)], vmem=VMEM_BIG)


def _head_masks():
    lane = lax.broadcasted_iota(jnp.int32, (1, LANES), 1)
    return lane < HEAD_DIM


def _two_heads(x, m0):
    z = jnp.zeros_like(x)
    return jnp.where(m0, x, z), jnp.where(m0, z, x)


def _transposed(pair):
    return tuple(x.astype(F32).T.astype(BF16) for x in pair)


def _store_transposed(dqkv_ref, dk_acc, dv_acc, nq, tq):
    for c in range(nq):
        cols = slice(c * tq, (c + 1) * tq)
        dqkv_ref[1, cols, :] = dk_acc[:, cols].T.astype(BF16)
        dqkv_ref[2, cols, :] = dv_acc[:, cols].T.astype(BF16)


def _qkv_specs(seq, nq, blocked_q):
    if blocked_q:
        q = pl.BlockSpec((None, TQ_(seq), LANES), lambda b, p, i: (0, b * nq + i, p))
        k = pl.BlockSpec((None, seq, LANES), lambda b, p, i: (1, b, p))
        v = pl.BlockSpec((None, seq, LANES), lambda b, p, i: (2, b, p))
    else:
        q = pl.BlockSpec((None, seq, LANES), lambda b, p: (0, b, p))
        k = pl.BlockSpec((None, seq, LANES), lambda b, p: (1, b, p))
        v = pl.BlockSpec((None, seq, LANES), lambda b, p: (2, b, p))
    return q, k, v


def TQ_(seq):
    return min(TQ, seq)


def _fox_gate_fwd(ft, b_f, seq):
    nh, t = ft.shape
    nch = seq // LANES

    def body(ft_ref, bf_ref, fr_ref):
        r = lax.broadcasted_iota(jnp.int32, (LANES, LANES), 0)
        c = lax.broadcasted_iota(jnp.int32, (LANES, LANES), 1)
        upper = jnp.where(r <= c, 1.0, 0.0).astype(BF16)
        carry = jnp.zeros((nh, 1), F32)
        for ci in range(nch):
            cols = slice(ci * LANES, (ci + 1) * LANES)
            lf = _log_sigmoid(ft_ref[:, cols] + bf_ref[...])
            cs = _dot_exact(lf, upper) + carry
            fr_ref[:, cols] = cs
            carry = cs[:, LANES - 1:LANES]

    return _call(body, "fox_gate_fwd", _sds((nh, t), F32), grid=(t // seq,),
                 in_specs=[pl.BlockSpec((nh, seq), lambda b: (0, b)), pl.BlockSpec((nh, 1), lambda b: (0, 0))],
                 out_specs=pl.BlockSpec((nh, seq), lambda b: (0, b)))(ft, b_f)


def _fox_gate_bwd(ft, b_f, dfk, dfq, seq):
    nh, t = ft.shape
    nch = seq // LANES

    def body(ft_ref, bf_ref, dfk_ref, dfq_ref, dl_ref, db_ref):
        @pl.when(pl.program_id(0) == 0)
        def _():
            db_ref[...] = jnp.zeros_like(db_ref)

        r = lax.broadcasted_iota(jnp.int32, (LANES, LANES), 0)
        c = lax.broadcasted_iota(jnp.int32, (LANES, LANES), 1)
        lower = jnp.where(r >= c, 1.0, 0.0).astype(BF16)
        carry = jnp.zeros((nh, 1), F32)
        tot = jnp.zeros((nh, 1), F32)
        for ci in reversed(range(nch)):
            cols = slice(ci * LANES, (ci + 1) * LANES)
            rc = _dot_exact(dfk_ref[:, cols] + dfq_ref[:, cols], lower) + carry
            carry = rc[:, 0:1]
            dl = rc * jax.nn.sigmoid(-(ft_ref[:, cols] + bf_ref[...]))
            dl_ref[:, cols] = dl
            tot = tot + jnp.sum(dl, axis=-1, keepdims=True)
        db_ref[...] += tot

    blk = pl.BlockSpec((nh, seq), lambda b: (0, b))
    one = pl.BlockSpec((nh, 1), lambda b: (0, 0))
    return _call(body, "fox_gate_bwd", (_sds((nh, t), F32), _sds((nh, 1), F32)), grid=(t // seq,),
                 in_specs=[blk, one, blk, blk], out_specs=(blk, one))(ft, b_f, dfk, dfq)


def _sweep(step, n_off, unroll, init, start=0):
    def group(_, st):
        base, carry = st[0], st[1:]
        for u in range(unroll):
            carry = step(base + u, carry, False)
        return (base + unroll, *carry)

    def tail(r):
        def run(st):
            base, carry = st[0], st[1:]
            for u in range(r):
                carry = step(base + u, carry, False)
            return step(base + r, carry, True)
        return run

    def pick(idx, fns, st):
        if len(fns) == 1:
            return fns[0](st)
        half = len(fns) // 2
        return lax.cond(idx < half, lambda s: pick(idx, fns[:half], s), lambda s: pick(idx - half, fns[half:], s), st)

    st = lax.fori_loop(0, n_off // unroll, group, (jnp.int32(0) + start, *init))
    return pick(n_off % unroll, [tail(r) for r in range(unroll)], st)


def _carried(n_comm, n_in, n_out, refs):
    own_in, send = refs[:n_in], refs[n_in:n_in + n_comm]
    rest = refs[n_in + n_comm:]
    n_sem = 3 if n_comm else 0
    return own_in, send, rest[:n_out], rest[n_out:n_out + n_comm], rest[n_out + n_comm:n_out + n_comm + n_sem], rest[n_out + n_comm + n_sem:]


def _fox_fwd(qkv, frow, nb, seq, gather=()):
    _, t, d = qkv.shape
    npair = d // LANES
    tq = TQ_(seq)
    nq = seq // tq
    scale = HEAD_DIM ** -0.5
    n_comm = len(gather)
    total = nb * npair * nq

    def body(*refs):
        (q_ref, k_ref, v_ref, fr_ref), send, (o_ref, lse_ref), land, sems, _ = _carried(n_comm, 4, 2, refs)
        at = (pl.program_id(0) * npair + pl.program_id(1)) * nq + pl.program_id(2)
        if n_comm:
            for stage, when in ((0, 0), (1, (3 * total) // 4)):
                @pl.when(at == when)
                def _(stage=stage):
                    _gather_stage(stage, send, land, *sems)

        m0 = _head_masks()
        qm = _two_heads(q_ref[...] * scale, m0)
        row = lax.broadcasted_iota(jnp.int32, (tq, tq), 0)
        col = lax.broadcasted_iota(jnp.int32, (tq, tq), 1)
        one = jnp.ones((tq, LANES), BF16)

        def step(j, carry, diag):
            off = pl.multiple_of(j * tq, tq)
            kb = k_ref[pl.ds(off, tq), :]
            vb = v_ref[pl.ds(off, tq), :]
            vv = (jnp.where(m0, vb, one), jnp.where(m0, one, vb))
            out = []
            for hh in range(2):
                m, acc = carry[2 * hh], carry[2 * hh + 1]
                s = lax.dot_general(qm[hh], kb, NT, preferred_element_type=F32) - fr_ref[hh:hh + 1, pl.ds(off, tq)]
                if diag:
                    s = jnp.where(col <= row, s, NEG_INF)
                mn = jnp.maximum(m, jnp.max(s, axis=-1, keepdims=True))
                p = jnp.exp(s - mn)
                out += [mn, jnp.exp(m - mn) * acc + jnp.dot(p.astype(BF16), vv[hh], preferred_element_type=F32)]
            return tuple(out)

        neg = jnp.full((tq, 1), NEG_INF, F32)
        zacc = jnp.zeros((tq, LANES), F32)
        m_a, acc_a, m_b, acc_b = _sweep(step, pl.program_id(2), FOX_FWD_UNROLL, (neg, zacc, neg, zacc))
        l_a = pltpu.roll(acc_a, HEAD_DIM, 1)
        l_b = pltpu.roll(acc_b, HEAD_DIM, 1)
        o_ref[...] = jnp.where(m0, acc_a / l_a, acc_b / l_b)
        lse_ref[:, 0:1] = m_a + jnp.log(l_a[:, 0:1])
        lse_ref[:, 1:2] = m_b + jnp.log(l_b[:, HEAD_DIM:HEAD_DIM + 1])
        if n_comm:
            @pl.when(at == total - 1)
            def _():
                _gather_stage(2, send, land, *sems)

    q_spec, k_spec, v_spec = _qkv_specs(seq, nq, True)
    col_spec = pl.BlockSpec((None, tq, 2), lambda b, p, i: (p, b * nq + i, 0))
    hbm = pl.BlockSpec(memory_space=pl.ANY)
    return _call(body, "fox_fwd", (_sds((t, d), F32), _sds((npair, t, 2), F32), *_gathered_shapes(gather)), grid=(nb, npair, nq),
                 in_specs=[q_spec, k_spec, v_spec, pl.BlockSpec((None, 2, seq), lambda b, p, i: (p, 0, b))] + [hbm] * n_comm,
                 out_specs=(pl.BlockSpec((tq, LANES), lambda b, p, i: (b * nq + i, p)), col_spec, *[hbm] * n_comm),
                 scratch=_comm_scratch(n_comm) if n_comm else (), vmem=VMEM_BIG)(qkv, qkv, qkv, frow, *gather)


def _scatter_beside(stage, n_comm, nb, npair, send, land, sems):
    if n_comm:
        at = pl.program_id(0) * npair + pl.program_id(1)

        @pl.when(at == (0 if stage == 0 else nb * npair - 1))
        def _():
            _scatter_stage(stage, send, land, *sems)


def _fox_bwd(qkv, frow, o, do, lse, nb, seq, scatter=()):
    _, t, d = qkv.shape
    npair = d // LANES
    tq = TQ_(seq)
    nq = seq // tq
    scale = HEAD_DIM ** -0.5
    n_comm = len(scatter)

    def body(*refs):
        own_in, send, (dqkv_ref, df_ref, dfq_ref), land, sems, (dk_acc, dv_acc) = _carried(n_comm, 7, 3, refs)
        q_ref, k_ref, v_ref, fr_ref, o_ref, do_ref, lse_ref = own_in
        _scatter_beside(0, n_comm, nb, npair, send, land, sems)
        m0 = _head_masks()
        row = lax.broadcasted_iota(jnp.int32, (tq, tq), 0)
        col = lax.broadcasted_iota(jnp.int32, (tq, tq), 1)
        dk_acc[...] = jnp.zeros_like(dk_acc)
        dv_acc[...] = jnp.zeros_like(dv_acc)
        df_ref[...] = jnp.zeros_like(df_ref)

        def q_block(i, _):
            qoff = pl.multiple_of(i * tq, tq)
            qrows = pl.ds(qoff, tq)
            qm = _two_heads(q_ref[qrows, :] * scale, m0)
            dov = do_ref[qrows, :]
            dd = dov * o_ref[qrows, :]
            dm = _two_heads(dov.astype(BF16), m0)
            qmt = _transposed(qm)
            dmt = _transposed(dm)
            delta = (jnp.sum(jnp.where(m0, dd, 0.0), axis=-1, keepdims=True),
                     jnp.sum(jnp.where(m0, 0.0, dd), axis=-1, keepdims=True))
            ls = (lse_ref[qrows, 0:1], lse_ref[qrows, 1:2])

            def step(j, carry, diag):
                off = pl.multiple_of(j * tq, tq)
                krows = pl.ds(off, tq)
                kb = k_ref[krows, :]
                vb = v_ref[krows, :]
                dqs, rowsums = [], []
                dk = jnp.zeros((LANES, tq), F32)
                dv = jnp.zeros((LANES, tq), F32)
                for hh in range(2):
                    s = lax.dot_general(qm[hh], kb, NT, preferred_element_type=F32) - fr_ref[hh:hh + 1, krows]
                    if diag:
                        s = jnp.where(col <= row, s, NEG_INF)
                    p = jnp.exp(s - ls[hh])
                    dp = lax.dot_general(dm[hh], vb, NT, preferred_element_type=F32)
                    ds = p * (dp - delta[hh])
                    df_ref[hh:hh + 1, krows] -= _colsum(ds)
                    rowsums.append(carry[1 + hh] + jnp.sum(ds, axis=-1, keepdims=True))
                    ds_b = ds.astype(BF16)
                    dqs.append(jnp.dot(ds_b, kb, preferred_element_type=F32))
                    dk = dk + jnp.dot(qmt[hh], ds_b, preferred_element_type=F32)
                    dv = dv + jnp.dot(dmt[hh], p.astype(BF16), preferred_element_type=F32)
                dk_acc[:, krows] += dk
                dv_acc[:, krows] += dv
                return (carry[0] + jnp.where(m0, dqs[0], dqs[1]), *rowsums)

            zero = jnp.zeros((tq, 1), F32)
            dq, rs_a, rs_b = _sweep(step, i, FOX_BWD_UNROLL, (jnp.zeros((tq, LANES), F32), zero, zero))
            dqkv_ref[0, qrows, :] = (dq * scale).astype(BF16)
            dfq_ref[qrows, 0:1] = rs_a
            dfq_ref[qrows, 1:2] = rs_b
            return 0

        lax.fori_loop(0, nq, q_block, 0)
        _store_transposed(dqkv_ref, dk_acc, dv_acc, nq, tq)
        _scatter_beside(1, n_comm, nb, npair, send, land, sems)

    q_spec, k_spec, v_spec = _qkv_specs(seq, nq, False)
    col_spec = pl.BlockSpec((None, seq, 2), lambda b, p: (p, b, 0))
    row_spec = pl.BlockSpec((None, 2, seq), lambda b, p: (p, 0, b))
    tile = pl.BlockSpec((seq, LANES), lambda b, p: (b, p))
    hbm = pl.BlockSpec(memory_space=pl.ANY)
    return _call(body, "fox_bwd", (_sds((3, t, d), BF16), _sds((npair, 2, t), F32), _sds((npair, t, 2), F32), *_scattered_shapes(scatter)),
                 grid=(nb, npair),
                 in_specs=[q_spec, k_spec, v_spec, row_spec, tile, tile, col_spec] + [hbm] * n_comm,
                 out_specs=(pl.BlockSpec((3, seq, LANES), lambda b, p: (0, b, p)), row_spec, col_spec, *[hbm] * n_comm),
                 scratch=(_comm_scratch(n_comm) if n_comm else []) + [pltpu.VMEM((LANES, seq), F32), pltpu.VMEM((LANES, seq), F32)],
                 vmem=VMEM_BIG)(qkv, qkv, qkv, frow, o, do, lse, *scatter)


def _split2(x):
    hi = x.astype(BF16)
    return hi, (x - hi.astype(F32)).astype(BF16)


def _sum_right(x, tri):
    hi, lo = _split2(x)
    return jnp.dot(hi, tri, preferred_element_type=F32) + jnp.dot(lo, tri, preferred_element_type=F32)


def _sb_scores(qm_h, kb, mask):
    z = lax.dot_general(qm_h, kb, NT, preferred_element_type=F32)
    lb = _log_sigmoid(z)
    l1m = lb - z
    if mask is not None:
        l1m = jnp.where(mask, l1m, 0.0)
    return lb, l1m


def _sb_fwd(qkv, nb, seq):
    _, t, d = qkv.shape
    npair = d // LANES
    tq = TQ_(seq)
    nq = seq // tq
    scale = HEAD_DIM ** -0.5

    def body(q_ref, k_ref, v_ref, o_ref, lt_ref):
        i = pl.program_id(2)
        m0 = _head_masks()
        qm = _two_heads(q_ref[...] * scale, m0)
        row = lax.broadcasted_iota(jnp.int32, (tq, tq), 0)
        col = lax.broadcasted_iota(jnp.int32, (tq, tq), 1)
        after = jnp.where(row > col, 1.0, 0.0).astype(BF16)

        def step(j, carry, diag):
            off = pl.multiple_of(j * tq, tq)
            kb = k_ref[pl.ds(off, tq), :]
            vb = v_ref[pl.ds(off, tq), :]
            mask = (col < row) if diag else None
            nxt, parts = [], []
            for hh in range(2):
                lb, l1m = _sb_scores(qm[hh], kb, mask)
                rest = _sum_right(l1m, after) + carry[hh]
                a = jnp.exp(lb + rest)
                if diag:
                    a = jnp.where(mask, a, 0.0)
                parts.append(jnp.dot(a.astype(BF16), vb, preferred_element_type=F32))
                nxt.append(carry[hh] + jnp.sum(l1m, axis=-1, keepdims=True))
            return (*nxt, carry[2] + jnp.where(m0, parts[0], parts[1]))

        zero = jnp.zeros((tq, 1), F32)
        init = (zero, zero, jnp.zeros((tq, LANES), F32))
        carry = lax.cond(i > 0, lambda c: step(i - 1, step(i, c, True), False), lambda c: step(i, c, True), init)

        def alive(st):
            return (st[0] < i) & (jnp.max(jnp.maximum(st[1], st[2])) > SB_DEAD)

        def more(st):
            return (st[0] + 1, *step(i - 1 - st[0], st[1:], False))

        done, lt_a, lt_b, acc = lax.while_loop(alive, more, (jnp.minimum(i, 1), *carry))
        o_ref[...] = acc
        lt_ref[:, 0:1] = lt_a
        lt_ref[:, 1:2] = lt_b
        lt_ref[:, 2:3] = jnp.zeros((tq, 1), F32) + done.astype(F32)
        lt_ref[:, 3:4] = zero

    q_spec, k_spec, v_spec = _qkv_specs(seq, nq, True)
    return _call(body, "sb_fwd", (_sds((t, d), F32), _sds((npair, t, 4), F32)), grid=(nb, npair, nq),
                 in_specs=[q_spec, k_spec, v_spec],
                 out_specs=(pl.BlockSpec((tq, LANES), lambda b, p, i: (b * nq + i, p)),
                            pl.BlockSpec((None, tq, 4), lambda b, p, i: (p, b * nq + i, 0))), vmem=VMEM_BIG)(qkv, qkv, qkv)


def _sb_bwd(qkv, do, ltot, nb, seq, scatter=()):
    _, t, d = qkv.shape
    npair = d // LANES
    tq = TQ_(seq)
    nq = seq // tq
    scale = HEAD_DIM ** -0.5
    n_comm = len(scatter)

    def body(*refs):
        (q_ref, k_ref, v_ref, do_ref, lt_ref), send, (dqkv_ref,), land, sems, (dk_acc, dv_acc) = _carried(n_comm, 5, 1, refs)
        _scatter_beside(0, n_comm, nb, npair, send, land, sems)
        m0 = _head_masks()
        row = lax.broadcasted_iota(jnp.int32, (tq, tq), 0)
        col = lax.broadcasted_iota(jnp.int32, (tq, tq), 1)
        upto = jnp.where(row <= col, 1.0, 0.0).astype(BF16)
        left_of = jnp.where(row < col, 1.0, 0.0).astype(BF16)
        dk_acc[...] = jnp.zeros_like(dk_acc)
        dv_acc[...] = jnp.zeros_like(dv_acc)

        def q_block(i, _):
            qoff = pl.multiple_of(i * tq, tq)
            qrows = pl.ds(qoff, tq)
            qm = _two_heads(q_ref[qrows, :] * scale, m0)
            dm = _two_heads(do_ref[qrows, :].astype(BF16), m0)
            qmt = _transposed(qm)
            dmt = _transposed(dm)
            ltot = (lt_ref[qrows, 0:1], lt_ref[qrows, 1:2])

            def step(j, carry, diag):
                off = pl.multiple_of(j * tq, tq)
                krows = pl.ds(off, tq)
                kb = k_ref[krows, :]
                vb = v_ref[krows, :]
                mask = (col < row) if diag else None
                nxt, dqs = [], []
                dk = jnp.zeros((LANES, tq), F32)
                dv = jnp.zeros((LANES, tq), F32)
                for hh in range(2):
                    cl, ce = carry[2 * hh], carry[2 * hh + 1]
                    lb, l1m = _sb_scores(qm[hh], kb, mask)
                    a = jnp.exp(lb + (ltot[hh] - (_sum_right(l1m, upto) + cl)))
                    if diag:
                        a = jnp.where(mask, a, 0.0)
                    e = lax.dot_general(dm[hh], vb, NT, preferred_element_type=F32) * a
                    before = _sum_right(e, left_of) + ce
                    beta = jnp.exp(lb)
                    dz = e * (1.0 - beta) - before * beta
                    if diag:
                        dz = jnp.where(mask, dz, 0.0)
                    dz_b = dz.astype(BF16)
                    dqs.append(jnp.dot(dz_b, kb, preferred_element_type=F32))
                    dk = dk + jnp.dot(qmt[hh], dz_b, preferred_element_type=F32)
                    dv = dv + jnp.dot(dmt[hh], a.astype(BF16), preferred_element_type=F32)
                    nxt += [cl + jnp.sum(l1m, axis=-1, keepdims=True), ce + jnp.sum(e, axis=-1, keepdims=True)]
                dk_acc[:, krows] += dk
                dv_acc[:, krows] += dv
                return (*nxt, carry[4] + jnp.where(m0, dqs[0], dqs[1]))

            zero = jnp.zeros((tq, 1), F32)
            visited = jnp.max(lt_ref[qrows, 2:3]).astype(jnp.int32)
            carry = _sweep(step, visited, SB_BWD_UNROLL, (zero, zero, zero, zero, jnp.zeros((tq, LANES), F32)), start=i - visited)
            dqkv_ref[0, qrows, :] = (carry[4] * scale).astype(BF16)
            return 0

        lax.fori_loop(0, nq, q_block, 0)
        _store_transposed(dqkv_ref, dk_acc, dv_acc, nq, tq)
        _scatter_beside(1, n_comm, nb, npair, send, land, sems)

    q_spec, k_spec, v_spec = _qkv_specs(seq, nq, False)
    tile = pl.BlockSpec((seq, LANES), lambda b, p: (b, p))
    hbm = pl.BlockSpec(memory_space=pl.ANY)
    return _call(body, "sb_bwd", (_sds((3, t, d), BF16), *_scattered_shapes(scatter)), grid=(nb, npair),
                 in_specs=[q_spec, k_spec, v_spec, tile, pl.BlockSpec((None, seq, 4), lambda b, p: (p, b, 0))] + [hbm] * n_comm,
                 out_specs=(pl.BlockSpec((3, seq, LANES), lambda b, p: (0, b, p)), *[hbm] * n_comm),
                 scratch=(_comm_scratch(n_comm) if n_comm else []) + [pltpu.VMEM((LANES, seq), F32), pltpu.VMEM((LANES, seq), F32)],
                 vmem=VMEM_BIG)(qkv, qkv, qkv, do, ltot, *scatter)


def _qkv_proj(name, h, w):
    t, d = h.shape
    tm = min(TMM, t)
    return _mm(name, h, w, _sds((3, t, d), BF16), (3, t // tm, 1),
               pl.BlockSpec((tm, d), lambda s, i, r: (i, 0)), pl.BlockSpec((d, d), lambda s, i, r: (0, s)),
               pl.BlockSpec((None, tm, d), lambda s, i, r: (s, i, 0)), NN)


def _qkv_dw(name, h, dqkv):
    t, d = h.shape
    tk = min(TMM, t)

    def terms(h_ref, g_ref):
        hv = h_ref[...]
        return [_dot(hv, g_ref[s], TN) for s in range(3)]

    def store(o_ref, acc_ref):
        for s in range(3):
            o_ref[:, s * d:(s + 1) * d] = acc_ref[s].astype(o_ref.dtype)

    return _accumulate_over_tokens(name, (h, dqkv), [pl.BlockSpec((tk, d), lambda r: (r, 0)), pl.BlockSpec((3, tk, d), lambda r: (0, r, 0))],
                                   _sds((d, 3 * d), GRAD_WIRE), pl.BlockSpec((d, 3 * d), lambda r: (0, 0)), (3, d, d), t // tk, terms, store)


def _qkv_dh(name, dqkv, w):
    _, t, d = dqkv.shape
    tm = min(TM, t)

    def body(g_ref, w_ref, o_ref):
        acc = None
        for s in range(3):
            term = _dot(g_ref[s], w_ref[:, s * d:(s + 1) * d], NT)
            acc = term if acc is None else acc + term
        o_ref[...] = acc

    return _call(body, name, _sds((t, d), F32), grid=(t // tm,),
                 in_specs=[pl.BlockSpec((3, tm, d), lambda i: (0, i, 0)), pl.BlockSpec((d, 3 * d), lambda i: (0, 0))],
                 out_specs=pl.BlockSpec((tm, d), lambda i: (i, 0)), vmem=VMEM_BIG)(dqkv, w)


def _glu(pre_block, d):
    return pre_block[:, :d] * jax.nn.sigmoid(pre_block[:, d:])


def _shifted_copies(ext_ref, sh_ref, tt):
    for r in range(1, SUBLANES):
        sh_ref[r - 1] = ext_ref[pl.ds(r, tt + CONV_HALO - SUBLANES), :]


def _rows_from(ext_ref, sh_ref, base, offset, n):
    q, r = divmod(offset, SUBLANES)
    if r == 0:
        return ext_ref[pl.ds(pl.multiple_of(base + offset, SUBLANES), n), :]
    return sh_ref[r - 1, pl.ds(pl.multiple_of(base + q * SUBLANES, SUBLANES), n), :]


def _cv_conv_fwd(pre, dw, dwb, lng, lnb, seq):
    t, d2 = pre.shape
    d = d2 // 2
    tt = min(TT, seq)
    nt = seq // tt
    hb = tt // CONV_HALO

    def body(pre_ref, halo_ref, dw_ref, dwb_ref, lng_ref, lnb_ref, y1_ref, y2_ref, ext_ref, sh_ref):
        i = pl.program_id(1)
        ext_ref[0:CONV_HALO, :] = jnp.where(i == 0, 0.0, _glu(halo_ref[...], d))
        ext_ref[CONV_HALO:, :] = _glu(pre_ref[...], d)
        _shifted_copies(ext_ref, sh_ref, tt)

        acc = jnp.zeros((tt, d), F32) + dwb_ref[...]
        for k in range(CONV_WIDTH):
            acc = acc + _rows_from(ext_ref, sh_ref, 0, CONV_HALO - (CONV_WIDTH - 1) + k, tt) * dw_ref[k:k + 1, :]
        y1_ref[...] = acc
        yc, rstd = _ln_stats(acc)
        y2_ref[...] = _silu(yc * rstd * lng_ref[...] + lnb_ref[...]).astype(BF16)

    vec = pl.BlockSpec((1, d), lambda b, i: (0, 0))
    tile = pl.BlockSpec((tt, d), lambda b, i: (b * nt + i, 0))
    return _call(body, "cv_conv_fwd", (_sds((t, d), F32), _sds((t, d), BF16)), grid=(t // seq, nt),
                 in_specs=[pl.BlockSpec((tt, d2), lambda b, i: (b * nt + i, 0)),
                           pl.BlockSpec((CONV_HALO, d2), lambda b, i: (jnp.maximum((b * nt + i) * hb - 1, 0), 0)),
                           pl.BlockSpec((CONV_HALO, d), lambda b, i: (0, 0)), vec, vec, vec],
                 out_specs=(tile, tile),
                 scratch=[pltpu.VMEM((tt + CONV_HALO, d), F32), pltpu.VMEM((SUBLANES - 1, tt + CONV_HALO - SUBLANES, d), F32)],
                 vmem=VMEM_BIG)(pre, pre, dw, dwb, lng, lnb)


def _cv_norm_bwd(y1, dy2, lng, lnb):
    t, d = y1.shape
    tm = min(TM, t)

    def body(y1_ref, dy2_ref, lng_ref, lnb_ref, dy1_ref, dlg_ref, dlb_ref, dsum_ref):
        @pl.when(pl.program_id(0) == 0)
        def _():
            dlg_ref[...] = jnp.zeros_like(dlg_ref)
            dlb_ref[...] = jnp.zeros_like(dlb_ref)
            dsum_ref[...] = jnp.zeros_like(dsum_ref)

        yc, rstd = _ln_stats(y1_ref[...])
        yhat = yc * rstd
        n = yhat * lng_ref[...] + lnb_ref[...]
        sg = jax.nn.sigmoid(n)
        dn = dy2_ref[...] * (sg * (1.0 + n * (1.0 - sg)))
        dlg_ref[...] += _colsum(dn * yhat)
        dlb_ref[...] += _colsum(dn)
        dyh = dn * lng_ref[...]
        m1 = jnp.mean(dyh, axis=-1, keepdims=True)
        m2 = jnp.mean(dyh * yhat, axis=-1, keepdims=True)
        dy1 = rstd * (dyh - m1 - yhat * m2)
        dy1_ref[...] = dy1
        dsum_ref[...] += _colsum(dy1)

    return _call(body, "cv_norm_bwd", (_sds((t, d), F32), _sds((1, d), F32), _sds((1, d), F32), _sds((1, d), F32)),
                 grid=(t // tm,), in_specs=[_tile_spec(tm, d), _tile_spec(tm, d), _vec_spec(d), _vec_spec(d)],
                 out_specs=(_tile_spec(tm, d), _vec_spec(d), _vec_spec(d), _vec_spec(d)))(y1, dy2, lng, lnb)


def _cv_conv_bwd(pre, dy1, dw, seq):
    t, d2 = pre.shape
    d = d2 // 2
    tt = min(TT // 2, seq)
    nt = seq // tt
    hb = tt // CONV_HALO
    last_halo = t // CONV_HALO - 1

    def body(pre_ref, halo_ref, dy_ref, dyn_ref, dw_ref, dpre_ref, ddw_ref, dbin_ref, ext_ref, dext_ref, sh_ref, dsh_ref, taps_ref):
        b, i = pl.program_id(0), pl.program_id(1)

        @pl.when((b == 0) & (i == 0))
        def _():
            ddw_ref[...] = jnp.zeros_like(ddw_ref)
            dbin_ref[...] = jnp.zeros_like(dbin_ref)

        pv = pre_ref[...]
        ext_ref[0:CONV_HALO, :] = jnp.where(i == 0, 0.0, _glu(halo_ref[...], d))
        ext_ref[CONV_HALO:, :] = _glu(pv, d)
        dyv = dy_ref[...]
        dext_ref[0:tt, :] = dyv
        dext_ref[tt:, :] = jnp.where(i == nt - 1, 0.0, dyn_ref[...])
        _shifted_copies(ext_ref, sh_ref, tt)
        _shifted_copies(dext_ref, dsh_ref, tt)
        nrows = tt // CONV_ROWS

        def input_grad(c, _):
            r0 = pl.multiple_of(c * CONV_ROWS, CONV_ROWS)
            dy0 = jnp.zeros((CONV_ROWS, d), F32)
            for k in range(CONV_WIDTH):
                dy0 = dy0 + _rows_from(dext_ref, dsh_ref, r0, CONV_WIDTH - 1 - k, CONV_ROWS) * dw_ref[k:k + 1, :]
            rows = pl.ds(r0, CONV_ROWS)
            a = pre_ref[rows, :d]
            sg = jax.nn.sigmoid(pre_ref[rows, d:])
            da = dy0 * sg
            dg = dy0 * a * sg * (1.0 - sg)
            dpre_ref[rows, :d] = da.astype(BF16)
            dpre_ref[rows, d:] = dg.astype(BF16)
            dbin_ref[:, :d] += _colsum(da)
            dbin_ref[:, d:] += _colsum(dg)
            return 0

        lax.fori_loop(0, nrows, input_grad, 0)

        @pl.when((b == 0) & (i == 0))
        def _():
            taps_ref[...] = jnp.zeros_like(taps_ref)

        def tap_grads(c, _):
            r0 = pl.multiple_of(c * CONV_ROWS, CONV_ROWS)
            dyc = dy_ref[pl.ds(r0, CONV_ROWS), :]
            for k in range(CONV_WIDTH):
                prod = dyc * _rows_from(ext_ref, sh_ref, r0, CONV_HALO - (CONV_WIDTH - 1) + k, CONV_ROWS)
                taps_ref[k] += jnp.sum(prod.reshape(CONV_ROWS // SUBLANES, SUBLANES, d), axis=0)
            return 0

        lax.fori_loop(0, nrows, tap_grads, 0)

        @pl.when((b == t // seq - 1) & (i == nt - 1))
        def _():
            for k in range(CONV_WIDTH):
                ddw_ref[k:k + 1, :] = _colsum(taps_ref[k])

    return _call(body, "cv_conv_bwd", (_sds((t, d2), BF16), _sds((CONV_HALO, d), F32), _sds((1, d2), F32)), grid=(t // seq, nt),
                 in_specs=[pl.BlockSpec((tt, d2), lambda b, i: (b * nt + i, 0)),
                           pl.BlockSpec((CONV_HALO, d2), lambda b, i: (jnp.maximum((b * nt + i) * hb - 1, 0), 0)),
                           pl.BlockSpec((tt, d), lambda b, i: (b * nt + i, 0)),
                           pl.BlockSpec((CONV_HALO, d), lambda b, i: (jnp.minimum((b * nt + i + 1) * hb, last_halo), 0)),
                           pl.BlockSpec((CONV_HALO, d), lambda b, i: (0, 0))],
                 out_specs=(pl.BlockSpec((tt, d2), lambda b, i: (b * nt + i, 0)),
                            pl.BlockSpec((CONV_HALO, d), lambda b, i: (0, 0)), pl.BlockSpec((1, d2), lambda b, i: (0, 0))),
                 scratch=[pltpu.VMEM((tt + CONV_HALO, d), F32), pltpu.VMEM((tt + CONV_HALO, d), F32),
                          pltpu.VMEM((SUBLANES - 1, tt + CONV_HALO - SUBLANES, d), F32),
                          pltpu.VMEM((SUBLANES - 1, tt + CONV_HALO - SUBLANES, d), F32),
                          pltpu.VMEM((CONV_HALO, SUBLANES, d), F32)],
                 vmem=VMEM_BIG)(pre, pre, dy1, dy1, dw)


def _adamw(name, w, m, v, g=None, parts=None):
    rows, cols = w.shape
    tr = rows
    for cand in ((512,) if parts is None else ()) + (256, 128, 64, 32, 16, 8):
        if rows % cand == 0 and rows > cand:
            tr = cand
            break
    bc1 = 1.0 - ADAM_B1 ** ADAM_STEP
    bc2 = 1.0 - ADAM_B2 ** ADAM_STEP

    def body(w_ref, m_ref, v_ref, g_ref, go_ref, d_ref, mo_ref, vo_ref):
        if parts is None:
            gv = g_ref[...]
        else:
            gv = g_ref[0].astype(F32)
            for s in range(1, N_DEV):
                gv = gv + g_ref[s].astype(F32)
        mn = ADAM_B1 * m_ref[...] + (1.0 - ADAM_B1) * gv
        vn = ADAM_B2 * v_ref[...] + (1.0 - ADAM_B2) * (gv * gv)
        m_hat = mn / bc1
        v_hat = vn / bc2
        go_ref[...] = gv
        d_ref[...] = -ADAM_LR * (m_hat / (jnp.sqrt(v_hat) + ADAM_EPS) + ADAM_WD * w_ref[...])
        mo_ref[...] = mn
        vo_ref[...] = vn

    blk = pl.BlockSpec((tr, cols), lambda i: (i, 0))
    g_in, g_spec = (g, blk) if parts is None else (parts, pl.BlockSpec((N_DEV, tr, cols), lambda i: (0, i, 0)))
    out = _sds((rows, cols), F32)
    return _call(body, name, (out, out, out, out), grid=(rows // tr,), in_specs=[blk, blk, blk, g_spec],
                 out_specs=(blk, blk, blk, blk), vmem=VMEM_BIG)(w, m, v, g_in)


def _pad_rows(a, rows):
    return jnp.pad(a, ((0, rows - a.shape[0]), (0, 0)))


def _full_cols(gathered, n):
    k = gathered.shape[2]
    return jnp.transpose(gathered[0], (1, 0, 2)).reshape(k, N_DEV * n)


def _col_blocks(full, n):
    k = full.shape[0]
    return jnp.transpose(full.reshape(k, N_DEV, n), (1, 0, 2))[None]


def kernel(x, c, mod_w, mod_b, ln1_g, ln1_b, ln2_g, ln2_b, ffn_w_in, ffn_w_out, gm_w_in, gm_b_in, gm_ln_g, gm_ln_b, gm_w_s, gm_b_s, gm_w_out, fox_w_in, fox_b_f, fox_w_out, sb_w_in, sb_w_out, cv_w_in, cv_b_in, cv_dw, cv_dw_b, cv_ln_g, cv_ln_b, cv_w_out, cv_b_out, loss_target, m_mod_w, m_mod_b, m_ln1_g, m_ln1_b, m_ln2_g, m_ln2_b, m_ffn_w_in, m_ffn_w_out, m_gm_w_in, m_gm_b_in, m_gm_ln_g, m_gm_ln_b, m_gm_w_s, m_gm_b_s, m_gm_w_out, m_fox_w_in, m_fox_b_f, m_fox_w_out, m_sb_w_in, m_sb_w_out, m_cv_w_in, m_cv_b_in, m_cv_dw, m_cv_dw_b, m_cv_ln_g, m_cv_ln_b, m_cv_w_out, m_cv_b_out, v_mod_w, v_mod_b, v_ln1_g, v_ln1_b, v_ln2_g, v_ln2_b, v_ffn_w_in, v_ffn_w_out, v_gm_w_in, v_gm_b_in, v_gm_ln_g, v_gm_ln_b, v_gm_w_s, v_gm_b_s, v_gm_w_out, v_fox_w_in, v_fox_b_f, v_fox_w_out, v_sb_w_in, v_sb_w_out, v_cv_w_in, v_cv_b_in, v_cv_dw, v_cv_dw_b, v_cv_ln_g, v_cv_ln_b, v_cv_w_out, v_cv_b_out):
    weights = dict(mod_w=mod_w, mod_b=mod_b, ln1_g=ln1_g, ln1_b=ln1_b, ln2_g=ln2_g, ln2_b=ln2_b, ffn_w_in=ffn_w_in, ffn_w_out=ffn_w_out, gm_w_in=gm_w_in, gm_b_in=gm_b_in, gm_ln_g=gm_ln_g, gm_ln_b=gm_ln_b, gm_w_s=gm_w_s, gm_b_s=gm_b_s, gm_w_out=gm_w_out, fox_w_in=fox_w_in, fox_b_f=fox_b_f, fox_w_out=fox_w_out, sb_w_in=sb_w_in, sb_w_out=sb_w_out, cv_w_in=cv_w_in, cv_b_in=cv_b_in, cv_dw=cv_dw, cv_dw_b=cv_dw_b, cv_ln_g=cv_ln_g, cv_ln_b=cv_ln_b, cv_w_out=cv_w_out, cv_b_out=cv_b_out)
    mom1 = dict(mod_w=m_mod_w, mod_b=m_mod_b, ln1_g=m_ln1_g, ln1_b=m_ln1_b, ln2_g=m_ln2_g, ln2_b=m_ln2_b, ffn_w_in=m_ffn_w_in, ffn_w_out=m_ffn_w_out, gm_w_in=m_gm_w_in, gm_b_in=m_gm_b_in, gm_ln_g=m_gm_ln_g, gm_ln_b=m_gm_ln_b, gm_w_s=m_gm_w_s, gm_b_s=m_gm_b_s, gm_w_out=m_gm_w_out, fox_w_in=m_fox_w_in, fox_b_f=m_fox_b_f, fox_w_out=m_fox_w_out, sb_w_in=m_sb_w_in, sb_w_out=m_sb_w_out, cv_w_in=m_cv_w_in, cv_b_in=m_cv_b_in, cv_dw=m_cv_dw, cv_dw_b=m_cv_dw_b, cv_ln_g=m_cv_ln_g, cv_ln_b=m_cv_ln_b, cv_w_out=m_cv_w_out, cv_b_out=m_cv_b_out)
    mom2 = dict(mod_w=v_mod_w, mod_b=v_mod_b, ln1_g=v_ln1_g, ln1_b=v_ln1_b, ln2_g=v_ln2_g, ln2_b=v_ln2_b, ffn_w_in=v_ffn_w_in, ffn_w_out=v_ffn_w_out, gm_w_in=v_gm_w_in, gm_b_in=v_gm_b_in, gm_ln_g=v_gm_ln_g, gm_ln_b=v_gm_ln_b, gm_w_s=v_gm_w_s, gm_b_s=v_gm_b_s, gm_w_out=v_gm_w_out, fox_w_in=v_fox_w_in, fox_b_f=v_fox_b_f, fox_w_out=v_fox_w_out, sb_w_in=v_sb_w_in, sb_w_out=v_sb_w_out, cv_w_in=v_cv_w_in, cv_b_in=v_cv_b_in, cv_dw=v_cv_dw, cv_dw_b=v_cv_dw_b, cv_ln_g=v_cv_ln_g, cv_ln_b=v_cv_ln_b, cv_w_out=v_cv_w_out, cv_b_out=v_cv_b_out)
    names = list(weights)

    nb, seq, d = x.shape
    t = nb * seq
    nl = mod_w.shape[0]
    alpha = (2.0 * nl) ** 0.25
    me = 4 * lax.axis_index("x") + 2 * lax.axis_index("y") + lax.axis_index("c")
    xs = x.reshape(t, d)
    tgt = loss_target.reshape(t, d)
    n_mod = mod_w.shape[2]
    n_ffn = ffn_w_in.shape[2]
    n_heads = d // HEAD_DIM
    npair = d // LANES

    cvp = d // N_DEV
    cv_small = jnp.concatenate([_pad_rows(cv_dw[0], CONV_HALO), cv_dw_b, cv_ln_g, cv_ln_b, cv_b_out,
                                cv_b_in.reshape(2, cvp), jnp.zeros((2, cvp), F32)], axis=0)
    cv_packed = cv_small.reshape(-1, d)
    first = _exchange_small(jnp.concatenate([_pad_rows(c, 8), _pad_rows(cv_packed, 8)], axis=0), "gather_c", False)
    c_all = first[:, :nb].reshape(N_DEV * nb, d)
    cv_all = first[:, 8:8 + cv_packed.shape[0]].reshape(N_DEV, cv_small.shape[0], cvp)
    mod_b_loc = lax.dynamic_slice_in_dim(mod_b, me * n_mod, n_mod, axis=1)[:, None, :]
    mod_loc = _mod_fwd(c_all, mod_w, mod_b_loc)
    mod_g = _exchange_small(mod_loc.reshape(nl * N_DEV * nb, n_mod), "gather_mod", False)
    mod_all = jnp.transpose(mod_g.reshape(N_DEV, nl, N_DEV * nb, n_mod), (1, 2, 0, 3)).reshape(nl, N_DEV * nb, N_DEV * n_mod)
    mod_me = lax.dynamic_slice_in_dim(mod_all, me * nb, nb, axis=1)
    mods = [[mod_me[l, :, k * d:(k + 1) * d][:, None, :] for k in range(6)] for l in range(nl)]

    assert nl == 4, "the exchange schedule below is written for the four-layer trunk"
    big = ["ffn_w_in", "ffn_w_out", "gm_w_in", "gm_w_out", "fox_w_in", "fox_w_out", "sb_w_in", "sb_w_out", "cv_w_in", "cv_w_out"]
    shard = {n: weights[n].astype(BF16) for n in big if not n.startswith("ffn")}
    for l in range(nl):
        shard["ffn_w_in", l] = ffn_w_in[l:l + 1].astype(BF16)
        shard["ffn_w_out", l] = ffn_w_out[l:l + 1].astype(BF16)
    now = ["gm_w_in", "gm_w_out"]
    with_gm_in = [("ffn_w_in", 0)]
    with_gm_gate = [("ffn_w_out", 0)]
    with_ffn_0 = ["fox_w_in", "fox_w_out"]
    later = [("ffn_w_in", 1), ("ffn_w_out", 1), "sb_w_in", "sb_w_out", ("ffn_w_in", 2), ("ffn_w_out", 2),
             "cv_w_in", "cv_w_out", ("ffn_w_in", 3), ("ffn_w_out", 3)]
    gathered = dict(zip(now, _gather_weights([shard[n] for n in now])))
    w_ffn_out_rows = lambda l: gathered["ffn_w_out", l].reshape(N_DEV // 2, n_ffn, d)
    sq = lambda n: gathered[n].reshape(d, d)
    cv_rows = jnp.transpose(cv_all, (1, 0, 2)).reshape(cv_small.shape[0], d)
    cv_dw_f, cv_dwb_f, cv_lng_f, cv_lnb_f, cv_bout_f = (cv_rows[:CONV_HALO], cv_rows[32:33], cv_rows[33:34], cv_rows[34:35], cv_rows[35:36])
    cv_bin_f = cv_all[:, 36:38, :].reshape(1, 2 * d)

    saved = []
    h = _modulate(xs, mods[0][1], mods[0][0], seq)
    xin = xs
    for l in range(nl):
        kind = l % 4
        sv = dict(x=xin, h=h)
        if kind == 0:
            pre, arrived = _proj_cols("gm_in", h, gathered["gm_w_in"], gm_w_in.shape[2], bias=gm_b_in,
                                      beside=("gather", [shard[n] for n in with_gm_in]))
            gathered.update(zip(with_gm_in, arrived))
            yv, arrived = _gm_spatial_fwd(pre, gm_ln_g, gm_ln_b, gm_w_s[0], jnp.transpose(gm_b_s[0]), seq,
                                          beside=("gather", [shard[n] for n in with_gm_gate]))
            gathered.update(zip(with_gm_gate, arrived))
            y = _mm_plain("gm_out", yv, sq("gm_w_out"), NN)
            sv.update(pre=pre, yv=yv)
        elif kind == 1:
            qkv = _qkv_proj("fox_qkv", h, fox_qkv_w)
            ft = _mm("fox_gate_proj", fox_f_wt, h, _sds((n_heads, t), F32), (t // min(TMM, t), 1),
                     pl.BlockSpec((n_heads, d), lambda i, r: (0, 0)), pl.BlockSpec((min(TMM, t), d), lambda i, r: (i, 0)),
                     pl.BlockSpec((n_heads, min(TMM, t)), lambda i, r: (0, i)), NT)
            b_f = jnp.transpose(fox_b_f)
            frow_p = _fox_gate_fwd(ft, b_f, seq).reshape(npair, 2, t)
            o, lse, *arrived = _fox_fwd(qkv, frow_p, nb, seq, gather=[shard[n] for n in later])
            gathered.update(zip(later, arrived))
            sb_qkv_w = _full_cols(gathered["sb_w_in"], sb_w_in.shape[2])
            y = _mm_plain("fox_out", o, sq("fox_w_out"), NN)
            sv.update(qkv=qkv, ft=ft, b_f=b_f, frow=frow_p, o=o, lse=lse)
        elif kind == 2:
            qkv = _qkv_proj("sb_qkv", h, sb_qkv_w)
            o, ltot = _sb_fwd(qkv, nb, seq)
            y = _mm_plain("sb_out", o, sq("sb_w_out"), NN)
            sv.update(qkv=qkv, o=o, ltot=ltot)
        else:
            pre = _proj_cols("cv_in", h, gathered["cv_w_in"], cv_w_in.shape[2], bias=cv_bin_f)
            y1, y2 = _cv_conv_fwd(pre, cv_dw_f, cv_dwb_f, cv_lng_f, cv_lnb_f, seq)
            y = _mm_plain("cv_out", y2, sq("cv_w_out"), NN, bias=cv_bout_f)
            sv.update(pre=pre, y1=y1, y2=y2)
        x1, h2 = _lnres_fwd(xin, y, mods[l][2], ln1_g[l:l + 1], ln1_b[l:l + 1], alpha, seq, nxt=(mods[l][4], mods[l][3]))
        if l == 0:
            (hg, hu, act), arrived = _ffn_in(h2, gathered["ffn_w_in", l], 0, beside=("gather", [shard[n] for n in with_ffn_0]))
            gathered.update(zip(with_ffn_0, arrived))
            fox_full = _full_cols(gathered["fox_w_in"], fox_w_in.shape[2])
            fox_qkv_w, fox_f_wt = fox_full[:, :3 * d], jnp.transpose(fox_full[:, 3 * d:])
        else:
            hg, hu, act = _ffn_in(h2, gathered["ffn_w_in", l], 0)
        y2f = _ffn_out(act, w_ffn_out_rows(l), 0)
        sv.update(y=y, x1=x1, h2=h2, hg=hg, hu=hu, act=act, y2f=y2f)
        if l + 1 < nl:
            xin, h = _lnres_fwd(x1, y2f, mods[l][5], ln2_g[l:l + 1], ln2_b[l:l + 1], alpha, seq, nxt=(mods[l + 1][1], mods[l + 1][0]))
        else:
            xin = _lnres_fwd(x1, y2f, mods[l][5], ln2_g[l:l + 1], ln2_b[l:l + 1], alpha, seq)
        saved.append(sv)

    dx, sq_err = _loss_head(xin, tgt)
    loss = lax.psum(0.5 * jnp.sum(sq_err) / d, ("x", "y", "c"))

    small = {}
    bigg = {}
    recv = {}
    dmod_parts = [dict() for _ in range(nl)]
    pending = None
    d_ln = dict(ln1_g=[None] * nl, ln1_b=[None] * nl, ln2_g=[None] * nl, ln2_b=[None] * nl)
    beside_sb = [("ffn_w_in", 3), ("ffn_w_out", 3), "cv_w_in", "cv_w_out", ("ffn_w_in", 2), ("ffn_w_out", 2)]
    beside_fox = ["sb_w_in", "sb_w_out", ("ffn_w_in", 1), ("ffn_w_out", 1)]
    beside_dact_0 = ["fox_w_in", "fox_w_out"]
    beside_dh_0 = [("ffn_w_in", 0)]
    beside_gm = [("ffn_w_out", 0), "gm_w_out"]
    beside_gm_in = ["gm_w_in"]
    for l in reversed(range(nl)):
        sv = saved[l]
        kind = l % 4
        if pending is None:
            dxr, dy2, dlg, dlb, _, dgate2 = _lnres_bwd(dx, sv["x1"], sv["y2f"], mods[l][5], ln2_g[l:l + 1], alpha, seq)
        else:
            dxr, dy2, dlg, dlb, _, dgate2, dsc1, dsh1 = _lnres_bwd(pending[1], sv["x1"], sv["y2f"], mods[l][5], ln2_g[l:l + 1], alpha, seq,
                                                                   through=(pending[0], pending[2], ln2_b[l:l + 1]))
            dmod_parts[l + 1].update(sc1=dsc1, sh1=dsh1)
        d_ln["ln2_g"][l], d_ln["ln2_b"][l] = dlg, dlb
        if l == 0:
            (dg_, du_), landed = _ffn_dact(dy2, w_ffn_out_rows(l), sv["hg"], sv["hu"], 0, beside=("scatter", [bigg[n] for n in beside_dact_0]))
            recv.update(zip(beside_dact_0, landed))
        else:
            dg_, du_ = _ffn_dact(dy2, w_ffn_out_rows(l), sv["hg"], sv["hu"], 0)
        dwg, dwu, dwo = _ffn_bwd_weights(sv["h2"], dy2, sv["act"], dg_, du_)
        bigg["ffn_w_in", l] = jnp.concatenate([dwg, dwu], axis=0)[None]
        bigg["ffn_w_out", l] = dwo.reshape(1, N_DEV, n_ffn // 2, d)
        if l == 0:
            dh2, landed = _ffn_dh(dg_, du_, gathered["ffn_w_in", l], 0, beside=("scatter", [bigg[n] for n in beside_dh_0]))
            recv.update(zip(beside_dh_0, landed))
        else:
            dh2 = _ffn_dh(dg_, du_, gathered["ffn_w_in", l], 0)
        dxr, dy, dlg, dlb, dysum, dgate1, dsc2, dsh2 = _lnres_bwd(dxr, sv["x"], sv["y"], mods[l][2], ln1_g[l:l + 1], alpha, seq,
                                                                  through=(dh2, mods[l][4], ln1_b[l:l + 1]))
        d_ln["ln1_g"][l], d_ln["ln1_b"][l] = dlg, dlb
        hh = sv["h"]
        if kind == 0:
            dyv = _mm_plain("gm_out_bwd", dy, sq("gm_w_out"), NT)
            bigg["gm_w_out"] = _mm_plain("gm_out_dw", sv["yv"], dy, TN, GRAD_WIRE).reshape(1, N_DEV, d // N_DEV, d)
            (dpre, dws, dbst, dlng, dlnb, dbin), landed = _gm_spatial_bwd(sv["pre"], dyv, gm_ln_g, gm_ln_b, gm_w_s[0], jnp.transpose(gm_b_s[0]), seq,
                                                                          beside=("scatter", [bigg[n] for n in beside_gm]))
            recv.update(zip(beside_gm, landed))
            small.update(gm_w_s=dws[None], gm_b_s=jnp.transpose(dbst)[None], gm_ln_g=dlng, gm_ln_b=dlnb, gm_b_in=dbin)
            bigg["gm_w_in"] = _grad_cols("gm_in_dw", hh, dpre, gm_w_in.shape[2])
            dh, landed = _back_cols("gm_in_bwd", dpre, gathered["gm_w_in"], gm_w_in.shape[2], beside=("scatter", [bigg[n] for n in beside_gm_in]))
            recv.update(zip(beside_gm_in, landed))
        elif kind == 1:
            do = _mm_plain("fox_out_bwd", dy, sq("fox_w_out"), NT)
            bigg["fox_w_out"] = _mm_plain("fox_out_dw", sv["o"], dy, TN, GRAD_WIRE).reshape(1, N_DEV, d // N_DEV, d)
            dqkv, dfr, dfq, *landed = _fox_bwd(sv["qkv"], sv["frow"], sv["o"], do, sv["lse"], nb, seq,
                                               scatter=[bigg[n] for n in beside_fox])
            recv.update(zip(beside_fox, landed))
            dft, dbf = _fox_gate_bwd(sv["ft"], sv["b_f"], dfr.reshape(n_heads, t),
                                     jnp.transpose(dfq, (0, 2, 1)).reshape(n_heads, t), seq)
            small["fox_b_f"] = jnp.transpose(dbf)
            dw_qkv = _qkv_dw("fox_qkv_dw", hh, dqkv)
            tk = min(TMM, t)
            dw_ft = _mm("fox_gate_dw", dft, hh, _sds((n_heads, d), F32), (1, t // tk),
                        pl.BlockSpec((n_heads, tk), lambda j, r: (0, r)), pl.BlockSpec((tk, d), lambda j, r: (r, 0)),
                        pl.BlockSpec((n_heads, d), lambda j, r: (0, 0)), NN)
            bigg["fox_w_in"] = _col_blocks(jnp.concatenate([dw_qkv, jnp.transpose(dw_ft).astype(GRAD_WIRE)], axis=1), fox_w_in.shape[2])
            dh_a = _qkv_dh("fox_qkv_bwd", dqkv, fox_qkv_w)
            tm = min(TMM, t)
            dh = _mm("fox_gate_bwd_h", dft, fox_f_wt, _sds((t, d), F32), (t // tm, 1),
                     pl.BlockSpec((n_heads, tm), lambda i, r: (0, i)), pl.BlockSpec((n_heads, d), lambda i, r: (0, 0)),
                     pl.BlockSpec((tm, d), lambda i, r: (i, 0)), TN, (dh_a,), (pl.BlockSpec((tm, d), lambda i, r: (i, 0)),), _add)
        elif kind == 2:
            do = _mm_plain("sb_out_bwd", dy, sq("sb_w_out"), NT)
            bigg["sb_w_out"] = _mm_plain("sb_out_dw", sv["o"], dy, TN, GRAD_WIRE).reshape(1, N_DEV, d // N_DEV, d)
            dqkv, *landed = _sb_bwd(sv["qkv"], do, sv["ltot"], nb, seq, scatter=[bigg[n] for n in beside_sb])
            recv.update(zip(beside_sb, landed))
            bigg["sb_w_in"] = _col_blocks(_qkv_dw("sb_qkv_dw", hh, dqkv), sb_w_in.shape[2])
            dh = _qkv_dh("sb_qkv_bwd", dqkv, sb_qkv_w)
        else:
            dy2c = _mm_plain("cv_out_bwd", dy, sq("cv_w_out"), NT)
            bigg["cv_w_out"] = _mm_plain("cv_out_dw", sv["y2"], dy, TN, GRAD_WIRE).reshape(1, N_DEV, d // N_DEV, d)
            dy1, dlng, dlnb, ddwb = _cv_norm_bwd(sv["y1"], dy2c, cv_lng_f, cv_lnb_f)
            dpre, ddw, dbin = _cv_conv_bwd(sv["pre"], dy1, cv_dw_f, seq)
            small.update(cv_b_out=dysum, cv_ln_g=dlng, cv_ln_b=dlnb, cv_dw_b=ddwb, cv_dw=ddw[:CONV_WIDTH], cv_b_in=dbin)
            bigg["cv_w_in"] = _grad_cols("cv_in_dw", hh, dpre, cv_w_in.shape[2])
            dh = _back_cols("cv_in_bwd", dpre, gathered["cv_w_in"], cv_w_in.shape[2])
        pending = (dh, dxr, mods[l][1])
        dmod_parts[l].update(g1=dgate1, sh2=dsh2, sc2=dsc2, g2=dgate2)
    dx, dsc1, dsh1 = _modulate_bwd(pending[0], pending[1], saved[0]["x"], pending[2], seq)
    dmod_parts[0].update(sc1=dsc1, sh1=dsh1)
    dmods = [jnp.concatenate([p["sh1"], p["sc1"], p["g1"], p["sh2"], p["sc2"], p["g2"]], axis=2)[:, 0, :] for p in dmod_parts]
    grad_x = dx.reshape(nb, seq, d)
    for n in d_ln:
        small[n] = jnp.concatenate(d_ln[n], axis=0)

    dmod_rows = jnp.stack(dmods).reshape(nl * nb, 6 * d)
    dmod_g = _exchange_small(_pad_rows(dmod_rows, 8 * ((nl * nb + 7) // 8)), "gather_dmod", False)[:, :nl * nb]
    dmod_all = jnp.transpose(dmod_g.reshape(N_DEV, nl, nb, 6 * d), (1, 0, 2, 3)).reshape(nl, N_DEV * nb, 6 * d)
    dmod_loc = lax.dynamic_slice_in_dim(dmod_all, me * n_mod, n_mod, axis=2)
    g_mod_w, g_mod_b = _mod_bwd(c_all, dmod_loc, dmod_all)
    grads = dict(mod_w=g_mod_w, mod_b=g_mod_b[:, 0, :])

    rep = ["ln1_g", "ln1_b", "ln2_g", "ln2_b", "gm_b_in", "gm_ln_g", "gm_ln_b", "gm_w_s", "gm_b_s", "fox_b_f"]
    cvs = ["cv_b_in", "cv_dw", "cv_dw_b", "cv_ln_g", "cv_ln_b", "cv_b_out"]

    def rows_of(a):
        flat = a.reshape(-1)
        pad = (-flat.shape[0]) % d
        return jnp.pad(flat, (0, pad)).reshape(-1, d)

    pack_rows = [rows_of(small[n]) for n in rep + cvs]
    counts = [r.shape[0] for r in pack_rows]
    total = sum(counts)
    pack = _pad_rows(jnp.concatenate(pack_rows, axis=0), 8 * ((total + 7) // 8))
    summed = _exchange_small(pack, "allreduce_small", True)
    offs = [sum(counts[:i]) for i in range(len(counts))]
    rep_rows = sum(counts[:len(rep)])
    for n, o_, cnt in zip(rep + cvs, offs, counts):
        full = summed[o_:o_ + cnt].reshape(-1)
        if n in rep:
            grads[n] = full[:weights[n].size].reshape(weights[n].shape)
        else:
            wshape = weights[n].shape
            cols = wshape[-1]
            full = full[:math.prod(wshape[:-1]) * cols * N_DEV].reshape(wshape[:-1] + (cols * N_DEV,))
            grads[n] = lax.dynamic_slice_in_dim(full, me * cols, cols, axis=full.ndim - 1)

    for n in ("ffn_w_in", "ffn_w_out"):
        recv[n] = jnp.concatenate([recv[n, l] for l in range(nl)], axis=1)

    outs = {}

    def view2(a):
        return a.reshape(-1, a.shape[-1])

    for n in big:
        w2 = view2(weights[n])
        res = _adamw("adamw_" + n, w2, view2(mom1[n]), view2(mom2[n]), parts=recv[n].reshape((N_DEV,) + w2.shape))
        outs[n] = [r.reshape(weights[n].shape) for r in res]
    res = _adamw("adamw_mod_w", view2(mod_w), view2(m_mod_w), view2(v_mod_w), g=view2(grads["mod_w"]))
    outs["mod_w"] = [r.reshape(mod_w.shape) for r in res]
    rp = lambda src: _pad_rows(jnp.concatenate([rows_of(src[n]) for n in rep], axis=0), 8 * ((rep_rows + 7) // 8))
    res = _adamw("adamw_replicated", rp(weights), rp(mom1), rp(mom2), g=rp(grads))
    for n, o_, cnt in zip(rep, offs, counts):
        outs[n] = [r[o_:o_ + cnt].reshape(-1)[:weights[n].size].reshape(weights[n].shape) for r in res]
    cv_cols = weights["cv_b_out"].shape[-1]
    cp = lambda src: jnp.concatenate([src[n].reshape(-1, cv_cols) for n in cvs], axis=0)
    cv_cnt = [weights[n].size // cv_cols for n in cvs]
    cv_tot = sum(cv_cnt)
    cpp = lambda src: _pad_rows(cp(src), 8 * ((cv_tot + 7) // 8))
    res = _adamw("adamw_cv_small", cpp(weights), cpp(mom1), cpp(mom2), g=cpp(grads))
    o_ = 0
    for n, cnt in zip(cvs, cv_cnt):
        outs[n] = [r[o_:o_ + cnt].reshape(weights[n].shape) for r in res]
        o_ += cnt
    res = _adamw("adamw_mod_b", mod_b, m_mod_b, v_mod_b, g=grads["mod_b"])
    outs["mod_b"] = list(res)

    return (loss, grad_x, *[outs[n][0] for n in names], *[outs[n][1] for n in names],
            *[outs[n][2] for n in names], *[outs[n][3] for n in names])
```

```python
import functools
import math

import jax
import jax.numpy as jnp
from jax import lax
from jax.experimental import pallas as pl
from jax.experimental.pallas import tpu as pltpu

F32 = jnp.float32
BF16 = jnp.bfloat16
MESH = pl.DeviceIdType.MESH

N_DEV = 8
HEAD_DIM = 64
LANES = 128
SUBLANES = 8
GM_CHUNK = 128
GM_GROUPS = 8
CONV_WIDTH = 31
CONV_HALO = 32
CONV_ROWS = 32
LN_EPS = 1e-5
NEG_INF = -1e30
SB_DEAD = -100.0
GRAD_WIRE = jnp.bfloat16
FFN_KEEP = jnp.bfloat16

ADAM_LR = 0.001
ADAM_B1 = 0.9
ADAM_B2 = 0.999
ADAM_EPS = 1e-08
ADAM_WD = 0.01
ADAM_STEP = 10

TM = 512
TMM = 1024
TQ = 256
FOX_FWD_UNROLL = 8
FOX_BWD_UNROLL = 4
SB_BWD_UNROLL = 2
TT = 512
VMEM_BIG = 56 * 1024 * 1024

NN = (((1,), (0,)), ((), ()))
NT = (((1,), (1,)), ((), ()))
TN = (((0,), (0,)), ((), ()))


def _call(body, name, out_shape, grid=None, in_specs=None, out_specs=None, scratch=(), vmem=None, aliases=None):
    params = {}
    if grid is not None:
        params["dimension_semantics"] = ("arbitrary",) * len(grid)
    if vmem is not None:
        params["vmem_limit_bytes"] = vmem
    kw = {}
    if grid is not None:
        kw["grid"] = grid
    if in_specs is not None:
        kw["in_specs"] = in_specs
    if out_specs is not None:
        kw["out_specs"] = out_specs
    if aliases is not None:
        kw["input_output_aliases"] = aliases
    return pl.pallas_call(body, name=name, out_shape=out_shape, scratch_shapes=list(scratch),
                          compiler_params=pltpu.CompilerParams(**params), **kw)


def _sds(shape, dtype):
    return jax.ShapeDtypeStruct(tuple(shape), dtype)


def _dot(a, b, dims=NN):
    return lax.dot_general(a.astype(BF16), b.astype(BF16), dims, preferred_element_type=F32)


def _split3(x):
    h1 = x.astype(BF16)
    r1 = x - h1.astype(F32)
    h2 = r1.astype(BF16)
    h3 = (r1 - h2.astype(F32)).astype(BF16)
    return h1, h2, h3


def _dot_exact(x, m, dims=NN):
    h1, h2, h3 = _split3(x)
    d = lambda h: lax.dot_general(h, m, dims, preferred_element_type=F32)
    return (d(h1) + d(h2)) + d(h3)


def _dot_exact_rhs(m, x, dims=NN):
    h1, h2, h3 = _split3(x)
    d = lambda h: lax.dot_general(m, h, dims, preferred_element_type=F32)
    return (d(h1) + d(h2)) + d(h3)


def _silu(x):
    return x * jax.nn.sigmoid(x)


def _gelu(x):
    return 0.5 * x * (1.0 + lax.erf(x * (2.0 ** -0.5)))


def _gelu_grad(x):
    return 0.5 * (1.0 + lax.erf(x * (2.0 ** -0.5))) + x * jnp.exp(-0.5 * x * x) * ((2.0 * math.pi) ** -0.5)


def _log_sigmoid(z):
    return jnp.minimum(z, 0.0) - jnp.log(1.0 + jnp.exp(-jnp.abs(z)))


def _ln_stats(r):
    mu = jnp.mean(r, axis=-1, keepdims=True)
    rc = r - mu
    var = jnp.mean(rc * rc, axis=-1, keepdims=True)
    return rc, lax.rsqrt(var + LN_EPS)


def _colsum(x):
    return jnp.sum(x, axis=0, keepdims=True)


def _peers():
    mx, my, mc = lax.axis_index("x"), lax.axis_index("y"), lax.axis_index("c")
    me = 4 * mx + 2 * my + mc
    out = []
    for k in range(1, N_DEV):
        px = 1 - mx if (k >> 2) & 1 else mx
        py = 1 - my if (k >> 1) & 1 else my
        pc = 1 - mc if k & 1 else mc
        out.append(((px, py, pc), 4 * px + 2 * py + pc))
    return me, out


def _exchange_small(x, name, reduce):
    rows, cols = x.shape

    def body(x_ref, o_ref, *rest):
        if reduce:
            land, send_sems, recv_sems, local_sem = rest
        else:
            land = o_ref
            send_sems, recv_sems, local_sem = rest
        me, peers = _peers()
        mine = pltpu.make_async_copy(x_ref, land.at[me], local_sem)
        mine.start()
        sends = []
        for k, (peer, _) in enumerate(peers):
            cp = pltpu.make_async_remote_copy(src_ref=x_ref, dst_ref=land.at[me], send_sem=send_sems.at[k],
                                              recv_sem=recv_sems.at[k], device_id=peer, device_id_type=MESH)
            cp.start()
            sends.append(cp)
        for k, (peer, blk) in enumerate(peers):
            pltpu.make_async_remote_copy(src_ref=x_ref, dst_ref=land.at[blk], send_sem=send_sems.at[k],
                                         recv_sem=recv_sems.at[k], device_id=peer, device_id_type=MESH).wait_recv()
        for cp in sends:
            cp.wait_send()
        mine.wait()
        if reduce:
            acc = land[0]
            for s in range(1, N_DEV):
                acc = acc + land[s]
            o_ref[...] = acc

    vm = pl.BlockSpec(memory_space=pltpu.VMEM)
    scratch = [pltpu.SemaphoreType.DMA((N_DEV - 1,)), pltpu.SemaphoreType.DMA((N_DEV - 1,)), pltpu.SemaphoreType.DMA]
    if reduce:
        scratch = [pltpu.VMEM((N_DEV, rows, cols), F32)] + scratch
        out = _sds((rows, cols), F32)
    else:
        out = _sds((N_DEV, rows, cols), F32)
    return _call(body, name, out, in_specs=[vm], out_specs=vm, scratch=scratch, vmem=VMEM_BIG)(x)


def _comm_scratch(n):
    return [pltpu.SemaphoreType.DMA((n, N_DEV - 1)), pltpu.SemaphoreType.DMA((n, N_DEV - 1)), pltpu.SemaphoreType.DMA((n,))]


def _gather_stage(stage, ins, outs, send_sems, recv_sems, local_sems):
    n = len(ins)
    mx, my, mc = lax.axis_index("x"), lax.axis_index("y"), lax.axis_index("c")
    here, sibling = (mx, my, mc), (mx, my, 1 - mc)
    chips = [(1 - mx, my), (mx, 1 - my), (1 - mx, 1 - my)]

    def block(px, py, pc):
        return 4 * px + 2 * py + pc

    def copy(a, k, blk, to, src=None):
        dst = outs[a].at[:, blk]
        return pltpu.make_async_remote_copy(src_ref=dst if src is None else src, dst_ref=dst, send_sem=send_sems.at[a, k],
                                            recv_sem=recv_sems.at[a, k], device_id=to, device_id_type=MESH)

    me = block(*here)
    for a in range(n):
        local = pltpu.make_async_copy(ins[a], outs[a].at[:, me], local_sems.at[a])
        first = [copy(a, 0, me, sibling, src=ins[a])] + [copy(a, 1 + j, me, (*chip, mc), src=ins[a]) for j, chip in enumerate(chips)]
        if stage == 0:
            local.start()
            for cp in first:
                cp.start()
        if stage == 1:
            for j, chip in enumerate(chips):
                copy(a, 1 + j, block(*chip, mc), here).wait_recv()
                copy(a, 4 + j, block(*chip, mc), sibling).start()
        if stage == 2:
            copy(a, 0, block(mx, my, 1 - mc), here).wait_recv()
            for j, chip in enumerate(chips):
                copy(a, 4 + j, block(*chip, 1 - mc), here).wait_recv()
            for cp in first:
                cp.wait_send()
            for j, chip in enumerate(chips):
                copy(a, 4 + j, block(*chip, mc), sibling).wait_send()
            local.wait()


def _gather_weights(shards):
    n = len(shards)

    def body(*refs):
        for stage in range(3):
            _gather_stage(stage, refs[:n], refs[n:2 * n], *refs[2 * n:])

    hbm = pl.BlockSpec(memory_space=pl.ANY)
    return _call(body, "gather_weights", _gathered_shapes(shards), in_specs=[hbm] * n, out_specs=[hbm] * n, scratch=_comm_scratch(n))(*shards)


def _gathered_shapes(shards):
    return [_sds((s.shape[0], N_DEV) + s.shape[1:], s.dtype) for s in shards]


def _scattered_shapes(grads):
    return [_sds((N_DEV, g.shape[0]) + g.shape[2:], g.dtype) for g in grads]


def _run(beside, body, name, out_shape, args, grid, in_specs, out_specs, scratch=(), vmem=None):
    if beside is None:
        return _call(body, name, out_shape, grid=grid, in_specs=in_specs, out_specs=out_specs, scratch=scratch, vmem=vmem)(*args)
    kind, arrays = beside
    n = len(arrays)
    outs = tuple(out_shape) if isinstance(out_shape, (tuple, list)) else (out_shape,)
    ospecs = tuple(out_specs) if isinstance(out_specs, (tuple, list)) else (out_specs,)
    total = math.prod(grid)
    gather = kind == "gather"
    stage_fn = _gather_stage if gather else _scatter_stage
    early = {0: 0, 1: (3 * total) // 4} if gather else {0: 0}

    def carrier(*refs):
        own_in, send, own_out, land, sems, own_scratch = _carried(n, len(in_specs), len(outs), refs)
        at = 0
        for ax, size in enumerate(grid):
            at = at * size + pl.program_id(ax)
        for stage, when in early.items():
            @pl.when(at == when)
            def _(stage=stage):
                stage_fn(stage, send, land, *sems)
        body(*own_in, *own_out, *own_scratch)

        @pl.when(at == total - 1)
        def _():
            stage_fn(2 if gather else 1, send, land, *sems)

    hbm = pl.BlockSpec(memory_space=pl.ANY)
    shapes = _gathered_shapes(arrays) if gather else _scattered_shapes(arrays)
    res = _call(carrier, name, (*outs, *shapes), grid=grid, in_specs=[*in_specs, *[hbm] * n], out_specs=(*ospecs, *[hbm] * n),
                scratch=[*_comm_scratch(n), *scratch], vmem=vmem)(*args, *arrays)
    own = res[:len(outs)]
    return (own if len(own) > 1 else own[0]), list(res[len(outs):])


def _scatter_stage(stage, ins, outs, send_sems, recv_sems, local_sems):
    me, peers = _peers()
    for a in range(len(ins)):
        local = pltpu.make_async_copy(ins[a].at[:, me], outs[a].at[me], local_sems.at[a])
        sends = [pltpu.make_async_remote_copy(src_ref=ins[a].at[:, blk], dst_ref=outs[a].at[me], send_sem=send_sems.at[a, k],
                                              recv_sem=recv_sems.at[a, k], device_id=peer, device_id_type=MESH)
                 for k, (peer, blk) in enumerate(peers)]
        if stage == 0:
            local.start()
            for cp in sends:
                cp.start()
        if stage == 1:
            for k, (peer, blk) in enumerate(peers):
                pltpu.make_async_remote_copy(src_ref=ins[a].at[:, me], dst_ref=outs[a].at[blk], send_sem=send_sems.at[a, k],
                                             recv_sem=recv_sems.at[a, k], device_id=peer, device_id_type=MESH).wait_recv()
            for cp in sends:
                cp.wait_send()
            local.wait()


def _mm(name, a, b, out, grid, a_spec, b_spec, o_spec, dims, extra=(), extra_specs=(), epilogue=None, vmem=VMEM_BIG, beside=None):
    nred = grid[-1]
    red_axis = len(grid) - 1
    acc_shape = tuple(d for d in o_spec.block_shape if d is not None)
    n_extra = len(extra)

    def body(a_ref, b_ref, *rest):
        ex = rest[:n_extra]
        o_ref = rest[n_extra]

        def finish(acc):
            if epilogue is not None:
                acc = epilogue(acc, *[e[...] for e in ex])
            o_ref[...] = acc.astype(o_ref.dtype)

        prod = _dot(a_ref[...], b_ref[...], dims)
        if nred == 1:
            finish(prod)
        else:
            acc_ref = rest[n_extra + 1]
            r = pl.program_id(red_axis)

            @pl.when(r == 0)
            def _():
                acc_ref[...] = prod

            @pl.when(r > 0)
            def _():
                acc_ref[...] += prod

            @pl.when(r == nred - 1)
            def _():
                finish(acc_ref[...])

    scratch = [pltpu.VMEM(acc_shape, F32)] if nred > 1 else []
    return _run(beside, body, name, out, (a, b, *extra), grid, [a_spec, b_spec, *extra_specs], o_spec, scratch, vmem)


def _add(acc, x):
    return acc + x


def _proj_cols(name, h, w, n_slot, bias=None, out_dtype=F32, beside=None):
    t, k = h.shape
    s = w.shape[1]
    tm = min(TM, t)

    def body(h_ref, w_ref, *rest):
        o_ref = rest[-1]
        hv = h_ref[...]
        for j in range(s):
            cols = slice(j * n_slot, (j + 1) * n_slot)
            acc = _dot(hv, w_ref[j])
            if bias is not None:
                acc = acc + rest[0][:, cols]
            o_ref[:, cols] = acc.astype(o_ref.dtype)

    ins, in_specs = [h, w], [pl.BlockSpec((tm, k), lambda i: (i, 0)), pl.BlockSpec((None, s, k, n_slot), lambda i: (0, 0, 0, 0))]
    if bias is not None:
        ins.append(bias)
        in_specs.append(pl.BlockSpec((1, s * n_slot), lambda i: (0, 0)))
    return _run(beside, body, name, _sds((t, s * n_slot), out_dtype), tuple(ins), (t // tm,), in_specs,
                pl.BlockSpec((tm, s * n_slot), lambda i: (i, 0)), vmem=VMEM_BIG)


def _accumulate_over_tokens(name, ins, in_specs, out, o_spec, acc_shape, n_steps, terms, store=None, fill=None):
    n_in = len(ins) + (fill is not None)

    def body(*refs):
        o_ref, acc_ref = refs[n_in], refs[n_in + 1]
        r = pl.program_id(0)

        @pl.when(r == 0)
        def _():
            acc_ref[...] = jnp.zeros_like(acc_ref)

        for s, prod in enumerate(terms(*refs[:len(ins)])):
            acc_ref[s] += prod

        @pl.when(r == n_steps - 1)
        def _():
            if store is None:
                o_ref[...] = acc_ref[...].reshape(o_ref.shape).astype(o_ref.dtype)
            else:
                store(o_ref, acc_ref)

    if fill is None:
        return _call(body, name, out, grid=(n_steps,), in_specs=in_specs, out_specs=o_spec,
                     scratch=[pltpu.VMEM(acc_shape, F32)], vmem=VMEM_BIG)(*ins)
    return _call(body, name, out, grid=(n_steps,), in_specs=[*in_specs, pl.BlockSpec(memory_space=pl.ANY)], out_specs=o_spec,
                 scratch=[pltpu.VMEM(acc_shape, F32)], vmem=VMEM_BIG, aliases={len(ins): 0})(*ins, fill)


def _grad_cols(name, h, g, n_slot):
    t, k = h.shape
    s = g.shape[1] // n_slot
    tk = min(TMM, t)

    def terms(h_ref, g_ref):
        hv = h_ref[...]
        return [_dot(hv, g_ref[:, j * n_slot:(j + 1) * n_slot], TN) for j in range(s)]

    return _accumulate_over_tokens(name, (h, g), [pl.BlockSpec((tk, k), lambda r: (r, 0)), pl.BlockSpec((tk, s * n_slot), lambda r: (r, 0))],
                                   _sds((1, s, k, n_slot), GRAD_WIRE), pl.BlockSpec((1, s, k, n_slot), lambda r: (0, 0, 0, 0)),
                                   (s, k, n_slot), t // tk, terms)


def _back_cols(name, g, w, n_slot, beside=None):
    t = g.shape[0]
    s, k = w.shape[1], w.shape[2]
    tm = min(TM, t)

    def body(g_ref, w_ref, o_ref):
        acc = None
        for j in range(s):
            term = _dot(g_ref[:, j * n_slot:(j + 1) * n_slot], w_ref[j], NT)
            acc = term if acc is None else acc + term
        o_ref[...] = acc

    return _run(beside, body, name, _sds((t, k), F32), (g, w), (t // tm,),
                [pl.BlockSpec((tm, s * n_slot), lambda i: (i, 0)), pl.BlockSpec((None, s, k, n_slot), lambda i: (0, 0, 0, 0))],
                pl.BlockSpec((tm, k), lambda i: (i, 0)), vmem=VMEM_BIG)


def _mm_plain(name, a, b, dims, out_dtype=F32, bias=None):
    if dims == TN:
        t, k = a.shape
        n = b.shape[1]
        tk = min(TMM, t)
        return _mm(name, a, b, _sds((k, n), out_dtype), (1, t // tk),
                   pl.BlockSpec((tk, k), lambda j, r: (r, 0)), pl.BlockSpec((tk, n), lambda j, r: (r, 0)),
                   pl.BlockSpec((k, n), lambda j, r: (0, 0)), TN)
    t = a.shape[0]
    tm = min(TMM, t)
    n = b.shape[1] if dims == NN else b.shape[0]
    extra, especs, epi = (), (), None
    if bias is not None:
        extra, especs, epi = (bias,), (pl.BlockSpec((1, n), lambda i, r: (0, 0)),), _add
    return _mm(name, a, b, _sds((t, n), out_dtype), (t // tm, 1),
               pl.BlockSpec((tm, a.shape[1]), lambda i, r: (i, 0)), pl.BlockSpec(b.shape, lambda i, r: (0, 0)),
               pl.BlockSpec((tm, n), lambda i, r: (i, 0)), dims, extra, especs, epi)


def _mod_fwd(c_all, mod_w, mod_b_loc):
    nl, d, n = mod_w.shape
    nb = c_all.shape[0]

    def body(c_ref, w_ref, b_ref, o_ref):
        o_ref[...] = _dot(_silu(c_ref[...]), w_ref[...]) + b_ref[...]

    return _call(body, "mod_fwd", _sds((nl, nb, n), F32), grid=(nl,),
                 in_specs=[pl.BlockSpec((nb, d), lambda l: (0, 0)), pl.BlockSpec((None, d, n), lambda l: (l, 0, 0)),
                           pl.BlockSpec((None, 1, n), lambda l: (l, 0, 0))],
                 out_specs=pl.BlockSpec((None, nb, n), lambda l: (l, 0, 0)))(c_all, mod_w, mod_b_loc)


def _mod_bwd(c_all, dmod_loc, dmod_all):
    nl, nb, n = dmod_loc.shape
    d = c_all.shape[1]
    n_all = dmod_all.shape[2]

    def body(c_ref, dl_ref, da_ref, gw_ref, gb_ref):
        gw_ref[...] = _dot(_silu(c_ref[...]), dl_ref[...], TN)
        gb_ref[...] = _colsum(da_ref[...])

    return _call(body, "mod_bwd", (_sds((nl, d, n), F32), _sds((nl, 1, n_all), F32)), grid=(nl,),
                 in_specs=[pl.BlockSpec((nb, d), lambda l: (0, 0)), pl.BlockSpec((None, nb, n), lambda l: (l, 0, 0)),
                           pl.BlockSpec((None, nb, n_all), lambda l: (l, 0, 0))],
                 out_specs=(pl.BlockSpec((None, d, n), lambda l: (l, 0, 0)), pl.BlockSpec((None, 1, n_all), lambda l: (l, 0, 0))),
                 )(c_all, dmod_loc, dmod_all)


def _row_spec(d, tpb):
    return pl.BlockSpec((None, 1, d), lambda i: (i // tpb, 0, 0))


def _tile_spec(tm, d):
    return pl.BlockSpec((tm, d), lambda i: (i, 0))


def _vec_spec(d):
    return pl.BlockSpec((1, d), lambda i: (0, 0))


def _modulate(x, sc, sh, seq):
    t, d = x.shape
    tm = min(TM, seq)
    tpb = seq // tm

    def body(x_ref, sc_ref, sh_ref, h_ref):
        h_ref[...] = (x_ref[...] * (1.0 + sc_ref[...]) + sh_ref[...]).astype(BF16)

    return _call(body, "modulate", _sds((t, d), BF16), grid=(t // tm,),
                 in_specs=[_tile_spec(tm, d), _row_spec(d, tpb), _row_spec(d, tpb)], out_specs=_tile_spec(tm, d))(x, sc, sh)


def _lnres_fwd(x, y, gate, lg, lb, alpha, seq, nxt=None):
    t, d = x.shape
    tm = min(TM, seq)
    tpb = seq // tm

    def body(x_ref, y_ref, g_ref, lg_ref, lb_ref, *rest):
        r = alpha * x_ref[...] + (1.0 + g_ref[...]) * y_ref[...]
        rc, rstd = _ln_stats(r)
        xn = rc * rstd * lg_ref[...] + lb_ref[...]
        if nxt is None:
            rest[0][...] = xn
        else:
            sc_ref, sh_ref, xo_ref, h_ref = rest
            xo_ref[...] = xn
            h_ref[...] = (xn * (1.0 + sc_ref[...]) + sh_ref[...]).astype(BF16)

    ins = [_tile_spec(tm, d), _tile_spec(tm, d), _row_spec(d, tpb), _vec_spec(d), _vec_spec(d)]
    if nxt is None:
        return _call(body, "lnres_fwd_last", _sds((t, d), F32), grid=(t // tm,), in_specs=ins,
                     out_specs=_tile_spec(tm, d))(x, y, gate, lg, lb)
    return _call(body, "lnres_fwd", (_sds((t, d), F32), _sds((t, d), BF16)), grid=(t // tm,),
                 in_specs=ins + [_row_spec(d, tpb), _row_spec(d, tpb)],
                 out_specs=(_tile_spec(tm, d), _tile_spec(tm, d)))(x, y, gate, lg, lb, *nxt)


def _loss_head(x, tgt):
    t, d = x.shape
    tm = min(TM, t)

    def body(x_ref, t_ref, dx_ref, sq_ref):
        e = x_ref[...] - t_ref[...]
        dx_ref[...] = e * (1.0 / d)

        @pl.when(pl.program_id(0) == 0)
        def _():
            sq_ref[...] = jnp.zeros_like(sq_ref)

        sq_ref[...] += _colsum(e * e)

    return _call(body, "loss_head", (_sds((t, d), F32), _sds((1, d), F32)), grid=(t // tm,),
                 in_specs=[_tile_spec(tm, d), _tile_spec(tm, d)], out_specs=(_tile_spec(tm, d), _vec_spec(d)))(x, tgt)


def _lnres_bwd(dxo, x, y, gate, lg, alpha, seq, through=None):
    t, d = x.shape
    tm = min(TM, seq)
    tpb = seq // tm
    nb = t // seq

    def body(dxo_ref, x_ref, y_ref, g_ref, lg_ref, *rest):
        if through is None:
            dxr_ref, dy_ref, dlg_ref, dlb_ref, dys_ref, dg_ref = rest
        else:
            dh_ref, sc_ref, lb_ref, dxr_ref, dy_ref, dlg_ref, dlb_ref, dys_ref, dg_ref, dsc_ref, dsh_ref = rest
        i = pl.program_id(0)
        yv = y_ref[...]
        r = alpha * x_ref[...] + (1.0 + g_ref[...]) * yv
        rc, rstd = _ln_stats(r)
        xhat = rc * rstd
        dxo_v = dxo_ref[...]
        if through is not None:
            dhv = dh_ref[...]
            dxo_v = dxo_v + dhv * (1.0 + sc_ref[...])

            @pl.when(i % tpb == 0)
            def _():
                dsc_ref[...] = jnp.zeros_like(dsc_ref)
                dsh_ref[...] = jnp.zeros_like(dsh_ref)

            dsc_ref[...] += _colsum(dhv * (xhat * lg_ref[...] + lb_ref[...]))
            dsh_ref[...] += _colsum(dhv)
        dxh = dxo_v * lg_ref[...]
        m1 = jnp.mean(dxh, axis=-1, keepdims=True)
        m2 = jnp.mean(dxh * xhat, axis=-1, keepdims=True)
        dr = rstd * (dxh - m1 - xhat * m2)
        dyv = (1.0 + g_ref[...]) * dr
        dxr_ref[...] = alpha * dr
        dy_ref[...] = dyv.astype(BF16)

        @pl.when(i == 0)
        def _():
            dlg_ref[...] = jnp.zeros_like(dlg_ref)
            dlb_ref[...] = jnp.zeros_like(dlb_ref)
            dys_ref[...] = jnp.zeros_like(dys_ref)

        @pl.when(i % tpb == 0)
        def _():
            dg_ref[...] = jnp.zeros_like(dg_ref)

        dlg_ref[...] += _colsum(dxo_v * xhat)
        dlb_ref[...] += _colsum(dxo_v)
        dys_ref[...] += _colsum(dyv)
        dg_ref[...] += _colsum(dr * yv)

    out = [_sds((t, d), F32), _sds((t, d), BF16), _sds((1, d), F32), _sds((1, d), F32), _sds((1, d), F32), _sds((nb, 1, d), F32)]
    out_specs = [_tile_spec(tm, d), _tile_spec(tm, d), _vec_spec(d), _vec_spec(d), _vec_spec(d), _row_spec(d, tpb)]
    ins = [dxo, x, y, gate, lg]
    in_specs = [_tile_spec(tm, d), _tile_spec(tm, d), _tile_spec(tm, d), _row_spec(d, tpb), _vec_spec(d)]
    if through is not None:
        ins += list(through)
        in_specs += [_tile_spec(tm, d), _row_spec(d, tpb), _vec_spec(d)]
        out += [_sds((nb, 1, d), F32), _sds((nb, 1, d), F32)]
        out_specs += [_row_spec(d, tpb), _row_spec(d, tpb)]
    return _call(body, "lnres_bwd" if through is None else "lnres_mod_bwd", tuple(out), grid=(t // tm,),
                 in_specs=in_specs, out_specs=tuple(out_specs))(*ins)


def _modulate_bwd(dh, dxr, x, sc, seq):
    t, d = x.shape
    tm = min(TM, seq)
    tpb = seq // tm
    nb = t // seq

    def body(dh_ref, dxr_ref, x_ref, sc_ref, dx_ref, dsc_ref, dsh_ref):
        dhv = dh_ref[...]
        dx_ref[...] = dxr_ref[...] + dhv * (1.0 + sc_ref[...])

        @pl.when(pl.program_id(0) % tpb == 0)
        def _():
            dsc_ref[...] = jnp.zeros_like(dsc_ref)
            dsh_ref[...] = jnp.zeros_like(dsh_ref)

        dsc_ref[...] += _colsum(dhv * x_ref[...])
        dsh_ref[...] += _colsum(dhv)

    return _call(body, "modulate_bwd", (_sds((t, d), F32), _sds((nb, 1, d), F32), _sds((nb, 1, d), F32)), grid=(t // tm,),
                 in_specs=[_tile_spec(tm, d), _tile_spec(tm, d), _tile_spec(tm, d), _row_spec(d, tpb)],
                 out_specs=(_tile_spec(tm, d), _row_spec(d, tpb), _row_spec(d, tpb)))(dh, dxr, x, sc)


def _ffn_in(h, w_in, layer, beside=None):
    t, d = h.shape
    n = w_in.shape[3]
    half = N_DEV // 2
    tm = min(TMM, t)

    def body(h_ref, wg_ref, wu_ref, g_ref, u_ref, a_ref):
        hv = h_ref[...]
        g = _dot(hv, wg_ref[...])
        u = _dot(hv, wu_ref[...])
        g_ref[...] = g.astype(FFN_KEEP)
        u_ref[...] = u.astype(FFN_KEEP)
        a_ref[...] = (_silu(g) * u).astype(BF16)

    blk = pl.BlockSpec((None, tm, n), lambda p, i: (p, i, 0))
    return _run(beside, body, "ffn_in", (_sds((half, t, n), FFN_KEEP), _sds((half, t, n), FFN_KEEP), _sds((half, t, n), BF16)),
                (h, w_in, w_in), (half, t // tm),
                [pl.BlockSpec((tm, d), lambda p, i: (i, 0)),
                 pl.BlockSpec((None, None, d, n), lambda p, i: (layer, p, 0, 0)),
                 pl.BlockSpec((None, None, d, n), lambda p, i: (layer, p + half, 0, 0))],
                (blk, blk, blk), vmem=VMEM_BIG)


def _ffn_out(act, w_out, layer):
    half, t, n = act.shape
    d = w_out.shape[2]
    tm = min(TM, t)

    def body(a_ref, w_ref, o_ref):
        acc = _dot(a_ref[0], w_ref[0])
        for p in range(1, half):
            acc = acc + _dot(a_ref[p], w_ref[p])
        o_ref[...] = acc

    return _call(body, "ffn_out", _sds((t, d), F32), grid=(t // tm,),
                 in_specs=[pl.BlockSpec((half, tm, n), lambda i: (0, i, 0)),
                           pl.BlockSpec((half, n, d), lambda i: (layer // half, 0, 0))],
                 out_specs=pl.BlockSpec((tm, d), lambda i: (i, 0)), vmem=VMEM_BIG)(act, w_out)


def _ffn_dact(dy, w_out, hg, hu, layer, beside=None):
    half, t, n = hg.shape
    d = dy.shape[1]
    tm = min(TMM, t)

    def body(dy_ref, w_ref, g_ref, u_ref, dg_ref, du_ref):
        da = _dot(dy_ref[...], w_ref[...], NT)
        g = g_ref[...].astype(F32)
        sg = jax.nn.sigmoid(g)
        dg_ref[...] = (da * u_ref[...].astype(F32) * (sg * (1.0 + g * (1.0 - sg)))).astype(BF16)
        du_ref[...] = (da * (g * sg)).astype(BF16)

    blk = pl.BlockSpec((None, tm, n), lambda p, i: (p, i, 0))
    return _run(beside, body, "ffn_dact", (_sds((half, t, n), BF16), _sds((half, t, n), BF16)), (dy, w_out, hg, hu), (half, t // tm),
                [pl.BlockSpec((tm, d), lambda p, i: (i, 0)), pl.BlockSpec((None, n, d), lambda p, i: (layer + p, 0, 0)), blk, blk],
                (blk, blk), vmem=VMEM_BIG)


def _ffn_bwd_weights(h, dy, act, dg, du):
    half, t, n = act.shape
    d = h.shape[1]
    tk = min(TMM, t)
    h_spec = pl.BlockSpec((tk, d), lambda r: (r, 0))
    g_spec = pl.BlockSpec((half, tk, n), lambda r: (0, r, 0))

    def in_terms(h_ref, g_ref):
        hv = h_ref[...]
        return [_dot(hv, g_ref[p], TN) for p in range(half)]

    def out_terms(a_ref, dy_ref):
        dyv = dy_ref[...]
        return [_dot(a_ref[p], dyv, TN) for p in range(half)]

    both = _sds((2 * half, d, n), GRAD_WIRE)
    dwi = _accumulate_over_tokens("ffn_dw_gate", (h, dg), [h_spec, g_spec], both, pl.BlockSpec((half, d, n), lambda r: (0, 0, 0)),
                                  (half, d, n), t // tk, in_terms)
    dwi = _accumulate_over_tokens("ffn_dw_up", (h, du), [h_spec, g_spec], both, pl.BlockSpec((half, d, n), lambda r: (1, 0, 0)),
                                  (half, d, n), t // tk, in_terms, fill=dwi)
    dwo = _accumulate_over_tokens("ffn_dw_out", (act, dy), [g_spec, h_spec], _sds((half, n, d), GRAD_WIRE),
                                  pl.BlockSpec((half, n, d), lambda r: (0, 0, 0)), (half, n, d), t // tk, out_terms)
    return dwi, dwo


def _ffn_dh(dg, du, w_in, layer, beside=None):
    half, t, n = dg.shape
    d = w_in.shape[2]
    tm = min(TM, t)

    def body(dg_ref, du_ref, w_ref, o_ref):
        acc = None
        for p in range(half):
            for ref, q in ((dg_ref, p), (du_ref, p + half)):
                term = _dot(ref[p], w_ref[q], NT)
                acc = term if acc is None else acc + term
        o_ref[...] = acc

    g_spec = pl.BlockSpec((half, tm, n), lambda i: (0, i, 0))
    return _run(beside, body, "ffn_dh", _sds((t, d), F32), (dg, du, w_in), (t // tm,),
                [g_spec, g_spec, pl.BlockSpec((None, 2 * half, d, n), lambda i: (layer, 0, 0, 0))],
                pl.BlockSpec((tm, d), lambda i: (i, 0)), vmem=VMEM_BIG)


def _tril(n, strict=False):
    r = lax.broadcasted_iota(jnp.int32, (n, n), 0)
    c = lax.broadcasted_iota(jnp.int32, (n, n), 1)
    return c < r if strict else c <= r


def _gm_spatial_fwd(pre, lng, lnb, w_s, b_st, seq, beside=None):
    t, w2 = pre.shape
    w = w2 // 2
    gd = w // GM_GROUPS
    tm = min(TM, seq)
    nch = tm // GM_CHUNK

    def body(pre_ref, lng_ref, lnb_ref, ws_ref, bs_ref, y_ref):
        v = _gelu(pre_ref[:, w:])
        vc, rstd = _ln_stats(v)
        vn = (vc * rstd * lng_ref[...] + lnb_ref[...]).astype(BF16)
        keep = _tril(GM_CHUNK)
        for g in range(GM_GROUPS):
            wm = jnp.where(keep, ws_ref[g], 0.0).astype(BF16)
            for ci in range(nch):
                rows = slice(ci * GM_CHUNK, (ci + 1) * GM_CHUNK)
                cols = slice(g * gd, (g + 1) * gd)
                sv = _dot(wm, vn[rows, cols]) + bs_ref[:, g:g + 1]
                u = _gelu(pre_ref[rows, cols])
                y_ref[rows, cols] = (u * sv).astype(BF16)

    return _run(beside, body, "gm_spatial_fwd", _sds((t, w), BF16), (pre, lng, lnb, w_s, b_st), (t // tm,),
                [_tile_spec(tm, w2), _vec_spec(w), _vec_spec(w),
                 pl.BlockSpec((GM_GROUPS, GM_CHUNK, GM_CHUNK), lambda i: (0, 0, 0)),
                 pl.BlockSpec((GM_CHUNK, GM_GROUPS), lambda i: (0, 0))],
                _tile_spec(tm, w), vmem=VMEM_BIG)


def _gm_spatial_bwd(pre, dyv, lng, lnb, w_s, b_st, seq, beside=None):
    t, w2 = pre.shape
    w = w2 // 2
    gd = w // GM_GROUPS
    tm = min(TM, seq)
    nch = tm // GM_CHUNK

    def body(pre_ref, dyv_ref, lng_ref, lnb_ref, ws_ref, bs_ref, dpre_ref, dws_ref, dbs_ref, dlg_ref, dlb_ref, dbin_ref, dvn_ref):
        @pl.when(pl.program_id(0) == 0)
        def _():
            dws_ref[...] = jnp.zeros_like(dws_ref)
            dbs_ref[...] = jnp.zeros_like(dbs_ref)
            dlg_ref[...] = jnp.zeros_like(dlg_ref)
            dlb_ref[...] = jnp.zeros_like(dlb_ref)
            dbin_ref[...] = jnp.zeros_like(dbin_ref)

        pv = pre_ref[:, w:]
        v = _gelu(pv)
        vc, rstd = _ln_stats(v)
        vhat = vc * rstd
        vn = (vhat * lng_ref[...] + lnb_ref[...]).astype(BF16)
        keep = _tril(GM_CHUNK)
        dbs_cols = []
        for g in range(GM_GROUPS):
            wm = jnp.where(keep, ws_ref[g], 0.0).astype(BF16)
            dwm = jnp.zeros((GM_CHUNK, GM_CHUNK), F32)
            dbs = jnp.zeros((GM_CHUNK, 1), F32)
            for ci in range(nch):
                rows = slice(ci * GM_CHUNK, (ci + 1) * GM_CHUNK)
                cols = slice(g * gd, (g + 1) * gd)
                vn_b = vn[rows, cols]
                sv = _dot(wm, vn_b) + bs_ref[:, g:g + 1]
                pu = pre_ref[rows, cols]
                dy = dyv_ref[rows, cols]
                du = dy * sv
                dsv = dy * _gelu(pu)
                dpu = du * _gelu_grad(pu)
                dpre_ref[rows, cols] = dpu.astype(BF16)
                dbin_ref[:, cols] += _colsum(dpu)
                dsv_b = dsv.astype(BF16)
                dwm = dwm + _dot(dsv_b, vn_b, NT)
                dbs = dbs + jnp.sum(dsv, axis=-1, keepdims=True)
                dvn_ref[rows, cols] = _dot(wm, dsv_b, TN)
            dws_ref[g] += jnp.where(keep, dwm, 0.0)
            dbs_cols.append(dbs)
        dbs_ref[...] += jnp.concatenate(dbs_cols, axis=1)
        dvn = dvn_ref[...]
        dlg_ref[...] += _colsum(dvn * vhat)
        dlb_ref[...] += _colsum(dvn)
        dvh = dvn * lng_ref[...]
        m1 = jnp.mean(dvh, axis=-1, keepdims=True)
        m2 = jnp.mean(dvh * vhat, axis=-1, keepdims=True)
        dv = rstd * (dvh - m1 - vhat * m2)
        dpv = dv * _gelu_grad(pv)
        dpre_ref[:, w:] = dpv.astype(BF16)
        dbin_ref[:, w:] += _colsum(dpv)

    full3 = pl.BlockSpec((GM_GROUPS, GM_CHUNK, GM_CHUNK), lambda i: (0, 0, 0))
    bst = pl.BlockSpec((GM_CHUNK, GM_GROUPS), lambda i: (0, 0))
    return _run(beside, body, "gm_spatial_bwd",
                (_sds((t, w2), BF16), _sds((GM_GROUPS, GM_CHUNK, GM_CHUNK), F32), _sds((GM_CHUNK, GM_GROUPS), F32),
                 _sds((1, w), F32), _sds((1, w), F32), _sds((1, w2), F32)),
                (pre, dyv, lng, lnb, w_s, b_st), (t // tm,),
                [_tile_spec(tm, w2), _tile_spec(tm, w), _vec_spec(w), _vec_spec(w), full3, bst],
                (_tile_spec(tm, w2), full3, bst, _vec_spec(w), _vec_spec(w), _vec_spec(w2)),
                scratch=[pltpu.VMEM((tm, w), F32)], vmem=VMEM_BIG)


def _head_masks():
    lane = lax.broadcasted_iota(jnp.int32, (1, LANES), 1)
    return lane < HEAD_DIM


def _two_heads(x, m0):
    z = jnp.zeros_like(x)
    return jnp.where(m0, x, z), jnp.where(m0, z, x)


def _transposed(pair):
    return tuple(x.astype(F32).T.astype(BF16) for x in pair)


def _store_transposed(dqkv_ref, dk_acc, dv_acc, nq, tq):
    for c in range(nq):
        cols = slice(c * tq, (c + 1) * tq)
        dqkv_ref[1, cols, :] = dk_acc[:, cols].T.astype(BF16)
        dqkv_ref[2, cols, :] = dv_acc[:, cols].T.astype(BF16)


def _qkv_specs(seq, nq, blocked_q):
    if blocked_q:
        q = pl.BlockSpec((None, TQ_(seq), LANES), lambda b, p, i: (0, b * nq + i, p))
        k = pl.BlockSpec((None, seq, LANES), lambda b, p, i: (1, b, p))
        v = pl.BlockSpec((None, seq, LANES), lambda b, p, i: (2, b, p))
    else:
        q = pl.BlockSpec((None, seq, LANES), lambda b, p: (0, b, p))
        k = pl.BlockSpec((None, seq, LANES), lambda b, p: (1, b, p))
        v = pl.BlockSpec((None, seq, LANES), lambda b, p: (2, b, p))
    return q, k, v


def TQ_(seq):
    return min(TQ, seq)


def _fox_gate_fwd(ft, b_f, seq):
    nh, t = ft.shape
    nch = seq // LANES

    def body(ft_ref, bf_ref, fr_ref):
        r = lax.broadcasted_iota(jnp.int32, (LANES, LANES), 0)
        c = lax.broadcasted_iota(jnp.int32, (LANES, LANES), 1)
        upper = jnp.where(r <= c, 1.0, 0.0).astype(BF16)
        carry = jnp.zeros((nh, 1), F32)
        for ci in range(nch):
            cols = slice(ci * LANES, (ci + 1) * LANES)
            lf = _log_sigmoid(ft_ref[:, cols] + bf_ref[...])
            cs = _dot_exact(lf, upper) + carry
            fr_ref[:, cols] = cs
            carry = cs[:, LANES - 1:LANES]

    return _call(body, "fox_gate_fwd", _sds((nh, t), F32), grid=(t // seq,),
                 in_specs=[pl.BlockSpec((nh, seq), lambda b: (0, b)), pl.BlockSpec((nh, 1), lambda b: (0, 0))],
                 out_specs=pl.BlockSpec((nh, seq), lambda b: (0, b)))(ft, b_f)


def _fox_gate_bwd(ft, b_f, dfk, dfq, seq):
    nh, t = ft.shape
    nch = seq // LANES

    def body(ft_ref, bf_ref, dfk_ref, dfq_ref, dl_ref, db_ref):
        @pl.when(pl.program_id(0) == 0)
        def _():
            db_ref[...] = jnp.zeros_like(db_ref)

        r = lax.broadcasted_iota(jnp.int32, (LANES, LANES), 0)
        c = lax.broadcasted_iota(jnp.int32, (LANES, LANES), 1)
        lower = jnp.where(r >= c, 1.0, 0.0).astype(BF16)
        carry = jnp.zeros((nh, 1), F32)
        tot = jnp.zeros((nh, 1), F32)
        for ci in reversed(range(nch)):
            cols = slice(ci * LANES, (ci + 1) * LANES)
            rc = _dot_exact(dfk_ref[:, cols] + dfq_ref[:, cols], lower) + carry
            carry = rc[:, 0:1]
            dl = rc * jax.nn.sigmoid(-(ft_ref[:, cols] + bf_ref[...]))
            dl_ref[:, cols] = dl
            tot = tot + jnp.sum(dl, axis=-1, keepdims=True)
        db_ref[...] += tot

    blk = pl.BlockSpec((nh, seq), lambda b: (0, b))
    one = pl.BlockSpec((nh, 1), lambda b: (0, 0))
    return _call(body, "fox_gate_bwd", (_sds((nh, t), F32), _sds((nh, 1), F32)), grid=(t // seq,),
                 in_specs=[blk, one, blk, blk], out_specs=(blk, one))(ft, b_f, dfk, dfq)


def _sweep(step, n_off, unroll, init, start=0):
    def group(_, st):
        base, carry = st[0], st[1:]
        for u in range(unroll):
            carry = step(base + u, carry, False)
        return (base + unroll, *carry)

    def tail(r):
        def run(st):
            base, carry = st[0], st[1:]
            for u in range(r):
                carry = step(base + u, carry, False)
            return step(base + r, carry, True)
        return run

    def pick(idx, fns, st):
        if len(fns) == 1:
            return fns[0](st)
        half = len(fns) // 2
        return lax.cond(idx < half, lambda s: pick(idx, fns[:half], s), lambda s: pick(idx - half, fns[half:], s), st)

    st = lax.fori_loop(0, n_off // unroll, group, (jnp.int32(0) + start, *init))
    return pick(n_off % unroll, [tail(r) for r in range(unroll)], st)


def _carried(n_comm, n_in, n_out, refs):
    own_in, send = refs[:n_in], refs[n_in:n_in + n_comm]
    rest = refs[n_in + n_comm:]
    n_sem = 3 if n_comm else 0
    return own_in, send, rest[:n_out], rest[n_out:n_out + n_comm], rest[n_out + n_comm:n_out + n_comm + n_sem], rest[n_out + n_comm + n_sem:]


def _fox_fwd(qkv, frow, nb, seq, gather=()):
    _, t, d = qkv.shape
    npair = d // LANES
    tq = TQ_(seq)
    nq = seq // tq
    scale = HEAD_DIM ** -0.5
    n_comm = len(gather)
    total = nb * npair * nq

    def body(*refs):
        (q_ref, k_ref, v_ref, fr_ref), send, (o_ref, lse_ref), land, sems, _ = _carried(n_comm, 4, 2, refs)
        at = (pl.program_id(0) * npair + pl.program_id(1)) * nq + pl.program_id(2)
        if n_comm:
            for stage, when in ((0, 0), (1, (3 * total) // 4)):
                @pl.when(at == when)
                def _(stage=stage):
                    _gather_stage(stage, send, land, *sems)

        m0 = _head_masks()
        qm = _two_heads(q_ref[...] * scale, m0)
        row = lax.broadcasted_iota(jnp.int32, (tq, tq), 0)
        col = lax.broadcasted_iota(jnp.int32, (tq, tq), 1)
        one = jnp.ones((tq, LANES), BF16)

        def step(j, carry, diag):
            off = pl.multiple_of(j * tq, tq)
            kb = k_ref[pl.ds(off, tq), :]
            vb = v_ref[pl.ds(off, tq), :]
            vv = (jnp.where(m0, vb, one), jnp.where(m0, one, vb))
            out = []
            for hh in range(2):
                m, acc = carry[2 * hh], carry[2 * hh + 1]
                s = lax.dot_general(qm[hh], kb, NT, preferred_element_type=F32) - fr_ref[hh:hh + 1, pl.ds(off, tq)]
                if diag:
                    s = jnp.where(col <= row, s, NEG_INF)
                mn = jnp.maximum(m, jnp.max(s, axis=-1, keepdims=True))
                p = jnp.exp(s - mn)
                out += [mn, jnp.exp(m - mn) * acc + jnp.dot(p.astype(BF16), vv[hh], preferred_element_type=F32)]
            return tuple(out)

        neg = jnp.full((tq, 1), NEG_INF, F32)
        zacc = jnp.zeros((tq, LANES), F32)
        m_a, acc_a, m_b, acc_b = _sweep(step, pl.program_id(2), FOX_FWD_UNROLL, (neg, zacc, neg, zacc))
        l_a = pltpu.roll(acc_a, HEAD_DIM, 1)
        l_b = pltpu.roll(acc_b, HEAD_DIM, 1)
        o_ref[...] = jnp.where(m0, acc_a / l_a, acc_b / l_b)
        lse_ref[:, 0:1] = m_a + jnp.log(l_a[:, 0:1])
        lse_ref[:, 1:2] = m_b + jnp.log(l_b[:, HEAD_DIM:HEAD_DIM + 1])
        if n_comm:
            @pl.when(at == total - 1)
            def _():
                _gather_stage(2, send, land, *sems)

    q_spec, k_spec, v_spec = _qkv_specs(seq, nq, True)
    col_spec = pl.BlockSpec((None, tq, 2), lambda b, p, i: (p, b * nq + i, 0))
    hbm = pl.BlockSpec(memory_space=pl.ANY)
    return _call(body, "fox_fwd", (_sds((t, d), F32), _sds((npair, t, 2), F32), *_gathered_shapes(gather)), grid=(nb, npair, nq),
                 in_specs=[q_spec, k_spec, v_spec, pl.BlockSpec((None, 2, seq), lambda b, p, i: (p, 0, b))] + [hbm] * n_comm,
                 out_specs=(pl.BlockSpec((tq, LANES), lambda b, p, i: (b * nq + i, p)), col_spec, *[hbm] * n_comm),
                 scratch=_comm_scratch(n_comm) if n_comm else (), vmem=VMEM_BIG)(qkv, qkv, qkv, frow, *gather)


def _scatter_beside(stage, n_comm, nb, npair, send, land, sems):
    if n_comm:
        at = pl.program_id(0) * npair + pl.program_id(1)

        @pl.when(at == (0 if stage == 0 else nb * npair - 1))
        def _():
            _scatter_stage(stage, send, land, *sems)


def _fox_bwd(qkv, frow, o, do, lse, nb, seq, scatter=()):
    _, t, d = qkv.shape
    npair = d // LANES
    tq = TQ_(seq)
    nq = seq // tq
    scale = HEAD_DIM ** -0.5
    n_comm = len(scatter)

    def body(*refs):
        own_in, send, (dqkv_ref, df_ref, dfq_ref), land, sems, (dk_acc, dv_acc) = _carried(n_comm, 7, 3, refs)
        q_ref, k_ref, v_ref, fr_ref, o_ref, do_ref, lse_ref = own_in
        _scatter_beside(0, n_comm, nb, npair, send, land, sems)
        m0 = _head_masks()
        row = lax.broadcasted_iota(jnp.int32, (tq, tq), 0)
        col = lax.broadcasted_iota(jnp.int32, (tq, tq), 1)
        dk_acc[...] = jnp.zeros_like(dk_acc)
        dv_acc[...] = jnp.zeros_like(dv_acc)
        df_ref[...] = jnp.zeros_like(df_ref)

        def q_block(i, _):
            qoff = pl.multiple_of(i * tq, tq)
            qrows = pl.ds(qoff, tq)
            qm = _two_heads(q_ref[qrows, :] * scale, m0)
            dov = do_ref[qrows, :]
            dd = dov * o_ref[qrows, :]
            dm = _two_heads(dov.astype(BF16), m0)
            qmt = _transposed(qm)
            dmt = _transposed(dm)
            delta = (jnp.sum(jnp.where(m0, dd, 0.0), axis=-1, keepdims=True),
                     jnp.sum(jnp.where(m0, 0.0, dd), axis=-1, keepdims=True))
            ls = (lse_ref[qrows, 0:1], lse_ref[qrows, 1:2])

            def step(j, carry, diag):
                off = pl.multiple_of(j * tq, tq)
                krows = pl.ds(off, tq)
                kb = k_ref[krows, :]
                vb = v_ref[krows, :]
                dqs, rowsums = [], []
                dk = jnp.zeros((LANES, tq), F32)
                dv = jnp.zeros((LANES, tq), F32)
                for hh in range(2):
                    s = lax.dot_general(qm[hh], kb, NT, preferred_element_type=F32) - fr_ref[hh:hh + 1, krows]
                    if diag:
                        s = jnp.where(col <= row, s, NEG_INF)
                    p = jnp.exp(s - ls[hh])
                    dp = lax.dot_general(dm[hh], vb, NT, preferred_element_type=F32)
                    ds = p * (dp - delta[hh])
                    df_ref[hh:hh + 1, krows] -= _colsum(ds)
                    rowsums.append(carry[1 + hh] + jnp.sum(ds, axis=-1, keepdims=True))
                    ds_b = ds.astype(BF16)
                    dqs.append(jnp.dot(ds_b, kb, preferred_element_type=F32))
                    dk = dk + jnp.dot(qmt[hh], ds_b, preferred_element_type=F32)
                    dv = dv + jnp.dot(dmt[hh], p.astype(BF16), preferred_element_type=F32)
                dk_acc[:, krows] += dk
                dv_acc[:, krows] += dv
                return (carry[0] + jnp.where(m0, dqs[0], dqs[1]), *rowsums)

            zero = jnp.zeros((tq, 1), F32)
            dq, rs_a, rs_b = _sweep(step, i, FOX_BWD_UNROLL, (jnp.zeros((tq, LANES), F32), zero, zero))
            dqkv_ref[0, qrows, :] = (dq * scale).astype(BF16)
            dfq_ref[qrows, 0:1] = rs_a
            dfq_ref[qrows, 1:2] = rs_b
            return 0

        lax.fori_loop(0, nq, q_block, 0)
        _store_transposed(dqkv_ref, dk_acc, dv_acc, nq, tq)
        _scatter_beside(1, n_comm, nb, npair, send, land, sems)

    q_spec, k_spec, v_spec = _qkv_specs(seq, nq, False)
    col_spec = pl.BlockSpec((None, seq, 2), lambda b, p: (p, b, 0))
    row_spec = pl.BlockSpec((None, 2, seq), lambda b, p: (p, 0, b))
    tile = pl.BlockSpec((seq, LANES), lambda b, p: (b, p))
    hbm = pl.BlockSpec(memory_space=pl.ANY)
    return _call(body, "fox_bwd", (_sds((3, t, d), BF16), _sds((npair, 2, t), F32), _sds((npair, t, 2), F32), *_scattered_shapes(scatter)),
                 grid=(nb, npair),
                 in_specs=[q_spec, k_spec, v_spec, row_spec, tile, tile, col_spec] + [hbm] * n_comm,
                 out_specs=(pl.BlockSpec((3, seq, LANES), lambda b, p: (0, b, p)), row_spec, col_spec, *[hbm] * n_comm),
                 scratch=(_comm_scratch(n_comm) if n_comm else []) + [pltpu.VMEM((LANES, seq), F32), pltpu.VMEM((LANES, seq), F32)],
                 vmem=VMEM_BIG)(qkv, qkv, qkv, frow, o, do, lse, *scatter)


def _split2(x):
    hi = x.astype(BF16)
    return hi, (x - hi.astype(F32)).astype(BF16)


def _sum_right(x, tri):
    hi, lo = _split2(x)
    return jnp.dot(hi, tri, preferred_element_type=F32) + jnp.dot(lo, tri, preferred_element_type=F32)


def _sb_scores(qm_h, kb, mask):
    z = lax.dot_general(qm_h, kb, NT, preferred_element_type=F32)
    lb = _log_sigmoid(z)
    l1m = lb - z
    if mask is not None:
        l1m = jnp.where(mask, l1m, 0.0)
    return lb, l1m


def _sb_fwd(qkv, nb, seq):
    _, t, d = qkv.shape
    npair = d // LANES
    tq = TQ_(seq)
    nq = seq // tq
    scale = HEAD_DIM ** -0.5

    def body(q_ref, k_ref, v_ref, o_ref, lt_ref):
        i = pl.program_id(2)
        m0 = _head_masks()
        qm = _two_heads(q_ref[...] * scale, m0)
        row = lax.broadcasted_iota(jnp.int32, (tq, tq), 0)
        col = lax.broadcasted_iota(jnp.int32, (tq, tq), 1)
        after = jnp.where(row > col, 1.0, 0.0).astype(BF16)

        def step(j, carry, diag):
            off = pl.multiple_of(j * tq, tq)
            kb = k_ref[pl.ds(off, tq), :]
            vb = v_ref[pl.ds(off, tq), :]
            mask = (col < row) if diag else None
            nxt, parts = [], []
            for hh in range(2):
                lb, l1m = _sb_scores(qm[hh], kb, mask)
                rest = _sum_right(l1m, after) + carry[hh]
                a = jnp.exp(lb + rest)
                if diag:
                    a = jnp.where(mask, a, 0.0)
                parts.append(jnp.dot(a.astype(BF16), vb, preferred_element_type=F32))
                nxt.append(carry[hh] + jnp.sum(l1m, axis=-1, keepdims=True))
            return (*nxt, carry[2] + jnp.where(m0, parts[0], parts[1]))

        zero = jnp.zeros((tq, 1), F32)
        init = (zero, zero, jnp.zeros((tq, LANES), F32))
        carry = lax.cond(i > 0, lambda c: step(i - 1, step(i, c, True), False), lambda c: step(i, c, True), init)

        def alive(st):
            return (st[0] < i) & (jnp.max(jnp.maximum(st[1], st[2])) > SB_DEAD)

        def more(st):
            return (st[0] + 1, *step(i - 1 - st[0], st[1:], False))

        done, lt_a, lt_b, acc = lax.while_loop(alive, more, (jnp.minimum(i, 1), *carry))
        o_ref[...] = acc
        lt_ref[:, 0:1] = lt_a
        lt_ref[:, 1:2] = lt_b
        lt_ref[:, 2:3] = jnp.zeros((tq, 1), F32) + done.astype(F32)
        lt_ref[:, 3:4] = zero

    q_spec, k_spec, v_spec = _qkv_specs(seq, nq, True)
    return _call(body, "sb_fwd", (_sds((t, d), F32), _sds((npair, t, 4), F32)), grid=(nb, npair, nq),
                 in_specs=[q_spec, k_spec, v_spec],
                 out_specs=(pl.BlockSpec((tq, LANES), lambda b, p, i: (b * nq + i, p)),
                            pl.BlockSpec((None, tq, 4), lambda b, p, i: (p, b * nq + i, 0))), vmem=VMEM_BIG)(qkv, qkv, qkv)


def _sb_bwd(qkv, do, ltot, nb, seq, scatter=()):
    _, t, d = qkv.shape
    npair = d // LANES
    tq = TQ_(seq)
    nq = seq // tq
    scale = HEAD_DIM ** -0.5
    n_comm = len(scatter)

    def body(*refs):
        (q_ref, k_ref, v_ref, do_ref, lt_ref), send, (dqkv_ref,), land, sems, (dk_acc, dv_acc) = _carried(n_comm, 5, 1, refs)
        _scatter_beside(0, n_comm, nb, npair, send, land, sems)
        m0 = _head_masks()
        row = lax.broadcasted_iota(jnp.int32, (tq, tq), 0)
        col = lax.broadcasted_iota(jnp.int32, (tq, tq), 1)
        upto = jnp.where(row <= col, 1.0, 0.0).astype(BF16)
        left_of = jnp.where(row < col, 1.0, 0.0).astype(BF16)
        dk_acc[...] = jnp.zeros_like(dk_acc)
        dv_acc[...] = jnp.zeros_like(dv_acc)

        def q_block(i, _):
            qoff = pl.multiple_of(i * tq, tq)
            qrows = pl.ds(qoff, tq)
            qm = _two_heads(q_ref[qrows, :] * scale, m0)
            dm = _two_heads(do_ref[qrows, :].astype(BF16), m0)
            qmt = _transposed(qm)
            dmt = _transposed(dm)
            ltot = (lt_ref[qrows, 0:1], lt_ref[qrows, 1:2])

            def step(j, carry, diag):
                off = pl.multiple_of(j * tq, tq)
                krows = pl.ds(off, tq)
                kb = k_ref[krows, :]
                vb = v_ref[krows, :]
                mask = (col < row) if diag else None
                nxt, dqs = [], []
                dk = jnp.zeros((LANES, tq), F32)
                dv = jnp.zeros((LANES, tq), F32)
                for hh in range(2):
                    cl, ce = carry[2 * hh], carry[2 * hh + 1]
                    lb, l1m = _sb_scores(qm[hh], kb, mask)
                    a = jnp.exp(lb + (ltot[hh] - (_sum_right(l1m, upto) + cl)))
                    if diag:
                        a = jnp.where(mask, a, 0.0)
                    e = lax.dot_general(dm[hh], vb, NT, preferred_element_type=F32) * a
                    before = _sum_right(e, left_of) + ce
                    beta = jnp.exp(lb)
                    dz = e * (1.0 - beta) - before * beta
                    if diag:
                        dz = jnp.where(mask, dz, 0.0)
                    dz_b = dz.astype(BF16)
                    dqs.append(jnp.dot(dz_b, kb, preferred_element_type=F32))
                    dk = dk + jnp.dot(qmt[hh], dz_b, preferred_element_type=F32)
                    dv = dv + jnp.dot(dmt[hh], a.astype(BF16), preferred_element_type=F32)
                    nxt += [cl + jnp.sum(l1m, axis=-1, keepdims=True), ce + jnp.sum(e, axis=-1, keepdims=True)]
                dk_acc[:, krows] += dk
                dv_acc[:, krows] += dv
                return (*nxt, carry[4] + jnp.where(m0, dqs[0], dqs[1]))

            zero = jnp.zeros((tq, 1), F32)
            visited = jnp.max(lt_ref[qrows, 2:3]).astype(jnp.int32)
            carry = _sweep(step, visited, SB_BWD_UNROLL, (zero, zero, zero, zero, jnp.zeros((tq, LANES), F32)), start=i - visited)
            dqkv_ref[0, qrows, :] = (carry[4] * scale).astype(BF16)
            return 0

        lax.fori_loop(0, nq, q_block, 0)
        _store_transposed(dqkv_ref, dk_acc, dv_acc, nq, tq)
        _scatter_beside(1, n_comm, nb, npair, send, land, sems)

    q_spec, k_spec, v_spec = _qkv_specs(seq, nq, False)
    tile = pl.BlockSpec((seq, LANES), lambda b, p: (b, p))
    hbm = pl.BlockSpec(memory_space=pl.ANY)
    return _call(body, "sb_bwd", (_sds((3, t, d), BF16), *_scattered_shapes(scatter)), grid=(nb, npair),
                 in_specs=[q_spec, k_spec, v_spec, tile, pl.BlockSpec((None, seq, 4), lambda b, p: (p, b, 0))] + [hbm] * n_comm,
                 out_specs=(pl.BlockSpec((3, seq, LANES), lambda b, p: (0, b, p)), *[hbm] * n_comm),
                 scratch=(_comm_scratch(n_comm) if n_comm else []) + [pltpu.VMEM((LANES, seq), F32), pltpu.VMEM((LANES, seq), F32)],
                 vmem=VMEM_BIG)(qkv, qkv, qkv, do, ltot, *scatter)


def _qkv_proj(name, h, w):
    t, d = h.shape
    tm = min(TMM, t)
    return _mm(name, h, w, _sds((3, t, d), BF16), (3, t // tm, 1),
               pl.BlockSpec((tm, d), lambda s, i, r: (i, 0)), pl.BlockSpec((d, d), lambda s, i, r: (0, s)),
               pl.BlockSpec((None, tm, d), lambda s, i, r: (s, i, 0)), NN)


def _qkv_dw(name, h, dqkv):
    t, d = h.shape
    tk = min(TMM, t)

    def terms(h_ref, g_ref):
        hv = h_ref[...]
        return [_dot(hv, g_ref[s], TN) for s in range(3)]

    def store(o_ref, acc_ref):
        for s in range(3):
            o_ref[:, s * d:(s + 1) * d] = acc_ref[s].astype(o_ref.dtype)

    return _accumulate_over_tokens(name, (h, dqkv), [pl.BlockSpec((tk, d), lambda r: (r, 0)), pl.BlockSpec((3, tk, d), lambda r: (0, r, 0))],
                                   _sds((d, 3 * d), GRAD_WIRE), pl.BlockSpec((d, 3 * d), lambda r: (0, 0)), (3, d, d), t // tk, terms, store)


def _qkv_dh(name, dqkv, w):
    _, t, d = dqkv.shape
    tm = min(TM, t)

    def body(g_ref, w_ref, o_ref):
        acc = None
        for s in range(3):
            term = _dot(g_ref[s], w_ref[:, s * d:(s + 1) * d], NT)
            acc = term if acc is None else acc + term
        o_ref[...] = acc

    return _call(body, name, _sds((t, d), F32), grid=(t // tm,),
                 in_specs=[pl.BlockSpec((3, tm, d), lambda i: (0, i, 0)), pl.BlockSpec((d, 3 * d), lambda i: (0, 0))],
                 out_specs=pl.BlockSpec((tm, d), lambda i: (i, 0)), vmem=VMEM_BIG)(dqkv, w)


def _glu(pre_block, d):
    return pre_block[:, :d] * jax.nn.sigmoid(pre_block[:, d:])


def _shifted_copies(ext_ref, sh_ref, tt):
    for r in range(1, SUBLANES):
        sh_ref[r - 1] = ext_ref[pl.ds(r, tt + CONV_HALO - SUBLANES), :]


def _rows_from(ext_ref, sh_ref, base, offset, n):
    q, r = divmod(offset, SUBLANES)
    if r == 0:
        return ext_ref[pl.ds(pl.multiple_of(base + offset, SUBLANES), n), :]
    return sh_ref[r - 1, pl.ds(pl.multiple_of(base + q * SUBLANES, SUBLANES), n), :]


def _cv_conv_fwd(pre, dw, dwb, lng, lnb, seq):
    t, d2 = pre.shape
    d = d2 // 2
    tt = min(TT, seq)
    nt = seq // tt
    hb = tt // CONV_HALO

    def body(pre_ref, halo_ref, dw_ref, dwb_ref, lng_ref, lnb_ref, y1_ref, y2_ref, ext_ref, sh_ref):
        i = pl.program_id(1)
        ext_ref[0:CONV_HALO, :] = jnp.where(i == 0, 0.0, _glu(halo_ref[...], d))
        ext_ref[CONV_HALO:, :] = _glu(pre_ref[...], d)
        _shifted_copies(ext_ref, sh_ref, tt)

        acc = jnp.zeros((tt, d), F32) + dwb_ref[...]
        for k in range(CONV_WIDTH):
            acc = acc + _rows_from(ext_ref, sh_ref, 0, CONV_HALO - (CONV_WIDTH - 1) + k, tt) * dw_ref[k:k + 1, :]
        y1_ref[...] = acc
        yc, rstd = _ln_stats(acc)
        y2_ref[...] = _silu(yc * rstd * lng_ref[...] + lnb_ref[...]).astype(BF16)

    vec = pl.BlockSpec((1, d), lambda b, i: (0, 0))
    tile = pl.BlockSpec((tt, d), lambda b, i: (b * nt + i, 0))
    return _call(body, "cv_conv_fwd", (_sds((t, d), F32), _sds((t, d), BF16)), grid=(t // seq, nt),
                 in_specs=[pl.BlockSpec((tt, d2), lambda b, i: (b * nt + i, 0)),
                           pl.BlockSpec((CONV_HALO, d2), lambda b, i: (jnp.maximum((b * nt + i) * hb - 1, 0), 0)),
                           pl.BlockSpec((CONV_HALO, d), lambda b, i: (0, 0)), vec, vec, vec],
                 out_specs=(tile, tile),
                 scratch=[pltpu.VMEM((tt + CONV_HALO, d), F32), pltpu.VMEM((SUBLANES - 1, tt + CONV_HALO - SUBLANES, d), F32)],
                 vmem=VMEM_BIG)(pre, pre, dw, dwb, lng, lnb)


def _cv_norm_bwd(y1, dy2, lng, lnb):
    t, d = y1.shape
    tm = min(TM, t)

    def body(y1_ref, dy2_ref, lng_ref, lnb_ref, dy1_ref, dlg_ref, dlb_ref, dsum_ref):
        @pl.when(pl.program_id(0) == 0)
        def _():
            dlg_ref[...] = jnp.zeros_like(dlg_ref)
            dlb_ref[...] = jnp.zeros_like(dlb_ref)
            dsum_ref[...] = jnp.zeros_like(dsum_ref)

        yc, rstd = _ln_stats(y1_ref[...])
        yhat = yc * rstd
        n = yhat * lng_ref[...] + lnb_ref[...]
        sg = jax.nn.sigmoid(n)
        dn = dy2_ref[...] * (sg * (1.0 + n * (1.0 - sg)))
        dlg_ref[...] += _colsum(dn * yhat)
        dlb_ref[...] += _colsum(dn)
        dyh = dn * lng_ref[...]
        m1 = jnp.mean(dyh, axis=-1, keepdims=True)
        m2 = jnp.mean(dyh * yhat, axis=-1, keepdims=True)
        dy1 = rstd * (dyh - m1 - yhat * m2)
        dy1_ref[...] = dy1
        dsum_ref[...] += _colsum(dy1)

    return _call(body, "cv_norm_bwd", (_sds((t, d), F32), _sds((1, d), F32), _sds((1, d), F32), _sds((1, d), F32)),
                 grid=(t // tm,), in_specs=[_tile_spec(tm, d), _tile_spec(tm, d), _vec_spec(d), _vec_spec(d)],
                 out_specs=(_tile_spec(tm, d), _vec_spec(d), _vec_spec(d), _vec_spec(d)))(y1, dy2, lng, lnb)


def _cv_conv_bwd(pre, dy1, dw, seq):
    t, d2 = pre.shape
    d = d2 // 2
    tt = min(TT // 2, seq)
    nt = seq // tt
    hb = tt // CONV_HALO
    last_halo = t // CONV_HALO - 1

    def body(pre_ref, halo_ref, dy_ref, dyn_ref, dw_ref, dpre_ref, ddw_ref, dbin_ref, ext_ref, dext_ref, sh_ref, dsh_ref, taps_ref):
        b, i = pl.program_id(0), pl.program_id(1)

        @pl.when((b == 0) & (i == 0))
        def _():
            ddw_ref[...] = jnp.zeros_like(ddw_ref)
            dbin_ref[...] = jnp.zeros_like(dbin_ref)

        pv = pre_ref[...]
        ext_ref[0:CONV_HALO, :] = jnp.where(i == 0, 0.0, _glu(halo_ref[...], d))
        ext_ref[CONV_HALO:, :] = _glu(pv, d)
        dyv = dy_ref[...]
        dext_ref[0:tt, :] = dyv
        dext_ref[tt:, :] = jnp.where(i == nt - 1, 0.0, dyn_ref[...])
        _shifted_copies(ext_ref, sh_ref, tt)
        _shifted_copies(dext_ref, dsh_ref, tt)
        nrows = tt // CONV_ROWS

        def input_grad(c, _):
            r0 = pl.multiple_of(c * CONV_ROWS, CONV_ROWS)
            dy0 = jnp.zeros((CONV_ROWS, d), F32)
            for k in range(CONV_WIDTH):
                dy0 = dy0 + _rows_from(dext_ref, dsh_ref, r0, CONV_WIDTH - 1 - k, CONV_ROWS) * dw_ref[k:k + 1, :]
            rows = pl.ds(r0, CONV_ROWS)
            a = pre_ref[rows, :d]
            sg = jax.nn.sigmoid(pre_ref[rows, d:])
            da = dy0 * sg
            dg = dy0 * a * sg * (1.0 - sg)
            dpre_ref[rows, :d] = da.astype(BF16)
            dpre_ref[rows, d:] = dg.astype(BF16)
            dbin_ref[:, :d] += _colsum(da)
            dbin_ref[:, d:] += _colsum(dg)
            return 0

        lax.fori_loop(0, nrows, input_grad, 0)

        @pl.when((b == 0) & (i == 0))
        def _():
            taps_ref[...] = jnp.zeros_like(taps_ref)

        def tap_grads(c, _):
            r0 = pl.multiple_of(c * CONV_ROWS, CONV_ROWS)
            dyc = dy_ref[pl.ds(r0, CONV_ROWS), :]
            for k in range(CONV_WIDTH):
                prod = dyc * _rows_from(ext_ref, sh_ref, r0, CONV_HALO - (CONV_WIDTH - 1) + k, CONV_ROWS)
                taps_ref[k] += jnp.sum(prod.reshape(CONV_ROWS // SUBLANES, SUBLANES, d), axis=0)
            return 0

        lax.fori_loop(0, nrows, tap_grads, 0)

        @pl.when((b == t // seq - 1) & (i == nt - 1))
        def _():
            for k in range(CONV_WIDTH):
                ddw_ref[k:k + 1, :] = _colsum(taps_ref[k])

    return _call(body, "cv_conv_bwd", (_sds((t, d2), BF16), _sds((CONV_HALO, d), F32), _sds((1, d2), F32)), grid=(t // seq, nt),
                 in_specs=[pl.BlockSpec((tt, d2), lambda b, i: (b * nt + i, 0)),
                           pl.BlockSpec((CONV_HALO, d2), lambda b, i: (jnp.maximum((b * nt + i) * hb - 1, 0), 0)),
                           pl.BlockSpec((tt, d), lambda b, i: (b * nt + i, 0)),
                           pl.BlockSpec((CONV_HALO, d), lambda b, i: (jnp.minimum((b * nt + i + 1) * hb, last_halo), 0)),
                           pl.BlockSpec((CONV_HALO, d), lambda b, i: (0, 0))],
                 out_specs=(pl.BlockSpec((tt, d2), lambda b, i: (b * nt + i, 0)),
                            pl.BlockSpec((CONV_HALO, d), lambda b, i: (0, 0)), pl.BlockSpec((1, d2), lambda b, i: (0, 0))),
                 scratch=[pltpu.VMEM((tt + CONV_HALO, d), F32), pltpu.VMEM((tt + CONV_HALO, d), F32),
                          pltpu.VMEM((SUBLANES - 1, tt + CONV_HALO - SUBLANES, d), F32),
                          pltpu.VMEM((SUBLANES - 1, tt + CONV_HALO - SUBLANES, d), F32),
                          pltpu.VMEM((CONV_HALO, SUBLANES, d), F32)],
                 vmem=VMEM_BIG)(pre, pre, dy1, dy1, dw)


def _adamw(name, w, m, v, g=None, parts=None):
    rows, cols = w.shape
    nseg = 1 if parts is None else len(parts)
    seg_rows = rows // nseg
    tr = seg_rows
    for cand in ((512,) if parts is None else ()) + (256, 128, 64, 32, 16, 8):
        if seg_rows % cand == 0 and seg_rows > cand:
            tr = cand
            break
    tps = seg_rows // tr
    bc1 = 1.0 - ADAM_B1 ** ADAM_STEP
    bc2 = 1.0 - ADAM_B2 ** ADAM_STEP

    def body(w_ref, m_ref, v_ref, *rest):
        g_refs, (go_ref, d_ref, mo_ref, vo_ref) = rest[:nseg], rest[nseg:]

        def update(gv):
            mn = ADAM_B1 * m_ref[...] + (1.0 - ADAM_B1) * gv
            vn = ADAM_B2 * v_ref[...] + (1.0 - ADAM_B2) * (gv * gv)
            m_hat = mn / bc1
            v_hat = vn / bc2
            go_ref[...] = gv
            d_ref[...] = -ADAM_LR * (m_hat / (jnp.sqrt(v_hat) + ADAM_EPS) + ADAM_WD * w_ref[...])
            mo_ref[...] = mn
            vo_ref[...] = vn

        if parts is None:
            update(g_refs[0][...])
        else:
            for k, g_ref in enumerate(g_refs):
                @pl.when(pl.program_id(0) == k)
                def _(g_ref=g_ref):
                    gv = g_ref[0].astype(F32)
                    for s in range(1, N_DEV):
                        gv = gv + g_ref[s].astype(F32)
                    update(gv)

    blk = pl.BlockSpec((tr, cols), lambda l, i: (l * tps + i, 0))
    if parts is None:
        g_ins, g_specs = [g], [blk]
    else:
        g_ins = list(parts)
        g_specs = [pl.BlockSpec((N_DEV, tr, cols), (lambda l, i, k=k: (0, jnp.where(l == k, i, 0), 0))) for k in range(nseg)]
    out = _sds((rows, cols), F32)
    return _call(body, name, (out, out, out, out), grid=(nseg, tps), in_specs=[blk, blk, blk, *g_specs],
                 out_specs=(blk, blk, blk, blk), vmem=VMEM_BIG)(w, m, v, *g_ins)


def _pad_rows(a, rows):
    return jnp.pad(a, ((0, rows - a.shape[0]), (0, 0)))


def _full_cols(gathered, n):
    k = gathered.shape[2]
    return jnp.transpose(gathered[0], (1, 0, 2)).reshape(k, N_DEV * n)


def _col_blocks(full, n):
    k = full.shape[0]
    return jnp.transpose(full.reshape(k, N_DEV, n), (1, 0, 2))[None]


def kernel(x, c, mod_w, mod_b, ln1_g, ln1_b, ln2_g, ln2_b, ffn_w_in, ffn_w_out, gm_w_in, gm_b_in, gm_ln_g, gm_ln_b, gm_w_s, gm_b_s, gm_w_out, fox_w_in, fox_b_f, fox_w_out, sb_w_in, sb_w_out, cv_w_in, cv_b_in, cv_dw, cv_dw_b, cv_ln_g, cv_ln_b, cv_w_out, cv_b_out, loss_target, m_mod_w, m_mod_b, m_ln1_g, m_ln1_b, m_ln2_g, m_ln2_b, m_ffn_w_in, m_ffn_w_out, m_gm_w_in, m_gm_b_in, m_gm_ln_g, m_gm_ln_b, m_gm_w_s, m_gm_b_s, m_gm_w_out, m_fox_w_in, m_fox_b_f, m_fox_w_out, m_sb_w_in, m_sb_w_out, m_cv_w_in, m_cv_b_in, m_cv_dw, m_cv_dw_b, m_cv_ln_g, m_cv_ln_b, m_cv_w_out, m_cv_b_out, v_mod_w, v_mod_b, v_ln1_g, v_ln1_b, v_ln2_g, v_ln2_b, v_ffn_w_in, v_ffn_w_out, v_gm_w_in, v_gm_b_in, v_gm_ln_g, v_gm_ln_b, v_gm_w_s, v_gm_b_s, v_gm_w_out, v_fox_w_in, v_fox_b_f, v_fox_w_out, v_sb_w_in, v_sb_w_out, v_cv_w_in, v_cv_b_in, v_cv_dw, v_cv_dw_b, v_cv_ln_g, v_cv_ln_b, v_cv_w_out, v_cv_b_out):
    weights = dict(mod_w=mod_w, mod_b=mod_b, ln1_g=ln1_g, ln1_b=ln1_b, ln2_g=ln2_g, ln2_b=ln2_b, ffn_w_in=ffn_w_in, ffn_w_out=ffn_w_out, gm_w_in=gm_w_in, gm_b_in=gm_b_in, gm_ln_g=gm_ln_g, gm_ln_b=gm_ln_b, gm_w_s=gm_w_s, gm_b_s=gm_b_s, gm_w_out=gm_w_out, fox_w_in=fox_w_in, fox_b_f=fox_b_f, fox_w_out=fox_w_out, sb_w_in=sb_w_in, sb_w_out=sb_w_out, cv_w_in=cv_w_in, cv_b_in=cv_b_in, cv_dw=cv_dw, cv_dw_b=cv_dw_b, cv_ln_g=cv_ln_g, cv_ln_b=cv_ln_b, cv_w_out=cv_w_out, cv_b_out=cv_b_out)
    mom1 = dict(mod_w=m_mod_w, mod_b=m_mod_b, ln1_g=m_ln1_g, ln1_b=m_ln1_b, ln2_g=m_ln2_g, ln2_b=m_ln2_b, ffn_w_in=m_ffn_w_in, ffn_w_out=m_ffn_w_out, gm_w_in=m_gm_w_in, gm_b_in=m_gm_b_in, gm_ln_g=m_gm_ln_g, gm_ln_b=m_gm_ln_b, gm_w_s=m_gm_w_s, gm_b_s=m_gm_b_s, gm_w_out=m_gm_w_out, fox_w_in=m_fox_w_in, fox_b_f=m_fox_b_f, fox_w_out=m_fox_w_out, sb_w_in=m_sb_w_in, sb_w_out=m_sb_w_out, cv_w_in=m_cv_w_in, cv_b_in=m_cv_b_in, cv_dw=m_cv_dw, cv_dw_b=m_cv_dw_b, cv_ln_g=m_cv_ln_g, cv_ln_b=m_cv_ln_b, cv_w_out=m_cv_w_out, cv_b_out=m_cv_b_out)
    mom2 = dict(mod_w=v_mod_w, mod_b=v_mod_b, ln1_g=v_ln1_g, ln1_b=v_ln1_b, ln2_g=v_ln2_g, ln2_b=v_ln2_b, ffn_w_in=v_ffn_w_in, ffn_w_out=v_ffn_w_out, gm_w_in=v_gm_w_in, gm_b_in=v_gm_b_in, gm_ln_g=v_gm_ln_g, gm_ln_b=v_gm_ln_b, gm_w_s=v_gm_w_s, gm_b_s=v_gm_b_s, gm_w_out=v_gm_w_out, fox_w_in=v_fox_w_in, fox_b_f=v_fox_b_f, fox_w_out=v_fox_w_out, sb_w_in=v_sb_w_in, sb_w_out=v_sb_w_out, cv_w_in=v_cv_w_in, cv_b_in=v_cv_b_in, cv_dw=v_cv_dw, cv_dw_b=v_cv_dw_b, cv_ln_g=v_cv_ln_g, cv_ln_b=v_cv_ln_b, cv_w_out=v_cv_w_out, cv_b_out=v_cv_b_out)
    names = list(weights)

    nb, seq, d = x.shape
    t = nb * seq
    nl = mod_w.shape[0]
    alpha = (2.0 * nl) ** 0.25
    me = 4 * lax.axis_index("x") + 2 * lax.axis_index("y") + lax.axis_index("c")
    xs = x.reshape(t, d)
    tgt = loss_target.reshape(t, d)
    n_mod = mod_w.shape[2]
    n_ffn = ffn_w_in.shape[2]
    n_heads = d // HEAD_DIM
    npair = d // LANES

    cvp = d // N_DEV
    cv_small = jnp.concatenate([_pad_rows(cv_dw[0], CONV_HALO), cv_dw_b, cv_ln_g, cv_ln_b, cv_b_out,
                                cv_b_in.reshape(2, cvp), jnp.zeros((2, cvp), F32)], axis=0)
    cv_packed = cv_small.reshape(-1, d)
    first = _exchange_small(jnp.concatenate([_pad_rows(c, 8), _pad_rows(cv_packed, 8)], axis=0), "gather_c", False)
    c_all = first[:, :nb].reshape(N_DEV * nb, d)
    cv_all = first[:, 8:8 + cv_packed.shape[0]].reshape(N_DEV, cv_small.shape[0], cvp)
    mod_b_loc = lax.dynamic_slice_in_dim(mod_b, me * n_mod, n_mod, axis=1)[:, None, :]
    mod_loc = _mod_fwd(c_all, mod_w, mod_b_loc)
    mod_g = _exchange_small(mod_loc.reshape(nl * N_DEV * nb, n_mod), "gather_mod", False)
    mod_all = jnp.transpose(mod_g.reshape(N_DEV, nl, N_DEV * nb, n_mod), (1, 2, 0, 3)).reshape(nl, N_DEV * nb, N_DEV * n_mod)
    mod_me = lax.dynamic_slice_in_dim(mod_all, me * nb, nb, axis=1)
    mods = [[mod_me[l, :, k * d:(k + 1) * d][:, None, :] for k in range(6)] for l in range(nl)]

    assert nl == 4, "the exchange schedule below is written for the four-layer trunk"
    big = ["ffn_w_in", "ffn_w_out", "gm_w_in", "gm_w_out", "fox_w_in", "fox_w_out", "sb_w_in", "sb_w_out", "cv_w_in", "cv_w_out"]
    shard = {n: weights[n].astype(BF16) for n in big if not n.startswith("ffn")}
    for l in range(nl):
        shard["ffn_w_in", l] = ffn_w_in[l:l + 1].astype(BF16)
        shard["ffn_w_out", l] = ffn_w_out[l:l + 1].astype(BF16)
    now = ["gm_w_in", "gm_w_out"]
    with_gm_in = [("ffn_w_in", 0)]
    with_gm_gate = [("ffn_w_out", 0)]
    with_ffn_0 = ["fox_w_in", "fox_w_out"]
    later = [("ffn_w_in", 1), ("ffn_w_out", 1), "sb_w_in", "sb_w_out", ("ffn_w_in", 2), ("ffn_w_out", 2),
             "cv_w_in", "cv_w_out", ("ffn_w_in", 3), ("ffn_w_out", 3)]
    gathered = dict(zip(now, _gather_weights([shard[n] for n in now])))
    w_ffn_out_rows = lambda l: gathered["ffn_w_out", l].reshape(N_DEV // 2, n_ffn, d)
    sq = lambda n: gathered[n].reshape(d, d)
    cv_rows = jnp.transpose(cv_all, (1, 0, 2)).reshape(cv_small.shape[0], d)
    cv_dw_f, cv_dwb_f, cv_lng_f, cv_lnb_f, cv_bout_f = (cv_rows[:CONV_HALO], cv_rows[32:33], cv_rows[33:34], cv_rows[34:35], cv_rows[35:36])
    cv_bin_f = cv_all[:, 36:38, :].reshape(1, 2 * d)

    saved = []
    h = _modulate(xs, mods[0][1], mods[0][0], seq)
    xin = xs
    for l in range(nl):
        kind = l % 4
        sv = dict(x=xin, h=h)
        if kind == 0:
            pre, arrived = _proj_cols("gm_in", h, gathered["gm_w_in"], gm_w_in.shape[2], bias=gm_b_in,
                                      beside=("gather", [shard[n] for n in with_gm_in]))
            gathered.update(zip(with_gm_in, arrived))
            yv, arrived = _gm_spatial_fwd(pre, gm_ln_g, gm_ln_b, gm_w_s[0], jnp.transpose(gm_b_s[0]), seq,
                                          beside=("gather", [shard[n] for n in with_gm_gate]))
            gathered.update(zip(with_gm_gate, arrived))
            y = _mm_plain("gm_out", yv, sq("gm_w_out"), NN)
            sv.update(pre=pre, yv=yv)
        elif kind == 1:
            qkv = _qkv_proj("fox_qkv", h, fox_qkv_w)
            ft = _mm("fox_gate_proj", fox_f_wt, h, _sds((n_heads, t), F32), (t // min(TMM, t), 1),
                     pl.BlockSpec((n_heads, d), lambda i, r: (0, 0)), pl.BlockSpec((min(TMM, t), d), lambda i, r: (i, 0)),
                     pl.BlockSpec((n_heads, min(TMM, t)), lambda i, r: (0, i)), NT)
            b_f = jnp.transpose(fox_b_f)
            frow_p = _fox_gate_fwd(ft, b_f, seq).reshape(npair, 2, t)
            o, lse, *arrived = _fox_fwd(qkv, frow_p, nb, seq, gather=[shard[n] for n in later])
            gathered.update(zip(later, arrived))
            sb_qkv_w = _full_cols(gathered["sb_w_in"], sb_w_in.shape[2])
            y = _mm_plain("fox_out", o, sq("fox_w_out"), NN)
            sv.update(qkv=qkv, ft=ft, b_f=b_f, frow=frow_p, o=o, lse=lse)
        elif kind == 2:
            qkv = _qkv_proj("sb_qkv", h, sb_qkv_w)
            o, ltot = _sb_fwd(qkv, nb, seq)
            y = _mm_plain("sb_out", o, sq("sb_w_out"), NN)
            sv.update(qkv=qkv, o=o, ltot=ltot)
        else:
            pre = _proj_cols("cv_in", h, gathered["cv_w_in"], cv_w_in.shape[2], bias=cv_bin_f)
            y1, y2 = _cv_conv_fwd(pre, cv_dw_f, cv_dwb_f, cv_lng_f, cv_lnb_f, seq)
            y = _mm_plain("cv_out", y2, sq("cv_w_out"), NN, bias=cv_bout_f)
            sv.update(pre=pre, y1=y1, y2=y2)
        x1, h2 = _lnres_fwd(xin, y, mods[l][2], ln1_g[l:l + 1], ln1_b[l:l + 1], alpha, seq, nxt=(mods[l][4], mods[l][3]))
        if l == 0:
            (hg, hu, act), arrived = _ffn_in(h2, gathered["ffn_w_in", l], 0, beside=("gather", [shard[n] for n in with_ffn_0]))
            gathered.update(zip(with_ffn_0, arrived))
            fox_full = _full_cols(gathered["fox_w_in"], fox_w_in.shape[2])
            fox_qkv_w, fox_f_wt = fox_full[:, :3 * d], jnp.transpose(fox_full[:, 3 * d:])
        else:
            hg, hu, act = _ffn_in(h2, gathered["ffn_w_in", l], 0)
        y2f = _ffn_out(act, w_ffn_out_rows(l), 0)
        sv.update(y=y, x1=x1, h2=h2, hg=hg, hu=hu, act=act, y2f=y2f)
        if l + 1 < nl:
            xin, h = _lnres_fwd(x1, y2f, mods[l][5], ln2_g[l:l + 1], ln2_b[l:l + 1], alpha, seq, nxt=(mods[l + 1][1], mods[l + 1][0]))
        else:
            xin = _lnres_fwd(x1, y2f, mods[l][5], ln2_g[l:l + 1], ln2_b[l:l + 1], alpha, seq)
        saved.append(sv)

    dx, sq_err = _loss_head(xin, tgt)
    loss = lax.psum(0.5 * jnp.sum(sq_err) / d, ("x", "y", "c"))

    small = {}
    bigg = {}
    recv = {}
    dmod_parts = [dict() for _ in range(nl)]
    pending = None
    d_ln = dict(ln1_g=[None] * nl, ln1_b=[None] * nl, ln2_g=[None] * nl, ln2_b=[None] * nl)
    beside_sb = [("ffn_w_in", 3), ("ffn_w_out", 3), "cv_w_in", "cv_w_out", ("ffn_w_in", 2), ("ffn_w_out", 2)]
    beside_fox = ["sb_w_in", "sb_w_out", ("ffn_w_in", 1), ("ffn_w_out", 1)]
    beside_dact_0 = ["fox_w_in", "fox_w_out"]
    beside_dh_0 = [("ffn_w_in", 0)]
    beside_gm = [("ffn_w_out", 0), "gm_w_out"]
    beside_gm_in = ["gm_w_in"]
    for l in reversed(range(nl)):
        sv = saved[l]
        kind = l % 4
        if pending is None:
            dxr, dy2, dlg, dlb, _, dgate2 = _lnres_bwd(dx, sv["x1"], sv["y2f"], mods[l][5], ln2_g[l:l + 1], alpha, seq)
        else:
            dxr, dy2, dlg, dlb, _, dgate2, dsc1, dsh1 = _lnres_bwd(pending[1], sv["x1"], sv["y2f"], mods[l][5], ln2_g[l:l + 1], alpha, seq,
                                                                   through=(pending[0], pending[2], ln2_b[l:l + 1]))
            dmod_parts[l + 1].update(sc1=dsc1, sh1=dsh1)
        d_ln["ln2_g"][l], d_ln["ln2_b"][l] = dlg, dlb
        if l == 0:
            (dg_, du_), landed = _ffn_dact(dy2, w_ffn_out_rows(l), sv["hg"], sv["hu"], 0, beside=("scatter", [bigg[n] for n in beside_dact_0]))
            recv.update(zip(beside_dact_0, landed))
        else:
            dg_, du_ = _ffn_dact(dy2, w_ffn_out_rows(l), sv["hg"], sv["hu"], 0)
        dwi, dwo = _ffn_bwd_weights(sv["h2"], dy2, sv["act"], dg_, du_)
        bigg["ffn_w_in", l] = dwi[None]
        bigg["ffn_w_out", l] = dwo.reshape(1, N_DEV, n_ffn // 2, d)
        if l == 0:
            dh2, landed = _ffn_dh(dg_, du_, gathered["ffn_w_in", l], 0, beside=("scatter", [bigg[n] for n in beside_dh_0]))
            recv.update(zip(beside_dh_0, landed))
        else:
            dh2 = _ffn_dh(dg_, du_, gathered["ffn_w_in", l], 0)
        dxr, dy, dlg, dlb, dysum, dgate1, dsc2, dsh2 = _lnres_bwd(dxr, sv["x"], sv["y"], mods[l][2], ln1_g[l:l + 1], alpha, seq,
                                                                  through=(dh2, mods[l][4], ln1_b[l:l + 1]))
        d_ln["ln1_g"][l], d_ln["ln1_b"][l] = dlg, dlb
        hh = sv["h"]
        if kind == 0:
            dyv = _mm_plain("gm_out_bwd", dy, sq("gm_w_out"), NT)
            bigg["gm_w_out"] = _mm_plain("gm_out_dw", sv["yv"], dy, TN, GRAD_WIRE).reshape(1, N_DEV, d // N_DEV, d)
            (dpre, dws, dbst, dlng, dlnb, dbin), landed = _gm_spatial_bwd(sv["pre"], dyv, gm_ln_g, gm_ln_b, gm_w_s[0], jnp.transpose(gm_b_s[0]), seq,
                                                                          beside=("scatter", [bigg[n] for n in beside_gm]))
            recv.update(zip(beside_gm, landed))
            small.update(gm_w_s=dws[None], gm_b_s=jnp.transpose(dbst)[None], gm_ln_g=dlng, gm_ln_b=dlnb, gm_b_in=dbin)
            bigg["gm_w_in"] = _grad_cols("gm_in_dw", hh, dpre, gm_w_in.shape[2])
            dh, landed = _back_cols("gm_in_bwd", dpre, gathered["gm_w_in"], gm_w_in.shape[2], beside=("scatter", [bigg[n] for n in beside_gm_in]))
            recv.update(zip(beside_gm_in, landed))
        elif kind == 1:
            do = _mm_plain("fox_out_bwd", dy, sq("fox_w_out"), NT)
            bigg["fox_w_out"] = _mm_plain("fox_out_dw", sv["o"], dy, TN, GRAD_WIRE).reshape(1, N_DEV, d // N_DEV, d)
            dqkv, dfr, dfq, *landed = _fox_bwd(sv["qkv"], sv["frow"], sv["o"], do, sv["lse"], nb, seq,
                                               scatter=[bigg[n] for n in beside_fox])
            recv.update(zip(beside_fox, landed))
            dft, dbf = _fox_gate_bwd(sv["ft"], sv["b_f"], dfr.reshape(n_heads, t),
                                     jnp.transpose(dfq, (0, 2, 1)).reshape(n_heads, t), seq)
            small["fox_b_f"] = jnp.transpose(dbf)
            dw_qkv = _qkv_dw("fox_qkv_dw", hh, dqkv)
            tk = min(TMM, t)
            dw_ft = _mm("fox_gate_dw", dft, hh, _sds((n_heads, d), F32), (1, t // tk),
                        pl.BlockSpec((n_heads, tk), lambda j, r: (0, r)), pl.BlockSpec((tk, d), lambda j, r: (r, 0)),
                        pl.BlockSpec((n_heads, d), lambda j, r: (0, 0)), NN)
            bigg["fox_w_in"] = _col_blocks(jnp.concatenate([dw_qkv, jnp.transpose(dw_ft).astype(GRAD_WIRE)], axis=1), fox_w_in.shape[2])
            dh_a = _qkv_dh("fox_qkv_bwd", dqkv, fox_qkv_w)
            tm = min(TMM, t)
            dh = _mm("fox_gate_bwd_h", dft, fox_f_wt, _sds((t, d), F32), (t // tm, 1),
                     pl.BlockSpec((n_heads, tm), lambda i, r: (0, i)), pl.BlockSpec((n_heads, d), lambda i, r: (0, 0)),
                     pl.BlockSpec((tm, d), lambda i, r: (i, 0)), TN, (dh_a,), (pl.BlockSpec((tm, d), lambda i, r: (i, 0)),), _add)
        elif kind == 2:
            do = _mm_plain("sb_out_bwd", dy, sq("sb_w_out"), NT)
            bigg["sb_w_out"] = _mm_plain("sb_out_dw", sv["o"], dy, TN, GRAD_WIRE).reshape(1, N_DEV, d // N_DEV, d)
            dqkv, *landed = _sb_bwd(sv["qkv"], do, sv["ltot"], nb, seq, scatter=[bigg[n] for n in beside_sb])
            recv.update(zip(beside_sb, landed))
            bigg["sb_w_in"] = _col_blocks(_qkv_dw("sb_qkv_dw", hh, dqkv), sb_w_in.shape[2])
            dh = _qkv_dh("sb_qkv_bwd", dqkv, sb_qkv_w)
        else:
            dy2c = _mm_plain("cv_out_bwd", dy, sq("cv_w_out"), NT)
            bigg["cv_w_out"] = _mm_plain("cv_out_dw", sv["y2"], dy, TN, GRAD_WIRE).reshape(1, N_DEV, d // N_DEV, d)
            dy1, dlng, dlnb, ddwb = _cv_norm_bwd(sv["y1"], dy2c, cv_lng_f, cv_lnb_f)
            dpre, ddw, dbin = _cv_conv_bwd(sv["pre"], dy1, cv_dw_f, seq)
            small.update(cv_b_out=dysum, cv_ln_g=dlng, cv_ln_b=dlnb, cv_dw_b=ddwb, cv_dw=ddw[:CONV_WIDTH], cv_b_in=dbin)
            bigg["cv_w_in"] = _grad_cols("cv_in_dw", hh, dpre, cv_w_in.shape[2])
            dh = _back_cols("cv_in_bwd", dpre, gathered["cv_w_in"], cv_w_in.shape[2])
        pending = (dh, dxr, mods[l][1])
        dmod_parts[l].update(g1=dgate1, sh2=dsh2, sc2=dsc2, g2=dgate2)
    dx, dsc1, dsh1 = _modulate_bwd(pending[0], pending[1], saved[0]["x"], pending[2], seq)
    dmod_parts[0].update(sc1=dsc1, sh1=dsh1)
    dmods = [jnp.concatenate([p["sh1"], p["sc1"], p["g1"], p["sh2"], p["sc2"], p["g2"]], axis=2)[:, 0, :] for p in dmod_parts]
    grad_x = dx.reshape(nb, seq, d)
    for n in d_ln:
        small[n] = jnp.concatenate(d_ln[n], axis=0)

    dmod_rows = jnp.stack(dmods).reshape(nl * nb, 6 * d)
    dmod_g = _exchange_small(_pad_rows(dmod_rows, 8 * ((nl * nb + 7) // 8)), "gather_dmod", False)[:, :nl * nb]
    dmod_all = jnp.transpose(dmod_g.reshape(N_DEV, nl, nb, 6 * d), (1, 0, 2, 3)).reshape(nl, N_DEV * nb, 6 * d)
    dmod_loc = lax.dynamic_slice_in_dim(dmod_all, me * n_mod, n_mod, axis=2)
    g_mod_w, g_mod_b = _mod_bwd(c_all, dmod_loc, dmod_all)
    grads = dict(mod_w=g_mod_w, mod_b=g_mod_b[:, 0, :])

    rep = ["ln1_g", "ln1_b", "ln2_g", "ln2_b", "gm_b_in", "gm_ln_g", "gm_ln_b", "gm_w_s", "gm_b_s", "fox_b_f"]
    cvs = ["cv_b_in", "cv_dw", "cv_dw_b", "cv_ln_g", "cv_ln_b", "cv_b_out"]

    def rows_of(a):
        flat = a.reshape(-1)
        pad = (-flat.shape[0]) % d
        return jnp.pad(flat, (0, pad)).reshape(-1, d)

    pack_rows = [rows_of(small[n]) for n in rep + cvs]
    counts = [r.shape[0] for r in pack_rows]
    total = sum(counts)
    pack = _pad_rows(jnp.concatenate(pack_rows, axis=0), 8 * ((total + 7) // 8))
    summed = _exchange_small(pack, "allreduce_small", True)
    offs = [sum(counts[:i]) for i in range(len(counts))]
    rep_rows = sum(counts[:len(rep)])
    for n, o_, cnt in zip(rep + cvs, offs, counts):
        full = summed[o_:o_ + cnt].reshape(-1)
        if n in rep:
            grads[n] = full[:weights[n].size].reshape(weights[n].shape)
        else:
            wshape = weights[n].shape
            cols = wshape[-1]
            full = full[:math.prod(wshape[:-1]) * cols * N_DEV].reshape(wshape[:-1] + (cols * N_DEV,))
            grads[n] = lax.dynamic_slice_in_dim(full, me * cols, cols, axis=full.ndim - 1)

    outs = {}

    def view2(a):
        return a.reshape(-1, a.shape[-1])

    for n in big:
        w2 = view2(weights[n])
        landed = [recv[n, l] for l in range(nl)] if n.startswith("ffn") else [recv[n]]
        parts = [r.reshape(N_DEV, w2.shape[0] // len(landed), w2.shape[1]) for r in landed]
        res = _adamw("adamw_" + n, w2, view2(mom1[n]), view2(mom2[n]), parts=parts)
        outs[n] = [r.reshape(weights[n].shape) for r in res]
    res = _adamw("adamw_mod_w", view2(mod_w), view2(m_mod_w), view2(v_mod_w), g=view2(grads["mod_w"]))
    outs["mod_w"] = [r.reshape(mod_w.shape) for r in res]
    rp = lambda src: _pad_rows(jnp.concatenate([rows_of(src[n]) for n in rep], axis=0), 8 * ((rep_rows + 7) // 8))
    res = _adamw("adamw_replicated", rp(weights), rp(mom1), rp(mom2), g=rp(grads))
    for n, o_, cnt in zip(rep, offs, counts):
        outs[n] = [r[o_:o_ + cnt].reshape(-1)[:weights[n].size].reshape(weights[n].shape) for r in res]
    cv_cols = weights["cv_b_out"].shape[-1]
    cp = lambda src: jnp.concatenate([src[n].reshape(-1, cv_cols) for n in cvs], axis=0)
    cv_cnt = [weights[n].size // cv_cols for n in cvs]
    cv_tot = sum(cv_cnt)
    cpp = lambda src: _pad_rows(cp(src), 8 * ((cv_tot + 7) // 8))
    res = _adamw("adamw_cv_small", cpp(weights), cpp(mom1), cpp(mom2), g=cpp(grads))
    o_ = 0
    for n, cnt in zip(cvs, cv_cnt):
        outs[n] = [r[o_:o_ + cnt].reshape(weights[n].shape) for r in res]
        o_ += cnt
    res = _adamw("adamw_mod_b", mod_b, m_mod_b, v_mod_b, g=grads["mod_b"])
    outs["mod_b"] = list(res)

    return (loss, grad_x, *[outs[n][0] for n in names], *[outs[n][1] for n in names],
            *[outs[n][2] for n in names], *[outs[n][3] for n in names])
```

```python
import functools
import math

import jax
import jax.numpy as jnp
from jax import lax
from jax.experimental import pallas as pl
from jax.experimental.pallas import tpu as pltpu

F32 = jnp.float32
BF16 = jnp.bfloat16
MESH = pl.DeviceIdType.MESH

N_DEV = 8
HEAD_DIM = 64
LANES = 128
SUBLANES = 8
GM_CHUNK = 128
GM_GROUPS = 8
CONV_WIDTH = 31
CONV_HALO = 32
CONV_ROWS = 32
LN_EPS = 1e-5
NEG_INF = -1e30
SB_DEAD = -100.0
GRAD_WIRE = jnp.bfloat16
FFN_KEEP = jnp.bfloat16

ADAM_LR = 0.001
ADAM_B1 = 0.9
ADAM_B2 = 0.999
ADAM_EPS = 1e-08
ADAM_WD = 0.01
ADAM_STEP = 10

TM = 512
TMM = 1024
TQ = 256
FOX_FWD_UNROLL = 8
FOX_BWD_UNROLL = 4
SB_BWD_UNROLL = 2
TT = 512
VMEM_BIG = 56 * 1024 * 1024

NN = (((1,), (0,)), ((), ()))
NT = (((1,), (1,)), ((), ()))
TN = (((0,), (0,)), ((), ()))


def _call(body, name, out_shape, grid=None, in_specs=None, out_specs=None, scratch=(), vmem=None, aliases=None):
    params = {}
    if grid is not None:
        params["dimension_semantics"] = ("arbitrary",) * len(grid)
    if vmem is not None:
        params["vmem_limit_bytes"] = vmem
    kw = {}
    if grid is not None:
        kw["grid"] = grid
    if in_specs is not None:
        kw["in_specs"] = in_specs
    if out_specs is not None:
        kw["out_specs"] = out_specs
    if aliases is not None:
        kw["input_output_aliases"] = aliases
    return pl.pallas_call(body, name=name, out_shape=out_shape, scratch_shapes=list(scratch),
                          compiler_params=pltpu.CompilerParams(**params), **kw)


def _sds(shape, dtype):
    return jax.ShapeDtypeStruct(tuple(shape), dtype)


def _dot(a, b, dims=NN):
    return lax.dot_general(a.astype(BF16), b.astype(BF16), dims, preferred_element_type=F32)


def _split3(x):
    h1 = x.astype(BF16)
    r1 = x - h1.astype(F32)
    h2 = r1.astype(BF16)
    h3 = (r1 - h2.astype(F32)).astype(BF16)
    return h1, h2, h3


def _dot_exact(x, m, dims=NN):
    h1, h2, h3 = _split3(x)
    d = lambda h: lax.dot_general(h, m, dims, preferred_element_type=F32)
    return (d(h1) + d(h2)) + d(h3)


def _dot_exact_rhs(m, x, dims=NN):
    h1, h2, h3 = _split3(x)
    d = lambda h: lax.dot_general(m, h, dims, preferred_element_type=F32)
    return (d(h1) + d(h2)) + d(h3)


def _silu(x):
    return x * jax.nn.sigmoid(x)


def _gelu(x):
    return 0.5 * x * (1.0 + lax.erf(x * (2.0 ** -0.5)))


def _gelu_grad(x):
    return 0.5 * (1.0 + lax.erf(x * (2.0 ** -0.5))) + x * jnp.exp(-0.5 * x * x) * ((2.0 * math.pi) ** -0.5)


def _log_sigmoid(z):
    return jnp.minimum(z, 0.0) - jnp.log(1.0 + jnp.exp(-jnp.abs(z)))


def _ln_stats(r):
    mu = jnp.mean(r, axis=-1, keepdims=True)
    rc = r - mu
    var = jnp.mean(rc * rc, axis=-1, keepdims=True)
    return rc, lax.rsqrt(var + LN_EPS)


def _colsum(x):
    return jnp.sum(x, axis=0, keepdims=True)


def _peers():
    mx, my, mc = lax.axis_index("x"), lax.axis_index("y"), lax.axis_index("c")
    me = 4 * mx + 2 * my + mc
    out = []
    for k in range(1, N_DEV):
        px = 1 - mx if (k >> 2) & 1 else mx
        py = 1 - my if (k >> 1) & 1 else my
        pc = 1 - mc if k & 1 else mc
        out.append(((px, py, pc), 4 * px + 2 * py + pc))
    return me, out


def _exchange_small(x, name, reduce):
    rows, cols = x.shape

    def body(x_ref, o_ref, *rest):
        if reduce:
            land, send_sems, recv_sems, local_sem = rest
        else:
            land = o_ref
            send_sems, recv_sems, local_sem = rest
        me, peers = _peers()
        mine = pltpu.make_async_copy(x_ref, land.at[me], local_sem)
        mine.start()
        sends = []
        for k, (peer, _) in enumerate(peers):
            cp = pltpu.make_async_remote_copy(src_ref=x_ref, dst_ref=land.at[me], send_sem=send_sems.at[k],
                                              recv_sem=recv_sems.at[k], device_id=peer, device_id_type=MESH)
            cp.start()
            sends.append(cp)
        for k, (peer, blk) in enumerate(peers):
            pltpu.make_async_remote_copy(src_ref=x_ref, dst_ref=land.at[blk], send_sem=send_sems.at[k],
                                         recv_sem=recv_sems.at[k], device_id=peer, device_id_type=MESH).wait_recv()
        for cp in sends:
            cp.wait_send()
        mine.wait()
        if reduce:
            acc = land[0]
            for s in range(1, N_DEV):
                acc = acc + land[s]
            o_ref[...] = acc

    vm = pl.BlockSpec(memory_space=pltpu.VMEM)
    scratch = [pltpu.SemaphoreType.DMA((N_DEV - 1,)), pltpu.SemaphoreType.DMA((N_DEV - 1,)), pltpu.SemaphoreType.DMA]
    if reduce:
        scratch = [pltpu.VMEM((N_DEV, rows, cols), F32)] + scratch
        out = _sds((rows, cols), F32)
    else:
        out = _sds((N_DEV, rows, cols), F32)
    return _call(body, name, out, in_specs=[vm], out_specs=vm, scratch=scratch, vmem=VMEM_BIG)(x)


def _comm_scratch(n):
    return [pltpu.SemaphoreType.DMA((n, N_DEV - 1)), pltpu.SemaphoreType.DMA((n, N_DEV - 1)), pltpu.SemaphoreType.DMA((n,))]


def _gather_stage(stage, ins, outs, send_sems, recv_sems, local_sems):
    n = len(ins)
    mx, my, mc = lax.axis_index("x"), lax.axis_index("y"), lax.axis_index("c")
    here, sibling = (mx, my, mc), (mx, my, 1 - mc)
    chips = [(1 - mx, my), (mx, 1 - my), (1 - mx, 1 - my)]

    def block(px, py, pc):
        return 4 * px + 2 * py + pc

    def copy(a, k, blk, to, src=None):
        dst = outs[a].at[:, blk]
        return pltpu.make_async_remote_copy(src_ref=dst if src is None else src, dst_ref=dst, send_sem=send_sems.at[a, k],
                                            recv_sem=recv_sems.at[a, k], device_id=to, device_id_type=MESH)

    me = block(*here)
    for a in range(n):
        local = pltpu.make_async_copy(ins[a], outs[a].at[:, me], local_sems.at[a])
        first = [copy(a, 0, me, sibling, src=ins[a])] + [copy(a, 1 + j, me, (*chip, mc), src=ins[a]) for j, chip in enumerate(chips)]
        if stage == 0:
            local.start()
            for cp in first:
                cp.start()
        if stage == 1:
            for j, chip in enumerate(chips):
                copy(a, 1 + j, block(*chip, mc), here).wait_recv()
                copy(a, 4 + j, block(*chip, mc), sibling).start()
        if stage == 2:
            copy(a, 0, block(mx, my, 1 - mc), here).wait_recv()
            for j, chip in enumerate(chips):
                copy(a, 4 + j, block(*chip, 1 - mc), here).wait_recv()
            for cp in first:
                cp.wait_send()
            for j, chip in enumerate(chips):
                copy(a, 4 + j, block(*chip, mc), sibling).wait_send()
            local.wait()


def _gather_weights(shards):
    n = len(shards)

    def body(*refs):
        for stage in range(3):
            _gather_stage(stage, refs[:n], refs[n:2 * n], *refs[2 * n:])

    hbm = pl.BlockSpec(memory_space=pl.ANY)
    return _call(body, "gather_weights", _gathered_shapes(shards), in_specs=[hbm] * n, out_specs=[hbm] * n, scratch=_comm_scratch(n))(*shards)


def _gathered_shapes(shards):
    return [_sds((s.shape[0], N_DEV) + s.shape[1:], s.dtype) for s in shards]


def _scattered_shapes(grads):
    return [_sds((N_DEV, g.shape[0]) + g.shape[2:], g.dtype) for g in grads]


def _run(beside, body, name, out_shape, args, grid, in_specs, out_specs, scratch=(), vmem=None):
    if beside is None:
        return _call(body, name, out_shape, grid=grid, in_specs=in_specs, out_specs=out_specs, scratch=scratch, vmem=vmem)(*args)
    kind, arrays = beside
    n = len(arrays)
    outs = tuple(out_shape) if isinstance(out_shape, (tuple, list)) else (out_shape,)
    ospecs = tuple(out_specs) if isinstance(out_specs, (tuple, list)) else (out_specs,)
    total = math.prod(grid)
    gather = kind == "gather"
    stage_fn = _gather_stage if gather else _scatter_stage
    early = {0: 0, 1: (3 * total) // 4} if gather else {0: 0}

    def carrier(*refs):
        own_in, send, own_out, land, sems, own_scratch = _carried(n, len(in_specs), len(outs), refs)
        at = 0
        for ax, size in enumerate(grid):
            at = at * size + pl.program_id(ax)
        for stage, when in early.items():
            @pl.when(at == when)
            def _(stage=stage):
                stage_fn(stage, send, land, *sems)
        body(*own_in, *own_out, *own_scratch)

        @pl.when(at == total - 1)
        def _():
            stage_fn(2 if gather else 1, send, land, *sems)

    hbm = pl.BlockSpec(memory_space=pl.ANY)
    shapes = _gathered_shapes(arrays) if gather else _scattered_shapes(arrays)
    res = _call(carrier, name, (*outs, *shapes), grid=grid, in_specs=[*in_specs, *[hbm] * n], out_specs=(*ospecs, *[hbm] * n),
                scratch=[*_comm_scratch(n), *scratch], vmem=vmem)(*args, *arrays)
    own = res[:len(outs)]
    return (own if len(own) > 1 else own[0]), list(res[len(outs):])


def _scatter_stage(stage, ins, outs, send_sems, recv_sems, local_sems):
    me, peers = _peers()
    for a in range(len(ins)):
        local = pltpu.make_async_copy(ins[a].at[:, me], outs[a].at[me], local_sems.at[a])
        sends = [pltpu.make_async_remote_copy(src_ref=ins[a].at[:, blk], dst_ref=outs[a].at[me], send_sem=send_sems.at[a, k],
                                              recv_sem=recv_sems.at[a, k], device_id=peer, device_id_type=MESH)
                 for k, (peer, blk) in enumerate(peers)]
        if stage == 0:
            local.start()
            for cp in sends:
                cp.start()
        if stage == 1:
            for k, (peer, blk) in enumerate(peers):
                pltpu.make_async_remote_copy(src_ref=ins[a].at[:, me], dst_ref=outs[a].at[blk], send_sem=send_sems.at[a, k],
                                             recv_sem=recv_sems.at[a, k], device_id=peer, device_id_type=MESH).wait_recv()
            for cp in sends:
                cp.wait_send()
            local.wait()


def _mm(name, a, b, out, grid, a_spec, b_spec, o_spec, dims, extra=(), extra_specs=(), epilogue=None, vmem=VMEM_BIG, beside=None):
    nred = grid[-1]
    red_axis = len(grid) - 1
    acc_shape = tuple(d for d in o_spec.block_shape if d is not None)
    n_extra = len(extra)

    def body(a_ref, b_ref, *rest):
        ex = rest[:n_extra]
        o_ref = rest[n_extra]

        def finish(acc):
            if epilogue is not None:
                acc = epilogue(acc, *[e[...] for e in ex])
            o_ref[...] = acc.astype(o_ref.dtype)

        prod = _dot(a_ref[...], b_ref[...], dims)
        if nred == 1:
            finish(prod)
        else:
            acc_ref = rest[n_extra + 1]
            r = pl.program_id(red_axis)

            @pl.when(r == 0)
            def _():
                acc_ref[...] = prod

            @pl.when(r > 0)
            def _():
                acc_ref[...] += prod

            @pl.when(r == nred - 1)
            def _():
                finish(acc_ref[...])

    scratch = [pltpu.VMEM(acc_shape, F32)] if nred > 1 else []
    return _run(beside, body, name, out, (a, b, *extra), grid, [a_spec, b_spec, *extra_specs], o_spec, scratch, vmem)


def _add(acc, x):
    return acc + x


def _proj_cols(name, h, w, n_slot, bias=None, out_dtype=F32, beside=None):
    t, k = h.shape
    s = w.shape[1]
    tm = min(TM, t)

    def body(h_ref, w_ref, *rest):
        o_ref = rest[-1]
        hv = h_ref[...]
        for j in range(s):
            cols = slice(j * n_slot, (j + 1) * n_slot)
            acc = _dot(hv, w_ref[j])
            if bias is not None:
                acc = acc + rest[0][:, cols]
            o_ref[:, cols] = acc.astype(o_ref.dtype)

    ins, in_specs = [h, w], [pl.BlockSpec((tm, k), lambda i: (i, 0)), pl.BlockSpec((None, s, k, n_slot), lambda i: (0, 0, 0, 0))]
    if bias is not None:
        ins.append(bias)
        in_specs.append(pl.BlockSpec((1, s * n_slot), lambda i: (0, 0)))
    return _run(beside, body, name, _sds((t, s * n_slot), out_dtype), tuple(ins), (t // tm,), in_specs,
                pl.BlockSpec((tm, s * n_slot), lambda i: (i, 0)), vmem=VMEM_BIG)


def _accumulate_over_tokens(name, ins, in_specs, out, o_spec, acc_shape, n_steps, terms, store=None, fill=None):
    n_in = len(ins) + (fill is not None)

    def body(*refs):
        o_ref, acc_ref = refs[n_in], refs[n_in + 1]
        r = pl.program_id(0)

        @pl.when(r == 0)
        def _():
            acc_ref[...] = jnp.zeros_like(acc_ref)

        for s, prod in enumerate(terms(*refs[:len(ins)])):
            acc_ref[s] += prod

        @pl.when(r == n_steps - 1)
        def _():
            if store is None:
                o_ref[...] = acc_ref[...].reshape(o_ref.shape).astype(o_ref.dtype)
            else:
                store(o_ref, acc_ref)

    if fill is None:
        return _call(body, name, out, grid=(n_steps,), in_specs=in_specs, out_specs=o_spec,
                     scratch=[pltpu.VMEM(acc_shape, F32)], vmem=VMEM_BIG)(*ins)
    return _call(body, name, out, grid=(n_steps,), in_specs=[*in_specs, pl.BlockSpec(memory_space=pl.ANY)], out_specs=o_spec,
                 scratch=[pltpu.VMEM(acc_shape, F32)], vmem=VMEM_BIG, aliases={len(ins): 0})(*ins, fill)


def _grad_cols(name, h, g, n_slot):
    t, k = h.shape
    s = g.shape[1] // n_slot
    tk = min(TMM, t)

    def terms(h_ref, g_ref):
        hv = h_ref[...]
        return [_dot(hv, g_ref[:, j * n_slot:(j + 1) * n_slot], TN) for j in range(s)]

    return _accumulate_over_tokens(name, (h, g), [pl.BlockSpec((tk, k), lambda r: (r, 0)), pl.BlockSpec((tk, s * n_slot), lambda r: (r, 0))],
                                   _sds((1, s, k, n_slot), GRAD_WIRE), pl.BlockSpec((1, s, k, n_slot), lambda r: (0, 0, 0, 0)),
                                   (s, k, n_slot), t // tk, terms)


def _back_cols(name, g, w, n_slot, beside=None):
    t = g.shape[0]
    s, k = w.shape[1], w.shape[2]
    tm = min(TM, t)

    def body(g_ref, w_ref, o_ref):
        acc = None
        for j in range(s):
            term = _dot(g_ref[:, j * n_slot:(j + 1) * n_slot], w_ref[j], NT)
            acc = term if acc is None else acc + term
        o_ref[...] = acc

    return _run(beside, body, name, _sds((t, k), F32), (g, w), (t // tm,),
                [pl.BlockSpec((tm, s * n_slot), lambda i: (i, 0)), pl.BlockSpec((None, s, k, n_slot), lambda i: (0, 0, 0, 0))],
                pl.BlockSpec((tm, k), lambda i: (i, 0)), vmem=VMEM_BIG)


def _mm_plain(name, a, b, dims, out_dtype=F32, bias=None):
    if dims == TN:
        t, k = a.shape
        n = b.shape[1]
        tk = min(TMM, t)
        return _mm(name, a, b, _sds((k, n), out_dtype), (1, t // tk),
                   pl.BlockSpec((tk, k), lambda j, r: (r, 0)), pl.BlockSpec((tk, n), lambda j, r: (r, 0)),
                   pl.BlockSpec((k, n), lambda j, r: (0, 0)), TN)
    t = a.shape[0]
    tm = min(TMM, t)
    n = b.shape[1] if dims == NN else b.shape[0]
    extra, especs, epi = (), (), None
    if bias is not None:
        extra, especs, epi = (bias,), (pl.BlockSpec((1, n), lambda i, r: (0, 0)),), _add
    return _mm(name, a, b, _sds((t, n), out_dtype), (t // tm, 1),
               pl.BlockSpec((tm, a.shape[1]), lambda i, r: (i, 0)), pl.BlockSpec(b.shape, lambda i, r: (0, 0)),
               pl.BlockSpec((tm, n), lambda i, r: (i, 0)), dims, extra, especs, epi)


def _mod_fwd(c_all, mod_w, mod_b_loc):
    nl, d, n = mod_w.shape
    nb = c_all.shape[0]

    def body(c_ref, w_ref, b_ref, o_ref):
        o_ref[...] = _dot(_silu(c_ref[...]), w_ref[...]) + b_ref[...]

    return _call(body, "mod_fwd", _sds((nl, nb, n), F32), grid=(nl,),
                 in_specs=[pl.BlockSpec((nb, d), lambda l: (0, 0)), pl.BlockSpec((None, d, n), lambda l: (l, 0, 0)),
                           pl.BlockSpec((None, 1, n), lambda l: (l, 0, 0))],
                 out_specs=pl.BlockSpec((None, nb, n), lambda l: (l, 0, 0)))(c_all, mod_w, mod_b_loc)


def _mod_bwd(c_all, dmod_loc, dmod_all):
    nl, nb, n = dmod_loc.shape
    d = c_all.shape[1]
    n_all = dmod_all.shape[2]

    def body(c_ref, dl_ref, da_ref, gw_ref, gb_ref):
        gw_ref[...] = _dot(_silu(c_ref[...]), dl_ref[...], TN)
        gb_ref[...] = _colsum(da_ref[...])

    return _call(body, "mod_bwd", (_sds((nl, d, n), F32), _sds((nl, 1, n_all), F32)), grid=(nl,),
                 in_specs=[pl.BlockSpec((nb, d), lambda l: (0, 0)), pl.BlockSpec((None, nb, n), lambda l: (l, 0, 0)),
                           pl.BlockSpec((None, nb, n_all), lambda l: (l, 0, 0))],
                 out_specs=(pl.BlockSpec((None, d, n), lambda l: (l, 0, 0)), pl.BlockSpec((None, 1, n_all), lambda l: (l, 0, 0))),
                 )(c_all, dmod_loc, dmod_all)


def _row_spec(d, tpb):
    return pl.BlockSpec((None, 1, d), lambda i: (i // tpb, 0, 0))


def _tile_spec(tm, d):
    return pl.BlockSpec((tm, d), lambda i: (i, 0))


def _vec_spec(d):
    return pl.BlockSpec((1, d), lambda i: (0, 0))


def _modulate(x, sc, sh, seq):
    t, d = x.shape
    tm = min(TM, seq)
    tpb = seq // tm

    def body(x_ref, sc_ref, sh_ref, h_ref):
        h_ref[...] = (x_ref[...] * (1.0 + sc_ref[...]) + sh_ref[...]).astype(BF16)

    return _call(body, "modulate", _sds((t, d), BF16), grid=(t // tm,),
                 in_specs=[_tile_spec(tm, d), _row_spec(d, tpb), _row_spec(d, tpb)], out_specs=_tile_spec(tm, d))(x, sc, sh)


def _lnres_fwd(x, y, gate, lg, lb, alpha, seq, nxt=None, tgt=None):
    t, d = x.shape
    tm = min(TM, seq)
    tpb = seq // tm

    def body(x_ref, y_ref, g_ref, lg_ref, lb_ref, *rest):
        r = alpha * x_ref[...] + (1.0 + g_ref[...]) * y_ref[...]
        rc, rstd = _ln_stats(r)
        xn = rc * rstd * lg_ref[...] + lb_ref[...]
        if nxt is None:
            t_ref, dx_ref, sq_ref = rest
            e = xn - t_ref[...]
            dx_ref[...] = e * (1.0 / d)

            @pl.when(pl.program_id(0) == 0)
            def _():
                sq_ref[...] = jnp.zeros_like(sq_ref)

            sq_ref[...] += _colsum(e * e)
        else:
            sc_ref, sh_ref, xo_ref, h_ref = rest
            xo_ref[...] = xn
            h_ref[...] = (xn * (1.0 + sc_ref[...]) + sh_ref[...]).astype(BF16)

    ins = [_tile_spec(tm, d), _tile_spec(tm, d), _row_spec(d, tpb), _vec_spec(d), _vec_spec(d)]
    if nxt is None:
        return _call(body, "lnres_fwd_loss", (_sds((t, d), F32), _sds((1, d), F32)), grid=(t // tm,), in_specs=ins + [_tile_spec(tm, d)],
                     out_specs=(_tile_spec(tm, d), _vec_spec(d)))(x, y, gate, lg, lb, tgt)
    return _call(body, "lnres_fwd", (_sds((t, d), F32), _sds((t, d), BF16)), grid=(t // tm,),
                 in_specs=ins + [_row_spec(d, tpb), _row_spec(d, tpb)],
                 out_specs=(_tile_spec(tm, d), _tile_spec(tm, d)))(x, y, gate, lg, lb, *nxt)


def _lnres_bwd(dxo, x, y, gate, lg, alpha, seq, through=None):
    t, d = x.shape
    tm = min(TM, seq)
    tpb = seq // tm
    nb = t // seq

    def body(dxo_ref, x_ref, y_ref, g_ref, lg_ref, *rest):
        if through is None:
            dxr_ref, dy_ref, dlg_ref, dlb_ref, dys_ref, dg_ref = rest
        else:
            dh_ref, sc_ref, lb_ref, dxr_ref, dy_ref, dlg_ref, dlb_ref, dys_ref, dg_ref, dsc_ref, dsh_ref = rest
        i = pl.program_id(0)
        yv = y_ref[...]
        r = alpha * x_ref[...] + (1.0 + g_ref[...]) * yv
        rc, rstd = _ln_stats(r)
        xhat = rc * rstd
        dxo_v = dxo_ref[...]
        if through is not None:
            dhv = dh_ref[...]
            dxo_v = dxo_v + dhv * (1.0 + sc_ref[...])

            @pl.when(i % tpb == 0)
            def _():
                dsc_ref[...] = jnp.zeros_like(dsc_ref)
                dsh_ref[...] = jnp.zeros_like(dsh_ref)

            dsc_ref[...] += _colsum(dhv * (xhat * lg_ref[...] + lb_ref[...]))
            dsh_ref[...] += _colsum(dhv)
        dxh = dxo_v * lg_ref[...]
        m1 = jnp.mean(dxh, axis=-1, keepdims=True)
        m2 = jnp.mean(dxh * xhat, axis=-1, keepdims=True)
        dr = rstd * (dxh - m1 - xhat * m2)
        dyv = (1.0 + g_ref[...]) * dr
        dxr_ref[...] = alpha * dr
        dy_ref[...] = dyv.astype(BF16)

        @pl.when(i == 0)
        def _():
            dlg_ref[...] = jnp.zeros_like(dlg_ref)
            dlb_ref[...] = jnp.zeros_like(dlb_ref)
            dys_ref[...] = jnp.zeros_like(dys_ref)

        @pl.when(i % tpb == 0)
        def _():
            dg_ref[...] = jnp.zeros_like(dg_ref)

        dlg_ref[...] += _colsum(dxo_v * xhat)
        dlb_ref[...] += _colsum(dxo_v)
        dys_ref[...] += _colsum(dyv)
        dg_ref[...] += _colsum(dr * yv)

    out = [_sds((t, d), F32), _sds((t, d), BF16), _sds((1, d), F32), _sds((1, d), F32), _sds((1, d), F32), _sds((nb, 1, d), F32)]
    out_specs = [_tile_spec(tm, d), _tile_spec(tm, d), _vec_spec(d), _vec_spec(d), _vec_spec(d), _row_spec(d, tpb)]
    ins = [dxo, x, y, gate, lg]
    in_specs = [_tile_spec(tm, d), _tile_spec(tm, d), _tile_spec(tm, d), _row_spec(d, tpb), _vec_spec(d)]
    if through is not None:
        ins += list(through)
        in_specs += [_tile_spec(tm, d), _row_spec(d, tpb), _vec_spec(d)]
        out += [_sds((nb, 1, d), F32), _sds((nb, 1, d), F32)]
        out_specs += [_row_spec(d, tpb), _row_spec(d, tpb)]
    return _call(body, "lnres_bwd" if through is None else "lnres_mod_bwd", tuple(out), grid=(t // tm,),
                 in_specs=in_specs, out_specs=tuple(out_specs))(*ins)


def _modulate_bwd(dh, dxr, x, sc, seq):
    t, d = x.shape
    tm = min(TM, seq)
    tpb = seq // tm
    nb = t // seq

    def body(dh_ref, dxr_ref, x_ref, sc_ref, dx_ref, dsc_ref, dsh_ref):
        dhv = dh_ref[...]
        dx_ref[...] = dxr_ref[...] + dhv * (1.0 + sc_ref[...])

        @pl.when(pl.program_id(0) % tpb == 0)
        def _():
            dsc_ref[...] = jnp.zeros_like(dsc_ref)
            dsh_ref[...] = jnp.zeros_like(dsh_ref)

        dsc_ref[...] += _colsum(dhv * x_ref[...])
        dsh_ref[...] += _colsum(dhv)

    return _call(body, "modulate_bwd", (_sds((t, d), F32), _sds((nb, 1, d), F32), _sds((nb, 1, d), F32)), grid=(t // tm,),
                 in_specs=[_tile_spec(tm, d), _tile_spec(tm, d), _tile_spec(tm, d), _row_spec(d, tpb)],
                 out_specs=(_tile_spec(tm, d), _row_spec(d, tpb), _row_spec(d, tpb)))(dh, dxr, x, sc)


def _ffn_in(h, w_in, layer, beside=None):
    t, d = h.shape
    n = w_in.shape[3]
    half = N_DEV // 2
    tm = min(TMM, t)

    def body(h_ref, wg_ref, wu_ref, g_ref, u_ref, a_ref):
        hv = h_ref[...]
        g = _dot(hv, wg_ref[...])
        u = _dot(hv, wu_ref[...])
        g_ref[...] = g.astype(FFN_KEEP)
        u_ref[...] = u.astype(FFN_KEEP)
        a_ref[...] = (_silu(g) * u).astype(BF16)

    blk = pl.BlockSpec((None, tm, n), lambda p, i: (p, i, 0))
    return _run(beside, body, "ffn_in", (_sds((half, t, n), FFN_KEEP), _sds((half, t, n), FFN_KEEP), _sds((half, t, n), BF16)),
                (h, w_in, w_in), (half, t // tm),
                [pl.BlockSpec((tm, d), lambda p, i: (i, 0)),
                 pl.BlockSpec((None, None, d, n), lambda p, i: (layer, p, 0, 0)),
                 pl.BlockSpec((None, None, d, n), lambda p, i: (layer, p + half, 0, 0))],
                (blk, blk, blk), vmem=VMEM_BIG)


def _ffn_out(act, w_out, layer):
    half, t, n = act.shape
    d = w_out.shape[2]
    tm = min(TM, t)

    def body(a_ref, w_ref, o_ref):
        acc = _dot(a_ref[0], w_ref[0])
        for p in range(1, half):
            acc = acc + _dot(a_ref[p], w_ref[p])
        o_ref[...] = acc

    return _call(body, "ffn_out", _sds((t, d), F32), grid=(t // tm,),
                 in_specs=[pl.BlockSpec((half, tm, n), lambda i: (0, i, 0)),
                           pl.BlockSpec((half, n, d), lambda i: (layer // half, 0, 0))],
                 out_specs=pl.BlockSpec((tm, d), lambda i: (i, 0)), vmem=VMEM_BIG)(act, w_out)


def _ffn_dact(dy, w_out, hg, hu, layer, beside=None):
    half, t, n = hg.shape
    d = dy.shape[1]
    tm = min(TMM, t)

    def body(dy_ref, w_ref, g_ref, u_ref, dg_ref, du_ref):
        da = _dot(dy_ref[...], w_ref[...], NT)
        g = g_ref[...].astype(F32)
        sg = jax.nn.sigmoid(g)
        dg_ref[...] = (da * u_ref[...].astype(F32) * (sg * (1.0 + g * (1.0 - sg)))).astype(BF16)
        du_ref[...] = (da * (g * sg)).astype(BF16)

    blk = pl.BlockSpec((None, tm, n), lambda p, i: (p, i, 0))
    return _run(beside, body, "ffn_dact", (_sds((half, t, n), BF16), _sds((half, t, n), BF16)), (dy, w_out, hg, hu), (half, t // tm),
                [pl.BlockSpec((tm, d), lambda p, i: (i, 0)), pl.BlockSpec((None, n, d), lambda p, i: (layer + p, 0, 0)), blk, blk],
                (blk, blk), vmem=VMEM_BIG)


def _ffn_bwd_weights(h, dy, act, dg, du):
    half, t, n = act.shape
    d = h.shape[1]
    tk = min(TMM, t)
    h_spec = pl.BlockSpec((tk, d), lambda r: (r, 0))
    g_spec = pl.BlockSpec((half, tk, n), lambda r: (0, r, 0))

    def in_terms(h_ref, g_ref):
        hv = h_ref[...]
        return [_dot(hv, g_ref[p], TN) for p in range(half)]

    def out_terms(a_ref, dy_ref):
        dyv = dy_ref[...]
        return [_dot(a_ref[p], dyv, TN) for p in range(half)]

    both = _sds((2 * half, d, n), GRAD_WIRE)
    dwi = _accumulate_over_tokens("ffn_dw_gate", (h, dg), [h_spec, g_spec], both, pl.BlockSpec((half, d, n), lambda r: (0, 0, 0)),
                                  (half, d, n), t // tk, in_terms)
    dwi = _accumulate_over_tokens("ffn_dw_up", (h, du), [h_spec, g_spec], both, pl.BlockSpec((half, d, n), lambda r: (1, 0, 0)),
                                  (half, d, n), t // tk, in_terms, fill=dwi)
    dwo = _accumulate_over_tokens("ffn_dw_out", (act, dy), [g_spec, h_spec], _sds((half, n, d), GRAD_WIRE),
                                  pl.BlockSpec((half, n, d), lambda r: (0, 0, 0)), (half, n, d), t // tk, out_terms)
    return dwi, dwo


def _ffn_dh(dg, du, w_in, layer, beside=None):
    half, t, n = dg.shape
    d = w_in.shape[2]
    tm = min(TM, t)

    def body(dg_ref, du_ref, w_ref, o_ref):
        acc = None
        for p in range(half):
            for ref, q in ((dg_ref, p), (du_ref, p + half)):
                term = _dot(ref[p], w_ref[q], NT)
                acc = term if acc is None else acc + term
        o_ref[...] = acc

    g_spec = pl.BlockSpec((half, tm, n), lambda i: (0, i, 0))
    return _run(beside, body, "ffn_dh", _sds((t, d), F32), (dg, du, w_in), (t // tm,),
                [g_spec, g_spec, pl.BlockSpec((None, 2 * half, d, n), lambda i: (layer, 0, 0, 0))],
                pl.BlockSpec((tm, d), lambda i: (i, 0)), vmem=VMEM_BIG)


def _tril(n, strict=False):
    r = lax.broadcasted_iota(jnp.int32, (n, n), 0)
    c = lax.broadcasted_iota(jnp.int32, (n, n), 1)
    return c < r if strict else c <= r


def _gm_spatial_fwd(pre, lng, lnb, w_s, b_st, seq, beside=None):
    t, w2 = pre.shape
    w = w2 // 2
    gd = w // GM_GROUPS
    tm = min(TM, seq)
    nch = tm // GM_CHUNK

    def body(pre_ref, lng_ref, lnb_ref, ws_ref, bs_ref, y_ref):
        v = _gelu(pre_ref[:, w:])
        vc, rstd = _ln_stats(v)
        vn = (vc * rstd * lng_ref[...] + lnb_ref[...]).astype(BF16)
        keep = _tril(GM_CHUNK)
        for g in range(GM_GROUPS):
            wm = jnp.where(keep, ws_ref[g], 0.0).astype(BF16)
            for ci in range(nch):
                rows = slice(ci * GM_CHUNK, (ci + 1) * GM_CHUNK)
                cols = slice(g * gd, (g + 1) * gd)
                sv = _dot(wm, vn[rows, cols]) + bs_ref[:, g:g + 1]
                u = _gelu(pre_ref[rows, cols])
                y_ref[rows, cols] = (u * sv).astype(BF16)

    return _run(beside, body, "gm_spatial_fwd", _sds((t, w), BF16), (pre, lng, lnb, w_s, b_st), (t // tm,),
                [_tile_spec(tm, w2), _vec_spec(w), _vec_spec(w),
                 pl.BlockSpec((GM_GROUPS, GM_CHUNK, GM_CHUNK), lambda i: (0, 0, 0)),
                 pl.BlockSpec((GM_CHUNK, GM_GROUPS), lambda i: (0, 0))],
                _tile_spec(tm, w), vmem=VMEM_BIG)


def _gm_spatial_bwd(pre, dyv, lng, lnb, w_s, b_st, seq, beside=None):
    t, w2 = pre.shape
    w = w2 // 2
    gd = w // GM_GROUPS
    tm = min(TM, seq)
    nch = tm // GM_CHUNK

    def body(pre_ref, dyv_ref, lng_ref, lnb_ref, ws_ref, bs_ref, dpre_ref, dws_ref, dbs_ref, dlg_ref, dlb_ref, dbin_ref, dvn_ref):
        @pl.when(pl.program_id(0) == 0)
        def _():
            dws_ref[...] = jnp.zeros_like(dws_ref)
            dbs_ref[...] = jnp.zeros_like(dbs_ref)
            dlg_ref[...] = jnp.zeros_like(dlg_ref)
            dlb_ref[...] = jnp.zeros_like(dlb_ref)
            dbin_ref[...] = jnp.zeros_like(dbin_ref)

        pv = pre_ref[:, w:]
        v = _gelu(pv)
        vc, rstd = _ln_stats(v)
        vhat = vc * rstd
        vn = (vhat * lng_ref[...] + lnb_ref[...]).astype(BF16)
        keep = _tril(GM_CHUNK)
        dbs_cols = []
        for g in range(GM_GROUPS):
            wm = jnp.where(keep, ws_ref[g], 0.0).astype(BF16)
            dwm = jnp.zeros((GM_CHUNK, GM_CHUNK), F32)
            dbs = jnp.zeros((GM_CHUNK, 1), F32)
            for ci in range(nch):
                rows = slice(ci * GM_CHUNK, (ci + 1) * GM_CHUNK)
                cols = slice(g * gd, (g + 1) * gd)
                vn_b = vn[rows, cols]
                sv = _dot(wm, vn_b) + bs_ref[:, g:g + 1]
                pu = pre_ref[rows, cols]
                dy = dyv_ref[rows, cols]
                du = dy * sv
                dsv = dy * _gelu(pu)
                dpu = du * _gelu_grad(pu)
                dpre_ref[rows, cols] = dpu.astype(BF16)
                dbin_ref[:, cols] += _colsum(dpu)
                dsv_b = dsv.astype(BF16)
                dwm = dwm + _dot(dsv_b, vn_b, NT)
                dbs = dbs + jnp.sum(dsv, axis=-1, keepdims=True)
                dvn_ref[rows, cols] = _dot(wm, dsv_b, TN)
            dws_ref[g] += jnp.where(keep, dwm, 0.0)
            dbs_cols.append(dbs)
        dbs_ref[...] += jnp.concatenate(dbs_cols, axis=1)
        dvn = dvn_ref[...]
        dlg_ref[...] += _colsum(dvn * vhat)
        dlb_ref[...] += _colsum(dvn)
        dvh = dvn * lng_ref[...]
        m1 = jnp.mean(dvh, axis=-1, keepdims=True)
        m2 = jnp.mean(dvh * vhat, axis=-1, keepdims=True)
        dv = rstd * (dvh - m1 - vhat * m2)
        dpv = dv * _gelu_grad(pv)
        dpre_ref[:, w:] = dpv.astype(BF16)
        dbin_ref[:, w:] += _colsum(dpv)

    full3 = pl.BlockSpec((GM_GROUPS, GM_CHUNK, GM_CHUNK), lambda i: (0, 0, 0))
    bst = pl.BlockSpec((GM_CHUNK, GM_GROUPS), lambda i: (0, 0))
    return _run(beside, body, "gm_spatial_bwd",
                (_sds((t, w2), BF16), _sds((GM_GROUPS, GM_CHUNK, GM_CHUNK), F32), _sds((GM_CHUNK, GM_GROUPS), F32),
                 _sds((1, w), F32), _sds((1, w), F32), _sds((1, w2), F32)),
                (pre, dyv, lng, lnb, w_s, b_st), (t // tm,),
                [_tile_spec(tm, w2), _tile_spec(tm, w), _vec_spec(w), _vec_spec(w), full3, bst],
                (_tile_spec(tm, w2), full3, bst, _vec_spec(w), _vec_spec(w), _vec_spec(w2)),
                scratch=[pltpu.VMEM((tm, w), F32)], vmem=VMEM_BIG)


def _head_masks():
    lane = lax.broadcasted_iota(jnp.int32, (1, LANES), 1)
    return lane < HEAD_DIM


def _two_heads(x, m0):
    z = jnp.zeros_like(x)
    return jnp.where(m0, x, z), jnp.where(m0, z, x)


def _transposed(pair):
    return tuple(x.astype(F32).T.astype(BF16) for x in pair)


def _store_transposed(dqkv_ref, dk_acc, dv_acc, nq, tq):
    for c in range(nq):
        cols = slice(c * tq, (c + 1) * tq)
        dqkv_ref[1, cols, :] = dk_acc[:, cols].T.astype(BF16)
        dqkv_ref[2, cols, :] = dv_acc[:, cols].T.astype(BF16)


def _qkv_specs(seq, nq, blocked_q):
    if blocked_q:
        q = pl.BlockSpec((None, TQ_(seq), LANES), lambda b, p, i: (0, b * nq + i, p))
        k = pl.BlockSpec((None, seq, LANES), lambda b, p, i: (1, b, p))
        v = pl.BlockSpec((None, seq, LANES), lambda b, p, i: (2, b, p))
    else:
        q = pl.BlockSpec((None, seq, LANES), lambda b, p: (0, b, p))
        k = pl.BlockSpec((None, seq, LANES), lambda b, p: (1, b, p))
        v = pl.BlockSpec((None, seq, LANES), lambda b, p: (2, b, p))
    return q, k, v


def TQ_(seq):
    return min(TQ, seq)


def _fox_gate_fwd(ft, b_f, seq):
    nh, t = ft.shape
    nch = seq // LANES

    def body(ft_ref, bf_ref, fr_ref):
        r = lax.broadcasted_iota(jnp.int32, (LANES, LANES), 0)
        c = lax.broadcasted_iota(jnp.int32, (LANES, LANES), 1)
        upper = jnp.where(r <= c, 1.0, 0.0).astype(BF16)
        carry = jnp.zeros((nh, 1), F32)
        for ci in range(nch):
            cols = slice(ci * LANES, (ci + 1) * LANES)
            lf = _log_sigmoid(ft_ref[:, cols] + bf_ref[...])
            cs = _dot_exact(lf, upper) + carry
            fr_ref[:, cols] = cs
            carry = cs[:, LANES - 1:LANES]

    return _call(body, "fox_gate_fwd", _sds((nh, t), F32), grid=(t // seq,),
                 in_specs=[pl.BlockSpec((nh, seq), lambda b: (0, b)), pl.BlockSpec((nh, 1), lambda b: (0, 0))],
                 out_specs=pl.BlockSpec((nh, seq), lambda b: (0, b)))(ft, b_f)


def _fox_gate_bwd(ft, b_f, dfk, dfq, seq):
    nh, t = ft.shape
    nch = seq // LANES

    def body(ft_ref, bf_ref, dfk_ref, dfq_ref, dl_ref, db_ref):
        @pl.when(pl.program_id(0) == 0)
        def _():
            db_ref[...] = jnp.zeros_like(db_ref)

        r = lax.broadcasted_iota(jnp.int32, (LANES, LANES), 0)
        c = lax.broadcasted_iota(jnp.int32, (LANES, LANES), 1)
        lower = jnp.where(r >= c, 1.0, 0.0).astype(BF16)
        carry = jnp.zeros((nh, 1), F32)
        tot = jnp.zeros((nh, 1), F32)
        for ci in reversed(range(nch)):
            cols = slice(ci * LANES, (ci + 1) * LANES)
            rc = _dot_exact(dfk_ref[:, cols] + dfq_ref[:, cols], lower) + carry
            carry = rc[:, 0:1]
            dl = rc * jax.nn.sigmoid(-(ft_ref[:, cols] + bf_ref[...]))
            dl_ref[:, cols] = dl
            tot = tot + jnp.sum(dl, axis=-1, keepdims=True)
        db_ref[...] += tot

    blk = pl.BlockSpec((nh, seq), lambda b: (0, b))
    one = pl.BlockSpec((nh, 1), lambda b: (0, 0))
    return _call(body, "fox_gate_bwd", (_sds((nh, t), F32), _sds((nh, 1), F32)), grid=(t // seq,),
                 in_specs=[blk, one, blk, blk], out_specs=(blk, one))(ft, b_f, dfk, dfq)


def _sweep(step, n_off, unroll, init, start=0):
    def group(_, st):
        base, carry = st[0], st[1:]
        for u in range(unroll):
            carry = step(base + u, carry, False)
        return (base + unroll, *carry)

    def tail(r):
        def run(st):
            base, carry = st[0], st[1:]
            for u in range(r):
                carry = step(base + u, carry, False)
            return step(base + r, carry, True)
        return run

    def pick(idx, fns, st):
        if len(fns) == 1:
            return fns[0](st)
        half = len(fns) // 2
        return lax.cond(idx < half, lambda s: pick(idx, fns[:half], s), lambda s: pick(idx - half, fns[half:], s), st)

    st = lax.fori_loop(0, n_off // unroll, group, (jnp.int32(0) + start, *init))
    return pick(n_off % unroll, [tail(r) for r in range(unroll)], st)


def _carried(n_comm, n_in, n_out, refs):
    own_in, send = refs[:n_in], refs[n_in:n_in + n_comm]
    rest = refs[n_in + n_comm:]
    n_sem = 3 if n_comm else 0
    return own_in, send, rest[:n_out], rest[n_out:n_out + n_comm], rest[n_out + n_comm:n_out + n_comm + n_sem], rest[n_out + n_comm + n_sem:]


def _fox_fwd(qkv, frow, nb, seq, gather=()):
    _, t, d = qkv.shape
    npair = d // LANES
    tq = TQ_(seq)
    nq = seq // tq
    scale = HEAD_DIM ** -0.5
    n_comm = len(gather)
    total = nb * npair * nq

    def body(*refs):
        (q_ref, k_ref, v_ref, fr_ref), send, (o_ref, lse_ref), land, sems, _ = _carried(n_comm, 4, 2, refs)
        at = (pl.program_id(0) * npair + pl.program_id(1)) * nq + pl.program_id(2)
        if n_comm:
            for stage, when in ((0, 0), (1, (3 * total) // 4)):
                @pl.when(at == when)
                def _(stage=stage):
                    _gather_stage(stage, send, land, *sems)

        m0 = _head_masks()
        qm = _two_heads(q_ref[...] * scale, m0)
        row = lax.broadcasted_iota(jnp.int32, (tq, tq), 0)
        col = lax.broadcasted_iota(jnp.int32, (tq, tq), 1)
        one = jnp.ones((tq, LANES), BF16)

        def step(j, carry, diag):
            off = pl.multiple_of(j * tq, tq)
            kb = k_ref[pl.ds(off, tq), :]
            vb = v_ref[pl.ds(off, tq), :]
            vv = (jnp.where(m0, vb, one), jnp.where(m0, one, vb))
            out = []
            for hh in range(2):
                m, acc = carry[2 * hh], carry[2 * hh + 1]
                s = lax.dot_general(qm[hh], kb, NT, preferred_element_type=F32) - fr_ref[hh:hh + 1, pl.ds(off, tq)]
                if diag:
                    s = jnp.where(col <= row, s, NEG_INF)
                mn = jnp.maximum(m, jnp.max(s, axis=-1, keepdims=True))
                p = jnp.exp(s - mn)
                out += [mn, jnp.exp(m - mn) * acc + jnp.dot(p.astype(BF16), vv[hh], preferred_element_type=F32)]
            return tuple(out)

        neg = jnp.full((tq, 1), NEG_INF, F32)
        zacc = jnp.zeros((tq, LANES), F32)
        m_a, acc_a, m_b, acc_b = _sweep(step, pl.program_id(2), FOX_FWD_UNROLL, (neg, zacc, neg, zacc))
        l_a = pltpu.roll(acc_a, HEAD_DIM, 1)
        l_b = pltpu.roll(acc_b, HEAD_DIM, 1)
        o_ref[...] = jnp.where(m0, acc_a / l_a, acc_b / l_b)
        lse_ref[:, 0:1] = m_a + jnp.log(l_a[:, 0:1])
        lse_ref[:, 1:2] = m_b + jnp.log(l_b[:, HEAD_DIM:HEAD_DIM + 1])
        if n_comm:
            @pl.when(at == total - 1)
            def _():
                _gather_stage(2, send, land, *sems)

    q_spec, k_spec, v_spec = _qkv_specs(seq, nq, True)
    col_spec = pl.BlockSpec((None, tq, 2), lambda b, p, i: (p, b * nq + i, 0))
    hbm = pl.BlockSpec(memory_space=pl.ANY)
    return _call(body, "fox_fwd", (_sds((t, d), F32), _sds((npair, t, 2), F32), *_gathered_shapes(gather)), grid=(nb, npair, nq),
                 in_specs=[q_spec, k_spec, v_spec, pl.BlockSpec((None, 2, seq), lambda b, p, i: (p, 0, b))] + [hbm] * n_comm,
                 out_specs=(pl.BlockSpec((tq, LANES), lambda b, p, i: (b * nq + i, p)), col_spec, *[hbm] * n_comm),
                 scratch=_comm_scratch(n_comm) if n_comm else (), vmem=VMEM_BIG)(qkv, qkv, qkv, frow, *gather)


def _scatter_beside(stage, n_comm, nb, npair, send, land, sems):
    if n_comm:
        at = pl.program_id(0) * npair + pl.program_id(1)

        @pl.when(at == (0 if stage == 0 else nb * npair - 1))
        def _():
            _scatter_stage(stage, send, land, *sems)


def _fox_bwd(qkv, frow, o, do, lse, nb, seq, scatter=()):
    _, t, d = qkv.shape
    npair = d // LANES
    tq = TQ_(seq)
    nq = seq // tq
    scale = HEAD_DIM ** -0.5
    n_comm = len(scatter)

    def body(*refs):
        own_in, send, (dqkv_ref, df_ref, dfq_ref), land, sems, (dk_acc, dv_acc) = _carried(n_comm, 7, 3, refs)
        q_ref, k_ref, v_ref, fr_ref, o_ref, do_ref, lse_ref = own_in
        _scatter_beside(0, n_comm, nb, npair, send, land, sems)
        m0 = _head_masks()
        row = lax.broadcasted_iota(jnp.int32, (tq, tq), 0)
        col = lax.broadcasted_iota(jnp.int32, (tq, tq), 1)
        dk_acc[...] = jnp.zeros_like(dk_acc)
        dv_acc[...] = jnp.zeros_like(dv_acc)
        df_ref[...] = jnp.zeros_like(df_ref)

        def q_block(i, _):
            qoff = pl.multiple_of(i * tq, tq)
            qrows = pl.ds(qoff, tq)
            qm = _two_heads(q_ref[qrows, :] * scale, m0)
            dov = do_ref[qrows, :]
            dd = dov * o_ref[qrows, :]
            dm = _two_heads(dov.astype(BF16), m0)
            qmt = _transposed(qm)
            dmt = _transposed(dm)
            delta = (jnp.sum(jnp.where(m0, dd, 0.0), axis=-1, keepdims=True),
                     jnp.sum(jnp.where(m0, 0.0, dd), axis=-1, keepdims=True))
            ls = (lse_ref[qrows, 0:1], lse_ref[qrows, 1:2])

            def step(j, carry, diag):
                off = pl.multiple_of(j * tq, tq)
                krows = pl.ds(off, tq)
                kb = k_ref[krows, :]
                vb = v_ref[krows, :]
                dqs, rowsums = [], []
                dk = jnp.zeros((LANES, tq), F32)
                dv = jnp.zeros((LANES, tq), F32)
                for hh in range(2):
                    s = lax.dot_general(qm[hh], kb, NT, preferred_element_type=F32) - fr_ref[hh:hh + 1, krows]
                    if diag:
                        s = jnp.where(col <= row, s, NEG_INF)
                    p = jnp.exp(s - ls[hh])
                    dp = lax.dot_general(dm[hh], vb, NT, preferred_element_type=F32)
                    ds = p * (dp - delta[hh])
                    df_ref[hh:hh + 1, krows] -= _colsum(ds)
                    rowsums.append(carry[1 + hh] + jnp.sum(ds, axis=-1, keepdims=True))
                    ds_b = ds.astype(BF16)
                    dqs.append(jnp.dot(ds_b, kb, preferred_element_type=F32))
                    dk = dk + jnp.dot(qmt[hh], ds_b, preferred_element_type=F32)
                    dv = dv + jnp.dot(dmt[hh], p.astype(BF16), preferred_element_type=F32)
                dk_acc[:, krows] += dk
                dv_acc[:, krows] += dv
                return (carry[0] + jnp.where(m0, dqs[0], dqs[1]), *rowsums)

            zero = jnp.zeros((tq, 1), F32)
            dq, rs_a, rs_b = _sweep(step, i, FOX_BWD_UNROLL, (jnp.zeros((tq, LANES), F32), zero, zero))
            dqkv_ref[0, qrows, :] = (dq * scale).astype(BF16)
            dfq_ref[qrows, 0:1] = rs_a
            dfq_ref[qrows, 1:2] = rs_b
            return 0

        lax.fori_loop(0, nq, q_block, 0)
        _store_transposed(dqkv_ref, dk_acc, dv_acc, nq, tq)
        _scatter_beside(1, n_comm, nb, npair, send, land, sems)

    q_spec, k_spec, v_spec = _qkv_specs(seq, nq, False)
    col_spec = pl.BlockSpec((None, seq, 2), lambda b, p: (p, b, 0))
    row_spec = pl.BlockSpec((None, 2, seq), lambda b, p: (p, 0, b))
    tile = pl.BlockSpec((seq, LANES), lambda b, p: (b, p))
    hbm = pl.BlockSpec(memory_space=pl.ANY)
    return _call(body, "fox_bwd", (_sds((3, t, d), BF16), _sds((npair, 2, t), F32), _sds((npair, t, 2), F32), *_scattered_shapes(scatter)),
                 grid=(nb, npair),
                 in_specs=[q_spec, k_spec, v_spec, row_spec, tile, tile, col_spec] + [hbm] * n_comm,
                 out_specs=(pl.BlockSpec((3, seq, LANES), lambda b, p: (0, b, p)), row_spec, col_spec, *[hbm] * n_comm),
                 scratch=(_comm_scratch(n_comm) if n_comm else []) + [pltpu.VMEM((LANES, seq), F32), pltpu.VMEM((LANES, seq), F32)],
                 vmem=VMEM_BIG)(qkv, qkv, qkv, frow, o, do, lse, *scatter)


def _split2(x):
    hi = x.astype(BF16)
    return hi, (x - hi.astype(F32)).astype(BF16)


def _sum_right(x, tri):
    hi, lo = _split2(x)
    return jnp.dot(hi, tri, preferred_element_type=F32) + jnp.dot(lo, tri, preferred_element_type=F32)


def _sb_scores(qm_h, kb, mask):
    z = lax.dot_general(qm_h, kb, NT, preferred_element_type=F32)
    lb = _log_sigmoid(z)
    l1m = lb - z
    if mask is not None:
        l1m = jnp.where(mask, l1m, 0.0)
    return lb, l1m


def _sb_fwd(qkv, nb, seq):
    _, t, d = qkv.shape
    npair = d // LANES
    tq = TQ_(seq)
    nq = seq // tq
    scale = HEAD_DIM ** -0.5

    def body(q_ref, k_ref, v_ref, o_ref, lt_ref):
        i = pl.program_id(2)
        m0 = _head_masks()
        qm = _two_heads(q_ref[...] * scale, m0)
        row = lax.broadcasted_iota(jnp.int32, (tq, tq), 0)
        col = lax.broadcasted_iota(jnp.int32, (tq, tq), 1)
        after = jnp.where(row > col, 1.0, 0.0).astype(BF16)

        def step(j, carry, diag):
            off = pl.multiple_of(j * tq, tq)
            kb = k_ref[pl.ds(off, tq), :]
            vb = v_ref[pl.ds(off, tq), :]
            mask = (col < row) if diag else None
            nxt, parts = [], []
            for hh in range(2):
                lb, l1m = _sb_scores(qm[hh], kb, mask)
                rest = _sum_right(l1m, after) + carry[hh]
                a = jnp.exp(lb + rest)
                if diag:
                    a = jnp.where(mask, a, 0.0)
                parts.append(jnp.dot(a.astype(BF16), vb, preferred_element_type=F32))
                nxt.append(carry[hh] + jnp.sum(l1m, axis=-1, keepdims=True))
            return (*nxt, carry[2] + jnp.where(m0, parts[0], parts[1]))

        zero = jnp.zeros((tq, 1), F32)
        init = (zero, zero, jnp.zeros((tq, LANES), F32))
        carry = lax.cond(i > 0, lambda c: step(i - 1, step(i, c, True), False), lambda c: step(i, c, True), init)

        def alive(st):
            return (st[0] < i) & (jnp.max(jnp.maximum(st[1], st[2])) > SB_DEAD)

        def more(st):
            return (st[0] + 1, *step(i - 1 - st[0], st[1:], False))

        done, lt_a, lt_b, acc = lax.while_loop(alive, more, (jnp.minimum(i, 1), *carry))
        o_ref[...] = acc
        lt_ref[:, 0:1] = lt_a
        lt_ref[:, 1:2] = lt_b
        lt_ref[:, 2:3] = jnp.zeros((tq, 1), F32) + done.astype(F32)
        lt_ref[:, 3:4] = zero

    q_spec, k_spec, v_spec = _qkv_specs(seq, nq, True)
    return _call(body, "sb_fwd", (_sds((t, d), F32), _sds((npair, t, 4), F32)), grid=(nb, npair, nq),
                 in_specs=[q_spec, k_spec, v_spec],
                 out_specs=(pl.BlockSpec((tq, LANES), lambda b, p, i: (b * nq + i, p)),
                            pl.BlockSpec((None, tq, 4), lambda b, p, i: (p, b * nq + i, 0))), vmem=VMEM_BIG)(qkv, qkv, qkv)


def _sb_bwd(qkv, do, ltot, nb, seq, scatter=()):
    _, t, d = qkv.shape
    npair = d // LANES
    tq = TQ_(seq)
    nq = seq // tq
    scale = HEAD_DIM ** -0.5
    n_comm = len(scatter)

    def body(*refs):
        (q_ref, k_ref, v_ref, do_ref, lt_ref), send, (dqkv_ref,), land, sems, (dk_acc, dv_acc) = _carried(n_comm, 5, 1, refs)
        _scatter_beside(0, n_comm, nb, npair, send, land, sems)
        m0 = _head_masks()
        row = lax.broadcasted_iota(jnp.int32, (tq, tq), 0)
        col = lax.broadcasted_iota(jnp.int32, (tq, tq), 1)
        upto = jnp.where(row <= col, 1.0, 0.0).astype(BF16)
        left_of = jnp.where(row < col, 1.0, 0.0).astype(BF16)
        dk_acc[...] = jnp.zeros_like(dk_acc)
        dv_acc[...] = jnp.zeros_like(dv_acc)

        def q_block(i, _):
            qoff = pl.multiple_of(i * tq, tq)
            qrows = pl.ds(qoff, tq)
            qm = _two_heads(q_ref[qrows, :] * scale, m0)
            dm = _two_heads(do_ref[qrows, :].astype(BF16), m0)
            qmt = _transposed(qm)
            dmt = _transposed(dm)
            ltot = (lt_ref[qrows, 0:1], lt_ref[qrows, 1:2])

            def step(j, carry, diag):
                off = pl.multiple_of(j * tq, tq)
                krows = pl.ds(off, tq)
                kb = k_ref[krows, :]
                vb = v_ref[krows, :]
                mask = (col < row) if diag else None
                nxt, dqs = [], []
                dk = jnp.zeros((LANES, tq), F32)
                dv = jnp.zeros((LANES, tq), F32)
                for hh in range(2):
                    cl, ce = carry[2 * hh], carry[2 * hh + 1]
                    lb, l1m = _sb_scores(qm[hh], kb, mask)
                    a = jnp.exp(lb + (ltot[hh] - (_sum_right(l1m, upto) + cl)))
                    if diag:
                        a = jnp.where(mask, a, 0.0)
                    e = lax.dot_general(dm[hh], vb, NT, preferred_element_type=F32) * a
                    before = _sum_right(e, left_of) + ce
                    beta = jnp.exp(lb)
                    dz = e * (1.0 - beta) - before * beta
                    if diag:
                        dz = jnp.where(mask, dz, 0.0)
                    dz_b = dz.astype(BF16)
                    dqs.append(jnp.dot(dz_b, kb, preferred_element_type=F32))
                    dk = dk + jnp.dot(qmt[hh], dz_b, preferred_element_type=F32)
                    dv = dv + jnp.dot(dmt[hh], a.astype(BF16), preferred_element_type=F32)
                    nxt += [cl + jnp.sum(l1m, axis=-1, keepdims=True), ce + jnp.sum(e, axis=-1, keepdims=True)]
                dk_acc[:, krows] += dk
                dv_acc[:, krows] += dv
                return (*nxt, carry[4] + jnp.where(m0, dqs[0], dqs[1]))

            zero = jnp.zeros((tq, 1), F32)
            visited = jnp.max(lt_ref[qrows, 2:3]).astype(jnp.int32)
            carry = _sweep(step, visited, SB_BWD_UNROLL, (zero, zero, zero, zero, jnp.zeros((tq, LANES), F32)), start=i - visited)
            dqkv_ref[0, qrows, :] = (carry[4] * scale).astype(BF16)
            return 0

        lax.fori_loop(0, nq, q_block, 0)
        _store_transposed(dqkv_ref, dk_acc, dv_acc, nq, tq)
        _scatter_beside(1, n_comm, nb, npair, send, land, sems)

    q_spec, k_spec, v_spec = _qkv_specs(seq, nq, False)
    tile = pl.BlockSpec((seq, LANES), lambda b, p: (b, p))
    hbm = pl.BlockSpec(memory_space=pl.ANY)
    return _call(body, "sb_bwd", (_sds((3, t, d), BF16), *_scattered_shapes(scatter)), grid=(nb, npair),
                 in_specs=[q_spec, k_spec, v_spec, tile, pl.BlockSpec((None, seq, 4), lambda b, p: (p, b, 0))] + [hbm] * n_comm,
                 out_specs=(pl.BlockSpec((3, seq, LANES), lambda b, p: (0, b, p)), *[hbm] * n_comm),
                 scratch=(_comm_scratch(n_comm) if n_comm else []) + [pltpu.VMEM((LANES, seq), F32), pltpu.VMEM((LANES, seq), F32)],
                 vmem=VMEM_BIG)(qkv, qkv, qkv, do, ltot, *scatter)


def _qkv_proj(name, h, w):
    t, d = h.shape
    tm = min(TMM, t)
    return _mm(name, h, w, _sds((3, t, d), BF16), (3, t // tm, 1),
               pl.BlockSpec((tm, d), lambda s, i, r: (i, 0)), pl.BlockSpec((d, d), lambda s, i, r: (0, s)),
               pl.BlockSpec((None, tm, d), lambda s, i, r: (s, i, 0)), NN)


def _qkv_dw(name, h, dqkv):
    t, d = h.shape
    tk = min(TMM, t)

    def terms(h_ref, g_ref):
        hv = h_ref[...]
        return [_dot(hv, g_ref[s], TN) for s in range(3)]

    def store(o_ref, acc_ref):
        for s in range(3):
            o_ref[:, s * d:(s + 1) * d] = acc_ref[s].astype(o_ref.dtype)

    return _accumulate_over_tokens(name, (h, dqkv), [pl.BlockSpec((tk, d), lambda r: (r, 0)), pl.BlockSpec((3, tk, d), lambda r: (0, r, 0))],
                                   _sds((d, 3 * d), GRAD_WIRE), pl.BlockSpec((d, 3 * d), lambda r: (0, 0)), (3, d, d), t // tk, terms, store)


def _qkv_dh(name, dqkv, w):
    _, t, d = dqkv.shape
    tm = min(TM, t)

    def body(g_ref, w_ref, o_ref):
        acc = None
        for s in range(3):
            term = _dot(g_ref[s], w_ref[:, s * d:(s + 1) * d], NT)
            acc = term if acc is None else acc + term
        o_ref[...] = acc

    return _call(body, name, _sds((t, d), F32), grid=(t // tm,),
                 in_specs=[pl.BlockSpec((3, tm, d), lambda i: (0, i, 0)), pl.BlockSpec((d, 3 * d), lambda i: (0, 0))],
                 out_specs=pl.BlockSpec((tm, d), lambda i: (i, 0)), vmem=VMEM_BIG)(dqkv, w)


def _glu(pre_block, d):
    return pre_block[:, :d] * jax.nn.sigmoid(pre_block[:, d:])


def _shifted_copies(ext_ref, sh_ref, tt):
    for r in range(1, SUBLANES):
        sh_ref[r - 1] = ext_ref[pl.ds(r, tt + CONV_HALO - SUBLANES), :]


def _rows_from(ext_ref, sh_ref, base, offset, n):
    q, r = divmod(offset, SUBLANES)
    if r == 0:
        return ext_ref[pl.ds(pl.multiple_of(base + offset, SUBLANES), n), :]
    return sh_ref[r - 1, pl.ds(pl.multiple_of(base + q * SUBLANES, SUBLANES), n), :]


def _cv_conv_fwd(pre, dw, dwb, lng, lnb, seq):
    t, d2 = pre.shape
    d = d2 // 2
    tt = min(TT, seq)
    nt = seq // tt
    hb = tt // CONV_HALO

    def body(pre_ref, halo_ref, dw_ref, dwb_ref, lng_ref, lnb_ref, y1_ref, y2_ref, ext_ref, sh_ref):
        i = pl.program_id(1)
        ext_ref[0:CONV_HALO, :] = jnp.where(i == 0, 0.0, _glu(halo_ref[...], d))
        ext_ref[CONV_HALO:, :] = _glu(pre_ref[...], d)
        _shifted_copies(ext_ref, sh_ref, tt)

        acc = jnp.zeros((tt, d), F32) + dwb_ref[...]
        for k in range(CONV_WIDTH):
            acc = acc + _rows_from(ext_ref, sh_ref, 0, CONV_HALO - (CONV_WIDTH - 1) + k, tt) * dw_ref[k:k + 1, :]
        y1_ref[...] = acc
        yc, rstd = _ln_stats(acc)
        y2_ref[...] = _silu(yc * rstd * lng_ref[...] + lnb_ref[...]).astype(BF16)

    vec = pl.BlockSpec((1, d), lambda b, i: (0, 0))
    tile = pl.BlockSpec((tt, d), lambda b, i: (b * nt + i, 0))
    return _call(body, "cv_conv_fwd", (_sds((t, d), F32), _sds((t, d), BF16)), grid=(t // seq, nt),
                 in_specs=[pl.BlockSpec((tt, d2), lambda b, i: (b * nt + i, 0)),
                           pl.BlockSpec((CONV_HALO, d2), lambda b, i: (jnp.maximum((b * nt + i) * hb - 1, 0), 0)),
                           pl.BlockSpec((CONV_HALO, d), lambda b, i: (0, 0)), vec, vec, vec],
                 out_specs=(tile, tile),
                 scratch=[pltpu.VMEM((tt + CONV_HALO, d), F32), pltpu.VMEM((SUBLANES - 1, tt + CONV_HALO - SUBLANES, d), F32)],
                 vmem=VMEM_BIG)(pre, pre, dw, dwb, lng, lnb)


def _cv_norm_bwd(y1, dy2, lng, lnb):
    t, d = y1.shape
    tm = min(TM, t)

    def body(y1_ref, dy2_ref, lng_ref, lnb_ref, dy1_ref, dlg_ref, dlb_ref, dsum_ref):
        @pl.when(pl.program_id(0) == 0)
        def _():
            dlg_ref[...] = jnp.zeros_like(dlg_ref)
            dlb_ref[...] = jnp.zeros_like(dlb_ref)
            dsum_ref[...] = jnp.zeros_like(dsum_ref)

        yc, rstd = _ln_stats(y1_ref[...])
        yhat = yc * rstd
        n = yhat * lng_ref[...] + lnb_ref[...]
        sg = jax.nn.sigmoid(n)
        dn = dy2_ref[...] * (sg * (1.0 + n * (1.0 - sg)))
        dlg_ref[...] += _colsum(dn * yhat)
        dlb_ref[...] += _colsum(dn)
        dyh = dn * lng_ref[...]
        m1 = jnp.mean(dyh, axis=-1, keepdims=True)
        m2 = jnp.mean(dyh * yhat, axis=-1, keepdims=True)
        dy1 = rstd * (dyh - m1 - yhat * m2)
        dy1_ref[...] = dy1
        dsum_ref[...] += _colsum(dy1)

    return _call(body, "cv_norm_bwd", (_sds((t, d), F32), _sds((1, d), F32), _sds((1, d), F32), _sds((1, d), F32)),
                 grid=(t // tm,), in_specs=[_tile_spec(tm, d), _tile_spec(tm, d), _vec_spec(d), _vec_spec(d)],
                 out_specs=(_tile_spec(tm, d), _vec_spec(d), _vec_spec(d), _vec_spec(d)))(y1, dy2, lng, lnb)


def _cv_conv_bwd(pre, dy1, dw, seq):
    t, d2 = pre.shape
    d = d2 // 2
    tt = min(TT // 2, seq)
    nt = seq // tt
    hb = tt // CONV_HALO
    last_halo = t // CONV_HALO - 1

    def body(pre_ref, halo_ref, dy_ref, dyn_ref, dw_ref, dpre_ref, ddw_ref, dbin_ref, ext_ref, dext_ref, sh_ref, dsh_ref, taps_ref):
        b, i = pl.program_id(0), pl.program_id(1)

        @pl.when((b == 0) & (i == 0))
        def _():
            ddw_ref[...] = jnp.zeros_like(ddw_ref)
            dbin_ref[...] = jnp.zeros_like(dbin_ref)

        pv = pre_ref[...]
        ext_ref[0:CONV_HALO, :] = jnp.where(i == 0, 0.0, _glu(halo_ref[...], d))
        ext_ref[CONV_HALO:, :] = _glu(pv, d)
        dyv = dy_ref[...]
        dext_ref[0:tt, :] = dyv
        dext_ref[tt:, :] = jnp.where(i == nt - 1, 0.0, dyn_ref[...])
        _shifted_copies(ext_ref, sh_ref, tt)
        _shifted_copies(dext_ref, dsh_ref, tt)
        nrows = tt // CONV_ROWS

        def input_grad(c, _):
            r0 = pl.multiple_of(c * CONV_ROWS, CONV_ROWS)
            dy0 = jnp.zeros((CONV_ROWS, d), F32)
            for k in range(CONV_WIDTH):
                dy0 = dy0 + _rows_from(dext_ref, dsh_ref, r0, CONV_WIDTH - 1 - k, CONV_ROWS) * dw_ref[k:k + 1, :]
            rows = pl.ds(r0, CONV_ROWS)
            a = pre_ref[rows, :d]
            sg = jax.nn.sigmoid(pre_ref[rows, d:])
            da = dy0 * sg
            dg = dy0 * a * sg * (1.0 - sg)
            dpre_ref[rows, :d] = da.astype(BF16)
            dpre_ref[rows, d:] = dg.astype(BF16)
            dbin_ref[:, :d] += _colsum(da)
            dbin_ref[:, d:] += _colsum(dg)
            return 0

        lax.fori_loop(0, nrows, input_grad, 0)

        @pl.when((b == 0) & (i == 0))
        def _():
            taps_ref[...] = jnp.zeros_like(taps_ref)

        def tap_grads(c, _):
            r0 = pl.multiple_of(c * CONV_ROWS, CONV_ROWS)
            dyc = dy_ref[pl.ds(r0, CONV_ROWS), :]
            for k in range(CONV_WIDTH):
                prod = dyc * _rows_from(ext_ref, sh_ref, r0, CONV_HALO - (CONV_WIDTH - 1) + k, CONV_ROWS)
                taps_ref[k] += jnp.sum(prod.reshape(CONV_ROWS // SUBLANES, SUBLANES, d), axis=0)
            return 0

        lax.fori_loop(0, nrows, tap_grads, 0)

        @pl.when((b == t // seq - 1) & (i == nt - 1))
        def _():
            for k in range(CONV_WIDTH):
                ddw_ref[k:k + 1, :] = _colsum(taps_ref[k])

    return _call(body, "cv_conv_bwd", (_sds((t, d2), BF16), _sds((CONV_HALO, d), F32), _sds((1, d2), F32)), grid=(t // seq, nt),
                 in_specs=[pl.BlockSpec((tt, d2), lambda b, i: (b * nt + i, 0)),
                           pl.BlockSpec((CONV_HALO, d2), lambda b, i: (jnp.maximum((b * nt + i) * hb - 1, 0), 0)),
                           pl.BlockSpec((tt, d), lambda b, i: (b * nt + i, 0)),
                           pl.BlockSpec((CONV_HALO, d), lambda b, i: (jnp.minimum((b * nt + i + 1) * hb, last_halo), 0)),
                           pl.BlockSpec((CONV_HALO, d), lambda b, i: (0, 0))],
                 out_specs=(pl.BlockSpec((tt, d2), lambda b, i: (b * nt + i, 0)),
                            pl.BlockSpec((CONV_HALO, d), lambda b, i: (0, 0)), pl.BlockSpec((1, d2), lambda b, i: (0, 0))),
                 scratch=[pltpu.VMEM((tt + CONV_HALO, d), F32), pltpu.VMEM((tt + CONV_HALO, d), F32),
                          pltpu.VMEM((SUBLANES - 1, tt + CONV_HALO - SUBLANES, d), F32),
                          pltpu.VMEM((SUBLANES - 1, tt + CONV_HALO - SUBLANES, d), F32),
                          pltpu.VMEM((CONV_HALO, SUBLANES, d), F32)],
                 vmem=VMEM_BIG)(pre, pre, dy1, dy1, dw)


def _adamw(name, w, m, v, g=None, parts=None):
    rows, cols = w.shape
    nseg = 1 if parts is None else len(parts)
    seg_rows = rows // nseg
    tr = seg_rows
    for cand in ((512,) if parts is None else ()) + (256, 128, 64, 32, 16, 8):
        if seg_rows % cand == 0 and seg_rows > cand:
            tr = cand
            break
    tps = seg_rows // tr
    bc1 = 1.0 - ADAM_B1 ** ADAM_STEP
    bc2 = 1.0 - ADAM_B2 ** ADAM_STEP

    def body(w_ref, m_ref, v_ref, *rest):
        g_refs, (go_ref, d_ref, mo_ref, vo_ref) = rest[:nseg], rest[nseg:]

        def update(gv):
            mn = ADAM_B1 * m_ref[...] + (1.0 - ADAM_B1) * gv
            vn = ADAM_B2 * v_ref[...] + (1.0 - ADAM_B2) * (gv * gv)
            m_hat = mn / bc1
            v_hat = vn / bc2
            go_ref[...] = gv
            d_ref[...] = -ADAM_LR * (m_hat / (jnp.sqrt(v_hat) + ADAM_EPS) + ADAM_WD * w_ref[...])
            mo_ref[...] = mn
            vo_ref[...] = vn

        if parts is None:
            update(g_refs[0][...])
        else:
            for k, g_ref in enumerate(g_refs):
                @pl.when(pl.program_id(0) == k)
                def _(g_ref=g_ref):
                    gv = g_ref[0].astype(F32)
                    for s in range(1, N_DEV):
                        gv = gv + g_ref[s].astype(F32)
                    update(gv)

    blk = pl.BlockSpec((tr, cols), lambda l, i: (l * tps + i, 0))
    if parts is None:
        g_ins, g_specs = [g], [blk]
    else:
        g_ins = list(parts)
        g_specs = [pl.BlockSpec((N_DEV, tr, cols), (lambda l, i, k=k: (0, jnp.where(l == k, i, 0), 0))) for k in range(nseg)]
    out = _sds((rows, cols), F32)
    return _call(body, name, (out, out, out, out), grid=(nseg, tps), in_specs=[blk, blk, blk, *g_specs],
                 out_specs=(blk, blk, blk, blk), vmem=VMEM_BIG)(w, m, v, *g_ins)


def _pad_rows(a, rows):
    return jnp.pad(a, ((0, rows - a.shape[0]), (0, 0)))


def _full_cols(gathered, n):
    k = gathered.shape[2]
    return jnp.transpose(gathered[0], (1, 0, 2)).reshape(k, N_DEV * n)


def _col_blocks(full, n):
    k = full.shape[0]
    return jnp.transpose(full.reshape(k, N_DEV, n), (1, 0, 2))[None]


def kernel(x, c, mod_w, mod_b, ln1_g, ln1_b, ln2_g, ln2_b, ffn_w_in, ffn_w_out, gm_w_in, gm_b_in, gm_ln_g, gm_ln_b, gm_w_s, gm_b_s, gm_w_out, fox_w_in, fox_b_f, fox_w_out, sb_w_in, sb_w_out, cv_w_in, cv_b_in, cv_dw, cv_dw_b, cv_ln_g, cv_ln_b, cv_w_out, cv_b_out, loss_target, m_mod_w, m_mod_b, m_ln1_g, m_ln1_b, m_ln2_g, m_ln2_b, m_ffn_w_in, m_ffn_w_out, m_gm_w_in, m_gm_b_in, m_gm_ln_g, m_gm_ln_b, m_gm_w_s, m_gm_b_s, m_gm_w_out, m_fox_w_in, m_fox_b_f, m_fox_w_out, m_sb_w_in, m_sb_w_out, m_cv_w_in, m_cv_b_in, m_cv_dw, m_cv_dw_b, m_cv_ln_g, m_cv_ln_b, m_cv_w_out, m_cv_b_out, v_mod_w, v_mod_b, v_ln1_g, v_ln1_b, v_ln2_g, v_ln2_b, v_ffn_w_in, v_ffn_w_out, v_gm_w_in, v_gm_b_in, v_gm_ln_g, v_gm_ln_b, v_gm_w_s, v_gm_b_s, v_gm_w_out, v_fox_w_in, v_fox_b_f, v_fox_w_out, v_sb_w_in, v_sb_w_out, v_cv_w_in, v_cv_b_in, v_cv_dw, v_cv_dw_b, v_cv_ln_g, v_cv_ln_b, v_cv_w_out, v_cv_b_out):
    weights = dict(mod_w=mod_w, mod_b=mod_b, ln1_g=ln1_g, ln1_b=ln1_b, ln2_g=ln2_g, ln2_b=ln2_b, ffn_w_in=ffn_w_in, ffn_w_out=ffn_w_out, gm_w_in=gm_w_in, gm_b_in=gm_b_in, gm_ln_g=gm_ln_g, gm_ln_b=gm_ln_b, gm_w_s=gm_w_s, gm_b_s=gm_b_s, gm_w_out=gm_w_out, fox_w_in=fox_w_in, fox_b_f=fox_b_f, fox_w_out=fox_w_out, sb_w_in=sb_w_in, sb_w_out=sb_w_out, cv_w_in=cv_w_in, cv_b_in=cv_b_in, cv_dw=cv_dw, cv_dw_b=cv_dw_b, cv_ln_g=cv_ln_g, cv_ln_b=cv_ln_b, cv_w_out=cv_w_out, cv_b_out=cv_b_out)
    mom1 = dict(mod_w=m_mod_w, mod_b=m_mod_b, ln1_g=m_ln1_g, ln1_b=m_ln1_b, ln2_g=m_ln2_g, ln2_b=m_ln2_b, ffn_w_in=m_ffn_w_in, ffn_w_out=m_ffn_w_out, gm_w_in=m_gm_w_in, gm_b_in=m_gm_b_in, gm_ln_g=m_gm_ln_g, gm_ln_b=m_gm_ln_b, gm_w_s=m_gm_w_s, gm_b_s=m_gm_b_s, gm_w_out=m_gm_w_out, fox_w_in=m_fox_w_in, fox_b_f=m_fox_b_f, fox_w_out=m_fox_w_out, sb_w_in=m_sb_w_in, sb_w_out=m_sb_w_out, cv_w_in=m_cv_w_in, cv_b_in=m_cv_b_in, cv_dw=m_cv_dw, cv_dw_b=m_cv_dw_b, cv_ln_g=m_cv_ln_g, cv_ln_b=m_cv_ln_b, cv_w_out=m_cv_w_out, cv_b_out=m_cv_b_out)
    mom2 = dict(mod_w=v_mod_w, mod_b=v_mod_b, ln1_g=v_ln1_g, ln1_b=v_ln1_b, ln2_g=v_ln2_g, ln2_b=v_ln2_b, ffn_w_in=v_ffn_w_in, ffn_w_out=v_ffn_w_out, gm_w_in=v_gm_w_in, gm_b_in=v_gm_b_in, gm_ln_g=v_gm_ln_g, gm_ln_b=v_gm_ln_b, gm_w_s=v_gm_w_s, gm_b_s=v_gm_b_s, gm_w_out=v_gm_w_out, fox_w_in=v_fox_w_in, fox_b_f=v_fox_b_f, fox_w_out=v_fox_w_out, sb_w_in=v_sb_w_in, sb_w_out=v_sb_w_out, cv_w_in=v_cv_w_in, cv_b_in=v_cv_b_in, cv_dw=v_cv_dw, cv_dw_b=v_cv_dw_b, cv_ln_g=v_cv_ln_g, cv_ln_b=v_cv_ln_b, cv_w_out=v_cv_w_out, cv_b_out=v_cv_b_out)
    names = list(weights)

    nb, seq, d = x.shape
    t = nb * seq
    nl = mod_w.shape[0]
    alpha = (2.0 * nl) ** 0.25
    me = 4 * lax.axis_index("x") + 2 * lax.axis_index("y") + lax.axis_index("c")
    xs = x.reshape(t, d)
    tgt = loss_target.reshape(t, d)
    n_mod = mod_w.shape[2]
    n_ffn = ffn_w_in.shape[2]
    n_heads = d // HEAD_DIM
    npair = d // LANES

    cvp = d // N_DEV
    cv_small = jnp.concatenate([_pad_rows(cv_dw[0], CONV_HALO), cv_dw_b, cv_ln_g, cv_ln_b, cv_b_out,
                                cv_b_in.reshape(2, cvp), jnp.zeros((2, cvp), F32)], axis=0)
    cv_packed = cv_small.reshape(-1, d)
    first = _exchange_small(jnp.concatenate([_pad_rows(c, 8), _pad_rows(cv_packed, 8)], axis=0), "gather_c", False)
    c_all = first[:, :nb].reshape(N_DEV * nb, d)
    cv_all = first[:, 8:8 + cv_packed.shape[0]].reshape(N_DEV, cv_small.shape[0], cvp)
    mod_b_loc = lax.dynamic_slice_in_dim(mod_b, me * n_mod, n_mod, axis=1)[:, None, :]
    mod_loc = _mod_fwd(c_all, mod_w, mod_b_loc)
    mod_g = _exchange_small(mod_loc.reshape(nl * N_DEV * nb, n_mod), "gather_mod", False)
    mod_all = jnp.transpose(mod_g.reshape(N_DEV, nl, N_DEV * nb, n_mod), (1, 2, 0, 3)).reshape(nl, N_DEV * nb, N_DEV * n_mod)
    mod_me = lax.dynamic_slice_in_dim(mod_all, me * nb, nb, axis=1)
    mods = [[mod_me[l, :, k * d:(k + 1) * d][:, None, :] for k in range(6)] for l in range(nl)]

    assert nl == 4, "the exchange schedule below is written for the four-layer trunk"
    big = ["ffn_w_in", "ffn_w_out", "gm_w_in", "gm_w_out", "fox_w_in", "fox_w_out", "sb_w_in", "sb_w_out", "cv_w_in", "cv_w_out"]
    shard = {n: weights[n].astype(BF16) for n in big if not n.startswith("ffn")}
    for l in range(nl):
        shard["ffn_w_in", l] = ffn_w_in[l:l + 1].astype(BF16)
        shard["ffn_w_out", l] = ffn_w_out[l:l + 1].astype(BF16)
    now = ["gm_w_in", "gm_w_out"]
    with_gm_in = [("ffn_w_in", 0)]
    with_gm_gate = [("ffn_w_out", 0)]
    with_ffn_0 = ["fox_w_in", "fox_w_out"]
    later = [("ffn_w_in", 1), ("ffn_w_out", 1), "sb_w_in", "sb_w_out", ("ffn_w_in", 2), ("ffn_w_out", 2),
             "cv_w_in", "cv_w_out", ("ffn_w_in", 3), ("ffn_w_out", 3)]
    gathered = dict(zip(now, _gather_weights([shard[n] for n in now])))
    w_ffn_out_rows = lambda l: gathered["ffn_w_out", l].reshape(N_DEV // 2, n_ffn, d)
    sq = lambda n: gathered[n].reshape(d, d)
    cv_rows = jnp.transpose(cv_all, (1, 0, 2)).reshape(cv_small.shape[0], d)
    cv_dw_f, cv_dwb_f, cv_lng_f, cv_lnb_f, cv_bout_f = (cv_rows[:CONV_HALO], cv_rows[32:33], cv_rows[33:34], cv_rows[34:35], cv_rows[35:36])
    cv_bin_f = cv_all[:, 36:38, :].reshape(1, 2 * d)

    saved = []
    h = _modulate(xs, mods[0][1], mods[0][0], seq)
    xin = xs
    for l in range(nl):
        kind = l % 4
        sv = dict(x=xin, h=h)
        if kind == 0:
            pre, arrived = _proj_cols("gm_in", h, gathered["gm_w_in"], gm_w_in.shape[2], bias=gm_b_in,
                                      beside=("gather", [shard[n] for n in with_gm_in]))
            gathered.update(zip(with_gm_in, arrived))
            yv, arrived = _gm_spatial_fwd(pre, gm_ln_g, gm_ln_b, gm_w_s[0], jnp.transpose(gm_b_s[0]), seq,
                                          beside=("gather", [shard[n] for n in with_gm_gate]))
            gathered.update(zip(with_gm_gate, arrived))
            y = _mm_plain("gm_out", yv, sq("gm_w_out"), NN)
            sv.update(pre=pre, yv=yv)
        elif kind == 1:
            qkv = _qkv_proj("fox_qkv", h, fox_qkv_w)
            ft = _mm("fox_gate_proj", fox_f_wt, h, _sds((n_heads, t), F32), (t // min(TMM, t), 1),
                     pl.BlockSpec((n_heads, d), lambda i, r: (0, 0)), pl.BlockSpec((min(TMM, t), d), lambda i, r: (i, 0)),
                     pl.BlockSpec((n_heads, min(TMM, t)), lambda i, r: (0, i)), NT)
            b_f = jnp.transpose(fox_b_f)
            frow_p = _fox_gate_fwd(ft, b_f, seq).reshape(npair, 2, t)
            o, lse, *arrived = _fox_fwd(qkv, frow_p, nb, seq, gather=[shard[n] for n in later])
            gathered.update(zip(later, arrived))
            sb_qkv_w = _full_cols(gathered["sb_w_in"], sb_w_in.shape[2])
            y = _mm_plain("fox_out", o, sq("fox_w_out"), NN)
            sv.update(qkv=qkv, ft=ft, b_f=b_f, frow=frow_p, o=o, lse=lse)
        elif kind == 2:
            qkv = _qkv_proj("sb_qkv", h, sb_qkv_w)
            o, ltot = _sb_fwd(qkv, nb, seq)
            y = _mm_plain("sb_out", o, sq("sb_w_out"), NN)
            sv.update(qkv=qkv, o=o, ltot=ltot)
        else:
            pre = _proj_cols("cv_in", h, gathered["cv_w_in"], cv_w_in.shape[2], bias=cv_bin_f)
            y1, y2 = _cv_conv_fwd(pre, cv_dw_f, cv_dwb_f, cv_lng_f, cv_lnb_f, seq)
            y = _mm_plain("cv_out", y2, sq("cv_w_out"), NN, bias=cv_bout_f)
            sv.update(pre=pre, y1=y1, y2=y2)
        x1, h2 = _lnres_fwd(xin, y, mods[l][2], ln1_g[l:l + 1], ln1_b[l:l + 1], alpha, seq, nxt=(mods[l][4], mods[l][3]))
        if l == 0:
            (hg, hu, act), arrived = _ffn_in(h2, gathered["ffn_w_in", l], 0, beside=("gather", [shard[n] for n in with_ffn_0]))
            gathered.update(zip(with_ffn_0, arrived))
            fox_full = _full_cols(gathered["fox_w_in"], fox_w_in.shape[2])
            fox_qkv_w, fox_f_wt = fox_full[:, :3 * d], jnp.transpose(fox_full[:, 3 * d:])
        else:
            hg, hu, act = _ffn_in(h2, gathered["ffn_w_in", l], 0)
        y2f = _ffn_out(act, w_ffn_out_rows(l), 0)
        sv.update(y=y, x1=x1, h2=h2, hg=hg, hu=hu, act=act, y2f=y2f)
        if l + 1 < nl:
            xin, h = _lnres_fwd(x1, y2f, mods[l][5], ln2_g[l:l + 1], ln2_b[l:l + 1], alpha, seq, nxt=(mods[l + 1][1], mods[l + 1][0]))
        else:
            dx, sq_err = _lnres_fwd(x1, y2f, mods[l][5], ln2_g[l:l + 1], ln2_b[l:l + 1], alpha, seq, tgt=tgt)
        saved.append(sv)

    loss = lax.psum(0.5 * jnp.sum(sq_err) / d, ("x", "y", "c"))

    small = {}
    bigg = {}
    recv = {}
    dmod_parts = [dict() for _ in range(nl)]
    pending = None
    d_ln = dict(ln1_g=[None] * nl, ln1_b=[None] * nl, ln2_g=[None] * nl, ln2_b=[None] * nl)
    beside_sb = [("ffn_w_in", 3), ("ffn_w_out", 3), "cv_w_in", "cv_w_out", ("ffn_w_in", 2), ("ffn_w_out", 2)]
    beside_fox = ["sb_w_in", "sb_w_out", ("ffn_w_in", 1), ("ffn_w_out", 1)]
    beside_dact_0 = ["fox_w_in", "fox_w_out"]
    beside_dh_0 = [("ffn_w_in", 0)]
    beside_gm = [("ffn_w_out", 0), "gm_w_out"]
    beside_gm_in = ["gm_w_in"]
    for l in reversed(range(nl)):
        sv = saved[l]
        kind = l % 4
        if pending is None:
            dxr, dy2, dlg, dlb, _, dgate2 = _lnres_bwd(dx, sv["x1"], sv["y2f"], mods[l][5], ln2_g[l:l + 1], alpha, seq)
        else:
            dxr, dy2, dlg, dlb, _, dgate2, dsc1, dsh1 = _lnres_bwd(pending[1], sv["x1"], sv["y2f"], mods[l][5], ln2_g[l:l + 1], alpha, seq,
                                                                   through=(pending[0], pending[2], ln2_b[l:l + 1]))
            dmod_parts[l + 1].update(sc1=dsc1, sh1=dsh1)
        d_ln["ln2_g"][l], d_ln["ln2_b"][l] = dlg, dlb
        if l == 0:
            (dg_, du_), landed = _ffn_dact(dy2, w_ffn_out_rows(l), sv["hg"], sv["hu"], 0, beside=("scatter", [bigg[n] for n in beside_dact_0]))
            recv.update(zip(beside_dact_0, landed))
        else:
            dg_, du_ = _ffn_dact(dy2, w_ffn_out_rows(l), sv["hg"], sv["hu"], 0)
        dwi, dwo = _ffn_bwd_weights(sv["h2"], dy2, sv["act"], dg_, du_)
        bigg["ffn_w_in", l] = dwi[None]
        bigg["ffn_w_out", l] = dwo.reshape(1, N_DEV, n_ffn // 2, d)
        if l == 0:
            dh2, landed = _ffn_dh(dg_, du_, gathered["ffn_w_in", l], 0, beside=("scatter", [bigg[n] for n in beside_dh_0]))
            recv.update(zip(beside_dh_0, landed))
        else:
            dh2 = _ffn_dh(dg_, du_, gathered["ffn_w_in", l], 0)
        dxr, dy, dlg, dlb, dysum, dgate1, dsc2, dsh2 = _lnres_bwd(dxr, sv["x"], sv["y"], mods[l][2], ln1_g[l:l + 1], alpha, seq,
                                                                  through=(dh2, mods[l][4], ln1_b[l:l + 1]))
        d_ln["ln1_g"][l], d_ln["ln1_b"][l] = dlg, dlb
        hh = sv["h"]
        if kind == 0:
            dyv = _mm_plain("gm_out_bwd", dy, sq("gm_w_out"), NT)
            bigg["gm_w_out"] = _mm_plain("gm_out_dw", sv["yv"], dy, TN, GRAD_WIRE).reshape(1, N_DEV, d // N_DEV, d)
            (dpre, dws, dbst, dlng, dlnb, dbin), landed = _gm_spatial_bwd(sv["pre"], dyv, gm_ln_g, gm_ln_b, gm_w_s[0], jnp.transpose(gm_b_s[0]), seq,
                                                                          beside=("scatter", [bigg[n] for n in beside_gm]))
            recv.update(zip(beside_gm, landed))
            small.update(gm_w_s=dws[None], gm_b_s=jnp.transpose(dbst)[None], gm_ln_g=dlng, gm_ln_b=dlnb, gm_b_in=dbin)
            bigg["gm_w_in"] = _grad_cols("gm_in_dw", hh, dpre, gm_w_in.shape[2])
            dh, landed = _back_cols("gm_in_bwd", dpre, gathered["gm_w_in"], gm_w_in.shape[2], beside=("scatter", [bigg[n] for n in beside_gm_in]))
            recv.update(zip(beside_gm_in, landed))
        elif kind == 1:
            do = _mm_plain("fox_out_bwd", dy, sq("fox_w_out"), NT)
            bigg["fox_w_out"] = _mm_plain("fox_out_dw", sv["o"], dy, TN, GRAD_WIRE).reshape(1, N_DEV, d // N_DEV, d)
            dqkv, dfr, dfq, *landed = _fox_bwd(sv["qkv"], sv["frow"], sv["o"], do, sv["lse"], nb, seq,
                                               scatter=[bigg[n] for n in beside_fox])
            recv.update(zip(beside_fox, landed))
            dft, dbf = _fox_gate_bwd(sv["ft"], sv["b_f"], dfr.reshape(n_heads, t),
                                     jnp.transpose(dfq, (0, 2, 1)).reshape(n_heads, t), seq)
            small["fox_b_f"] = jnp.transpose(dbf)
            dw_qkv = _qkv_dw("fox_qkv_dw", hh, dqkv)
            tk = min(TMM, t)
            dw_ft = _mm("fox_gate_dw", dft, hh, _sds((n_heads, d), F32), (1, t // tk),
                        pl.BlockSpec((n_heads, tk), lambda j, r: (0, r)), pl.BlockSpec((tk, d), lambda j, r: (r, 0)),
                        pl.BlockSpec((n_heads, d), lambda j, r: (0, 0)), NN)
            bigg["fox_w_in"] = _col_blocks(jnp.concatenate([dw_qkv, jnp.transpose(dw_ft).astype(GRAD_WIRE)], axis=1), fox_w_in.shape[2])
            dh_a = _qkv_dh("fox_qkv_bwd", dqkv, fox_qkv_w)
            tm = min(TMM, t)
            dh = _mm("fox_gate_bwd_h", dft, fox_f_wt, _sds((t, d), F32), (t // tm, 1),
                     pl.BlockSpec((n_heads, tm), lambda i, r: (0, i)), pl.BlockSpec((n_heads, d), lambda i, r: (0, 0)),
                     pl.BlockSpec((tm, d), lambda i, r: (i, 0)), TN, (dh_a,), (pl.BlockSpec((tm, d), lambda i, r: (i, 0)),), _add)
        elif kind == 2:
            do = _mm_plain("sb_out_bwd", dy, sq("sb_w_out"), NT)
            bigg["sb_w_out"] = _mm_plain("sb_out_dw", sv["o"], dy, TN, GRAD_WIRE).reshape(1, N_DEV, d // N_DEV, d)
            dqkv, *landed = _sb_bwd(sv["qkv"], do, sv["ltot"], nb, seq, scatter=[bigg[n] for n in beside_sb])
            recv.update(zip(beside_sb, landed))
            bigg["sb_w_in"] = _col_blocks(_qkv_dw("sb_qkv_dw", hh, dqkv), sb_w_in.shape[2])
            dh = _qkv_dh("sb_qkv_bwd", dqkv, sb_qkv_w)
        else:
            dy2c = _mm_plain("cv_out_bwd", dy, sq("cv_w_out"), NT)
            bigg["cv_w_out"] = _mm_plain("cv_out_dw", sv["y2"], dy, TN, GRAD_WIRE).reshape(1, N_DEV, d // N_DEV, d)
            dy1, dlng, dlnb, ddwb = _cv_norm_bwd(sv["y1"], dy2c, cv_lng_f, cv_lnb_f)
            dpre, ddw, dbin = _cv_conv_bwd(sv["pre"], dy1, cv_dw_f, seq)
            small.update(cv_b_out=dysum, cv_ln_g=dlng, cv_ln_b=dlnb, cv_dw_b=ddwb, cv_dw=ddw[:CONV_WIDTH], cv_b_in=dbin)
            bigg["cv_w_in"] = _grad_cols("cv_in_dw", hh, dpre, cv_w_in.shape[2])
            dh = _back_cols("cv_in_bwd", dpre, gathered["cv_w_in"], cv_w_in.shape[2])
        pending = (dh, dxr, mods[l][1])
        dmod_parts[l].update(g1=dgate1, sh2=dsh2, sc2=dsc2, g2=dgate2)
    dx, dsc1, dsh1 = _modulate_bwd(pending[0], pending[1], saved[0]["x"], pending[2], seq)
    dmod_parts[0].update(sc1=dsc1, sh1=dsh1)
    dmods = [jnp.concatenate([p["sh1"], p["sc1"], p["g1"], p["sh2"], p["sc2"], p["g2"]], axis=2)[:, 0, :] for p in dmod_parts]
    grad_x = dx.reshape(nb, seq, d)
    for n in d_ln:
        small[n] = jnp.concatenate(d_ln[n], axis=0)

    dmod_rows = jnp.stack(dmods).reshape(nl * nb, 6 * d)
    dmod_g = _exchange_small(_pad_rows(dmod_rows, 8 * ((nl * nb + 7) // 8)), "gather_dmod", False)[:, :nl * nb]
    dmod_all = jnp.transpose(dmod_g.reshape(N_DEV, nl, nb, 6 * d), (1, 0, 2, 3)).reshape(nl, N_DEV * nb, 6 * d)
    dmod_loc = lax.dynamic_slice_in_dim(dmod_all, me * n_mod, n_mod, axis=2)
    g_mod_w, g_mod_b = _mod_bwd(c_all, dmod_loc, dmod_all)
    grads = dict(mod_w=g_mod_w, mod_b=g_mod_b[:, 0, :])

    rep = ["ln1_g", "ln1_b", "ln2_g", "ln2_b", "gm_b_in", "gm_ln_g", "gm_ln_b", "gm_w_s", "gm_b_s", "fox_b_f"]
    cvs = ["cv_b_in", "cv_dw", "cv_dw_b", "cv_ln_g", "cv_ln_b", "cv_b_out"]

    def rows_of(a):
        flat = a.reshape(-1)
        pad = (-flat.shape[0]) % d
        return jnp.pad(flat, (0, pad)).reshape(-1, d)

    pack_rows = [rows_of(small[n]) for n in rep + cvs]
    counts = [r.shape[0] for r in pack_rows]
    total = sum(counts)
    pack = _pad_rows(jnp.concatenate(pack_rows, axis=0), 8 * ((total + 7) // 8))
    summed = _exchange_small(pack, "allreduce_small", True)
    offs = [sum(counts[:i]) for i in range(len(counts))]
    rep_rows = sum(counts[:len(rep)])
    for n, o_, cnt in zip(rep + cvs, offs, counts):
        full = summed[o_:o_ + cnt].reshape(-1)
        if n in rep:
            grads[n] = full[:weights[n].size].reshape(weights[n].shape)
        else:
            wshape = weights[n].shape
            cols = wshape[-1]
            full = full[:math.prod(wshape[:-1]) * cols * N_DEV].reshape(wshape[:-1] + (cols * N_DEV,))
            grads[n] = lax.dynamic_slice_in_dim(full, me * cols, cols, axis=full.ndim - 1)

    outs = {}

    def view2(a):
        return a.reshape(-1, a.shape[-1])

    for n in big:
        w2 = view2(weights[n])
        landed = [recv[n, l] for l in range(nl)] if n.startswith("ffn") else [recv[n]]
        parts = [r.reshape(N_DEV, w2.shape[0] // len(landed), w2.shape[1]) for r in landed]
        res = _adamw("adamw_" + n, w2, view2(mom1[n]), view2(mom2[n]), parts=parts)
        outs[n] = [r.reshape(weights[n].shape) for r in res]
    res = _adamw("adamw_mod_w", view2(mod_w), view2(m_mod_w), view2(v_mod_w), g=view2(grads["mod_w"]))
    outs["mod_w"] = [r.reshape(mod_w.shape) for r in res]
    rp = lambda src: _pad_rows(jnp.concatenate([rows_of(src[n]) for n in rep], axis=0), 8 * ((rep_rows + 7) // 8))
    res = _adamw("adamw_replicated", rp(weights), rp(mom1), rp(mom2), g=rp(grads))
    for n, o_, cnt in zip(rep, offs, counts):
        outs[n] = [r[o_:o_ + cnt].reshape(-1)[:weights[n].size].reshape(weights[n].shape) for r in res]
    cv_cols = weights["cv_b_out"].shape[-1]
    cp = lambda src: jnp.concatenate([src[n].reshape(-1, cv_cols) for n in cvs], axis=0)
    cv_cnt = [weights[n].size // cv_cols for n in cvs]
    cv_tot = sum(cv_cnt)
    cpp = lambda src: _pad_rows(cp(src), 8 * ((cv_tot + 7) // 8))
    res = _adamw("adamw_cv_small", cpp(weights), cpp(mom1), cpp(mom2), g=cpp(grads))
    o_ = 0
    for n, cnt in zip(cvs, cv_cnt):
        outs[n] = [r[o_:o_ + cnt].reshape(weights[n].shape) for r in res]
        o_ += cnt
    res = _adamw("adamw_mod_b", mod_b, m_mod_b, v_mod_b, g=grads["mod_b"])
    outs["mod_b"] = list(res)

    return (loss, grad_x, *[outs[n][0] for n in names], *[outs[n][1] for n in names],
            *[outs[n][2] for n in names], *[outs[n][3] for n in names])
```

```python
import functools
import math

import jax
import jax.numpy as jnp
from jax import lax
from jax.experimental import pallas as pl
from jax.experimental.pallas import tpu as pltpu

F32 = jnp.float32
BF16 = jnp.bfloat16
MESH = pl.DeviceIdType.MESH

N_DEV = 8
HEAD_DIM = 64
LANES = 128
SUBLANES = 8
GM_CHUNK = 128
GM_GROUPS = 8
CONV_WIDTH = 31
CONV_HALO = 32
CONV_ROWS = 32
LN_EPS = 1e-5
NEG_INF = -1e30
SB_DEAD = -100.0
GRAD_WIRE = jnp.bfloat16
FFN_KEEP = jnp.bfloat16

ADAM_LR = 0.001
ADAM_B1 = 0.9
ADAM_B2 = 0.999
ADAM_EPS = 1e-08
ADAM_WD = 0.01
ADAM_STEP = 10

TM = 512
TMM = 1024
TQ = 256
FOX_FWD_UNROLL = 8
FOX_BWD_UNROLL = 4
SB_BWD_UNROLL = 2
TT = 512
VMEM_BIG = 56 * 1024 * 1024

NN = (((1,), (0,)), ((), ()))
NT = (((1,), (1,)), ((), ()))
TN = (((0,), (0,)), ((), ()))


def _call(body, name, out_shape, grid=None, in_specs=None, out_specs=None, scratch=(), vmem=None, aliases=None):
    params = {}
    if grid is not None:
        params["dimension_semantics"] = ("arbitrary",) * len(grid)
    if vmem is not None:
        params["vmem_limit_bytes"] = vmem
    kw = {}
    if grid is not None:
        kw["grid"] = grid
    if in_specs is not None:
        kw["in_specs"] = in_specs
    if out_specs is not None:
        kw["out_specs"] = out_specs
    if aliases is not None:
        kw["input_output_aliases"] = aliases
    return pl.pallas_call(body, name=name, out_shape=out_shape, scratch_shapes=list(scratch),
                          compiler_params=pltpu.CompilerParams(**params), **kw)


def _sds(shape, dtype):
    return jax.ShapeDtypeStruct(tuple(shape), dtype)


def _dot(a, b, dims=NN):
    return lax.dot_general(a.astype(BF16), b.astype(BF16), dims, preferred_element_type=F32)


def _split3(x):
    h1 = x.astype(BF16)
    r1 = x - h1.astype(F32)
    h2 = r1.astype(BF16)
    h3 = (r1 - h2.astype(F32)).astype(BF16)
    return h1, h2, h3


def _dot_exact(x, m, dims=NN):
    h1, h2, h3 = _split3(x)
    d = lambda h: lax.dot_general(h, m, dims, preferred_element_type=F32)
    return (d(h1) + d(h2)) + d(h3)


def _dot_exact_rhs(m, x, dims=NN):
    h1, h2, h3 = _split3(x)
    d = lambda h: lax.dot_general(m, h, dims, preferred_element_type=F32)
    return (d(h1) + d(h2)) + d(h3)


def _silu(x):
    return x * jax.nn.sigmoid(x)


def _gelu(x):
    return 0.5 * x * (1.0 + lax.erf(x * (2.0 ** -0.5)))


def _gelu_grad(x):
    return 0.5 * (1.0 + lax.erf(x * (2.0 ** -0.5))) + x * jnp.exp(-0.5 * x * x) * ((2.0 * math.pi) ** -0.5)


def _log_sigmoid(z):
    return jnp.minimum(z, 0.0) - jnp.log(1.0 + jnp.exp(-jnp.abs(z)))


def _ln_stats(r):
    mu = jnp.mean(r, axis=-1, keepdims=True)
    rc = r - mu
    var = jnp.mean(rc * rc, axis=-1, keepdims=True)
    return rc, lax.rsqrt(var + LN_EPS)


def _colsum(x):
    return jnp.sum(x, axis=0, keepdims=True)


def _peers():
    mx, my, mc = lax.axis_index("x"), lax.axis_index("y"), lax.axis_index("c")
    me = 4 * mx + 2 * my + mc
    out = []
    for k in range(1, N_DEV):
        px = 1 - mx if (k >> 2) & 1 else mx
        py = 1 - my if (k >> 1) & 1 else my
        pc = 1 - mc if k & 1 else mc
        out.append(((px, py, pc), 4 * px + 2 * py + pc))
    return me, out


def _exchange_small(x, name, reduce):
    rows, cols = x.shape

    def body(x_ref, o_ref, *rest):
        if reduce:
            land, send_sems, recv_sems, local_sem = rest
        else:
            land = o_ref
            send_sems, recv_sems, local_sem = rest
        me, peers = _peers()
        mine = pltpu.make_async_copy(x_ref, land.at[me], local_sem)
        mine.start()
        sends = []
        for k, (peer, _) in enumerate(peers):
            cp = pltpu.make_async_remote_copy(src_ref=x_ref, dst_ref=land.at[me], send_sem=send_sems.at[k],
                                              recv_sem=recv_sems.at[k], device_id=peer, device_id_type=MESH)
            cp.start()
            sends.append(cp)
        for k, (peer, blk) in enumerate(peers):
            pltpu.make_async_remote_copy(src_ref=x_ref, dst_ref=land.at[blk], send_sem=send_sems.at[k],
                                         recv_sem=recv_sems.at[k], device_id=peer, device_id_type=MESH).wait_recv()
        for cp in sends:
            cp.wait_send()
        mine.wait()
        if reduce:
            acc = land[0]
            for s in range(1, N_DEV):
                acc = acc + land[s]
            o_ref[...] = acc

    vm = pl.BlockSpec(memory_space=pltpu.VMEM)
    scratch = [pltpu.SemaphoreType.DMA((N_DEV - 1,)), pltpu.SemaphoreType.DMA((N_DEV - 1,)), pltpu.SemaphoreType.DMA]
    if reduce:
        scratch = [pltpu.VMEM((N_DEV, rows, cols), F32)] + scratch
        out = _sds((rows, cols), F32)
    else:
        out = _sds((N_DEV, rows, cols), F32)
    return _call(body, name, out, in_specs=[vm], out_specs=vm, scratch=scratch, vmem=VMEM_BIG)(x)


def _comm_scratch(n):
    return [pltpu.SemaphoreType.DMA((n, N_DEV - 1)), pltpu.SemaphoreType.DMA((n, N_DEV - 1)), pltpu.SemaphoreType.DMA((n,))]


def _gather_stage(stage, ins, outs, send_sems, recv_sems, local_sems):
    n = len(ins)
    mx, my, mc = lax.axis_index("x"), lax.axis_index("y"), lax.axis_index("c")
    here, sibling = (mx, my, mc), (mx, my, 1 - mc)
    chips = [(1 - mx, my), (mx, 1 - my), (1 - mx, 1 - my)]

    def block(px, py, pc):
        return 4 * px + 2 * py + pc

    def copy(a, k, blk, to, src=None):
        dst = outs[a].at[:, blk]
        return pltpu.make_async_remote_copy(src_ref=dst if src is None else src, dst_ref=dst, send_sem=send_sems.at[a, k],
                                            recv_sem=recv_sems.at[a, k], device_id=to, device_id_type=MESH)

    me = block(*here)
    for a in range(n):
        local = pltpu.make_async_copy(ins[a], outs[a].at[:, me], local_sems.at[a])
        first = [copy(a, 0, me, sibling, src=ins[a])] + [copy(a, 1 + j, me, (*chip, mc), src=ins[a]) for j, chip in enumerate(chips)]
        if stage == 0:
            local.start()
            for cp in first:
                cp.start()
        if stage == 1:
            for j, chip in enumerate(chips):
                copy(a, 1 + j, block(*chip, mc), here).wait_recv()
                copy(a, 4 + j, block(*chip, mc), sibling).start()
        if stage == 2:
            copy(a, 0, block(mx, my, 1 - mc), here).wait_recv()
            for j, chip in enumerate(chips):
                copy(a, 4 + j, block(*chip, 1 - mc), here).wait_recv()
            for cp in first:
                cp.wait_send()
            for j, chip in enumerate(chips):
                copy(a, 4 + j, block(*chip, mc), sibling).wait_send()
            local.wait()


def _gathered_shapes(shards):
    return [_sds((s.shape[0], N_DEV) + s.shape[1:], s.dtype) for s in shards]


def _scattered_shapes(grads):
    return [_sds((N_DEV, g.shape[0]) + g.shape[2:], g.dtype) for g in grads]


def _run(beside, body, name, out_shape, args, grid, in_specs, out_specs, scratch=(), vmem=None):
    if beside is None:
        return _call(body, name, out_shape, grid=grid, in_specs=in_specs, out_specs=out_specs, scratch=scratch, vmem=vmem)(*args)
    kind, arrays = beside
    n = len(arrays)
    outs = tuple(out_shape) if isinstance(out_shape, (tuple, list)) else (out_shape,)
    ospecs = tuple(out_specs) if isinstance(out_specs, (tuple, list)) else (out_specs,)
    total = math.prod(grid)
    gather = kind == "gather"
    stage_fn = _gather_stage if gather else _scatter_stage
    early = {0: 0, 1: (3 * total) // 4} if gather else {0: 0}

    def carrier(*refs):
        own_in, send, own_out, land, sems, own_scratch = _carried(n, len(in_specs), len(outs), refs)
        at = 0
        for ax, size in enumerate(grid):
            at = at * size + pl.program_id(ax)
        for stage, when in early.items():
            @pl.when(at == when)
            def _(stage=stage):
                stage_fn(stage, send, land, *sems)
        body(*own_in, *own_out, *own_scratch)

        @pl.when(at == total - 1)
        def _():
            stage_fn(2 if gather else 1, send, land, *sems)

    hbm = pl.BlockSpec(memory_space=pl.ANY)
    shapes = _gathered_shapes(arrays) if gather else _scattered_shapes(arrays)
    res = _call(carrier, name, (*outs, *shapes), grid=grid, in_specs=[*in_specs, *[hbm] * n], out_specs=(*ospecs, *[hbm] * n),
                scratch=[*_comm_scratch(n), *scratch], vmem=vmem)(*args, *arrays)
    own = res[:len(outs)]
    return (own if len(own) > 1 else own[0]), list(res[len(outs):])


def _scatter_stage(stage, ins, outs, send_sems, recv_sems, local_sems):
    me, peers = _peers()
    for a in range(len(ins)):
        local = pltpu.make_async_copy(ins[a].at[:, me], outs[a].at[me], local_sems.at[a])
        sends = [pltpu.make_async_remote_copy(src_ref=ins[a].at[:, blk], dst_ref=outs[a].at[me], send_sem=send_sems.at[a, k],
                                              recv_sem=recv_sems.at[a, k], device_id=peer, device_id_type=MESH)
                 for k, (peer, blk) in enumerate(peers)]
        if stage == 0:
            local.start()
            for cp in sends:
                cp.start()
        if stage == 1:
            for k, (peer, blk) in enumerate(peers):
                pltpu.make_async_remote_copy(src_ref=ins[a].at[:, me], dst_ref=outs[a].at[blk], send_sem=send_sems.at[a, k],
                                             recv_sem=recv_sems.at[a, k], device_id=peer, device_id_type=MESH).wait_recv()
            for cp in sends:
                cp.wait_send()
            local.wait()


def _mm(name, a, b, out, grid, a_spec, b_spec, o_spec, dims, extra=(), extra_specs=(), epilogue=None, vmem=VMEM_BIG, beside=None):
    nred = grid[-1]
    red_axis = len(grid) - 1
    acc_shape = tuple(d for d in o_spec.block_shape if d is not None)
    n_extra = len(extra)

    def body(a_ref, b_ref, *rest):
        ex = rest[:n_extra]
        o_ref = rest[n_extra]

        def finish(acc):
            if epilogue is not None:
                acc = epilogue(acc, *[e[...] for e in ex])
            o_ref[...] = acc.astype(o_ref.dtype)

        prod = _dot(a_ref[...], b_ref[...], dims)
        if nred == 1:
            finish(prod)
        else:
            acc_ref = rest[n_extra + 1]
            r = pl.program_id(red_axis)

            @pl.when(r == 0)
            def _():
                acc_ref[...] = prod

            @pl.when(r > 0)
            def _():
                acc_ref[...] += prod

            @pl.when(r == nred - 1)
            def _():
                finish(acc_ref[...])

    scratch = [pltpu.VMEM(acc_shape, F32)] if nred > 1 else []
    return _run(beside, body, name, out, (a, b, *extra), grid, [a_spec, b_spec, *extra_specs], o_spec, scratch, vmem)


def _add(acc, x):
    return acc + x


def _proj_cols(name, h, w, n_slot, bias=None, out_dtype=F32, beside=None):
    t, k = h.shape
    s = w.shape[1]
    tm = min(TM, t)

    def body(h_ref, w_ref, *rest):
        o_ref = rest[-1]
        hv = h_ref[...]
        for j in range(s):
            cols = slice(j * n_slot, (j + 1) * n_slot)
            acc = _dot(hv, w_ref[j])
            if bias is not None:
                acc = acc + rest[0][:, cols]
            o_ref[:, cols] = acc.astype(o_ref.dtype)

    ins, in_specs = [h, w], [pl.BlockSpec((tm, k), lambda i: (i, 0)), pl.BlockSpec((None, s, k, n_slot), lambda i: (0, 0, 0, 0))]
    if bias is not None:
        ins.append(bias)
        in_specs.append(pl.BlockSpec((1, s * n_slot), lambda i: (0, 0)))
    return _run(beside, body, name, _sds((t, s * n_slot), out_dtype), tuple(ins), (t // tm,), in_specs,
                pl.BlockSpec((tm, s * n_slot), lambda i: (i, 0)), vmem=VMEM_BIG)


def _accumulate_over_tokens(name, ins, in_specs, out, o_spec, acc_shape, n_steps, terms, store=None, fill=None):
    n_in = len(ins) + (fill is not None)

    def body(*refs):
        o_ref, acc_ref = refs[n_in], refs[n_in + 1]
        r = pl.program_id(0)

        @pl.when(r == 0)
        def _():
            acc_ref[...] = jnp.zeros_like(acc_ref)

        for s, prod in enumerate(terms(*refs[:len(ins)])):
            acc_ref[s] += prod

        @pl.when(r == n_steps - 1)
        def _():
            if store is None:
                o_ref[...] = acc_ref[...].reshape(o_ref.shape).astype(o_ref.dtype)
            else:
                store(o_ref, acc_ref)

    if fill is None:
        return _call(body, name, out, grid=(n_steps,), in_specs=in_specs, out_specs=o_spec,
                     scratch=[pltpu.VMEM(acc_shape, F32)], vmem=VMEM_BIG)(*ins)
    return _call(body, name, out, grid=(n_steps,), in_specs=[*in_specs, pl.BlockSpec(memory_space=pl.ANY)], out_specs=o_spec,
                 scratch=[pltpu.VMEM(acc_shape, F32)], vmem=VMEM_BIG, aliases={len(ins): 0})(*ins, fill)


def _grad_cols(name, h, g, n_slot):
    t, k = h.shape
    s = g.shape[1] // n_slot
    tk = min(TMM, t)

    def terms(h_ref, g_ref):
        hv = h_ref[...]
        return [_dot(hv, g_ref[:, j * n_slot:(j + 1) * n_slot], TN) for j in range(s)]

    return _accumulate_over_tokens(name, (h, g), [pl.BlockSpec((tk, k), lambda r: (r, 0)), pl.BlockSpec((tk, s * n_slot), lambda r: (r, 0))],
                                   _sds((1, s, k, n_slot), GRAD_WIRE), pl.BlockSpec((1, s, k, n_slot), lambda r: (0, 0, 0, 0)),
                                   (s, k, n_slot), t // tk, terms)


def _back_cols(name, g, w, n_slot, beside=None):
    t = g.shape[0]
    s, k = w.shape[1], w.shape[2]
    tm = min(TM, t)

    def body(g_ref, w_ref, o_ref):
        acc = None
        for j in range(s):
            term = _dot(g_ref[:, j * n_slot:(j + 1) * n_slot], w_ref[j], NT)
            acc = term if acc is None else acc + term
        o_ref[...] = acc

    return _run(beside, body, name, _sds((t, k), F32), (g, w), (t // tm,),
                [pl.BlockSpec((tm, s * n_slot), lambda i: (i, 0)), pl.BlockSpec((None, s, k, n_slot), lambda i: (0, 0, 0, 0))],
                pl.BlockSpec((tm, k), lambda i: (i, 0)), vmem=VMEM_BIG)


def _mm_plain(name, a, b, dims, out_dtype=F32, bias=None):
    if dims == TN:
        t, k = a.shape
        n = b.shape[1]
        tk = min(TMM, t)
        return _mm(name, a, b, _sds((k, n), out_dtype), (1, t // tk),
                   pl.BlockSpec((tk, k), lambda j, r: (r, 0)), pl.BlockSpec((tk, n), lambda j, r: (r, 0)),
                   pl.BlockSpec((k, n), lambda j, r: (0, 0)), TN)
    t = a.shape[0]
    tm = min(TMM, t)
    n = b.shape[1] if dims == NN else b.shape[0]
    extra, especs, epi = (), (), None
    if bias is not None:
        extra, especs, epi = (bias,), (pl.BlockSpec((1, n), lambda i, r: (0, 0)),), _add
    return _mm(name, a, b, _sds((t, n), out_dtype), (t // tm, 1),
               pl.BlockSpec((tm, a.shape[1]), lambda i, r: (i, 0)), pl.BlockSpec(b.shape, lambda i, r: (0, 0)),
               pl.BlockSpec((tm, n), lambda i, r: (i, 0)), dims, extra, especs, epi)


def _mod_fwd(c_all, mod_w, mod_b_loc):
    nl, d, n = mod_w.shape
    nb = c_all.shape[0]

    def body(c_ref, w_ref, b_ref, o_ref):
        o_ref[...] = _dot(_silu(c_ref[...]), w_ref[...]) + b_ref[...]

    return _call(body, "mod_fwd", _sds((nl, nb, n), F32), grid=(nl,),
                 in_specs=[pl.BlockSpec((nb, d), lambda l: (0, 0)), pl.BlockSpec((None, d, n), lambda l: (l, 0, 0)),
                           pl.BlockSpec((None, 1, n), lambda l: (l, 0, 0))],
                 out_specs=pl.BlockSpec((None, nb, n), lambda l: (l, 0, 0)))(c_all, mod_w, mod_b_loc)


def _mod_bwd(c_all, dmod_loc, dmod_all):
    nl, nb, n = dmod_loc.shape
    d = c_all.shape[1]
    n_all = dmod_all.shape[2]

    def body(c_ref, dl_ref, da_ref, gw_ref, gb_ref):
        gw_ref[...] = _dot(_silu(c_ref[...]), dl_ref[...], TN)
        gb_ref[...] = _colsum(da_ref[...])

    return _call(body, "mod_bwd", (_sds((nl, d, n), F32), _sds((nl, 1, n_all), F32)), grid=(nl,),
                 in_specs=[pl.BlockSpec((nb, d), lambda l: (0, 0)), pl.BlockSpec((None, nb, n), lambda l: (l, 0, 0)),
                           pl.BlockSpec((None, nb, n_all), lambda l: (l, 0, 0))],
                 out_specs=(pl.BlockSpec((None, d, n), lambda l: (l, 0, 0)), pl.BlockSpec((None, 1, n_all), lambda l: (l, 0, 0))),
                 )(c_all, dmod_loc, dmod_all)


def _row_spec(d, tpb):
    return pl.BlockSpec((None, 1, d), lambda i: (i // tpb, 0, 0))


def _tile_spec(tm, d):
    return pl.BlockSpec((tm, d), lambda i: (i, 0))


def _vec_spec(d):
    return pl.BlockSpec((1, d), lambda i: (0, 0))


def _modulate(x, sc, sh, seq, beside=None):
    t, d = x.shape
    tm = min(TM, seq)
    tpb = seq // tm

    def body(x_ref, sc_ref, sh_ref, h_ref):
        h_ref[...] = (x_ref[...] * (1.0 + sc_ref[...]) + sh_ref[...]).astype(BF16)

    return _run(beside, body, "modulate", _sds((t, d), BF16), (x, sc, sh), (t // tm,),
                [_tile_spec(tm, d), _row_spec(d, tpb), _row_spec(d, tpb)], _tile_spec(tm, d))


def _lnres_fwd(x, y, gate, lg, lb, alpha, seq, nxt=None, tgt=None):
    t, d = x.shape
    tm = min(TM, seq)
    tpb = seq // tm

    def body(x_ref, y_ref, g_ref, lg_ref, lb_ref, *rest):
        r = alpha * x_ref[...] + (1.0 + g_ref[...]) * y_ref[...]
        rc, rstd = _ln_stats(r)
        xn = rc * rstd * lg_ref[...] + lb_ref[...]
        if nxt is None:
            t_ref, dx_ref, sq_ref = rest
            e = xn - t_ref[...]
            dx_ref[...] = e * (1.0 / d)

            @pl.when(pl.program_id(0) == 0)
            def _():
                sq_ref[...] = jnp.zeros_like(sq_ref)

            sq_ref[...] += _colsum(e * e)
        else:
            sc_ref, sh_ref, xo_ref, h_ref = rest
            xo_ref[...] = xn
            h_ref[...] = (xn * (1.0 + sc_ref[...]) + sh_ref[...]).astype(BF16)

    ins = [_tile_spec(tm, d), _tile_spec(tm, d), _row_spec(d, tpb), _vec_spec(d), _vec_spec(d)]
    if nxt is None:
        return _call(body, "lnres_fwd_loss", (_sds((t, d), F32), _sds((1, d), F32)), grid=(t // tm,), in_specs=ins + [_tile_spec(tm, d)],
                     out_specs=(_tile_spec(tm, d), _vec_spec(d)))(x, y, gate, lg, lb, tgt)
    return _call(body, "lnres_fwd", (_sds((t, d), F32), _sds((t, d), BF16)), grid=(t // tm,),
                 in_specs=ins + [_row_spec(d, tpb), _row_spec(d, tpb)],
                 out_specs=(_tile_spec(tm, d), _tile_spec(tm, d)))(x, y, gate, lg, lb, *nxt)


def _lnres_bwd(dxo, x, y, gate, lg, alpha, seq, through=None):
    t, d = x.shape
    tm = min(TM, seq)
    tpb = seq // tm
    nb = t // seq

    def body(dxo_ref, x_ref, y_ref, g_ref, lg_ref, *rest):
        if through is None:
            dxr_ref, dy_ref, dlg_ref, dlb_ref, dys_ref, dg_ref = rest
        else:
            dh_ref, sc_ref, lb_ref, dxr_ref, dy_ref, dlg_ref, dlb_ref, dys_ref, dg_ref, dsc_ref, dsh_ref = rest
        i = pl.program_id(0)
        yv = y_ref[...]
        r = alpha * x_ref[...] + (1.0 + g_ref[...]) * yv
        rc, rstd = _ln_stats(r)
        xhat = rc * rstd
        dxo_v = dxo_ref[...]
        if through is not None:
            dhv = dh_ref[...]
            dxo_v = dxo_v + dhv * (1.0 + sc_ref[...])

            @pl.when(i % tpb == 0)
            def _():
                dsc_ref[...] = jnp.zeros_like(dsc_ref)
                dsh_ref[...] = jnp.zeros_like(dsh_ref)

            dsc_ref[...] += _colsum(dhv * (xhat * lg_ref[...] + lb_ref[...]))
            dsh_ref[...] += _colsum(dhv)
        dxh = dxo_v * lg_ref[...]
        m1 = jnp.mean(dxh, axis=-1, keepdims=True)
        m2 = jnp.mean(dxh * xhat, axis=-1, keepdims=True)
        dr = rstd * (dxh - m1 - xhat * m2)
        dyv = (1.0 + g_ref[...]) * dr
        dxr_ref[...] = alpha * dr
        dy_ref[...] = dyv.astype(BF16)

        @pl.when(i == 0)
        def _():
            dlg_ref[...] = jnp.zeros_like(dlg_ref)
            dlb_ref[...] = jnp.zeros_like(dlb_ref)
            dys_ref[...] = jnp.zeros_like(dys_ref)

        @pl.when(i % tpb == 0)
        def _():
            dg_ref[...] = jnp.zeros_like(dg_ref)

        dlg_ref[...] += _colsum(dxo_v * xhat)
        dlb_ref[...] += _colsum(dxo_v)
        dys_ref[...] += _colsum(dyv)
        dg_ref[...] += _colsum(dr * yv)

    out = [_sds((t, d), F32), _sds((t, d), BF16), _sds((1, d), F32), _sds((1, d), F32), _sds((1, d), F32), _sds((nb, 1, d), F32)]
    out_specs = [_tile_spec(tm, d), _tile_spec(tm, d), _vec_spec(d), _vec_spec(d), _vec_spec(d), _row_spec(d, tpb)]
    ins = [dxo, x, y, gate, lg]
    in_specs = [_tile_spec(tm, d), _tile_spec(tm, d), _tile_spec(tm, d), _row_spec(d, tpb), _vec_spec(d)]
    if through is not None:
        ins += list(through)
        in_specs += [_tile_spec(tm, d), _row_spec(d, tpb), _vec_spec(d)]
        out += [_sds((nb, 1, d), F32), _sds((nb, 1, d), F32)]
        out_specs += [_row_spec(d, tpb), _row_spec(d, tpb)]
    return _call(body, "lnres_bwd" if through is None else "lnres_mod_bwd", tuple(out), grid=(t // tm,),
                 in_specs=in_specs, out_specs=tuple(out_specs))(*ins)


def _modulate_bwd(dh, dxr, x, sc, seq):
    t, d = x.shape
    tm = min(TM, seq)
    tpb = seq // tm
    nb = t // seq

    def body(dh_ref, dxr_ref, x_ref, sc_ref, dx_ref, dsc_ref, dsh_ref):
        dhv = dh_ref[...]
        dx_ref[...] = dxr_ref[...] + dhv * (1.0 + sc_ref[...])

        @pl.when(pl.program_id(0) % tpb == 0)
        def _():
            dsc_ref[...] = jnp.zeros_like(dsc_ref)
            dsh_ref[...] = jnp.zeros_like(dsh_ref)

        dsc_ref[...] += _colsum(dhv * x_ref[...])
        dsh_ref[...] += _colsum(dhv)

    return _call(body, "modulate_bwd", (_sds((t, d), F32), _sds((nb, 1, d), F32), _sds((nb, 1, d), F32)), grid=(t // tm,),
                 in_specs=[_tile_spec(tm, d), _tile_spec(tm, d), _tile_spec(tm, d), _row_spec(d, tpb)],
                 out_specs=(_tile_spec(tm, d), _row_spec(d, tpb), _row_spec(d, tpb)))(dh, dxr, x, sc)


def _ffn_in(h, w_in, layer, beside=None):
    t, d = h.shape
    n = w_in.shape[3]
    half = N_DEV // 2
    tm = min(TMM, t)

    def body(h_ref, wg_ref, wu_ref, g_ref, u_ref, a_ref):
        hv = h_ref[...]
        g = _dot(hv, wg_ref[...])
        u = _dot(hv, wu_ref[...])
        g_ref[...] = g.astype(FFN_KEEP)
        u_ref[...] = u.astype(FFN_KEEP)
        a_ref[...] = (_silu(g) * u).astype(BF16)

    blk = pl.BlockSpec((None, tm, n), lambda p, i: (p, i, 0))
    return _run(beside, body, "ffn_in", (_sds((half, t, n), FFN_KEEP), _sds((half, t, n), FFN_KEEP), _sds((half, t, n), BF16)),
                (h, w_in, w_in), (half, t // tm),
                [pl.BlockSpec((tm, d), lambda p, i: (i, 0)),
                 pl.BlockSpec((None, None, d, n), lambda p, i: (layer, p, 0, 0)),
                 pl.BlockSpec((None, None, d, n), lambda p, i: (layer, p + half, 0, 0))],
                (blk, blk, blk), vmem=VMEM_BIG)


def _ffn_out(act, w_out, layer):
    half, t, n = act.shape
    d = w_out.shape[2]
    tm = min(TM, t)

    def body(a_ref, w_ref, o_ref):
        acc = _dot(a_ref[0], w_ref[0])
        for p in range(1, half):
            acc = acc + _dot(a_ref[p], w_ref[p])
        o_ref[...] = acc

    return _call(body, "ffn_out", _sds((t, d), F32), grid=(t // tm,),
                 in_specs=[pl.BlockSpec((half, tm, n), lambda i: (0, i, 0)),
                           pl.BlockSpec((half, n, d), lambda i: (layer // half, 0, 0))],
                 out_specs=pl.BlockSpec((tm, d), lambda i: (i, 0)), vmem=VMEM_BIG)(act, w_out)


def _ffn_dact(dy, w_out, hg, hu, layer, beside=None):
    half, t, n = hg.shape
    d = dy.shape[1]
    tm = min(TMM, t)

    def body(dy_ref, w_ref, g_ref, u_ref, dg_ref, du_ref):
        da = _dot(dy_ref[...], w_ref[...], NT)
        g = g_ref[...].astype(F32)
        sg = jax.nn.sigmoid(g)
        dg_ref[...] = (da * u_ref[...].astype(F32) * (sg * (1.0 + g * (1.0 - sg)))).astype(BF16)
        du_ref[...] = (da * (g * sg)).astype(BF16)

    blk = pl.BlockSpec((None, tm, n), lambda p, i: (p, i, 0))
    return _run(beside, body, "ffn_dact", (_sds((half, t, n), BF16), _sds((half, t, n), BF16)), (dy, w_out, hg, hu), (half, t // tm),
                [pl.BlockSpec((tm, d), lambda p, i: (i, 0)), pl.BlockSpec((None, n, d), lambda p, i: (layer + p, 0, 0)), blk, blk],
                (blk, blk), vmem=VMEM_BIG)


def _ffn_bwd_weights(h, dy, act, dg, du):
    half, t, n = act.shape
    d = h.shape[1]
    tk = min(TMM, t)
    h_spec = pl.BlockSpec((tk, d), lambda r: (r, 0))
    g_spec = pl.BlockSpec((half, tk, n), lambda r: (0, r, 0))

    def in_terms(h_ref, g_ref):
        hv = h_ref[...]
        return [_dot(hv, g_ref[p], TN) for p in range(half)]

    def out_terms(a_ref, dy_ref):
        dyv = dy_ref[...]
        return [_dot(a_ref[p], dyv, TN) for p in range(half)]

    both = _sds((2 * half, d, n), GRAD_WIRE)
    dwi = _accumulate_over_tokens("ffn_dw_gate", (h, dg), [h_spec, g_spec], both, pl.BlockSpec((half, d, n), lambda r: (0, 0, 0)),
                                  (half, d, n), t // tk, in_terms)
    dwi = _accumulate_over_tokens("ffn_dw_up", (h, du), [h_spec, g_spec], both, pl.BlockSpec((half, d, n), lambda r: (1, 0, 0)),
                                  (half, d, n), t // tk, in_terms, fill=dwi)
    dwo = _accumulate_over_tokens("ffn_dw_out", (act, dy), [g_spec, h_spec], _sds((half, n, d), GRAD_WIRE),
                                  pl.BlockSpec((half, n, d), lambda r: (0, 0, 0)), (half, n, d), t // tk, out_terms)
    return dwi, dwo


def _ffn_dh(dg, du, w_in, layer, beside=None):
    half, t, n = dg.shape
    d = w_in.shape[2]
    tm = min(TM, t)

    def body(dg_ref, du_ref, w_ref, o_ref):
        acc = None
        for p in range(half):
            for ref, q in ((dg_ref, p), (du_ref, p + half)):
                term = _dot(ref[p], w_ref[q], NT)
                acc = term if acc is None else acc + term
        o_ref[...] = acc

    g_spec = pl.BlockSpec((half, tm, n), lambda i: (0, i, 0))
    return _run(beside, body, "ffn_dh", _sds((t, d), F32), (dg, du, w_in), (t // tm,),
                [g_spec, g_spec, pl.BlockSpec((None, 2 * half, d, n), lambda i: (layer, 0, 0, 0))],
                pl.BlockSpec((tm, d), lambda i: (i, 0)), vmem=VMEM_BIG)


def _tril(n, strict=False):
    r = lax.broadcasted_iota(jnp.int32, (n, n), 0)
    c = lax.broadcasted_iota(jnp.int32, (n, n), 1)
    return c < r if strict else c <= r


def _gm_spatial_fwd(pre, lng, lnb, w_s, b_st, seq, beside=None):
    t, w2 = pre.shape
    w = w2 // 2
    gd = w // GM_GROUPS
    tm = min(TM, seq)
    nch = tm // GM_CHUNK

    def body(pre_ref, lng_ref, lnb_ref, ws_ref, bs_ref, y_ref):
        v = _gelu(pre_ref[:, w:])
        vc, rstd = _ln_stats(v)
        vn = (vc * rstd * lng_ref[...] + lnb_ref[...]).astype(BF16)
        keep = _tril(GM_CHUNK)
        for g in range(GM_GROUPS):
            wm = jnp.where(keep, ws_ref[g], 0.0).astype(BF16)
            for ci in range(nch):
                rows = slice(ci * GM_CHUNK, (ci + 1) * GM_CHUNK)
                cols = slice(g * gd, (g + 1) * gd)
                sv = _dot(wm, vn[rows, cols]) + bs_ref[:, g:g + 1]
                u = _gelu(pre_ref[rows, cols])
                y_ref[rows, cols] = (u * sv).astype(BF16)

    return _run(beside, body, "gm_spatial_fwd", _sds((t, w), BF16), (pre, lng, lnb, w_s, b_st), (t // tm,),
                [_tile_spec(tm, w2), _vec_spec(w), _vec_spec(w),
                 pl.BlockSpec((GM_GROUPS, GM_CHUNK, GM_CHUNK), lambda i: (0, 0, 0)),
                 pl.BlockSpec((GM_CHUNK, GM_GROUPS), lambda i: (0, 0))],
                _tile_spec(tm, w), vmem=VMEM_BIG)


def _gm_spatial_bwd(pre, dyv, lng, lnb, w_s, b_st, seq, beside=None):
    t, w2 = pre.shape
    w = w2 // 2
    gd = w // GM_GROUPS
    tm = min(TM, seq)
    nch = tm // GM_CHUNK

    def body(pre_ref, dyv_ref, lng_ref, lnb_ref, ws_ref, bs_ref, dpre_ref, dws_ref, dbs_ref, dlg_ref, dlb_ref, dbin_ref, dvn_ref):
        @pl.when(pl.program_id(0) == 0)
        def _():
            dws_ref[...] = jnp.zeros_like(dws_ref)
            dbs_ref[...] = jnp.zeros_like(dbs_ref)
            dlg_ref[...] = jnp.zeros_like(dlg_ref)
            dlb_ref[...] = jnp.zeros_like(dlb_ref)
            dbin_ref[...] = jnp.zeros_like(dbin_ref)

        pv = pre_ref[:, w:]
        v = _gelu(pv)
        vc, rstd = _ln_stats(v)
        vhat = vc * rstd
        vn = (vhat * lng_ref[...] + lnb_ref[...]).astype(BF16)
        keep = _tril(GM_CHUNK)
        dbs_cols = []
        for g in range(GM_GROUPS):
            wm = jnp.where(keep, ws_ref[g], 0.0).astype(BF16)
            dwm = jnp.zeros((GM_CHUNK, GM_CHUNK), F32)
            dbs = jnp.zeros((GM_CHUNK, 1), F32)
            for ci in range(nch):
                rows = slice(ci * GM_CHUNK, (ci + 1) * GM_CHUNK)
                cols = slice(g * gd, (g + 1) * gd)
                vn_b = vn[rows, cols]
                sv = _dot(wm, vn_b) + bs_ref[:, g:g + 1]
                pu = pre_ref[rows, cols]
                dy = dyv_ref[rows, cols]
                du = dy * sv
                dsv = dy * _gelu(pu)
                dpu = du * _gelu_grad(pu)
                dpre_ref[rows, cols] = dpu.astype(BF16)
                dbin_ref[:, cols] += _colsum(dpu)
                dsv_b = dsv.astype(BF16)
                dwm = dwm + _dot(dsv_b, vn_b, NT)
                dbs = dbs + jnp.sum(dsv, axis=-1, keepdims=True)
                dvn_ref[rows, cols] = _dot(wm, dsv_b, TN)
            dws_ref[g] += jnp.where(keep, dwm, 0.0)
            dbs_cols.append(dbs)
        dbs_ref[...] += jnp.concatenate(dbs_cols, axis=1)
        dvn = dvn_ref[...]
        dlg_ref[...] += _colsum(dvn * vhat)
        dlb_ref[...] += _colsum(dvn)
        dvh = dvn * lng_ref[...]
        m1 = jnp.mean(dvh, axis=-1, keepdims=True)
        m2 = jnp.mean(dvh * vhat, axis=-1, keepdims=True)
        dv = rstd * (dvh - m1 - vhat * m2)
        dpv = dv * _gelu_grad(pv)
        dpre_ref[:, w:] = dpv.astype(BF16)
        dbin_ref[:, w:] += _colsum(dpv)

    full3 = pl.BlockSpec((GM_GROUPS, GM_CHUNK, GM_CHUNK), lambda i: (0, 0, 0))
    bst = pl.BlockSpec((GM_CHUNK, GM_GROUPS), lambda i: (0, 0))
    return _run(beside, body, "gm_spatial_bwd",
                (_sds((t, w2), BF16), _sds((GM_GROUPS, GM_CHUNK, GM_CHUNK), F32), _sds((GM_CHUNK, GM_GROUPS), F32),
                 _sds((1, w), F32), _sds((1, w), F32), _sds((1, w2), F32)),
                (pre, dyv, lng, lnb, w_s, b_st), (t // tm,),
                [_tile_spec(tm, w2), _tile_spec(tm, w), _vec_spec(w), _vec_spec(w), full3, bst],
                (_tile_spec(tm, w2), full3, bst, _vec_spec(w), _vec_spec(w), _vec_spec(w2)),
                scratch=[pltpu.VMEM((tm, w), F32)], vmem=VMEM_BIG)


def _head_masks():
    lane = lax.broadcasted_iota(jnp.int32, (1, LANES), 1)
    return lane < HEAD_DIM


def _two_heads(x, m0):
    z = jnp.zeros_like(x)
    return jnp.where(m0, x, z), jnp.where(m0, z, x)


def _transposed(pair):
    return tuple(x.astype(F32).T.astype(BF16) for x in pair)


def _store_transposed(dqkv_ref, dk_acc, dv_acc, nq, tq):
    for c in range(nq):
        cols = slice(c * tq, (c + 1) * tq)
        dqkv_ref[1, cols, :] = dk_acc[:, cols].T.astype(BF16)
        dqkv_ref[2, cols, :] = dv_acc[:, cols].T.astype(BF16)


def _qkv_specs(seq, nq, blocked_q):
    if blocked_q:
        q = pl.BlockSpec((None, TQ_(seq), LANES), lambda b, p, i: (0, b * nq + i, p))
        k = pl.BlockSpec((None, seq, LANES), lambda b, p, i: (1, b, p))
        v = pl.BlockSpec((None, seq, LANES), lambda b, p, i: (2, b, p))
    else:
        q = pl.BlockSpec((None, seq, LANES), lambda b, p: (0, b, p))
        k = pl.BlockSpec((None, seq, LANES), lambda b, p: (1, b, p))
        v = pl.BlockSpec((None, seq, LANES), lambda b, p: (2, b, p))
    return q, k, v


def TQ_(seq):
    return min(TQ, seq)


def _fox_gate_fwd(ft, b_f, seq):
    nh, t = ft.shape
    nch = seq // LANES

    def body(ft_ref, bf_ref, fr_ref):
        r = lax.broadcasted_iota(jnp.int32, (LANES, LANES), 0)
        c = lax.broadcasted_iota(jnp.int32, (LANES, LANES), 1)
        upper = jnp.where(r <= c, 1.0, 0.0).astype(BF16)
        carry = jnp.zeros((nh, 1), F32)
        for ci in range(nch):
            cols = slice(ci * LANES, (ci + 1) * LANES)
            lf = _log_sigmoid(ft_ref[:, cols] + bf_ref[...])
            cs = _dot_exact(lf, upper) + carry
            fr_ref[:, cols] = cs
            carry = cs[:, LANES - 1:LANES]

    return _call(body, "fox_gate_fwd", _sds((nh, t), F32), grid=(t // seq,),
                 in_specs=[pl.BlockSpec((nh, seq), lambda b: (0, b)), pl.BlockSpec((nh, 1), lambda b: (0, 0))],
                 out_specs=pl.BlockSpec((nh, seq), lambda b: (0, b)))(ft, b_f)


def _fox_gate_bwd(ft, b_f, dfk, dfq, seq):
    nh, t = ft.shape
    nch = seq // LANES

    def body(ft_ref, bf_ref, dfk_ref, dfq_ref, dl_ref, db_ref):
        @pl.when(pl.program_id(0) == 0)
        def _():
            db_ref[...] = jnp.zeros_like(db_ref)

        r = lax.broadcasted_iota(jnp.int32, (LANES, LANES), 0)
        c = lax.broadcasted_iota(jnp.int32, (LANES, LANES), 1)
        lower = jnp.where(r >= c, 1.0, 0.0).astype(BF16)
        carry = jnp.zeros((nh, 1), F32)
        tot = jnp.zeros((nh, 1), F32)
        for ci in reversed(range(nch)):
            cols = slice(ci * LANES, (ci + 1) * LANES)
            rc = _dot_exact(dfk_ref[:, cols] + dfq_ref[:, cols], lower) + carry
            carry = rc[:, 0:1]
            dl = rc * jax.nn.sigmoid(-(ft_ref[:, cols] + bf_ref[...]))
            dl_ref[:, cols] = dl
            tot = tot + jnp.sum(dl, axis=-1, keepdims=True)
        db_ref[...] += tot

    blk = pl.BlockSpec((nh, seq), lambda b: (0, b))
    one = pl.BlockSpec((nh, 1), lambda b: (0, 0))
    return _call(body, "fox_gate_bwd", (_sds((nh, t), F32), _sds((nh, 1), F32)), grid=(t // seq,),
                 in_specs=[blk, one, blk, blk], out_specs=(blk, one))(ft, b_f, dfk, dfq)


def _sweep(step, n_off, unroll, init, start=0):
    def group(_, st):
        base, carry = st[0], st[1:]
        for u in range(unroll):
            carry = step(base + u, carry, False)
        return (base + unroll, *carry)

    def tail(r):
        def run(st):
            base, carry = st[0], st[1:]
            for u in range(r):
                carry = step(base + u, carry, False)
            return step(base + r, carry, True)
        return run

    def pick(idx, fns, st):
        if len(fns) == 1:
            return fns[0](st)
        half = len(fns) // 2
        return lax.cond(idx < half, lambda s: pick(idx, fns[:half], s), lambda s: pick(idx - half, fns[half:], s), st)

    st = lax.fori_loop(0, n_off // unroll, group, (jnp.int32(0) + start, *init))
    return pick(n_off % unroll, [tail(r) for r in range(unroll)], st)


def _carried(n_comm, n_in, n_out, refs):
    own_in, send = refs[:n_in], refs[n_in:n_in + n_comm]
    rest = refs[n_in + n_comm:]
    n_sem = 3 if n_comm else 0
    return own_in, send, rest[:n_out], rest[n_out:n_out + n_comm], rest[n_out + n_comm:n_out + n_comm + n_sem], rest[n_out + n_comm + n_sem:]


def _fox_fwd(qkv, frow, nb, seq, gather=()):
    _, t, d = qkv.shape
    npair = d // LANES
    tq = TQ_(seq)
    nq = seq // tq
    scale = HEAD_DIM ** -0.5
    n_comm = len(gather)
    total = nb * npair * nq

    def body(*refs):
        (q_ref, k_ref, v_ref, fr_ref), send, (o_ref, lse_ref), land, sems, _ = _carried(n_comm, 4, 2, refs)
        at = (pl.program_id(0) * npair + pl.program_id(1)) * nq + pl.program_id(2)
        if n_comm:
            for stage, when in ((0, 0), (1, (3 * total) // 4)):
                @pl.when(at == when)
                def _(stage=stage):
                    _gather_stage(stage, send, land, *sems)

        m0 = _head_masks()
        qm = _two_heads(q_ref[...] * scale, m0)
        row = lax.broadcasted_iota(jnp.int32, (tq, tq), 0)
        col = lax.broadcasted_iota(jnp.int32, (tq, tq), 1)
        one = jnp.ones((tq, LANES), BF16)

        def step(j, carry, diag):
            off = pl.multiple_of(j * tq, tq)
            kb = k_ref[pl.ds(off, tq), :]
            vb = v_ref[pl.ds(off, tq), :]
            vv = (jnp.where(m0, vb, one), jnp.where(m0, one, vb))
            out = []
            for hh in range(2):
                m, acc = carry[2 * hh], carry[2 * hh + 1]
                s = lax.dot_general(qm[hh], kb, NT, preferred_element_type=F32) - fr_ref[hh:hh + 1, pl.ds(off, tq)]
                if diag:
                    s = jnp.where(col <= row, s, NEG_INF)
                mn = jnp.maximum(m, jnp.max(s, axis=-1, keepdims=True))
                p = jnp.exp(s - mn)
                out += [mn, jnp.exp(m - mn) * acc + jnp.dot(p.astype(BF16), vv[hh], preferred_element_type=F32)]
            return tuple(out)

        neg = jnp.full((tq, 1), NEG_INF, F32)
        zacc = jnp.zeros((tq, LANES), F32)
        m_a, acc_a, m_b, acc_b = _sweep(step, pl.program_id(2), FOX_FWD_UNROLL, (neg, zacc, neg, zacc))
        l_a = pltpu.roll(acc_a, HEAD_DIM, 1)
        l_b = pltpu.roll(acc_b, HEAD_DIM, 1)
        o_ref[...] = jnp.where(m0, acc_a / l_a, acc_b / l_b)
        lse_ref[:, 0:1] = m_a + jnp.log(l_a[:, 0:1])
        lse_ref[:, 1:2] = m_b + jnp.log(l_b[:, HEAD_DIM:HEAD_DIM + 1])
        if n_comm:
            @pl.when(at == total - 1)
            def _():
                _gather_stage(2, send, land, *sems)

    q_spec, k_spec, v_spec = _qkv_specs(seq, nq, True)
    col_spec = pl.BlockSpec((None, tq, 2), lambda b, p, i: (p, b * nq + i, 0))
    hbm = pl.BlockSpec(memory_space=pl.ANY)
    return _call(body, "fox_fwd", (_sds((t, d), F32), _sds((npair, t, 2), F32), *_gathered_shapes(gather)), grid=(nb, npair, nq),
                 in_specs=[q_spec, k_spec, v_spec, pl.BlockSpec((None, 2, seq), lambda b, p, i: (p, 0, b))] + [hbm] * n_comm,
                 out_specs=(pl.BlockSpec((tq, LANES), lambda b, p, i: (b * nq + i, p)), col_spec, *[hbm] * n_comm),
                 scratch=_comm_scratch(n_comm) if n_comm else (), vmem=VMEM_BIG)(qkv, qkv, qkv, frow, *gather)


def _scatter_beside(stage, n_comm, nb, npair, send, land, sems):
    if n_comm:
        at = pl.program_id(0) * npair + pl.program_id(1)

        @pl.when(at == (0 if stage == 0 else nb * npair - 1))
        def _():
            _scatter_stage(stage, send, land, *sems)


def _fox_bwd(qkv, frow, o, do, lse, nb, seq, scatter=()):
    _, t, d = qkv.shape
    npair = d // LANES
    tq = TQ_(seq)
    nq = seq // tq
    scale = HEAD_DIM ** -0.5
    n_comm = len(scatter)

    def body(*refs):
        own_in, send, (dqkv_ref, df_ref, dfq_ref), land, sems, (dk_acc, dv_acc) = _carried(n_comm, 7, 3, refs)
        q_ref, k_ref, v_ref, fr_ref, o_ref, do_ref, lse_ref = own_in
        _scatter_beside(0, n_comm, nb, npair, send, land, sems)
        m0 = _head_masks()
        row = lax.broadcasted_iota(jnp.int32, (tq, tq), 0)
        col = lax.broadcasted_iota(jnp.int32, (tq, tq), 1)
        dk_acc[...] = jnp.zeros_like(dk_acc)
        dv_acc[...] = jnp.zeros_like(dv_acc)
        df_ref[...] = jnp.zeros_like(df_ref)

        def q_block(i, _):
            qoff = pl.multiple_of(i * tq, tq)
            qrows = pl.ds(qoff, tq)
            qm = _two_heads(q_ref[qrows, :] * scale, m0)
            dov = do_ref[qrows, :]
            dd = dov * o_ref[qrows, :]
            dm = _two_heads(dov.astype(BF16), m0)
            qmt = _transposed(qm)
            dmt = _transposed(dm)
            delta = (jnp.sum(jnp.where(m0, dd, 0.0), axis=-1, keepdims=True),
                     jnp.sum(jnp.where(m0, 0.0, dd), axis=-1, keepdims=True))
            ls = (lse_ref[qrows, 0:1], lse_ref[qrows, 1:2])

            def step(j, carry, diag):
                off = pl.multiple_of(j * tq, tq)
                krows = pl.ds(off, tq)
                kb = k_ref[krows, :]
                vb = v_ref[krows, :]
                dqs, rowsums = [], []
                dk = jnp.zeros((LANES, tq), F32)
                dv = jnp.zeros((LANES, tq), F32)
                for hh in range(2):
                    s = lax.dot_general(qm[hh], kb, NT, preferred_element_type=F32) - fr_ref[hh:hh + 1, krows]
                    if diag:
                        s = jnp.where(col <= row, s, NEG_INF)
                    p = jnp.exp(s - ls[hh])
                    dp = lax.dot_general(dm[hh], vb, NT, preferred_element_type=F32)
                    ds = p * (dp - delta[hh])
                    df_ref[hh:hh + 1, krows] -= _colsum(ds)
                    rowsums.append(carry[1 + hh] + jnp.sum(ds, axis=-1, keepdims=True))
                    ds_b = ds.astype(BF16)
                    dqs.append(jnp.dot(ds_b, kb, preferred_element_type=F32))
                    dk = dk + jnp.dot(qmt[hh], ds_b, preferred_element_type=F32)
                    dv = dv + jnp.dot(dmt[hh], p.astype(BF16), preferred_element_type=F32)
                dk_acc[:, krows] += dk
                dv_acc[:, krows] += dv
                return (carry[0] + jnp.where(m0, dqs[0], dqs[1]), *rowsums)

            zero = jnp.zeros((tq, 1), F32)
            dq, rs_a, rs_b = _sweep(step, i, FOX_BWD_UNROLL, (jnp.zeros((tq, LANES), F32), zero, zero))
            dqkv_ref[0, qrows, :] = (dq * scale).astype(BF16)
            dfq_ref[qrows, 0:1] = rs_a
            dfq_ref[qrows, 1:2] = rs_b
            return 0

        lax.fori_loop(0, nq, q_block, 0)
        _store_transposed(dqkv_ref, dk_acc, dv_acc, nq, tq)
        _scatter_beside(1, n_comm, nb, npair, send, land, sems)

    q_spec, k_spec, v_spec = _qkv_specs(seq, nq, False)
    col_spec = pl.BlockSpec((None, seq, 2), lambda b, p: (p, b, 0))
    row_spec = pl.BlockSpec((None, 2, seq), lambda b, p: (p, 0, b))
    tile = pl.BlockSpec((seq, LANES), lambda b, p: (b, p))
    hbm = pl.BlockSpec(memory_space=pl.ANY)
    return _call(body, "fox_bwd", (_sds((3, t, d), BF16), _sds((npair, 2, t), F32), _sds((npair, t, 2), F32), *_scattered_shapes(scatter)),
                 grid=(nb, npair),
                 in_specs=[q_spec, k_spec, v_spec, row_spec, tile, tile, col_spec] + [hbm] * n_comm,
                 out_specs=(pl.BlockSpec((3, seq, LANES), lambda b, p: (0, b, p)), row_spec, col_spec, *[hbm] * n_comm),
                 scratch=(_comm_scratch(n_comm) if n_comm else []) + [pltpu.VMEM((LANES, seq), F32), pltpu.VMEM((LANES, seq), F32)],
                 vmem=VMEM_BIG)(qkv, qkv, qkv, frow, o, do, lse, *scatter)


def _split2(x):
    hi = x.astype(BF16)
    return hi, (x - hi.astype(F32)).astype(BF16)


def _sum_right(x, tri):
    hi, lo = _split2(x)
    return jnp.dot(hi, tri, preferred_element_type=F32) + jnp.dot(lo, tri, preferred_element_type=F32)


def _sb_scores(qm_h, kb, mask):
    z = lax.dot_general(qm_h, kb, NT, preferred_element_type=F32)
    lb = _log_sigmoid(z)
    l1m = lb - z
    if mask is not None:
        l1m = jnp.where(mask, l1m, 0.0)
    return lb, l1m


def _sb_fwd(qkv, nb, seq):
    _, t, d = qkv.shape
    npair = d // LANES
    tq = TQ_(seq)
    nq = seq // tq
    scale = HEAD_DIM ** -0.5

    def body(q_ref, k_ref, v_ref, o_ref, lt_ref):
        i = pl.program_id(2)
        m0 = _head_masks()
        qm = _two_heads(q_ref[...] * scale, m0)
        row = lax.broadcasted_iota(jnp.int32, (tq, tq), 0)
        col = lax.broadcasted_iota(jnp.int32, (tq, tq), 1)
        after = jnp.where(row > col, 1.0, 0.0).astype(BF16)

        def step(j, carry, diag):
            off = pl.multiple_of(j * tq, tq)
            kb = k_ref[pl.ds(off, tq), :]
            vb = v_ref[pl.ds(off, tq), :]
            mask = (col < row) if diag else None
            nxt, parts = [], []
            for hh in range(2):
                lb, l1m = _sb_scores(qm[hh], kb, mask)
                rest = _sum_right(l1m, after) + carry[hh]
                a = jnp.exp(lb + rest)
                if diag:
                    a = jnp.where(mask, a, 0.0)
                parts.append(jnp.dot(a.astype(BF16), vb, preferred_element_type=F32))
                nxt.append(carry[hh] + jnp.sum(l1m, axis=-1, keepdims=True))
            return (*nxt, carry[2] + jnp.where(m0, parts[0], parts[1]))

        zero = jnp.zeros((tq, 1), F32)
        init = (zero, zero, jnp.zeros((tq, LANES), F32))
        carry = lax.cond(i > 0, lambda c: step(i - 1, step(i, c, True), False), lambda c: step(i, c, True), init)

        def alive(st):
            return (st[0] < i) & (jnp.max(jnp.maximum(st[1], st[2])) > SB_DEAD)

        def more(st):
            return (st[0] + 1, *step(i - 1 - st[0], st[1:], False))

        done, lt_a, lt_b, acc = lax.while_loop(alive, more, (jnp.minimum(i, 1), *carry))
        o_ref[...] = acc
        lt_ref[:, 0:1] = lt_a
        lt_ref[:, 1:2] = lt_b
        lt_ref[:, 2:3] = jnp.zeros((tq, 1), F32) + done.astype(F32)
        lt_ref[:, 3:4] = zero

    q_spec, k_spec, v_spec = _qkv_specs(seq, nq, True)
    return _call(body, "sb_fwd", (_sds((t, d), F32), _sds((npair, t, 4), F32)), grid=(nb, npair, nq),
                 in_specs=[q_spec, k_spec, v_spec],
                 out_specs=(pl.BlockSpec((tq, LANES), lambda b, p, i: (b * nq + i, p)),
                            pl.BlockSpec((None, tq, 4), lambda b, p, i: (p, b * nq + i, 0))), vmem=VMEM_BIG)(qkv, qkv, qkv)


def _sb_bwd(qkv, do, ltot, nb, seq, scatter=()):
    _, t, d = qkv.shape
    npair = d // LANES
    tq = TQ_(seq)
    nq = seq // tq
    scale = HEAD_DIM ** -0.5
    n_comm = len(scatter)

    def body(*refs):
        (q_ref, k_ref, v_ref, do_ref, lt_ref), send, (dqkv_ref,), land, sems, (dk_acc, dv_acc) = _carried(n_comm, 5, 1, refs)
        _scatter_beside(0, n_comm, nb, npair, send, land, sems)
        m0 = _head_masks()
        row = lax.broadcasted_iota(jnp.int32, (tq, tq), 0)
        col = lax.broadcasted_iota(jnp.int32, (tq, tq), 1)
        upto = jnp.where(row <= col, 1.0, 0.0).astype(BF16)
        left_of = jnp.where(row < col, 1.0, 0.0).astype(BF16)
        dk_acc[...] = jnp.zeros_like(dk_acc)
        dv_acc[...] = jnp.zeros_like(dv_acc)

        def q_block(i, _):
            qoff = pl.multiple_of(i * tq, tq)
            qrows = pl.ds(qoff, tq)
            qm = _two_heads(q_ref[qrows, :] * scale, m0)
            dm = _two_heads(do_ref[qrows, :].astype(BF16), m0)
            qmt = _transposed(qm)
            dmt = _transposed(dm)
            ltot = (lt_ref[qrows, 0:1], lt_ref[qrows, 1:2])

            def step(j, carry, diag):
                off = pl.multiple_of(j * tq, tq)
                krows = pl.ds(off, tq)
                kb = k_ref[krows, :]
                vb = v_ref[krows, :]
                mask = (col < row) if diag else None
                nxt, dqs = [], []
                dk = jnp.zeros((LANES, tq), F32)
                dv = jnp.zeros((LANES, tq), F32)
                for hh in range(2):
                    cl, ce = carry[2 * hh], carry[2 * hh + 1]
                    lb, l1m = _sb_scores(qm[hh], kb, mask)
                    a = jnp.exp(lb + (ltot[hh] - (_sum_right(l1m, upto) + cl)))
                    if diag:
                        a = jnp.where(mask, a, 0.0)
                    e = lax.dot_general(dm[hh], vb, NT, preferred_element_type=F32) * a
                    before = _sum_right(e, left_of) + ce
                    beta = jnp.exp(lb)
                    dz = e * (1.0 - beta) - before * beta
                    if diag:
                        dz = jnp.where(mask, dz, 0.0)
                    dz_b = dz.astype(BF16)
                    dqs.append(jnp.dot(dz_b, kb, preferred_element_type=F32))
                    dk = dk + jnp.dot(qmt[hh], dz_b, preferred_element_type=F32)
                    dv = dv + jnp.dot(dmt[hh], a.astype(BF16), preferred_element_type=F32)
                    nxt += [cl + jnp.sum(l1m, axis=-1, keepdims=True), ce + jnp.sum(e, axis=-1, keepdims=True)]
                dk_acc[:, krows] += dk
                dv_acc[:, krows] += dv
                return (*nxt, carry[4] + jnp.where(m0, dqs[0], dqs[1]))

            zero = jnp.zeros((tq, 1), F32)
            visited = jnp.max(lt_ref[qrows, 2:3]).astype(jnp.int32)
            carry = _sweep(step, visited, SB_BWD_UNROLL, (zero, zero, zero, zero, jnp.zeros((tq, LANES), F32)), start=i - visited)
            dqkv_ref[0, qrows, :] = (carry[4] * scale).astype(BF16)
            return 0

        lax.fori_loop(0, nq, q_block, 0)
        _store_transposed(dqkv_ref, dk_acc, dv_acc, nq, tq)
        _scatter_beside(1, n_comm, nb, npair, send, land, sems)

    q_spec, k_spec, v_spec = _qkv_specs(seq, nq, False)
    tile = pl.BlockSpec((seq, LANES), lambda b, p: (b, p))
    hbm = pl.BlockSpec(memory_space=pl.ANY)
    return _call(body, "sb_bwd", (_sds((3, t, d), BF16), *_scattered_shapes(scatter)), grid=(nb, npair),
                 in_specs=[q_spec, k_spec, v_spec, tile, pl.BlockSpec((None, seq, 4), lambda b, p: (p, b, 0))] + [hbm] * n_comm,
                 out_specs=(pl.BlockSpec((3, seq, LANES), lambda b, p: (0, b, p)), *[hbm] * n_comm),
                 scratch=(_comm_scratch(n_comm) if n_comm else []) + [pltpu.VMEM((LANES, seq), F32), pltpu.VMEM((LANES, seq), F32)],
                 vmem=VMEM_BIG)(qkv, qkv, qkv, do, ltot, *scatter)


def _qkv_proj(name, h, w):
    t, d = h.shape
    tm = min(TMM, t)
    return _mm(name, h, w, _sds((3, t, d), BF16), (3, t // tm, 1),
               pl.BlockSpec((tm, d), lambda s, i, r: (i, 0)), pl.BlockSpec((d, d), lambda s, i, r: (0, s)),
               pl.BlockSpec((None, tm, d), lambda s, i, r: (s, i, 0)), NN)


def _qkv_dw(name, h, dqkv):
    t, d = h.shape
    tk = min(TMM, t)

    def terms(h_ref, g_ref):
        hv = h_ref[...]
        return [_dot(hv, g_ref[s], TN) for s in range(3)]

    def store(o_ref, acc_ref):
        for s in range(3):
            o_ref[:, s * d:(s + 1) * d] = acc_ref[s].astype(o_ref.dtype)

    return _accumulate_over_tokens(name, (h, dqkv), [pl.BlockSpec((tk, d), lambda r: (r, 0)), pl.BlockSpec((3, tk, d), lambda r: (0, r, 0))],
                                   _sds((d, 3 * d), GRAD_WIRE), pl.BlockSpec((d, 3 * d), lambda r: (0, 0)), (3, d, d), t // tk, terms, store)


def _qkv_dh(name, dqkv, w):
    _, t, d = dqkv.shape
    tm = min(TM, t)

    def body(g_ref, w_ref, o_ref):
        acc = None
        for s in range(3):
            term = _dot(g_ref[s], w_ref[:, s * d:(s + 1) * d], NT)
            acc = term if acc is None else acc + term
        o_ref[...] = acc

    return _call(body, name, _sds((t, d), F32), grid=(t // tm,),
                 in_specs=[pl.BlockSpec((3, tm, d), lambda i: (0, i, 0)), pl.BlockSpec((d, 3 * d), lambda i: (0, 0))],
                 out_specs=pl.BlockSpec((tm, d), lambda i: (i, 0)), vmem=VMEM_BIG)(dqkv, w)


def _glu(pre_block, d):
    return pre_block[:, :d] * jax.nn.sigmoid(pre_block[:, d:])


def _shifted_copies(ext_ref, sh_ref, tt):
    for r in range(1, SUBLANES):
        sh_ref[r - 1] = ext_ref[pl.ds(r, tt + CONV_HALO - SUBLANES), :]


def _rows_from(ext_ref, sh_ref, base, offset, n):
    q, r = divmod(offset, SUBLANES)
    if r == 0:
        return ext_ref[pl.ds(pl.multiple_of(base + offset, SUBLANES), n), :]
    return sh_ref[r - 1, pl.ds(pl.multiple_of(base + q * SUBLANES, SUBLANES), n), :]


def _cv_conv_fwd(pre, dw, dwb, lng, lnb, seq):
    t, d2 = pre.shape
    d = d2 // 2
    tt = min(TT, seq)
    nt = seq // tt
    hb = tt // CONV_HALO

    def body(pre_ref, halo_ref, dw_ref, dwb_ref, lng_ref, lnb_ref, y1_ref, y2_ref, ext_ref, sh_ref):
        i = pl.program_id(1)
        ext_ref[0:CONV_HALO, :] = jnp.where(i == 0, 0.0, _glu(halo_ref[...], d))
        ext_ref[CONV_HALO:, :] = _glu(pre_ref[...], d)
        _shifted_copies(ext_ref, sh_ref, tt)

        acc = jnp.zeros((tt, d), F32) + dwb_ref[...]
        for k in range(CONV_WIDTH):
            acc = acc + _rows_from(ext_ref, sh_ref, 0, CONV_HALO - (CONV_WIDTH - 1) + k, tt) * dw_ref[k:k + 1, :]
        y1_ref[...] = acc
        yc, rstd = _ln_stats(acc)
        y2_ref[...] = _silu(yc * rstd * lng_ref[...] + lnb_ref[...]).astype(BF16)

    vec = pl.BlockSpec((1, d), lambda b, i: (0, 0))
    tile = pl.BlockSpec((tt, d), lambda b, i: (b * nt + i, 0))
    return _call(body, "cv_conv_fwd", (_sds((t, d), F32), _sds((t, d), BF16)), grid=(t // seq, nt),
                 in_specs=[pl.BlockSpec((tt, d2), lambda b, i: (b * nt + i, 0)),
                           pl.BlockSpec((CONV_HALO, d2), lambda b, i: (jnp.maximum((b * nt + i) * hb - 1, 0), 0)),
                           pl.BlockSpec((CONV_HALO, d), lambda b, i: (0, 0)), vec, vec, vec],
                 out_specs=(tile, tile),
                 scratch=[pltpu.VMEM((tt + CONV_HALO, d), F32), pltpu.VMEM((SUBLANES - 1, tt + CONV_HALO - SUBLANES, d), F32)],
                 vmem=VMEM_BIG)(pre, pre, dw, dwb, lng, lnb)


def _cv_norm_bwd(y1, dy2, lng, lnb):
    t, d = y1.shape
    tm = min(TM, t)

    def body(y1_ref, dy2_ref, lng_ref, lnb_ref, dy1_ref, dlg_ref, dlb_ref, dsum_ref):
        @pl.when(pl.program_id(0) == 0)
        def _():
            dlg_ref[...] = jnp.zeros_like(dlg_ref)
            dlb_ref[...] = jnp.zeros_like(dlb_ref)
            dsum_ref[...] = jnp.zeros_like(dsum_ref)

        yc, rstd = _ln_stats(y1_ref[...])
        yhat = yc * rstd
        n = yhat * lng_ref[...] + lnb_ref[...]
        sg = jax.nn.sigmoid(n)
        dn = dy2_ref[...] * (sg * (1.0 + n * (1.0 - sg)))
        dlg_ref[...] += _colsum(dn * yhat)
        dlb_ref[...] += _colsum(dn)
        dyh = dn * lng_ref[...]
        m1 = jnp.mean(dyh, axis=-1, keepdims=True)
        m2 = jnp.mean(dyh * yhat, axis=-1, keepdims=True)
        dy1 = rstd * (dyh - m1 - yhat * m2)
        dy1_ref[...] = dy1
        dsum_ref[...] += _colsum(dy1)

    return _call(body, "cv_norm_bwd", (_sds((t, d), F32), _sds((1, d), F32), _sds((1, d), F32), _sds((1, d), F32)),
                 grid=(t // tm,), in_specs=[_tile_spec(tm, d), _tile_spec(tm, d), _vec_spec(d), _vec_spec(d)],
                 out_specs=(_tile_spec(tm, d), _vec_spec(d), _vec_spec(d), _vec_spec(d)))(y1, dy2, lng, lnb)


def _cv_conv_bwd(pre, dy1, dw, seq):
    t, d2 = pre.shape
    d = d2 // 2
    tt = min(TT // 2, seq)
    nt = seq // tt
    hb = tt // CONV_HALO
    last_halo = t // CONV_HALO - 1

    def body(pre_ref, halo_ref, dy_ref, dyn_ref, dw_ref, dpre_ref, ddw_ref, dbin_ref, ext_ref, dext_ref, sh_ref, dsh_ref, taps_ref):
        b, i = pl.program_id(0), pl.program_id(1)

        @pl.when((b == 0) & (i == 0))
        def _():
            ddw_ref[...] = jnp.zeros_like(ddw_ref)
            dbin_ref[...] = jnp.zeros_like(dbin_ref)

        pv = pre_ref[...]
        ext_ref[0:CONV_HALO, :] = jnp.where(i == 0, 0.0, _glu(halo_ref[...], d))
        ext_ref[CONV_HALO:, :] = _glu(pv, d)
        dyv = dy_ref[...]
        dext_ref[0:tt, :] = dyv
        dext_ref[tt:, :] = jnp.where(i == nt - 1, 0.0, dyn_ref[...])
        _shifted_copies(ext_ref, sh_ref, tt)
        _shifted_copies(dext_ref, dsh_ref, tt)
        nrows = tt // CONV_ROWS

        def input_grad(c, _):
            r0 = pl.multiple_of(c * CONV_ROWS, CONV_ROWS)
            dy0 = jnp.zeros((CONV_ROWS, d), F32)
            for k in range(CONV_WIDTH):
                dy0 = dy0 + _rows_from(dext_ref, dsh_ref, r0, CONV_WIDTH - 1 - k, CONV_ROWS) * dw_ref[k:k + 1, :]
            rows = pl.ds(r0, CONV_ROWS)
            a = pre_ref[rows, :d]
            sg = jax.nn.sigmoid(pre_ref[rows, d:])
            da = dy0 * sg
            dg = dy0 * a * sg * (1.0 - sg)
            dpre_ref[rows, :d] = da.astype(BF16)
            dpre_ref[rows, d:] = dg.astype(BF16)
            dbin_ref[:, :d] += _colsum(da)
            dbin_ref[:, d:] += _colsum(dg)
            return 0

        lax.fori_loop(0, nrows, input_grad, 0)

        @pl.when((b == 0) & (i == 0))
        def _():
            taps_ref[...] = jnp.zeros_like(taps_ref)

        def tap_grads(c, _):
            r0 = pl.multiple_of(c * CONV_ROWS, CONV_ROWS)
            dyc = dy_ref[pl.ds(r0, CONV_ROWS), :]
            for k in range(CONV_WIDTH):
                prod = dyc * _rows_from(ext_ref, sh_ref, r0, CONV_HALO - (CONV_WIDTH - 1) + k, CONV_ROWS)
                taps_ref[k] += jnp.sum(prod.reshape(CONV_ROWS // SUBLANES, SUBLANES, d), axis=0)
            return 0

        lax.fori_loop(0, nrows, tap_grads, 0)

        @pl.when((b == t // seq - 1) & (i == nt - 1))
        def _():
            for k in range(CONV_WIDTH):
                ddw_ref[k:k + 1, :] = _colsum(taps_ref[k])

    return _call(body, "cv_conv_bwd", (_sds((t, d2), BF16), _sds((CONV_HALO, d), F32), _sds((1, d2), F32)), grid=(t // seq, nt),
                 in_specs=[pl.BlockSpec((tt, d2), lambda b, i: (b * nt + i, 0)),
                           pl.BlockSpec((CONV_HALO, d2), lambda b, i: (jnp.maximum((b * nt + i) * hb - 1, 0), 0)),
                           pl.BlockSpec((tt, d), lambda b, i: (b * nt + i, 0)),
                           pl.BlockSpec((CONV_HALO, d), lambda b, i: (jnp.minimum((b * nt + i + 1) * hb, last_halo), 0)),
                           pl.BlockSpec((CONV_HALO, d), lambda b, i: (0, 0))],
                 out_specs=(pl.BlockSpec((tt, d2), lambda b, i: (b * nt + i, 0)),
                            pl.BlockSpec((CONV_HALO, d), lambda b, i: (0, 0)), pl.BlockSpec((1, d2), lambda b, i: (0, 0))),
                 scratch=[pltpu.VMEM((tt + CONV_HALO, d), F32), pltpu.VMEM((tt + CONV_HALO, d), F32),
                          pltpu.VMEM((SUBLANES - 1, tt + CONV_HALO - SUBLANES, d), F32),
                          pltpu.VMEM((SUBLANES - 1, tt + CONV_HALO - SUBLANES, d), F32),
                          pltpu.VMEM((CONV_HALO, SUBLANES, d), F32)],
                 vmem=VMEM_BIG)(pre, pre, dy1, dy1, dw)


def _adamw(name, w, m, v, g=None, parts=None):
    rows, cols = w.shape
    nseg = 1 if parts is None else len(parts)
    seg_rows = rows // nseg
    tr = seg_rows
    for cand in ((512,) if parts is None else ()) + (256, 128, 64, 32, 16, 8):
        if seg_rows % cand == 0 and seg_rows > cand:
            tr = cand
            break
    tps = seg_rows // tr
    bc1 = 1.0 - ADAM_B1 ** ADAM_STEP
    bc2 = 1.0 - ADAM_B2 ** ADAM_STEP

    def body(w_ref, m_ref, v_ref, *rest):
        g_refs, (go_ref, d_ref, mo_ref, vo_ref) = rest[:nseg], rest[nseg:]

        def update(gv):
            mn = ADAM_B1 * m_ref[...] + (1.0 - ADAM_B1) * gv
            vn = ADAM_B2 * v_ref[...] + (1.0 - ADAM_B2) * (gv * gv)
            m_hat = mn / bc1
            v_hat = vn / bc2
            go_ref[...] = gv
            d_ref[...] = -ADAM_LR * (m_hat / (jnp.sqrt(v_hat) + ADAM_EPS) + ADAM_WD * w_ref[...])
            mo_ref[...] = mn
            vo_ref[...] = vn

        if parts is None:
            update(g_refs[0][...])
        else:
            for k, g_ref in enumerate(g_refs):
                @pl.when(pl.program_id(0) == k)
                def _(g_ref=g_ref):
                    gv = g_ref[0].astype(F32)
                    for s in range(1, N_DEV):
                        gv = gv + g_ref[s].astype(F32)
                    update(gv)

    blk = pl.BlockSpec((tr, cols), lambda l, i: (l * tps + i, 0))
    if parts is None:
        g_ins, g_specs = [g], [blk]
    else:
        g_ins = list(parts)
        g_specs = [pl.BlockSpec((N_DEV, tr, cols), (lambda l, i, k=k: (0, jnp.where(l == k, i, 0), 0))) for k in range(nseg)]
    out = _sds((rows, cols), F32)
    return _call(body, name, (out, out, out, out), grid=(nseg, tps), in_specs=[blk, blk, blk, *g_specs],
                 out_specs=(blk, blk, blk, blk), vmem=VMEM_BIG)(w, m, v, *g_ins)


def _pad_rows(a, rows):
    return jnp.pad(a, ((0, rows - a.shape[0]), (0, 0)))


def _full_cols(gathered, n):
    k = gathered.shape[2]
    return jnp.transpose(gathered[0], (1, 0, 2)).reshape(k, N_DEV * n)


def _col_blocks(full, n):
    k = full.shape[0]
    return jnp.transpose(full.reshape(k, N_DEV, n), (1, 0, 2))[None]


def kernel(x, c, mod_w, mod_b, ln1_g, ln1_b, ln2_g, ln2_b, ffn_w_in, ffn_w_out, gm_w_in, gm_b_in, gm_ln_g, gm_ln_b, gm_w_s, gm_b_s, gm_w_out, fox_w_in, fox_b_f, fox_w_out, sb_w_in, sb_w_out, cv_w_in, cv_b_in, cv_dw, cv_dw_b, cv_ln_g, cv_ln_b, cv_w_out, cv_b_out, loss_target, m_mod_w, m_mod_b, m_ln1_g, m_ln1_b, m_ln2_g, m_ln2_b, m_ffn_w_in, m_ffn_w_out, m_gm_w_in, m_gm_b_in, m_gm_ln_g, m_gm_ln_b, m_gm_w_s, m_gm_b_s, m_gm_w_out, m_fox_w_in, m_fox_b_f, m_fox_w_out, m_sb_w_in, m_sb_w_out, m_cv_w_in, m_cv_b_in, m_cv_dw, m_cv_dw_b, m_cv_ln_g, m_cv_ln_b, m_cv_w_out, m_cv_b_out, v_mod_w, v_mod_b, v_ln1_g, v_ln1_b, v_ln2_g, v_ln2_b, v_ffn_w_in, v_ffn_w_out, v_gm_w_in, v_gm_b_in, v_gm_ln_g, v_gm_ln_b, v_gm_w_s, v_gm_b_s, v_gm_w_out, v_fox_w_in, v_fox_b_f, v_fox_w_out, v_sb_w_in, v_sb_w_out, v_cv_w_in, v_cv_b_in, v_cv_dw, v_cv_dw_b, v_cv_ln_g, v_cv_ln_b, v_cv_w_out, v_cv_b_out):
    weights = dict(mod_w=mod_w, mod_b=mod_b, ln1_g=ln1_g, ln1_b=ln1_b, ln2_g=ln2_g, ln2_b=ln2_b, ffn_w_in=ffn_w_in, ffn_w_out=ffn_w_out, gm_w_in=gm_w_in, gm_b_in=gm_b_in, gm_ln_g=gm_ln_g, gm_ln_b=gm_ln_b, gm_w_s=gm_w_s, gm_b_s=gm_b_s, gm_w_out=gm_w_out, fox_w_in=fox_w_in, fox_b_f=fox_b_f, fox_w_out=fox_w_out, sb_w_in=sb_w_in, sb_w_out=sb_w_out, cv_w_in=cv_w_in, cv_b_in=cv_b_in, cv_dw=cv_dw, cv_dw_b=cv_dw_b, cv_ln_g=cv_ln_g, cv_ln_b=cv_ln_b, cv_w_out=cv_w_out, cv_b_out=cv_b_out)
    mom1 = dict(mod_w=m_mod_w, mod_b=m_mod_b, ln1_g=m_ln1_g, ln1_b=m_ln1_b, ln2_g=m_ln2_g, ln2_b=m_ln2_b, ffn_w_in=m_ffn_w_in, ffn_w_out=m_ffn_w_out, gm_w_in=m_gm_w_in, gm_b_in=m_gm_b_in, gm_ln_g=m_gm_ln_g, gm_ln_b=m_gm_ln_b, gm_w_s=m_gm_w_s, gm_b_s=m_gm_b_s, gm_w_out=m_gm_w_out, fox_w_in=m_fox_w_in, fox_b_f=m_fox_b_f, fox_w_out=m_fox_w_out, sb_w_in=m_sb_w_in, sb_w_out=m_sb_w_out, cv_w_in=m_cv_w_in, cv_b_in=m_cv_b_in, cv_dw=m_cv_dw, cv_dw_b=m_cv_dw_b, cv_ln_g=m_cv_ln_g, cv_ln_b=m_cv_ln_b, cv_w_out=m_cv_w_out, cv_b_out=m_cv_b_out)
    mom2 = dict(mod_w=v_mod_w, mod_b=v_mod_b, ln1_g=v_ln1_g, ln1_b=v_ln1_b, ln2_g=v_ln2_g, ln2_b=v_ln2_b, ffn_w_in=v_ffn_w_in, ffn_w_out=v_ffn_w_out, gm_w_in=v_gm_w_in, gm_b_in=v_gm_b_in, gm_ln_g=v_gm_ln_g, gm_ln_b=v_gm_ln_b, gm_w_s=v_gm_w_s, gm_b_s=v_gm_b_s, gm_w_out=v_gm_w_out, fox_w_in=v_fox_w_in, fox_b_f=v_fox_b_f, fox_w_out=v_fox_w_out, sb_w_in=v_sb_w_in, sb_w_out=v_sb_w_out, cv_w_in=v_cv_w_in, cv_b_in=v_cv_b_in, cv_dw=v_cv_dw, cv_dw_b=v_cv_dw_b, cv_ln_g=v_cv_ln_g, cv_ln_b=v_cv_ln_b, cv_w_out=v_cv_w_out, cv_b_out=v_cv_b_out)
    names = list(weights)

    nb, seq, d = x.shape
    t = nb * seq
    nl = mod_w.shape[0]
    alpha = (2.0 * nl) ** 0.25
    me = 4 * lax.axis_index("x") + 2 * lax.axis_index("y") + lax.axis_index("c")
    xs = x.reshape(t, d)
    tgt = loss_target.reshape(t, d)
    n_mod = mod_w.shape[2]
    n_ffn = ffn_w_in.shape[2]
    n_heads = d // HEAD_DIM
    npair = d // LANES

    cvp = d // N_DEV
    cv_small = jnp.concatenate([_pad_rows(cv_dw[0], CONV_HALO), cv_dw_b, cv_ln_g, cv_ln_b, cv_b_out,
                                cv_b_in.reshape(2, cvp), jnp.zeros((2, cvp), F32)], axis=0)
    cv_packed = cv_small.reshape(-1, d)
    first = _exchange_small(jnp.concatenate([_pad_rows(c, 8), _pad_rows(cv_packed, 8)], axis=0), "gather_c", False)
    c_all = first[:, :nb].reshape(N_DEV * nb, d)
    cv_all = first[:, 8:8 + cv_packed.shape[0]].reshape(N_DEV, cv_small.shape[0], cvp)
    mod_b_loc = lax.dynamic_slice_in_dim(mod_b, me * n_mod, n_mod, axis=1)[:, None, :]
    mod_loc = _mod_fwd(c_all, mod_w, mod_b_loc)
    mod_g = _exchange_small(mod_loc.reshape(nl * N_DEV * nb, n_mod), "gather_mod", False)
    mod_all = jnp.transpose(mod_g.reshape(N_DEV, nl, N_DEV * nb, n_mod), (1, 2, 0, 3)).reshape(nl, N_DEV * nb, N_DEV * n_mod)
    mod_me = lax.dynamic_slice_in_dim(mod_all, me * nb, nb, axis=1)
    mods = [[mod_me[l, :, k * d:(k + 1) * d][:, None, :] for k in range(6)] for l in range(nl)]

    assert nl == 4, "the exchange schedule below is written for the four-layer trunk"
    big = ["ffn_w_in", "ffn_w_out", "gm_w_in", "gm_w_out", "fox_w_in", "fox_w_out", "sb_w_in", "sb_w_out", "cv_w_in", "cv_w_out"]
    shard = {n: weights[n].astype(BF16) for n in big if not n.startswith("ffn")}
    for l in range(nl):
        shard["ffn_w_in", l] = ffn_w_in[l:l + 1].astype(BF16)
        shard["ffn_w_out", l] = ffn_w_out[l:l + 1].astype(BF16)
    now = ["gm_w_in", "gm_w_out"]
    with_gm_in = [("ffn_w_in", 0)]
    with_gm_gate = [("ffn_w_out", 0)]
    with_ffn_0 = ["fox_w_in", "fox_w_out"]
    later = [("ffn_w_in", 1), ("ffn_w_out", 1), "sb_w_in", "sb_w_out", ("ffn_w_in", 2), ("ffn_w_out", 2),
             "cv_w_in", "cv_w_out", ("ffn_w_in", 3), ("ffn_w_out", 3)]
    gathered = {}
    w_ffn_out_rows = lambda l: gathered["ffn_w_out", l].reshape(N_DEV // 2, n_ffn, d)
    sq = lambda n: gathered[n].reshape(d, d)
    cv_rows = jnp.transpose(cv_all, (1, 0, 2)).reshape(cv_small.shape[0], d)
    cv_dw_f, cv_dwb_f, cv_lng_f, cv_lnb_f, cv_bout_f = (cv_rows[:CONV_HALO], cv_rows[32:33], cv_rows[33:34], cv_rows[34:35], cv_rows[35:36])
    cv_bin_f = cv_all[:, 36:38, :].reshape(1, 2 * d)

    saved = []
    h, arrived = _modulate(xs, mods[0][1], mods[0][0], seq, beside=("gather", [shard[n] for n in now]))
    gathered.update(zip(now, arrived))
    xin = xs
    for l in range(nl):
        kind = l % 4
        sv = dict(x=xin, h=h)
        if kind == 0:
            pre, arrived = _proj_cols("gm_in", h, gathered["gm_w_in"], gm_w_in.shape[2], bias=gm_b_in,
                                      beside=("gather", [shard[n] for n in with_gm_in]))
            gathered.update(zip(with_gm_in, arrived))
            yv, arrived = _gm_spatial_fwd(pre, gm_ln_g, gm_ln_b, gm_w_s[0], jnp.transpose(gm_b_s[0]), seq,
                                          beside=("gather", [shard[n] for n in with_gm_gate]))
            gathered.update(zip(with_gm_gate, arrived))
            y = _mm_plain("gm_out", yv, sq("gm_w_out"), NN)
            sv.update(pre=pre, yv=yv)
        elif kind == 1:
            qkv = _qkv_proj("fox_qkv", h, fox_qkv_w)
            ft = _mm("fox_gate_proj", fox_f_wt, h, _sds((n_heads, t), F32), (t // min(TMM, t), 1),
                     pl.BlockSpec((n_heads, d), lambda i, r: (0, 0)), pl.BlockSpec((min(TMM, t), d), lambda i, r: (i, 0)),
                     pl.BlockSpec((n_heads, min(TMM, t)), lambda i, r: (0, i)), NT)
            b_f = jnp.transpose(fox_b_f)
            frow_p = _fox_gate_fwd(ft, b_f, seq).reshape(npair, 2, t)
            o, lse, *arrived = _fox_fwd(qkv, frow_p, nb, seq, gather=[shard[n] for n in later])
            gathered.update(zip(later, arrived))
            sb_qkv_w = _full_cols(gathered["sb_w_in"], sb_w_in.shape[2])
            y = _mm_plain("fox_out", o, sq("fox_w_out"), NN)
            sv.update(qkv=qkv, ft=ft, b_f=b_f, frow=frow_p, o=o, lse=lse)
        elif kind == 2:
            qkv = _qkv_proj("sb_qkv", h, sb_qkv_w)
            o, ltot = _sb_fwd(qkv, nb, seq)
            y = _mm_plain("sb_out", o, sq("sb_w_out"), NN)
            sv.update(qkv=qkv, o=o, ltot=ltot)
        else:
            pre = _proj_cols("cv_in", h, gathered["cv_w_in"], cv_w_in.shape[2], bias=cv_bin_f)
            y1, y2 = _cv_conv_fwd(pre, cv_dw_f, cv_dwb_f, cv_lng_f, cv_lnb_f, seq)
            y = _mm_plain("cv_out", y2, sq("cv_w_out"), NN, bias=cv_bout_f)
            sv.update(pre=pre, y1=y1, y2=y2)
        x1, h2 = _lnres_fwd(xin, y, mods[l][2], ln1_g[l:l + 1], ln1_b[l:l + 1], alpha, seq, nxt=(mods[l][4], mods[l][3]))
        if l == 0:
            (hg, hu, act), arrived = _ffn_in(h2, gathered["ffn_w_in", l], 0, beside=("gather", [shard[n] for n in with_ffn_0]))
            gathered.update(zip(with_ffn_0, arrived))
            fox_full = _full_cols(gathered["fox_w_in"], fox_w_in.shape[2])
            fox_qkv_w, fox_f_wt = fox_full[:, :3 * d], jnp.transpose(fox_full[:, 3 * d:])
        else:
            hg, hu, act = _ffn_in(h2, gathered["ffn_w_in", l], 0)
        y2f = _ffn_out(act, w_ffn_out_rows(l), 0)
        sv.update(y=y, x1=x1, h2=h2, hg=hg, hu=hu, act=act, y2f=y2f)
        if l + 1 < nl:
            xin, h = _lnres_fwd(x1, y2f, mods[l][5], ln2_g[l:l + 1], ln2_b[l:l + 1], alpha, seq, nxt=(mods[l + 1][1], mods[l + 1][0]))
        else:
            dx, sq_err = _lnres_fwd(x1, y2f, mods[l][5], ln2_g[l:l + 1], ln2_b[l:l + 1], alpha, seq, tgt=tgt)
        saved.append(sv)

    loss = lax.psum(0.5 * jnp.sum(sq_err) / d, ("x", "y", "c"))

    small = {}
    bigg = {}
    recv = {}
    dmod_parts = [dict() for _ in range(nl)]
    pending = None
    d_ln = dict(ln1_g=[None] * nl, ln1_b=[None] * nl, ln2_g=[None] * nl, ln2_b=[None] * nl)
    beside_sb = [("ffn_w_in", 3), ("ffn_w_out", 3), "cv_w_in", "cv_w_out", ("ffn_w_in", 2), ("ffn_w_out", 2)]
    beside_fox = ["sb_w_in", "sb_w_out", ("ffn_w_in", 1), ("ffn_w_out", 1)]
    beside_dact_0 = ["fox_w_in", "fox_w_out"]
    beside_dh_0 = [("ffn_w_in", 0)]
    beside_gm = [("ffn_w_out", 0), "gm_w_out"]
    beside_gm_in = ["gm_w_in"]
    for l in reversed(range(nl)):
        sv = saved[l]
        kind = l % 4
        if pending is None:
            dxr, dy2, dlg, dlb, _, dgate2 = _lnres_bwd(dx, sv["x1"], sv["y2f"], mods[l][5], ln2_g[l:l + 1], alpha, seq)
        else:
            dxr, dy2, dlg, dlb, _, dgate2, dsc1, dsh1 = _lnres_bwd(pending[1], sv["x1"], sv["y2f"], mods[l][5], ln2_g[l:l + 1], alpha, seq,
                                                                   through=(pending[0], pending[2], ln2_b[l:l + 1]))
            dmod_parts[l + 1].update(sc1=dsc1, sh1=dsh1)
        d_ln["ln2_g"][l], d_ln["ln2_b"][l] = dlg, dlb
        if l == 0:
            (dg_, du_), landed = _ffn_dact(dy2, w_ffn_out_rows(l), sv["hg"], sv["hu"], 0, beside=("scatter", [bigg[n] for n in beside_dact_0]))
            recv.update(zip(beside_dact_0, landed))
        else:
            dg_, du_ = _ffn_dact(dy2, w_ffn_out_rows(l), sv["hg"], sv["hu"], 0)
        dwi, dwo = _ffn_bwd_weights(sv["h2"], dy2, sv["act"], dg_, du_)
        bigg["ffn_w_in", l] = dwi[None]
        bigg["ffn_w_out", l] = dwo.reshape(1, N_DEV, n_ffn // 2, d)
        if l == 0:
            dh2, landed = _ffn_dh(dg_, du_, gathered["ffn_w_in", l], 0, beside=("scatter", [bigg[n] for n in beside_dh_0]))
            recv.update(zip(beside_dh_0, landed))
        else:
            dh2 = _ffn_dh(dg_, du_, gathered["ffn_w_in", l], 0)
        dxr, dy, dlg, dlb, dysum, dgate1, dsc2, dsh2 = _lnres_bwd(dxr, sv["x"], sv["y"], mods[l][2], ln1_g[l:l + 1], alpha, seq,
                                                                  through=(dh2, mods[l][4], ln1_b[l:l + 1]))
        d_ln["ln1_g"][l], d_ln["ln1_b"][l] = dlg, dlb
        hh = sv["h"]
        if kind == 0:
            dyv = _mm_plain("gm_out_bwd", dy, sq("gm_w_out"), NT)
            bigg["gm_w_out"] = _mm_plain("gm_out_dw", sv["yv"], dy, TN, GRAD_WIRE).reshape(1, N_DEV, d // N_DEV, d)
            (dpre, dws, dbst, dlng, dlnb, dbin), landed = _gm_spatial_bwd(sv["pre"], dyv, gm_ln_g, gm_ln_b, gm_w_s[0], jnp.transpose(gm_b_s[0]), seq,
                                                                          beside=("scatter", [bigg[n] for n in beside_gm]))
            recv.update(zip(beside_gm, landed))
            small.update(gm_w_s=dws[None], gm_b_s=jnp.transpose(dbst)[None], gm_ln_g=dlng, gm_ln_b=dlnb, gm_b_in=dbin)
            bigg["gm_w_in"] = _grad_cols("gm_in_dw", hh, dpre, gm_w_in.shape[2])
            dh, landed = _back_cols("gm_in_bwd", dpre, gathered["gm_w_in"], gm_w_in.shape[2], beside=("scatter", [bigg[n] for n in beside_gm_in]))
            recv.update(zip(beside_gm_in, landed))
        elif kind == 1:
            do = _mm_plain("fox_out_bwd", dy, sq("fox_w_out"), NT)
            bigg["fox_w_out"] = _mm_plain("fox_out_dw", sv["o"], dy, TN, GRAD_WIRE).reshape(1, N_DEV, d // N_DEV, d)
            dqkv, dfr, dfq, *landed = _fox_bwd(sv["qkv"], sv["frow"], sv["o"], do, sv["lse"], nb, seq,
                                               scatter=[bigg[n] for n in beside_fox])
            recv.update(zip(beside_fox, landed))
            dft, dbf = _fox_gate_bwd(sv["ft"], sv["b_f"], dfr.reshape(n_heads, t),
                                     jnp.transpose(dfq, (0, 2, 1)).reshape(n_heads, t), seq)
            small["fox_b_f"] = jnp.transpose(dbf)
            dw_qkv = _qkv_dw("fox_qkv_dw", hh, dqkv)
            tk = min(TMM, t)
            dw_ft = _mm("fox_gate_dw", dft, hh, _sds((n_heads, d), F32), (1, t // tk),
                        pl.BlockSpec((n_heads, tk), lambda j, r: (0, r)), pl.BlockSpec((tk, d), lambda j, r: (r, 0)),
                        pl.BlockSpec((n_heads, d), lambda j, r: (0, 0)), NN)
            bigg["fox_w_in"] = _col_blocks(jnp.concatenate([dw_qkv, jnp.transpose(dw_ft).astype(GRAD_WIRE)], axis=1), fox_w_in.shape[2])
            dh_a = _qkv_dh("fox_qkv_bwd", dqkv, fox_qkv_w)
            tm = min(TMM, t)
            dh = _mm("fox_gate_bwd_h", dft, fox_f_wt, _sds((t, d), F32), (t // tm, 1),
                     pl.BlockSpec((n_heads, tm), lambda i, r: (0, i)), pl.BlockSpec((n_heads, d), lambda i, r: (0, 0)),
                     pl.BlockSpec((tm, d), lambda i, r: (i, 0)), TN, (dh_a,), (pl.BlockSpec((tm, d), lambda i, r: (i, 0)),), _add)
        elif kind == 2:
            do = _mm_plain("sb_out_bwd", dy, sq("sb_w_out"), NT)
            bigg["sb_w_out"] = _mm_plain("sb_out_dw", sv["o"], dy, TN, GRAD_WIRE).reshape(1, N_DEV, d // N_DEV, d)
            dqkv, *landed = _sb_bwd(sv["qkv"], do, sv["ltot"], nb, seq, scatter=[bigg[n] for n in beside_sb])
            recv.update(zip(beside_sb, landed))
            bigg["sb_w_in"] = _col_blocks(_qkv_dw("sb_qkv_dw", hh, dqkv), sb_w_in.shape[2])
            dh = _qkv_dh("sb_qkv_bwd", dqkv, sb_qkv_w)
        else:
            dy2c = _mm_plain("cv_out_bwd", dy, sq("cv_w_out"), NT)
            bigg["cv_w_out"] = _mm_plain("cv_out_dw", sv["y2"], dy, TN, GRAD_WIRE).reshape(1, N_DEV, d // N_DEV, d)
            dy1, dlng, dlnb, ddwb = _cv_norm_bwd(sv["y1"], dy2c, cv_lng_f, cv_lnb_f)
            dpre, ddw, dbin = _cv_conv_bwd(sv["pre"], dy1, cv_dw_f, seq)
            small.update(cv_b_out=dysum, cv_ln_g=dlng, cv_ln_b=dlnb, cv_dw_b=ddwb, cv_dw=ddw[:CONV_WIDTH], cv_b_in=dbin)
            bigg["cv_w_in"] = _grad_cols("cv_in_dw", hh, dpre, cv_w_in.shape[2])
            dh = _back_cols("cv_in_bwd", dpre, gathered["cv_w_in"], cv_w_in.shape[2])
        pending = (dh, dxr, mods[l][1])
        dmod_parts[l].update(g1=dgate1, sh2=dsh2, sc2=dsc2, g2=dgate2)
    dx, dsc1, dsh1 = _modulate_bwd(pending[0], pending[1], saved[0]["x"], pending[2], seq)
    dmod_parts[0].update(sc1=dsc1, sh1=dsh1)
    dmods = [jnp.concatenate([p["sh1"], p["sc1"], p["g1"], p["sh2"], p["sc2"], p["g2"]], axis=2)[:, 0, :] for p in dmod_parts]
    grad_x = dx.reshape(nb, seq, d)
    for n in d_ln:
        small[n] = jnp.concatenate(d_ln[n], axis=0)

    dmod_rows = jnp.stack(dmods).reshape(nl * nb, 6 * d)
    dmod_g = _exchange_small(_pad_rows(dmod_rows, 8 * ((nl * nb + 7) // 8)), "gather_dmod", False)[:, :nl * nb]
    dmod_all = jnp.transpose(dmod_g.reshape(N_DEV, nl, nb, 6 * d), (1, 0, 2, 3)).reshape(nl, N_DEV * nb, 6 * d)
    dmod_loc = lax.dynamic_slice_in_dim(dmod_all, me * n_mod, n_mod, axis=2)
    g_mod_w, g_mod_b = _mod_bwd(c_all, dmod_loc, dmod_all)
    grads = dict(mod_w=g_mod_w, mod_b=g_mod_b[:, 0, :])

    rep = ["ln1_g", "ln1_b", "ln2_g", "ln2_b", "gm_b_in", "gm_ln_g", "gm_ln_b", "gm_w_s", "gm_b_s", "fox_b_f"]
    cvs = ["cv_b_in", "cv_dw", "cv_dw_b", "cv_ln_g", "cv_ln_b", "cv_b_out"]

    def rows_of(a):
        flat = a.reshape(-1)
        pad = (-flat.shape[0]) % d
        return jnp.pad(flat, (0, pad)).reshape(-1, d)

    pack_rows = [rows_of(small[n]) for n in rep + cvs]
    counts = [r.shape[0] for r in pack_rows]
    total = sum(counts)
    pack = _pad_rows(jnp.concatenate(pack_rows, axis=0), 8 * ((total + 7) // 8))
    summed = _exchange_small(pack, "allreduce_small", True)
    offs = [sum(counts[:i]) for i in range(len(counts))]
    rep_rows = sum(counts[:len(rep)])
    for n, o_, cnt in zip(rep + cvs, offs, counts):
        full = summed[o_:o_ + cnt].reshape(-1)
        if n in rep:
            grads[n] = full[:weights[n].size].reshape(weights[n].shape)
        else:
            wshape = weights[n].shape
            cols = wshape[-1]
            full = full[:math.prod(wshape[:-1]) * cols * N_DEV].reshape(wshape[:-1] + (cols * N_DEV,))
            grads[n] = lax.dynamic_slice_in_dim(full, me * cols, cols, axis=full.ndim - 1)

    outs = {}

    def view2(a):
        return a.reshape(-1, a.shape[-1])

    for n in big:
        w2 = view2(weights[n])
        landed = [recv[n, l] for l in range(nl)] if n.startswith("ffn") else [recv[n]]
        parts = [r.reshape(N_DEV, w2.shape[0] // len(landed), w2.shape[1]) for r in landed]
        res = _adamw("adamw_" + n, w2, view2(mom1[n]), view2(mom2[n]), parts=parts)
        outs[n] = [r.reshape(weights[n].shape) for r in res]
    res = _adamw("adamw_mod_w", view2(mod_w), view2(m_mod_w), view2(v_mod_w), g=view2(grads["mod_w"]))
    outs["mod_w"] = [r.reshape(mod_w.shape) for r in res]
    rp = lambda src: _pad_rows(jnp.concatenate([rows_of(src[n]) for n in rep], axis=0), 8 * ((rep_rows + 7) // 8))
    res = _adamw("adamw_replicated", rp(weights), rp(mom1), rp(mom2), g=rp(grads))
    for n, o_, cnt in zip(rep, offs, counts):
        outs[n] = [r[o_:o_ + cnt].reshape(-1)[:weights[n].size].reshape(weights[n].shape) for r in res]
    cv_cols = weights["cv_b_out"].shape[-1]
    cp = lambda src: jnp.concatenate([src[n].reshape(-1, cv_cols) for n in cvs], axis=0)
    cv_cnt = [weights[n].size // cv_cols for n in cvs]
    cv_tot = sum(cv_cnt)
    cpp = lambda src: _pad_rows(cp(src), 8 * ((cv_tot + 7) // 8))
    res = _adamw("adamw_cv_small", cpp(weights), cpp(mom1), cpp(mom2), g=cpp(grads))
    o_ = 0
    for n, cnt in zip(cvs, cv_cnt):
        outs[n] = [r[o_:o_ + cnt].reshape(weights[n].shape) for r in res]
        o_ += cnt
    res = _adamw("adamw_mod_b", mod_b, m_mod_b, v_mod_b, g=grads["mod_b"])
    outs["mod_b"] = list(res)

    return (loss, grad_x, *[outs[n][0] for n in names], *[outs[n][1] for n in names],
            *[outs[n][2] for n in names], *[outs[n][3] for n in names])
```

```python
import functools
import math

import jax
import jax.numpy as jnp
from jax import lax
from jax.experimental import pallas as pl
from jax.experimental.pallas import tpu as pltpu

F32 = jnp.float32
BF16 = jnp.bfloat16
MESH = pl.DeviceIdType.MESH

N_DEV = 8
HEAD_DIM = 64
LANES = 128
SUBLANES = 8
GM_CHUNK = 128
GM_GROUPS = 8
CONV_WIDTH = 31
CONV_HALO = 32
CONV_ROWS = 32
LN_EPS = 1e-5
NEG_INF = -1e30
SB_DEAD = -100.0
GRAD_WIRE = jnp.bfloat16
FFN_KEEP = jnp.bfloat16

ADAM_LR = 0.001
ADAM_B1 = 0.9
ADAM_B2 = 0.999
ADAM_EPS = 1e-08
ADAM_WD = 0.01
ADAM_STEP = 10

TM = 512
TMM = 1024
TQ = 256
FOX_FWD_UNROLL = 8
FOX_BWD_UNROLL = 4
SB_BWD_UNROLL = 2
TT = 512
VMEM_BIG = 56 * 1024 * 1024

NN = (((1,), (0,)), ((), ()))
NT = (((1,), (1,)), ((), ()))
TN = (((0,), (0,)), ((), ()))


def _call(body, name, out_shape, grid=None, in_specs=None, out_specs=None, scratch=(), vmem=None, aliases=None):
    params = {}
    if grid is not None:
        params["dimension_semantics"] = ("arbitrary",) * len(grid)
    if vmem is not None:
        params["vmem_limit_bytes"] = vmem
    kw = {}
    if grid is not None:
        kw["grid"] = grid
    if in_specs is not None:
        kw["in_specs"] = in_specs
    if out_specs is not None:
        kw["out_specs"] = out_specs
    if aliases is not None:
        kw["input_output_aliases"] = aliases
    return pl.pallas_call(body, name=name, out_shape=out_shape, scratch_shapes=list(scratch),
                          compiler_params=pltpu.CompilerParams(**params), **kw)


def _sds(shape, dtype):
    return jax.ShapeDtypeStruct(tuple(shape), dtype)


def _dot(a, b, dims=NN):
    return lax.dot_general(a.astype(BF16), b.astype(BF16), dims, preferred_element_type=F32)


def _split3(x):
    h1 = x.astype(BF16)
    r1 = x - h1.astype(F32)
    h2 = r1.astype(BF16)
    h3 = (r1 - h2.astype(F32)).astype(BF16)
    return h1, h2, h3


def _dot_exact(x, m, dims=NN):
    h1, h2, h3 = _split3(x)
    d = lambda h: lax.dot_general(h, m, dims, preferred_element_type=F32)
    return (d(h1) + d(h2)) + d(h3)


def _dot_exact_rhs(m, x, dims=NN):
    h1, h2, h3 = _split3(x)
    d = lambda h: lax.dot_general(m, h, dims, preferred_element_type=F32)
    return (d(h1) + d(h2)) + d(h3)


def _silu(x):
    return x * jax.nn.sigmoid(x)


def _gelu(x):
    return 0.5 * x * (1.0 + lax.erf(x * (2.0 ** -0.5)))


def _gelu_grad(x):
    return 0.5 * (1.0 + lax.erf(x * (2.0 ** -0.5))) + x * jnp.exp(-0.5 * x * x) * ((2.0 * math.pi) ** -0.5)


def _log_sigmoid(z):
    return jnp.minimum(z, 0.0) - jnp.log(1.0 + jnp.exp(-jnp.abs(z)))


def _ln_stats(r):
    mu = jnp.mean(r, axis=-1, keepdims=True)
    rc = r - mu
    var = jnp.mean(rc * rc, axis=-1, keepdims=True)
    return rc, lax.rsqrt(var + LN_EPS)


def _colsum(x):
    return jnp.sum(x, axis=0, keepdims=True)


def _peers():
    mx, my, mc = lax.axis_index("x"), lax.axis_index("y"), lax.axis_index("c")
    me = 4 * mx + 2 * my + mc
    out = []
    for k in range(1, N_DEV):
        px = 1 - mx if (k >> 2) & 1 else mx
        py = 1 - my if (k >> 1) & 1 else my
        pc = 1 - mc if k & 1 else mc
        out.append(((px, py, pc), 4 * px + 2 * py + pc))
    return me, out


def _exchange_small(x, name, reduce, sum_from=None):
    rows, cols = x.shape

    def body(x_ref, o_ref, *rest):
        if sum_from is not None:
            s_ref, rest = rest[0], rest[1:]
        if reduce:
            land, send_sems, recv_sems, local_sem = rest
        else:
            land = o_ref
            send_sems, recv_sems, local_sem = rest
        me, peers = _peers()
        mine = pltpu.make_async_copy(x_ref, land.at[me], local_sem)
        mine.start()
        sends = []
        for k, (peer, _) in enumerate(peers):
            cp = pltpu.make_async_remote_copy(src_ref=x_ref, dst_ref=land.at[me], send_sem=send_sems.at[k],
                                              recv_sem=recv_sems.at[k], device_id=peer, device_id_type=MESH)
            cp.start()
            sends.append(cp)
        for k, (peer, blk) in enumerate(peers):
            pltpu.make_async_remote_copy(src_ref=x_ref, dst_ref=land.at[blk], send_sem=send_sems.at[k],
                                         recv_sem=recv_sems.at[k], device_id=peer, device_id_type=MESH).wait_recv()
        for cp in sends:
            cp.wait_send()
        mine.wait()
        if reduce:
            acc = land[0]
            for s in range(1, N_DEV):
                acc = acc + land[s]
            o_ref[...] = acc
        if sum_from is not None:
            acc = land[0, sum_from:]
            for s in range(1, N_DEV):
                acc = acc + land[s, sum_from:]
            s_ref[...] = acc

    vm = pl.BlockSpec(memory_space=pltpu.VMEM)
    scratch = [pltpu.SemaphoreType.DMA((N_DEV - 1,)), pltpu.SemaphoreType.DMA((N_DEV - 1,)), pltpu.SemaphoreType.DMA]
    if reduce:
        scratch = [pltpu.VMEM((N_DEV, rows, cols), F32)] + scratch
        out = _sds((rows, cols), F32)
    else:
        out = _sds((N_DEV, rows, cols), F32)
    if sum_from is not None:
        return _call(body, name, (out, _sds((rows - sum_from, cols), F32)), in_specs=[vm], out_specs=[vm, vm],
                     scratch=scratch, vmem=VMEM_BIG)(x)
    return _call(body, name, out, in_specs=[vm], out_specs=vm, scratch=scratch, vmem=VMEM_BIG)(x)


def _comm_scratch(n):
    return [pltpu.SemaphoreType.DMA((n, N_DEV - 1)), pltpu.SemaphoreType.DMA((n, N_DEV - 1)), pltpu.SemaphoreType.DMA((n,))]


def _gather_stage(stage, ins, outs, send_sems, recv_sems, local_sems):
    n = len(ins)
    mx, my, mc = lax.axis_index("x"), lax.axis_index("y"), lax.axis_index("c")
    here, sibling = (mx, my, mc), (mx, my, 1 - mc)
    chips = [(1 - mx, my), (mx, 1 - my), (1 - mx, 1 - my)]

    def block(px, py, pc):
        return 4 * px + 2 * py + pc

    def copy(a, k, blk, to, src=None):
        dst = outs[a].at[:, blk]
        return pltpu.make_async_remote_copy(src_ref=dst if src is None else src, dst_ref=dst, send_sem=send_sems.at[a, k],
                                            recv_sem=recv_sems.at[a, k], device_id=to, device_id_type=MESH)

    me = block(*here)
    for a in range(n):
        local = pltpu.make_async_copy(ins[a], outs[a].at[:, me], local_sems.at[a])
        first = [copy(a, 0, me, sibling, src=ins[a])] + [copy(a, 1 + j, me, (*chip, mc), src=ins[a]) for j, chip in enumerate(chips)]
        if stage == 0:
            local.start()
            for cp in first:
                cp.start()
        if stage == 1:
            for j, chip in enumerate(chips):
                copy(a, 1 + j, block(*chip, mc), here).wait_recv()
                copy(a, 4 + j, block(*chip, mc), sibling).start()
        if stage == 2:
            copy(a, 0, block(mx, my, 1 - mc), here).wait_recv()
            for j, chip in enumerate(chips):
                copy(a, 4 + j, block(*chip, 1 - mc), here).wait_recv()
            for cp in first:
                cp.wait_send()
            for j, chip in enumerate(chips):
                copy(a, 4 + j, block(*chip, mc), sibling).wait_send()
            local.wait()


def _gathered_shapes(shards):
    return [_sds((s.shape[0], N_DEV) + s.shape[1:], s.dtype) for s in shards]


def _scattered_shapes(grads):
    return [_sds((N_DEV, g.shape[0]) + g.shape[2:], g.dtype) for g in grads]


def _run(beside, body, name, out_shape, args, grid, in_specs, out_specs, scratch=(), vmem=None):
    if beside is None:
        return _call(body, name, out_shape, grid=grid, in_specs=in_specs, out_specs=out_specs, scratch=scratch, vmem=vmem)(*args)
    kind, arrays = beside
    n = len(arrays)
    outs = tuple(out_shape) if isinstance(out_shape, (tuple, list)) else (out_shape,)
    ospecs = tuple(out_specs) if isinstance(out_specs, (tuple, list)) else (out_specs,)
    total = math.prod(grid)
    gather = kind == "gather"
    stage_fn = _gather_stage if gather else _scatter_stage
    early = {0: 0, 1: (3 * total) // 4} if gather else {0: 0}

    def carrier(*refs):
        own_in, send, own_out, land, sems, own_scratch = _carried(n, len(in_specs), len(outs), refs)
        at = 0
        for ax, size in enumerate(grid):
            at = at * size + pl.program_id(ax)
        for stage, when in early.items():
            @pl.when(at == when)
            def _(stage=stage):
                stage_fn(stage, send, land, *sems)
        body(*own_in, *own_out, *own_scratch)

        @pl.when(at == total - 1)
        def _():
            stage_fn(2 if gather else 1, send, land, *sems)

    hbm = pl.BlockSpec(memory_space=pl.ANY)
    shapes = _gathered_shapes(arrays) if gather else _scattered_shapes(arrays)
    res = _call(carrier, name, (*outs, *shapes), grid=grid, in_specs=[*in_specs, *[hbm] * n], out_specs=(*ospecs, *[hbm] * n),
                scratch=[*_comm_scratch(n), *scratch], vmem=vmem)(*args, *arrays)
    own = res[:len(outs)]
    return (own if len(own) > 1 else own[0]), list(res[len(outs):])


def _scatter_stage(stage, ins, outs, send_sems, recv_sems, local_sems):
    me, peers = _peers()
    for a in range(len(ins)):
        local = pltpu.make_async_copy(ins[a].at[:, me], outs[a].at[me], local_sems.at[a])
        sends = [pltpu.make_async_remote_copy(src_ref=ins[a].at[:, blk], dst_ref=outs[a].at[me], send_sem=send_sems.at[a, k],
                                              recv_sem=recv_sems.at[a, k], device_id=peer, device_id_type=MESH)
                 for k, (peer, blk) in enumerate(peers)]
        if stage == 0:
            local.start()
            for cp in sends:
                cp.start()
        if stage == 1:
            for k, (peer, blk) in enumerate(peers):
                pltpu.make_async_remote_copy(src_ref=ins[a].at[:, me], dst_ref=outs[a].at[blk], send_sem=send_sems.at[a, k],
                                             recv_sem=recv_sems.at[a, k], device_id=peer, device_id_type=MESH).wait_recv()
            for cp in sends:
                cp.wait_send()
            local.wait()


def _mm(name, a, b, out, grid, a_spec, b_spec, o_spec, dims, extra=(), extra_specs=(), epilogue=None, vmem=VMEM_BIG, beside=None):
    nred = grid[-1]
    red_axis = len(grid) - 1
    acc_shape = tuple(d for d in o_spec.block_shape if d is not None)
    n_extra = len(extra)

    def body(a_ref, b_ref, *rest):
        ex = rest[:n_extra]
        o_ref = rest[n_extra]

        def finish(acc):
            if epilogue is not None:
                acc = epilogue(acc, *[e[...] for e in ex])
            o_ref[...] = acc.astype(o_ref.dtype)

        prod = _dot(a_ref[...], b_ref[...], dims)
        if nred == 1:
            finish(prod)
        else:
            acc_ref = rest[n_extra + 1]
            r = pl.program_id(red_axis)

            @pl.when(r == 0)
            def _():
                acc_ref[...] = prod

            @pl.when(r > 0)
            def _():
                acc_ref[...] += prod

            @pl.when(r == nred - 1)
            def _():
                finish(acc_ref[...])

    scratch = [pltpu.VMEM(acc_shape, F32)] if nred > 1 else []
    return _run(beside, body, name, out, (a, b, *extra), grid, [a_spec, b_spec, *extra_specs], o_spec, scratch, vmem)


def _add(acc, x):
    return acc + x


def _proj_cols(name, h, w, n_slot, bias=None, out_dtype=F32, beside=None):
    t, k = h.shape
    s = w.shape[1]
    tm = min(TM, t)

    def body(h_ref, w_ref, *rest):
        o_ref = rest[-1]
        hv = h_ref[...]
        for j in range(s):
            cols = slice(j * n_slot, (j + 1) * n_slot)
            acc = _dot(hv, w_ref[j])
            if bias is not None:
                acc = acc + rest[0][:, cols]
            o_ref[:, cols] = acc.astype(o_ref.dtype)

    ins, in_specs = [h, w], [pl.BlockSpec((tm, k), lambda i: (i, 0)), pl.BlockSpec((None, s, k, n_slot), lambda i: (0, 0, 0, 0))]
    if bias is not None:
        ins.append(bias)
        in_specs.append(pl.BlockSpec((1, s * n_slot), lambda i: (0, 0)))
    return _run(beside, body, name, _sds((t, s * n_slot), out_dtype), tuple(ins), (t // tm,), in_specs,
                pl.BlockSpec((tm, s * n_slot), lambda i: (i, 0)), vmem=VMEM_BIG)


def _accumulate_over_tokens(name, ins, in_specs, out, o_spec, acc_shape, n_steps, terms, store=None, fill=None):
    n_in = len(ins) + (fill is not None)

    def body(*refs):
        o_ref, acc_ref = refs[n_in], refs[n_in + 1]
        r = pl.program_id(0)

        @pl.when(r == 0)
        def _():
            acc_ref[...] = jnp.zeros_like(acc_ref)

        for s, prod in enumerate(terms(*refs[:len(ins)])):
            acc_ref[s] += prod

        @pl.when(r == n_steps - 1)
        def _():
            if store is None:
                o_ref[...] = acc_ref[...].reshape(o_ref.shape).astype(o_ref.dtype)
            else:
                store(o_ref, acc_ref)

    if fill is None:
        return _call(body, name, out, grid=(n_steps,), in_specs=in_specs, out_specs=o_spec,
                     scratch=[pltpu.VMEM(acc_shape, F32)], vmem=VMEM_BIG)(*ins)
    return _call(body, name, out, grid=(n_steps,), in_specs=[*in_specs, pl.BlockSpec(memory_space=pl.ANY)], out_specs=o_spec,
                 scratch=[pltpu.VMEM(acc_shape, F32)], vmem=VMEM_BIG, aliases={len(ins): 0})(*ins, fill)


def _grad_cols(name, h, g, n_slot):
    t, k = h.shape
    s = g.shape[1] // n_slot
    tk = min(TMM, t)

    def terms(h_ref, g_ref):
        hv = h_ref[...]
        return [_dot(hv, g_ref[:, j * n_slot:(j + 1) * n_slot], TN) for j in range(s)]

    return _accumulate_over_tokens(name, (h, g), [pl.BlockSpec((tk, k), lambda r: (r, 0)), pl.BlockSpec((tk, s * n_slot), lambda r: (r, 0))],
                                   _sds((1, s, k, n_slot), GRAD_WIRE), pl.BlockSpec((1, s, k, n_slot), lambda r: (0, 0, 0, 0)),
                                   (s, k, n_slot), t // tk, terms)


def _back_cols(name, g, w, n_slot, beside=None):
    t = g.shape[0]
    s, k = w.shape[1], w.shape[2]
    tm = min(TM, t)

    def body(g_ref, w_ref, o_ref):
        acc = None
        for j in range(s):
            term = _dot(g_ref[:, j * n_slot:(j + 1) * n_slot], w_ref[j], NT)
            acc = term if acc is None else acc + term
        o_ref[...] = acc

    return _run(beside, body, name, _sds((t, k), F32), (g, w), (t // tm,),
                [pl.BlockSpec((tm, s * n_slot), lambda i: (i, 0)), pl.BlockSpec((None, s, k, n_slot), lambda i: (0, 0, 0, 0))],
                pl.BlockSpec((tm, k), lambda i: (i, 0)), vmem=VMEM_BIG)


def _mm_plain(name, a, b, dims, out_dtype=F32, bias=None):
    if dims == TN:
        t, k = a.shape
        n = b.shape[1]
        tk = min(TMM, t)
        return _mm(name, a, b, _sds((k, n), out_dtype), (1, t // tk),
                   pl.BlockSpec((tk, k), lambda j, r: (r, 0)), pl.BlockSpec((tk, n), lambda j, r: (r, 0)),
                   pl.BlockSpec((k, n), lambda j, r: (0, 0)), TN)
    t = a.shape[0]
    tm = min(TMM, t)
    n = b.shape[1] if dims == NN else b.shape[0]
    extra, especs, epi = (), (), None
    if bias is not None:
        extra, especs, epi = (bias,), (pl.BlockSpec((1, n), lambda i, r: (0, 0)),), _add
    return _mm(name, a, b, _sds((t, n), out_dtype), (t // tm, 1),
               pl.BlockSpec((tm, a.shape[1]), lambda i, r: (i, 0)), pl.BlockSpec(b.shape, lambda i, r: (0, 0)),
               pl.BlockSpec((tm, n), lambda i, r: (i, 0)), dims, extra, especs, epi)


def _mod_fwd(c_all, mod_w, mod_b_loc):
    nl, d, n = mod_w.shape
    nb = c_all.shape[0]

    def body(c_ref, w_ref, b_ref, o_ref):
        o_ref[...] = _dot(_silu(c_ref[...]), w_ref[...]) + b_ref[...]

    return _call(body, "mod_fwd", _sds((nl, nb, n), F32), grid=(nl,),
                 in_specs=[pl.BlockSpec((nb, d), lambda l: (0, 0)), pl.BlockSpec((None, d, n), lambda l: (l, 0, 0)),
                           pl.BlockSpec((None, 1, n), lambda l: (l, 0, 0))],
                 out_specs=pl.BlockSpec((None, nb, n), lambda l: (l, 0, 0)))(c_all, mod_w, mod_b_loc)


def _mod_bwd(c_all, dmod_loc, dmod_all):
    nl, nb, n = dmod_loc.shape
    d = c_all.shape[1]
    n_all = dmod_all.shape[2]

    def body(c_ref, dl_ref, da_ref, gw_ref, gb_ref):
        gw_ref[...] = _dot(_silu(c_ref[...]), dl_ref[...], TN)
        gb_ref[...] = _colsum(da_ref[...])

    return _call(body, "mod_bwd", (_sds((nl, d, n), F32), _sds((nl, 1, n_all), F32)), grid=(nl,),
                 in_specs=[pl.BlockSpec((nb, d), lambda l: (0, 0)), pl.BlockSpec((None, nb, n), lambda l: (l, 0, 0)),
                           pl.BlockSpec((None, nb, n_all), lambda l: (l, 0, 0))],
                 out_specs=(pl.BlockSpec((None, d, n), lambda l: (l, 0, 0)), pl.BlockSpec((None, 1, n_all), lambda l: (l, 0, 0))),
                 )(c_all, dmod_loc, dmod_all)


def _row_spec(d, tpb):
    return pl.BlockSpec((None, 1, d), lambda i: (i // tpb, 0, 0))


def _tile_spec(tm, d):
    return pl.BlockSpec((tm, d), lambda i: (i, 0))


def _vec_spec(d):
    return pl.BlockSpec((1, d), lambda i: (0, 0))


def _modulate(x, sc, sh, seq, beside=None):
    t, d = x.shape
    tm = min(TM, seq)
    tpb = seq // tm

    def body(x_ref, sc_ref, sh_ref, h_ref):
        h_ref[...] = (x_ref[...] * (1.0 + sc_ref[...]) + sh_ref[...]).astype(BF16)

    return _run(beside, body, "modulate", _sds((t, d), BF16), (x, sc, sh), (t // tm,),
                [_tile_spec(tm, d), _row_spec(d, tpb), _row_spec(d, tpb)], _tile_spec(tm, d))


def _lnres_fwd(x, y, gate, lg, lb, alpha, seq, nxt=None, tgt=None):
    t, d = x.shape
    tm = min(TM, seq)
    tpb = seq // tm

    def body(x_ref, y_ref, g_ref, lg_ref, lb_ref, *rest):
        r = alpha * x_ref[...] + (1.0 + g_ref[...]) * y_ref[...]
        rc, rstd = _ln_stats(r)
        xn = rc * rstd * lg_ref[...] + lb_ref[...]
        if nxt is None:
            t_ref, dx_ref, sq_ref = rest
            e = xn - t_ref[...]
            dx_ref[...] = e * (1.0 / d)

            @pl.when(pl.program_id(0) == 0)
            def _():
                sq_ref[...] = jnp.zeros_like(sq_ref)

            sq_ref[...] += _colsum(e * e)
        else:
            sc_ref, sh_ref, xo_ref, h_ref = rest
            xo_ref[...] = xn
            h_ref[...] = (xn * (1.0 + sc_ref[...]) + sh_ref[...]).astype(BF16)

    ins = [_tile_spec(tm, d), _tile_spec(tm, d), _row_spec(d, tpb), _vec_spec(d), _vec_spec(d)]
    if nxt is None:
        return _call(body, "lnres_fwd_loss", (_sds((t, d), F32), _sds((1, d), F32)), grid=(t // tm,), in_specs=ins + [_tile_spec(tm, d)],
                     out_specs=(_tile_spec(tm, d), _vec_spec(d)))(x, y, gate, lg, lb, tgt)
    return _call(body, "lnres_fwd", (_sds((t, d), F32), _sds((t, d), BF16)), grid=(t // tm,),
                 in_specs=ins + [_row_spec(d, tpb), _row_spec(d, tpb)],
                 out_specs=(_tile_spec(tm, d), _tile_spec(tm, d)))(x, y, gate, lg, lb, *nxt)


def _lnres_bwd(dxo, x, y, gate, lg, alpha, seq, through=None):
    t, d = x.shape
    tm = min(TM, seq)
    tpb = seq // tm
    nb = t // seq

    def body(dxo_ref, x_ref, y_ref, g_ref, lg_ref, *rest):
        if through is None:
            dxr_ref, dy_ref, dlg_ref, dlb_ref, dys_ref, dg_ref = rest
        else:
            dh_ref, sc_ref, lb_ref, dxr_ref, dy_ref, dlg_ref, dlb_ref, dys_ref, dg_ref, dsc_ref, dsh_ref = rest
        i = pl.program_id(0)
        yv = y_ref[...]
        r = alpha * x_ref[...] + (1.0 + g_ref[...]) * yv
        rc, rstd = _ln_stats(r)
        xhat = rc * rstd
        dxo_v = dxo_ref[...]
        if through is not None:
            dhv = dh_ref[...]
            dxo_v = dxo_v + dhv * (1.0 + sc_ref[...])

            @pl.when(i % tpb == 0)
            def _():
                dsc_ref[...] = jnp.zeros_like(dsc_ref)
                dsh_ref[...] = jnp.zeros_like(dsh_ref)

            dsc_ref[...] += _colsum(dhv * (xhat * lg_ref[...] + lb_ref[...]))
            dsh_ref[...] += _colsum(dhv)
        dxh = dxo_v * lg_ref[...]
        m1 = jnp.mean(dxh, axis=-1, keepdims=True)
        m2 = jnp.mean(dxh * xhat, axis=-1, keepdims=True)
        dr = rstd * (dxh - m1 - xhat * m2)
        dyv = (1.0 + g_ref[...]) * dr
        dxr_ref[...] = alpha * dr
        dy_ref[...] = dyv.astype(BF16)

        @pl.when(i == 0)
        def _():
            dlg_ref[...] = jnp.zeros_like(dlg_ref)
            dlb_ref[...] = jnp.zeros_like(dlb_ref)
            dys_ref[...] = jnp.zeros_like(dys_ref)

        @pl.when(i % tpb == 0)
        def _():
            dg_ref[...] = jnp.zeros_like(dg_ref)

        dlg_ref[...] += _colsum(dxo_v * xhat)
        dlb_ref[...] += _colsum(dxo_v)
        dys_ref[...] += _colsum(dyv)
        dg_ref[...] += _colsum(dr * yv)

    out = [_sds((t, d), F32), _sds((t, d), BF16), _sds((1, d), F32), _sds((1, d), F32), _sds((1, d), F32), _sds((nb, 1, d), F32)]
    out_specs = [_tile_spec(tm, d), _tile_spec(tm, d), _vec_spec(d), _vec_spec(d), _vec_spec(d), _row_spec(d, tpb)]
    ins = [dxo, x, y, gate, lg]
    in_specs = [_tile_spec(tm, d), _tile_spec(tm, d), _tile_spec(tm, d), _row_spec(d, tpb), _vec_spec(d)]
    if through is not None:
        ins += list(through)
        in_specs += [_tile_spec(tm, d), _row_spec(d, tpb), _vec_spec(d)]
        out += [_sds((nb, 1, d), F32), _sds((nb, 1, d), F32)]
        out_specs += [_row_spec(d, tpb), _row_spec(d, tpb)]
    return _call(body, "lnres_bwd" if through is None else "lnres_mod_bwd", tuple(out), grid=(t // tm,),
                 in_specs=in_specs, out_specs=tuple(out_specs))(*ins)


def _modulate_bwd(dh, dxr, x, sc, seq):
    t, d = x.shape
    tm = min(TM, seq)
    tpb = seq // tm
    nb = t // seq

    def body(dh_ref, dxr_ref, x_ref, sc_ref, dx_ref, dsc_ref, dsh_ref):
        dhv = dh_ref[...]
        dx_ref[...] = dxr_ref[...] + dhv * (1.0 + sc_ref[...])

        @pl.when(pl.program_id(0) % tpb == 0)
        def _():
            dsc_ref[...] = jnp.zeros_like(dsc_ref)
            dsh_ref[...] = jnp.zeros_like(dsh_ref)

        dsc_ref[...] += _colsum(dhv * x_ref[...])
        dsh_ref[...] += _colsum(dhv)

    return _call(body, "modulate_bwd", (_sds((t, d), F32), _sds((nb, 1, d), F32), _sds((nb, 1, d), F32)), grid=(t // tm,),
                 in_specs=[_tile_spec(tm, d), _tile_spec(tm, d), _tile_spec(tm, d), _row_spec(d, tpb)],
                 out_specs=(_tile_spec(tm, d), _row_spec(d, tpb), _row_spec(d, tpb)))(dh, dxr, x, sc)


def _ffn_in(h, w_in, layer, beside=None):
    t, d = h.shape
    n = w_in.shape[3]
    half = N_DEV // 2
    tm = min(TMM, t)

    def body(h_ref, wg_ref, wu_ref, g_ref, u_ref, a_ref):
        hv = h_ref[...]
        g = _dot(hv, wg_ref[...])
        u = _dot(hv, wu_ref[...])
        g_ref[...] = g.astype(FFN_KEEP)
        u_ref[...] = u.astype(FFN_KEEP)
        a_ref[...] = (_silu(g) * u).astype(BF16)

    blk = pl.BlockSpec((None, tm, n), lambda p, i: (p, i, 0))
    return _run(beside, body, "ffn_in", (_sds((half, t, n), FFN_KEEP), _sds((half, t, n), FFN_KEEP), _sds((half, t, n), BF16)),
                (h, w_in, w_in), (half, t // tm),
                [pl.BlockSpec((tm, d), lambda p, i: (i, 0)),
                 pl.BlockSpec((None, None, d, n), lambda p, i: (layer, p, 0, 0)),
                 pl.BlockSpec((None, None, d, n), lambda p, i: (layer, p + half, 0, 0))],
                (blk, blk, blk), vmem=VMEM_BIG)


def _ffn_out(act, w_out, layer):
    half, t, n = act.shape
    d = w_out.shape[2]
    tm = min(TM, t)

    def body(a_ref, w_ref, o_ref):
        acc = _dot(a_ref[0], w_ref[0])
        for p in range(1, half):
            acc = acc + _dot(a_ref[p], w_ref[p])
        o_ref[...] = acc

    return _call(body, "ffn_out", _sds((t, d), F32), grid=(t // tm,),
                 in_specs=[pl.BlockSpec((half, tm, n), lambda i: (0, i, 0)),
                           pl.BlockSpec((half, n, d), lambda i: (layer // half, 0, 0))],
                 out_specs=pl.BlockSpec((tm, d), lambda i: (i, 0)), vmem=VMEM_BIG)(act, w_out)


def _ffn_dact(dy, w_out, hg, hu, layer, beside=None):
    half, t, n = hg.shape
    d = dy.shape[1]
    tm = min(TMM, t)

    def body(dy_ref, w_ref, g_ref, u_ref, dg_ref, du_ref):
        da = _dot(dy_ref[...], w_ref[...], NT)
        g = g_ref[...].astype(F32)
        sg = jax.nn.sigmoid(g)
        dg_ref[...] = (da * u_ref[...].astype(F32) * (sg * (1.0 + g * (1.0 - sg)))).astype(BF16)
        du_ref[...] = (da * (g * sg)).astype(BF16)

    blk = pl.BlockSpec((None, tm, n), lambda p, i: (p, i, 0))
    return _run(beside, body, "ffn_dact", (_sds((half, t, n), BF16), _sds((half, t, n), BF16)), (dy, w_out, hg, hu), (half, t // tm),
                [pl.BlockSpec((tm, d), lambda p, i: (i, 0)), pl.BlockSpec((None, n, d), lambda p, i: (layer + p, 0, 0)), blk, blk],
                (blk, blk), vmem=VMEM_BIG)


def _ffn_bwd_weights(h, dy, act, dg, du):
    half, t, n = act.shape
    d = h.shape[1]
    tk = min(TMM, t)
    h_spec = pl.BlockSpec((tk, d), lambda r: (r, 0))
    g_spec = pl.BlockSpec((half, tk, n), lambda r: (0, r, 0))

    def in_terms(h_ref, g_ref):
        hv = h_ref[...]
        return [_dot(hv, g_ref[p], TN) for p in range(half)]

    def out_terms(a_ref, dy_ref):
        dyv = dy_ref[...]
        return [_dot(a_ref[p], dyv, TN) for p in range(half)]

    both = _sds((2 * half, d, n), GRAD_WIRE)
    dwi = _accumulate_over_tokens("ffn_dw_gate", (h, dg), [h_spec, g_spec], both, pl.BlockSpec((half, d, n), lambda r: (0, 0, 0)),
                                  (half, d, n), t // tk, in_terms)
    dwi = _accumulate_over_tokens("ffn_dw_up", (h, du), [h_spec, g_spec], both, pl.BlockSpec((half, d, n), lambda r: (1, 0, 0)),
                                  (half, d, n), t // tk, in_terms, fill=dwi)
    dwo = _accumulate_over_tokens("ffn_dw_out", (act, dy), [g_spec, h_spec], _sds((half, n, d), GRAD_WIRE),
                                  pl.BlockSpec((half, n, d), lambda r: (0, 0, 0)), (half, n, d), t // tk, out_terms)
    return dwi, dwo


def _ffn_dh(dg, du, w_in, layer, beside=None):
    half, t, n = dg.shape
    d = w_in.shape[2]
    tm = min(TM, t)

    def body(dg_ref, du_ref, w_ref, o_ref):
        acc = None
        for p in range(half):
            for ref, q in ((dg_ref, p), (du_ref, p + half)):
                term = _dot(ref[p], w_ref[q], NT)
                acc = term if acc is None else acc + term
        o_ref[...] = acc

    g_spec = pl.BlockSpec((half, tm, n), lambda i: (0, i, 0))
    return _run(beside, body, "ffn_dh", _sds((t, d), F32), (dg, du, w_in), (t // tm,),
                [g_spec, g_spec, pl.BlockSpec((None, 2 * half, d, n), lambda i: (layer, 0, 0, 0))],
                pl.BlockSpec((tm, d), lambda i: (i, 0)), vmem=VMEM_BIG)


def _tril(n, strict=False):
    r = lax.broadcasted_iota(jnp.int32, (n, n), 0)
    c = lax.broadcasted_iota(jnp.int32, (n, n), 1)
    return c < r if strict else c <= r


def _gm_spatial_fwd(pre, lng, lnb, w_s, b_st, seq, beside=None):
    t, w2 = pre.shape
    w = w2 // 2
    gd = w // GM_GROUPS
    tm = min(TM, seq)
    nch = tm // GM_CHUNK

    def body(pre_ref, lng_ref, lnb_ref, ws_ref, bs_ref, y_ref):
        v = _gelu(pre_ref[:, w:])
        vc, rstd = _ln_stats(v)
        vn = (vc * rstd * lng_ref[...] + lnb_ref[...]).astype(BF16)
        keep = _tril(GM_CHUNK)
        for g in range(GM_GROUPS):
            wm = jnp.where(keep, ws_ref[g], 0.0).astype(BF16)
            for ci in range(nch):
                rows = slice(ci * GM_CHUNK, (ci + 1) * GM_CHUNK)
                cols = slice(g * gd, (g + 1) * gd)
                sv = _dot(wm, vn[rows, cols]) + bs_ref[:, g:g + 1]
                u = _gelu(pre_ref[rows, cols])
                y_ref[rows, cols] = (u * sv).astype(BF16)

    return _run(beside, body, "gm_spatial_fwd", _sds((t, w), BF16), (pre, lng, lnb, w_s, b_st), (t // tm,),
                [_tile_spec(tm, w2), _vec_spec(w), _vec_spec(w),
                 pl.BlockSpec((GM_GROUPS, GM_CHUNK, GM_CHUNK), lambda i: (0, 0, 0)),
                 pl.BlockSpec((GM_CHUNK, GM_GROUPS), lambda i: (0, 0))],
                _tile_spec(tm, w), vmem=VMEM_BIG)


def _gm_spatial_bwd(pre, dyv, lng, lnb, w_s, b_st, seq, beside=None):
    t, w2 = pre.shape
    w = w2 // 2
    gd = w // GM_GROUPS
    tm = min(TM, seq)
    nch = tm // GM_CHUNK

    def body(pre_ref, dyv_ref, lng_ref, lnb_ref, ws_ref, bs_ref, dpre_ref, dws_ref, dbs_ref, dlg_ref, dlb_ref, dbin_ref, dvn_ref):
        @pl.when(pl.program_id(0) == 0)
        def _():
            dws_ref[...] = jnp.zeros_like(dws_ref)
            dbs_ref[...] = jnp.zeros_like(dbs_ref)
            dlg_ref[...] = jnp.zeros_like(dlg_ref)
            dlb_ref[...] = jnp.zeros_like(dlb_ref)
            dbin_ref[...] = jnp.zeros_like(dbin_ref)

        pv = pre_ref[:, w:]
        v = _gelu(pv)
        vc, rstd = _ln_stats(v)
        vhat = vc * rstd
        vn = (vhat * lng_ref[...] + lnb_ref[...]).astype(BF16)
        keep = _tril(GM_CHUNK)
        dbs_cols = []
        for g in range(GM_GROUPS):
            wm = jnp.where(keep, ws_ref[g], 0.0).astype(BF16)
            dwm = jnp.zeros((GM_CHUNK, GM_CHUNK), F32)
            dbs = jnp.zeros((GM_CHUNK, 1), F32)
            for ci in range(nch):
                rows = slice(ci * GM_CHUNK, (ci + 1) * GM_CHUNK)
                cols = slice(g * gd, (g + 1) * gd)
                vn_b = vn[rows, cols]
                sv = _dot(wm, vn_b) + bs_ref[:, g:g + 1]
                pu = pre_ref[rows, cols]
                dy = dyv_ref[rows, cols]
                du = dy * sv
                dsv = dy * _gelu(pu)
                dpu = du * _gelu_grad(pu)
                dpre_ref[rows, cols] = dpu.astype(BF16)
                dbin_ref[:, cols] += _colsum(dpu)
                dsv_b = dsv.astype(BF16)
                dwm = dwm + _dot(dsv_b, vn_b, NT)
                dbs = dbs + jnp.sum(dsv, axis=-1, keepdims=True)
                dvn_ref[rows, cols] = _dot(wm, dsv_b, TN)
            dws_ref[g] += jnp.where(keep, dwm, 0.0)
            dbs_cols.append(dbs)
        dbs_ref[...] += jnp.concatenate(dbs_cols, axis=1)
        dvn = dvn_ref[...]
        dlg_ref[...] += _colsum(dvn * vhat)
        dlb_ref[...] += _colsum(dvn)
        dvh = dvn * lng_ref[...]
        m1 = jnp.mean(dvh, axis=-1, keepdims=True)
        m2 = jnp.mean(dvh * vhat, axis=-1, keepdims=True)
        dv = rstd * (dvh - m1 - vhat * m2)
        dpv = dv * _gelu_grad(pv)
        dpre_ref[:, w:] = dpv.astype(BF16)
        dbin_ref[:, w:] += _colsum(dpv)

    full3 = pl.BlockSpec((GM_GROUPS, GM_CHUNK, GM_CHUNK), lambda i: (0, 0, 0))
    bst = pl.BlockSpec((GM_CHUNK, GM_GROUPS), lambda i: (0, 0))
    return _run(beside, body, "gm_spatial_bwd",
                (_sds((t, w2), BF16), _sds((GM_GROUPS, GM_CHUNK, GM_CHUNK), F32), _sds((GM_CHUNK, GM_GROUPS), F32),
                 _sds((1, w), F32), _sds((1, w), F32), _sds((1, w2), F32)),
                (pre, dyv, lng, lnb, w_s, b_st), (t // tm,),
                [_tile_spec(tm, w2), _tile_spec(tm, w), _vec_spec(w), _vec_spec(w), full3, bst],
                (_tile_spec(tm, w2), full3, bst, _vec_spec(w), _vec_spec(w), _vec_spec(w2)),
                scratch=[pltpu.VMEM((tm, w), F32)], vmem=VMEM_BIG)


def _head_masks():
    lane = lax.broadcasted_iota(jnp.int32, (1, LANES), 1)
    return lane < HEAD_DIM


def _two_heads(x, m0):
    z = jnp.zeros_like(x)
    return jnp.where(m0, x, z), jnp.where(m0, z, x)


def _transposed(pair):
    return tuple(x.astype(F32).T.astype(BF16) for x in pair)


def _store_transposed(dqkv_ref, dk_acc, dv_acc, nq, tq):
    for c in range(nq):
        cols = slice(c * tq, (c + 1) * tq)
        dqkv_ref[1, cols, :] = dk_acc[:, cols].T.astype(BF16)
        dqkv_ref[2, cols, :] = dv_acc[:, cols].T.astype(BF16)


def _qkv_specs(seq, nq, blocked_q):
    if blocked_q:
        q = pl.BlockSpec((None, TQ_(seq), LANES), lambda b, p, i: (0, b * nq + i, p))
        k = pl.BlockSpec((None, seq, LANES), lambda b, p, i: (1, b, p))
        v = pl.BlockSpec((None, seq, LANES), lambda b, p, i: (2, b, p))
    else:
        q = pl.BlockSpec((None, seq, LANES), lambda b, p: (0, b, p))
        k = pl.BlockSpec((None, seq, LANES), lambda b, p: (1, b, p))
        v = pl.BlockSpec((None, seq, LANES), lambda b, p: (2, b, p))
    return q, k, v


def TQ_(seq):
    return min(TQ, seq)


def _fox_gate_fwd(ft, b_f, seq):
    nh, t = ft.shape
    nch = seq // LANES

    def body(ft_ref, bf_ref, fr_ref):
        r = lax.broadcasted_iota(jnp.int32, (LANES, LANES), 0)
        c = lax.broadcasted_iota(jnp.int32, (LANES, LANES), 1)
        upper = jnp.where(r <= c, 1.0, 0.0).astype(BF16)
        carry = jnp.zeros((nh, 1), F32)
        for ci in range(nch):
            cols = slice(ci * LANES, (ci + 1) * LANES)
            lf = _log_sigmoid(ft_ref[:, cols] + bf_ref[...])
            cs = _dot_exact(lf, upper) + carry
            fr_ref[:, cols] = cs
            carry = cs[:, LANES - 1:LANES]

    return _call(body, "fox_gate_fwd", _sds((nh, t), F32), grid=(t // seq,),
                 in_specs=[pl.BlockSpec((nh, seq), lambda b: (0, b)), pl.BlockSpec((nh, 1), lambda b: (0, 0))],
                 out_specs=pl.BlockSpec((nh, seq), lambda b: (0, b)))(ft, b_f)


def _fox_gate_bwd(ft, b_f, dfk, dfq, seq):
    nh, t = ft.shape
    nch = seq // LANES

    def body(ft_ref, bf_ref, dfk_ref, dfq_ref, dl_ref, db_ref):
        @pl.when(pl.program_id(0) == 0)
        def _():
            db_ref[...] = jnp.zeros_like(db_ref)

        r = lax.broadcasted_iota(jnp.int32, (LANES, LANES), 0)
        c = lax.broadcasted_iota(jnp.int32, (LANES, LANES), 1)
        lower = jnp.where(r >= c, 1.0, 0.0).astype(BF16)
        carry = jnp.zeros((nh, 1), F32)
        tot = jnp.zeros((nh, 1), F32)
        for ci in reversed(range(nch)):
            cols = slice(ci * LANES, (ci + 1) * LANES)
            rc = _dot_exact(dfk_ref[:, cols] + dfq_ref[:, cols], lower) + carry
            carry = rc[:, 0:1]
            dl = rc * jax.nn.sigmoid(-(ft_ref[:, cols] + bf_ref[...]))
            dl_ref[:, cols] = dl
            tot = tot + jnp.sum(dl, axis=-1, keepdims=True)
        db_ref[...] += tot

    blk = pl.BlockSpec((nh, seq), lambda b: (0, b))
    one = pl.BlockSpec((nh, 1), lambda b: (0, 0))
    return _call(body, "fox_gate_bwd", (_sds((nh, t), F32), _sds((nh, 1), F32)), grid=(t // seq,),
                 in_specs=[blk, one, blk, blk], out_specs=(blk, one))(ft, b_f, dfk, dfq)


def _sweep(step, n_off, unroll, init, start=0):
    def group(_, st):
        base, carry = st[0], st[1:]
        for u in range(unroll):
            carry = step(base + u, carry, False)
        return (base + unroll, *carry)

    def tail(r):
        def run(st):
            base, carry = st[0], st[1:]
            for u in range(r):
                carry = step(base + u, carry, False)
            return step(base + r, carry, True)
        return run

    def pick(idx, fns, st):
        if len(fns) == 1:
            return fns[0](st)
        half = len(fns) // 2
        return lax.cond(idx < half, lambda s: pick(idx, fns[:half], s), lambda s: pick(idx - half, fns[half:], s), st)

    st = lax.fori_loop(0, n_off // unroll, group, (jnp.int32(0) + start, *init))
    return pick(n_off % unroll, [tail(r) for r in range(unroll)], st)


def _carried(n_comm, n_in, n_out, refs):
    own_in, send = refs[:n_in], refs[n_in:n_in + n_comm]
    rest = refs[n_in + n_comm:]
    n_sem = 3 if n_comm else 0
    return own_in, send, rest[:n_out], rest[n_out:n_out + n_comm], rest[n_out + n_comm:n_out + n_comm + n_sem], rest[n_out + n_comm + n_sem:]


def _fox_fwd(qkv, frow, nb, seq, gather=()):
    _, t, d = qkv.shape
    npair = d // LANES
    tq = TQ_(seq)
    nq = seq // tq
    scale = HEAD_DIM ** -0.5
    n_comm = len(gather)
    total = nb * npair * nq

    def body(*refs):
        (q_ref, k_ref, v_ref, fr_ref), send, (o_ref, lse_ref), land, sems, _ = _carried(n_comm, 4, 2, refs)
        at = (pl.program_id(0) * npair + pl.program_id(1)) * nq + pl.program_id(2)
        if n_comm:
            for stage, when in ((0, 0), (1, (3 * total) // 4)):
                @pl.when(at == when)
                def _(stage=stage):
                    _gather_stage(stage, send, land, *sems)

        m0 = _head_masks()
        qm = _two_heads(q_ref[...] * scale, m0)
        row = lax.broadcasted_iota(jnp.int32, (tq, tq), 0)
        col = lax.broadcasted_iota(jnp.int32, (tq, tq), 1)
        one = jnp.ones((tq, LANES), BF16)

        def step(j, carry, diag):
            off = pl.multiple_of(j * tq, tq)
            kb = k_ref[pl.ds(off, tq), :]
            vb = v_ref[pl.ds(off, tq), :]
            vv = (jnp.where(m0, vb, one), jnp.where(m0, one, vb))
            out = []
            for hh in range(2):
                m, acc = carry[2 * hh], carry[2 * hh + 1]
                s = lax.dot_general(qm[hh], kb, NT, preferred_element_type=F32) - fr_ref[hh:hh + 1, pl.ds(off, tq)]
                if diag:
                    s = jnp.where(col <= row, s, NEG_INF)
                mn = jnp.maximum(m, jnp.max(s, axis=-1, keepdims=True))
                p = jnp.exp(s - mn)
                out += [mn, jnp.exp(m - mn) * acc + jnp.dot(p.astype(BF16), vv[hh], preferred_element_type=F32)]
            return tuple(out)

        neg = jnp.full((tq, 1), NEG_INF, F32)
        zacc = jnp.zeros((tq, LANES), F32)
        m_a, acc_a, m_b, acc_b = _sweep(step, pl.program_id(2), FOX_FWD_UNROLL, (neg, zacc, neg, zacc))
        l_a = pltpu.roll(acc_a, HEAD_DIM, 1)
        l_b = pltpu.roll(acc_b, HEAD_DIM, 1)
        o_ref[...] = jnp.where(m0, acc_a / l_a, acc_b / l_b)
        lse_ref[:, 0:1] = m_a + jnp.log(l_a[:, 0:1])
        lse_ref[:, 1:2] = m_b + jnp.log(l_b[:, HEAD_DIM:HEAD_DIM + 1])
        if n_comm:
            @pl.when(at == total - 1)
            def _():
                _gather_stage(2, send, land, *sems)

    q_spec, k_spec, v_spec = _qkv_specs(seq, nq, True)
    col_spec = pl.BlockSpec((None, tq, 2), lambda b, p, i: (p, b * nq + i, 0))
    hbm = pl.BlockSpec(memory_space=pl.ANY)
    return _call(body, "fox_fwd", (_sds((t, d), F32), _sds((npair, t, 2), F32), *_gathered_shapes(gather)), grid=(nb, npair, nq),
                 in_specs=[q_spec, k_spec, v_spec, pl.BlockSpec((None, 2, seq), lambda b, p, i: (p, 0, b))] + [hbm] * n_comm,
                 out_specs=(pl.BlockSpec((tq, LANES), lambda b, p, i: (b * nq + i, p)), col_spec, *[hbm] * n_comm),
                 scratch=_comm_scratch(n_comm) if n_comm else (), vmem=VMEM_BIG)(qkv, qkv, qkv, frow, *gather)


def _scatter_beside(stage, n_comm, nb, npair, send, land, sems):
    if n_comm:
        at = pl.program_id(0) * npair + pl.program_id(1)

        @pl.when(at == (0 if stage == 0 else nb * npair - 1))
        def _():
            _scatter_stage(stage, send, land, *sems)


def _fox_bwd(qkv, frow, o, do, lse, nb, seq, scatter=()):
    _, t, d = qkv.shape
    npair = d // LANES
    tq = TQ_(seq)
    nq = seq // tq
    scale = HEAD_DIM ** -0.5
    n_comm = len(scatter)

    def body(*refs):
        own_in, send, (dqkv_ref, df_ref, dfq_ref), land, sems, (dk_acc, dv_acc) = _carried(n_comm, 7, 3, refs)
        q_ref, k_ref, v_ref, fr_ref, o_ref, do_ref, lse_ref = own_in
        _scatter_beside(0, n_comm, nb, npair, send, land, sems)
        m0 = _head_masks()
        row = lax.broadcasted_iota(jnp.int32, (tq, tq), 0)
        col = lax.broadcasted_iota(jnp.int32, (tq, tq), 1)
        dk_acc[...] = jnp.zeros_like(dk_acc)
        dv_acc[...] = jnp.zeros_like(dv_acc)
        df_ref[...] = jnp.zeros_like(df_ref)

        def q_block(i, _):
            qoff = pl.multiple_of(i * tq, tq)
            qrows = pl.ds(qoff, tq)
            qm = _two_heads(q_ref[qrows, :] * scale, m0)
            dov = do_ref[qrows, :]
            dd = dov * o_ref[qrows, :]
            dm = _two_heads(dov.astype(BF16), m0)
            qmt = _transposed(qm)
            dmt = _transposed(dm)
            delta = (jnp.sum(jnp.where(m0, dd, 0.0), axis=-1, keepdims=True),
                     jnp.sum(jnp.where(m0, 0.0, dd), axis=-1, keepdims=True))
            ls = (lse_ref[qrows, 0:1], lse_ref[qrows, 1:2])

            def step(j, carry, diag):
                off = pl.multiple_of(j * tq, tq)
                krows = pl.ds(off, tq)
                kb = k_ref[krows, :]
                vb = v_ref[krows, :]
                dqs, rowsums = [], []
                dk = jnp.zeros((LANES, tq), F32)
                dv = jnp.zeros((LANES, tq), F32)
                for hh in range(2):
                    s = lax.dot_general(qm[hh], kb, NT, preferred_element_type=F32) - fr_ref[hh:hh + 1, krows]
                    if diag:
                        s = jnp.where(col <= row, s, NEG_INF)
                    p = jnp.exp(s - ls[hh])
                    dp = lax.dot_general(dm[hh], vb, NT, preferred_element_type=F32)
                    ds = p * (dp - delta[hh])
                    df_ref[hh:hh + 1, krows] -= _colsum(ds)
                    rowsums.append(carry[1 + hh] + jnp.sum(ds, axis=-1, keepdims=True))
                    ds_b = ds.astype(BF16)
                    dqs.append(jnp.dot(ds_b, kb, preferred_element_type=F32))
                    dk = dk + jnp.dot(qmt[hh], ds_b, preferred_element_type=F32)
                    dv = dv + jnp.dot(dmt[hh], p.astype(BF16), preferred_element_type=F32)
                dk_acc[:, krows] += dk
                dv_acc[:, krows] += dv
                return (carry[0] + jnp.where(m0, dqs[0], dqs[1]), *rowsums)

            zero = jnp.zeros((tq, 1), F32)
            dq, rs_a, rs_b = _sweep(step, i, FOX_BWD_UNROLL, (jnp.zeros((tq, LANES), F32), zero, zero))
            dqkv_ref[0, qrows, :] = (dq * scale).astype(BF16)
            dfq_ref[qrows, 0:1] = rs_a
            dfq_ref[qrows, 1:2] = rs_b
            return 0

        lax.fori_loop(0, nq, q_block, 0)
        _store_transposed(dqkv_ref, dk_acc, dv_acc, nq, tq)
        _scatter_beside(1, n_comm, nb, npair, send, land, sems)

    q_spec, k_spec, v_spec = _qkv_specs(seq, nq, False)
    col_spec = pl.BlockSpec((None, seq, 2), lambda b, p: (p, b, 0))
    row_spec = pl.BlockSpec((None, 2, seq), lambda b, p: (p, 0, b))
    tile = pl.BlockSpec((seq, LANES), lambda b, p: (b, p))
    hbm = pl.BlockSpec(memory_space=pl.ANY)
    return _call(body, "fox_bwd", (_sds((3, t, d), BF16), _sds((npair, 2, t), F32), _sds((npair, t, 2), F32), *_scattered_shapes(scatter)),
                 grid=(nb, npair),
                 in_specs=[q_spec, k_spec, v_spec, row_spec, tile, tile, col_spec] + [hbm] * n_comm,
                 out_specs=(pl.BlockSpec((3, seq, LANES), lambda b, p: (0, b, p)), row_spec, col_spec, *[hbm] * n_comm),
                 scratch=(_comm_scratch(n_comm) if n_comm else []) + [pltpu.VMEM((LANES, seq), F32), pltpu.VMEM((LANES, seq), F32)],
                 vmem=VMEM_BIG)(qkv, qkv, qkv, frow, o, do, lse, *scatter)


def _split2(x):
    hi = x.astype(BF16)
    return hi, (x - hi.astype(F32)).astype(BF16)


def _sum_right(x, tri):
    hi, lo = _split2(x)
    return jnp.dot(hi, tri, preferred_element_type=F32) + jnp.dot(lo, tri, preferred_element_type=F32)


def _sb_scores(qm_h, kb, mask):
    z = lax.dot_general(qm_h, kb, NT, preferred_element_type=F32)
    lb = _log_sigmoid(z)
    l1m = lb - z
    if mask is not None:
        l1m = jnp.where(mask, l1m, 0.0)
    return lb, l1m


def _sb_fwd(qkv, nb, seq):
    _, t, d = qkv.shape
    npair = d // LANES
    tq = TQ_(seq)
    nq = seq // tq
    scale = HEAD_DIM ** -0.5

    def body(q_ref, k_ref, v_ref, o_ref, lt_ref):
        i = pl.program_id(2)
        m0 = _head_masks()
        qm = _two_heads(q_ref[...] * scale, m0)
        row = lax.broadcasted_iota(jnp.int32, (tq, tq), 0)
        col = lax.broadcasted_iota(jnp.int32, (tq, tq), 1)
        after = jnp.where(row > col, 1.0, 0.0).astype(BF16)

        def step(j, carry, diag):
            off = pl.multiple_of(j * tq, tq)
            kb = k_ref[pl.ds(off, tq), :]
            vb = v_ref[pl.ds(off, tq), :]
            mask = (col < row) if diag else None
            nxt, parts = [], []
            for hh in range(2):
                lb, l1m = _sb_scores(qm[hh], kb, mask)
                rest = _sum_right(l1m, after) + carry[hh]
                a = jnp.exp(lb + rest)
                if diag:
                    a = jnp.where(mask, a, 0.0)
                parts.append(jnp.dot(a.astype(BF16), vb, preferred_element_type=F32))
                nxt.append(carry[hh] + jnp.sum(l1m, axis=-1, keepdims=True))
            return (*nxt, carry[2] + jnp.where(m0, parts[0], parts[1]))

        zero = jnp.zeros((tq, 1), F32)
        init = (zero, zero, jnp.zeros((tq, LANES), F32))
        carry = lax.cond(i > 0, lambda c: step(i - 1, step(i, c, True), False), lambda c: step(i, c, True), init)

        def alive(st):
            return (st[0] < i) & (jnp.max(jnp.maximum(st[1], st[2])) > SB_DEAD)

        def more(st):
            return (st[0] + 1, *step(i - 1 - st[0], st[1:], False))

        done, lt_a, lt_b, acc = lax.while_loop(alive, more, (jnp.minimum(i, 1), *carry))
        o_ref[...] = acc
        lt_ref[:, 0:1] = lt_a
        lt_ref[:, 1:2] = lt_b
        lt_ref[:, 2:3] = jnp.zeros((tq, 1), F32) + done.astype(F32)
        lt_ref[:, 3:4] = zero

    q_spec, k_spec, v_spec = _qkv_specs(seq, nq, True)
    return _call(body, "sb_fwd", (_sds((t, d), F32), _sds((npair, t, 4), F32)), grid=(nb, npair, nq),
                 in_specs=[q_spec, k_spec, v_spec],
                 out_specs=(pl.BlockSpec((tq, LANES), lambda b, p, i: (b * nq + i, p)),
                            pl.BlockSpec((None, tq, 4), lambda b, p, i: (p, b * nq + i, 0))), vmem=VMEM_BIG)(qkv, qkv, qkv)


def _sb_bwd(qkv, do, ltot, nb, seq, scatter=()):
    _, t, d = qkv.shape
    npair = d // LANES
    tq = TQ_(seq)
    nq = seq // tq
    scale = HEAD_DIM ** -0.5
    n_comm = len(scatter)

    def body(*refs):
        (q_ref, k_ref, v_ref, do_ref, lt_ref), send, (dqkv_ref,), land, sems, (dk_acc, dv_acc) = _carried(n_comm, 5, 1, refs)
        _scatter_beside(0, n_comm, nb, npair, send, land, sems)
        m0 = _head_masks()
        row = lax.broadcasted_iota(jnp.int32, (tq, tq), 0)
        col = lax.broadcasted_iota(jnp.int32, (tq, tq), 1)
        upto = jnp.where(row <= col, 1.0, 0.0).astype(BF16)
        left_of = jnp.where(row < col, 1.0, 0.0).astype(BF16)
        dk_acc[...] = jnp.zeros_like(dk_acc)
        dv_acc[...] = jnp.zeros_like(dv_acc)

        def q_block(i, _):
            qoff = pl.multiple_of(i * tq, tq)
            qrows = pl.ds(qoff, tq)
            qm = _two_heads(q_ref[qrows, :] * scale, m0)
            dm = _two_heads(do_ref[qrows, :].astype(BF16), m0)
            qmt = _transposed(qm)
            dmt = _transposed(dm)
            ltot = (lt_ref[qrows, 0:1], lt_ref[qrows, 1:2])

            def step(j, carry, diag):
                off = pl.multiple_of(j * tq, tq)
                krows = pl.ds(off, tq)
                kb = k_ref[krows, :]
                vb = v_ref[krows, :]
                mask = (col < row) if diag else None
                nxt, dqs = [], []
                dk = jnp.zeros((LANES, tq), F32)
                dv = jnp.zeros((LANES, tq), F32)
                for hh in range(2):
                    cl, ce = carry[2 * hh], carry[2 * hh + 1]
                    lb, l1m = _sb_scores(qm[hh], kb, mask)
                    a = jnp.exp(lb + (ltot[hh] - (_sum_right(l1m, upto) + cl)))
                    if diag:
                        a = jnp.where(mask, a, 0.0)
                    e = lax.dot_general(dm[hh], vb, NT, preferred_element_type=F32) * a
                    before = _sum_right(e, left_of) + ce
                    beta = jnp.exp(lb)
                    dz = e * (1.0 - beta) - before * beta
                    if diag:
                        dz = jnp.where(mask, dz, 0.0)
                    dz_b = dz.astype(BF16)
                    dqs.append(jnp.dot(dz_b, kb, preferred_element_type=F32))
                    dk = dk + jnp.dot(qmt[hh], dz_b, preferred_element_type=F32)
                    dv = dv + jnp.dot(dmt[hh], a.astype(BF16), preferred_element_type=F32)
                    nxt += [cl + jnp.sum(l1m, axis=-1, keepdims=True), ce + jnp.sum(e, axis=-1, keepdims=True)]
                dk_acc[:, krows] += dk
                dv_acc[:, krows] += dv
                return (*nxt, carry[4] + jnp.where(m0, dqs[0], dqs[1]))

            zero = jnp.zeros((tq, 1), F32)
            visited = jnp.max(lt_ref[qrows, 2:3]).astype(jnp.int32)
            carry = _sweep(step, visited, SB_BWD_UNROLL, (zero, zero, zero, zero, jnp.zeros((tq, LANES), F32)), start=i - visited)
            dqkv_ref[0, qrows, :] = (carry[4] * scale).astype(BF16)
            return 0

        lax.fori_loop(0, nq, q_block, 0)
        _store_transposed(dqkv_ref, dk_acc, dv_acc, nq, tq)
        _scatter_beside(1, n_comm, nb, npair, send, land, sems)

    q_spec, k_spec, v_spec = _qkv_specs(seq, nq, False)
    tile = pl.BlockSpec((seq, LANES), lambda b, p: (b, p))
    hbm = pl.BlockSpec(memory_space=pl.ANY)
    return _call(body, "sb_bwd", (_sds((3, t, d), BF16), *_scattered_shapes(scatter)), grid=(nb, npair),
                 in_specs=[q_spec, k_spec, v_spec, tile, pl.BlockSpec((None, seq, 4), lambda b, p: (p, b, 0))] + [hbm] * n_comm,
                 out_specs=(pl.BlockSpec((3, seq, LANES), lambda b, p: (0, b, p)), *[hbm] * n_comm),
                 scratch=(_comm_scratch(n_comm) if n_comm else []) + [pltpu.VMEM((LANES, seq), F32), pltpu.VMEM((LANES, seq), F32)],
                 vmem=VMEM_BIG)(qkv, qkv, qkv, do, ltot, *scatter)


def _qkv_proj(name, h, w):
    t, d = h.shape
    tm = min(TMM, t)
    return _mm(name, h, w, _sds((3, t, d), BF16), (3, t // tm, 1),
               pl.BlockSpec((tm, d), lambda s, i, r: (i, 0)), pl.BlockSpec((d, d), lambda s, i, r: (0, s)),
               pl.BlockSpec((None, tm, d), lambda s, i, r: (s, i, 0)), NN)


def _qkv_dw(name, h, dqkv):
    t, d = h.shape
    tk = min(TMM, t)

    def terms(h_ref, g_ref):
        hv = h_ref[...]
        return [_dot(hv, g_ref[s], TN) for s in range(3)]

    def store(o_ref, acc_ref):
        for s in range(3):
            o_ref[:, s * d:(s + 1) * d] = acc_ref[s].astype(o_ref.dtype)

    return _accumulate_over_tokens(name, (h, dqkv), [pl.BlockSpec((tk, d), lambda r: (r, 0)), pl.BlockSpec((3, tk, d), lambda r: (0, r, 0))],
                                   _sds((d, 3 * d), GRAD_WIRE), pl.BlockSpec((d, 3 * d), lambda r: (0, 0)), (3, d, d), t // tk, terms, store)


def _qkv_dh(name, dqkv, w):
    _, t, d = dqkv.shape
    tm = min(TM, t)

    def body(g_ref, w_ref, o_ref):
        acc = None
        for s in range(3):
            term = _dot(g_ref[s], w_ref[:, s * d:(s + 1) * d], NT)
            acc = term if acc is None else acc + term
        o_ref[...] = acc

    return _call(body, name, _sds((t, d), F32), grid=(t // tm,),
                 in_specs=[pl.BlockSpec((3, tm, d), lambda i: (0, i, 0)), pl.BlockSpec((d, 3 * d), lambda i: (0, 0))],
                 out_specs=pl.BlockSpec((tm, d), lambda i: (i, 0)), vmem=VMEM_BIG)(dqkv, w)


def _glu(pre_block, d):
    return pre_block[:, :d] * jax.nn.sigmoid(pre_block[:, d:])


def _shifted_copies(ext_ref, sh_ref, tt):
    for r in range(1, SUBLANES):
        sh_ref[r - 1] = ext_ref[pl.ds(r, tt + CONV_HALO - SUBLANES), :]


def _rows_from(ext_ref, sh_ref, base, offset, n):
    q, r = divmod(offset, SUBLANES)
    if r == 0:
        return ext_ref[pl.ds(pl.multiple_of(base + offset, SUBLANES), n), :]
    return sh_ref[r - 1, pl.ds(pl.multiple_of(base + q * SUBLANES, SUBLANES), n), :]


def _cv_conv_fwd(pre, dw, dwb, lng, lnb, seq):
    t, d2 = pre.shape
    d = d2 // 2
    tt = min(TT, seq)
    nt = seq // tt
    hb = tt // CONV_HALO

    def body(pre_ref, halo_ref, dw_ref, dwb_ref, lng_ref, lnb_ref, y1_ref, y2_ref, ext_ref, sh_ref):
        i = pl.program_id(1)
        ext_ref[0:CONV_HALO, :] = jnp.where(i == 0, 0.0, _glu(halo_ref[...], d))
        ext_ref[CONV_HALO:, :] = _glu(pre_ref[...], d)
        _shifted_copies(ext_ref, sh_ref, tt)

        acc = jnp.zeros((tt, d), F32) + dwb_ref[...]
        for k in range(CONV_WIDTH):
            acc = acc + _rows_from(ext_ref, sh_ref, 0, CONV_HALO - (CONV_WIDTH - 1) + k, tt) * dw_ref[k:k + 1, :]
        y1_ref[...] = acc
        yc, rstd = _ln_stats(acc)
        y2_ref[...] = _silu(yc * rstd * lng_ref[...] + lnb_ref[...]).astype(BF16)

    vec = pl.BlockSpec((1, d), lambda b, i: (0, 0))
    tile = pl.BlockSpec((tt, d), lambda b, i: (b * nt + i, 0))
    return _call(body, "cv_conv_fwd", (_sds((t, d), F32), _sds((t, d), BF16)), grid=(t // seq, nt),
                 in_specs=[pl.BlockSpec((tt, d2), lambda b, i: (b * nt + i, 0)),
                           pl.BlockSpec((CONV_HALO, d2), lambda b, i: (jnp.maximum((b * nt + i) * hb - 1, 0), 0)),
                           pl.BlockSpec((CONV_HALO, d), lambda b, i: (0, 0)), vec, vec, vec],
                 out_specs=(tile, tile),
                 scratch=[pltpu.VMEM((tt + CONV_HALO, d), F32), pltpu.VMEM((SUBLANES - 1, tt + CONV_HALO - SUBLANES, d), F32)],
                 vmem=VMEM_BIG)(pre, pre, dw, dwb, lng, lnb)


def _cv_norm_bwd(y1, dy2, lng, lnb):
    t, d = y1.shape
    tm = min(TM, t)

    def body(y1_ref, dy2_ref, lng_ref, lnb_ref, dy1_ref, dlg_ref, dlb_ref, dsum_ref):
        @pl.when(pl.program_id(0) == 0)
        def _():
            dlg_ref[...] = jnp.zeros_like(dlg_ref)
            dlb_ref[...] = jnp.zeros_like(dlb_ref)
            dsum_ref[...] = jnp.zeros_like(dsum_ref)

        yc, rstd = _ln_stats(y1_ref[...])
        yhat = yc * rstd
        n = yhat * lng_ref[...] + lnb_ref[...]
        sg = jax.nn.sigmoid(n)
        dn = dy2_ref[...] * (sg * (1.0 + n * (1.0 - sg)))
        dlg_ref[...] += _colsum(dn * yhat)
        dlb_ref[...] += _colsum(dn)
        dyh = dn * lng_ref[...]
        m1 = jnp.mean(dyh, axis=-1, keepdims=True)
        m2 = jnp.mean(dyh * yhat, axis=-1, keepdims=True)
        dy1 = rstd * (dyh - m1 - yhat * m2)
        dy1_ref[...] = dy1
        dsum_ref[...] += _colsum(dy1)

    return _call(body, "cv_norm_bwd", (_sds((t, d), F32), _sds((1, d), F32), _sds((1, d), F32), _sds((1, d), F32)),
                 grid=(t // tm,), in_specs=[_tile_spec(tm, d), _tile_spec(tm, d), _vec_spec(d), _vec_spec(d)],
                 out_specs=(_tile_spec(tm, d), _vec_spec(d), _vec_spec(d), _vec_spec(d)))(y1, dy2, lng, lnb)


def _cv_conv_bwd(pre, dy1, dw, seq):
    t, d2 = pre.shape
    d = d2 // 2
    tt = min(TT // 2, seq)
    nt = seq // tt
    hb = tt // CONV_HALO
    last_halo = t // CONV_HALO - 1

    def body(pre_ref, halo_ref, dy_ref, dyn_ref, dw_ref, dpre_ref, ddw_ref, dbin_ref, ext_ref, dext_ref, sh_ref, dsh_ref, taps_ref):
        b, i = pl.program_id(0), pl.program_id(1)

        @pl.when((b == 0) & (i == 0))
        def _():
            ddw_ref[...] = jnp.zeros_like(ddw_ref)
            dbin_ref[...] = jnp.zeros_like(dbin_ref)

        pv = pre_ref[...]
        ext_ref[0:CONV_HALO, :] = jnp.where(i == 0, 0.0, _glu(halo_ref[...], d))
        ext_ref[CONV_HALO:, :] = _glu(pv, d)
        dyv = dy_ref[...]
        dext_ref[0:tt, :] = dyv
        dext_ref[tt:, :] = jnp.where(i == nt - 1, 0.0, dyn_ref[...])
        _shifted_copies(ext_ref, sh_ref, tt)
        _shifted_copies(dext_ref, dsh_ref, tt)
        nrows = tt // CONV_ROWS

        def input_grad(c, _):
            r0 = pl.multiple_of(c * CONV_ROWS, CONV_ROWS)
            dy0 = jnp.zeros((CONV_ROWS, d), F32)
            for k in range(CONV_WIDTH):
                dy0 = dy0 + _rows_from(dext_ref, dsh_ref, r0, CONV_WIDTH - 1 - k, CONV_ROWS) * dw_ref[k:k + 1, :]
            rows = pl.ds(r0, CONV_ROWS)
            a = pre_ref[rows, :d]
            sg = jax.nn.sigmoid(pre_ref[rows, d:])
            da = dy0 * sg
            dg = dy0 * a * sg * (1.0 - sg)
            dpre_ref[rows, :d] = da.astype(BF16)
            dpre_ref[rows, d:] = dg.astype(BF16)
            dbin_ref[:, :d] += _colsum(da)
            dbin_ref[:, d:] += _colsum(dg)
            return 0

        lax.fori_loop(0, nrows, input_grad, 0)

        @pl.when((b == 0) & (i == 0))
        def _():
            taps_ref[...] = jnp.zeros_like(taps_ref)

        def tap_grads(c, _):
            r0 = pl.multiple_of(c * CONV_ROWS, CONV_ROWS)
            dyc = dy_ref[pl.ds(r0, CONV_ROWS), :]
            for k in range(CONV_WIDTH):
                prod = dyc * _rows_from(ext_ref, sh_ref, r0, CONV_HALO - (CONV_WIDTH - 1) + k, CONV_ROWS)
                taps_ref[k] += jnp.sum(prod.reshape(CONV_ROWS // SUBLANES, SUBLANES, d), axis=0)
            return 0

        lax.fori_loop(0, nrows, tap_grads, 0)

        @pl.when((b == t // seq - 1) & (i == nt - 1))
        def _():
            for k in range(CONV_WIDTH):
                ddw_ref[k:k + 1, :] = _colsum(taps_ref[k])

    return _call(body, "cv_conv_bwd", (_sds((t, d2), BF16), _sds((CONV_HALO, d), F32), _sds((1, d2), F32)), grid=(t // seq, nt),
                 in_specs=[pl.BlockSpec((tt, d2), lambda b, i: (b * nt + i, 0)),
                           pl.BlockSpec((CONV_HALO, d2), lambda b, i: (jnp.maximum((b * nt + i) * hb - 1, 0), 0)),
                           pl.BlockSpec((tt, d), lambda b, i: (b * nt + i, 0)),
                           pl.BlockSpec((CONV_HALO, d), lambda b, i: (jnp.minimum((b * nt + i + 1) * hb, last_halo), 0)),
                           pl.BlockSpec((CONV_HALO, d), lambda b, i: (0, 0))],
                 out_specs=(pl.BlockSpec((tt, d2), lambda b, i: (b * nt + i, 0)),
                            pl.BlockSpec((CONV_HALO, d), lambda b, i: (0, 0)), pl.BlockSpec((1, d2), lambda b, i: (0, 0))),
                 scratch=[pltpu.VMEM((tt + CONV_HALO, d), F32), pltpu.VMEM((tt + CONV_HALO, d), F32),
                          pltpu.VMEM((SUBLANES - 1, tt + CONV_HALO - SUBLANES, d), F32),
                          pltpu.VMEM((SUBLANES - 1, tt + CONV_HALO - SUBLANES, d), F32),
                          pltpu.VMEM((CONV_HALO, SUBLANES, d), F32)],
                 vmem=VMEM_BIG)(pre, pre, dy1, dy1, dw)


def _adamw(name, w, m, v, g=None, parts=None):
    rows, cols = w.shape
    nseg = 1 if parts is None else len(parts)
    seg_rows = rows // nseg
    tr = seg_rows
    for cand in ((512,) if parts is None else ()) + (256, 128, 64, 32, 16, 8):
        if seg_rows % cand == 0 and seg_rows > cand:
            tr = cand
            break
    tps = seg_rows // tr
    bc1 = 1.0 - ADAM_B1 ** ADAM_STEP
    bc2 = 1.0 - ADAM_B2 ** ADAM_STEP

    def body(w_ref, m_ref, v_ref, *rest):
        g_refs, (go_ref, d_ref, mo_ref, vo_ref) = rest[:nseg], rest[nseg:]

        def update(gv):
            mn = ADAM_B1 * m_ref[...] + (1.0 - ADAM_B1) * gv
            vn = ADAM_B2 * v_ref[...] + (1.0 - ADAM_B2) * (gv * gv)
            m_hat = mn / bc1
            v_hat = vn / bc2
            go_ref[...] = gv
            d_ref[...] = -ADAM_LR * (m_hat / (jnp.sqrt(v_hat) + ADAM_EPS) + ADAM_WD * w_ref[...])
            mo_ref[...] = mn
            vo_ref[...] = vn

        if parts is None:
            update(g_refs[0][...])
        else:
            for k, g_ref in enumerate(g_refs):
                @pl.when(pl.program_id(0) == k)
                def _(g_ref=g_ref):
                    gv = g_ref[0].astype(F32)
                    for s in range(1, N_DEV):
                        gv = gv + g_ref[s].astype(F32)
                    update(gv)

    blk = pl.BlockSpec((tr, cols), lambda l, i: (l * tps + i, 0))
    if parts is None:
        g_ins, g_specs = [g], [blk]
    else:
        g_ins = list(parts)
        g_specs = [pl.BlockSpec((N_DEV, tr, cols), (lambda l, i, k=k: (0, jnp.where(l == k, i, 0), 0))) for k in range(nseg)]
    out = _sds((rows, cols), F32)
    return _call(body, name, (out, out, out, out), grid=(nseg, tps), in_specs=[blk, blk, blk, *g_specs],
                 out_specs=(blk, blk, blk, blk), vmem=VMEM_BIG)(w, m, v, *g_ins)


def _pad_rows(a, rows):
    return jnp.pad(a, ((0, rows - a.shape[0]), (0, 0)))


def _full_cols(gathered, n):
    k = gathered.shape[2]
    return jnp.transpose(gathered[0], (1, 0, 2)).reshape(k, N_DEV * n)


def _col_blocks(full, n):
    k = full.shape[0]
    return jnp.transpose(full.reshape(k, N_DEV, n), (1, 0, 2))[None]


def kernel(x, c, mod_w, mod_b, ln1_g, ln1_b, ln2_g, ln2_b, ffn_w_in, ffn_w_out, gm_w_in, gm_b_in, gm_ln_g, gm_ln_b, gm_w_s, gm_b_s, gm_w_out, fox_w_in, fox_b_f, fox_w_out, sb_w_in, sb_w_out, cv_w_in, cv_b_in, cv_dw, cv_dw_b, cv_ln_g, cv_ln_b, cv_w_out, cv_b_out, loss_target, m_mod_w, m_mod_b, m_ln1_g, m_ln1_b, m_ln2_g, m_ln2_b, m_ffn_w_in, m_ffn_w_out, m_gm_w_in, m_gm_b_in, m_gm_ln_g, m_gm_ln_b, m_gm_w_s, m_gm_b_s, m_gm_w_out, m_fox_w_in, m_fox_b_f, m_fox_w_out, m_sb_w_in, m_sb_w_out, m_cv_w_in, m_cv_b_in, m_cv_dw, m_cv_dw_b, m_cv_ln_g, m_cv_ln_b, m_cv_w_out, m_cv_b_out, v_mod_w, v_mod_b, v_ln1_g, v_ln1_b, v_ln2_g, v_ln2_b, v_ffn_w_in, v_ffn_w_out, v_gm_w_in, v_gm_b_in, v_gm_ln_g, v_gm_ln_b, v_gm_w_s, v_gm_b_s, v_gm_w_out, v_fox_w_in, v_fox_b_f, v_fox_w_out, v_sb_w_in, v_sb_w_out, v_cv_w_in, v_cv_b_in, v_cv_dw, v_cv_dw_b, v_cv_ln_g, v_cv_ln_b, v_cv_w_out, v_cv_b_out):
    weights = dict(mod_w=mod_w, mod_b=mod_b, ln1_g=ln1_g, ln1_b=ln1_b, ln2_g=ln2_g, ln2_b=ln2_b, ffn_w_in=ffn_w_in, ffn_w_out=ffn_w_out, gm_w_in=gm_w_in, gm_b_in=gm_b_in, gm_ln_g=gm_ln_g, gm_ln_b=gm_ln_b, gm_w_s=gm_w_s, gm_b_s=gm_b_s, gm_w_out=gm_w_out, fox_w_in=fox_w_in, fox_b_f=fox_b_f, fox_w_out=fox_w_out, sb_w_in=sb_w_in, sb_w_out=sb_w_out, cv_w_in=cv_w_in, cv_b_in=cv_b_in, cv_dw=cv_dw, cv_dw_b=cv_dw_b, cv_ln_g=cv_ln_g, cv_ln_b=cv_ln_b, cv_w_out=cv_w_out, cv_b_out=cv_b_out)
    mom1 = dict(mod_w=m_mod_w, mod_b=m_mod_b, ln1_g=m_ln1_g, ln1_b=m_ln1_b, ln2_g=m_ln2_g, ln2_b=m_ln2_b, ffn_w_in=m_ffn_w_in, ffn_w_out=m_ffn_w_out, gm_w_in=m_gm_w_in, gm_b_in=m_gm_b_in, gm_ln_g=m_gm_ln_g, gm_ln_b=m_gm_ln_b, gm_w_s=m_gm_w_s, gm_b_s=m_gm_b_s, gm_w_out=m_gm_w_out, fox_w_in=m_fox_w_in, fox_b_f=m_fox_b_f, fox_w_out=m_fox_w_out, sb_w_in=m_sb_w_in, sb_w_out=m_sb_w_out, cv_w_in=m_cv_w_in, cv_b_in=m_cv_b_in, cv_dw=m_cv_dw, cv_dw_b=m_cv_dw_b, cv_ln_g=m_cv_ln_g, cv_ln_b=m_cv_ln_b, cv_w_out=m_cv_w_out, cv_b_out=m_cv_b_out)
    mom2 = dict(mod_w=v_mod_w, mod_b=v_mod_b, ln1_g=v_ln1_g, ln1_b=v_ln1_b, ln2_g=v_ln2_g, ln2_b=v_ln2_b, ffn_w_in=v_ffn_w_in, ffn_w_out=v_ffn_w_out, gm_w_in=v_gm_w_in, gm_b_in=v_gm_b_in, gm_ln_g=v_gm_ln_g, gm_ln_b=v_gm_ln_b, gm_w_s=v_gm_w_s, gm_b_s=v_gm_b_s, gm_w_out=v_gm_w_out, fox_w_in=v_fox_w_in, fox_b_f=v_fox_b_f, fox_w_out=v_fox_w_out, sb_w_in=v_sb_w_in, sb_w_out=v_sb_w_out, cv_w_in=v_cv_w_in, cv_b_in=v_cv_b_in, cv_dw=v_cv_dw, cv_dw_b=v_cv_dw_b, cv_ln_g=v_cv_ln_g, cv_ln_b=v_cv_ln_b, cv_w_out=v_cv_w_out, cv_b_out=v_cv_b_out)
    names = list(weights)

    nb, seq, d = x.shape
    t = nb * seq
    nl = mod_w.shape[0]
    alpha = (2.0 * nl) ** 0.25
    me = 4 * lax.axis_index("x") + 2 * lax.axis_index("y") + lax.axis_index("c")
    xs = x.reshape(t, d)
    tgt = loss_target.reshape(t, d)
    n_mod = mod_w.shape[2]
    n_ffn = ffn_w_in.shape[2]
    n_heads = d // HEAD_DIM
    npair = d // LANES

    cvp = d // N_DEV
    cv_small = jnp.concatenate([_pad_rows(cv_dw[0], CONV_HALO), cv_dw_b, cv_ln_g, cv_ln_b, cv_b_out,
                                cv_b_in.reshape(2, cvp), jnp.zeros((2, cvp), F32)], axis=0)
    cv_packed = cv_small.reshape(-1, d)
    first = _exchange_small(jnp.concatenate([_pad_rows(c, 8), _pad_rows(cv_packed, 8)], axis=0), "gather_c", False)
    c_all = first[:, :nb].reshape(N_DEV * nb, d)
    cv_all = first[:, 8:8 + cv_packed.shape[0]].reshape(N_DEV, cv_small.shape[0], cvp)
    mod_b_loc = lax.dynamic_slice_in_dim(mod_b, me * n_mod, n_mod, axis=1)[:, None, :]
    mod_loc = _mod_fwd(c_all, mod_w, mod_b_loc)
    mod_g = _exchange_small(mod_loc.reshape(nl * N_DEV * nb, n_mod), "gather_mod", False)
    mod_all = jnp.transpose(mod_g.reshape(N_DEV, nl, N_DEV * nb, n_mod), (1, 2, 0, 3)).reshape(nl, N_DEV * nb, N_DEV * n_mod)
    mod_me = lax.dynamic_slice_in_dim(mod_all, me * nb, nb, axis=1)
    mods = [[mod_me[l, :, k * d:(k + 1) * d][:, None, :] for k in range(6)] for l in range(nl)]

    assert nl == 4, "the exchange schedule below is written for the four-layer trunk"
    big = ["ffn_w_in", "ffn_w_out", "gm_w_in", "gm_w_out", "fox_w_in", "fox_w_out", "sb_w_in", "sb_w_out", "cv_w_in", "cv_w_out"]
    shard = {n: weights[n].astype(BF16) for n in big if not n.startswith("ffn")}
    for l in range(nl):
        shard["ffn_w_in", l] = ffn_w_in[l:l + 1].astype(BF16)
        shard["ffn_w_out", l] = ffn_w_out[l:l + 1].astype(BF16)
    now = ["gm_w_in", "gm_w_out"]
    with_gm_in = [("ffn_w_in", 0)]
    with_gm_gate = [("ffn_w_out", 0)]
    with_ffn_0 = ["fox_w_in", "fox_w_out"]
    later = [("ffn_w_in", 1), ("ffn_w_out", 1), "sb_w_in", "sb_w_out", ("ffn_w_in", 2), ("ffn_w_out", 2),
             "cv_w_in", "cv_w_out", ("ffn_w_in", 3), ("ffn_w_out", 3)]
    gathered = {}
    w_ffn_out_rows = lambda l: gathered["ffn_w_out", l].reshape(N_DEV // 2, n_ffn, d)
    sq = lambda n: gathered[n].reshape(d, d)
    cv_rows = jnp.transpose(cv_all, (1, 0, 2)).reshape(cv_small.shape[0], d)
    cv_dw_f, cv_dwb_f, cv_lng_f, cv_lnb_f, cv_bout_f = (cv_rows[:CONV_HALO], cv_rows[32:33], cv_rows[33:34], cv_rows[34:35], cv_rows[35:36])
    cv_bin_f = cv_all[:, 36:38, :].reshape(1, 2 * d)

    saved = []
    h, arrived = _modulate(xs, mods[0][1], mods[0][0], seq, beside=("gather", [shard[n] for n in now]))
    gathered.update(zip(now, arrived))
    xin = xs
    for l in range(nl):
        kind = l % 4
        sv = dict(x=xin, h=h)
        if kind == 0:
            pre, arrived = _proj_cols("gm_in", h, gathered["gm_w_in"], gm_w_in.shape[2], bias=gm_b_in,
                                      beside=("gather", [shard[n] for n in with_gm_in]))
            gathered.update(zip(with_gm_in, arrived))
            yv, arrived = _gm_spatial_fwd(pre, gm_ln_g, gm_ln_b, gm_w_s[0], jnp.transpose(gm_b_s[0]), seq,
                                          beside=("gather", [shard[n] for n in with_gm_gate]))
            gathered.update(zip(with_gm_gate, arrived))
            y = _mm_plain("gm_out", yv, sq("gm_w_out"), NN)
            sv.update(pre=pre, yv=yv)
        elif kind == 1:
            qkv = _qkv_proj("fox_qkv", h, fox_qkv_w)
            ft = _mm("fox_gate_proj", fox_f_wt, h, _sds((n_heads, t), F32), (t // min(TMM, t), 1),
                     pl.BlockSpec((n_heads, d), lambda i, r: (0, 0)), pl.BlockSpec((min(TMM, t), d), lambda i, r: (i, 0)),
                     pl.BlockSpec((n_heads, min(TMM, t)), lambda i, r: (0, i)), NT)
            b_f = jnp.transpose(fox_b_f)
            frow_p = _fox_gate_fwd(ft, b_f, seq).reshape(npair, 2, t)
            o, lse, *arrived = _fox_fwd(qkv, frow_p, nb, seq, gather=[shard[n] for n in later])
            gathered.update(zip(later, arrived))
            sb_qkv_w = _full_cols(gathered["sb_w_in"], sb_w_in.shape[2])
            y = _mm_plain("fox_out", o, sq("fox_w_out"), NN)
            sv.update(qkv=qkv, ft=ft, b_f=b_f, frow=frow_p, o=o, lse=lse)
        elif kind == 2:
            qkv = _qkv_proj("sb_qkv", h, sb_qkv_w)
            o, ltot = _sb_fwd(qkv, nb, seq)
            y = _mm_plain("sb_out", o, sq("sb_w_out"), NN)
            sv.update(qkv=qkv, o=o, ltot=ltot)
        else:
            pre = _proj_cols("cv_in", h, gathered["cv_w_in"], cv_w_in.shape[2], bias=cv_bin_f)
            y1, y2 = _cv_conv_fwd(pre, cv_dw_f, cv_dwb_f, cv_lng_f, cv_lnb_f, seq)
            y = _mm_plain("cv_out", y2, sq("cv_w_out"), NN, bias=cv_bout_f)
            sv.update(pre=pre, y1=y1, y2=y2)
        x1, h2 = _lnres_fwd(xin, y, mods[l][2], ln1_g[l:l + 1], ln1_b[l:l + 1], alpha, seq, nxt=(mods[l][4], mods[l][3]))
        if l == 0:
            (hg, hu, act), arrived = _ffn_in(h2, gathered["ffn_w_in", l], 0, beside=("gather", [shard[n] for n in with_ffn_0]))
            gathered.update(zip(with_ffn_0, arrived))
            fox_full = _full_cols(gathered["fox_w_in"], fox_w_in.shape[2])
            fox_qkv_w, fox_f_wt = fox_full[:, :3 * d], jnp.transpose(fox_full[:, 3 * d:])
        else:
            hg, hu, act = _ffn_in(h2, gathered["ffn_w_in", l], 0)
        y2f = _ffn_out(act, w_ffn_out_rows(l), 0)
        sv.update(y=y, x1=x1, h2=h2, hg=hg, hu=hu, act=act, y2f=y2f)
        if l + 1 < nl:
            xin, h = _lnres_fwd(x1, y2f, mods[l][5], ln2_g[l:l + 1], ln2_b[l:l + 1], alpha, seq, nxt=(mods[l + 1][1], mods[l + 1][0]))
        else:
            dx, sq_err = _lnres_fwd(x1, y2f, mods[l][5], ln2_g[l:l + 1], ln2_b[l:l + 1], alpha, seq, tgt=tgt)
        saved.append(sv)

    loss = lax.psum(0.5 * jnp.sum(sq_err) / d, ("x", "y", "c"))

    small = {}
    bigg = {}
    recv = {}
    dmod_parts = [dict() for _ in range(nl)]
    pending = None
    d_ln = dict(ln1_g=[None] * nl, ln1_b=[None] * nl, ln2_g=[None] * nl, ln2_b=[None] * nl)
    beside_sb = [("ffn_w_in", 3), ("ffn_w_out", 3), "cv_w_in", "cv_w_out", ("ffn_w_in", 2), ("ffn_w_out", 2)]
    beside_fox = ["sb_w_in", "sb_w_out", ("ffn_w_in", 1), ("ffn_w_out", 1)]
    beside_dact_0 = ["fox_w_in", "fox_w_out"]
    beside_dh_0 = [("ffn_w_in", 0)]
    beside_gm = [("ffn_w_out", 0), "gm_w_out"]
    beside_gm_in = ["gm_w_in"]
    for l in reversed(range(nl)):
        sv = saved[l]
        kind = l % 4
        if pending is None:
            dxr, dy2, dlg, dlb, _, dgate2 = _lnres_bwd(dx, sv["x1"], sv["y2f"], mods[l][5], ln2_g[l:l + 1], alpha, seq)
        else:
            dxr, dy2, dlg, dlb, _, dgate2, dsc1, dsh1 = _lnres_bwd(pending[1], sv["x1"], sv["y2f"], mods[l][5], ln2_g[l:l + 1], alpha, seq,
                                                                   through=(pending[0], pending[2], ln2_b[l:l + 1]))
            dmod_parts[l + 1].update(sc1=dsc1, sh1=dsh1)
        d_ln["ln2_g"][l], d_ln["ln2_b"][l] = dlg, dlb
        if l == 0:
            (dg_, du_), landed = _ffn_dact(dy2, w_ffn_out_rows(l), sv["hg"], sv["hu"], 0, beside=("scatter", [bigg[n] for n in beside_dact_0]))
            recv.update(zip(beside_dact_0, landed))
        else:
            dg_, du_ = _ffn_dact(dy2, w_ffn_out_rows(l), sv["hg"], sv["hu"], 0)
        dwi, dwo = _ffn_bwd_weights(sv["h2"], dy2, sv["act"], dg_, du_)
        bigg["ffn_w_in", l] = dwi[None]
        bigg["ffn_w_out", l] = dwo.reshape(1, N_DEV, n_ffn // 2, d)
        if l == 0:
            dh2, landed = _ffn_dh(dg_, du_, gathered["ffn_w_in", l], 0, beside=("scatter", [bigg[n] for n in beside_dh_0]))
            recv.update(zip(beside_dh_0, landed))
        else:
            dh2 = _ffn_dh(dg_, du_, gathered["ffn_w_in", l], 0)
        dxr, dy, dlg, dlb, dysum, dgate1, dsc2, dsh2 = _lnres_bwd(dxr, sv["x"], sv["y"], mods[l][2], ln1_g[l:l + 1], alpha, seq,
                                                                  through=(dh2, mods[l][4], ln1_b[l:l + 1]))
        d_ln["ln1_g"][l], d_ln["ln1_b"][l] = dlg, dlb
        hh = sv["h"]
        if kind == 0:
            dyv = _mm_plain("gm_out_bwd", dy, sq("gm_w_out"), NT)
            bigg["gm_w_out"] = _mm_plain("gm_out_dw", sv["yv"], dy, TN, GRAD_WIRE).reshape(1, N_DEV, d // N_DEV, d)
            (dpre, dws, dbst, dlng, dlnb, dbin), landed = _gm_spatial_bwd(sv["pre"], dyv, gm_ln_g, gm_ln_b, gm_w_s[0], jnp.transpose(gm_b_s[0]), seq,
                                                                          beside=("scatter", [bigg[n] for n in beside_gm]))
            recv.update(zip(beside_gm, landed))
            small.update(gm_w_s=dws[None], gm_b_s=jnp.transpose(dbst)[None], gm_ln_g=dlng, gm_ln_b=dlnb, gm_b_in=dbin)
            bigg["gm_w_in"] = _grad_cols("gm_in_dw", hh, dpre, gm_w_in.shape[2])
            dh, landed = _back_cols("gm_in_bwd", dpre, gathered["gm_w_in"], gm_w_in.shape[2], beside=("scatter", [bigg[n] for n in beside_gm_in]))
            recv.update(zip(beside_gm_in, landed))
        elif kind == 1:
            do = _mm_plain("fox_out_bwd", dy, sq("fox_w_out"), NT)
            bigg["fox_w_out"] = _mm_plain("fox_out_dw", sv["o"], dy, TN, GRAD_WIRE).reshape(1, N_DEV, d // N_DEV, d)
            dqkv, dfr, dfq, *landed = _fox_bwd(sv["qkv"], sv["frow"], sv["o"], do, sv["lse"], nb, seq,
                                               scatter=[bigg[n] for n in beside_fox])
            recv.update(zip(beside_fox, landed))
            dft, dbf = _fox_gate_bwd(sv["ft"], sv["b_f"], dfr.reshape(n_heads, t),
                                     jnp.transpose(dfq, (0, 2, 1)).reshape(n_heads, t), seq)
            small["fox_b_f"] = jnp.transpose(dbf)
            dw_qkv = _qkv_dw("fox_qkv_dw", hh, dqkv)
            tk = min(TMM, t)
            dw_ft = _mm("fox_gate_dw", dft, hh, _sds((n_heads, d), F32), (1, t // tk),
                        pl.BlockSpec((n_heads, tk), lambda j, r: (0, r)), pl.BlockSpec((tk, d), lambda j, r: (r, 0)),
                        pl.BlockSpec((n_heads, d), lambda j, r: (0, 0)), NN)
            bigg["fox_w_in"] = _col_blocks(jnp.concatenate([dw_qkv, jnp.transpose(dw_ft).astype(GRAD_WIRE)], axis=1), fox_w_in.shape[2])
            dh_a = _qkv_dh("fox_qkv_bwd", dqkv, fox_qkv_w)
            tm = min(TMM, t)
            dh = _mm("fox_gate_bwd_h", dft, fox_f_wt, _sds((t, d), F32), (t // tm, 1),
                     pl.BlockSpec((n_heads, tm), lambda i, r: (0, i)), pl.BlockSpec((n_heads, d), lambda i, r: (0, 0)),
                     pl.BlockSpec((tm, d), lambda i, r: (i, 0)), TN, (dh_a,), (pl.BlockSpec((tm, d), lambda i, r: (i, 0)),), _add)
        elif kind == 2:
            do = _mm_plain("sb_out_bwd", dy, sq("sb_w_out"), NT)
            bigg["sb_w_out"] = _mm_plain("sb_out_dw", sv["o"], dy, TN, GRAD_WIRE).reshape(1, N_DEV, d // N_DEV, d)
            dqkv, *landed = _sb_bwd(sv["qkv"], do, sv["ltot"], nb, seq, scatter=[bigg[n] for n in beside_sb])
            recv.update(zip(beside_sb, landed))
            bigg["sb_w_in"] = _col_blocks(_qkv_dw("sb_qkv_dw", hh, dqkv), sb_w_in.shape[2])
            dh = _qkv_dh("sb_qkv_bwd", dqkv, sb_qkv_w)
        else:
            dy2c = _mm_plain("cv_out_bwd", dy, sq("cv_w_out"), NT)
            bigg["cv_w_out"] = _mm_plain("cv_out_dw", sv["y2"], dy, TN, GRAD_WIRE).reshape(1, N_DEV, d // N_DEV, d)
            dy1, dlng, dlnb, ddwb = _cv_norm_bwd(sv["y1"], dy2c, cv_lng_f, cv_lnb_f)
            dpre, ddw, dbin = _cv_conv_bwd(sv["pre"], dy1, cv_dw_f, seq)
            small.update(cv_b_out=dysum, cv_ln_g=dlng, cv_ln_b=dlnb, cv_dw_b=ddwb, cv_dw=ddw[:CONV_WIDTH], cv_b_in=dbin)
            bigg["cv_w_in"] = _grad_cols("cv_in_dw", hh, dpre, cv_w_in.shape[2])
            dh = _back_cols("cv_in_bwd", dpre, gathered["cv_w_in"], cv_w_in.shape[2])
        pending = (dh, dxr, mods[l][1])
        dmod_parts[l].update(g1=dgate1, sh2=dsh2, sc2=dsc2, g2=dgate2)
    dx, dsc1, dsh1 = _modulate_bwd(pending[0], pending[1], saved[0]["x"], pending[2], seq)
    dmod_parts[0].update(sc1=dsc1, sh1=dsh1)
    dmods = [jnp.concatenate([p["sh1"], p["sc1"], p["g1"], p["sh2"], p["sc2"], p["g2"]], axis=2)[:, 0, :] for p in dmod_parts]
    grad_x = dx.reshape(nb, seq, d)
    for n in d_ln:
        small[n] = jnp.concatenate(d_ln[n], axis=0)

    rep = ["ln1_g", "ln1_b", "ln2_g", "ln2_b", "gm_b_in", "gm_ln_g", "gm_ln_b", "gm_w_s", "gm_b_s", "fox_b_f"]
    cvs = ["cv_b_in", "cv_dw", "cv_dw_b", "cv_ln_g", "cv_ln_b", "cv_b_out"]

    def rows_of(a):
        flat = a.reshape(-1)
        pad = (-flat.shape[0]) % d
        return jnp.pad(flat, (0, pad)).reshape(-1, d)

    pack_rows = [rows_of(small[n]) for n in rep + cvs]
    counts = [r.shape[0] for r in pack_rows]
    total = sum(counts)
    pack = _pad_rows(jnp.concatenate(pack_rows, axis=0), 48 * ((total + 47) // 48))
    dmod_pad = 8 * ((nl * nb + 7) // 8)
    dmod_rows = _pad_rows(jnp.stack(dmods).reshape(nl * nb, 6 * d), dmod_pad)
    both = jnp.concatenate([dmod_rows, pack.reshape(-1, 6 * d)], axis=0)
    gathered, summed = _exchange_small(both, "exchange_small", False, sum_from=dmod_pad)
    summed = summed.reshape(-1, d)

    dmod_g = gathered[:, :nl * nb]
    dmod_all = jnp.transpose(dmod_g.reshape(N_DEV, nl, nb, 6 * d), (1, 0, 2, 3)).reshape(nl, N_DEV * nb, 6 * d)
    dmod_loc = lax.dynamic_slice_in_dim(dmod_all, me * n_mod, n_mod, axis=2)
    g_mod_w, g_mod_b = _mod_bwd(c_all, dmod_loc, dmod_all)
    grads = dict(mod_w=g_mod_w, mod_b=g_mod_b[:, 0, :])
    offs = [sum(counts[:i]) for i in range(len(counts))]
    rep_rows = sum(counts[:len(rep)])
    for n, o_, cnt in zip(rep + cvs, offs, counts):
        full = summed[o_:o_ + cnt].reshape(-1)
        if n in rep:
            grads[n] = full[:weights[n].size].reshape(weights[n].shape)
        else:
            wshape = weights[n].shape
            cols = wshape[-1]
            full = full[:math.prod(wshape[:-1]) * cols * N_DEV].reshape(wshape[:-1] + (cols * N_DEV,))
            grads[n] = lax.dynamic_slice_in_dim(full, me * cols, cols, axis=full.ndim - 1)

    outs = {}

    def view2(a):
        return a.reshape(-1, a.shape[-1])

    for n in big:
        w2 = view2(weights[n])
        landed = [recv[n, l] for l in range(nl)] if n.startswith("ffn") else [recv[n]]
        parts = [r.reshape(N_DEV, w2.shape[0] // len(landed), w2.shape[1]) for r in landed]
        res = _adamw("adamw_" + n, w2, view2(mom1[n]), view2(mom2[n]), parts=parts)
        outs[n] = [r.reshape(weights[n].shape) for r in res]
    res = _adamw("adamw_mod_w", view2(mod_w), view2(m_mod_w), view2(v_mod_w), g=view2(grads["mod_w"]))
    outs["mod_w"] = [r.reshape(mod_w.shape) for r in res]
    rp = lambda src: _pad_rows(jnp.concatenate([rows_of(src[n]) for n in rep], axis=0), 8 * ((rep_rows + 7) // 8))
    res = _adamw("adamw_replicated", rp(weights), rp(mom1), rp(mom2), g=rp(grads))
    for n, o_, cnt in zip(rep, offs, counts):
        outs[n] = [r[o_:o_ + cnt].reshape(-1)[:weights[n].size].reshape(weights[n].shape) for r in res]
    cv_cols = weights["cv_b_out"].shape[-1]
    cp = lambda src: jnp.concatenate([src[n].reshape(-1, cv_cols) for n in cvs], axis=0)
    cv_cnt = [weights[n].size // cv_cols for n in cvs]
    cv_tot = sum(cv_cnt)
    cpp = lambda src: _pad_rows(cp(src), 8 * ((cv_tot + 7) // 8))
    res = _adamw("adamw_cv_small", cpp(weights), cpp(mom1), cpp(mom2), g=cpp(grads))
    o_ = 0
    for n, cnt in zip(cvs, cv_cnt):
        outs[n] = [r[o_:o_ + cnt].reshape(weights[n].shape) for r in res]
        o_ += cnt
    res = _adamw("adamw_mod_b", mod_b, m_mod_b, v_mod_b, g=grads["mod_b"])
    outs["mod_b"] = list(res)

    return (loss, grad_x, *[outs[n][0] for n in names], *[outs[n][1] for n in names],
            *[outs[n][2] for n in names], *[outs[n][3] for n in names])
```
